```python
import math
import jax, jax.numpy as jnp
from jax import lax
import numpy as np

D_MODEL = 1024
BATCH = 16
SEQ = 2048
DEPTH = 1

HEAD_DIM = 64
ATTN_WIDTH = D_MODEL // 2
N_ATTN_HEADS = ATTN_WIDTH // HEAD_DIM
DILATED_CONFIGS = ((128, 1), (512, 4), (2048, 16))
SGU_WIDTH = D_MODEL // 4
N_SGU_GROUPS = 4
SGU_GROUP = SGU_WIDTH // N_SGU_GROUPS
SGU_CHUNK = 128
MEM_WIDTH = D_MODEL // 4
N_MEM_HEADS = 4
MEM_HEAD_DIM = MEM_WIDTH // N_MEM_HEADS
N_MEM = 256
MIX_WIDTH = ATTN_WIDTH + SGU_WIDTH + MEM_WIDTH
IN_COLS = 4 * ATTN_WIDTH + 3 * SGU_WIDTH + 2 * MEM_WIDTH
EPS = 1e-6
NEG_INF = -1e30

kernel_name = "hybrid_dilated_sgu_memory_encoder"


def rmsnorm(x, g):
    xf = x.astype(jnp.float32)
    y = xf * lax.rsqrt(jnp.mean(xf * xf, axis=-1, keepdims=True) + EPS) * g.astype(jnp.float32)
    return y.astype(x.dtype)


def alibi_slopes(n):
    return jnp.power(2.0, -8.0 * (jnp.arange(n, dtype=jnp.float32) + 1.0) / n)


def dilated_window_attention(q, k, v, slopes, window, dilation):
    B, S, H, E = q.shape
    radius = window // (2 * dilation)
    blk = radius
    L = S // dilation
    N = B * dilation
    nb = -(-L // blk)
    Lp = nb * blk

    def to_sub(t):
        return t.reshape(B, L, dilation, H, E).transpose(0, 2, 1, 3, 4).reshape(N, L, H, E)

    qs = jnp.pad(to_sub(q), ((0, 0), (0, Lp - L), (0, 0), (0, 0))).reshape(N, nb, blk, H, E)

    def neighbour_blocks(t):
        tp = jnp.pad(to_sub(t), ((0, 0), (blk, Lp - L + blk), (0, 0), (0, 0))).reshape(N, nb + 2, blk, H, E)
        return jnp.concatenate([tp[:, :-2], tp[:, 1:-1], tp[:, 2:]], axis=2)

    kb = neighbour_blocks(k)
    vb = neighbour_blocks(v)
    q_idx = jnp.arange(Lp).reshape(nb, blk)
    k_idx = (jnp.arange(nb) * blk)[:, None] - blk + jnp.arange(3 * blk)[None, :]
    rel = jnp.abs(k_idx[:, None, :] - q_idx[:, :, None])
    valid = (rel <= radius) & (k_idx[:, None, :] >= 0) & (k_idx[:, None, :] < L)
    dist = (rel * dilation).astype(jnp.float32)

    s = jnp.einsum('nbqhe,nbkhe->nhbqk', qs.astype(jnp.float32), kb.astype(jnp.float32)) * (E ** -0.5)
    s = s - slopes[None, :, None, None, None] * dist[None, None]
    s = jnp.where(valid[None, None], s, NEG_INF)
    m = jnp.max(s, axis=-1, keepdims=True)
    p = jnp.exp(s - m)
    l = jnp.sum(p, axis=-1)
    o = jnp.einsum('nhbqk,nbkhe->nbqhe', p, vb.astype(jnp.float32))
    o = o / jnp.transpose(l, (0, 2, 3, 1))[..., None]
    lse = m[..., 0] + jnp.log(l)

    o = o.reshape(N, Lp, H, E)[:, :L].reshape(B, dilation, L, H, E).transpose(0, 2, 1, 3, 4).reshape(B, S, H, E)
    lse = lse.reshape(N, H, Lp)[..., :L].reshape(B, dilation, H, L).transpose(0, 3, 1, 2).reshape(B, S, H)
    return o, lse


def mixture_of_dilations(q, k, v):
    slopes = alibi_slopes(q.shape[2])
    outs, lses = [], []
    for window, dilation in DILATED_CONFIGS:
        o, lse = dilated_window_attention(q, k, v, slopes, window, dilation)
        outs.append(o)
        lses.append(lse)
    w = jax.nn.softmax(jnp.stack(lses, axis=0), axis=0)
    return jnp.sum(w[..., None] * jnp.stack(outs, axis=0), axis=0)


def chunked_spatial_gating(u, v, g_v, w_s, b_s):
    B, S, _ = v.shape
    n = S // SGU_CHUNK
    vn = rmsnorm(v, g_v).reshape(B, n, SGU_CHUNK, N_SGU_GROUPS, SGU_GROUP).astype(jnp.float32)
    mixed = jnp.einsum('gts,bnsgc->bntgc', w_s.astype(jnp.float32), vn)
    mixed = mixed + b_s.astype(jnp.float32).T[None, None, :, :, None]
    return u.astype(jnp.float32) * mixed.reshape(B, S, SGU_WIDTH)


def memory_cross_attention(q, mem, g_mem, w_mem_kv):
    B, M, _ = mem.shape
    kv = rmsnorm(mem, g_mem) @ w_mem_kv
    k, v = jnp.split(kv, 2, axis=-1)
    k = k.reshape(B, M, N_MEM_HEADS, MEM_HEAD_DIM).astype(jnp.float32)
    v = v.reshape(B, M, N_MEM_HEADS, MEM_HEAD_DIM).astype(jnp.float32)
    s = jnp.einsum('bshe,bmhe->bhsm', q.astype(jnp.float32), k) * (MEM_HEAD_DIM ** -0.5)
    p = jax.nn.softmax(s, axis=-1)
    return jnp.einsum('bhsm,bmhe->bshe', p, v)


def hybrid_layer(x, mem, g_norm, w_in, w_s, b_s, g_v, g_mem, w_mem_kv, w_out):
    B, S, _ = x.shape
    h = rmsnorm(x, g_norm)
    proj = h @ w_in
    offs = np.cumsum([ATTN_WIDTH] * 4 + [SGU_WIDTH] * 3 + [MEM_WIDTH])
    qa, ka, va, za, ub, vb, zb, qm, zm = jnp.split(proj, list(offs), axis=-1)

    shp = (B, S, N_ATTN_HEADS, HEAD_DIM)
    a = mixture_of_dilations(qa.reshape(shp), ka.reshape(shp), va.reshape(shp)).reshape(B, S, ATTN_WIDTH)
    sg = chunked_spatial_gating(jax.nn.gelu(ub), jax.nn.gelu(vb), g_v, w_s, b_s)
    mo = memory_cross_attention(qm.reshape(B, S, N_MEM_HEADS, MEM_HEAD_DIM), mem, g_mem, w_mem_kv)
    mo = mo.reshape(B, S, MEM_WIDTH)

    gated = jnp.concatenate([
        jax.nn.silu(za.astype(jnp.float32)) * a,
        jax.nn.silu(zb.astype(jnp.float32)) * sg,
        jax.nn.silu(zm.astype(jnp.float32)) * mo,
    ], axis=-1).astype(x.dtype)
    return x + gated @ w_out


def _fwd_setup_inputs(seed: int = 0) -> dict:
    key = jax.random.key(seed)
    ks = jax.random.split(key, 12)
    f32 = jnp.float32
    x = jax.random.normal(ks[0], (BATCH, SEQ, D_MODEL), f32)
    mem = jax.random.normal(ks[1], (BATCH, N_MEM, D_MODEL), f32)
    g_norm = 1.0 + 0.02 * jax.random.normal(ks[2], (DEPTH, D_MODEL), f32)
    w_in = jax.random.normal(ks[3], (DEPTH, D_MODEL, IN_COLS), f32) * D_MODEL ** -0.5
    w_sgu_spatial = jax.random.normal(ks[4], (DEPTH, N_SGU_GROUPS, SGU_CHUNK, SGU_CHUNK), f32) * (0.5 * SGU_CHUNK ** -0.5)
    b_sgu_spatial = 1.0 + 0.02 * jax.random.normal(ks[5], (DEPTH, N_SGU_GROUPS, SGU_CHUNK), f32)
    g_sgu_v = 1.0 + 0.02 * jax.random.normal(ks[6], (DEPTH, SGU_WIDTH), f32)
    g_mem = 1.0 + 0.02 * jax.random.normal(ks[7], (DEPTH, D_MODEL), f32)
    w_mem_kv = jax.random.normal(ks[8], (DEPTH, D_MODEL, 2 * MEM_WIDTH), f32) * D_MODEL ** -0.5
    w_out = jax.random.normal(ks[9], (DEPTH, MIX_WIDTH, D_MODEL), f32) * MIX_WIDTH ** -0.5
    g_final = 1.0 + 0.02 * jax.random.normal(ks[10], (D_MODEL,), f32)
    return {"x": x, "mem": mem, "g_norm": g_norm, "w_in": w_in,
            "w_sgu_spatial": w_sgu_spatial, "b_sgu_spatial": b_sgu_spatial, "g_sgu_v": g_sgu_v,
            "g_mem": g_mem, "w_mem_kv": w_mem_kv, "w_out": w_out, "g_final": g_final}


def _fwd_reference(x, mem, g_norm, w_in, w_sgu_spatial, b_sgu_spatial, g_sgu_v, g_mem, w_mem_kv, w_out, g_final):
    h = x
    for layer in range(DEPTH):
        h = hybrid_layer(h, mem, g_norm[layer], w_in[layer], w_sgu_spatial[layer], b_sgu_spatial[layer],
                         g_sgu_v[layer], g_mem[layer], w_mem_kv[layer], w_out[layer])
    return rmsnorm(h, g_final)


import jax as _jax
import jax.numpy as _jnp

TWIN_FORMAT = 'train_step'
FWD_PARAMS = ['x', 'mem', 'g_norm', 'w_in', 'w_sgu_spatial', 'b_sgu_spatial', 'g_sgu_v', 'g_mem', 'w_mem_kv', 'w_out', 'g_final']
TWIN_WEIGHTS = ['g_norm', 'w_in', 'w_sgu_spatial', 'b_sgu_spatial', 'g_sgu_v', 'g_mem', 'w_mem_kv', 'w_out', 'g_final']
TWIN_DIFF_INPUT = 'x'
TWIN_INPUTS = ['x', 'mem', 'g_norm', 'w_in', 'w_sgu_spatial', 'b_sgu_spatial', 'g_sgu_v', 'g_mem', 'w_mem_kv', 'w_out', 'g_final', 'loss_target', 'm_g_norm', 'm_w_in', 'm_w_sgu_spatial', 'm_b_sgu_spatial', 'm_g_sgu_v', 'm_g_mem', 'm_w_mem_kv', 'm_w_out', 'm_g_final', 'v_g_norm', 'v_w_in', 'v_w_sgu_spatial', 'v_b_sgu_spatial', 'v_g_sgu_v', 'v_g_mem', 'v_w_mem_kv', 'v_w_out', 'v_g_final']
TWIN_OUTPUTS = ['loss', 'grad_x', 'grad_g_norm', 'grad_w_in', 'grad_w_sgu_spatial', 'grad_b_sgu_spatial', 'grad_g_sgu_v', 'grad_g_mem', 'grad_w_mem_kv', 'grad_w_out', 'grad_g_final', 'delta_g_norm', 'delta_w_in', 'delta_w_sgu_spatial', 'delta_b_sgu_spatial', 'delta_g_sgu_v', 'delta_g_mem', 'delta_w_mem_kv', 'delta_w_out', 'delta_g_final', 'new_m_g_norm', 'new_m_w_in', 'new_m_w_sgu_spatial', 'new_m_b_sgu_spatial', 'new_m_g_sgu_v', 'new_m_g_mem', 'new_m_w_mem_kv', 'new_m_w_out', 'new_m_g_final', 'new_v_g_norm', 'new_v_w_in', 'new_v_w_sgu_spatial', 'new_v_b_sgu_spatial', 'new_v_g_sgu_v', 'new_v_g_mem', 'new_v_w_mem_kv', 'new_v_w_out', 'new_v_g_final']
TWIN_LEAF_KINDS = {'loss': 'loss', 'grad_x': 'grad_x', 'grad_g_norm': 'grad_w', 'grad_w_in': 'grad_w', 'grad_w_sgu_spatial': 'grad_w', 'grad_b_sgu_spatial': 'grad_w', 'grad_g_sgu_v': 'grad_w', 'grad_g_mem': 'grad_w', 'grad_w_mem_kv': 'grad_w', 'grad_w_out': 'grad_w', 'grad_g_final': 'grad_w', 'delta_g_norm': 'delta_w', 'delta_w_in': 'delta_w', 'delta_w_sgu_spatial': 'delta_w', 'delta_b_sgu_spatial': 'delta_w', 'delta_g_sgu_v': 'delta_w', 'delta_g_mem': 'delta_w', 'delta_w_mem_kv': 'delta_w', 'delta_w_out': 'delta_w', 'delta_g_final': 'delta_w', 'new_m_g_norm': 'new_m', 'new_m_w_in': 'new_m', 'new_m_w_sgu_spatial': 'new_m', 'new_m_b_sgu_spatial': 'new_m', 'new_m_g_sgu_v': 'new_m', 'new_m_g_mem': 'new_m', 'new_m_w_mem_kv': 'new_m', 'new_m_w_out': 'new_m', 'new_m_g_final': 'new_m', 'new_v_g_norm': 'new_v', 'new_v_w_in': 'new_v', 'new_v_w_sgu_spatial': 'new_v', 'new_v_b_sgu_spatial': 'new_v', 'new_v_g_sgu_v': 'new_v', 'new_v_g_mem': 'new_v', 'new_v_w_mem_kv': 'new_v', 'new_v_w_out': 'new_v', 'new_v_g_final': 'new_v'}


def _forward(args):
    return _fwd_reference(*[args[k] for k in FWD_PARAMS])


def _output_shape():
    out = _jax.eval_shape(lambda: _forward(_fwd_setup_inputs(0)))
    return out.shape, out.dtype

N_MICROBATCH = 1
ADAM_LR = 0.001
ADAM_B1 = 0.9
ADAM_B2 = 0.999
ADAM_EPS = 1e-08
ADAM_WD = 0.01
ADAM_STEP = 10
PER_EXAMPLE_BATCH_AXIS = {'x': 0, 'mem': 0, 'loss_target': 0}
SHARED_INPUTS = []
_WEIGHT_DTYPES = {'g_norm': _jnp.float32, 'w_in': _jnp.float32, 'w_sgu_spatial': _jnp.float32, 'b_sgu_spatial': _jnp.float32, 'g_sgu_v': _jnp.float32, 'g_mem': _jnp.float32, 'w_mem_kv': _jnp.float32, 'w_out': _jnp.float32, 'g_final': _jnp.float32}
MOMENT_SCALE = {'g_norm': 7.217426e-02, 'w_in': 3.871265e-02, 'w_sgu_spatial': 4.758003e-02, 'b_sgu_spatial': 4.809920e-02, 'g_sgu_v': 3.404053e-02, 'g_mem': 8.867302e-03, 'w_mem_kv': 1.088142e-02, 'w_out': 4.384962e-02, 'g_final': 3.197987e+01}


def _to_microbatches(a, axis):
    t = _jnp.moveaxis(a, axis, 0)
    t = t.reshape((N_MICROBATCH, t.shape[0] // N_MICROBATCH) + t.shape[1:])
    return _jnp.moveaxis(t, 1, axis + 1)


def setup_inputs(seed: int = 0) -> dict:
    inp = _fwd_setup_inputs(seed)
    key = _jax.random.fold_in(_jax.random.key(seed), 7919)
    shape, _ = _output_shape()
    out = dict(inp)
    out["loss_target"] = _jax.random.normal(_jax.random.fold_in(key, 0), shape, _jnp.float32)
    for i, name in enumerate(TWIN_WEIGHTS):
        w = inp[name].astype(_jnp.float32)
        if MOMENT_SCALE is None:
            s = _jnp.sqrt(_jnp.mean(_jnp.square(w)) + 1e-30)
        else:
            s = MOMENT_SCALE[name]
        km, kv = _jax.random.split(_jax.random.fold_in(key, i + 1))
        out[name] = w
        out["m_" + name] = s * _jax.random.normal(km, w.shape, _jnp.float32)
        out["v_" + name] = (s * s) * _jax.random.uniform(kv, w.shape, _jnp.float32, 0.5, 1.5)
    if N_MICROBATCH > 1:
        for name, axis in PER_EXAMPLE_BATCH_AXIS.items():
            out[name] = _to_microbatches(out[name], axis)
    return {'x': out['x'], 'mem': out['mem'], 'g_norm': out['g_norm'], 'w_in': out['w_in'], 'w_sgu_spatial': out['w_sgu_spatial'], 'b_sgu_spatial': out['b_sgu_spatial'], 'g_sgu_v': out['g_sgu_v'], 'g_mem': out['g_mem'], 'w_mem_kv': out['w_mem_kv'], 'w_out': out['w_out'], 'g_final': out['g_final'], 'loss_target': out['loss_target'], 'm_g_norm': out['m_g_norm'], 'm_w_in': out['m_w_in'], 'm_w_sgu_spatial': out['m_w_sgu_spatial'], 'm_b_sgu_spatial': out['m_b_sgu_spatial'], 'm_g_sgu_v': out['m_g_sgu_v'], 'm_g_mem': out['m_g_mem'], 'm_w_mem_kv': out['m_w_mem_kv'], 'm_w_out': out['m_w_out'], 'm_g_final': out['m_g_final'], 'v_g_norm': out['v_g_norm'], 'v_w_in': out['v_w_in'], 'v_w_sgu_spatial': out['v_w_sgu_spatial'], 'v_b_sgu_spatial': out['v_b_sgu_spatial'], 'v_g_sgu_v': out['v_g_sgu_v'], 'v_g_mem': out['v_g_mem'], 'v_w_mem_kv': out['v_w_mem_kv'], 'v_w_out': out['v_w_out'], 'v_g_final': out['v_g_final']}


def _loss(weights, diff, rest, loss_target):
    with _jax.named_scope("forward"):
        args = {**rest, TWIN_DIFF_INPUT: diff, **{k: w.astype(_WEIGHT_DTYPES[k]) for k, w in weights.items()}}
        y = _forward(args)
    with _jax.named_scope("loss_head"):
        err = _jnp.square(y.astype(_jnp.float32) - loss_target)
        return 0.5 * _jnp.sum(_jnp.mean(err, axis=-1)) if err.ndim else 0.5 * err


def _adamw(w, g, m, v):
    m = ADAM_B1 * m + (1.0 - ADAM_B1) * g
    v = ADAM_B2 * v + (1.0 - ADAM_B2) * _jnp.square(g)
    m_hat = m / (1.0 - ADAM_B1 ** ADAM_STEP)
    v_hat = v / (1.0 - ADAM_B2 ** ADAM_STEP)
    delta = -ADAM_LR * (m_hat / (_jnp.sqrt(v_hat) + ADAM_EPS) + ADAM_WD * w)
    return delta, m, v


def reference(x, mem, g_norm, w_in, w_sgu_spatial, b_sgu_spatial, g_sgu_v, g_mem, w_mem_kv, w_out, g_final, loss_target, m_g_norm, m_w_in, m_w_sgu_spatial, m_b_sgu_spatial, m_g_sgu_v, m_g_mem, m_w_mem_kv, m_w_out, m_g_final, v_g_norm, v_w_in, v_w_sgu_spatial, v_b_sgu_spatial, v_g_sgu_v, v_g_mem, v_w_mem_kv, v_w_out, v_g_final):
    given = dict(x=x, mem=mem, g_norm=g_norm, w_in=w_in, w_sgu_spatial=w_sgu_spatial, b_sgu_spatial=b_sgu_spatial, g_sgu_v=g_sgu_v, g_mem=g_mem, w_mem_kv=w_mem_kv, w_out=w_out, g_final=g_final, loss_target=loss_target, m_g_norm=m_g_norm, m_w_in=m_w_in, m_w_sgu_spatial=m_w_sgu_spatial, m_b_sgu_spatial=m_b_sgu_spatial, m_g_sgu_v=m_g_sgu_v, m_g_mem=m_g_mem, m_w_mem_kv=m_w_mem_kv, m_w_out=m_w_out, m_g_final=m_g_final, v_g_norm=v_g_norm, v_w_in=v_w_in, v_w_sgu_spatial=v_w_sgu_spatial, v_b_sgu_spatial=v_b_sgu_spatial, v_g_sgu_v=v_g_sgu_v, v_g_mem=v_g_mem, v_w_mem_kv=v_w_mem_kv, v_w_out=v_w_out, v_g_final=v_g_final)
    weights = {n: given[n] for n in TWIN_WEIGHTS}
    shared = {n: given[n] for n in SHARED_INPUTS}
    per_example = {n: given[n] for n in ['x', 'mem']}
    grad_fn = _jax.value_and_grad(_loss, argnums=(0, 1))

    def one_microbatch(ex, loss_target):
        ex = dict(ex)
        diff = ex.pop(TWIN_DIFF_INPUT)
        return grad_fn(weights, diff, {**shared, **ex}, loss_target)

    if N_MICROBATCH == 1:
        loss, (grad_w, grad_x) = one_microbatch(per_example, given["loss_target"])
    else:
        def body(carry, xs):
            loss_sum, grad_sum = carry
            l_k, (gw_k, gx_k) = one_microbatch(xs[0], xs[1])
            with _jax.named_scope("update"):
                return (loss_sum + l_k, _jax.tree.map(_jnp.add, grad_sum, gw_k)), gx_k

        init = (_jnp.zeros((), _jnp.float32), _jax.tree.map(_jnp.zeros_like, weights))
        (loss, grad_w), grad_x = _jax.lax.scan(body, init, (per_example, given["loss_target"]))
    with _jax.named_scope("update"):
        delta_w, new_m, new_v = {}, {}, {}
        for n in TWIN_WEIGHTS:
            delta_w[n], new_m[n], new_v[n] = _adamw(weights[n], grad_w[n], given["m_" + n], given["v_" + n])
    return (loss, grad_x, *[grad_w[n] for n in TWIN_WEIGHTS], *[delta_w[n] for n in TWIN_WEIGHTS],
            *[new_m[n] for n in TWIN_WEIGHTS], *[new_v[n] for n in TWIN_WEIGHTS])
```

```python
import functools

import jax
import jax.numpy as jnp
from jax import lax
from jax.experimental import pallas as pl
from jax.experimental.pallas import tpu as pltpu

F32 = jnp.float32
BF16 = jnp.bfloat16
SDS = jax.ShapeDtypeStruct
MESH = pl.DeviceIdType.MESH

N_DEV = 8
D_MODEL = 1024
SEQ = 2048
B_LOC = 2
T_LOC = B_LOC * SEQ
N_MEM = 256
HEAD = 64
ATTN_W = 512
SGU_W = 256
MEM_W = 256
IN_COLS = 3328
W_IN_SHARD = IN_COLS // N_DEV
ROW_SHARD = D_MODEL // N_DEV
CHUNK = 128
DILATIONS = ((1, 2048), (4, 512), (16, 128))
RADIUS = 64
EPS = 1e-6
NEG = -1e30
SCALE = HEAD ** -0.5
C_QA, C_KA, C_VA, C_ZA, C_UB, C_VB, C_ZB, C_QM, C_ZM = 0, 512, 1024, 1536, 2048, 2304, 2560, 2816, 3072
QKV_W = 1536
REST_W = IN_COLS - QKV_W

ADAM_LR, ADAM_B1, ADAM_B2, ADAM_EPS, ADAM_WD, ADAM_STEP = 0.001, 0.9, 0.999, 1e-08, 0.01, 10

V7X_VMEM_MIB = 64


def _params(vmem_mib, sem=None, **kw):
    assert vmem_mib < V7X_VMEM_MIB
    return pltpu.CompilerParams(vmem_limit_bytes=vmem_mib << 20, dimension_semantics=sem, **kw)


def _dot(a, b):
    return jnp.dot(a.astype(BF16), b.astype(BF16), preferred_element_type=F32)


def _dot_nt(a, b):
    return lax.dot_general(a.astype(BF16), b.astype(BF16), (((1,), (1,)), ((), ())), preferred_element_type=F32)


def _dot_tn(a, b):
    return lax.dot_general(a.astype(BF16), b.astype(BF16), (((0,), (0,)), ((), ())), preferred_element_type=F32)


def _rstd(v):
    return lax.rsqrt(jnp.mean(v * v, axis=-1, keepdims=True) + EPS)


def _rms_bwd(v, r, g, dy):
    gdy = g * dy
    return r * gdy - v * (r * r * r * jnp.mean(gdy * v, axis=-1, keepdims=True))


def _sigmoid(z):
    return 1.0 / (1.0 + jnp.exp(-z))


def _silu_and_grad(z):
    s = _sigmoid(z)
    return z * s, s * (1.0 + z * (1.0 - s))


_G_C = 0.7978845608028654
_G_K = 0.044715


def _gelu_and_grad(v):
    t = jnp.tanh(_G_C * (v + _G_K * (v * v * v)))
    cdf = 0.5 * (1.0 + t)
    return v * cdf, cdf + 0.5 * v * (1.0 - t * t) * (_G_C * (1.0 + 3.0 * _G_K * v * v))


def _left_lanes(rows):
    return lax.broadcasted_iota(jnp.int32, (rows, 128), 1) < HEAD


def _mesh_pos():
    return lax.axis_index("x"), lax.axis_index("y"), lax.axis_index("c")


def _peer(pos, k):
    x, y, c = pos
    return (1 - x if k & 4 else x, 1 - y if k & 2 else y, 1 - c if k & 1 else c)


def _flat(pos):
    return 4 * pos[0] + 2 * pos[1] + pos[2]


def _allgather_weights(w_in, w_kv, w_out):
    shards = (W_IN_SHARD, ROW_SHARD, ROW_SHARD)

    def body(win_ref, wkv_ref, wout_ref, wint_o, wkv_o, wout_o, tr, send_sems, recv_sems):
        x, y, c = _mesh_pos()
        me, sib = (x, y, c), (x, y, 1 - c)
        chips = [(1 - x, y), (x, 1 - y), (1 - x, 1 - y)]
        outs = (wint_o, wkv_o, wout_o)

        def rows(a, p):
            n = shards[a]
            return outs[a].at[pl.ds(pl.multiple_of(_flat(p) * n, 16), n), :]

        tr[...] = jnp.zeros_like(tr)
        tr[:, 0:W_IN_SHARD] = win_ref[...]
        rows(0, me)[...] = tr[...].T[0:W_IN_SHARD, :].astype(BF16)
        rows(1, me)[...] = wkv_ref[...].astype(BF16)
        rows(2, me)[...] = wout_ref[...].astype(BF16)

        def copy(a, k, block, to):
            return pltpu.make_async_remote_copy(
                src_ref=rows(a, block), dst_ref=rows(a, block),
                send_sem=send_sems.at[7 * a + k], recv_sem=recv_sems.at[7 * a + k],
                device_id=to, device_id_type=MESH)

        first = []
        for a in range(3):
            first.append(copy(a, 0, me, sib))
            first += [copy(a, 1 + j, me, (*chip, c)) for j, chip in enumerate(chips)]
        for cp in first:
            cp.start()
        passed = []
        for a in range(3):
            for j, chip in enumerate(chips):
                copy(a, 1 + j, (*chip, c), me).wait_recv()
                fwd = copy(a, 4 + j, (*chip, c), sib)
                fwd.start()
                passed.append(fwd)
        for a in range(3):
            copy(a, 0, sib, me).wait_recv()
            for j, chip in enumerate(chips):
                copy(a, 4 + j, (*chip, 1 - c), me).wait_recv()
        for cp in first + passed:
            cp.wait_send()

    vmem = pl.BlockSpec(memory_space=pltpu.VMEM)
    return pl.pallas_call(
        body, name="allgather_weights",
        out_shape=(SDS((IN_COLS, D_MODEL), BF16), SDS((D_MODEL, 2 * MEM_W), BF16), SDS((D_MODEL, D_MODEL), BF16)),
        in_specs=[vmem, vmem, vmem], out_specs=(vmem, vmem, vmem),
        scratch_shapes=[pltpu.VMEM((D_MODEL, 512), F32), pltpu.SemaphoreType.DMA((21,)), pltpu.SemaphoreType.DMA((21,))],
        compiler_params=_params(40),
    )(w_in, w_kv, w_out)


def _proj_fwd(x2, g_norm, wint):
    tm = 256

    def body(x_ref, g_ref, w_ref, o_ref):
        xv = x_ref[...]
        h = xv * _rstd(xv) * g_ref[...]
        o_ref[...] = _dot_nt(h, w_ref[...])

    return pl.pallas_call(
        body, name="proj_fwd", grid=(T_LOC // tm,),
        in_specs=[pl.BlockSpec((tm, D_MODEL), lambda i: (i, 0)), pl.BlockSpec((1, D_MODEL), lambda i: (0, 0)),
                  pl.BlockSpec((IN_COLS, D_MODEL), lambda i: (0, 0))],
        out_specs=pl.BlockSpec((tm, IN_COLS), lambda i: (i, 0)),
        out_shape=SDS((T_LOC, IN_COLS), F32),
        compiler_params=_params(48, ("arbitrary",)),
    )(x2, g_norm, wint)


def _memkv_fwd(mem2, g_mem, wkv):
    def body(m_ref, g_ref, w_ref, o_ref):
        mv = m_ref[...]
        o_ref[...] = _dot(mv * _rstd(mv) * g_ref[...], w_ref[...])

    return pl.pallas_call(
        body, name="memkv_fwd", out_shape=SDS((B_LOC * N_MEM, 2 * MEM_W), F32), compiler_params=_params(32),
    )(mem2, g_mem, wkv)


def _attn_geometry(cfg, it):
    d, length = DILATIONS[cfg]
    nblk = length // CHUNK
    if nblk == 1:
        rho, i = it, 0
    else:
        rho, i = it // nblk, it % nblk
    nk = min(length, 2 * CHUNK)
    ks = jnp.clip(i * CHUNK - RADIUS, 0, length - nk) if length > nk else 0
    return rho + d * (i * CHUNK), rho + d * ks, i * CHUNK - ks, nk


def _band(off, nk):
    r = lax.broadcasted_iota(jnp.int32, (CHUNK, nk), 0)
    c = lax.broadcasted_iota(jnp.int32, (CHUNK, nk), 1)
    rel = jnp.abs(r - c + off)
    return rel.astype(F32), rel <= RADIUS


def _rows(start, n, d):
    return pl.ds(start, n) if d == 1 else pl.ds(start, n, stride=d)


def _attn_fwd(proj, slopes):
    def body(sl_ref, q_ref, k_ref, v_ref, a_ref, lse_ref, *scr):
        o_c, m_c, l_c = scr[0:3], scr[3:6], scr[6:9]
        left = _left_lanes(CHUNK)
        sl = (sl_ref[0, 0:1, 0:1], sl_ref[0, 0:1, HEAD:HEAD + 1])
        for cfg, (d, _) in enumerate(DILATIONS):
            def block(it, carry, cfg=cfg, d=d):
                q0, k0, off, nk = _attn_geometry(cfg, it)
                qb = q_ref[_rows(q0, CHUNK, d), :]
                kw = k_ref[_rows(k0, nk, d), :].astype(BF16)
                vw = v_ref[_rows(k0, nk, d), :].astype(BF16)
                dist, valid = _band(off, nk)
                res = []
                for h in range(2):
                    qh = jnp.where(left if h == 0 else ~left, qb, 0.0)
                    s = _dot_nt(qh, kw) * SCALE - (sl[h] * float(d)) * dist
                    s = jnp.where(valid, s, NEG)
                    m = jnp.max(s, axis=-1, keepdims=True)
                    p = jnp.exp(s - m)
                    res.append((_dot(p, vw), m, jnp.sum(p, axis=-1, keepdims=True)))
                rq = _rows(q0, CHUNK, d)
                o_c[cfg][rq, :] = jnp.where(left, res[0][0], res[1][0])
                m_c[cfg][rq, :] = jnp.where(left, res[0][1], res[1][1])
                l_c[cfg][rq, :] = jnp.where(left, res[0][2], res[1][2])
                return carry
            lax.fori_loop(0, 16, block, 0)

        def merge(j, carry):
            rows = pl.ds(pl.multiple_of(j * 256, 256), 256)
            ms = [m_c[i][rows, :] for i in range(3)]
            top = jnp.maximum(jnp.maximum(ms[0], ms[1]), ms[2])
            ws = [jnp.exp(m - top) for m in ms]
            den = l_c[0][rows, :] * ws[0] + l_c[1][rows, :] * ws[1] + l_c[2][rows, :] * ws[2]
            num = o_c[0][rows, :] * ws[0] + o_c[1][rows, :] * ws[1] + o_c[2][rows, :] * ws[2]
            a_ref[rows, :] = num / den
            lse_ref[rows, :] = top + jnp.log(den)
            return carry
        lax.fori_loop(0, SEQ // 256, merge, 0)

    blk = lambda col0: pl.BlockSpec((SEQ, 128), lambda b, hp: (b, col0 // 128 + hp))
    out = pl.BlockSpec((SEQ, 128), lambda b, hp: (b, hp))
    return pl.pallas_call(
        body, name="attn_fwd", grid=(B_LOC, 4),
        in_specs=[pl.BlockSpec((1, 8, 128), lambda b, hp: (hp, 0, 0)), blk(C_QA), blk(C_KA), blk(C_VA)],
        out_specs=(out, out),
        out_shape=(SDS((T_LOC, ATTN_W), F32), SDS((T_LOC, ATTN_W), F32)),
        scratch_shapes=[pltpu.VMEM((SEQ, 128), F32)] * 9,
        compiler_params=_params(40, ("arbitrary", "arbitrary")),
    )(slopes, proj, proj, proj)


def _sgu_mix(vn, ws_ref, dst_ref, tm):
    left = _left_lanes(CHUNK)
    for ch in range(tm // CHUNK):
        for pr in range(2):
            vp = vn[ch * CHUNK:(ch + 1) * CHUNK, pr * 128:(pr + 1) * 128]
            dst_ref[ch * CHUNK:(ch + 1) * CHUNK, pr * 128:(pr + 1) * 128] = jnp.where(
                left, _dot(ws_ref[2 * pr], vp), _dot(ws_ref[2 * pr + 1], vp))


def _mem_attn_head(qp, kp, h, left):
    qh = jnp.where(left if h == 0 else ~left, qp, 0.0)
    s = _dot_nt(qh, kp) * SCALE
    e = jnp.exp(s - jnp.max(s, axis=-1, keepdims=True))
    return e * (1.0 / jnp.sum(e, axis=-1, keepdims=True)), qh


def _branch_blocks(tm):
    col = lambda w, c0: pl.BlockSpec((tm, w), lambda i: (i, c0 // w))
    return [col(512, C_ZA), col(256, C_UB), col(256, C_VB), col(256, C_ZB), col(256, C_QM), col(256, C_ZM)]


def _branch_fwd(proj, a, kv, w_s, b_exp, g_v):
    tm = 256
    per_ex = SEQ // tm

    def body(za_ref, ub_ref, vb_ref, zb_ref, qm_ref, zm_ref, a_ref, kv_ref, ws_ref, be_ref, gv_ref, o_ref, mix):
        left = _left_lanes(tm)
        o_ref[:, 0:ATTN_W] = (_silu_and_grad(za_ref[...])[0] * a_ref[...]).astype(BF16)
        gu = _gelu_and_grad(ub_ref[...])[0]
        gv = _gelu_and_grad(vb_ref[...])[0]
        vn = gv * _rstd(gv) * gv_ref[...]
        _sgu_mix(vn.astype(BF16), ws_ref, mix, tm)
        sg = gu * (mix[...] + be_ref[...])
        o_ref[:, ATTN_W:ATTN_W + SGU_W] = (_silu_and_grad(zb_ref[...])[0] * sg).astype(BF16)
        szm = _silu_and_grad(zm_ref[...])[0]
        for hp in range(2):
            cols = slice(hp * 128, (hp + 1) * 128)
            qp, kp, vp = qm_ref[:, cols], kv_ref[:, cols], kv_ref[:, MEM_W + hp * 128:MEM_W + (hp + 1) * 128]
            o = [_dot(_mem_attn_head(qp, kp, h, left)[0], vp) for h in range(2)]
            c0 = ATTN_W + SGU_W + hp * 128
            o_ref[:, c0:c0 + 128] = (szm[:, cols] * jnp.where(left, o[0], o[1])).astype(BF16)

    full = lambda shape: pl.BlockSpec(shape, lambda i: (0,) * len(shape))
    return pl.pallas_call(
        body, name="branch_fwd", grid=(T_LOC // tm,),
        in_specs=_branch_blocks(tm) + [
            pl.BlockSpec((tm, ATTN_W), lambda i: (i, 0)), pl.BlockSpec((N_MEM, 2 * MEM_W), lambda i: (i // per_ex, 0)),
            full((4, CHUNK, CHUNK)), full((tm, SGU_W)), full((1, SGU_W))],
        out_specs=pl.BlockSpec((tm, D_MODEL), lambda i: (i, 0)),
        out_shape=SDS((T_LOC, D_MODEL), BF16),
        scratch_shapes=[pltpu.VMEM((tm, SGU_W), F32)],
        compiler_params=_params(40, ("arbitrary",)),
    )(proj, proj, proj, proj, proj, proj, a, kv, w_s, b_exp, g_v)


def _outproj_loss(gated, wout, x2, tgt2, g_final):
    tm = 512

    def body(g_ref, w_ref, x_ref, t_ref, gf_ref, dh2_ref, loss_ref, dgf_ref):
        @pl.when(pl.program_id(0) == 0)
        def _():
            loss_ref[...] = jnp.zeros_like(loss_ref)
            dgf_ref[...] = jnp.zeros_like(dgf_ref)
        h2 = x_ref[...] + _dot(g_ref[...], w_ref[...])
        r = _rstd(h2)
        gf = gf_ref[...]
        err = h2 * r * gf - t_ref[...]
        loss_ref[...] += 0.5 * jnp.sum(jnp.mean(err * err, axis=-1, keepdims=True))
        dy = err * (1.0 / D_MODEL)
        dh2_ref[...] = _rms_bwd(h2, r, gf, dy)
        dgf_ref[...] += jnp.sum(dy * (h2 * r), axis=0, keepdims=True)

    row = pl.BlockSpec((tm, D_MODEL), lambda i: (i, 0))
    vec = pl.BlockSpec((1, D_MODEL), lambda i: (0, 0))
    return pl.pallas_call(
        body, name="outproj_loss", grid=(T_LOC // tm,),
        in_specs=[row, pl.BlockSpec((D_MODEL, D_MODEL), lambda i: (0, 0)), row, row, vec],
        out_specs=(row, pl.BlockSpec((8, 128), lambda i: (0, 0)), vec),
        out_shape=(SDS((T_LOC, D_MODEL), F32), SDS((8, 128), F32), SDS((1, D_MODEL), F32)),
        compiler_params=_params(40, ("arbitrary",)),
    )(gated, wout, x2, tgt2, g_final)


def _branch_bwd(dh2, wout, gated, proj, a, kv, w_s, b_exp, g_v):
    tm = 256
    per_ex = SEQ // tm

    def body(dh2_ref, w_ref, g_ref, za_ref, ub_ref, vb_ref, zb_ref, qm_ref, zm_ref, a_ref, kv_ref, ws_ref,
             be_ref, gv_ref, da_ref, dr_ref, dkv_ref, dwo_ref, dws_ref, db_ref, dgv_ref, mix, dvn, dmsum):
        i = pl.program_id(0)
        left = _left_lanes(tm)
        leftc = _left_lanes(CHUNK)

        @pl.when(i == 0)
        def _():
            dwo_ref[...] = jnp.zeros_like(dwo_ref)
            dws_ref[...] = jnp.zeros_like(dws_ref)
            dgv_ref[...] = jnp.zeros_like(dgv_ref)
            dmsum[...] = jnp.zeros_like(dmsum)

        @pl.when(i % per_ex == 0)
        def _():
            dkv_ref[...] = jnp.zeros_like(dkv_ref)

        dh2 = dh2_ref[...].astype(BF16)
        dwo_ref[...] += _dot_tn(g_ref[...], dh2)
        dg = _dot_nt(dh2, w_ref[...])

        sa, dsa = _silu_and_grad(za_ref[...])
        dga = dg[:, 0:ATTN_W]
        da_ref[...] = dga * sa
        dr_ref[:, 0:512] = (dga * a_ref[...] * dsa).astype(BF16)

        ub, vb = ub_ref[...], vb_ref[...]
        gu, dgu = _gelu_and_grad(ub)
        gv, dgv = _gelu_and_grad(vb)
        rv = _rstd(gv)
        gain = gv_ref[...]
        vn = (gv * rv * gain).astype(BF16)
        _sgu_mix(vn, ws_ref, mix, tm)
        mixed = mix[...] + be_ref[...]
        sb, dsb = _silu_and_grad(zb_ref[...])
        dgb = dg[:, ATTN_W:ATTN_W + SGU_W]
        dsg = dgb * sb
        dr_ref[:, 512:768] = (dsg * mixed * dgu).astype(BF16)
        dr_ref[:, 1024:1280] = (dgb * (gu * mixed) * dsb).astype(BF16)
        dmix = dsg * gu
        for ch in range(tm // CHUNK):
            rows = slice(ch * CHUNK, (ch + 1) * CHUNK)
            dmsum[...] += dmix[rows, :]
            for pr in range(2):
                cols = slice(pr * 128, (pr + 1) * 128)
                dmp, vp = dmix[rows, cols], vn[rows, cols]
                for h in range(2):
                    g = 2 * pr + h
                    dws_ref[g] += _dot_nt(jnp.where(leftc if h == 0 else ~leftc, dmp, 0.0), vp)
                dvn[rows, cols] = jnp.where(leftc, _dot_tn(ws_ref[2 * pr], dmp), _dot_tn(ws_ref[2 * pr + 1], dmp))
        dvn_v = dvn[...]
        dgv_ref[...] += jnp.sum(dvn_v * (gv * rv), axis=0, keepdims=True)
        dr_ref[:, 768:1024] = (_rms_bwd(gv, rv, gain, dvn_v) * dgv).astype(BF16)

        szm, dszm = _silu_and_grad(zm_ref[...])
        dgm = dg[:, ATTN_W + SGU_W:D_MODEL]
        dmo = dgm * szm
        for hp in range(2):
            cols = slice(hp * 128, (hp + 1) * 128)
            vcols = slice(MEM_W + hp * 128, MEM_W + (hp + 1) * 128)
            qp, kp, vp, dmop = qm_ref[:, cols], kv_ref[:, cols], kv_ref[:, vcols], dmo[:, cols]
            o, dq = [], []
            dk = jnp.zeros((N_MEM, 128), F32)
            dv = jnp.zeros((N_MEM, 128), F32)
            for h in range(2):
                p, qh = _mem_attn_head(qp, kp, h, left)
                dmoh = jnp.where(left if h == 0 else ~left, dmop, 0.0)
                o.append(_dot(p, vp))
                dp = _dot_nt(dmoh, vp)
                ds = p * (dp - jnp.sum(p * dp, axis=-1, keepdims=True)) * SCALE
                dq.append(_dot(ds, kp))
                dk += _dot_tn(ds, qh)
                dv += _dot_tn(p, dmoh)
            dr_ref[:, 1280 + hp * 128:1280 + (hp + 1) * 128] = jnp.where(left, dq[0], dq[1]).astype(BF16)
            dr_ref[:, 1536 + hp * 128:1536 + (hp + 1) * 128] = (
                dgm[:, cols] * jnp.where(left, o[0], o[1]) * dszm[:, cols]).astype(BF16)
            dkv_ref[:, cols] += dk
            dkv_ref[:, vcols] += dv

        @pl.when(i == pl.num_programs(0) - 1)
        def _():
            tot = dmsum[...]
            hi = tot.astype(BF16)
            lo = (tot - hi.astype(F32)).astype(BF16)
            grp = (lax.broadcasted_iota(jnp.int32, (SGU_W, 128), 0) // HEAD
                   == lax.broadcasted_iota(jnp.int32, (SGU_W, 128), 1)).astype(BF16)
            db_ref[...] = (_dot(hi, grp) + _dot(lo, grp)).T[0:4, :]

    full = lambda shape: pl.BlockSpec(shape, lambda i: (0,) * len(shape))
    row = lambda w: pl.BlockSpec((tm, w), lambda i: (i, 0))
    return pl.pallas_call(
        body, name="branch_bwd", grid=(T_LOC // tm,),
        in_specs=[row(D_MODEL), full((D_MODEL, D_MODEL)), row(D_MODEL)] + _branch_blocks(tm) + [
            row(ATTN_W), pl.BlockSpec((N_MEM, 2 * MEM_W), lambda i: (i // per_ex, 0)),
            full((4, CHUNK, CHUNK)), full((tm, SGU_W)), full((1, SGU_W))],
        out_specs=(row(ATTN_W), row(REST_W), pl.BlockSpec((N_MEM, 2 * MEM_W), lambda i: (i // per_ex, 0)),
                   full((D_MODEL, D_MODEL)), full((4, CHUNK, CHUNK)), full((4, CHUNK)), full((1, SGU_W))),
        out_shape=(SDS((T_LOC, ATTN_W), F32), SDS((T_LOC, REST_W), BF16), SDS((B_LOC * N_MEM, 2 * MEM_W), F32),
                   SDS((D_MODEL, D_MODEL), F32), SDS((4, CHUNK, CHUNK), F32), SDS((4, CHUNK), F32), SDS((1, SGU_W), F32)),
        scratch_shapes=[pltpu.VMEM((tm, SGU_W), F32), pltpu.VMEM((tm, SGU_W), F32), pltpu.VMEM((CHUNK, SGU_W), F32)],
        compiler_params=_params(56, ("arbitrary",)),
    )(dh2, wout, gated, proj, proj, proj, proj, proj, proj, a, kv, w_s, b_exp, g_v)


def _attn_bwd(proj, slopes, da, a, lse):
    def body(sl_ref, q_ref, k_ref, v_ref, da_ref, a_ref, lse_ref, dq_ref, dk_ref, dv_ref, dq_s, dk_s, dv_s, delta_s):
        left = _left_lanes(CHUNK)
        sl = (sl_ref[0, 0:1, 0:1], sl_ref[0, 0:1, HEAD:HEAD + 1])

        def prep(j, carry):
            rows = pl.ds(pl.multiple_of(j * 256, 256), 256)
            l256 = _left_lanes(256)
            prod = da_ref[rows, :] * a_ref[rows, :]
            d0 = jnp.sum(jnp.where(l256, prod, 0.0), axis=-1, keepdims=True)
            d1 = jnp.sum(jnp.where(l256, 0.0, prod), axis=-1, keepdims=True)
            delta_s[rows, :] = jnp.where(l256, d0, d1)
            zero = jnp.zeros((256, 128), F32)
            dq_s[rows, :] = zero
            dk_s[rows, :] = zero
            dv_s[rows, :] = zero
            return carry
        lax.fori_loop(0, SEQ // 256, prep, 0)

        for cfg, (d, _) in enumerate(DILATIONS):
            def block(it, carry, cfg=cfg, d=d):
                q0, k0, off, nk = _attn_geometry(cfg, it)
                rq, rk = _rows(q0, CHUNK, d), _rows(k0, nk, d)
                qb, dab = q_ref[rq, :], da_ref[rq, :]
                lse_b, delta_b = lse_ref[rq, :], delta_s[rq, :]
                kw = k_ref[rk, :].astype(BF16)
                vw = v_ref[rk, :].astype(BF16)
                dist, valid = _band(off, nk)
                dq = []
                dk = jnp.zeros((nk, 128), F32)
                dv = jnp.zeros((nk, 128), F32)
                for h in range(2):
                    mask = left if h == 0 else ~left
                    c0 = 0 if h == 0 else HEAD
                    qh = jnp.where(mask, qb, 0.0).astype(BF16)
                    dah = jnp.where(mask, dab, 0.0).astype(BF16)
                    s = _dot_nt(qh, kw) * SCALE - (sl[h] * float(d)) * dist
                    p = jnp.exp(jnp.where(valid, s, NEG) - lse_b[:, c0:c0 + 1])
                    dp = _dot_nt(dah, vw)
                    ds = (p * (dp - delta_b[:, c0:c0 + 1]) * SCALE).astype(BF16)
                    dq.append(_dot(ds, kw))
                    dk += _dot_tn(ds, qh)
                    dv += _dot_tn(p, dah)
                dq_s[rq, :] += jnp.where(left, dq[0], dq[1])
                dk_s[rk, :] += dk
                dv_s[rk, :] += dv
                return carry
            lax.fori_loop(0, 16, block, 0)

        def flush(j, carry):
            rows = pl.ds(pl.multiple_of(j * 256, 256), 256)
            dq_ref[rows, :] = dq_s[rows, :].astype(BF16)
            dk_ref[rows, :] = dk_s[rows, :].astype(BF16)
            dv_ref[rows, :] = dv_s[rows, :].astype(BF16)
            return carry
        lax.fori_loop(0, SEQ // 256, flush, 0)

    blk = lambda col0: pl.BlockSpec((SEQ, 128), lambda b, hp: (b, col0 // 128 + hp))
    own = pl.BlockSpec((SEQ, 128), lambda b, hp: (b, hp))
    return pl.pallas_call(
        body, name="attn_bwd", grid=(B_LOC, 4),
        in_specs=[pl.BlockSpec((1, 8, 128), lambda b, hp: (hp, 0, 0)), blk(C_QA), blk(C_KA), blk(C_VA), own, own, own],
        out_specs=(own, own, own),
        out_shape=(SDS((T_LOC, ATTN_W), BF16),) * 3,
        scratch_shapes=[pltpu.VMEM((SEQ, 128), F32)] * 4,
        compiler_params=_params(40, ("arbitrary", "arbitrary")),
    )(slopes, proj, proj, proj, da, a, lse)


def _dproj_specs(tm):
    third = pl.BlockSpec((tm, ATTN_W), lambda i: (i, 0))
    return [third, third, third, pl.BlockSpec((tm, REST_W), lambda i: (i, 0))]


def _dx(dq, dk, dv, dr, wint, x2, dh2, g_norm):
    tm = 256

    def body(dq_ref, dk_ref, dv_ref, dr_ref, w_ref, x_ref, dh2_ref, g_ref, gx_ref, dgn_ref):
        @pl.when(pl.program_id(0) == 0)
        def _():
            dgn_ref[...] = jnp.zeros_like(dgn_ref)
        dh = (_dot(dq_ref[...], w_ref[C_QA:C_KA, :]) + _dot(dk_ref[...], w_ref[C_KA:C_VA, :])
              + _dot(dv_ref[...], w_ref[C_VA:C_ZA, :]) + _dot(dr_ref[...], w_ref[C_ZA:IN_COLS, :]))
        xv = x_ref[...]
        r = _rstd(xv)
        gx_ref[...] = dh2_ref[...] + _rms_bwd(xv, r, g_ref[...], dh)
        dgn_ref[...] += jnp.sum(dh * (xv * r), axis=0, keepdims=True)

    row = pl.BlockSpec((tm, D_MODEL), lambda i: (i, 0))
    vec = pl.BlockSpec((1, D_MODEL), lambda i: (0, 0))
    return pl.pallas_call(
        body, name="dx", grid=(T_LOC // tm,),
        in_specs=_dproj_specs(tm) + [pl.BlockSpec((IN_COLS, D_MODEL), lambda i: (0, 0)), row, row, vec],
        out_specs=(row, vec),
        out_shape=(SDS((T_LOC, D_MODEL), F32), SDS((1, D_MODEL), F32)),
        compiler_params=_params(48, ("arbitrary",)),
    )(dq, dk, dv, dr, wint, x2, dh2, g_norm)


def _dwin(dq, dk, dv, dr, x2, g_norm):
    tm = 512

    def body(dq_ref, dk_ref, dv_ref, dr_ref, x_ref, g_ref, o_ref):
        @pl.when(pl.program_id(0) == 0)
        def _():
            o_ref[...] = jnp.zeros_like(o_ref)
        xv = x_ref[...]
        h = (xv * _rstd(xv) * g_ref[...]).astype(BF16)
        o_ref[C_QA:C_KA, :] += _dot_tn(dq_ref[...], h)
        o_ref[C_KA:C_VA, :] += _dot_tn(dk_ref[...], h)
        o_ref[C_VA:C_ZA, :] += _dot_tn(dv_ref[...], h)
        o_ref[C_ZA:IN_COLS, :] += _dot_tn(dr_ref[...], h)

    return pl.pallas_call(
        body, name="dwin", grid=(T_LOC // tm,),
        in_specs=_dproj_specs(tm) + [pl.BlockSpec((tm, D_MODEL), lambda i: (i, 0)), pl.BlockSpec((1, D_MODEL), lambda i: (0, 0))],
        out_specs=pl.BlockSpec((IN_COLS, D_MODEL), lambda i: (0, 0)),
        out_shape=SDS((IN_COLS, D_MODEL), F32),
        compiler_params=_params(56, ("arbitrary",)),
    )(dq, dk, dv, dr, x2, g_norm)


def _memkv_bwd(dkv, mem2, g_mem, wkv):
    def body(dkv_ref, m_ref, g_ref, w_ref, dw_ref, dg_ref):
        mv = m_ref[...]
        r = _rstd(mv)
        dkv_v = dkv_ref[...].astype(BF16)
        dw_ref[...] = _dot_tn(mv * r * g_ref[...], dkv_v)
        dg_ref[...] = jnp.sum(_dot_nt(dkv_v, w_ref[...]) * (mv * r), axis=0, keepdims=True)

    return pl.pallas_call(
        body, name="memkv_bwd", out_shape=(SDS((D_MODEL, 2 * MEM_W), F32), SDS((1, D_MODEL), F32)),
        compiler_params=_params(32),
    )(dkv, mem2, g_mem, wkv)


def _reduce_scatter(dwint, dwkv, dwout):
    shards = (W_IN_SHARD, ROW_SHARD, ROW_SHARD)

    def body(a_ref, b_ref, c_ref, ra_ref, rb_ref, rc_ref, send_sems, recv_sems, own_sems):
        pos = _mesh_pos()
        me = _flat(pos)
        srcs, dsts = (a_ref, b_ref, c_ref), (ra_ref, rb_ref, rc_ref)

        def piece(a, p):
            n = shards[a]
            return srcs[a].at[pl.ds(pl.multiple_of(_flat(p) * n, 8), n), :]

        def copy(a, k):
            peer = _peer(pos, k)
            return pltpu.make_async_remote_copy(
                src_ref=piece(a, peer), dst_ref=dsts[a].at[me],
                send_sem=send_sems.at[7 * a + k - 1], recv_sem=recv_sems.at[7 * a + k - 1],
                device_id=peer, device_id_type=MESH)

        def arrival(a, k):
            peer = _peer(pos, k)
            return pltpu.make_async_remote_copy(
                src_ref=piece(a, peer), dst_ref=dsts[a].at[_flat(peer)],
                send_sem=send_sems.at[7 * a + k - 1], recv_sem=recv_sems.at[7 * a + k - 1],
                device_id=peer, device_id_type=MESH)

        own = [pltpu.make_async_copy(piece(a, pos), dsts[a].at[me], own_sems.at[a]) for a in range(3)]
        sent = [copy(a, k) for a in range(3) for k in range(1, N_DEV)]
        for cp in own + sent:
            cp.start()
        for a in range(3):
            for k in range(1, N_DEV):
                arrival(a, k).wait_recv()
        for cp in sent:
            cp.wait_send()
        for cp in own:
            cp.wait()

    hbm = pl.BlockSpec(memory_space=pl.ANY)
    return pl.pallas_call(
        body, name="reduce_scatter",
        out_shape=(SDS((N_DEV, W_IN_SHARD, D_MODEL), F32), SDS((N_DEV, ROW_SHARD, 2 * MEM_W), F32),
                   SDS((N_DEV, ROW_SHARD, D_MODEL), F32)),
        in_specs=[hbm, hbm, hbm], out_specs=(hbm, hbm, hbm),
        scratch_shapes=[pltpu.SemaphoreType.DMA((21,)), pltpu.SemaphoreType.DMA((21,)), pltpu.SemaphoreType.DMA((3,))],
    )(dwint, dwkv, dwout)


def _allreduce_small(parts):
    n = len(parts)

    def body(*refs):
        ins, outs, bufs = refs[0:n], refs[n:2 * n], refs[2 * n:3 * n]
        send_sems, recv_sems = refs[3 * n], refs[3 * n + 1]
        pos = _mesh_pos()
        me = _flat(pos)
        for a in range(n):
            bufs[a][me] = ins[a][...]

        def copy(a, k, slot):
            return pltpu.make_async_remote_copy(
                src_ref=ins[a], dst_ref=bufs[a].at[slot],
                send_sem=send_sems.at[7 * a + k - 1], recv_sem=recv_sems.at[7 * a + k - 1],
                device_id=_peer(pos, k), device_id_type=MESH)

        sent = [copy(a, k, me) for a in range(n) for k in range(1, N_DEV)]
        for cp in sent:
            cp.start()
        for a in range(n):
            for k in range(1, N_DEV):
                copy(a, k, _flat(_peer(pos, k))).wait_recv()
        for cp in sent:
            cp.wait_send()
        for a in range(n):
            acc = bufs[a][0]
            for s in range(1, N_DEV):
                acc = acc + bufs[a][s]
            outs[a][...] = acc

    vmem = pl.BlockSpec(memory_space=pltpu.VMEM)
    return pl.pallas_call(
        body, name="allreduce_small",
        out_shape=tuple(SDS(p.shape, F32) for p in parts),
        in_specs=[vmem] * n, out_specs=(vmem,) * n,
        scratch_shapes=[pltpu.VMEM((N_DEV,) + p.shape, F32) for p in parts]
        + [pltpu.SemaphoreType.DMA((7 * n,)), pltpu.SemaphoreType.DMA((7 * n,))],
        compiler_params=_params(16),
    )(*parts)


def _adamw(w, g, m, v):
    m = ADAM_B1 * m + (1.0 - ADAM_B1) * g
    v = ADAM_B2 * v + (1.0 - ADAM_B2) * (g * g)
    m_hat = m / (1.0 - ADAM_B1 ** ADAM_STEP)
    v_hat = v / (1.0 - ADAM_B2 ** ADAM_STEP)
    return -ADAM_LR * (m_hat / (jnp.sqrt(v_hat) + ADAM_EPS) + ADAM_WD * w), m, v


def _adam_shard(recv, w, m, v, transposed):
    _, rows, cols = recv.shape
    acc_rows = 512 if transposed else rows

    def body(r_ref, w_ref, m_ref, v_ref, g_o, d_o, m_o, v_o, acc):
        s = pl.program_id(0)

        @pl.when(s == 0)
        def _():
            if transposed:
                acc[...] = jnp.zeros_like(acc)
            acc[0:rows, :] = r_ref[0]

        @pl.when(s > 0)
        def _():
            acc[0:rows, :] += r_ref[0]

        @pl.when(s == N_DEV - 1)
        def _():
            g = acc[...].T[:, 0:rows] if transposed else acc[...]
            g_o[...] = g
            d_o[...], m_o[...], v_o[...] = _adamw(w_ref[...], g, m_ref[...], v_ref[...])

    full = pl.BlockSpec(w.shape, lambda s: (0, 0))
    return pl.pallas_call(
        body, name="adam_" + ("w_in" if transposed else "rows%d" % cols), grid=(N_DEV,),
        in_specs=[pl.BlockSpec((1, rows, cols), lambda s: (s, 0, 0)), full, full, full],
        out_specs=(full,) * 4, out_shape=(SDS(w.shape, F32),) * 4,
        scratch_shapes=[pltpu.VMEM((acc_rows, cols), F32)],
        compiler_params=_params(48, ("arbitrary",)),
    )(recv, w, m, v)


def _adam_small(ws, gs, ms, vs):
    n = len(ws)

    def body(*refs):
        w_r, g_r, m_r, v_r = refs[0:n], refs[n:2 * n], refs[2 * n:3 * n], refs[3 * n:4 * n]
        outs = refs[4 * n:]
        for a in range(n):
            outs[3 * a][...], outs[3 * a + 1][...], outs[3 * a + 2][...] = _adamw(
                w_r[a][...], g_r[a][...], m_r[a][...], v_r[a][...])

    return pl.pallas_call(
        body, name="adam_small",
        out_shape=tuple(SDS(w.shape, F32) for w in ws for _ in range(3)),
        compiler_params=_params(16),
    )(*ws, *gs, *ms, *vs)


def kernel(x, mem, g_norm, w_in, w_sgu_spatial, b_sgu_spatial, g_sgu_v, g_mem, w_mem_kv, w_out, g_final, loss_target, m_g_norm, m_w_in, m_w_sgu_spatial, m_b_sgu_spatial, m_g_sgu_v, m_g_mem, m_w_mem_kv, m_w_out, m_g_final, v_g_norm, v_w_in, v_w_sgu_spatial, v_b_sgu_spatial, v_g_sgu_v, v_g_mem, v_w_mem_kv, v_w_out, v_g_final):
    x2 = x.reshape(T_LOC, D_MODEL)
    tgt2 = loss_target.reshape(T_LOC, D_MODEL)
    mem2 = mem.reshape(B_LOC * N_MEM, D_MODEL)
    w_s = w_sgu_spatial[0]
    b_exp = jnp.tile(jnp.repeat(b_sgu_spatial[0].T, HEAD, axis=1), (2, 1))
    slope = jnp.power(2.0, -8.0 * (jnp.arange(8, dtype=F32) + 1.0) / 8)
    slopes = jnp.broadcast_to(jnp.repeat(slope.reshape(4, 2), HEAD, axis=1)[:, None, :], (4, 8, 128))

    wint, wkv, wout = _allgather_weights(w_in[0], w_mem_kv[0], w_out[0])
    proj = _proj_fwd(x2, g_norm, wint)
    kv = _memkv_fwd(mem2, g_mem, wkv)
    a, lse = _attn_fwd(proj, slopes)
    gated = _branch_fwd(proj, a, kv, w_s, b_exp, g_sgu_v)
    dh2, loss8, dgf = _outproj_loss(gated, wout, x2, tgt2, g_final.reshape(1, D_MODEL))

    da, dr, dkv, dwout, dws, dbs, dgv = _branch_bwd(dh2, wout, gated, proj, a, kv, w_s, b_exp, g_sgu_v)
    dq, dk, dv = _attn_bwd(proj, slopes, da, a, lse)
    grad_x, dgn = _dx(dq, dk, dv, dr, wint, x2, dh2, g_norm)
    dwint = _dwin(dq, dk, dv, dr, x2, g_norm)
    dwkv, dgm = _memkv_bwd(dkv, mem2, g_mem, wkv)

    r_in, r_kv, r_out = _reduce_scatter(dwint, dwkv, dwout)
    g_small = _allreduce_small([dgn, dws.reshape(4 * CHUNK, CHUNK), dbs, dgv, dgm, dgf])

    g_win, d_win, m_win, v_win = _adam_shard(r_in, w_in[0], m_w_in[0], v_w_in[0], True)
    g_wkv, d_wkv, m_wkv, v_wkv = _adam_shard(r_kv, w_mem_kv[0], m_w_mem_kv[0], v_w_mem_kv[0], False)
    g_wout, d_wout, m_wout, v_wout = _adam_shard(r_out, w_out[0], m_w_out[0], v_w_out[0], False)

    small_shapes = [(1, D_MODEL), (4 * CHUNK, CHUNK), (4, CHUNK), (1, SGU_W), (1, D_MODEL), (1, D_MODEL)]
    pack = lambda arrs: [t.reshape(s) for t, s in zip(arrs, small_shapes)]
    upd = _adam_small(
        pack([g_norm, w_sgu_spatial, b_sgu_spatial, g_sgu_v, g_mem, g_final]), g_small,
        pack([m_g_norm, m_w_sgu_spatial, m_b_sgu_spatial, m_g_sgu_v, m_g_mem, m_g_final]),
        pack([v_g_norm, v_w_sgu_spatial, v_b_sgu_spatial, v_g_sgu_v, v_g_mem, v_g_final]))
    out_shapes = [g_norm.shape, w_sgu_spatial.shape, b_sgu_spatial.shape, g_sgu_v.shape, g_mem.shape, g_final.shape]
    unpack = lambda arrs: [t.reshape(s) for t, s in zip(arrs, out_shapes)]
    gs = unpack(g_small)
    ds, nms, nvs = unpack(upd[0::3]), unpack(upd[1::3]), unpack(upd[2::3])

    loss = lax.psum(loss8[0, 0], ("x", "y", "c"))

    def assemble(small, win, wkv_, wout_):
        return [small[0], win[None], small[1], small[2], small[3], small[4], wkv_[None], wout_[None], small[5]]

    return (loss, grad_x.reshape(x.shape),
            *assemble(gs, g_win, g_wkv, g_wout), *assemble(ds, d_win, d_wkv, d_wout),
            *assemble(nms, m_win, m_wkv, m_wout), *assemble(nvs, v_win, v_wkv, v_wout))
```

```python
import functools

import jax
import jax.numpy as jnp
from jax import lax
from jax.experimental import pallas as pl
from jax.experimental.pallas import tpu as pltpu

F32 = jnp.float32
BF16 = jnp.bfloat16
SDS = jax.ShapeDtypeStruct
MESH = pl.DeviceIdType.MESH

N_DEV = 8
D_MODEL = 1024
SEQ = 2048
B_LOC = 2
T_LOC = B_LOC * SEQ
N_MEM = 256
HEAD = 64
ATTN_W = 512
SGU_W = 256
MEM_W = 256
IN_COLS = 3328
W_IN_SHARD = IN_COLS // N_DEV
ROW_SHARD = D_MODEL // N_DEV
CHUNK = 128
DILATIONS = ((1, 2048), (4, 512), (16, 128))
RADIUS = 64
EPS = 1e-6
NEG = -1e30
SCALE = HEAD ** -0.5
C_QA, C_KA, C_VA, C_ZA, C_UB, C_VB, C_ZB, C_QM, C_ZM = 0, 512, 1024, 1536, 2048, 2304, 2560, 2816, 3072
QKV_W = 1536
REST_W = IN_COLS - QKV_W

ADAM_LR, ADAM_B1, ADAM_B2, ADAM_EPS, ADAM_WD, ADAM_STEP = 0.001, 0.9, 0.999, 1e-08, 0.01, 10

V7X_VMEM_MIB = 64


def _params(vmem_mib, sem=None, **kw):
    assert vmem_mib < V7X_VMEM_MIB
    return pltpu.CompilerParams(vmem_limit_bytes=vmem_mib << 20, dimension_semantics=sem, **kw)


def _dot(a, b):
    return jnp.dot(a.astype(BF16), b.astype(BF16), preferred_element_type=F32)


def _dot_nt(a, b):
    return lax.dot_general(a.astype(BF16), b.astype(BF16), (((1,), (1,)), ((), ())), preferred_element_type=F32)


def _dot_tn(a, b):
    return lax.dot_general(a.astype(BF16), b.astype(BF16), (((0,), (0,)), ((), ())), preferred_element_type=F32)


def _rstd(v):
    return lax.rsqrt(jnp.mean(v * v, axis=-1, keepdims=True) + EPS)


def _rms_bwd(v, r, g, dy):
    gdy = g * dy
    return r * gdy - v * (r * r * r * jnp.mean(gdy * v, axis=-1, keepdims=True))


def _sigmoid(z):
    return 1.0 / (1.0 + jnp.exp(-z))


def _silu_and_grad(z):
    s = _sigmoid(z)
    return z * s, s * (1.0 + z * (1.0 - s))


_G_C = 0.7978845608028654
_G_K = 0.044715


def _gelu_and_grad(v):
    t = jnp.tanh(_G_C * (v + _G_K * (v * v * v)))
    cdf = 0.5 * (1.0 + t)
    return v * cdf, cdf + 0.5 * v * (1.0 - t * t) * (_G_C * (1.0 + 3.0 * _G_K * v * v))


def _cast_rows(src_ref, dst_ref, rows, step=256):
    def one(i, carry):
        r = pl.ds(pl.multiple_of(i * step, step), step)
        dst_ref[r, :] = src_ref[r, :].astype(dst_ref.dtype)
        return carry
    lax.fori_loop(0, rows // step, one, 0)


def _left_lanes(rows):
    return lax.broadcasted_iota(jnp.int32, (rows, 128), 1) < HEAD


def _mesh_pos():
    return lax.axis_index("x"), lax.axis_index("y"), lax.axis_index("c")


def _peer(pos, k):
    x, y, c = pos
    return (1 - x if k & 4 else x, 1 - y if k & 2 else y, 1 - c if k & 1 else c)


def _flat(pos):
    return 4 * pos[0] + 2 * pos[1] + pos[2]


def _allgather_weights(w_in, w_kv, w_out):
    shards = (W_IN_SHARD, ROW_SHARD, ROW_SHARD)

    def body(win_ref, wkv_ref, wout_ref, wint_o, wkv_o, wout_o, tr, send_sems, recv_sems):
        x, y, c = _mesh_pos()
        me, sib = (x, y, c), (x, y, 1 - c)
        chips = [(1 - x, y), (x, 1 - y), (1 - x, 1 - y)]
        outs = (wint_o, wkv_o, wout_o)

        def rows(a, p):
            n = shards[a]
            return outs[a].at[pl.ds(pl.multiple_of(_flat(p) * n, 16), n), :]

        tr[...] = jnp.zeros_like(tr)
        tr[:, 0:W_IN_SHARD] = win_ref[...]
        rows(0, me)[...] = tr[...].T[0:W_IN_SHARD, :].astype(BF16)
        rows(1, me)[...] = wkv_ref[...].astype(BF16)
        rows(2, me)[...] = wout_ref[...].astype(BF16)

        def copy(a, k, block, to):
            return pltpu.make_async_remote_copy(
                src_ref=rows(a, block), dst_ref=rows(a, block),
                send_sem=send_sems.at[7 * a + k], recv_sem=recv_sems.at[7 * a + k],
                device_id=to, device_id_type=MESH)

        first = []
        for a in range(3):
            first.append(copy(a, 0, me, sib))
            first += [copy(a, 1 + j, me, (*chip, c)) for j, chip in enumerate(chips)]
        for cp in first:
            cp.start()
        passed = []
        for a in range(3):
            for j, chip in enumerate(chips):
                copy(a, 1 + j, (*chip, c), me).wait_recv()
                fwd = copy(a, 4 + j, (*chip, c), sib)
                fwd.start()
                passed.append(fwd)
        for a in range(3):
            copy(a, 0, sib, me).wait_recv()
            for j, chip in enumerate(chips):
                copy(a, 4 + j, (*chip, 1 - c), me).wait_recv()
        for cp in first + passed:
            cp.wait_send()

    vmem = pl.BlockSpec(memory_space=pltpu.VMEM)
    return pl.pallas_call(
        body, name="allgather_weights",
        out_shape=(SDS((IN_COLS, D_MODEL), BF16), SDS((D_MODEL, 2 * MEM_W), BF16), SDS((D_MODEL, D_MODEL), BF16)),
        in_specs=[vmem, vmem, vmem], out_specs=(vmem, vmem, vmem),
        scratch_shapes=[pltpu.VMEM((D_MODEL, 512), F32), pltpu.SemaphoreType.DMA((21,)), pltpu.SemaphoreType.DMA((21,))],
        compiler_params=_params(40),
    )(w_in, w_kv, w_out)


def _proj_fwd(x2, g_norm, wint):
    tm = 256

    def body(x_ref, g_ref, w_ref, o_ref):
        xv = x_ref[...]
        h = xv * _rstd(xv) * g_ref[...]
        o_ref[...] = _dot_nt(h, w_ref[...])

    return pl.pallas_call(
        body, name="proj_fwd", grid=(T_LOC // tm,),
        in_specs=[pl.BlockSpec((tm, D_MODEL), lambda i: (i, 0)), pl.BlockSpec((1, D_MODEL), lambda i: (0, 0)),
                  pl.BlockSpec((IN_COLS, D_MODEL), lambda i: (0, 0))],
        out_specs=pl.BlockSpec((tm, IN_COLS), lambda i: (i, 0)),
        out_shape=SDS((T_LOC, IN_COLS), F32),
        compiler_params=_params(48, ("arbitrary",)),
    )(x2, g_norm, wint)


def _memkv_fwd(mem2, g_mem, wkv):
    def body(m_ref, g_ref, w_ref, o_ref):
        mv = m_ref[...]
        o_ref[...] = _dot(mv * _rstd(mv) * g_ref[...], w_ref[...])

    return pl.pallas_call(
        body, name="memkv_fwd", out_shape=SDS((B_LOC * N_MEM, 2 * MEM_W), F32), compiler_params=_params(32),
    )(mem2, g_mem, wkv)


def _attn_geometry(cfg, it):
    d, length = DILATIONS[cfg]
    nblk = length // CHUNK
    if nblk == 1:
        rho, i = it, 0
    else:
        rho, i = it // nblk, it % nblk
    nk = min(length, 2 * CHUNK)
    ks = jnp.clip(i * CHUNK - RADIUS, 0, length - nk) if length > nk else 0
    return rho + d * (i * CHUNK), rho + d * ks, i * CHUNK - ks, nk


def _band(off, nk):
    r = lax.broadcasted_iota(jnp.int32, (CHUNK, nk), 0)
    c = lax.broadcasted_iota(jnp.int32, (CHUNK, nk), 1)
    rel = jnp.abs(r - c + off)
    return rel.astype(F32), rel <= RADIUS


def _rows(start, n, d):
    return pl.ds(start, n) if d == 1 else pl.ds(start, n, stride=d)


def _attn_fwd(proj, slopes):
    def body(sl_ref, q_ref, k_ref, v_ref, a_ref, lse_ref, *scr):
        o_c, m_c, l_c = scr[0:3], scr[3:6], scr[6:9]
        left = _left_lanes(CHUNK)
        sl = (sl_ref[0, 0:1, 0:1], sl_ref[0, 0:1, HEAD:HEAD + 1])
        for cfg, (d, _) in enumerate(DILATIONS):
            def block(it, carry, cfg=cfg, d=d):
                q0, k0, off, nk = _attn_geometry(cfg, it)
                qb = q_ref[_rows(q0, CHUNK, d), :]
                kw = k_ref[_rows(k0, nk, d), :].astype(BF16)
                vw = v_ref[_rows(k0, nk, d), :].astype(BF16)
                dist, valid = _band(off, nk)
                res = []
                for h in range(2):
                    qh = jnp.where(left if h == 0 else ~left, qb, 0.0)
                    s = _dot_nt(qh, kw) * SCALE - (sl[h] * float(d)) * dist
                    s = jnp.where(valid, s, NEG)
                    m = jnp.max(s, axis=-1, keepdims=True)
                    p = jnp.exp(s - m)
                    res.append((_dot(p, vw), m, jnp.sum(p, axis=-1, keepdims=True)))
                rq = _rows(q0, CHUNK, d)
                o_c[cfg][rq, :] = jnp.where(left, res[0][0], res[1][0])
                m_c[cfg][rq, :] = jnp.where(left, res[0][1], res[1][1])
                l_c[cfg][rq, :] = jnp.where(left, res[0][2], res[1][2])
                return carry
            lax.fori_loop(0, 16, block, 0)

        def merge(j, carry):
            rows = pl.ds(pl.multiple_of(j * 256, 256), 256)
            ms = [m_c[i][rows, :] for i in range(3)]
            top = jnp.maximum(jnp.maximum(ms[0], ms[1]), ms[2])
            ws = [jnp.exp(m - top) for m in ms]
            den = l_c[0][rows, :] * ws[0] + l_c[1][rows, :] * ws[1] + l_c[2][rows, :] * ws[2]
            num = o_c[0][rows, :] * ws[0] + o_c[1][rows, :] * ws[1] + o_c[2][rows, :] * ws[2]
            a_ref[rows, :] = num / den
            lse_ref[rows, :] = top + jnp.log(den)
            return carry
        lax.fori_loop(0, SEQ // 256, merge, 0)

    blk = lambda col0: pl.BlockSpec((SEQ, 128), lambda b, hp: (b, col0 // 128 + hp))
    out = pl.BlockSpec((SEQ, 128), lambda b, hp: (b, hp))
    return pl.pallas_call(
        body, name="attn_fwd", grid=(B_LOC, 4),
        in_specs=[pl.BlockSpec((1, 8, 128), lambda b, hp: (hp, 0, 0)), blk(C_QA), blk(C_KA), blk(C_VA)],
        out_specs=(out, out),
        out_shape=(SDS((T_LOC, ATTN_W), F32), SDS((T_LOC, ATTN_W), F32)),
        scratch_shapes=[pltpu.VMEM((SEQ, 128), F32)] * 9,
        compiler_params=_params(40, ("arbitrary", "arbitrary")),
    )(slopes, proj, proj, proj)


def _sgu_mix(vn, ws_ref, dst_ref, tm):
    left = _left_lanes(CHUNK)
    for ch in range(tm // CHUNK):
        for pr in range(2):
            vp = vn[ch * CHUNK:(ch + 1) * CHUNK, pr * 128:(pr + 1) * 128]
            dst_ref[ch * CHUNK:(ch + 1) * CHUNK, pr * 128:(pr + 1) * 128] = jnp.where(
                left, _dot(ws_ref[2 * pr], vp), _dot(ws_ref[2 * pr + 1], vp))


def _mem_attn_head(qp, kp, h, left):
    qh = jnp.where(left if h == 0 else ~left, qp, 0.0)
    s = _dot_nt(qh, kp) * SCALE
    e = jnp.exp(s - jnp.max(s, axis=-1, keepdims=True))
    return e * (1.0 / jnp.sum(e, axis=-1, keepdims=True)), qh


def _branch_blocks(tm):
    col = lambda w, c0: pl.BlockSpec((tm, w), lambda i: (i, c0 // w))
    return [col(512, C_ZA), col(256, C_UB), col(256, C_VB), col(256, C_ZB), col(256, C_QM), col(256, C_ZM)]


def _branch_fwd(proj, a, kv, w_s, b_exp, g_v):
    tm = 256
    per_ex = SEQ // tm

    def body(za_ref, ub_ref, vb_ref, zb_ref, qm_ref, zm_ref, a_ref, kv_ref, ws_ref, be_ref, gv_ref, o_ref, mix):
        left = _left_lanes(tm)
        o_ref[:, 0:ATTN_W] = (_silu_and_grad(za_ref[...])[0] * a_ref[...]).astype(BF16)
        gu = _gelu_and_grad(ub_ref[...])[0]
        gv = _gelu_and_grad(vb_ref[...])[0]
        vn = gv * _rstd(gv) * gv_ref[...]
        _sgu_mix(vn.astype(BF16), ws_ref, mix, tm)
        sg = gu * (mix[...] + be_ref[...])
        o_ref[:, ATTN_W:ATTN_W + SGU_W] = (_silu_and_grad(zb_ref[...])[0] * sg).astype(BF16)
        szm = _silu_and_grad(zm_ref[...])[0]
        for hp in range(2):
            cols = slice(hp * 128, (hp + 1) * 128)
            qp, kp, vp = qm_ref[:, cols], kv_ref[:, cols], kv_ref[:, MEM_W + hp * 128:MEM_W + (hp + 1) * 128]
            o = [_dot(_mem_attn_head(qp, kp, h, left)[0], vp) for h in range(2)]
            c0 = ATTN_W + SGU_W + hp * 128
            o_ref[:, c0:c0 + 128] = (szm[:, cols] * jnp.where(left, o[0], o[1])).astype(BF16)

    full = lambda shape: pl.BlockSpec(shape, lambda i: (0,) * len(shape))
    return pl.pallas_call(
        body, name="branch_fwd", grid=(T_LOC // tm,),
        in_specs=_branch_blocks(tm) + [
            pl.BlockSpec((tm, ATTN_W), lambda i: (i, 0)), pl.BlockSpec((N_MEM, 2 * MEM_W), lambda i: (i // per_ex, 0)),
            full((4, CHUNK, CHUNK)), full((tm, SGU_W)), full((1, SGU_W))],
        out_specs=pl.BlockSpec((tm, D_MODEL), lambda i: (i, 0)),
        out_shape=SDS((T_LOC, D_MODEL), BF16),
        scratch_shapes=[pltpu.VMEM((tm, SGU_W), F32)],
        compiler_params=_params(40, ("arbitrary",)),
    )(proj, proj, proj, proj, proj, proj, a, kv, w_s, b_exp, g_v)


def _outproj_loss(gated, wout, x2, tgt2, g_final):
    tm = 512

    def body(g_ref, w_ref, x_ref, t_ref, gf_ref, dh2_ref, loss_ref, dgf_ref):
        @pl.when(pl.program_id(0) == 0)
        def _():
            loss_ref[...] = jnp.zeros_like(loss_ref)
            dgf_ref[...] = jnp.zeros_like(dgf_ref)
        h2 = x_ref[...] + _dot(g_ref[...], w_ref[...])
        r = _rstd(h2)
        gf = gf_ref[...]
        err = h2 * r * gf - t_ref[...]
        loss_ref[...] += 0.5 * jnp.sum(jnp.mean(err * err, axis=-1, keepdims=True))
        dy = err * (1.0 / D_MODEL)
        dh2_ref[...] = _rms_bwd(h2, r, gf, dy)
        dgf_ref[...] += jnp.sum(dy * (h2 * r), axis=0, keepdims=True)

    row = pl.BlockSpec((tm, D_MODEL), lambda i: (i, 0))
    vec = pl.BlockSpec((1, D_MODEL), lambda i: (0, 0))
    return pl.pallas_call(
        body, name="outproj_loss", grid=(T_LOC // tm,),
        in_specs=[row, pl.BlockSpec((D_MODEL, D_MODEL), lambda i: (0, 0)), row, row, vec],
        out_specs=(row, pl.BlockSpec((8, 128), lambda i: (0, 0)), vec),
        out_shape=(SDS((T_LOC, D_MODEL), F32), SDS((8, 128), F32), SDS((1, D_MODEL), F32)),
        compiler_params=_params(40, ("arbitrary",)),
    )(gated, wout, x2, tgt2, g_final)


def _branch_bwd(dh2, wout, gated, proj, a, kv, w_s, b_exp, g_v):
    tm = 256
    per_ex = SEQ // tm

    def body(dh2_ref, w_ref, g_ref, za_ref, ub_ref, vb_ref, zb_ref, qm_ref, zm_ref, a_ref, kv_ref, ws_ref,
             be_ref, gv_ref, da_ref, dr_ref, dkv_ref, dwo_ref, dws_ref, db_ref, dgv_ref, mix, dvn, dmsum, dwo_acc):
        i = pl.program_id(0)
        left = _left_lanes(tm)
        leftc = _left_lanes(CHUNK)

        @pl.when(i == 0)
        def _():
            dwo_acc[...] = jnp.zeros_like(dwo_acc)
            dws_ref[...] = jnp.zeros_like(dws_ref)
            dgv_ref[...] = jnp.zeros_like(dgv_ref)
            dmsum[...] = jnp.zeros_like(dmsum)

        @pl.when(i % per_ex == 0)
        def _():
            dkv_ref[...] = jnp.zeros_like(dkv_ref)

        dh2 = dh2_ref[...].astype(BF16)
        dwo_acc[...] += _dot_tn(g_ref[...], dh2)
        dg = _dot_nt(dh2, w_ref[...])

        sa, dsa = _silu_and_grad(za_ref[...])
        dga = dg[:, 0:ATTN_W]
        da_ref[...] = dga * sa
        dr_ref[:, 0:512] = (dga * a_ref[...] * dsa).astype(BF16)

        ub, vb = ub_ref[...], vb_ref[...]
        gu, dgu = _gelu_and_grad(ub)
        gv, dgv = _gelu_and_grad(vb)
        rv = _rstd(gv)
        gain = gv_ref[...]
        vn = (gv * rv * gain).astype(BF16)
        _sgu_mix(vn, ws_ref, mix, tm)
        mixed = mix[...] + be_ref[...]
        sb, dsb = _silu_and_grad(zb_ref[...])
        dgb = dg[:, ATTN_W:ATTN_W + SGU_W]
        dsg = dgb * sb
        dr_ref[:, 512:768] = (dsg * mixed * dgu).astype(BF16)
        dr_ref[:, 1024:1280] = (dgb * (gu * mixed) * dsb).astype(BF16)
        dmix = dsg * gu
        for ch in range(tm // CHUNK):
            rows = slice(ch * CHUNK, (ch + 1) * CHUNK)
            dmsum[...] += dmix[rows, :]
            for pr in range(2):
                cols = slice(pr * 128, (pr + 1) * 128)
                dmp, vp = dmix[rows, cols], vn[rows, cols]
                for h in range(2):
                    g = 2 * pr + h
                    dws_ref[g] += _dot_nt(jnp.where(leftc if h == 0 else ~leftc, dmp, 0.0), vp)
                dvn[rows, cols] = jnp.where(leftc, _dot_tn(ws_ref[2 * pr], dmp), _dot_tn(ws_ref[2 * pr + 1], dmp))
        dvn_v = dvn[...]
        dgv_ref[...] += jnp.sum(dvn_v * (gv * rv), axis=0, keepdims=True)
        dr_ref[:, 768:1024] = (_rms_bwd(gv, rv, gain, dvn_v) * dgv).astype(BF16)

        szm, dszm = _silu_and_grad(zm_ref[...])
        dgm = dg[:, ATTN_W + SGU_W:D_MODEL]
        dmo = dgm * szm
        for hp in range(2):
            cols = slice(hp * 128, (hp + 1) * 128)
            vcols = slice(MEM_W + hp * 128, MEM_W + (hp + 1) * 128)
            qp, kp, vp, dmop = qm_ref[:, cols], kv_ref[:, cols], kv_ref[:, vcols], dmo[:, cols]
            o, dq = [], []
            dk = jnp.zeros((N_MEM, 128), F32)
            dv = jnp.zeros((N_MEM, 128), F32)
            for h in range(2):
                p, qh = _mem_attn_head(qp, kp, h, left)
                dmoh = jnp.where(left if h == 0 else ~left, dmop, 0.0)
                o.append(_dot(p, vp))
                dp = _dot_nt(dmoh, vp)
                ds = p * (dp - jnp.sum(p * dp, axis=-1, keepdims=True)) * SCALE
                dq.append(_dot(ds, kp))
                dk += _dot_tn(ds, qh)
                dv += _dot_tn(p, dmoh)
            dr_ref[:, 1280 + hp * 128:1280 + (hp + 1) * 128] = jnp.where(left, dq[0], dq[1]).astype(BF16)
            dr_ref[:, 1536 + hp * 128:1536 + (hp + 1) * 128] = (
                dgm[:, cols] * jnp.where(left, o[0], o[1]) * dszm[:, cols]).astype(BF16)
            dkv_ref[:, cols] += dk
            dkv_ref[:, vcols] += dv

        @pl.when(i == pl.num_programs(0) - 1)
        def _():
            tot = dmsum[...]
            hi = tot.astype(BF16)
            lo = (tot - hi.astype(F32)).astype(BF16)
            grp = (lax.broadcasted_iota(jnp.int32, (SGU_W, 128), 0) // HEAD
                   == lax.broadcasted_iota(jnp.int32, (SGU_W, 128), 1)).astype(BF16)
            db_ref[...] = (_dot(hi, grp) + _dot(lo, grp)).T[0:4, :]
            _cast_rows(dwo_acc, dwo_ref, D_MODEL)

    full = lambda shape: pl.BlockSpec(shape, lambda i: (0,) * len(shape))
    row = lambda w: pl.BlockSpec((tm, w), lambda i: (i, 0))
    return pl.pallas_call(
        body, name="branch_bwd", grid=(T_LOC // tm,),
        in_specs=[row(D_MODEL), full((D_MODEL, D_MODEL)), row(D_MODEL)] + _branch_blocks(tm) + [
            row(ATTN_W), pl.BlockSpec((N_MEM, 2 * MEM_W), lambda i: (i // per_ex, 0)),
            full((4, CHUNK, CHUNK)), full((tm, SGU_W)), full((1, SGU_W))],
        out_specs=(row(ATTN_W), row(REST_W), pl.BlockSpec((N_MEM, 2 * MEM_W), lambda i: (i // per_ex, 0)),
                   full((D_MODEL, D_MODEL)), full((4, CHUNK, CHUNK)), full((4, CHUNK)), full((1, SGU_W))),
        out_shape=(SDS((T_LOC, ATTN_W), F32), SDS((T_LOC, REST_W), BF16), SDS((B_LOC * N_MEM, 2 * MEM_W), F32),
                   SDS((D_MODEL, D_MODEL), BF16), SDS((4, CHUNK, CHUNK), F32), SDS((4, CHUNK), F32), SDS((1, SGU_W), F32)),
        scratch_shapes=[pltpu.VMEM((tm, SGU_W), F32), pltpu.VMEM((tm, SGU_W), F32), pltpu.VMEM((CHUNK, SGU_W), F32),
                        pltpu.VMEM((D_MODEL, D_MODEL), F32)],
        compiler_params=_params(56, ("arbitrary",)),
    )(dh2, wout, gated, proj, proj, proj, proj, proj, proj, a, kv, w_s, b_exp, g_v)


def _attn_bwd(proj, slopes, da, a, lse):
    def body(sl_ref, q_ref, k_ref, v_ref, da_ref, a_ref, lse_ref, dq_ref, dk_ref, dv_ref, dq_s, dk_s, dv_s, delta_s):
        left = _left_lanes(CHUNK)
        sl = (sl_ref[0, 0:1, 0:1], sl_ref[0, 0:1, HEAD:HEAD + 1])

        def prep(j, carry):
            rows = pl.ds(pl.multiple_of(j * 256, 256), 256)
            l256 = _left_lanes(256)
            prod = da_ref[rows, :] * a_ref[rows, :]
            d0 = jnp.sum(jnp.where(l256, prod, 0.0), axis=-1, keepdims=True)
            d1 = jnp.sum(jnp.where(l256, 0.0, prod), axis=-1, keepdims=True)
            delta_s[rows, :] = jnp.where(l256, d0, d1)
            zero = jnp.zeros((256, 128), F32)
            dq_s[rows, :] = zero
            dk_s[rows, :] = zero
            dv_s[rows, :] = zero
            return carry
        lax.fori_loop(0, SEQ // 256, prep, 0)

        for cfg, (d, _) in enumerate(DILATIONS):
            def block(it, carry, cfg=cfg, d=d):
                q0, k0, off, nk = _attn_geometry(cfg, it)
                rq, rk = _rows(q0, CHUNK, d), _rows(k0, nk, d)
                qb, dab = q_ref[rq, :], da_ref[rq, :]
                lse_b, delta_b = lse_ref[rq, :], delta_s[rq, :]
                kw = k_ref[rk, :].astype(BF16)
                vw = v_ref[rk, :].astype(BF16)
                dist, valid = _band(off, nk)
                dq = []
                dk = jnp.zeros((nk, 128), F32)
                dv = jnp.zeros((nk, 128), F32)
                for h in range(2):
                    mask = left if h == 0 else ~left
                    c0 = 0 if h == 0 else HEAD
                    qh = jnp.where(mask, qb, 0.0).astype(BF16)
                    dah = jnp.where(mask, dab, 0.0).astype(BF16)
                    s = _dot_nt(qh, kw) * SCALE - (sl[h] * float(d)) * dist
                    p = jnp.exp(jnp.where(valid, s, NEG) - lse_b[:, c0:c0 + 1])
                    dp = _dot_nt(dah, vw)
                    ds = (p * (dp - delta_b[:, c0:c0 + 1]) * SCALE).astype(BF16)
                    dq.append(_dot(ds, kw))
                    dk += _dot_tn(ds, qh)
                    dv += _dot_tn(p, dah)
                dq_s[rq, :] += jnp.where(left, dq[0], dq[1])
                dk_s[rk, :] += dk
                dv_s[rk, :] += dv
                return carry
            lax.fori_loop(0, 16, block, 0)

        def flush(j, carry):
            rows = pl.ds(pl.multiple_of(j * 256, 256), 256)
            dq_ref[rows, :] = dq_s[rows, :].astype(BF16)
            dk_ref[rows, :] = dk_s[rows, :].astype(BF16)
            dv_ref[rows, :] = dv_s[rows, :].astype(BF16)
            return carry
        lax.fori_loop(0, SEQ // 256, flush, 0)

    blk = lambda col0: pl.BlockSpec((SEQ, 128), lambda b, hp: (b, col0 // 128 + hp))
    own = pl.BlockSpec((SEQ, 128), lambda b, hp: (b, hp))
    return pl.pallas_call(
        body, name="attn_bwd", grid=(B_LOC, 4),
        in_specs=[pl.BlockSpec((1, 8, 128), lambda b, hp: (hp, 0, 0)), blk(C_QA), blk(C_KA), blk(C_VA), own, own, own],
        out_specs=(own, own, own),
        out_shape=(SDS((T_LOC, ATTN_W), BF16),) * 3,
        scratch_shapes=[pltpu.VMEM((SEQ, 128), F32)] * 4,
        compiler_params=_params(40, ("arbitrary", "arbitrary")),
    )(slopes, proj, proj, proj, da, a, lse)


def _dproj_specs(tm):
    third = pl.BlockSpec((tm, ATTN_W), lambda i: (i, 0))
    return [third, third, third, pl.BlockSpec((tm, REST_W), lambda i: (i, 0))]


def _dx(dq, dk, dv, dr, wint, x2, dh2, g_norm):
    tm = 256

    def body(dq_ref, dk_ref, dv_ref, dr_ref, w_ref, x_ref, dh2_ref, g_ref, gx_ref, dgn_ref):
        @pl.when(pl.program_id(0) == 0)
        def _():
            dgn_ref[...] = jnp.zeros_like(dgn_ref)
        dh = (_dot(dq_ref[...], w_ref[C_QA:C_KA, :]) + _dot(dk_ref[...], w_ref[C_KA:C_VA, :])
              + _dot(dv_ref[...], w_ref[C_VA:C_ZA, :]) + _dot(dr_ref[...], w_ref[C_ZA:IN_COLS, :]))
        xv = x_ref[...]
        r = _rstd(xv)
        gx_ref[...] = dh2_ref[...] + _rms_bwd(xv, r, g_ref[...], dh)
        dgn_ref[...] += jnp.sum(dh * (xv * r), axis=0, keepdims=True)

    row = pl.BlockSpec((tm, D_MODEL), lambda i: (i, 0))
    vec = pl.BlockSpec((1, D_MODEL), lambda i: (0, 0))
    return pl.pallas_call(
        body, name="dx", grid=(T_LOC // tm,),
        in_specs=_dproj_specs(tm) + [pl.BlockSpec((IN_COLS, D_MODEL), lambda i: (0, 0)), row, row, vec],
        out_specs=(row, vec),
        out_shape=(SDS((T_LOC, D_MODEL), F32), SDS((1, D_MODEL), F32)),
        compiler_params=_params(48, ("arbitrary",)),
    )(dq, dk, dv, dr, wint, x2, dh2, g_norm)


def _dwin(dq, dk, dv, dr, x2, g_norm):
    tm = 512

    def body(dq_ref, dk_ref, dv_ref, dr_ref, x_ref, g_ref, o_ref, acc):
        @pl.when(pl.program_id(0) == 0)
        def _():
            acc[...] = jnp.zeros_like(acc)
        xv = x_ref[...]
        h = (xv * _rstd(xv) * g_ref[...]).astype(BF16)
        acc[C_QA:C_KA, :] += _dot_tn(dq_ref[...], h)
        acc[C_KA:C_VA, :] += _dot_tn(dk_ref[...], h)
        acc[C_VA:C_ZA, :] += _dot_tn(dv_ref[...], h)
        acc[C_ZA:IN_COLS, :] += _dot_tn(dr_ref[...], h)

        @pl.when(pl.program_id(0) == pl.num_programs(0) - 1)
        def _():
            _cast_rows(acc, o_ref, IN_COLS)

    return pl.pallas_call(
        body, name="dwin", grid=(T_LOC // tm,),
        in_specs=_dproj_specs(tm) + [pl.BlockSpec((tm, D_MODEL), lambda i: (i, 0)), pl.BlockSpec((1, D_MODEL), lambda i: (0, 0))],
        out_specs=pl.BlockSpec((IN_COLS, D_MODEL), lambda i: (0, 0)),
        out_shape=SDS((IN_COLS, D_MODEL), BF16),
        scratch_shapes=[pltpu.VMEM((IN_COLS, D_MODEL), F32)],
        compiler_params=_params(56, ("arbitrary",)),
    )(dq, dk, dv, dr, x2, g_norm)


def _memkv_bwd(dkv, mem2, g_mem, wkv):
    def body(dkv_ref, m_ref, g_ref, w_ref, dw_ref, dg_ref):
        mv = m_ref[...]
        r = _rstd(mv)
        dkv_v = dkv_ref[...].astype(BF16)
        dw_ref[...] = _dot_tn(mv * r * g_ref[...], dkv_v).astype(BF16)
        dg_ref[...] = jnp.sum(_dot_nt(dkv_v, w_ref[...]) * (mv * r), axis=0, keepdims=True)

    return pl.pallas_call(
        body, name="memkv_bwd", out_shape=(SDS((D_MODEL, 2 * MEM_W), BF16), SDS((1, D_MODEL), F32)),
        compiler_params=_params(32),
    )(dkv, mem2, g_mem, wkv)


def _reduce_scatter(dwint, dwkv, dwout):
    shards = (W_IN_SHARD, ROW_SHARD, ROW_SHARD)
    widths = (D_MODEL, 2 * MEM_W, D_MODEL)
    step = 32

    def body(a_ref, b_ref, c_ref, ga_ref, gb_ref, gc_ref, *scr):
        from_sib, pair_sum, from_chip = scr[0:3], scr[3:6], scr[6:9]
        send_sems, recv_sems = scr[9], scr[10]
        x, y, c = _mesh_pos()
        srcs, outs = (a_ref, b_ref, c_ref), (ga_ref, gb_ref, gc_ref)
        others = [(1 - x, y), (x, 1 - y), (1 - x, 1 - y)]

        def piece(a, p):
            n = shards[a]
            return srcs[a].at[pl.ds(pl.multiple_of(_flat(p) * n, 16), n), :]

        def to_sibling(a, q):
            chip = (q // 2, q % 2)
            return pltpu.make_async_remote_copy(
                src_ref=piece(a, (*chip, 1 - c)), dst_ref=from_sib[a].at[q],
                send_sem=send_sems.at[4 * a + q], recv_sem=recv_sems.at[4 * a + q],
                device_id=(x, y, 1 - c), device_id_type=MESH)

        def to_chip(a, j):
            return pltpu.make_async_remote_copy(
                src_ref=pair_sum[a].at[j], dst_ref=from_chip[a].at[j],
                send_sem=send_sems.at[12 + 3 * a + j], recv_sem=recv_sems.at[12 + 3 * a + j],
                device_id=(*others[j], c), device_id_type=MESH)

        def add_rows(a, terms, store):
            def one(i, carry):
                rows = pl.ds(pl.multiple_of(i * step, step), step)
                tot = terms[0][rows, :].astype(F32)
                for t in terms[1:]:
                    tot = tot + t[rows, :].astype(F32)
                store(rows, tot)
                return carry
            lax.fori_loop(0, shards[a] // step, one, 0)

        first = [to_sibling(a, q) for a in range(3) for q in range(4)]
        for cp in first:
            cp.start()
        second = []
        for a in range(3):
            for q in range(4):
                to_sibling(a, q).wait_recv()
            for j, chip in enumerate(others):
                def store(rows, tot, a=a, j=j):
                    pair_sum[a][j, rows, :] = tot.astype(BF16)
                add_rows(a, [piece(a, (*chip, c)), from_sib[a].at[2 * chip[0] + chip[1]]], store)
                cp = to_chip(a, j)
                cp.start()
                second.append(cp)
        for a in range(3):
            for j in range(3):
                to_chip(a, j).wait_recv()

            def store(rows, tot, a=a):
                outs[a][rows, :] = tot
            add_rows(a, [piece(a, (x, y, c)), from_sib[a].at[2 * x + y]] + [from_chip[a].at[j] for j in range(3)], store)
        for cp in first + second:
            cp.wait_send()

    vmem = pl.BlockSpec(memory_space=pltpu.VMEM)
    return pl.pallas_call(
        body, name="reduce_scatter",
        out_shape=tuple(SDS((n, w), F32) for n, w in zip(shards, widths)),
        in_specs=[vmem, vmem, vmem], out_specs=(vmem, vmem, vmem),
        scratch_shapes=[pltpu.VMEM((4, n, w), BF16) for n, w in zip(shards, widths)]
        + [pltpu.VMEM((3, n, w), BF16) for n, w in zip(shards, widths)] * 2
        + [pltpu.SemaphoreType.DMA((21,)), pltpu.SemaphoreType.DMA((21,))],
        compiler_params=_params(40),
    )(dwint, dwkv, dwout)


def _allreduce_small(parts):
    n = len(parts)

    def body(*refs):
        ins, outs, bufs = refs[0:n], refs[n:2 * n], refs[2 * n:3 * n]
        send_sems, recv_sems = refs[3 * n], refs[3 * n + 1]
        pos = _mesh_pos()
        me = _flat(pos)
        for a in range(n):
            bufs[a][me] = ins[a][...]

        def copy(a, k, slot):
            return pltpu.make_async_remote_copy(
                src_ref=ins[a], dst_ref=bufs[a].at[slot],
                send_sem=send_sems.at[7 * a + k - 1], recv_sem=recv_sems.at[7 * a + k - 1],
                device_id=_peer(pos, k), device_id_type=MESH)

        sent = [copy(a, k, me) for a in range(n) for k in range(1, N_DEV)]
        for cp in sent:
            cp.start()
        for a in range(n):
            for k in range(1, N_DEV):
                copy(a, k, _flat(_peer(pos, k))).wait_recv()
        for cp in sent:
            cp.wait_send()
        for a in range(n):
            acc = bufs[a][0]
            for s in range(1, N_DEV):
                acc = acc + bufs[a][s]
            outs[a][...] = acc

    vmem = pl.BlockSpec(memory_space=pltpu.VMEM)
    return pl.pallas_call(
        body, name="allreduce_small",
        out_shape=tuple(SDS(p.shape, F32) for p in parts),
        in_specs=[vmem] * n, out_specs=(vmem,) * n,
        scratch_shapes=[pltpu.VMEM((N_DEV,) + p.shape, F32) for p in parts]
        + [pltpu.SemaphoreType.DMA((7 * n,)), pltpu.SemaphoreType.DMA((7 * n,))],
        compiler_params=_params(16),
    )(*parts)


def _adamw(w, g, m, v):
    m = ADAM_B1 * m + (1.0 - ADAM_B1) * g
    v = ADAM_B2 * v + (1.0 - ADAM_B2) * (g * g)
    m_hat = m / (1.0 - ADAM_B1 ** ADAM_STEP)
    v_hat = v / (1.0 - ADAM_B2 ** ADAM_STEP)
    return -ADAM_LR * (m_hat / (jnp.sqrt(v_hat) + ADAM_EPS) + ADAM_WD * w), m, v


def _adam_shard(grad, w, m, v, transposed):
    rows, cols = grad.shape

    def body(g_ref, w_ref, m_ref, v_ref, g_o, d_o, m_o, v_o, *scr):
        if transposed:
            pad, = scr
            pad[...] = jnp.zeros_like(pad)
            pad[0:rows, :] = g_ref[...]
            g = pad[...].T[:, 0:rows]
        else:
            g = g_ref[...]
        g_o[...] = g
        d_o[...], m_o[...], v_o[...] = _adamw(w_ref[...], g, m_ref[...], v_ref[...])

    return pl.pallas_call(
        body, name="adam_" + ("w_in" if transposed else "rows%d" % cols),
        out_shape=(SDS(w.shape, F32),) * 4,
        scratch_shapes=[pltpu.VMEM((512, cols), F32)] if transposed else [],
        compiler_params=_params(40),
    )(grad, w, m, v)


def _adam_small(ws, gs, ms, vs):
    n = len(ws)

    def body(*refs):
        w_r, g_r, m_r, v_r = refs[0:n], refs[n:2 * n], refs[2 * n:3 * n], refs[3 * n:4 * n]
        outs = refs[4 * n:]
        for a in range(n):
            outs[3 * a][...], outs[3 * a + 1][...], outs[3 * a + 2][...] = _adamw(
                w_r[a][...], g_r[a][...], m_r[a][...], v_r[a][...])

    return pl.pallas_call(
        body, name="adam_small",
        out_shape=tuple(SDS(w.shape, F32) for w in ws for _ in range(3)),
        compiler_params=_params(16),
    )(*ws, *gs, *ms, *vs)


def kernel(x, mem, g_norm, w_in, w_sgu_spatial, b_sgu_spatial, g_sgu_v, g_mem, w_mem_kv, w_out, g_final, loss_target, m_g_norm, m_w_in, m_w_sgu_spatial, m_b_sgu_spatial, m_g_sgu_v, m_g_mem, m_w_mem_kv, m_w_out, m_g_final, v_g_norm, v_w_in, v_w_sgu_spatial, v_b_sgu_spatial, v_g_sgu_v, v_g_mem, v_w_mem_kv, v_w_out, v_g_final):
    x2 = x.reshape(T_LOC, D_MODEL)
    tgt2 = loss_target.reshape(T_LOC, D_MODEL)
    mem2 = mem.reshape(B_LOC * N_MEM, D_MODEL)
    w_s = w_sgu_spatial[0]
    b_exp = jnp.tile(jnp.repeat(b_sgu_spatial[0].T, HEAD, axis=1), (2, 1))
    slope = jnp.power(2.0, -8.0 * (jnp.arange(8, dtype=F32) + 1.0) / 8)
    slopes = jnp.broadcast_to(jnp.repeat(slope.reshape(4, 2), HEAD, axis=1)[:, None, :], (4, 8, 128))

    wint, wkv, wout = _allgather_weights(w_in[0], w_mem_kv[0], w_out[0])
    proj = _proj_fwd(x2, g_norm, wint)
    kv = _memkv_fwd(mem2, g_mem, wkv)
    a, lse = _attn_fwd(proj, slopes)
    gated = _branch_fwd(proj, a, kv, w_s, b_exp, g_sgu_v)
    dh2, loss8, dgf = _outproj_loss(gated, wout, x2, tgt2, g_final.reshape(1, D_MODEL))

    da, dr, dkv, dwout, dws, dbs, dgv = _branch_bwd(dh2, wout, gated, proj, a, kv, w_s, b_exp, g_sgu_v)
    dq, dk, dv = _attn_bwd(proj, slopes, da, a, lse)
    grad_x, dgn = _dx(dq, dk, dv, dr, wint, x2, dh2, g_norm)
    dwint = _dwin(dq, dk, dv, dr, x2, g_norm)
    dwkv, dgm = _memkv_bwd(dkv, mem2, g_mem, wkv)

    r_in, r_kv, r_out = _reduce_scatter(dwint, dwkv, dwout)
    g_small = _allreduce_small([dgn, dws.reshape(4 * CHUNK, CHUNK), dbs, dgv, dgm, dgf])

    g_win, d_win, m_win, v_win = _adam_shard(r_in, w_in[0], m_w_in[0], v_w_in[0], True)
    g_wkv, d_wkv, m_wkv, v_wkv = _adam_shard(r_kv, w_mem_kv[0], m_w_mem_kv[0], v_w_mem_kv[0], False)
    g_wout, d_wout, m_wout, v_wout = _adam_shard(r_out, w_out[0], m_w_out[0], v_w_out[0], False)

    small_shapes = [(1, D_MODEL), (4 * CHUNK, CHUNK), (4, CHUNK), (1, SGU_W), (1, D_MODEL), (1, D_MODEL)]
    pack = lambda arrs: [t.reshape(s) for t, s in zip(arrs, small_shapes)]
    upd = _adam_small(
        pack([g_norm, w_sgu_spatial, b_sgu_spatial, g_sgu_v, g_mem, g_final]), g_small,
        pack([m_g_norm, m_w_sgu_spatial, m_b_sgu_spatial, m_g_sgu_v, m_g_mem, m_g_final]),
        pack([v_g_norm, v_w_sgu_spatial, v_b_sgu_spatial, v_g_sgu_v, v_g_mem, v_g_final]))
    out_shapes = [g_norm.shape, w_sgu_spatial.shape, b_sgu_spatial.shape, g_sgu_v.shape, g_mem.shape, g_final.shape]
    unpack = lambda arrs: [t.reshape(s) for t, s in zip(arrs, out_shapes)]
    gs = unpack(g_small)
    ds, nms, nvs = unpack(upd[0::3]), unpack(upd[1::3]), unpack(upd[2::3])

    loss = lax.psum(loss8[0, 0], ("x", "y", "c"))

    def assemble(small, win, wkv_, wout_):
        return [small[0], win[None], small[1], small[2], small[3], small[4], wkv_[None], wout_[None], small[5]]

    return (loss, grad_x.reshape(x.shape),
            *assemble(gs, g_win, g_wkv, g_wout), *assemble(ds, d_win, d_wkv, d_wout),
            *assemble(nms, m_win, m_wkv, m_wout), *assemble(nvs, v_win, v_wkv, v_wout))
```

```python
import functools

import jax
import jax.numpy as jnp
from jax import lax
from jax.experimental import pallas as pl
from jax.experimental.pallas import tpu as pltpu

F32 = jnp.float32
BF16 = jnp.bfloat16
SDS = jax.ShapeDtypeStruct
MESH = pl.DeviceIdType.MESH

N_DEV = 8
D_MODEL = 1024
SEQ = 2048
B_LOC = 2
T_LOC = B_LOC * SEQ
N_MEM = 256
HEAD = 64
ATTN_W = 512
SGU_W = 256
MEM_W = 256
IN_COLS = 3328
W_IN_SHARD = IN_COLS // N_DEV
ROW_SHARD = D_MODEL // N_DEV
CHUNK = 128
DILATIONS = ((1, 2048), (4, 512), (16, 128))
RADIUS = 64
EPS = 1e-6
NEG = -1e30
SCALE = HEAD ** -0.5
C_QA, C_KA, C_VA, C_ZA, C_UB, C_VB, C_ZB, C_QM, C_ZM = 0, 512, 1024, 1536, 2048, 2304, 2560, 2816, 3072
QKV_W = 1536
REST_W = IN_COLS - QKV_W

ADAM_LR, ADAM_B1, ADAM_B2, ADAM_EPS, ADAM_WD, ADAM_STEP = 0.001, 0.9, 0.999, 1e-08, 0.01, 10

V7X_VMEM_MIB = 64


def _params(vmem_mib, sem=None, **kw):
    assert vmem_mib < V7X_VMEM_MIB
    return pltpu.CompilerParams(vmem_limit_bytes=vmem_mib << 20, dimension_semantics=sem, **kw)


def _dot(a, b):
    return jnp.dot(a.astype(BF16), b.astype(BF16), preferred_element_type=F32)


def _dot_nt(a, b):
    return lax.dot_general(a.astype(BF16), b.astype(BF16), (((1,), (1,)), ((), ())), preferred_element_type=F32)


def _dot_tn(a, b):
    return lax.dot_general(a.astype(BF16), b.astype(BF16), (((0,), (0,)), ((), ())), preferred_element_type=F32)


def _rstd(v):
    return lax.rsqrt(jnp.mean(v * v, axis=-1, keepdims=True) + EPS)


def _rms_bwd(v, r, g, dy):
    gdy = g * dy
    return r * gdy - v * (r * r * r * jnp.mean(gdy * v, axis=-1, keepdims=True))


def _sigmoid(z):
    return 1.0 / (1.0 + jnp.exp(-z))


def _silu_and_grad(z):
    s = _sigmoid(z)
    return z * s, s * (1.0 + z * (1.0 - s))


_G_C = 0.7978845608028654
_G_K = 0.044715


def _gelu_and_grad(v):
    t = jnp.tanh(_G_C * (v + _G_K * (v * v * v)))
    cdf = 0.5 * (1.0 + t)
    return v * cdf, cdf + 0.5 * v * (1.0 - t * t) * (_G_C * (1.0 + 3.0 * _G_K * v * v))


def _cast_rows(src_ref, dst_ref, rows, step=256):
    def one(i, carry):
        r = pl.ds(pl.multiple_of(i * step, step), step)
        dst_ref[r, :] = src_ref[r, :].astype(dst_ref.dtype)
        return carry
    lax.fori_loop(0, rows // step, one, 0)


def _left_lanes(rows):
    return lax.broadcasted_iota(jnp.int32, (rows, 128), 1) < HEAD


def _mesh_pos():
    return lax.axis_index("x"), lax.axis_index("y"), lax.axis_index("c")


def _peer(pos, k):
    x, y, c = pos
    return (1 - x if k & 4 else x, 1 - y if k & 2 else y, 1 - c if k & 1 else c)


def _flat(pos):
    return 4 * pos[0] + 2 * pos[1] + pos[2]


def _allgather_weights(w_in, w_kv, w_out):
    shards = (W_IN_SHARD, ROW_SHARD, ROW_SHARD)

    def body(win_ref, wkv_ref, wout_ref, wint_o, wkv_o, wout_o, tr, send_sems, recv_sems):
        x, y, c = _mesh_pos()
        me, sib = (x, y, c), (x, y, 1 - c)
        chips = [(1 - x, y), (x, 1 - y), (1 - x, 1 - y)]
        outs = (wint_o, wkv_o, wout_o)

        def rows(a, p):
            n = shards[a]
            return outs[a].at[pl.ds(pl.multiple_of(_flat(p) * n, 16), n), :]

        tr[...] = jnp.zeros_like(tr)
        tr[:, 0:W_IN_SHARD] = win_ref[...]
        rows(0, me)[...] = tr[...].T[0:W_IN_SHARD, :].astype(BF16)
        rows(1, me)[...] = wkv_ref[...].astype(BF16)
        rows(2, me)[...] = wout_ref[...].astype(BF16)

        def copy(a, k, block, to):
            return pltpu.make_async_remote_copy(
                src_ref=rows(a, block), dst_ref=rows(a, block),
                send_sem=send_sems.at[7 * a + k], recv_sem=recv_sems.at[7 * a + k],
                device_id=to, device_id_type=MESH)

        first = []
        for a in range(3):
            first.append(copy(a, 0, me, sib))
            first += [copy(a, 1 + j, me, (*chip, c)) for j, chip in enumerate(chips)]
        for cp in first:
            cp.start()
        passed = []
        for a in range(3):
            for j, chip in enumerate(chips):
                copy(a, 1 + j, (*chip, c), me).wait_recv()
                fwd = copy(a, 4 + j, (*chip, c), sib)
                fwd.start()
                passed.append(fwd)
        for a in range(3):
            copy(a, 0, sib, me).wait_recv()
            for j, chip in enumerate(chips):
                copy(a, 4 + j, (*chip, 1 - c), me).wait_recv()
        for cp in first + passed:
            cp.wait_send()

    vmem = pl.BlockSpec(memory_space=pltpu.VMEM)
    return pl.pallas_call(
        body, name="allgather_weights",
        out_shape=(SDS((IN_COLS, D_MODEL), BF16), SDS((D_MODEL, 2 * MEM_W), BF16), SDS((D_MODEL, D_MODEL), BF16)),
        in_specs=[vmem, vmem, vmem], out_specs=(vmem, vmem, vmem),
        scratch_shapes=[pltpu.VMEM((D_MODEL, 512), F32), pltpu.SemaphoreType.DMA((21,)), pltpu.SemaphoreType.DMA((21,))],
        compiler_params=_params(40),
    )(w_in, w_kv, w_out)


def _proj_fwd(x2, g_norm, wint):
    tm = 256

    def body(x_ref, g_ref, w_ref, o_ref):
        xv = x_ref[...]
        h = xv * _rstd(xv) * g_ref[...]
        o_ref[...] = _dot_nt(h, w_ref[...])

    return pl.pallas_call(
        body, name="proj_fwd", grid=(T_LOC // tm,),
        in_specs=[pl.BlockSpec((tm, D_MODEL), lambda i: (i, 0)), pl.BlockSpec((1, D_MODEL), lambda i: (0, 0)),
                  pl.BlockSpec((IN_COLS, D_MODEL), lambda i: (0, 0))],
        out_specs=pl.BlockSpec((tm, IN_COLS), lambda i: (i, 0)),
        out_shape=SDS((T_LOC, IN_COLS), F32),
        compiler_params=_params(48, ("arbitrary",)),
    )(x2, g_norm, wint)


def _memkv_fwd(mem2, g_mem, wkv):
    def body(m_ref, g_ref, w_ref, o_ref):
        mv = m_ref[...]
        o_ref[...] = _dot(mv * _rstd(mv) * g_ref[...], w_ref[...])

    return pl.pallas_call(
        body, name="memkv_fwd", out_shape=SDS((B_LOC * N_MEM, 2 * MEM_W), F32), compiler_params=_params(32),
    )(mem2, g_mem, wkv)


N_BIAS = 7


def _fill_bias_tables(sl_ref, tab):
    for cfg, (d, length) in enumerate(DILATIONS):
        nk = min(length, 2 * CHUNK)
        r = lax.broadcasted_iota(jnp.int32, (CHUNK, nk), 0)
        c = lax.broadcasted_iota(jnp.int32, (CHUNK, nk), 1)
        for var in range(3 if length > nk else 1):
            rel = jnp.abs(r - c + var * RADIUS)
            dist = rel.astype(F32) * float(d)
            for h in range(2):
                slope = sl_ref[0, 0:1, h * HEAD:h * HEAD + 1]
                tab[3 * cfg + var, h * CHUNK:(h + 1) * CHUNK, 0:nk] = jnp.where(rel <= RADIUS, -slope * dist, NEG)


def _attn_blocks(visit, unroll):
    def step(t, carry):
        for cfg, (d, length) in enumerate(DILATIONS):
            nblk = length // CHUNK
            if nblk == 1:
                visit(cfg, 0, t, t, length)
                continue
            rho, i = (0, t) if d == 1 else (t // nblk, t % nblk)
            ks = jnp.clip(i * CHUNK - RADIUS, 0, length - 2 * CHUNK)
            visit(cfg, (i * CHUNK - ks) // RADIUS, rho + d * (i * CHUNK), rho + d * ks, 2 * CHUNK)
        return carry
    lax.fori_loop(0, 16, step, 0, unroll=unroll)


def _stack_heads(v, left):
    return jnp.concatenate([jnp.where(left, v, 0.0), jnp.where(left, 0.0, v)], axis=0)


def _unstack_heads(v, left):
    return jnp.where(left, v[0:CHUNK], v[CHUNK:2 * CHUNK])


def _rows(start, n, d):
    return pl.ds(start, n) if d == 1 else pl.ds(start, n, stride=d)


def _attn_fwd(proj, slopes):
    def body(sl_ref, q_ref, k_ref, v_ref, a_ref, lse_ref, *scr):
        o_c, m_c, l_c, tab = scr[0:3], scr[3:6], scr[6:9], scr[9]
        left = _left_lanes(CHUNK)
        _fill_bias_tables(sl_ref, tab)

        def block(cfg, var, q0, k0, nk):
            d = DILATIONS[cfg][0]
            rq, rk = _rows(q0, CHUNK, d), _rows(k0, nk, d)
            qs = _stack_heads(q_ref[rq, :] * SCALE, left)
            s = _dot_nt(qs, k_ref[rk, :]) + tab[3 * cfg + var, :, 0:nk]
            m = jnp.max(s, axis=-1, keepdims=True)
            p = jnp.exp(s - m)
            o_c[cfg][rq, :] = _unstack_heads(_dot(p, v_ref[rk, :]), left)
            m_c[cfg][rq, :] = _unstack_heads(m, left)
            l_c[cfg][rq, :] = _unstack_heads(jnp.sum(p, axis=-1, keepdims=True), left)
        _attn_blocks(block, 2)

        def merge(j, carry):
            rows = pl.ds(pl.multiple_of(j * 256, 256), 256)
            ms = [m_c[i][rows, :] for i in range(3)]
            top = jnp.maximum(jnp.maximum(ms[0], ms[1]), ms[2])
            ws = [jnp.exp(m - top) for m in ms]
            den = l_c[0][rows, :] * ws[0] + l_c[1][rows, :] * ws[1] + l_c[2][rows, :] * ws[2]
            num = o_c[0][rows, :] * ws[0] + o_c[1][rows, :] * ws[1] + o_c[2][rows, :] * ws[2]
            a_ref[rows, :] = num / den
            lse_ref[rows, :] = top + jnp.log(den)
            return carry
        lax.fori_loop(0, SEQ // 256, merge, 0)

    blk = lambda col0: pl.BlockSpec((SEQ, 128), lambda b, hp: (b, col0 // 128 + hp))
    out = pl.BlockSpec((SEQ, 128), lambda b, hp: (b, hp))
    return pl.pallas_call(
        body, name="attn_fwd", grid=(B_LOC, 4),
        in_specs=[pl.BlockSpec((1, 8, 128), lambda b, hp: (hp, 0, 0)), blk(C_QA), blk(C_KA), blk(C_VA)],
        out_specs=(out, out),
        out_shape=(SDS((T_LOC, ATTN_W), F32), SDS((T_LOC, ATTN_W), F32)),
        scratch_shapes=[pltpu.VMEM((SEQ, 128), F32)] * 9 + [pltpu.VMEM((N_BIAS, 2 * CHUNK, 2 * CHUNK), F32)],
        compiler_params=_params(40, ("arbitrary", "arbitrary")),
    )(slopes, proj, proj, proj)


def _sgu_mix(vn, ws_ref, dst_ref, tm):
    left = _left_lanes(CHUNK)
    for ch in range(tm // CHUNK):
        for pr in range(2):
            vp = vn[ch * CHUNK:(ch + 1) * CHUNK, pr * 128:(pr + 1) * 128]
            dst_ref[ch * CHUNK:(ch + 1) * CHUNK, pr * 128:(pr + 1) * 128] = jnp.where(
                left, _dot(ws_ref[2 * pr], vp), _dot(ws_ref[2 * pr + 1], vp))


def _mem_attn_head(qp, kp, h, left):
    qh = jnp.where(left if h == 0 else ~left, qp, 0.0)
    s = _dot_nt(qh, kp) * SCALE
    e = jnp.exp(s - jnp.max(s, axis=-1, keepdims=True))
    return e * (1.0 / jnp.sum(e, axis=-1, keepdims=True)), qh


def _branch_blocks(tm):
    col = lambda w, c0: pl.BlockSpec((tm, w), lambda i: (i, c0 // w))
    return [col(512, C_ZA), col(256, C_UB), col(256, C_VB), col(256, C_ZB), col(256, C_QM), col(256, C_ZM)]


def _branch_fwd(proj, a, kv, w_s, b_exp, g_v):
    tm = 256
    per_ex = SEQ // tm

    def body(za_ref, ub_ref, vb_ref, zb_ref, qm_ref, zm_ref, a_ref, kv_ref, ws_ref, be_ref, gv_ref, o_ref, mix):
        left = _left_lanes(tm)
        o_ref[:, 0:ATTN_W] = (_silu_and_grad(za_ref[...])[0] * a_ref[...]).astype(BF16)
        gu = _gelu_and_grad(ub_ref[...])[0]
        gv = _gelu_and_grad(vb_ref[...])[0]
        vn = gv * _rstd(gv) * gv_ref[...]
        _sgu_mix(vn.astype(BF16), ws_ref, mix, tm)
        sg = gu * (mix[...] + be_ref[...])
        o_ref[:, ATTN_W:ATTN_W + SGU_W] = (_silu_and_grad(zb_ref[...])[0] * sg).astype(BF16)
        szm = _silu_and_grad(zm_ref[...])[0]
        for hp in range(2):
            cols = slice(hp * 128, (hp + 1) * 128)
            qp, kp, vp = qm_ref[:, cols], kv_ref[:, cols], kv_ref[:, MEM_W + hp * 128:MEM_W + (hp + 1) * 128]
            o = [_dot(_mem_attn_head(qp, kp, h, left)[0], vp) for h in range(2)]
            c0 = ATTN_W + SGU_W + hp * 128
            o_ref[:, c0:c0 + 128] = (szm[:, cols] * jnp.where(left, o[0], o[1])).astype(BF16)

    full = lambda shape: pl.BlockSpec(shape, lambda i: (0,) * len(shape))
    return pl.pallas_call(
        body, name="branch_fwd", grid=(T_LOC // tm,),
        in_specs=_branch_blocks(tm) + [
            pl.BlockSpec((tm, ATTN_W), lambda i: (i, 0)), pl.BlockSpec((N_MEM, 2 * MEM_W), lambda i: (i // per_ex, 0)),
            full((4, CHUNK, CHUNK)), full((tm, SGU_W)), full((1, SGU_W))],
        out_specs=pl.BlockSpec((tm, D_MODEL), lambda i: (i, 0)),
        out_shape=SDS((T_LOC, D_MODEL), BF16),
        scratch_shapes=[pltpu.VMEM((tm, SGU_W), F32)],
        compiler_params=_params(40, ("arbitrary",)),
    )(proj, proj, proj, proj, proj, proj, a, kv, w_s, b_exp, g_v)


def _outproj_loss(gated, wout, x2, tgt2, g_final):
    tm = 512

    def body(g_ref, w_ref, x_ref, t_ref, gf_ref, dh2_ref, loss_ref, dgf_ref):
        @pl.when(pl.program_id(0) == 0)
        def _():
            loss_ref[...] = jnp.zeros_like(loss_ref)
            dgf_ref[...] = jnp.zeros_like(dgf_ref)
        h2 = x_ref[...] + _dot(g_ref[...], w_ref[...])
        r = _rstd(h2)
        gf = gf_ref[...]
        err = h2 * r * gf - t_ref[...]
        loss_ref[...] += 0.5 * jnp.sum(jnp.mean(err * err, axis=-1, keepdims=True))
        dy = err * (1.0 / D_MODEL)
        dh2_ref[...] = _rms_bwd(h2, r, gf, dy)
        dgf_ref[...] += jnp.sum(dy * (h2 * r), axis=0, keepdims=True)

    row = pl.BlockSpec((tm, D_MODEL), lambda i: (i, 0))
    vec = pl.BlockSpec((1, D_MODEL), lambda i: (0, 0))
    return pl.pallas_call(
        body, name="outproj_loss", grid=(T_LOC // tm,),
        in_specs=[row, pl.BlockSpec((D_MODEL, D_MODEL), lambda i: (0, 0)), row, row, vec],
        out_specs=(row, pl.BlockSpec((8, 128), lambda i: (0, 0)), vec),
        out_shape=(SDS((T_LOC, D_MODEL), F32), SDS((8, 128), F32), SDS((1, D_MODEL), F32)),
        compiler_params=_params(40, ("arbitrary",)),
    )(gated, wout, x2, tgt2, g_final)


def _branch_bwd(dh2, wout, gated, proj, a, kv, w_s, b_exp, g_v):
    tm = 256
    per_ex = SEQ // tm

    def body(dh2_ref, w_ref, g_ref, za_ref, ub_ref, vb_ref, zb_ref, qm_ref, zm_ref, a_ref, kv_ref, ws_ref,
             be_ref, gv_ref, da_ref, dr_ref, dkv_ref, dwo_ref, dws_ref, db_ref, dgv_ref, mix, dvn, dmsum, dwo_acc):
        i = pl.program_id(0)
        left = _left_lanes(tm)
        leftc = _left_lanes(CHUNK)

        @pl.when(i == 0)
        def _():
            dwo_acc[...] = jnp.zeros_like(dwo_acc)
            dws_ref[...] = jnp.zeros_like(dws_ref)
            dgv_ref[...] = jnp.zeros_like(dgv_ref)
            dmsum[...] = jnp.zeros_like(dmsum)

        @pl.when(i % per_ex == 0)
        def _():
            dkv_ref[...] = jnp.zeros_like(dkv_ref)

        dh2 = dh2_ref[...].astype(BF16)
        dwo_acc[...] += _dot_tn(g_ref[...], dh2)
        dg = _dot_nt(dh2, w_ref[...])

        sa, dsa = _silu_and_grad(za_ref[...])
        dga = dg[:, 0:ATTN_W]
        da_ref[...] = dga * sa
        dr_ref[:, 0:512] = (dga * a_ref[...] * dsa).astype(BF16)

        ub, vb = ub_ref[...], vb_ref[...]
        gu, dgu = _gelu_and_grad(ub)
        gv, dgv = _gelu_and_grad(vb)
        rv = _rstd(gv)
        gain = gv_ref[...]
        vn = (gv * rv * gain).astype(BF16)
        _sgu_mix(vn, ws_ref, mix, tm)
        mixed = mix[...] + be_ref[...]
        sb, dsb = _silu_and_grad(zb_ref[...])
        dgb = dg[:, ATTN_W:ATTN_W + SGU_W]
        dsg = dgb * sb
        dr_ref[:, 512:768] = (dsg * mixed * dgu).astype(BF16)
        dr_ref[:, 1024:1280] = (dgb * (gu * mixed) * dsb).astype(BF16)
        dmix = dsg * gu
        for ch in range(tm // CHUNK):
            rows = slice(ch * CHUNK, (ch + 1) * CHUNK)
            dmsum[...] += dmix[rows, :]
            for pr in range(2):
                cols = slice(pr * 128, (pr + 1) * 128)
                dmp, vp = dmix[rows, cols], vn[rows, cols]
                for h in range(2):
                    g = 2 * pr + h
                    dws_ref[g] += _dot_nt(jnp.where(leftc if h == 0 else ~leftc, dmp, 0.0), vp)
                dvn[rows, cols] = jnp.where(leftc, _dot_tn(ws_ref[2 * pr], dmp), _dot_tn(ws_ref[2 * pr + 1], dmp))
        dvn_v = dvn[...]
        dgv_ref[...] += jnp.sum(dvn_v * (gv * rv), axis=0, keepdims=True)
        dr_ref[:, 768:1024] = (_rms_bwd(gv, rv, gain, dvn_v) * dgv).astype(BF16)

        szm, dszm = _silu_and_grad(zm_ref[...])
        dgm = dg[:, ATTN_W + SGU_W:D_MODEL]
        dmo = dgm * szm
        for hp in range(2):
            cols = slice(hp * 128, (hp + 1) * 128)
            vcols = slice(MEM_W + hp * 128, MEM_W + (hp + 1) * 128)
            qp, kp, vp, dmop = qm_ref[:, cols], kv_ref[:, cols], kv_ref[:, vcols], dmo[:, cols]
            o, dq = [], []
            dk = jnp.zeros((N_MEM, 128), F32)
            dv = jnp.zeros((N_MEM, 128), F32)
            for h in range(2):
                p, qh = _mem_attn_head(qp, kp, h, left)
                dmoh = jnp.where(left if h == 0 else ~left, dmop, 0.0)
                o.append(_dot(p, vp))
                dp = _dot_nt(dmoh, vp)
                ds = p * (dp - jnp.sum(p * dp, axis=-1, keepdims=True)) * SCALE
                dq.append(_dot(ds, kp))
                dk += _dot_tn(ds, qh)
                dv += _dot_tn(p, dmoh)
            dr_ref[:, 1280 + hp * 128:1280 + (hp + 1) * 128] = jnp.where(left, dq[0], dq[1]).astype(BF16)
            dr_ref[:, 1536 + hp * 128:1536 + (hp + 1) * 128] = (
                dgm[:, cols] * jnp.where(left, o[0], o[1]) * dszm[:, cols]).astype(BF16)
            dkv_ref[:, cols] += dk
            dkv_ref[:, vcols] += dv

        @pl.when(i == pl.num_programs(0) - 1)
        def _():
            tot = dmsum[...]
            hi = tot.astype(BF16)
            lo = (tot - hi.astype(F32)).astype(BF16)
            grp = (lax.broadcasted_iota(jnp.int32, (SGU_W, 128), 0) // HEAD
                   == lax.broadcasted_iota(jnp.int32, (SGU_W, 128), 1)).astype(BF16)
            db_ref[...] = (_dot(hi, grp) + _dot(lo, grp)).T[0:4, :]
            _cast_rows(dwo_acc, dwo_ref, D_MODEL)

    full = lambda shape: pl.BlockSpec(shape, lambda i: (0,) * len(shape))
    row = lambda w: pl.BlockSpec((tm, w), lambda i: (i, 0))
    return pl.pallas_call(
        body, name="branch_bwd", grid=(T_LOC // tm,),
        in_specs=[row(D_MODEL), full((D_MODEL, D_MODEL)), row(D_MODEL)] + _branch_blocks(tm) + [
            row(ATTN_W), pl.BlockSpec((N_MEM, 2 * MEM_W), lambda i: (i // per_ex, 0)),
            full((4, CHUNK, CHUNK)), full((tm, SGU_W)), full((1, SGU_W))],
        out_specs=(row(ATTN_W), row(REST_W), pl.BlockSpec((N_MEM, 2 * MEM_W), lambda i: (i // per_ex, 0)),
                   full((D_MODEL, D_MODEL)), full((4, CHUNK, CHUNK)), full((4, CHUNK)), full((1, SGU_W))),
        out_shape=(SDS((T_LOC, ATTN_W), F32), SDS((T_LOC, REST_W), BF16), SDS((B_LOC * N_MEM, 2 * MEM_W), F32),
                   SDS((D_MODEL, D_MODEL), BF16), SDS((4, CHUNK, CHUNK), F32), SDS((4, CHUNK), F32), SDS((1, SGU_W), F32)),
        scratch_shapes=[pltpu.VMEM((tm, SGU_W), F32), pltpu.VMEM((tm, SGU_W), F32), pltpu.VMEM((CHUNK, SGU_W), F32),
                        pltpu.VMEM((D_MODEL, D_MODEL), F32)],
        compiler_params=_params(56, ("arbitrary",)),
    )(dh2, wout, gated, proj, proj, proj, proj, proj, proj, a, kv, w_s, b_exp, g_v)


def _attn_bwd(proj, slopes, da, a, lse):
    def body(sl_ref, q_ref, k_ref, v_ref, da_ref, a_ref, lse_ref, dq_ref, dk_ref, dv_ref, *scr):
        dq_s, dk_s, dv_s, delta_s, tab = scr[0:3], scr[3:6], scr[6:9], scr[9], scr[10]
        left = _left_lanes(CHUNK)
        _fill_bias_tables(sl_ref, tab)

        def prep(j, carry):
            rows = pl.ds(pl.multiple_of(j * 256, 256), 256)
            l256 = _left_lanes(256)
            prod = da_ref[rows, :] * a_ref[rows, :]
            d0 = jnp.sum(jnp.where(l256, prod, 0.0), axis=-1, keepdims=True)
            d1 = jnp.sum(jnp.where(l256, 0.0, prod), axis=-1, keepdims=True)
            delta_s[rows, :] = jnp.where(l256, d0, d1)
            zero = jnp.zeros((256, 128), F32)
            for cfg in range(3):
                dk_s[cfg][rows, :] = zero
                dv_s[cfg][rows, :] = zero
            return carry
        lax.fori_loop(0, SEQ // 256, prep, 0)

        def column(pair):
            return jnp.concatenate([pair[:, 0:1], pair[:, HEAD:HEAD + 1]], axis=0)

        def block(cfg, var, q0, k0, nk):
            d = DILATIONS[cfg][0]
            rq, rk = _rows(q0, CHUNK, d), _rows(k0, nk, d)
            qs = _stack_heads(q_ref[rq, :] * SCALE, left).astype(BF16)
            das = _stack_heads(da_ref[rq, :], left).astype(BF16)
            kw = k_ref[rk, :].astype(BF16)
            vw = v_ref[rk, :].astype(BF16)
            s = _dot_nt(qs, kw) + tab[3 * cfg + var, :, 0:nk]
            p = jnp.exp(s - column(lse_ref[rq, :]))
            ds = (p * (_dot_nt(das, vw) - column(delta_s[rq, :]))).astype(BF16)
            dq_s[cfg][rq, :] = _unstack_heads(_dot(ds, kw), left) * SCALE
            dk_s[cfg][rk, :] += _dot_tn(ds, qs)
            dv_s[cfg][rk, :] += _dot_tn(p, das)
        _attn_blocks(block, 2)

        def flush(j, carry):
            rows = pl.ds(pl.multiple_of(j * 256, 256), 256)
            for acc, dst in ((dq_s, dq_ref), (dk_s, dk_ref), (dv_s, dv_ref)):
                dst[rows, :] = (acc[0][rows, :] + acc[1][rows, :] + acc[2][rows, :]).astype(BF16)
            return carry
        lax.fori_loop(0, SEQ // 256, flush, 0)

    blk = lambda col0: pl.BlockSpec((SEQ, 128), lambda b, hp: (b, col0 // 128 + hp))
    own = pl.BlockSpec((SEQ, 128), lambda b, hp: (b, hp))
    return pl.pallas_call(
        body, name="attn_bwd", grid=(B_LOC, 4),
        in_specs=[pl.BlockSpec((1, 8, 128), lambda b, hp: (hp, 0, 0)), blk(C_QA), blk(C_KA), blk(C_VA), own, own, own],
        out_specs=(own, own, own),
        out_shape=(SDS((T_LOC, ATTN_W), BF16),) * 3,
        scratch_shapes=[pltpu.VMEM((SEQ, 128), F32)] * 10 + [pltpu.VMEM((N_BIAS, 2 * CHUNK, 2 * CHUNK), F32)],
        compiler_params=_params(40, ("arbitrary", "arbitrary")),
    )(slopes, proj, proj, proj, da, a, lse)


def _dproj_specs(tm):
    third = pl.BlockSpec((tm, ATTN_W), lambda i: (i, 0))
    return [third, third, third, pl.BlockSpec((tm, REST_W), lambda i: (i, 0))]


def _dx(dq, dk, dv, dr, wint, x2, dh2, g_norm):
    tm = 256

    def body(dq_ref, dk_ref, dv_ref, dr_ref, w_ref, x_ref, dh2_ref, g_ref, gx_ref, dgn_ref):
        @pl.when(pl.program_id(0) == 0)
        def _():
            dgn_ref[...] = jnp.zeros_like(dgn_ref)
        dh = (_dot(dq_ref[...], w_ref[C_QA:C_KA, :]) + _dot(dk_ref[...], w_ref[C_KA:C_VA, :])
              + _dot(dv_ref[...], w_ref[C_VA:C_ZA, :]) + _dot(dr_ref[...], w_ref[C_ZA:IN_COLS, :]))
        xv = x_ref[...]
        r = _rstd(xv)
        gx_ref[...] = dh2_ref[...] + _rms_bwd(xv, r, g_ref[...], dh)
        dgn_ref[...] += jnp.sum(dh * (xv * r), axis=0, keepdims=True)

    row = pl.BlockSpec((tm, D_MODEL), lambda i: (i, 0))
    vec = pl.BlockSpec((1, D_MODEL), lambda i: (0, 0))
    return pl.pallas_call(
        body, name="dx", grid=(T_LOC // tm,),
        in_specs=_dproj_specs(tm) + [pl.BlockSpec((IN_COLS, D_MODEL), lambda i: (0, 0)), row, row, vec],
        out_specs=(row, vec),
        out_shape=(SDS((T_LOC, D_MODEL), F32), SDS((1, D_MODEL), F32)),
        compiler_params=_params(48, ("arbitrary",)),
    )(dq, dk, dv, dr, wint, x2, dh2, g_norm)


def _dwin(dq, dk, dv, dr, x2, g_norm):
    tm = 512

    def body(dq_ref, dk_ref, dv_ref, dr_ref, x_ref, g_ref, o_ref, acc):
        @pl.when(pl.program_id(0) == 0)
        def _():
            acc[...] = jnp.zeros_like(acc)
        xv = x_ref[...]
        h = (xv * _rstd(xv) * g_ref[...]).astype(BF16)
        acc[C_QA:C_KA, :] += _dot_tn(dq_ref[...], h)
        acc[C_KA:C_VA, :] += _dot_tn(dk_ref[...], h)
        acc[C_VA:C_ZA, :] += _dot_tn(dv_ref[...], h)
        acc[C_ZA:IN_COLS, :] += _dot_tn(dr_ref[...], h)

        @pl.when(pl.program_id(0) == pl.num_programs(0) - 1)
        def _():
            _cast_rows(acc, o_ref, IN_COLS)

    return pl.pallas_call(
        body, name="dwin", grid=(T_LOC // tm,),
        in_specs=_dproj_specs(tm) + [pl.BlockSpec((tm, D_MODEL), lambda i: (i, 0)), pl.BlockSpec((1, D_MODEL), lambda i: (0, 0))],
        out_specs=pl.BlockSpec((IN_COLS, D_MODEL), lambda i: (0, 0)),
        out_shape=SDS((IN_COLS, D_MODEL), BF16),
        scratch_shapes=[pltpu.VMEM((IN_COLS, D_MODEL), F32)],
        compiler_params=_params(56, ("arbitrary",)),
    )(dq, dk, dv, dr, x2, g_norm)


def _memkv_bwd(dkv, mem2, g_mem, wkv):
    def body(dkv_ref, m_ref, g_ref, w_ref, dw_ref, dg_ref):
        mv = m_ref[...]
        r = _rstd(mv)
        dkv_v = dkv_ref[...].astype(BF16)
        dw_ref[...] = _dot_tn(mv * r * g_ref[...], dkv_v).astype(BF16)
        dg_ref[...] = jnp.sum(_dot_nt(dkv_v, w_ref[...]) * (mv * r), axis=0, keepdims=True)

    return pl.pallas_call(
        body, name="memkv_bwd", out_shape=(SDS((D_MODEL, 2 * MEM_W), BF16), SDS((1, D_MODEL), F32)),
        compiler_params=_params(32),
    )(dkv, mem2, g_mem, wkv)


def _reduce_scatter(dwint, dwkv, dwout):
    shards = (W_IN_SHARD, ROW_SHARD, ROW_SHARD)
    widths = (D_MODEL, 2 * MEM_W, D_MODEL)
    step = 32

    def body(a_ref, b_ref, c_ref, ga_ref, gb_ref, gc_ref, *scr):
        from_sib, pair_sum, from_chip = scr[0:3], scr[3:6], scr[6:9]
        send_sems, recv_sems = scr[9], scr[10]
        x, y, c = _mesh_pos()
        srcs, outs = (a_ref, b_ref, c_ref), (ga_ref, gb_ref, gc_ref)
        others = [(1 - x, y), (x, 1 - y), (1 - x, 1 - y)]

        def piece(a, p):
            n = shards[a]
            return srcs[a].at[pl.ds(pl.multiple_of(_flat(p) * n, 16), n), :]

        def to_sibling(a, q):
            chip = (q // 2, q % 2)
            return pltpu.make_async_remote_copy(
                src_ref=piece(a, (*chip, 1 - c)), dst_ref=from_sib[a].at[q],
                send_sem=send_sems.at[4 * a + q], recv_sem=recv_sems.at[4 * a + q],
                device_id=(x, y, 1 - c), device_id_type=MESH)

        def to_chip(a, j):
            return pltpu.make_async_remote_copy(
                src_ref=pair_sum[a].at[j], dst_ref=from_chip[a].at[j],
                send_sem=send_sems.at[12 + 3 * a + j], recv_sem=recv_sems.at[12 + 3 * a + j],
                device_id=(*others[j], c), device_id_type=MESH)

        def add_rows(a, terms, store):
            def one(i, carry):
                rows = pl.ds(pl.multiple_of(i * step, step), step)
                tot = terms[0][rows, :].astype(F32)
                for t in terms[1:]:
                    tot = tot + t[rows, :].astype(F32)
                store(rows, tot)
                return carry
            lax.fori_loop(0, shards[a] // step, one, 0)

        first = [to_sibling(a, q) for a in range(3) for q in range(4)]
        for cp in first:
            cp.start()
        second = []
        for a in range(3):
            for q in range(4):
                to_sibling(a, q).wait_recv()
            for j, chip in enumerate(others):
                def store(rows, tot, a=a, j=j):
                    pair_sum[a][j, rows, :] = tot.astype(BF16)
                add_rows(a, [piece(a, (*chip, c)), from_sib[a].at[2 * chip[0] + chip[1]]], store)
                cp = to_chip(a, j)
                cp.start()
                second.append(cp)
        for a in range(3):
            for j in range(3):
                to_chip(a, j).wait_recv()

            def store(rows, tot, a=a):
                outs[a][rows, :] = tot
            add_rows(a, [piece(a, (x, y, c)), from_sib[a].at[2 * x + y]] + [from_chip[a].at[j] for j in range(3)], store)
        for cp in first + second:
            cp.wait_send()

    vmem = pl.BlockSpec(memory_space=pltpu.VMEM)
    return pl.pallas_call(
        body, name="reduce_scatter",
        out_shape=tuple(SDS((n, w), F32) for n, w in zip(shards, widths)),
        in_specs=[vmem, vmem, vmem], out_specs=(vmem, vmem, vmem),
        scratch_shapes=[pltpu.VMEM((4, n, w), BF16) for n, w in zip(shards, widths)]
        + [pltpu.VMEM((3, n, w), BF16) for n, w in zip(shards, widths)] * 2
        + [pltpu.SemaphoreType.DMA((21,)), pltpu.SemaphoreType.DMA((21,))],
        compiler_params=_params(40),
    )(dwint, dwkv, dwout)


def _allreduce_small(parts):
    n = len(parts)

    def body(*refs):
        ins, outs, bufs = refs[0:n], refs[n:2 * n], refs[2 * n:3 * n]
        send_sems, recv_sems = refs[3 * n], refs[3 * n + 1]
        pos = _mesh_pos()
        me = _flat(pos)
        for a in range(n):
            bufs[a][me] = ins[a][...]

        def copy(a, k, slot):
            return pltpu.make_async_remote_copy(
                src_ref=ins[a], dst_ref=bufs[a].at[slot],
                send_sem=send_sems.at[7 * a + k - 1], recv_sem=recv_sems.at[7 * a + k - 1],
                device_id=_peer(pos, k), device_id_type=MESH)

        sent = [copy(a, k, me) for a in range(n) for k in range(1, N_DEV)]
        for cp in sent:
            cp.start()
        for a in range(n):
            for k in range(1, N_DEV):
                copy(a, k, _flat(_peer(pos, k))).wait_recv()
        for cp in sent:
            cp.wait_send()
        for a in range(n):
            acc = bufs[a][0]
            for s in range(1, N_DEV):
                acc = acc + bufs[a][s]
            outs[a][...] = acc

    vmem = pl.BlockSpec(memory_space=pltpu.VMEM)
    return pl.pallas_call(
        body, name="allreduce_small",
        out_shape=tuple(SDS(p.shape, F32) for p in parts),
        in_specs=[vmem] * n, out_specs=(vmem,) * n,
        scratch_shapes=[pltpu.VMEM((N_DEV,) + p.shape, F32) for p in parts]
        + [pltpu.SemaphoreType.DMA((7 * n,)), pltpu.SemaphoreType.DMA((7 * n,))],
        compiler_params=_params(16),
    )(*parts)


def _adamw(w, g, m, v):
    m = ADAM_B1 * m + (1.0 - ADAM_B1) * g
    v = ADAM_B2 * v + (1.0 - ADAM_B2) * (g * g)
    m_hat = m / (1.0 - ADAM_B1 ** ADAM_STEP)
    v_hat = v / (1.0 - ADAM_B2 ** ADAM_STEP)
    return -ADAM_LR * (m_hat / (jnp.sqrt(v_hat) + ADAM_EPS) + ADAM_WD * w), m, v


def _adam_shard(grad, w, m, v, transposed):
    rows, cols = grad.shape

    def body(g_ref, w_ref, m_ref, v_ref, g_o, d_o, m_o, v_o, *scr):
        if transposed:
            pad, = scr
            pad[...] = jnp.zeros_like(pad)
            pad[0:rows, :] = g_ref[...]
            g = pad[...].T[:, 0:rows]
        else:
            g = g_ref[...]
        g_o[...] = g
        d_o[...], m_o[...], v_o[...] = _adamw(w_ref[...], g, m_ref[...], v_ref[...])

    return pl.pallas_call(
        body, name="adam_" + ("w_in" if transposed else "rows%d" % cols),
        out_shape=(SDS(w.shape, F32),) * 4,
        scratch_shapes=[pltpu.VMEM((512, cols), F32)] if transposed else [],
        compiler_params=_params(40),
    )(grad, w, m, v)


def _adam_small(ws, gs, ms, vs):
    n = len(ws)

    def body(*refs):
        w_r, g_r, m_r, v_r = refs[0:n], refs[n:2 * n], refs[2 * n:3 * n], refs[3 * n:4 * n]
        outs = refs[4 * n:]
        for a in range(n):
            outs[3 * a][...], outs[3 * a + 1][...], outs[3 * a + 2][...] = _adamw(
                w_r[a][...], g_r[a][...], m_r[a][...], v_r[a][...])

    return pl.pallas_call(
        body, name="adam_small",
        out_shape=tuple(SDS(w.shape, F32) for w in ws for _ in range(3)),
        compiler_params=_params(16),
    )(*ws, *gs, *ms, *vs)


def kernel(x, mem, g_norm, w_in, w_sgu_spatial, b_sgu_spatial, g_sgu_v, g_mem, w_mem_kv, w_out, g_final, loss_target, m_g_norm, m_w_in, m_w_sgu_spatial, m_b_sgu_spatial, m_g_sgu_v, m_g_mem, m_w_mem_kv, m_w_out, m_g_final, v_g_norm, v_w_in, v_w_sgu_spatial, v_b_sgu_spatial, v_g_sgu_v, v_g_mem, v_w_mem_kv, v_w_out, v_g_final):
    x2 = x.reshape(T_LOC, D_MODEL)
    tgt2 = loss_target.reshape(T_LOC, D_MODEL)
    mem2 = mem.reshape(B_LOC * N_MEM, D_MODEL)
    w_s = w_sgu_spatial[0]
    b_exp = jnp.tile(jnp.repeat(b_sgu_spatial[0].T, HEAD, axis=1), (2, 1))
    slope = jnp.power(2.0, -8.0 * (jnp.arange(8, dtype=F32) + 1.0) / 8)
    slopes = jnp.broadcast_to(jnp.repeat(slope.reshape(4, 2), HEAD, axis=1)[:, None, :], (4, 8, 128))

    wint, wkv, wout = _allgather_weights(w_in[0], w_mem_kv[0], w_out[0])
    proj = _proj_fwd(x2, g_norm, wint)
    kv = _memkv_fwd(mem2, g_mem, wkv)
    a, lse = _attn_fwd(proj, slopes)
    gated = _branch_fwd(proj, a, kv, w_s, b_exp, g_sgu_v)
    dh2, loss8, dgf = _outproj_loss(gated, wout, x2, tgt2, g_final.reshape(1, D_MODEL))

    da, dr, dkv, dwout, dws, dbs, dgv = _branch_bwd(dh2, wout, gated, proj, a, kv, w_s, b_exp, g_sgu_v)
    dq, dk, dv = _attn_bwd(proj, slopes, da, a, lse)
    grad_x, dgn = _dx(dq, dk, dv, dr, wint, x2, dh2, g_norm)
    dwint = _dwin(dq, dk, dv, dr, x2, g_norm)
    dwkv, dgm = _memkv_bwd(dkv, mem2, g_mem, wkv)

    r_in, r_kv, r_out = _reduce_scatter(dwint, dwkv, dwout)
    g_small = _allreduce_small([dgn, dws.reshape(4 * CHUNK, CHUNK), dbs, dgv, dgm, dgf])

    g_win, d_win, m_win, v_win = _adam_shard(r_in, w_in[0], m_w_in[0], v_w_in[0], True)
    g_wkv, d_wkv, m_wkv, v_wkv = _adam_shard(r_kv, w_mem_kv[0], m_w_mem_kv[0], v_w_mem_kv[0], False)
    g_wout, d_wout, m_wout, v_wout = _adam_shard(r_out, w_out[0], m_w_out[0], v_w_out[0], False)

    small_shapes = [(1, D_MODEL), (4 * CHUNK, CHUNK), (4, CHUNK), (1, SGU_W), (1, D_MODEL), (1, D_MODEL)]
    pack = lambda arrs: [t.reshape(s) for t, s in zip(arrs, small_shapes)]
    upd = _adam_small(
        pack([g_norm, w_sgu_spatial, b_sgu_spatial, g_sgu_v, g_mem, g_final]), g_small,
        pack([m_g_norm, m_w_sgu_spatial, m_b_sgu_spatial, m_g_sgu_v, m_g_mem, m_g_final]),
        pack([v_g_norm, v_w_sgu_spatial, v_b_sgu_spatial, v_g_sgu_v, v_g_mem, v_g_final]))
    out_shapes = [g_norm.shape, w_sgu_spatial.shape, b_sgu_spatial.shape, g_sgu_v.shape, g_mem.shape, g_final.shape]
    unpack = lambda arrs: [t.reshape(s) for t, s in zip(arrs, out_shapes)]
    gs = unpack(g_small)
    ds, nms, nvs = unpack(upd[0::3]), unpack(upd[1::3]), unpack(upd[2::3])

    loss = lax.psum(loss8[0, 0], ("x", "y", "c"))

    def assemble(small, win, wkv_, wout_):
        return [small[0], win[None], small[1], small[2], small[3], small[4], wkv_[None], wout_[None], small[5]]

    return (loss, grad_x.reshape(x.shape),
            *assemble(gs, g_win, g_wkv, g_wout), *assemble(ds, d_win, d_wkv, d_wout),
            *assemble(nms, m_win, m_wkv, m_wout), *assemble(nvs, v_win, v_wkv, v_wout))
```

```python
import functools

import jax
import jax.numpy as jnp
from jax import lax
from jax.experimental import pallas as pl
from jax.experimental.pallas import tpu as pltpu

F32 = jnp.float32
BF16 = jnp.bfloat16
SDS = jax.ShapeDtypeStruct
MESH = pl.DeviceIdType.MESH

N_DEV = 8
D_MODEL = 1024
SEQ = 2048
B_LOC = 2
T_LOC = B_LOC * SEQ
N_MEM = 256
HEAD = 64
ATTN_W = 512
SGU_W = 256
MEM_W = 256
IN_COLS = 3328
W_IN_SHARD = IN_COLS // N_DEV
ROW_SHARD = D_MODEL // N_DEV
CHUNK = 128
DILATIONS = ((1, 2048), (4, 512), (16, 128))
RADIUS = 64
EPS = 1e-6
NEG = -1e30
SCALE = HEAD ** -0.5
C_QA, C_KA, C_VA, C_ZA, C_UB, C_VB, C_ZB, C_QM, C_ZM = 0, 512, 1024, 1536, 2048, 2304, 2560, 2816, 3072
QKV_W = 1536
REST_W = IN_COLS - QKV_W

ADAM_LR, ADAM_B1, ADAM_B2, ADAM_EPS, ADAM_WD, ADAM_STEP = 0.001, 0.9, 0.999, 1e-08, 0.01, 10

V7X_VMEM_MIB = 64


def _params(vmem_mib, sem=None, **kw):
    assert vmem_mib < V7X_VMEM_MIB
    return pltpu.CompilerParams(vmem_limit_bytes=vmem_mib << 20, dimension_semantics=sem, **kw)


def _dot(a, b):
    return jnp.dot(a.astype(BF16), b.astype(BF16), preferred_element_type=F32)


def _dot_nt(a, b):
    return lax.dot_general(a.astype(BF16), b.astype(BF16), (((1,), (1,)), ((), ())), preferred_element_type=F32)


def _dot_tn(a, b):
    return lax.dot_general(a.astype(BF16), b.astype(BF16), (((0,), (0,)), ((), ())), preferred_element_type=F32)


def _rstd(v):
    return lax.rsqrt(jnp.mean(v * v, axis=-1, keepdims=True) + EPS)


def _rms_bwd(v, r, g, dy):
    gdy = g * dy
    return r * gdy - v * (r * r * r * jnp.mean(gdy * v, axis=-1, keepdims=True))


def _sigmoid(z):
    return 1.0 / (1.0 + jnp.exp(-z))


def _silu_and_grad(z):
    s = _sigmoid(z)
    return z * s, s * (1.0 + z * (1.0 - s))


_G_C = 0.7978845608028654
_G_K = 0.044715


def _gelu_and_grad(v):
    t = jnp.tanh(_G_C * (v + _G_K * (v * v * v)))
    cdf = 0.5 * (1.0 + t)
    return v * cdf, cdf + 0.5 * v * (1.0 - t * t) * (_G_C * (1.0 + 3.0 * _G_K * v * v))


def _cast_rows(src_ref, dst_ref, rows, step=256):
    def one(i, carry):
        r = pl.ds(pl.multiple_of(i * step, step), step)
        dst_ref[r, :] = src_ref[r, :].astype(dst_ref.dtype)
        return carry
    lax.fori_loop(0, rows // step, one, 0)


def _left_lanes(rows):
    return lax.broadcasted_iota(jnp.int32, (rows, 128), 1) < HEAD


def _mesh_pos():
    return lax.axis_index("x"), lax.axis_index("y"), lax.axis_index("c")


def _peer(pos, k):
    x, y, c = pos
    return (1 - x if k & 4 else x, 1 - y if k & 2 else y, 1 - c if k & 1 else c)


def _flat(pos):
    return 4 * pos[0] + 2 * pos[1] + pos[2]


def _allgather_weights(w_in_t, w_kv, w_out):
    shards = (W_IN_SHARD, ROW_SHARD, ROW_SHARD)

    def body(win_ref, wkv_ref, wout_ref, wint_o, wkv_o, wout_o, send_sems, recv_sems):
        x, y, c = _mesh_pos()
        me, sib = (x, y, c), (x, y, 1 - c)
        chips = [(1 - x, y), (x, 1 - y), (1 - x, 1 - y)]
        outs = (wint_o, wkv_o, wout_o)

        def rows(a, p):
            n = shards[a]
            return outs[a].at[pl.ds(pl.multiple_of(_flat(p) * n, 16), n), :]

        rows(0, me)[...] = win_ref[...].astype(BF16)
        rows(1, me)[...] = wkv_ref[...].astype(BF16)
        rows(2, me)[...] = wout_ref[...].astype(BF16)

        def copy(a, k, block, to):
            return pltpu.make_async_remote_copy(
                src_ref=rows(a, block), dst_ref=rows(a, block),
                send_sem=send_sems.at[7 * a + k], recv_sem=recv_sems.at[7 * a + k],
                device_id=to, device_id_type=MESH)

        first = []
        for a in range(3):
            first.append(copy(a, 0, me, sib))
            first += [copy(a, 1 + j, me, (*chip, c)) for j, chip in enumerate(chips)]
        for cp in first:
            cp.start()
        passed = []
        for a in range(3):
            for j, chip in enumerate(chips):
                copy(a, 1 + j, (*chip, c), me).wait_recv()
                fwd = copy(a, 4 + j, (*chip, c), sib)
                fwd.start()
                passed.append(fwd)
        for a in range(3):
            copy(a, 0, sib, me).wait_recv()
            for j, chip in enumerate(chips):
                copy(a, 4 + j, (*chip, 1 - c), me).wait_recv()
        for cp in first + passed:
            cp.wait_send()

    vmem = pl.BlockSpec(memory_space=pltpu.VMEM)
    return pl.pallas_call(
        body, name="allgather_weights",
        out_shape=(SDS((IN_COLS, D_MODEL), BF16), SDS((D_MODEL, 2 * MEM_W), BF16), SDS((D_MODEL, D_MODEL), BF16)),
        in_specs=[vmem, vmem, vmem], out_specs=(vmem, vmem, vmem),
        scratch_shapes=[pltpu.SemaphoreType.DMA((21,)), pltpu.SemaphoreType.DMA((21,))],
        compiler_params=_params(40),
    )(w_in_t, w_kv, w_out)


def _proj_fwd(x2, g_norm, wint):
    tm = 256

    def body(x_ref, g_ref, w_ref, o_ref):
        xv = x_ref[...]
        h = xv * _rstd(xv) * g_ref[...]
        o_ref[...] = _dot_nt(h, w_ref[...])

    return pl.pallas_call(
        body, name="proj_fwd", grid=(T_LOC // tm,),
        in_specs=[pl.BlockSpec((tm, D_MODEL), lambda i: (i, 0)), pl.BlockSpec((1, D_MODEL), lambda i: (0, 0)),
                  pl.BlockSpec((IN_COLS, D_MODEL), lambda i: (0, 0))],
        out_specs=pl.BlockSpec((tm, IN_COLS), lambda i: (i, 0)),
        out_shape=SDS((T_LOC, IN_COLS), F32),
        compiler_params=_params(48, ("arbitrary",)),
    )(x2, g_norm, wint)


def _memkv_fwd(mem2, g_mem, wkv):
    def body(m_ref, g_ref, w_ref, o_ref):
        mv = m_ref[...]
        o_ref[...] = _dot(mv * _rstd(mv) * g_ref[...], w_ref[...])

    return pl.pallas_call(
        body, name="memkv_fwd", out_shape=SDS((B_LOC * N_MEM, 2 * MEM_W), F32), compiler_params=_params(32),
    )(mem2, g_mem, wkv)


N_BIAS = 7


def _fill_bias_tables(sl_ref, tab):
    for cfg, (d, length) in enumerate(DILATIONS):
        nk = min(length, 2 * CHUNK)
        r = lax.broadcasted_iota(jnp.int32, (CHUNK, nk), 0)
        c = lax.broadcasted_iota(jnp.int32, (CHUNK, nk), 1)
        for var in range(3 if length > nk else 1):
            rel = jnp.abs(r - c + var * RADIUS)
            dist = rel.astype(F32) * float(d)
            for h in range(2):
                slope = sl_ref[0, 0:1, h * HEAD:h * HEAD + 1]
                tab[3 * cfg + var, h * CHUNK:(h + 1) * CHUNK, 0:nk] = jnp.where(rel <= RADIUS, -slope * dist, NEG)


def _attn_blocks(visit, unroll):
    def step(t, carry):
        for cfg, (d, length) in enumerate(DILATIONS):
            nblk = length // CHUNK
            if nblk == 1:
                visit(cfg, 0, t, t, length)
                continue
            rho, i = (0, t) if d == 1 else (t // nblk, t % nblk)
            ks = jnp.clip(i * CHUNK - RADIUS, 0, length - 2 * CHUNK)
            visit(cfg, (i * CHUNK - ks) // RADIUS, rho + d * (i * CHUNK), rho + d * ks, 2 * CHUNK)
        return carry
    lax.fori_loop(0, 16, step, 0, unroll=unroll)


def _stack_heads(v, left):
    return jnp.concatenate([jnp.where(left, v, 0.0), jnp.where(left, 0.0, v)], axis=0)


def _unstack_heads(v, left):
    return jnp.where(left, v[0:CHUNK], v[CHUNK:2 * CHUNK])


def _rows(start, n, d):
    return pl.ds(start, n) if d == 1 else pl.ds(start, n, stride=d)


def _attn_fwd(proj, slopes):
    def body(sl_ref, q_ref, k_ref, v_ref, a_ref, lse_ref, *scr):
        o_c, m_c, l_c, tab = scr[0:3], scr[3:6], scr[6:9], scr[9]
        left = _left_lanes(CHUNK)
        _fill_bias_tables(sl_ref, tab)

        def block(cfg, var, q0, k0, nk):
            d = DILATIONS[cfg][0]
            rq, rk = _rows(q0, CHUNK, d), _rows(k0, nk, d)
            qs = _stack_heads(q_ref[rq, :] * SCALE, left)
            s = _dot_nt(qs, k_ref[rk, :]) + tab[3 * cfg + var, :, 0:nk]
            m = jnp.max(s, axis=-1, keepdims=True)
            p = jnp.exp(s - m)
            o_c[cfg][rq, :] = _unstack_heads(_dot(p, v_ref[rk, :]), left)
            m_c[cfg][rq, :] = _unstack_heads(m, left)
            l_c[cfg][rq, :] = _unstack_heads(jnp.sum(p, axis=-1, keepdims=True), left)
        _attn_blocks(block, 4)

        def merge(j, carry):
            rows = pl.ds(pl.multiple_of(j * 256, 256), 256)
            ms = [m_c[i][rows, :] for i in range(3)]
            top = jnp.maximum(jnp.maximum(ms[0], ms[1]), ms[2])
            ws = [jnp.exp(m - top) for m in ms]
            den = l_c[0][rows, :] * ws[0] + l_c[1][rows, :] * ws[1] + l_c[2][rows, :] * ws[2]
            num = o_c[0][rows, :] * ws[0] + o_c[1][rows, :] * ws[1] + o_c[2][rows, :] * ws[2]
            a_ref[rows, :] = num / den
            lse_ref[rows, :] = top + jnp.log(den)
            return carry
        lax.fori_loop(0, SEQ // 256, merge, 0)

    blk = lambda col0: pl.BlockSpec((SEQ, 128), lambda b, hp: (b, col0 // 128 + hp))
    out = pl.BlockSpec((SEQ, 128), lambda b, hp: (b, hp))
    return pl.pallas_call(
        body, name="attn_fwd", grid=(B_LOC, 4),
        in_specs=[pl.BlockSpec((1, 8, 128), lambda b, hp: (hp, 0, 0)), blk(C_QA), blk(C_KA), blk(C_VA)],
        out_specs=(out, out),
        out_shape=(SDS((T_LOC, ATTN_W), F32), SDS((T_LOC, ATTN_W), F32)),
        scratch_shapes=[pltpu.VMEM((SEQ, 128), F32)] * 9 + [pltpu.VMEM((N_BIAS, 2 * CHUNK, 2 * CHUNK), F32)],
        compiler_params=_params(40, ("arbitrary", "arbitrary")),
    )(slopes, proj, proj, proj)


def _sgu_mix(vn, ws_ref, dst_ref, tm):
    left = _left_lanes(CHUNK)
    for ch in range(tm // CHUNK):
        for pr in range(2):
            vp = vn[ch * CHUNK:(ch + 1) * CHUNK, pr * 128:(pr + 1) * 128]
            dst_ref[ch * CHUNK:(ch + 1) * CHUNK, pr * 128:(pr + 1) * 128] = jnp.where(
                left, _dot(ws_ref[2 * pr], vp), _dot(ws_ref[2 * pr + 1], vp))


def _mem_attn_head(qp, kp, h, left):
    qh = jnp.where(left if h == 0 else ~left, qp, 0.0)
    s = _dot_nt(qh, kp) * SCALE
    e = jnp.exp(s - jnp.max(s, axis=-1, keepdims=True))
    return e * (1.0 / jnp.sum(e, axis=-1, keepdims=True)), qh


def _branch_blocks(tm):
    col = lambda w, c0: pl.BlockSpec((tm, w), lambda i: (i, c0 // w))
    return [col(512, C_ZA), col(256, C_UB), col(256, C_VB), col(256, C_ZB), col(256, C_QM), col(256, C_ZM)]


def _branch_fwd(proj, a, kv, w_s, b_exp, g_v):
    tm = 256
    per_ex = SEQ // tm

    def body(za_ref, ub_ref, vb_ref, zb_ref, qm_ref, zm_ref, a_ref, kv_ref, ws_ref, be_ref, gv_ref, o_ref, mix):
        left = _left_lanes(tm)
        o_ref[:, 0:ATTN_W] = (_silu_and_grad(za_ref[...])[0] * a_ref[...]).astype(BF16)
        gu = _gelu_and_grad(ub_ref[...])[0]
        gv = _gelu_and_grad(vb_ref[...])[0]
        vn = gv * _rstd(gv) * gv_ref[...]
        _sgu_mix(vn.astype(BF16), ws_ref, mix, tm)
        sg = gu * (mix[...] + be_ref[...])
        o_ref[:, ATTN_W:ATTN_W + SGU_W] = (_silu_and_grad(zb_ref[...])[0] * sg).astype(BF16)
        szm = _silu_and_grad(zm_ref[...])[0]
        for hp in range(2):
            cols = slice(hp * 128, (hp + 1) * 128)
            qp, kp, vp = qm_ref[:, cols], kv_ref[:, cols], kv_ref[:, MEM_W + hp * 128:MEM_W + (hp + 1) * 128]
            o = [_dot(_mem_attn_head(qp, kp, h, left)[0], vp) for h in range(2)]
            c0 = ATTN_W + SGU_W + hp * 128
            o_ref[:, c0:c0 + 128] = (szm[:, cols] * jnp.where(left, o[0], o[1])).astype(BF16)

    full = lambda shape: pl.BlockSpec(shape, lambda i: (0,) * len(shape))
    return pl.pallas_call(
        body, name="branch_fwd", grid=(T_LOC // tm,),
        in_specs=_branch_blocks(tm) + [
            pl.BlockSpec((tm, ATTN_W), lambda i: (i, 0)), pl.BlockSpec((N_MEM, 2 * MEM_W), lambda i: (i // per_ex, 0)),
            full((4, CHUNK, CHUNK)), full((tm, SGU_W)), full((1, SGU_W))],
        out_specs=pl.BlockSpec((tm, D_MODEL), lambda i: (i, 0)),
        out_shape=SDS((T_LOC, D_MODEL), BF16),
        scratch_shapes=[pltpu.VMEM((tm, SGU_W), F32)],
        compiler_params=_params(40, ("arbitrary",)),
    )(proj, proj, proj, proj, proj, proj, a, kv, w_s, b_exp, g_v)


def _outproj_loss(gated, wout, x2, tgt2, g_final):
    tm = 512

    def body(g_ref, w_ref, x_ref, t_ref, gf_ref, dh2_ref, loss_ref, dgf_ref):
        @pl.when(pl.program_id(0) == 0)
        def _():
            loss_ref[...] = jnp.zeros_like(loss_ref)
            dgf_ref[...] = jnp.zeros_like(dgf_ref)
        h2 = x_ref[...] + _dot(g_ref[...], w_ref[...])
        r = _rstd(h2)
        gf = gf_ref[...]
        err = h2 * r * gf - t_ref[...]
        loss_ref[...] += 0.5 * jnp.sum(jnp.mean(err * err, axis=-1, keepdims=True))
        dy = err * (1.0 / D_MODEL)
        dh2_ref[...] = _rms_bwd(h2, r, gf, dy)
        dgf_ref[...] += jnp.sum(dy * (h2 * r), axis=0, keepdims=True)

    row = pl.BlockSpec((tm, D_MODEL), lambda i: (i, 0))
    vec = pl.BlockSpec((1, D_MODEL), lambda i: (0, 0))
    return pl.pallas_call(
        body, name="outproj_loss", grid=(T_LOC // tm,),
        in_specs=[row, pl.BlockSpec((D_MODEL, D_MODEL), lambda i: (0, 0)), row, row, vec],
        out_specs=(row, pl.BlockSpec((8, 128), lambda i: (0, 0)), vec),
        out_shape=(SDS((T_LOC, D_MODEL), F32), SDS((8, 128), F32), SDS((1, D_MODEL), F32)),
        compiler_params=_params(40, ("arbitrary",)),
    )(gated, wout, x2, tgt2, g_final)


def _branch_bwd(dh2, wout, gated, proj, a, kv, w_s, b_exp, g_v):
    tm = 256
    per_ex = SEQ // tm

    def body(dh2_ref, w_ref, g_ref, za_ref, ub_ref, vb_ref, zb_ref, qm_ref, zm_ref, a_ref, kv_ref, ws_ref,
             be_ref, gv_ref, da_ref, dr_ref, dkv_ref, dwo_ref, dws_ref, db_ref, dgv_ref, mix, dvn, dmsum, dwo_acc):
        i = pl.program_id(0)
        left = _left_lanes(tm)
        leftc = _left_lanes(CHUNK)

        @pl.when(i == 0)
        def _():
            dwo_acc[...] = jnp.zeros_like(dwo_acc)
            dws_ref[...] = jnp.zeros_like(dws_ref)
            dgv_ref[...] = jnp.zeros_like(dgv_ref)
            dmsum[...] = jnp.zeros_like(dmsum)

        @pl.when(i % per_ex == 0)
        def _():
            dkv_ref[...] = jnp.zeros_like(dkv_ref)

        dh2 = dh2_ref[...].astype(BF16)
        dwo_acc[...] += _dot_tn(g_ref[...], dh2)
        dg = _dot_nt(dh2, w_ref[...])

        sa, dsa = _silu_and_grad(za_ref[...])
        dga = dg[:, 0:ATTN_W]
        da_ref[...] = dga * sa
        dr_ref[:, 0:512] = (dga * a_ref[...] * dsa).astype(BF16)

        ub, vb = ub_ref[...], vb_ref[...]
        gu, dgu = _gelu_and_grad(ub)
        gv, dgv = _gelu_and_grad(vb)
        rv = _rstd(gv)
        gain = gv_ref[...]
        vn = (gv * rv * gain).astype(BF16)
        _sgu_mix(vn, ws_ref, mix, tm)
        mixed = mix[...] + be_ref[...]
        sb, dsb = _silu_and_grad(zb_ref[...])
        dgb = dg[:, ATTN_W:ATTN_W + SGU_W]
        dsg = dgb * sb
        dr_ref[:, 512:768] = (dsg * mixed * dgu).astype(BF16)
        dr_ref[:, 1024:1280] = (dgb * (gu * mixed) * dsb).astype(BF16)
        dmix = dsg * gu
        for ch in range(tm // CHUNK):
            rows = slice(ch * CHUNK, (ch + 1) * CHUNK)
            dmsum[...] += dmix[rows, :]
            for pr in range(2):
                cols = slice(pr * 128, (pr + 1) * 128)
                dmp, vp = dmix[rows, cols], vn[rows, cols]
                for h in range(2):
                    g = 2 * pr + h
                    dws_ref[g] += _dot_nt(jnp.where(leftc if h == 0 else ~leftc, dmp, 0.0), vp)
                dvn[rows, cols] = jnp.where(leftc, _dot_tn(ws_ref[2 * pr], dmp), _dot_tn(ws_ref[2 * pr + 1], dmp))
        dvn_v = dvn[...]
        dgv_ref[...] += jnp.sum(dvn_v * (gv * rv), axis=0, keepdims=True)
        dr_ref[:, 768:1024] = (_rms_bwd(gv, rv, gain, dvn_v) * dgv).astype(BF16)

        szm, dszm = _silu_and_grad(zm_ref[...])
        dgm = dg[:, ATTN_W + SGU_W:D_MODEL]
        dmo = dgm * szm
        for hp in range(2):
            cols = slice(hp * 128, (hp + 1) * 128)
            vcols = slice(MEM_W + hp * 128, MEM_W + (hp + 1) * 128)
            qp, kp, vp, dmop = qm_ref[:, cols], kv_ref[:, cols], kv_ref[:, vcols], dmo[:, cols]
            o, dq = [], []
            dk = jnp.zeros((N_MEM, 128), F32)
            dv = jnp.zeros((N_MEM, 128), F32)
            for h in range(2):
                p, qh = _mem_attn_head(qp, kp, h, left)
                dmoh = jnp.where(left if h == 0 else ~left, dmop, 0.0)
                o.append(_dot(p, vp))
                dp = _dot_nt(dmoh, vp)
                ds = p * (dp - jnp.sum(p * dp, axis=-1, keepdims=True)) * SCALE
                dq.append(_dot(ds, kp))
                dk += _dot_tn(ds, qh)
                dv += _dot_tn(p, dmoh)
            dr_ref[:, 1280 + hp * 128:1280 + (hp + 1) * 128] = jnp.where(left, dq[0], dq[1]).astype(BF16)
            dr_ref[:, 1536 + hp * 128:1536 + (hp + 1) * 128] = (
                dgm[:, cols] * jnp.where(left, o[0], o[1]) * dszm[:, cols]).astype(BF16)
            dkv_ref[:, cols] += dk
            dkv_ref[:, vcols] += dv

        @pl.when(i == pl.num_programs(0) - 1)
        def _():
            tot = dmsum[...]
            hi = tot.astype(BF16)
            lo = (tot - hi.astype(F32)).astype(BF16)
            grp = (lax.broadcasted_iota(jnp.int32, (SGU_W, 128), 0) // HEAD
                   == lax.broadcasted_iota(jnp.int32, (SGU_W, 128), 1)).astype(BF16)
            db_ref[...] = (_dot(hi, grp) + _dot(lo, grp)).T[0:4, :]
            _cast_rows(dwo_acc, dwo_ref, D_MODEL)

    full = lambda shape: pl.BlockSpec(shape, lambda i: (0,) * len(shape))
    row = lambda w: pl.BlockSpec((tm, w), lambda i: (i, 0))
    return pl.pallas_call(
        body, name="branch_bwd", grid=(T_LOC // tm,),
        in_specs=[row(D_MODEL), full((D_MODEL, D_MODEL)), row(D_MODEL)] + _branch_blocks(tm) + [
            row(ATTN_W), pl.BlockSpec((N_MEM, 2 * MEM_W), lambda i: (i // per_ex, 0)),
            full((4, CHUNK, CHUNK)), full((tm, SGU_W)), full((1, SGU_W))],
        out_specs=(row(ATTN_W), row(REST_W), pl.BlockSpec((N_MEM, 2 * MEM_W), lambda i: (i // per_ex, 0)),
                   full((D_MODEL, D_MODEL)), full((4, CHUNK, CHUNK)), full((4, CHUNK)), full((1, SGU_W))),
        out_shape=(SDS((T_LOC, ATTN_W), F32), SDS((T_LOC, REST_W), BF16), SDS((B_LOC * N_MEM, 2 * MEM_W), F32),
                   SDS((D_MODEL, D_MODEL), BF16), SDS((4, CHUNK, CHUNK), F32), SDS((4, CHUNK), F32), SDS((1, SGU_W), F32)),
        scratch_shapes=[pltpu.VMEM((tm, SGU_W), F32), pltpu.VMEM((tm, SGU_W), F32), pltpu.VMEM((CHUNK, SGU_W), F32),
                        pltpu.VMEM((D_MODEL, D_MODEL), F32)],
        compiler_params=_params(56, ("arbitrary",)),
    )(dh2, wout, gated, proj, proj, proj, proj, proj, proj, a, kv, w_s, b_exp, g_v)


def _attn_bwd(proj, slopes, da, a, lse):
    def body(sl_ref, q_ref, k_ref, v_ref, da_ref, a_ref, lse_ref, dq_ref, dk_ref, dv_ref, *scr):
        dq_s, dk_s, dv_s, tab = scr[0:3], scr[3:6], scr[6:9], scr[9]
        lse_h, delta_h = scr[10:12], scr[12:14]
        left = _left_lanes(CHUNK)
        _fill_bias_tables(sl_ref, tab)

        def prep(j, carry):
            rows = pl.ds(pl.multiple_of(j * 256, 256), 256)
            l256 = _left_lanes(256)
            prod = da_ref[rows, :] * a_ref[rows, :]
            delta_h[0][rows, :] = jnp.broadcast_to(jnp.sum(jnp.where(l256, prod, 0.0), axis=-1, keepdims=True), (256, 128))
            delta_h[1][rows, :] = jnp.broadcast_to(jnp.sum(jnp.where(l256, 0.0, prod), axis=-1, keepdims=True), (256, 128))
            pair = lse_ref[rows, :]
            other = pltpu.roll(pair, HEAD, axis=1)
            lse_h[0][rows, :] = jnp.where(l256, pair, other)
            lse_h[1][rows, :] = jnp.where(l256, other, pair)
            zero = jnp.zeros((256, 128), F32)
            for cfg in range(3):
                dk_s[cfg][rows, :] = zero
                dv_s[cfg][rows, :] = zero
            return carry
        lax.fori_loop(0, SEQ // 256, prep, 0)

        def per_row(halves, rq, nk):
            v = jnp.concatenate([halves[0][rq, :], halves[1][rq, :]], axis=0)
            return v if nk == 128 else jnp.concatenate([v, v], axis=1)

        def block(cfg, var, q0, k0, nk):
            d = DILATIONS[cfg][0]
            rq, rk = _rows(q0, CHUNK, d), _rows(k0, nk, d)
            qs = _stack_heads(q_ref[rq, :] * SCALE, left).astype(BF16)
            das = _stack_heads(da_ref[rq, :], left).astype(BF16)
            kw = k_ref[rk, :].astype(BF16)
            vw = v_ref[rk, :].astype(BF16)
            s = _dot_nt(qs, kw) + tab[3 * cfg + var, :, 0:nk]
            p = jnp.exp(s - per_row(lse_h, rq, nk))
            ds = (p * (_dot_nt(das, vw) - per_row(delta_h, rq, nk))).astype(BF16)
            dq_s[cfg][rq, :] = _unstack_heads(_dot(ds, kw), left) * SCALE
            dk_s[cfg][rk, :] += _dot_tn(ds, qs)
            dv_s[cfg][rk, :] += _dot_tn(p, das)
        _attn_blocks(block, 2)

        def flush(j, carry):
            rows = pl.ds(pl.multiple_of(j * 256, 256), 256)
            for acc, dst in ((dq_s, dq_ref), (dk_s, dk_ref), (dv_s, dv_ref)):
                dst[rows, :] = (acc[0][rows, :] + acc[1][rows, :] + acc[2][rows, :]).astype(BF16)
            return carry
        lax.fori_loop(0, SEQ // 256, flush, 0)

    blk = lambda col0: pl.BlockSpec((SEQ, 128), lambda b, hp: (b, col0 // 128 + hp))
    own = pl.BlockSpec((SEQ, 128), lambda b, hp: (b, hp))
    return pl.pallas_call(
        body, name="attn_bwd", grid=(B_LOC, 4),
        in_specs=[pl.BlockSpec((1, 8, 128), lambda b, hp: (hp, 0, 0)), blk(C_QA), blk(C_KA), blk(C_VA), own, own, own],
        out_specs=(own, own, own),
        out_shape=(SDS((T_LOC, ATTN_W), BF16),) * 3,
        scratch_shapes=[pltpu.VMEM((SEQ, 128), F32)] * 9 + [pltpu.VMEM((N_BIAS, 2 * CHUNK, 2 * CHUNK), F32)]
        + [pltpu.VMEM((SEQ, 128), F32)] * 4,
        compiler_params=_params(40, ("arbitrary", "arbitrary")),
    )(slopes, proj, proj, proj, da, a, lse)


def _dproj_specs(tm):
    third = pl.BlockSpec((tm, ATTN_W), lambda i: (i, 0))
    return [third, third, third, pl.BlockSpec((tm, REST_W), lambda i: (i, 0))]


def _dx(dq, dk, dv, dr, wint, x2, dh2, g_norm):
    tm = 256

    def body(dq_ref, dk_ref, dv_ref, dr_ref, w_ref, x_ref, dh2_ref, g_ref, gx_ref, dgn_ref):
        @pl.when(pl.program_id(0) == 0)
        def _():
            dgn_ref[...] = jnp.zeros_like(dgn_ref)
        dh = (_dot(dq_ref[...], w_ref[C_QA:C_KA, :]) + _dot(dk_ref[...], w_ref[C_KA:C_VA, :])
              + _dot(dv_ref[...], w_ref[C_VA:C_ZA, :]) + _dot(dr_ref[...], w_ref[C_ZA:IN_COLS, :]))
        xv = x_ref[...]
        r = _rstd(xv)
        gx_ref[...] = dh2_ref[...] + _rms_bwd(xv, r, g_ref[...], dh)
        dgn_ref[...] += jnp.sum(dh * (xv * r), axis=0, keepdims=True)

    row = pl.BlockSpec((tm, D_MODEL), lambda i: (i, 0))
    vec = pl.BlockSpec((1, D_MODEL), lambda i: (0, 0))
    return pl.pallas_call(
        body, name="dx", grid=(T_LOC // tm,),
        in_specs=_dproj_specs(tm) + [pl.BlockSpec((IN_COLS, D_MODEL), lambda i: (0, 0)), row, row, vec],
        out_specs=(row, vec),
        out_shape=(SDS((T_LOC, D_MODEL), F32), SDS((1, D_MODEL), F32)),
        compiler_params=_params(48, ("arbitrary",)),
    )(dq, dk, dv, dr, wint, x2, dh2, g_norm)


def _dwin(dq, dk, dv, dr, x2, g_norm):
    tm = 512

    def body(dq_ref, dk_ref, dv_ref, dr_ref, x_ref, g_ref, o_ref, acc):
        @pl.when(pl.program_id(0) == 0)
        def _():
            acc[...] = jnp.zeros_like(acc)
        xv = x_ref[...]
        h = (xv * _rstd(xv) * g_ref[...]).astype(BF16)
        acc[C_QA:C_KA, :] += _dot_tn(dq_ref[...], h)
        acc[C_KA:C_VA, :] += _dot_tn(dk_ref[...], h)
        acc[C_VA:C_ZA, :] += _dot_tn(dv_ref[...], h)
        acc[C_ZA:IN_COLS, :] += _dot_tn(dr_ref[...], h)

        @pl.when(pl.program_id(0) == pl.num_programs(0) - 1)
        def _():
            _cast_rows(acc, o_ref, IN_COLS)

    return pl.pallas_call(
        body, name="dwin", grid=(T_LOC // tm,),
        in_specs=_dproj_specs(tm) + [pl.BlockSpec((tm, D_MODEL), lambda i: (i, 0)), pl.BlockSpec((1, D_MODEL), lambda i: (0, 0))],
        out_specs=pl.BlockSpec((IN_COLS, D_MODEL), lambda i: (0, 0)),
        out_shape=SDS((IN_COLS, D_MODEL), BF16),
        scratch_shapes=[pltpu.VMEM((IN_COLS, D_MODEL), F32)],
        compiler_params=_params(56, ("arbitrary",)),
    )(dq, dk, dv, dr, x2, g_norm)


def _memkv_bwd(dkv, mem2, g_mem, wkv):
    def body(dkv_ref, m_ref, g_ref, w_ref, dw_ref, dg_ref):
        mv = m_ref[...]
        r = _rstd(mv)
        dkv_v = dkv_ref[...].astype(BF16)
        dw_ref[...] = _dot_tn(mv * r * g_ref[...], dkv_v).astype(BF16)
        dg_ref[...] = jnp.sum(_dot_nt(dkv_v, w_ref[...]) * (mv * r), axis=0, keepdims=True)

    return pl.pallas_call(
        body, name="memkv_bwd", out_shape=(SDS((D_MODEL, 2 * MEM_W), BF16), SDS((1, D_MODEL), F32)),
        compiler_params=_params(32),
    )(dkv, mem2, g_mem, wkv)


def _reduce_scatter(dwint, dwkv, dwout):
    shards = (W_IN_SHARD, ROW_SHARD, ROW_SHARD)
    widths = (D_MODEL, 2 * MEM_W, D_MODEL)
    step = 32

    def body(a_ref, b_ref, c_ref, ga_ref, gb_ref, gc_ref, *scr):
        from_sib, pair_sum, from_chip = scr[0:3], scr[3:6], scr[6:9]
        send_sems, recv_sems = scr[9], scr[10]
        x, y, c = _mesh_pos()
        srcs, outs = (a_ref, b_ref, c_ref), (ga_ref, gb_ref, gc_ref)
        others = [(1 - x, y), (x, 1 - y), (1 - x, 1 - y)]

        def piece(a, p):
            n = shards[a]
            return srcs[a].at[pl.ds(pl.multiple_of(_flat(p) * n, 16), n), :]

        def to_sibling(a, q):
            chip = (q // 2, q % 2)
            return pltpu.make_async_remote_copy(
                src_ref=piece(a, (*chip, 1 - c)), dst_ref=from_sib[a].at[q],
                send_sem=send_sems.at[4 * a + q], recv_sem=recv_sems.at[4 * a + q],
                device_id=(x, y, 1 - c), device_id_type=MESH)

        def to_chip(a, j):
            return pltpu.make_async_remote_copy(
                src_ref=pair_sum[a].at[j], dst_ref=from_chip[a].at[j],
                send_sem=send_sems.at[12 + 3 * a + j], recv_sem=recv_sems.at[12 + 3 * a + j],
                device_id=(*others[j], c), device_id_type=MESH)

        def add_rows(a, terms, store):
            def one(i, carry):
                rows = pl.ds(pl.multiple_of(i * step, step), step)
                tot = terms[0][rows, :].astype(F32)
                for t in terms[1:]:
                    tot = tot + t[rows, :].astype(F32)
                store(rows, tot)
                return carry
            lax.fori_loop(0, shards[a] // step, one, 0)

        first = [to_sibling(a, q) for a in range(3) for q in range(4)]
        for cp in first:
            cp.start()
        second = []
        for a in range(3):
            for q in range(4):
                to_sibling(a, q).wait_recv()
            for j, chip in enumerate(others):
                def store(rows, tot, a=a, j=j):
                    pair_sum[a][j, rows, :] = tot.astype(BF16)
                add_rows(a, [piece(a, (*chip, c)), from_sib[a].at[2 * chip[0] + chip[1]]], store)
                cp = to_chip(a, j)
                cp.start()
                second.append(cp)
        for a in range(3):
            for j in range(3):
                to_chip(a, j).wait_recv()

            def store(rows, tot, a=a):
                outs[a][rows, :] = tot
            add_rows(a, [piece(a, (x, y, c)), from_sib[a].at[2 * x + y]] + [from_chip[a].at[j] for j in range(3)], store)
        for cp in first + second:
            cp.wait_send()

    vmem = pl.BlockSpec(memory_space=pltpu.VMEM)
    return pl.pallas_call(
        body, name="reduce_scatter",
        out_shape=tuple(SDS((n, w), F32) for n, w in zip(shards, widths)),
        in_specs=[vmem, vmem, vmem], out_specs=(vmem, vmem, vmem),
        scratch_shapes=[pltpu.VMEM((4, n, w), BF16) for n, w in zip(shards, widths)]
        + [pltpu.VMEM((3, n, w), BF16) for n, w in zip(shards, widths)] * 2
        + [pltpu.SemaphoreType.DMA((21,)), pltpu.SemaphoreType.DMA((21,))],
        compiler_params=_params(40),
    )(dwint, dwkv, dwout)


def _allreduce_small(parts):
    n = len(parts)

    def body(*refs):
        ins, outs, bufs = refs[0:n], refs[n:2 * n], refs[2 * n:3 * n]
        send_sems, recv_sems = refs[3 * n], refs[3 * n + 1]
        pos = _mesh_pos()
        me = _flat(pos)
        for a in range(n):
            bufs[a][me] = ins[a][...]

        def copy(a, k, slot):
            return pltpu.make_async_remote_copy(
                src_ref=ins[a], dst_ref=bufs[a].at[slot],
                send_sem=send_sems.at[7 * a + k - 1], recv_sem=recv_sems.at[7 * a + k - 1],
                device_id=_peer(pos, k), device_id_type=MESH)

        sent = [copy(a, k, me) for a in range(n) for k in range(1, N_DEV)]
        for cp in sent:
            cp.start()
        for a in range(n):
            for k in range(1, N_DEV):
                copy(a, k, _flat(_peer(pos, k))).wait_recv()
        for cp in sent:
            cp.wait_send()
        for a in range(n):
            acc = bufs[a][0]
            for s in range(1, N_DEV):
                acc = acc + bufs[a][s]
            outs[a][...] = acc

    vmem = pl.BlockSpec(memory_space=pltpu.VMEM)
    return pl.pallas_call(
        body, name="allreduce_small",
        out_shape=tuple(SDS(p.shape, F32) for p in parts),
        in_specs=[vmem] * n, out_specs=(vmem,) * n,
        scratch_shapes=[pltpu.VMEM((N_DEV,) + p.shape, F32) for p in parts]
        + [pltpu.SemaphoreType.DMA((7 * n,)), pltpu.SemaphoreType.DMA((7 * n,))],
        compiler_params=_params(16),
    )(*parts)


def _adamw(w, g, m, v):
    m = ADAM_B1 * m + (1.0 - ADAM_B1) * g
    v = ADAM_B2 * v + (1.0 - ADAM_B2) * (g * g)
    m_hat = m / (1.0 - ADAM_B1 ** ADAM_STEP)
    v_hat = v / (1.0 - ADAM_B2 ** ADAM_STEP)
    return -ADAM_LR * (m_hat / (jnp.sqrt(v_hat) + ADAM_EPS) + ADAM_WD * w), m, v


def _adam_shard(name, grad, w, m, v):
    def body(g_ref, w_ref, m_ref, v_ref, d_o, m_o, v_o):
        d_o[...], m_o[...], v_o[...] = _adamw(w_ref[...], g_ref[...], m_ref[...], v_ref[...])

    return pl.pallas_call(
        body, name="adam_" + name, out_shape=(SDS(w.shape, F32),) * 3, compiler_params=_params(40),
    )(grad, w, m, v)


def _adam_small(ws, gs, ms, vs):
    n = len(ws)

    def body(*refs):
        w_r, g_r, m_r, v_r = refs[0:n], refs[n:2 * n], refs[2 * n:3 * n], refs[3 * n:4 * n]
        outs = refs[4 * n:]
        for a in range(n):
            outs[3 * a][...], outs[3 * a + 1][...], outs[3 * a + 2][...] = _adamw(
                w_r[a][...], g_r[a][...], m_r[a][...], v_r[a][...])

    return pl.pallas_call(
        body, name="adam_small",
        out_shape=tuple(SDS(w.shape, F32) for w in ws for _ in range(3)),
        compiler_params=_params(16),
    )(*ws, *gs, *ms, *vs)


def kernel(x, mem, g_norm, w_in, w_sgu_spatial, b_sgu_spatial, g_sgu_v, g_mem, w_mem_kv, w_out, g_final, loss_target, m_g_norm, m_w_in, m_w_sgu_spatial, m_b_sgu_spatial, m_g_sgu_v, m_g_mem, m_w_mem_kv, m_w_out, m_g_final, v_g_norm, v_w_in, v_w_sgu_spatial, v_b_sgu_spatial, v_g_sgu_v, v_g_mem, v_w_mem_kv, v_w_out, v_g_final):
    x2 = x.reshape(T_LOC, D_MODEL)
    tgt2 = loss_target.reshape(T_LOC, D_MODEL)
    mem2 = mem.reshape(B_LOC * N_MEM, D_MODEL)
    w_s = w_sgu_spatial[0]
    b_exp = jnp.tile(jnp.repeat(b_sgu_spatial[0].T, HEAD, axis=1), (2, 1))
    slope = jnp.power(2.0, -8.0 * (jnp.arange(8, dtype=F32) + 1.0) / 8)
    slopes = jnp.broadcast_to(jnp.repeat(slope.reshape(4, 2), HEAD, axis=1)[:, None, :], (4, 8, 128))

    tr = lambda t: jnp.transpose(t[0])
    hbm = lambda t: pltpu.with_memory_space_constraint(t, pltpu.HBM)

    wint, wkv, wout = _allgather_weights(tr(w_in), w_mem_kv[0], w_out[0])
    proj = hbm(_proj_fwd(x2, g_norm, wint))
    kv = _memkv_fwd(mem2, g_mem, wkv)
    a, lse = map(hbm, _attn_fwd(proj, slopes))
    gated = hbm(_branch_fwd(proj, a, kv, w_s, b_exp, g_sgu_v))
    dh2, loss8, dgf = _outproj_loss(gated, wout, x2, tgt2, g_final.reshape(1, D_MODEL))
    dh2 = hbm(dh2)

    da, dr, dkv, dwout, dws, dbs, dgv = _branch_bwd(dh2, wout, gated, proj, a, kv, w_s, b_exp, g_sgu_v)
    da, dr = hbm(da), hbm(dr)
    dq, dk, dv = map(hbm, _attn_bwd(proj, slopes, da, a, lse))
    grad_x, dgn = _dx(dq, dk, dv, dr, wint, x2, dh2, g_norm)
    dwint = _dwin(dq, dk, dv, dr, x2, g_norm)
    dwkv, dgm = _memkv_bwd(dkv, mem2, g_mem, wkv)

    g_win_t, g_wkv, g_wout = _reduce_scatter(dwint, dwkv, dwout)
    *g_small, loss_sum = _allreduce_small([dgn, dws.reshape(4 * CHUNK, CHUNK), dbs, dgv, dgm, dgf, loss8])

    d_win, m_win, v_win = map(jnp.transpose, _adam_shard("w_in", g_win_t, tr(w_in), tr(m_w_in), tr(v_w_in)))
    g_win = jnp.transpose(g_win_t)
    d_wkv, m_wkv, v_wkv = _adam_shard("w_mem_kv", g_wkv, w_mem_kv[0], m_w_mem_kv[0], v_w_mem_kv[0])
    d_wout, m_wout, v_wout = _adam_shard("w_out", g_wout, w_out[0], m_w_out[0], v_w_out[0])

    small_shapes = [(1, D_MODEL), (4 * CHUNK, CHUNK), (4, CHUNK), (1, SGU_W), (1, D_MODEL), (1, D_MODEL)]
    pack = lambda arrs: [t.reshape(s) for t, s in zip(arrs, small_shapes)]
    upd = _adam_small(
        pack([g_norm, w_sgu_spatial, b_sgu_spatial, g_sgu_v, g_mem, g_final]), g_small,
        pack([m_g_norm, m_w_sgu_spatial, m_b_sgu_spatial, m_g_sgu_v, m_g_mem, m_g_final]),
        pack([v_g_norm, v_w_sgu_spatial, v_b_sgu_spatial, v_g_sgu_v, v_g_mem, v_g_final]))
    out_shapes = [g_norm.shape, w_sgu_spatial.shape, b_sgu_spatial.shape, g_sgu_v.shape, g_mem.shape, g_final.shape]
    unpack = lambda arrs: [t.reshape(s) for t, s in zip(arrs, out_shapes)]
    gs = unpack(g_small)
    ds, nms, nvs = unpack(upd[0::3]), unpack(upd[1::3]), unpack(upd[2::3])

    loss = loss_sum[0, 0]

    def assemble(small, win, wkv_, wout_):
        return [small[0], win[None], small[1], small[2], small[3], small[4], wkv_[None], wout_[None], small[5]]

    return (loss, grad_x.reshape(x.shape),
            *assemble(gs, g_win, g_wkv, g_wout), *assemble(ds, d_win, d_wkv, d_wout),
            *assemble(nms, m_win, m_wkv, m_wout), *assemble(nvs, v_win, v_wkv, v_wout))
```

```python
import functools

import jax
import jax.numpy as jnp
from jax import lax
from jax.experimental import pallas as pl
from jax.experimental.pallas import tpu as pltpu

F32 = jnp.float32
BF16 = jnp.bfloat16
SDS = jax.ShapeDtypeStruct
MESH = pl.DeviceIdType.MESH

N_DEV = 8
D_MODEL = 1024
SEQ = 2048
B_LOC = 2
T_LOC = B_LOC * SEQ
N_MEM = 256
HEAD = 64
ATTN_W = 512
SGU_W = 256
MEM_W = 256
IN_COLS = 3328
W_IN_SHARD = IN_COLS // N_DEV
ROW_SHARD = D_MODEL // N_DEV
CHUNK = 128
DILATIONS = ((1, 2048), (4, 512), (16, 128))
RADIUS = 64
EPS = 1e-6
NEG = -1e30
SCALE = HEAD ** -0.5
C_QA, C_KA, C_VA, C_ZA, C_UB, C_VB, C_ZB, C_QM, C_ZM = 0, 512, 1024, 1536, 2048, 2304, 2560, 2816, 3072
QKV_W = 1536
REST_W = IN_COLS - QKV_W

ADAM_LR, ADAM_B1, ADAM_B2, ADAM_EPS, ADAM_WD, ADAM_STEP = 0.001, 0.9, 0.999, 1e-08, 0.01, 10

V7X_VMEM_MIB = 64


def _params(vmem_mib, sem=None, **kw):
    assert vmem_mib < V7X_VMEM_MIB
    return pltpu.CompilerParams(vmem_limit_bytes=vmem_mib << 20, dimension_semantics=sem, **kw)


def _dot(a, b):
    return jnp.dot(a.astype(BF16), b.astype(BF16), preferred_element_type=F32)


def _dot_nt(a, b):
    return lax.dot_general(a.astype(BF16), b.astype(BF16), (((1,), (1,)), ((), ())), preferred_element_type=F32)


def _dot_tn(a, b):
    return lax.dot_general(a.astype(BF16), b.astype(BF16), (((0,), (0,)), ((), ())), preferred_element_type=F32)


def _rstd(v):
    return lax.rsqrt(jnp.mean(v * v, axis=-1, keepdims=True) + EPS)


def _rms_bwd(v, r, g, dy):
    gdy = g * dy
    return r * gdy - v * (r * r * r * jnp.mean(gdy * v, axis=-1, keepdims=True))


def _sigmoid(z):
    return 1.0 / (1.0 + jnp.exp(-z))


def _silu_and_grad(z):
    s = _sigmoid(z)
    return z * s, s * (1.0 + z * (1.0 - s))


_G_C = 0.7978845608028654
_G_K = 0.044715


def _gelu_and_grad(v):
    t = jnp.tanh(_G_C * (v + _G_K * (v * v * v)))
    cdf = 0.5 * (1.0 + t)
    return v * cdf, cdf + 0.5 * v * (1.0 - t * t) * (_G_C * (1.0 + 3.0 * _G_K * v * v))


def _cast_rows(src_ref, dst_ref, rows, step=256):
    def one(i, carry):
        r = pl.ds(pl.multiple_of(i * step, step), step)
        dst_ref[r, :] = src_ref[r, :].astype(dst_ref.dtype)
        return carry
    lax.fori_loop(0, rows // step, one, 0)


def _left_lanes(rows):
    return lax.broadcasted_iota(jnp.int32, (rows, 128), 1) < HEAD


def _mesh_pos():
    return lax.axis_index("x"), lax.axis_index("y"), lax.axis_index("c")


def _peer(pos, k):
    x, y, c = pos
    return (1 - x if k & 4 else x, 1 - y if k & 2 else y, 1 - c if k & 1 else c)


def _flat(pos):
    return 4 * pos[0] + 2 * pos[1] + pos[2]


def _allgather_weights(w_in_t, w_kv, w_out):
    shards = (W_IN_SHARD, ROW_SHARD, ROW_SHARD)

    def body(win_ref, wkv_ref, wout_ref, wint_o, wkv_o, wout_o, send_sems, recv_sems):
        x, y, c = _mesh_pos()
        me, sib = (x, y, c), (x, y, 1 - c)
        chips = [(1 - x, y), (x, 1 - y), (1 - x, 1 - y)]
        outs = (wint_o, wkv_o, wout_o)

        def rows(a, p):
            n = shards[a]
            return outs[a].at[pl.ds(pl.multiple_of(_flat(p) * n, 16), n), :]

        rows(0, me)[...] = win_ref[...].astype(BF16)
        rows(1, me)[...] = wkv_ref[...].astype(BF16)
        rows(2, me)[...] = wout_ref[...].astype(BF16)

        def copy(a, k, block, to):
            return pltpu.make_async_remote_copy(
                src_ref=rows(a, block), dst_ref=rows(a, block),
                send_sem=send_sems.at[7 * a + k], recv_sem=recv_sems.at[7 * a + k],
                device_id=to, device_id_type=MESH)

        first = []
        for a in range(3):
            first.append(copy(a, 0, me, sib))
            first += [copy(a, 1 + j, me, (*chip, c)) for j, chip in enumerate(chips)]
        for cp in first:
            cp.start()
        passed = []
        for a in range(3):
            for j, chip in enumerate(chips):
                copy(a, 1 + j, (*chip, c), me).wait_recv()
                fwd = copy(a, 4 + j, (*chip, c), sib)
                fwd.start()
                passed.append(fwd)
        for a in range(3):
            copy(a, 0, sib, me).wait_recv()
            for j, chip in enumerate(chips):
                copy(a, 4 + j, (*chip, 1 - c), me).wait_recv()
        for cp in first + passed:
            cp.wait_send()

    vmem = pl.BlockSpec(memory_space=pltpu.VMEM)
    return pl.pallas_call(
        body, name="allgather_weights",
        out_shape=(SDS((IN_COLS, D_MODEL), BF16), SDS((D_MODEL, 2 * MEM_W), BF16), SDS((D_MODEL, D_MODEL), BF16)),
        in_specs=[vmem, vmem, vmem], out_specs=(vmem, vmem, vmem),
        scratch_shapes=[pltpu.SemaphoreType.DMA((21,)), pltpu.SemaphoreType.DMA((21,))],
        compiler_params=_params(40),
    )(w_in_t, w_kv, w_out)


def _proj_fwd(x2, g_norm, wint):
    tm = 256

    def body(x_ref, g_ref, w_ref, o_ref):
        xv = x_ref[...]
        h = xv * _rstd(xv) * g_ref[...]
        o_ref[...] = _dot_nt(h, w_ref[...])

    return pl.pallas_call(
        body, name="proj_fwd", grid=(T_LOC // tm,),
        in_specs=[pl.BlockSpec((tm, D_MODEL), lambda i: (i, 0)), pl.BlockSpec((1, D_MODEL), lambda i: (0, 0)),
                  pl.BlockSpec((IN_COLS, D_MODEL), lambda i: (0, 0))],
        out_specs=pl.BlockSpec((tm, IN_COLS), lambda i: (i, 0)),
        out_shape=SDS((T_LOC, IN_COLS), F32),
        compiler_params=_params(48, ("arbitrary",)),
    )(x2, g_norm, wint)


def _memkv_fwd(mem2, g_mem, wkv):
    def body(m_ref, g_ref, w_ref, o_ref):
        mv = m_ref[...]
        o_ref[...] = _dot(mv * _rstd(mv) * g_ref[...], w_ref[...])

    return pl.pallas_call(
        body, name="memkv_fwd", out_shape=SDS((B_LOC * N_MEM, 2 * MEM_W), F32), compiler_params=_params(32),
    )(mem2, g_mem, wkv)


N_BIAS = 7


def _fill_bias_tables(sl_ref, tab):
    for cfg, (d, length) in enumerate(DILATIONS):
        nk = min(length, 2 * CHUNK)
        r = lax.broadcasted_iota(jnp.int32, (CHUNK, nk), 0)
        c = lax.broadcasted_iota(jnp.int32, (CHUNK, nk), 1)
        for var in range(3 if length > nk else 1):
            rel = jnp.abs(r - c + var * RADIUS)
            dist = rel.astype(F32) * float(d)
            for h in range(2):
                slope = sl_ref[0, 0:1, h * HEAD:h * HEAD + 1]
                tab[3 * cfg + var, h * CHUNK:(h + 1) * CHUNK, 0:nk] = jnp.where(rel <= RADIUS, -slope * dist, NEG)


def _attn_blocks(visit, unroll):
    def step(t, carry):
        for cfg, (d, length) in enumerate(DILATIONS):
            nblk = length // CHUNK
            if nblk == 1:
                visit(cfg, 0, t, t, length)
                continue
            rho, i = (0, t) if d == 1 else (t // nblk, t % nblk)
            ks = jnp.clip(i * CHUNK - RADIUS, 0, length - 2 * CHUNK)
            visit(cfg, (i * CHUNK - ks) // RADIUS, rho + d * (i * CHUNK), rho + d * ks, 2 * CHUNK)
        return carry
    lax.fori_loop(0, 16, step, 0, unroll=unroll)


def _stack_heads(v, left):
    return jnp.concatenate([jnp.where(left, v, 0.0), jnp.where(left, 0.0, v)], axis=0)


def _unstack_heads(v, left):
    return jnp.where(left, v[0:CHUNK], v[CHUNK:2 * CHUNK])


def _rows(start, n, d):
    return pl.ds(start, n) if d == 1 else pl.ds(start, n, stride=d)


def _attn_fwd(proj, slopes):
    def body(sl_ref, q_ref, k_ref, v_ref, a_ref, lse_ref, *scr):
        o_c, m_c, l_c, tab = scr[0:3], scr[3:6], scr[6:9], scr[9]
        left = _left_lanes(CHUNK)
        _fill_bias_tables(sl_ref, tab)

        def block(cfg, var, q0, k0, nk):
            d = DILATIONS[cfg][0]
            rq, rk = _rows(q0, CHUNK, d), _rows(k0, nk, d)
            qs = _stack_heads(q_ref[rq, :] * SCALE, left)
            s = _dot_nt(qs, k_ref[rk, :]) + tab[3 * cfg + var, :, 0:nk]
            m = jnp.max(s, axis=-1, keepdims=True)
            p = jnp.exp(s - m)
            o_c[cfg][rq, :] = _unstack_heads(_dot(p, v_ref[rk, :]), left)
            m_c[cfg][rq, :] = _unstack_heads(m, left)
            l_c[cfg][rq, :] = _unstack_heads(jnp.sum(p, axis=-1, keepdims=True), left)
        _attn_blocks(block, 4)

        def merge(j, carry):
            rows = pl.ds(pl.multiple_of(j * 256, 256), 256)
            ms = [m_c[i][rows, :] for i in range(3)]
            top = jnp.maximum(jnp.maximum(ms[0], ms[1]), ms[2])
            ws = [jnp.exp(m - top) for m in ms]
            den = l_c[0][rows, :] * ws[0] + l_c[1][rows, :] * ws[1] + l_c[2][rows, :] * ws[2]
            num = o_c[0][rows, :] * ws[0] + o_c[1][rows, :] * ws[1] + o_c[2][rows, :] * ws[2]
            a_ref[rows, :] = num / den
            lse_ref[rows, :] = top + jnp.log(den)
            return carry
        lax.fori_loop(0, SEQ // 256, merge, 0)

    blk = lambda col0: pl.BlockSpec((SEQ, 128), lambda b, hp: (b, col0 // 128 + hp))
    out = pl.BlockSpec((SEQ, 128), lambda b, hp: (b, hp))
    return pl.pallas_call(
        body, name="attn_fwd", grid=(B_LOC, 4),
        in_specs=[pl.BlockSpec((1, 8, 128), lambda b, hp: (hp, 0, 0)), blk(C_QA), blk(C_KA), blk(C_VA)],
        out_specs=(out, out),
        out_shape=(SDS((T_LOC, ATTN_W), F32), SDS((T_LOC, ATTN_W), F32)),
        scratch_shapes=[pltpu.VMEM((SEQ, 128), F32)] * 9 + [pltpu.VMEM((N_BIAS, 2 * CHUNK, 2 * CHUNK), F32)],
        compiler_params=_params(40, ("arbitrary", "arbitrary")),
    )(slopes, proj, proj, proj)


def _sgu_mix(vn, ws_ref, dst_ref, tm):
    left = _left_lanes(CHUNK)
    for ch in range(tm // CHUNK):
        for pr in range(2):
            vp = vn[ch * CHUNK:(ch + 1) * CHUNK, pr * 128:(pr + 1) * 128]
            dst_ref[ch * CHUNK:(ch + 1) * CHUNK, pr * 128:(pr + 1) * 128] = jnp.where(
                left, _dot(ws_ref[2 * pr], vp), _dot(ws_ref[2 * pr + 1], vp))


def _mem_attn_head(qp, kp, h, left):
    qh = jnp.where(left if h == 0 else ~left, qp, 0.0)
    s = _dot_nt(qh, kp) * SCALE
    e = jnp.exp(s - jnp.max(s, axis=-1, keepdims=True))
    return e * (1.0 / jnp.sum(e, axis=-1, keepdims=True)), qh


def _branch_blocks(tm):
    col = lambda w, c0: pl.BlockSpec((tm, w), lambda i: (i, c0 // w))
    return [col(512, C_ZA), col(256, C_UB), col(256, C_VB), col(256, C_ZB), col(256, C_QM), col(256, C_ZM)]


def _branch_fwd(proj, a, kv, w_s, b_exp, g_v):
    tm = 256
    per_ex = SEQ // tm

    def body(za_ref, ub_ref, vb_ref, zb_ref, qm_ref, zm_ref, a_ref, kv_ref, ws_ref, be_ref, gv_ref, o_ref, mix):
        left = _left_lanes(tm)
        o_ref[:, 0:ATTN_W] = (_silu_and_grad(za_ref[...])[0] * a_ref[...]).astype(BF16)
        gu = _gelu_and_grad(ub_ref[...])[0]
        gv = _gelu_and_grad(vb_ref[...])[0]
        vn = gv * _rstd(gv) * gv_ref[...]
        _sgu_mix(vn.astype(BF16), ws_ref, mix, tm)
        sg = gu * (mix[...] + be_ref[...])
        o_ref[:, ATTN_W:ATTN_W + SGU_W] = (_silu_and_grad(zb_ref[...])[0] * sg).astype(BF16)
        szm = _silu_and_grad(zm_ref[...])[0]
        for hp in range(2):
            cols = slice(hp * 128, (hp + 1) * 128)
            qp, kp, vp = qm_ref[:, cols], kv_ref[:, cols], kv_ref[:, MEM_W + hp * 128:MEM_W + (hp + 1) * 128]
            o = [_dot(_mem_attn_head(qp, kp, h, left)[0], vp) for h in range(2)]
            c0 = ATTN_W + SGU_W + hp * 128
            o_ref[:, c0:c0 + 128] = (szm[:, cols] * jnp.where(left, o[0], o[1])).astype(BF16)

    full = lambda shape: pl.BlockSpec(shape, lambda i: (0,) * len(shape))
    return pl.pallas_call(
        body, name="branch_fwd", grid=(T_LOC // tm,),
        in_specs=_branch_blocks(tm) + [
            pl.BlockSpec((tm, ATTN_W), lambda i: (i, 0)), pl.BlockSpec((N_MEM, 2 * MEM_W), lambda i: (i // per_ex, 0)),
            full((4, CHUNK, CHUNK)), full((tm, SGU_W)), full((1, SGU_W))],
        out_specs=pl.BlockSpec((tm, D_MODEL), lambda i: (i, 0)),
        out_shape=SDS((T_LOC, D_MODEL), BF16),
        scratch_shapes=[pltpu.VMEM((tm, SGU_W), F32)],
        compiler_params=_params(40, ("arbitrary",)),
    )(proj, proj, proj, proj, proj, proj, a, kv, w_s, b_exp, g_v)


def _outproj_loss(gated, wout, x2, tgt2, g_final):
    tm = 512

    def body(g_ref, w_ref, x_ref, t_ref, gf_ref, dh2_ref, loss_ref, dgf_ref):
        @pl.when(pl.program_id(0) == 0)
        def _():
            loss_ref[...] = jnp.zeros_like(loss_ref)
            dgf_ref[...] = jnp.zeros_like(dgf_ref)
        h2 = x_ref[...] + _dot(g_ref[...], w_ref[...])
        r = _rstd(h2)
        gf = gf_ref[...]
        err = h2 * r * gf - t_ref[...]
        loss_ref[...] += 0.5 * jnp.sum(jnp.mean(err * err, axis=-1, keepdims=True))
        dy = err * (1.0 / D_MODEL)
        dh2_ref[...] = _rms_bwd(h2, r, gf, dy)
        dgf_ref[...] += jnp.sum(dy * (h2 * r), axis=0, keepdims=True)

    row = pl.BlockSpec((tm, D_MODEL), lambda i: (i, 0))
    vec = pl.BlockSpec((1, D_MODEL), lambda i: (0, 0))
    return pl.pallas_call(
        body, name="outproj_loss", grid=(T_LOC // tm,),
        in_specs=[row, pl.BlockSpec((D_MODEL, D_MODEL), lambda i: (0, 0)), row, row, vec],
        out_specs=(row, pl.BlockSpec((8, 128), lambda i: (0, 0)), vec),
        out_shape=(SDS((T_LOC, D_MODEL), F32), SDS((8, 128), F32), SDS((1, D_MODEL), F32)),
        compiler_params=_params(40, ("arbitrary",)),
    )(gated, wout, x2, tgt2, g_final)


def _branch_bwd(dh2, wout, gated, proj, a, kv, w_s, b_exp, g_v):
    tm = 256
    per_ex = SEQ // tm

    def body(dh2_ref, w_ref, g_ref, za_ref, ub_ref, vb_ref, zb_ref, qm_ref, zm_ref, a_ref, kv_ref, ws_ref,
             be_ref, gv_ref, da_ref, dr_ref, dkv_ref, dwo_ref, dws_ref, db_ref, dgv_ref, mix, dvn, dmsum, dwo_acc):
        i = pl.program_id(0)
        left = _left_lanes(tm)
        leftc = _left_lanes(CHUNK)

        @pl.when(i == 0)
        def _():
            dwo_acc[...] = jnp.zeros_like(dwo_acc)
            dws_ref[...] = jnp.zeros_like(dws_ref)
            dgv_ref[...] = jnp.zeros_like(dgv_ref)
            dmsum[...] = jnp.zeros_like(dmsum)

        @pl.when(i % per_ex == 0)
        def _():
            dkv_ref[...] = jnp.zeros_like(dkv_ref)

        dh2 = dh2_ref[...].astype(BF16)
        dwo_acc[...] += _dot_tn(g_ref[...], dh2)
        dg = _dot_nt(dh2, w_ref[...])

        sa, dsa = _silu_and_grad(za_ref[...])
        dga = dg[:, 0:ATTN_W]
        da_ref[...] = dga * sa
        dr_ref[:, 0:512] = (dga * a_ref[...] * dsa).astype(BF16)

        ub, vb = ub_ref[...], vb_ref[...]
        gu, dgu = _gelu_and_grad(ub)
        gv, dgv = _gelu_and_grad(vb)
        rv = _rstd(gv)
        gain = gv_ref[...]
        vn = (gv * rv * gain).astype(BF16)
        _sgu_mix(vn, ws_ref, mix, tm)
        mixed = mix[...] + be_ref[...]
        sb, dsb = _silu_and_grad(zb_ref[...])
        dgb = dg[:, ATTN_W:ATTN_W + SGU_W]
        dsg = dgb * sb
        dr_ref[:, 512:768] = (dsg * mixed * dgu).astype(BF16)
        dr_ref[:, 1024:1280] = (dgb * (gu * mixed) * dsb).astype(BF16)
        dmix = dsg * gu
        for ch in range(tm // CHUNK):
            rows = slice(ch * CHUNK, (ch + 1) * CHUNK)
            dmsum[...] += dmix[rows, :]
            for pr in range(2):
                cols = slice(pr * 128, (pr + 1) * 128)
                dmp, vp = dmix[rows, cols], vn[rows, cols]
                for h in range(2):
                    g = 2 * pr + h
                    dws_ref[g] += _dot_nt(jnp.where(leftc if h == 0 else ~leftc, dmp, 0.0), vp)
                dvn[rows, cols] = jnp.where(leftc, _dot_tn(ws_ref[2 * pr], dmp), _dot_tn(ws_ref[2 * pr + 1], dmp))
        dvn_v = dvn[...]
        dgv_ref[...] += jnp.sum(dvn_v * (gv * rv), axis=0, keepdims=True)
        dr_ref[:, 768:1024] = (_rms_bwd(gv, rv, gain, dvn_v) * dgv).astype(BF16)

        szm, dszm = _silu_and_grad(zm_ref[...])
        dgm = dg[:, ATTN_W + SGU_W:D_MODEL]
        dmo = dgm * szm
        for hp in range(2):
            cols = slice(hp * 128, (hp + 1) * 128)
            vcols = slice(MEM_W + hp * 128, MEM_W + (hp + 1) * 128)
            qp, kp, vp, dmop = qm_ref[:, cols], kv_ref[:, cols], kv_ref[:, vcols], dmo[:, cols]
            o, dq = [], []
            dk = jnp.zeros((N_MEM, 128), F32)
            dv = jnp.zeros((N_MEM, 128), F32)
            for h in range(2):
                p, qh = _mem_attn_head(qp, kp, h, left)
                dmoh = jnp.where(left if h == 0 else ~left, dmop, 0.0)
                o.append(_dot(p, vp))
                dp = _dot_nt(dmoh, vp)
                ds = p * (dp - jnp.sum(p * dp, axis=-1, keepdims=True)) * SCALE
                dq.append(_dot(ds, kp))
                dk += _dot_tn(ds, qh)
                dv += _dot_tn(p, dmoh)
            dr_ref[:, 1280 + hp * 128:1280 + (hp + 1) * 128] = jnp.where(left, dq[0], dq[1]).astype(BF16)
            dr_ref[:, 1536 + hp * 128:1536 + (hp + 1) * 128] = (
                dgm[:, cols] * jnp.where(left, o[0], o[1]) * dszm[:, cols]).astype(BF16)
            dkv_ref[:, cols] += dk
            dkv_ref[:, vcols] += dv

        @pl.when(i == pl.num_programs(0) - 1)
        def _():
            tot = dmsum[...]
            hi = tot.astype(BF16)
            lo = (tot - hi.astype(F32)).astype(BF16)
            grp = (lax.broadcasted_iota(jnp.int32, (SGU_W, 128), 0) // HEAD
                   == lax.broadcasted_iota(jnp.int32, (SGU_W, 128), 1)).astype(BF16)
            db_ref[...] = (_dot(hi, grp) + _dot(lo, grp)).T[0:4, :]
            _cast_rows(dwo_acc, dwo_ref, D_MODEL)

    full = lambda shape: pl.BlockSpec(shape, lambda i: (0,) * len(shape))
    row = lambda w: pl.BlockSpec((tm, w), lambda i: (i, 0))
    return pl.pallas_call(
        body, name="branch_bwd", grid=(T_LOC // tm,),
        in_specs=[row(D_MODEL), full((D_MODEL, D_MODEL)), row(D_MODEL)] + _branch_blocks(tm) + [
            row(ATTN_W), pl.BlockSpec((N_MEM, 2 * MEM_W), lambda i: (i // per_ex, 0)),
            full((4, CHUNK, CHUNK)), full((tm, SGU_W)), full((1, SGU_W))],
        out_specs=(row(ATTN_W), row(REST_W), pl.BlockSpec((N_MEM, 2 * MEM_W), lambda i: (i // per_ex, 0)),
                   full((D_MODEL, D_MODEL)), full((4, CHUNK, CHUNK)), full((4, CHUNK)), full((1, SGU_W))),
        out_shape=(SDS((T_LOC, ATTN_W), F32), SDS((T_LOC, REST_W), BF16), SDS((B_LOC * N_MEM, 2 * MEM_W), F32),
                   SDS((D_MODEL, D_MODEL), BF16), SDS((4, CHUNK, CHUNK), F32), SDS((4, CHUNK), F32), SDS((1, SGU_W), F32)),
        scratch_shapes=[pltpu.VMEM((tm, SGU_W), F32), pltpu.VMEM((tm, SGU_W), F32), pltpu.VMEM((CHUNK, SGU_W), F32),
                        pltpu.VMEM((D_MODEL, D_MODEL), F32)],
        compiler_params=_params(56, ("arbitrary",)),
    )(dh2, wout, gated, proj, proj, proj, proj, proj, proj, a, kv, w_s, b_exp, g_v)


def _attn_bwd(proj, slopes, da, a, lse):
    def body(sl_ref, q_ref, k_ref, v_ref, da_ref, a_ref, lse_ref, dq_ref, dk_ref, dv_ref, *scr):
        dq_s, dk_s, dv_s, tab = scr[0:3], scr[3:6], scr[6:9], scr[9]
        lse_h, delta_h = scr[10:12], scr[12:14]
        left = _left_lanes(CHUNK)
        _fill_bias_tables(sl_ref, tab)

        def prep(j, carry):
            rows = pl.ds(pl.multiple_of(j * 256, 256), 256)
            l256 = _left_lanes(256)
            prod = da_ref[rows, :] * a_ref[rows, :]
            delta_h[0][rows, :] = jnp.broadcast_to(jnp.sum(jnp.where(l256, prod, 0.0), axis=-1, keepdims=True), (256, 128))
            delta_h[1][rows, :] = jnp.broadcast_to(jnp.sum(jnp.where(l256, 0.0, prod), axis=-1, keepdims=True), (256, 128))
            pair = lse_ref[rows, :]
            other = pltpu.roll(pair, HEAD, axis=1)
            lse_h[0][rows, :] = jnp.where(l256, pair, other)
            lse_h[1][rows, :] = jnp.where(l256, other, pair)
            zero = jnp.zeros((256, 128), F32)
            for cfg in range(3):
                dk_s[cfg][rows, :] = zero
                dv_s[cfg][rows, :] = zero
            return carry
        lax.fori_loop(0, SEQ // 256, prep, 0)

        def per_row(halves, rq, nk):
            v = jnp.concatenate([halves[0][rq, :], halves[1][rq, :]], axis=0)
            return v if nk == 128 else jnp.concatenate([v, v], axis=1)

        def block(cfg, var, q0, k0, nk):
            d = DILATIONS[cfg][0]
            rq, rk = _rows(q0, CHUNK, d), _rows(k0, nk, d)
            qs = _stack_heads(q_ref[rq, :] * SCALE, left).astype(BF16)
            das = _stack_heads(da_ref[rq, :], left).astype(BF16)
            kw = k_ref[rk, :].astype(BF16)
            vw = v_ref[rk, :].astype(BF16)
            s = _dot_nt(qs, kw) + tab[3 * cfg + var, :, 0:nk]
            p = jnp.exp(s - per_row(lse_h, rq, nk))
            ds = (p * (_dot_nt(das, vw) - per_row(delta_h, rq, nk))).astype(BF16)
            dq_s[cfg][rq, :] = _unstack_heads(_dot(ds, kw), left) * SCALE
            dk_s[cfg][rk, :] += _dot_tn(ds, qs)
            dv_s[cfg][rk, :] += _dot_tn(p, das)
        _attn_blocks(block, 2)

        def flush(j, carry):
            rows = pl.ds(pl.multiple_of(j * 256, 256), 256)
            for acc, dst in ((dq_s, dq_ref), (dk_s, dk_ref), (dv_s, dv_ref)):
                dst[rows, :] = (acc[0][rows, :] + acc[1][rows, :] + acc[2][rows, :]).astype(BF16)
            return carry
        lax.fori_loop(0, SEQ // 256, flush, 0)

    blk = lambda col0: pl.BlockSpec((SEQ, 128), lambda b, hp: (b, col0 // 128 + hp))
    own = pl.BlockSpec((SEQ, 128), lambda b, hp: (b, hp))
    return pl.pallas_call(
        body, name="attn_bwd", grid=(B_LOC, 4),
        in_specs=[pl.BlockSpec((1, 8, 128), lambda b, hp: (hp, 0, 0)), blk(C_QA), blk(C_KA), blk(C_VA), own, own, own],
        out_specs=(own, own, own),
        out_shape=(SDS((T_LOC, ATTN_W), BF16),) * 3,
        scratch_shapes=[pltpu.VMEM((SEQ, 128), F32)] * 9 + [pltpu.VMEM((N_BIAS, 2 * CHUNK, 2 * CHUNK), F32)]
        + [pltpu.VMEM((SEQ, 128), F32)] * 4,
        compiler_params=_params(40, ("arbitrary", "arbitrary")),
    )(slopes, proj, proj, proj, da, a, lse)


def _dproj_specs(tm):
    third = pl.BlockSpec((tm, ATTN_W), lambda i: (i, 0))
    return [third, third, third, pl.BlockSpec((tm, REST_W), lambda i: (i, 0))]


def _dx(dq, dk, dv, dr, wint, x2, dh2, g_norm):
    tm = 256

    def body(dq_ref, dk_ref, dv_ref, dr_ref, w_ref, x_ref, dh2_ref, g_ref, gx_ref, dgn_ref):
        @pl.when(pl.program_id(0) == 0)
        def _():
            dgn_ref[...] = jnp.zeros_like(dgn_ref)
        dh = (_dot(dq_ref[...], w_ref[C_QA:C_KA, :]) + _dot(dk_ref[...], w_ref[C_KA:C_VA, :])
              + _dot(dv_ref[...], w_ref[C_VA:C_ZA, :]) + _dot(dr_ref[...], w_ref[C_ZA:IN_COLS, :]))
        xv = x_ref[...]
        r = _rstd(xv)
        gx_ref[...] = dh2_ref[...] + _rms_bwd(xv, r, g_ref[...], dh)
        dgn_ref[...] += jnp.sum(dh * (xv * r), axis=0, keepdims=True)

    row = pl.BlockSpec((tm, D_MODEL), lambda i: (i, 0))
    vec = pl.BlockSpec((1, D_MODEL), lambda i: (0, 0))
    return pl.pallas_call(
        body, name="dx", grid=(T_LOC // tm,),
        in_specs=_dproj_specs(tm) + [pl.BlockSpec((IN_COLS, D_MODEL), lambda i: (0, 0)), row, row, vec],
        out_specs=(row, vec),
        out_shape=(SDS((T_LOC, D_MODEL), F32), SDS((1, D_MODEL), F32)),
        compiler_params=_params(48, ("arbitrary",)),
    )(dq, dk, dv, dr, wint, x2, dh2, g_norm)


def _dwin(dq, dk, dv, dr, x2, g_norm):
    tm = 512

    def body(dq_ref, dk_ref, dv_ref, dr_ref, x_ref, g_ref, o_ref, acc):
        @pl.when(pl.program_id(0) == 0)
        def _():
            acc[...] = jnp.zeros_like(acc)
        xv = x_ref[...]
        h = (xv * _rstd(xv) * g_ref[...]).astype(BF16)
        acc[C_QA:C_KA, :] += _dot_tn(dq_ref[...], h)
        acc[C_KA:C_VA, :] += _dot_tn(dk_ref[...], h)
        acc[C_VA:C_ZA, :] += _dot_tn(dv_ref[...], h)
        acc[C_ZA:IN_COLS, :] += _dot_tn(dr_ref[...], h)

        @pl.when(pl.program_id(0) == pl.num_programs(0) - 1)
        def _():
            _cast_rows(acc, o_ref, IN_COLS)

    return pl.pallas_call(
        body, name="dwin", grid=(T_LOC // tm,),
        in_specs=_dproj_specs(tm) + [pl.BlockSpec((tm, D_MODEL), lambda i: (i, 0)), pl.BlockSpec((1, D_MODEL), lambda i: (0, 0))],
        out_specs=pl.BlockSpec((IN_COLS, D_MODEL), lambda i: (0, 0)),
        out_shape=SDS((IN_COLS, D_MODEL), BF16),
        scratch_shapes=[pltpu.VMEM((IN_COLS, D_MODEL), F32)],
        compiler_params=_params(56, ("arbitrary",)),
    )(dq, dk, dv, dr, x2, g_norm)


def _memkv_bwd(dkv, mem2, g_mem, wkv):
    def body(dkv_ref, m_ref, g_ref, w_ref, dw_ref, dg_ref):
        mv = m_ref[...]
        r = _rstd(mv)
        dkv_v = dkv_ref[...].astype(BF16)
        dw_ref[...] = _dot_tn(mv * r * g_ref[...], dkv_v).astype(BF16)
        dg_ref[...] = jnp.sum(_dot_nt(dkv_v, w_ref[...]) * (mv * r), axis=0, keepdims=True)

    return pl.pallas_call(
        body, name="memkv_bwd", out_shape=(SDS((D_MODEL, 2 * MEM_W), BF16), SDS((1, D_MODEL), F32)),
        compiler_params=_params(32),
    )(dkv, mem2, g_mem, wkv)


def _reduce_scatter(parts):
    na = len(parts)
    shards = tuple(p.shape[0] // N_DEV for p in parts)
    widths = tuple(p.shape[1] for p in parts)
    step = 32

    def body(*refs):
        srcs, outs, scr = refs[0:na], refs[na:2 * na], refs[2 * na:]
        from_sib, pair_sum, from_chip = scr[0:na], scr[na:2 * na], scr[2 * na:3 * na]
        send_sems, recv_sems = scr[3 * na], scr[3 * na + 1]
        x, y, c = _mesh_pos()
        others = [(1 - x, y), (x, 1 - y), (1 - x, 1 - y)]

        def piece(a, p):
            n = shards[a]
            return srcs[a].at[pl.ds(pl.multiple_of(_flat(p) * n, 16), n), :]

        def to_sibling(a, q):
            chip = (q // 2, q % 2)
            return pltpu.make_async_remote_copy(
                src_ref=piece(a, (*chip, 1 - c)), dst_ref=from_sib[a].at[q],
                send_sem=send_sems.at[4 * a + q], recv_sem=recv_sems.at[4 * a + q],
                device_id=(x, y, 1 - c), device_id_type=MESH)

        def to_chip(a, j):
            return pltpu.make_async_remote_copy(
                src_ref=pair_sum[a].at[j], dst_ref=from_chip[a].at[j],
                send_sem=send_sems.at[4 * na + 3 * a + j], recv_sem=recv_sems.at[4 * na + 3 * a + j],
                device_id=(*others[j], c), device_id_type=MESH)

        def add_rows(a, terms, store):
            def one(i, carry):
                rows = pl.ds(pl.multiple_of(i * step, step), step)
                tot = terms[0][rows, :].astype(F32)
                for t in terms[1:]:
                    tot = tot + t[rows, :].astype(F32)
                store(rows, tot)
                return carry
            lax.fori_loop(0, shards[a] // step, one, 0)

        first = [to_sibling(a, q) for a in range(na) for q in range(4)]
        for cp in first:
            cp.start()
        second = []
        for a in range(na):
            for q in range(4):
                to_sibling(a, q).wait_recv()
            for j, chip in enumerate(others):
                def store(rows, tot, a=a, j=j):
                    pair_sum[a][j, rows, :] = tot.astype(BF16)
                add_rows(a, [piece(a, (*chip, c)), from_sib[a].at[2 * chip[0] + chip[1]]], store)
                cp = to_chip(a, j)
                cp.start()
                second.append(cp)
        for a in range(na):
            for j in range(3):
                to_chip(a, j).wait_recv()

            def store(rows, tot, a=a):
                outs[a][rows, :] = tot
            add_rows(a, [piece(a, (x, y, c)), from_sib[a].at[2 * x + y]] + [from_chip[a].at[j] for j in range(3)], store)
        for cp in first + second:
            cp.wait_send()

    vmem = pl.BlockSpec(memory_space=pltpu.VMEM)
    return pl.pallas_call(
        body, name="reduce_scatter",
        out_shape=tuple(SDS((n, w), F32) for n, w in zip(shards, widths)),
        in_specs=[vmem] * na, out_specs=(vmem,) * na,
        scratch_shapes=[pltpu.VMEM((4, n, w), BF16) for n, w in zip(shards, widths)]
        + [pltpu.VMEM((3, n, w), BF16) for n, w in zip(shards, widths)] * 2
        + [pltpu.SemaphoreType.DMA((7 * na,)), pltpu.SemaphoreType.DMA((7 * na,))],
        compiler_params=_params(40),
    )(*parts)


def _allreduce_small(parts):
    n = len(parts)

    def body(*refs):
        ins, outs, bufs = refs[0:n], refs[n:2 * n], refs[2 * n:3 * n]
        send_sems, recv_sems = refs[3 * n], refs[3 * n + 1]
        pos = _mesh_pos()
        me = _flat(pos)
        for a in range(n):
            bufs[a][me] = ins[a][...]

        def copy(a, k, slot):
            return pltpu.make_async_remote_copy(
                src_ref=ins[a], dst_ref=bufs[a].at[slot],
                send_sem=send_sems.at[7 * a + k - 1], recv_sem=recv_sems.at[7 * a + k - 1],
                device_id=_peer(pos, k), device_id_type=MESH)

        sent = [copy(a, k, me) for a in range(n) for k in range(1, N_DEV)]
        for cp in sent:
            cp.start()
        for a in range(n):
            for k in range(1, N_DEV):
                copy(a, k, _flat(_peer(pos, k))).wait_recv()
        for cp in sent:
            cp.wait_send()
        for a in range(n):
            acc = bufs[a][0]
            for s in range(1, N_DEV):
                acc = acc + bufs[a][s]
            outs[a][...] = acc

    vmem = pl.BlockSpec(memory_space=pltpu.VMEM)
    return pl.pallas_call(
        body, name="allreduce_small",
        out_shape=tuple(SDS(p.shape, F32) for p in parts),
        in_specs=[vmem] * n, out_specs=(vmem,) * n,
        scratch_shapes=[pltpu.VMEM((N_DEV,) + p.shape, F32) for p in parts]
        + [pltpu.SemaphoreType.DMA((7 * n,)), pltpu.SemaphoreType.DMA((7 * n,))],
        compiler_params=_params(16),
    )(*parts)


_HBM = pl.BlockSpec(memory_space=pltpu.HBM)
_SEM = pl.BlockSpec(memory_space=pltpu.SEMAPHORE)
_SIDE_EFFECT = pltpu.SideEffectType.DATAFLOW_SIDE_EFFECTING


def _exchange_copies(src_refs, land_refs, scatter, send_sems, recv_sems):
    pos = _mesh_pos()
    copies = []
    for a, (src, land) in enumerate(zip(src_refs, land_refs)):
        n = land.shape[1]
        for k in range(1, N_DEV):
            peer = _peer(pos, k)
            piece = src.at[pl.ds(pl.multiple_of(_flat(peer) * n, 16), n), :] if scatter[a] else src
            copies.append(pltpu.make_async_remote_copy(
                src_ref=piece, dst_ref=land.at[_flat(pos)],
                send_sem=send_sems.at[7 * a + k - 1], recv_sem=recv_sems.at[7 * a + k - 1],
                device_id=peer, device_id_type=MESH))
    return copies


def _exchange_start(name, srcs, scatter, lands):
    n = len(srcs)

    def body(*refs):
        for cp in _exchange_copies(refs[0:n], refs[n:2 * n], scatter, refs[2 * n], refs[2 * n + 1]):
            cp.start()
        refs[-1][...] = jnp.zeros_like(refs[-1])

    ops = [pltpu.with_memory_space_constraint(t, pltpu.HBM) for t in (*srcs, *lands)]
    out = pl.pallas_call(
        body, name=name,
        out_shape=(pltpu.SemaphoreType.DMA((7 * n,)), pltpu.SemaphoreType.DMA((7 * n,)),
                   *[pltpu.HBM(t.shape, t.dtype) for t in ops], SDS((8, 128), F32)),
        in_specs=[_HBM] * (2 * n),
        out_specs=(_SEM, _SEM, *[_HBM] * (2 * n), pl.BlockSpec(memory_space=pltpu.VMEM)),
        input_output_aliases={i: 2 + i for i in range(2 * n)},
        compiler_params=pltpu.CompilerParams(has_side_effects=_SIDE_EFFECT),
    )(*ops)
    return out[0], out[1], out[2:2 + n], out[2 + n:2 + 2 * n], out[-1]


def _exchange_wait(name, started, scatter, after):
    send_sems, recv_sems, srcs, lands, _ = started
    n = len(srcs)

    def body(*refs):
        for cp in _exchange_copies(refs[0:n], refs[n:2 * n], scatter, refs[2 * n], refs[2 * n + 1]):
            cp.wait_send()
            cp.wait_recv()

    out = pl.pallas_call(
        body, name=name,
        out_shape=tuple(pltpu.HBM(t.shape, t.dtype) for t in (*srcs, *lands)),
        in_specs=[_HBM] * (2 * n) + [_SEM, _SEM, pl.BlockSpec(memory_space=pl.ANY)],
        out_specs=(_HBM,) * (2 * n),
        input_output_aliases={i: i for i in range(2 * n)},
        compiler_params=pltpu.CompilerParams(has_side_effects=_SIDE_EFFECT),
    )(*srcs, *lands, send_sems, recv_sems, after)
    return out[n:]


def _landing(own, me):
    return lax.dynamic_update_slice(lax.empty((N_DEV,) + own.shape, own.dtype), own[None], (me,) + (0,) * own.ndim)


def _adamw(w, g, m, v):
    m = ADAM_B1 * m + (1.0 - ADAM_B1) * g
    v = ADAM_B2 * v + (1.0 - ADAM_B2) * (g * g)
    m_hat = m / (1.0 - ADAM_B1 ** ADAM_STEP)
    v_hat = v / (1.0 - ADAM_B2 ** ADAM_STEP)
    return -ADAM_LR * (m_hat / (jnp.sqrt(v_hat) + ADAM_EPS) + ADAM_WD * w), m, v


def _adam_shard(name, grad, w, m, v):
    def body(g_ref, w_ref, m_ref, v_ref, d_o, m_o, v_o):
        d_o[...], m_o[...], v_o[...] = _adamw(w_ref[...], g_ref[...], m_ref[...], v_ref[...])

    return pl.pallas_call(
        body, name="adam_" + name, out_shape=(SDS(w.shape, F32),) * 3, compiler_params=_params(40),
    )(grad, w, m, v)


def _adam_slots(name, slots, w, m, v):
    _, rows, cols = slots.shape

    def body(s_ref, w_ref, m_ref, v_ref, g_o, d_o, m_o, v_o, acc):
        s = pl.program_id(0)

        @pl.when(s == 0)
        def _():
            acc[...] = s_ref[0].astype(F32)

        @pl.when(s > 0)
        def _():
            acc[...] += s_ref[0].astype(F32)

        @pl.when(s == N_DEV - 1)
        def _():
            g = acc[...]
            g_o[...] = g
            d_o[...], m_o[...], v_o[...] = _adamw(w_ref[...], g, m_ref[...], v_ref[...])

    full = pl.BlockSpec((rows, cols), lambda s: (0, 0))
    return pl.pallas_call(
        body, name="adam_" + name, grid=(N_DEV,),
        in_specs=[pl.BlockSpec((1, rows, cols), lambda s: (s, 0, 0)), full, full, full],
        out_specs=(full,) * 4, out_shape=(SDS((rows, cols), F32),) * 4,
        scratch_shapes=[pltpu.VMEM((rows, cols), F32)],
        compiler_params=_params(40, ("arbitrary",)),
    )(slots, w, m, v)


def _adam_small(ws, gs, ms, vs, loss_slots):
    n = len(ws)

    def total(ref, like):
        if len(ref.shape) == len(like.shape):
            return ref[...]
        acc = ref[0]
        for s in range(1, N_DEV):
            acc = acc + ref[s]
        return acc

    def body(*refs):
        w_r, g_r, m_r, v_r = refs[0:n], refs[n:2 * n], refs[2 * n:3 * n], refs[3 * n:4 * n]
        loss_r, outs = refs[4 * n], refs[4 * n + 1:]
        for a in range(n):
            g = total(g_r[a], w_r[a])
            outs[a][...] = g
            outs[n + 1 + 3 * a][...], outs[n + 2 + 3 * a][...], outs[n + 3 + 3 * a][...] = _adamw(
                w_r[a][...], g, m_r[a][...], v_r[a][...])
        outs[n][...] = total(loss_r, outs[n])

    out = pl.pallas_call(
        body, name="adam_small",
        out_shape=tuple(SDS(w.shape, F32) for w in ws) + (SDS(loss_slots.shape[1:], F32),)
        + tuple(SDS(w.shape, F32) for w in ws for _ in range(3)),
        compiler_params=_params(16),
    )(*ws, *gs, *ms, *vs, loss_slots)
    return out[0:n], out[n], out[n + 1:]


def kernel(x, mem, g_norm, w_in, w_sgu_spatial, b_sgu_spatial, g_sgu_v, g_mem, w_mem_kv, w_out, g_final, loss_target, m_g_norm, m_w_in, m_w_sgu_spatial, m_b_sgu_spatial, m_g_sgu_v, m_g_mem, m_w_mem_kv, m_w_out, m_g_final, v_g_norm, v_w_in, v_w_sgu_spatial, v_b_sgu_spatial, v_g_sgu_v, v_g_mem, v_w_mem_kv, v_w_out, v_g_final):
    x2 = x.reshape(T_LOC, D_MODEL)
    tgt2 = loss_target.reshape(T_LOC, D_MODEL)
    mem2 = mem.reshape(B_LOC * N_MEM, D_MODEL)
    w_s = w_sgu_spatial[0]
    b_exp = jnp.tile(jnp.repeat(b_sgu_spatial[0].T, HEAD, axis=1), (2, 1))
    slope = jnp.power(2.0, -8.0 * (jnp.arange(8, dtype=F32) + 1.0) / 8)
    slopes = jnp.broadcast_to(jnp.repeat(slope.reshape(4, 2), HEAD, axis=1)[:, None, :], (4, 8, 128))

    tr = lambda t: jnp.transpose(t[0])
    hbm = lambda t: pltpu.with_memory_space_constraint(t, pltpu.HBM)

    wint, wkv, wout = _allgather_weights(tr(w_in), w_mem_kv[0], w_out[0])
    proj = hbm(_proj_fwd(x2, g_norm, wint))
    kv = _memkv_fwd(mem2, g_mem, wkv)
    a, lse = map(hbm, _attn_fwd(proj, slopes))
    gated = hbm(_branch_fwd(proj, a, kv, w_s, b_exp, g_sgu_v))
    dh2, loss8, dgf = _outproj_loss(gated, wout, x2, tgt2, g_final.reshape(1, D_MODEL))
    dh2 = hbm(dh2)

    da, dr, dkv, dwout, dws, dbs, dgv = _branch_bwd(dh2, wout, gated, proj, a, kv, w_s, b_exp, g_sgu_v)
    da, dr = hbm(da), hbm(dr)
    dwkv, dgm = _memkv_bwd(dkv, mem2, g_mem, wkv)

    me = _flat(_mesh_pos())
    own_rows = lambda t: lax.dynamic_slice_in_dim(t, me * (t.shape[0] // N_DEV), t.shape[0] // N_DEV)
    early = [dws.reshape(4 * CHUNK, CHUNK), dbs, dgv, dgm, dgf, loss8]
    scatter1 = [True, True] + [False] * len(early)
    started1 = _exchange_start(
        "exchange1_start", [dwkv, dwout] + early, scatter1,
        [_landing(own_rows(dwkv), me), _landing(own_rows(dwout), me)] + [_landing(t, me) for t in early])
    dq, dk, dv = map(hbm, _attn_bwd(proj, slopes + started1[4][0:1, 0:1], da, a, lse))
    s_wkv, s_wout, s_ws, s_bs, s_gv, s_gm, s_gf, s_loss = _exchange_wait("exchange1_wait", started1, scatter1, dq)

    dwint = _dwin(dq, dk, dv, dr, x2, g_norm)
    g_win_t, = _reduce_scatter([dwint])
    grad_x, dgn = _dx(dq, dk, dv, dr, wint, x2, dh2, g_norm)
    dgn_sum, = _allreduce_small([dgn])

    d_win, m_win, v_win = map(jnp.transpose, _adam_shard("w_in", g_win_t, tr(w_in), tr(m_w_in), tr(v_w_in)))
    g_win = jnp.transpose(g_win_t)
    g_wkv, d_wkv, m_wkv, v_wkv = _adam_slots("w_mem_kv", s_wkv, w_mem_kv[0], m_w_mem_kv[0], v_w_mem_kv[0])
    g_wout, d_wout, m_wout, v_wout = _adam_slots("w_out", s_wout, w_out[0], m_w_out[0], v_w_out[0])

    small_shapes = [(1, D_MODEL), (4 * CHUNK, CHUNK), (4, CHUNK), (1, SGU_W), (1, D_MODEL), (1, D_MODEL)]
    pack = lambda arrs: [t.reshape(s) for t, s in zip(arrs, small_shapes)]
    g_small, loss_sum, upd = _adam_small(
        pack([g_norm, w_sgu_spatial, b_sgu_spatial, g_sgu_v, g_mem, g_final]),
        [dgn_sum, s_ws, s_bs, s_gv, s_gm, s_gf],
        pack([m_g_norm, m_w_sgu_spatial, m_b_sgu_spatial, m_g_sgu_v, m_g_mem, m_g_final]),
        pack([v_g_norm, v_w_sgu_spatial, v_b_sgu_spatial, v_g_sgu_v, v_g_mem, v_g_final]), s_loss)
    out_shapes = [g_norm.shape, w_sgu_spatial.shape, b_sgu_spatial.shape, g_sgu_v.shape, g_mem.shape, g_final.shape]
    unpack = lambda arrs: [t.reshape(s) for t, s in zip(arrs, out_shapes)]
    gs = unpack(g_small)
    ds, nms, nvs = unpack(upd[0::3]), unpack(upd[1::3]), unpack(upd[2::3])

    loss = loss_sum[0, 0]

    def assemble(small, win, wkv_, wout_):
        return [small[0], win[None], small[1], small[2], small[3], small[4], wkv_[None], wout_[None], small[5]]

    return (loss, grad_x.reshape(x.shape),
            *assemble(gs, g_win, g_wkv, g_wout), *assemble(ds, d_win, d_wkv, d_wout),
            *assemble(nms, m_win, m_wkv, m_wout), *assemble(nvs, v_win, v_wkv, v_wout))
```

```python
import functools

import jax
import jax.numpy as jnp
from jax import lax
from jax.experimental import pallas as pl
from jax.experimental.pallas import tpu as pltpu

F32 = jnp.float32
BF16 = jnp.bfloat16
SDS = jax.ShapeDtypeStruct
MESH = pl.DeviceIdType.MESH

N_DEV = 8
D_MODEL = 1024
SEQ = 2048
B_LOC = 2
T_LOC = B_LOC * SEQ
N_MEM = 256
HEAD = 64
ATTN_W = 512
SGU_W = 256
MEM_W = 256
IN_COLS = 3328
W_IN_SHARD = IN_COLS // N_DEV
ROW_SHARD = D_MODEL // N_DEV
CHUNK = 128
DILATIONS = ((1, 2048), (4, 512), (16, 128))
RADIUS = 64
EPS = 1e-6
NEG = -1e30
SCALE = HEAD ** -0.5
C_QA, C_KA, C_VA, C_ZA, C_UB, C_VB, C_ZB, C_QM, C_ZM = 0, 512, 1024, 1536, 2048, 2304, 2560, 2816, 3072
QKV_W = 1536
REST_W = IN_COLS - QKV_W

ADAM_LR, ADAM_B1, ADAM_B2, ADAM_EPS, ADAM_WD, ADAM_STEP = 0.001, 0.9, 0.999, 1e-08, 0.01, 10

V7X_VMEM_MIB = 64


def _params(vmem_mib, sem=None, **kw):
    assert vmem_mib < V7X_VMEM_MIB
    return pltpu.CompilerParams(vmem_limit_bytes=vmem_mib << 20, dimension_semantics=sem, **kw)


def _dot(a, b):
    return jnp.dot(a.astype(BF16), b.astype(BF16), preferred_element_type=F32)


def _dot_nt(a, b):
    return lax.dot_general(a.astype(BF16), b.astype(BF16), (((1,), (1,)), ((), ())), preferred_element_type=F32)


def _dot_tn(a, b):
    return lax.dot_general(a.astype(BF16), b.astype(BF16), (((0,), (0,)), ((), ())), preferred_element_type=F32)


def _rstd(v):
    return lax.rsqrt(jnp.mean(v * v, axis=-1, keepdims=True) + EPS)


def _rms_bwd(v, r, g, dy):
    gdy = g * dy
    return r * gdy - v * (r * r * r * jnp.mean(gdy * v, axis=-1, keepdims=True))


def _sigmoid(z):
    return 1.0 / (1.0 + jnp.exp(-z))


def _silu_and_grad(z):
    s = _sigmoid(z)
    return z * s, s * (1.0 + z * (1.0 - s))


_G_C = 0.7978845608028654
_G_K = 0.044715


def _gelu_and_grad(v):
    t = jnp.tanh(_G_C * (v + _G_K * (v * v * v)))
    cdf = 0.5 * (1.0 + t)
    return v * cdf, cdf + 0.5 * v * (1.0 - t * t) * (_G_C * (1.0 + 3.0 * _G_K * v * v))


def _cast_rows(src_ref, dst_ref, rows, step=256):
    def one(i, carry):
        r = pl.ds(pl.multiple_of(i * step, step), step)
        dst_ref[r, :] = src_ref[r, :].astype(dst_ref.dtype)
        return carry
    lax.fori_loop(0, rows // step, one, 0)


def _left_lanes(rows):
    return lax.broadcasted_iota(jnp.int32, (rows, 128), 1) < HEAD


def _mesh_pos():
    return lax.axis_index("x"), lax.axis_index("y"), lax.axis_index("c")


def _peer(pos, k):
    x, y, c = pos
    return (1 - x if k & 4 else x, 1 - y if k & 2 else y, 1 - c if k & 1 else c)


def _flat(pos):
    return 4 * pos[0] + 2 * pos[1] + pos[2]


def _allgather_weights(w_in_t, w_kv, w_out):
    def body(win_ref, wkv_ref, wout_ref, wint_o, wkv_o, wout_o, send_sems, recv_sems):
        x, y, c = _mesh_pos()
        me, sib = (x, y, c), (x, y, 1 - c)
        chips = [(1 - x, y), (x, 1 - y), (1 - x, 1 - y)]

        def rows(p):
            return wint_o.at[pl.ds(pl.multiple_of(_flat(p) * W_IN_SHARD, 16), W_IN_SHARD), :]

        rows(me)[...] = win_ref[...].astype(BF16)

        def copy(k, block, to):
            return pltpu.make_async_remote_copy(
                src_ref=rows(block), dst_ref=rows(block), send_sem=send_sems.at[k], recv_sem=recv_sems.at[k],
                device_id=to, device_id_type=MESH)

        first = [copy(0, me, sib)] + [copy(1 + j, me, (*chip, c)) for j, chip in enumerate(chips)]
        for cp in first:
            cp.start()
        wkv_o[...] = wkv_ref[...].astype(BF16)
        wout_o[...] = wout_ref[...].astype(BF16)
        passed = []
        for j, chip in enumerate(chips):
            copy(1 + j, (*chip, c), me).wait_recv()
            fwd = copy(4 + j, (*chip, c), sib)
            fwd.start()
            passed.append(fwd)
        copy(0, sib, me).wait_recv()
        for j, chip in enumerate(chips):
            copy(4 + j, (*chip, 1 - c), me).wait_recv()
        for cp in first + passed:
            cp.wait_send()

    vmem = pl.BlockSpec(memory_space=pltpu.VMEM)
    return pl.pallas_call(
        body, name="allgather_weights",
        out_shape=(SDS((IN_COLS, D_MODEL), BF16), SDS(w_kv.shape, BF16), SDS(w_out.shape, BF16)),
        in_specs=[vmem, vmem, vmem], out_specs=(vmem, vmem, vmem),
        scratch_shapes=[pltpu.SemaphoreType.DMA((7,)), pltpu.SemaphoreType.DMA((7,))],
        compiler_params=_params(40),
    )(w_in_t, w_kv, w_out)


def _proj_fwd(x2, g_norm, wint):
    tm = 256

    def body(x_ref, g_ref, w_ref, o_ref):
        xv = x_ref[...]
        h = xv * _rstd(xv) * g_ref[...]
        o_ref[...] = _dot_nt(h, w_ref[...])

    return pl.pallas_call(
        body, name="proj_fwd", grid=(T_LOC // tm,),
        in_specs=[pl.BlockSpec((tm, D_MODEL), lambda i: (i, 0)), pl.BlockSpec((1, D_MODEL), lambda i: (0, 0)),
                  pl.BlockSpec((IN_COLS, D_MODEL), lambda i: (0, 0))],
        out_specs=pl.BlockSpec((tm, IN_COLS), lambda i: (i, 0)),
        out_shape=SDS((T_LOC, IN_COLS), F32),
        compiler_params=_params(48, ("arbitrary",)),
    )(x2, g_norm, wint)


def _memkv_fwd(mem2, g_mem, wkv):
    def body(m_ref, g_ref, w_ref, o_ref):
        mv = m_ref[...]
        o_ref[...] = _dot(mv * _rstd(mv) * g_ref[...], w_ref[...])

    return pl.pallas_call(
        body, name="memkv_fwd", out_shape=SDS((B_LOC * N_MEM, 2 * MEM_W), F32), compiler_params=_params(32),
    )(mem2, g_mem, wkv)


N_BIAS = 7


def _fill_bias_tables(sl_ref, tab):
    for cfg, (d, length) in enumerate(DILATIONS):
        nk = min(length, 2 * CHUNK)
        r = lax.broadcasted_iota(jnp.int32, (CHUNK, nk), 0)
        c = lax.broadcasted_iota(jnp.int32, (CHUNK, nk), 1)
        for var in range(3 if length > nk else 1):
            rel = jnp.abs(r - c + var * RADIUS)
            dist = rel.astype(F32) * float(d)
            for h in range(2):
                slope = sl_ref[0, 0:1, h * HEAD:h * HEAD + 1]
                tab[3 * cfg + var, h * CHUNK:(h + 1) * CHUNK, 0:nk] = jnp.where(rel <= RADIUS, -slope * dist, NEG)


def _attn_blocks(visit, unroll):
    def step(t, carry):
        for cfg, (d, length) in enumerate(DILATIONS):
            nblk = length // CHUNK
            if nblk == 1:
                visit(cfg, 0, t, t, length)
                continue
            rho, i = (0, t) if d == 1 else (t // nblk, t % nblk)
            ks = jnp.clip(i * CHUNK - RADIUS, 0, length - 2 * CHUNK)
            visit(cfg, (i * CHUNK - ks) // RADIUS, rho + d * (i * CHUNK), rho + d * ks, 2 * CHUNK)
        return carry
    lax.fori_loop(0, 16, step, 0, unroll=unroll)


def _stack_heads(v, left):
    return jnp.concatenate([jnp.where(left, v, 0.0), jnp.where(left, 0.0, v)], axis=0)


def _unstack_heads(v, left):
    return jnp.where(left, v[0:CHUNK], v[CHUNK:2 * CHUNK])


def _rows(start, n, d):
    return pl.ds(start, n) if d == 1 else pl.ds(start, n, stride=d)


def _attn_fwd(proj, slopes):
    def body(sl_ref, q_ref, k_ref, v_ref, a_ref, lse_ref, *scr):
        o_c, m_c, l_c, tab = scr[0:3], scr[3:6], scr[6:9], scr[9]
        left = _left_lanes(CHUNK)
        _fill_bias_tables(sl_ref, tab)

        def block(cfg, var, q0, k0, nk):
            d = DILATIONS[cfg][0]
            rq, rk = _rows(q0, CHUNK, d), _rows(k0, nk, d)
            qs = _stack_heads(q_ref[rq, :] * SCALE, left)
            s = _dot_nt(qs, k_ref[rk, :]) + tab[3 * cfg + var, :, 0:nk]
            m = jnp.max(s, axis=-1, keepdims=True)
            p = jnp.exp(s - m)
            o_c[cfg][rq, :] = _unstack_heads(_dot(p, v_ref[rk, :]), left)
            m_c[cfg][rq, :] = _unstack_heads(m, left)
            l_c[cfg][rq, :] = _unstack_heads(jnp.sum(p, axis=-1, keepdims=True), left)
        _attn_blocks(block, 4)

        def merge(j, carry):
            rows = pl.ds(pl.multiple_of(j * 256, 256), 256)
            ms = [m_c[i][rows, :] for i in range(3)]
            top = jnp.maximum(jnp.maximum(ms[0], ms[1]), ms[2])
            ws = [jnp.exp(m - top) for m in ms]
            den = l_c[0][rows, :] * ws[0] + l_c[1][rows, :] * ws[1] + l_c[2][rows, :] * ws[2]
            num = o_c[0][rows, :] * ws[0] + o_c[1][rows, :] * ws[1] + o_c[2][rows, :] * ws[2]
            a_ref[rows, :] = num / den
            lse_ref[rows, :] = top + jnp.log(den)
            return carry
        lax.fori_loop(0, SEQ // 256, merge, 0)

    blk = lambda col0: pl.BlockSpec((SEQ, 128), lambda b, hp: (b, col0 // 128 + hp))
    out = pl.BlockSpec((SEQ, 128), lambda b, hp: (b, hp))
    return pl.pallas_call(
        body, name="attn_fwd", grid=(B_LOC, 4),
        in_specs=[pl.BlockSpec((1, 8, 128), lambda b, hp: (hp, 0, 0)), blk(C_QA), blk(C_KA), blk(C_VA)],
        out_specs=(out, out),
        out_shape=(SDS((T_LOC, ATTN_W), F32), SDS((T_LOC, ATTN_W), F32)),
        scratch_shapes=[pltpu.VMEM((SEQ, 128), F32)] * 9 + [pltpu.VMEM((N_BIAS, 2 * CHUNK, 2 * CHUNK), F32)],
        compiler_params=_params(40, ("arbitrary", "arbitrary")),
    )(slopes, proj, proj, proj)


def _sgu_mix(vn, ws_ref, dst_ref, tm):
    left = _left_lanes(CHUNK)
    for ch in range(tm // CHUNK):
        for pr in range(2):
            vp = vn[ch * CHUNK:(ch + 1) * CHUNK, pr * 128:(pr + 1) * 128]
            dst_ref[ch * CHUNK:(ch + 1) * CHUNK, pr * 128:(pr + 1) * 128] = jnp.where(
                left, _dot(ws_ref[2 * pr], vp), _dot(ws_ref[2 * pr + 1], vp))


def _mem_attn_head(qp, kp, h, left):
    qh = jnp.where(left if h == 0 else ~left, qp, 0.0)
    s = _dot_nt(qh, kp) * SCALE
    e = jnp.exp(s - jnp.max(s, axis=-1, keepdims=True))
    return e * (1.0 / jnp.sum(e, axis=-1, keepdims=True)), qh


def _branch_blocks(tm):
    col = lambda w, c0: pl.BlockSpec((tm, w), lambda i: (i, c0 // w))
    return [col(512, C_ZA), col(256, C_UB), col(256, C_VB), col(256, C_ZB), col(256, C_QM), col(256, C_ZM)]


def _branch_fwd(proj, a, kv, w_s, b_exp, g_v):
    tm = 256
    per_ex = SEQ // tm

    def body(za_ref, ub_ref, vb_ref, zb_ref, qm_ref, zm_ref, a_ref, kv_ref, ws_ref, be_ref, gv_ref, o_ref, mix):
        left = _left_lanes(tm)
        o_ref[:, 0:ATTN_W] = (_silu_and_grad(za_ref[...])[0] * a_ref[...]).astype(BF16)
        gu = _gelu_and_grad(ub_ref[...])[0]
        gv = _gelu_and_grad(vb_ref[...])[0]
        vn = gv * _rstd(gv) * gv_ref[...]
        _sgu_mix(vn.astype(BF16), ws_ref, mix, tm)
        sg = gu * (mix[...] + be_ref[...])
        o_ref[:, ATTN_W:ATTN_W + SGU_W] = (_silu_and_grad(zb_ref[...])[0] * sg).astype(BF16)
        szm = _silu_and_grad(zm_ref[...])[0]
        for hp in range(2):
            cols = slice(hp * 128, (hp + 1) * 128)
            qp, kp, vp = qm_ref[:, cols], kv_ref[:, cols], kv_ref[:, MEM_W + hp * 128:MEM_W + (hp + 1) * 128]
            o = [_dot(_mem_attn_head(qp, kp, h, left)[0], vp) for h in range(2)]
            c0 = ATTN_W + SGU_W + hp * 128
            o_ref[:, c0:c0 + 128] = (szm[:, cols] * jnp.where(left, o[0], o[1])).astype(BF16)

    full = lambda shape: pl.BlockSpec(shape, lambda i: (0,) * len(shape))
    return pl.pallas_call(
        body, name="branch_fwd", grid=(T_LOC // tm,),
        in_specs=_branch_blocks(tm) + [
            pl.BlockSpec((tm, ATTN_W), lambda i: (i, 0)), pl.BlockSpec((N_MEM, 2 * MEM_W), lambda i: (i // per_ex, 0)),
            full((4, CHUNK, CHUNK)), full((tm, SGU_W)), full((1, SGU_W))],
        out_specs=pl.BlockSpec((tm, D_MODEL), lambda i: (i, 0)),
        out_shape=SDS((T_LOC, D_MODEL), BF16),
        scratch_shapes=[pltpu.VMEM((tm, SGU_W), F32)],
        compiler_params=_params(40, ("arbitrary",)),
    )(proj, proj, proj, proj, proj, proj, a, kv, w_s, b_exp, g_v)


def _outproj_loss(gated, wout, x2, tgt2, g_final):
    tm = 512

    def body(g_ref, w_ref, x_ref, t_ref, gf_ref, dh2_ref, loss_ref, dgf_ref):
        @pl.when(pl.program_id(0) == 0)
        def _():
            loss_ref[...] = jnp.zeros_like(loss_ref)
            dgf_ref[...] = jnp.zeros_like(dgf_ref)
        h2 = x_ref[...] + _dot(g_ref[...], w_ref[...])
        r = _rstd(h2)
        gf = gf_ref[...]
        err = h2 * r * gf - t_ref[...]
        loss_ref[...] += 0.5 * jnp.sum(jnp.mean(err * err, axis=-1, keepdims=True))
        dy = err * (1.0 / D_MODEL)
        dh2_ref[...] = _rms_bwd(h2, r, gf, dy)
        dgf_ref[...] += jnp.sum(dy * (h2 * r), axis=0, keepdims=True)

    row = pl.BlockSpec((tm, D_MODEL), lambda i: (i, 0))
    vec = pl.BlockSpec((1, D_MODEL), lambda i: (0, 0))
    return pl.pallas_call(
        body, name="outproj_loss", grid=(T_LOC // tm,),
        in_specs=[row, pl.BlockSpec((D_MODEL, D_MODEL), lambda i: (0, 0)), row, row, vec],
        out_specs=(row, pl.BlockSpec((8, 128), lambda i: (0, 0)), vec),
        out_shape=(SDS((T_LOC, D_MODEL), F32), SDS((8, 128), F32), SDS((1, D_MODEL), F32)),
        compiler_params=_params(40, ("arbitrary",)),
    )(gated, wout, x2, tgt2, g_final)


def _branch_bwd(dh2, wout, gated, proj, a, kv, w_s, b_exp, g_v):
    tm = 256
    per_ex = SEQ // tm

    def body(dh2_ref, w_ref, g_ref, za_ref, ub_ref, vb_ref, zb_ref, qm_ref, zm_ref, a_ref, kv_ref, ws_ref,
             be_ref, gv_ref, da_ref, dr_ref, dkv_ref, dwo_ref, dws_ref, db_ref, dgv_ref, mix, dvn, dmsum, dwo_acc):
        i = pl.program_id(0)
        left = _left_lanes(tm)
        leftc = _left_lanes(CHUNK)

        @pl.when(i == 0)
        def _():
            dwo_acc[...] = jnp.zeros_like(dwo_acc)
            dws_ref[...] = jnp.zeros_like(dws_ref)
            dgv_ref[...] = jnp.zeros_like(dgv_ref)
            dmsum[...] = jnp.zeros_like(dmsum)

        @pl.when(i % per_ex == 0)
        def _():
            dkv_ref[...] = jnp.zeros_like(dkv_ref)

        dh2 = dh2_ref[...].astype(BF16)
        dwo_acc[...] += _dot_tn(g_ref[...], dh2)
        dg = _dot_nt(dh2, w_ref[...])

        sa, dsa = _silu_and_grad(za_ref[...])
        dga = dg[:, 0:ATTN_W]
        da_ref[...] = dga * sa
        dr_ref[:, 0:512] = (dga * a_ref[...] * dsa).astype(BF16)

        ub, vb = ub_ref[...], vb_ref[...]
        gu, dgu = _gelu_and_grad(ub)
        gv, dgv = _gelu_and_grad(vb)
        rv = _rstd(gv)
        gain = gv_ref[...]
        vn = (gv * rv * gain).astype(BF16)
        _sgu_mix(vn, ws_ref, mix, tm)
        mixed = mix[...] + be_ref[...]
        sb, dsb = _silu_and_grad(zb_ref[...])
        dgb = dg[:, ATTN_W:ATTN_W + SGU_W]
        dsg = dgb * sb
        dr_ref[:, 512:768] = (dsg * mixed * dgu).astype(BF16)
        dr_ref[:, 1024:1280] = (dgb * (gu * mixed) * dsb).astype(BF16)
        dmix = dsg * gu
        for ch in range(tm // CHUNK):
            rows = slice(ch * CHUNK, (ch + 1) * CHUNK)
            dmsum[...] += dmix[rows, :]
            for pr in range(2):
                cols = slice(pr * 128, (pr + 1) * 128)
                dmp, vp = dmix[rows, cols], vn[rows, cols]
                for h in range(2):
                    g = 2 * pr + h
                    dws_ref[g] += _dot_nt(jnp.where(leftc if h == 0 else ~leftc, dmp, 0.0), vp)
                dvn[rows, cols] = jnp.where(leftc, _dot_tn(ws_ref[2 * pr], dmp), _dot_tn(ws_ref[2 * pr + 1], dmp))
        dvn_v = dvn[...]
        dgv_ref[...] += jnp.sum(dvn_v * (gv * rv), axis=0, keepdims=True)
        dr_ref[:, 768:1024] = (_rms_bwd(gv, rv, gain, dvn_v) * dgv).astype(BF16)

        szm, dszm = _silu_and_grad(zm_ref[...])
        dgm = dg[:, ATTN_W + SGU_W:D_MODEL]
        dmo = dgm * szm
        for hp in range(2):
            cols = slice(hp * 128, (hp + 1) * 128)
            vcols = slice(MEM_W + hp * 128, MEM_W + (hp + 1) * 128)
            qp, kp, vp, dmop = qm_ref[:, cols], kv_ref[:, cols], kv_ref[:, vcols], dmo[:, cols]
            o, dq = [], []
            dk = jnp.zeros((N_MEM, 128), F32)
            dv = jnp.zeros((N_MEM, 128), F32)
            for h in range(2):
                p, qh = _mem_attn_head(qp, kp, h, left)
                dmoh = jnp.where(left if h == 0 else ~left, dmop, 0.0)
                o.append(_dot(p, vp))
                dp = _dot_nt(dmoh, vp)
                ds = p * (dp - jnp.sum(p * dp, axis=-1, keepdims=True)) * SCALE
                dq.append(_dot(ds, kp))
                dk += _dot_tn(ds, qh)
                dv += _dot_tn(p, dmoh)
            dr_ref[:, 1280 + hp * 128:1280 + (hp + 1) * 128] = jnp.where(left, dq[0], dq[1]).astype(BF16)
            dr_ref[:, 1536 + hp * 128:1536 + (hp + 1) * 128] = (
                dgm[:, cols] * jnp.where(left, o[0], o[1]) * dszm[:, cols]).astype(BF16)
            dkv_ref[:, cols] += dk
            dkv_ref[:, vcols] += dv

        @pl.when(i == pl.num_programs(0) - 1)
        def _():
            tot = dmsum[...]
            hi = tot.astype(BF16)
            lo = (tot - hi.astype(F32)).astype(BF16)
            grp = (lax.broadcasted_iota(jnp.int32, (SGU_W, 128), 0) // HEAD
                   == lax.broadcasted_iota(jnp.int32, (SGU_W, 128), 1)).astype(BF16)
            db_ref[...] = (_dot(hi, grp) + _dot(lo, grp)).T[0:4, :]
            _cast_rows(dwo_acc, dwo_ref, D_MODEL)

    full = lambda shape: pl.BlockSpec(shape, lambda i: (0,) * len(shape))
    row = lambda w: pl.BlockSpec((tm, w), lambda i: (i, 0))
    return pl.pallas_call(
        body, name="branch_bwd", grid=(T_LOC // tm,),
        in_specs=[row(D_MODEL), full((D_MODEL, D_MODEL)), row(D_MODEL)] + _branch_blocks(tm) + [
            row(ATTN_W), pl.BlockSpec((N_MEM, 2 * MEM_W), lambda i: (i // per_ex, 0)),
            full((4, CHUNK, CHUNK)), full((tm, SGU_W)), full((1, SGU_W))],
        out_specs=(row(ATTN_W), row(REST_W), pl.BlockSpec((N_MEM, 2 * MEM_W), lambda i: (i // per_ex, 0)),
                   full((D_MODEL, D_MODEL)), full((4, CHUNK, CHUNK)), full((4, CHUNK)), full((1, SGU_W))),
        out_shape=(SDS((T_LOC, ATTN_W), F32), SDS((T_LOC, REST_W), BF16), SDS((B_LOC * N_MEM, 2 * MEM_W), F32),
                   SDS((D_MODEL, D_MODEL), BF16), SDS((4, CHUNK, CHUNK), F32), SDS((4, CHUNK), F32), SDS((1, SGU_W), F32)),
        scratch_shapes=[pltpu.VMEM((tm, SGU_W), F32), pltpu.VMEM((tm, SGU_W), F32), pltpu.VMEM((CHUNK, SGU_W), F32),
                        pltpu.VMEM((D_MODEL, D_MODEL), F32)],
        compiler_params=_params(56, ("arbitrary",)),
    )(dh2, wout, gated, proj, proj, proj, proj, proj, proj, a, kv, w_s, b_exp, g_v)


def _attn_bwd(proj, slopes, da, a, lse):
    def body(sl_ref, q_ref, k_ref, v_ref, da_ref, a_ref, lse_ref, dq_ref, dk_ref, dv_ref, *scr):
        dq_s, dk_s, dv_s, tab = scr[0:3], scr[3:6], scr[6:9], scr[9]
        lse_h, delta_h = scr[10:12], scr[12:14]
        left = _left_lanes(CHUNK)
        _fill_bias_tables(sl_ref, tab)

        def prep(j, carry):
            rows = pl.ds(pl.multiple_of(j * 256, 256), 256)
            l256 = _left_lanes(256)
            prod = da_ref[rows, :] * a_ref[rows, :]
            delta_h[0][rows, :] = jnp.broadcast_to(jnp.sum(jnp.where(l256, prod, 0.0), axis=-1, keepdims=True), (256, 128))
            delta_h[1][rows, :] = jnp.broadcast_to(jnp.sum(jnp.where(l256, 0.0, prod), axis=-1, keepdims=True), (256, 128))
            pair = lse_ref[rows, :]
            other = pltpu.roll(pair, HEAD, axis=1)
            lse_h[0][rows, :] = jnp.where(l256, pair, other)
            lse_h[1][rows, :] = jnp.where(l256, other, pair)
            zero = jnp.zeros((256, 128), F32)
            for cfg in range(3):
                dk_s[cfg][rows, :] = zero
                dv_s[cfg][rows, :] = zero
            return carry
        lax.fori_loop(0, SEQ // 256, prep, 0)

        def per_row(halves, rq, nk):
            v = jnp.concatenate([halves[0][rq, :], halves[1][rq, :]], axis=0)
            return v if nk == 128 else jnp.concatenate([v, v], axis=1)

        def block(cfg, var, q0, k0, nk):
            d = DILATIONS[cfg][0]
            rq, rk = _rows(q0, CHUNK, d), _rows(k0, nk, d)
            qs = _stack_heads(q_ref[rq, :] * SCALE, left).astype(BF16)
            das = _stack_heads(da_ref[rq, :], left).astype(BF16)
            kw = k_ref[rk, :].astype(BF16)
            vw = v_ref[rk, :].astype(BF16)
            s = _dot_nt(qs, kw) + tab[3 * cfg + var, :, 0:nk]
            p = jnp.exp(s - per_row(lse_h, rq, nk))
            ds = (p * (_dot_nt(das, vw) - per_row(delta_h, rq, nk))).astype(BF16)
            dq_s[cfg][rq, :] = _unstack_heads(_dot(ds, kw), left) * SCALE
            dk_s[cfg][rk, :] += _dot_tn(ds, qs)
            dv_s[cfg][rk, :] += _dot_tn(p, das)
        _attn_blocks(block, 2)

        def flush(j, carry):
            rows = pl.ds(pl.multiple_of(j * 256, 256), 256)
            for acc, dst in ((dq_s, dq_ref), (dk_s, dk_ref), (dv_s, dv_ref)):
                dst[rows, :] = (acc[0][rows, :] + acc[1][rows, :] + acc[2][rows, :]).astype(BF16)
            return carry
        lax.fori_loop(0, SEQ // 256, flush, 0)

    blk = lambda col0: pl.BlockSpec((SEQ, 128), lambda b, hp: (b, col0 // 128 + hp))
    own = pl.BlockSpec((SEQ, 128), lambda b, hp: (b, hp))
    return pl.pallas_call(
        body, name="attn_bwd", grid=(B_LOC, 4),
        in_specs=[pl.BlockSpec((1, 8, 128), lambda b, hp: (hp, 0, 0)), blk(C_QA), blk(C_KA), blk(C_VA), own, own, own],
        out_specs=(own, own, own),
        out_shape=(SDS((T_LOC, ATTN_W), BF16),) * 3,
        scratch_shapes=[pltpu.VMEM((SEQ, 128), F32)] * 9 + [pltpu.VMEM((N_BIAS, 2 * CHUNK, 2 * CHUNK), F32)]
        + [pltpu.VMEM((SEQ, 128), F32)] * 4,
        compiler_params=_params(40, ("arbitrary", "arbitrary")),
    )(slopes, proj, proj, proj, da, a, lse)


def _dproj_specs(tm):
    third = pl.BlockSpec((tm, ATTN_W), lambda i: (i, 0))
    return [third, third, third, pl.BlockSpec((tm, REST_W), lambda i: (i, 0))]


def _dx(dq, dk, dv, dr, wint, x2, dh2, g_norm):
    tm = 256

    def body(dq_ref, dk_ref, dv_ref, dr_ref, w_ref, x_ref, dh2_ref, g_ref, gx_ref, dgn_ref):
        @pl.when(pl.program_id(0) == 0)
        def _():
            dgn_ref[...] = jnp.zeros_like(dgn_ref)
        dh = (_dot(dq_ref[...], w_ref[C_QA:C_KA, :]) + _dot(dk_ref[...], w_ref[C_KA:C_VA, :])
              + _dot(dv_ref[...], w_ref[C_VA:C_ZA, :]) + _dot(dr_ref[...], w_ref[C_ZA:IN_COLS, :]))
        xv = x_ref[...]
        r = _rstd(xv)
        gx_ref[...] = dh2_ref[...] + _rms_bwd(xv, r, g_ref[...], dh)
        dgn_ref[...] += jnp.sum(dh * (xv * r), axis=0, keepdims=True)

    row = pl.BlockSpec((tm, D_MODEL), lambda i: (i, 0))
    vec = pl.BlockSpec((1, D_MODEL), lambda i: (0, 0))
    return pl.pallas_call(
        body, name="dx", grid=(T_LOC // tm,),
        in_specs=_dproj_specs(tm) + [pl.BlockSpec((IN_COLS, D_MODEL), lambda i: (0, 0)), row, row, vec],
        out_specs=(row, vec),
        out_shape=(SDS((T_LOC, D_MODEL), F32), SDS((1, D_MODEL), F32)),
        compiler_params=_params(48, ("arbitrary",)),
    )(dq, dk, dv, dr, wint, x2, dh2, g_norm)


def _dwin(dq, dk, dv, dr, x2, g_norm):
    tm = 512

    def body(dq_ref, dk_ref, dv_ref, dr_ref, x_ref, g_ref, o_ref, acc):
        @pl.when(pl.program_id(0) == 0)
        def _():
            acc[...] = jnp.zeros_like(acc)
        xv = x_ref[...]
        h = (xv * _rstd(xv) * g_ref[...]).astype(BF16)
        acc[C_QA:C_KA, :] += _dot_tn(dq_ref[...], h)
        acc[C_KA:C_VA, :] += _dot_tn(dk_ref[...], h)
        acc[C_VA:C_ZA, :] += _dot_tn(dv_ref[...], h)
        acc[C_ZA:IN_COLS, :] += _dot_tn(dr_ref[...], h)

        @pl.when(pl.program_id(0) == pl.num_programs(0) - 1)
        def _():
            _cast_rows(acc, o_ref, IN_COLS)

    return pl.pallas_call(
        body, name="dwin", grid=(T_LOC // tm,),
        in_specs=_dproj_specs(tm) + [pl.BlockSpec((tm, D_MODEL), lambda i: (i, 0)), pl.BlockSpec((1, D_MODEL), lambda i: (0, 0))],
        out_specs=pl.BlockSpec((IN_COLS, D_MODEL), lambda i: (0, 0)),
        out_shape=SDS((IN_COLS, D_MODEL), BF16),
        scratch_shapes=[pltpu.VMEM((IN_COLS, D_MODEL), F32)],
        compiler_params=_params(56, ("arbitrary",)),
    )(dq, dk, dv, dr, x2, g_norm)


def _memkv_bwd(dkv, mem2, g_mem, wkv):
    def body(dkv_ref, m_ref, g_ref, w_ref, dw_ref, dg_ref):
        mv = m_ref[...]
        r = _rstd(mv)
        dkv_v = dkv_ref[...].astype(BF16)
        dw_ref[...] = _dot_tn(mv * r * g_ref[...], dkv_v).astype(BF16)
        dg_ref[...] = jnp.sum(_dot_nt(dkv_v, w_ref[...]) * (mv * r), axis=0, keepdims=True)

    return pl.pallas_call(
        body, name="memkv_bwd", out_shape=(SDS((D_MODEL, 2 * MEM_W), BF16), SDS((1, D_MODEL), F32)),
        compiler_params=_params(32),
    )(dkv, mem2, g_mem, wkv)


def _allreduce_small(parts):
    n = len(parts)

    def body(*refs):
        ins, outs, bufs = refs[0:n], refs[n:2 * n], refs[2 * n:3 * n]
        send_sems, recv_sems = refs[3 * n], refs[3 * n + 1]
        pos = _mesh_pos()
        me = _flat(pos)
        for a in range(n):
            bufs[a][me] = ins[a][...]

        def copy(a, k, slot):
            return pltpu.make_async_remote_copy(
                src_ref=ins[a], dst_ref=bufs[a].at[slot],
                send_sem=send_sems.at[7 * a + k - 1], recv_sem=recv_sems.at[7 * a + k - 1],
                device_id=_peer(pos, k), device_id_type=MESH)

        sent = [copy(a, k, me) for a in range(n) for k in range(1, N_DEV)]
        for cp in sent:
            cp.start()
        for a in range(n):
            for k in range(1, N_DEV):
                copy(a, k, _flat(_peer(pos, k))).wait_recv()
        for cp in sent:
            cp.wait_send()
        for a in range(n):
            acc = bufs[a][0]
            for s in range(1, N_DEV):
                acc = acc + bufs[a][s]
            outs[a][...] = acc

    vmem = pl.BlockSpec(memory_space=pltpu.VMEM)
    return pl.pallas_call(
        body, name="allreduce_small",
        out_shape=tuple(SDS(p.shape, F32) for p in parts),
        in_specs=[vmem] * n, out_specs=(vmem,) * n,
        scratch_shapes=[pltpu.VMEM((N_DEV,) + p.shape, F32) for p in parts]
        + [pltpu.SemaphoreType.DMA((7 * n,)), pltpu.SemaphoreType.DMA((7 * n,))],
        compiler_params=_params(16),
    )(*parts)


_HBM = pl.BlockSpec(memory_space=pltpu.HBM)
_SEM = pl.BlockSpec(memory_space=pltpu.SEMAPHORE)
_SIDE_EFFECT = pltpu.SideEffectType.DATAFLOW_SIDE_EFFECTING


def _exchange_copies(src_refs, land_refs, scatter, send_sems, recv_sems):
    pos = _mesh_pos()
    copies = []
    for a, (src, land) in enumerate(zip(src_refs, land_refs)):
        n = land.shape[1]
        for k in range(1, N_DEV):
            peer = _peer(pos, k)
            piece = src.at[pl.ds(pl.multiple_of(_flat(peer) * n, 16), n), :] if scatter[a] else src
            copies.append(pltpu.make_async_remote_copy(
                src_ref=piece, dst_ref=land.at[_flat(pos)],
                send_sem=send_sems.at[7 * a + k - 1], recv_sem=recv_sems.at[7 * a + k - 1],
                device_id=peer, device_id_type=MESH))
    return copies


def _exchange_start(name, srcs, scatter, lands):
    n = len(srcs)

    def body(*refs):
        for cp in _exchange_copies(refs[0:n], refs[n:2 * n], scatter, refs[2 * n], refs[2 * n + 1]):
            cp.start()
        refs[-1][...] = jnp.zeros_like(refs[-1])

    ops = [pltpu.with_memory_space_constraint(t, pltpu.HBM) for t in (*srcs, *lands)]
    out = pl.pallas_call(
        body, name=name,
        out_shape=(pltpu.SemaphoreType.DMA((7 * n,)), pltpu.SemaphoreType.DMA((7 * n,)),
                   *[pltpu.HBM(t.shape, t.dtype) for t in ops], SDS((8, 128), F32)),
        in_specs=[_HBM] * (2 * n),
        out_specs=(_SEM, _SEM, *[_HBM] * (2 * n), pl.BlockSpec(memory_space=pltpu.VMEM)),
        input_output_aliases={i: 2 + i for i in range(2 * n)},
        compiler_params=pltpu.CompilerParams(has_side_effects=_SIDE_EFFECT),
    )(*ops)
    return out[0], out[1], out[2:2 + n], out[2 + n:2 + 2 * n], out[-1]


def _exchange_wait(name, started, scatter, after):
    send_sems, recv_sems, srcs, lands, _ = started
    n = len(srcs)

    def body(*refs):
        for cp in _exchange_copies(refs[0:n], refs[n:2 * n], scatter, refs[2 * n], refs[2 * n + 1]):
            cp.wait_send()
            cp.wait_recv()

    out = pl.pallas_call(
        body, name=name,
        out_shape=tuple(pltpu.HBM(t.shape, t.dtype) for t in (*srcs, *lands)),
        in_specs=[_HBM] * (2 * n) + [_SEM, _SEM, pl.BlockSpec(memory_space=pl.ANY)],
        out_specs=(_HBM,) * (2 * n),
        input_output_aliases={i: i for i in range(2 * n)},
        compiler_params=pltpu.CompilerParams(has_side_effects=_SIDE_EFFECT),
    )(*srcs, *lands, send_sems, recv_sems, after)
    return out[n:]


def _landing(own, me):
    return lax.dynamic_update_slice(lax.empty((N_DEV,) + own.shape, own.dtype), own[None], (me,) + (0,) * own.ndim)


def _adamw(w, g, m, v):
    m = ADAM_B1 * m + (1.0 - ADAM_B1) * g
    v = ADAM_B2 * v + (1.0 - ADAM_B2) * (g * g)
    m_hat = m / (1.0 - ADAM_B1 ** ADAM_STEP)
    v_hat = v / (1.0 - ADAM_B2 ** ADAM_STEP)
    return -ADAM_LR * (m_hat / (jnp.sqrt(v_hat) + ADAM_EPS) + ADAM_WD * w), m, v


def _adam_slots(name, slots, w, m, v):
    _, rows, cols = slots.shape

    def body(s_ref, w_ref, m_ref, v_ref, g_o, d_o, m_o, v_o, acc):
        s = pl.program_id(0)

        @pl.when(s == 0)
        def _():
            acc[...] = s_ref[0].astype(F32)

        @pl.when(s > 0)
        def _():
            acc[...] += s_ref[0].astype(F32)

        @pl.when(s == N_DEV - 1)
        def _():
            g = acc[...]
            g_o[...] = g
            d_o[...], m_o[...], v_o[...] = _adamw(w_ref[...], g, m_ref[...], v_ref[...])

    full = pl.BlockSpec((rows, cols), lambda s: (0, 0))
    return pl.pallas_call(
        body, name="adam_" + name, grid=(N_DEV,),
        in_specs=[pl.BlockSpec((1, rows, cols), lambda s: (s, 0, 0)), full, full, full],
        out_specs=(full,) * 4, out_shape=(SDS((rows, cols), F32),) * 4,
        scratch_shapes=[pltpu.VMEM((rows, cols), F32)],
        compiler_params=_params(40, ("arbitrary",)),
    )(slots, w, m, v)


def _adam_small(ws, gs, ms, vs, loss_slots):
    n = len(ws)

    def total(ref, like):
        if len(ref.shape) == len(like.shape):
            return ref[...]
        acc = ref[0]
        for s in range(1, N_DEV):
            acc = acc + ref[s]
        return acc

    def body(*refs):
        w_r, g_r, m_r, v_r = refs[0:n], refs[n:2 * n], refs[2 * n:3 * n], refs[3 * n:4 * n]
        loss_r, outs = refs[4 * n], refs[4 * n + 1:]
        for a in range(n):
            g = total(g_r[a], w_r[a])
            outs[a][...] = g
            outs[n + 1 + 3 * a][...], outs[n + 2 + 3 * a][...], outs[n + 3 + 3 * a][...] = _adamw(
                w_r[a][...], g, m_r[a][...], v_r[a][...])
        outs[n][...] = total(loss_r, outs[n])

    out = pl.pallas_call(
        body, name="adam_small",
        out_shape=tuple(SDS(w.shape, F32) for w in ws) + (SDS(loss_slots.shape[1:], F32),)
        + tuple(SDS(w.shape, F32) for w in ws for _ in range(3)),
        compiler_params=_params(16),
    )(*ws, *gs, *ms, *vs, loss_slots)
    return out[0:n], out[n], out[n + 1:]


def kernel(x, mem, g_norm, w_in, w_sgu_spatial, b_sgu_spatial, g_sgu_v, g_mem, w_mem_kv, w_out, g_final, loss_target, m_g_norm, m_w_in, m_w_sgu_spatial, m_b_sgu_spatial, m_g_sgu_v, m_g_mem, m_w_mem_kv, m_w_out, m_g_final, v_g_norm, v_w_in, v_w_sgu_spatial, v_b_sgu_spatial, v_g_sgu_v, v_g_mem, v_w_mem_kv, v_w_out, v_g_final):
    x2 = x.reshape(T_LOC, D_MODEL)
    tgt2 = loss_target.reshape(T_LOC, D_MODEL)
    mem2 = mem.reshape(B_LOC * N_MEM, D_MODEL)
    w_s = w_sgu_spatial[0]
    b_exp = jnp.tile(jnp.repeat(b_sgu_spatial[0].T, HEAD, axis=1), (2, 1))
    slope = jnp.power(2.0, -8.0 * (jnp.arange(8, dtype=F32) + 1.0) / 8)
    slopes = jnp.broadcast_to(jnp.repeat(slope.reshape(4, 2), HEAD, axis=1)[:, None, :], (4, 8, 128))

    tr = lambda t: jnp.transpose(t[0])
    hbm = lambda t: pltpu.with_memory_space_constraint(t, pltpu.HBM)

    me = _flat(_mesh_pos())
    own_rows = lambda t: lax.dynamic_slice_in_dim(t, me * (t.shape[0] // N_DEV), t.shape[0] // N_DEV)

    wint, wkv_own, wout_own = _allgather_weights(tr(w_in), w_mem_kv[0], w_out[0])
    started0 = _exchange_start("exchange0_start", [wkv_own, wout_own], [False, False],
                               [_landing(wkv_own, me), _landing(wout_own, me)])
    proj = hbm(_proj_fwd(x2, g_norm + started0[4][0:1, 0:1], wint))
    wkv, wout = _exchange_wait("exchange0_wait", started0, [False, False], proj)
    wkv, wout = wkv.reshape(D_MODEL, 2 * MEM_W), wout.reshape(D_MODEL, D_MODEL)
    kv = _memkv_fwd(mem2, g_mem, wkv)
    a, lse = map(hbm, _attn_fwd(proj, slopes))
    gated = hbm(_branch_fwd(proj, a, kv, w_s, b_exp, g_sgu_v))
    dh2, loss8, dgf = _outproj_loss(gated, wout, x2, tgt2, g_final.reshape(1, D_MODEL))
    dh2 = hbm(dh2)

    da, dr, dkv, dwout, dws, dbs, dgv = _branch_bwd(dh2, wout, gated, proj, a, kv, w_s, b_exp, g_sgu_v)
    da, dr = hbm(da), hbm(dr)
    dwkv, dgm = _memkv_bwd(dkv, mem2, g_mem, wkv)

    early = [dws.reshape(4 * CHUNK, CHUNK), dbs, dgv, dgm, dgf, loss8]
    scatter1 = [True, True] + [False] * len(early)
    started1 = _exchange_start(
        "exchange1_start", [dwkv, dwout] + early, scatter1,
        [_landing(own_rows(dwkv), me), _landing(own_rows(dwout), me)] + [_landing(t, me) for t in early])
    dq, dk, dv = map(hbm, _attn_bwd(proj, slopes + started1[4][0:1, 0:1], da, a, lse))
    s_wkv, s_wout, s_ws, s_bs, s_gv, s_gm, s_gf, s_loss = _exchange_wait("exchange1_wait", started1, scatter1, dq)

    dwint = _dwin(dq, dk, dv, dr, x2, g_norm)
    started2 = _exchange_start("exchange2_start", [dwint], [True], [_landing(own_rows(dwint), me)])
    grad_x, dgn = _dx(dq, dk, dv, dr, wint, x2, dh2, g_norm + started2[4][0:1, 0:1])
    dgn_sum, = _allreduce_small([dgn])
    s_win, = _exchange_wait("exchange2_wait", started2, [True], dgn_sum)

    g_win, d_win, m_win, v_win = map(
        jnp.transpose, _adam_slots("w_in", s_win, tr(w_in), tr(m_w_in), tr(v_w_in)))
    g_wkv, d_wkv, m_wkv, v_wkv = _adam_slots("w_mem_kv", s_wkv, w_mem_kv[0], m_w_mem_kv[0], v_w_mem_kv[0])
    g_wout, d_wout, m_wout, v_wout = _adam_slots("w_out", s_wout, w_out[0], m_w_out[0], v_w_out[0])

    small_shapes = [(1, D_MODEL), (4 * CHUNK, CHUNK), (4, CHUNK), (1, SGU_W), (1, D_MODEL), (1, D_MODEL)]
    pack = lambda arrs: [t.reshape(s) for t, s in zip(arrs, small_shapes)]
    g_small, loss_sum, upd = _adam_small(
        pack([g_norm, w_sgu_spatial, b_sgu_spatial, g_sgu_v, g_mem, g_final]),
        [dgn_sum, s_ws, s_bs, s_gv, s_gm, s_gf],
        pack([m_g_norm, m_w_sgu_spatial, m_b_sgu_spatial, m_g_sgu_v, m_g_mem, m_g_final]),
        pack([v_g_norm, v_w_sgu_spatial, v_b_sgu_spatial, v_g_sgu_v, v_g_mem, v_g_final]), s_loss)
    out_shapes = [g_norm.shape, w_sgu_spatial.shape, b_sgu_spatial.shape, g_sgu_v.shape, g_mem.shape, g_final.shape]
    unpack = lambda arrs: [t.reshape(s) for t, s in zip(arrs, out_shapes)]
    gs = unpack(g_small)
    ds, nms, nvs = unpack(upd[0::3]), unpack(upd[1::3]), unpack(upd[2::3])

    loss = loss_sum[0, 0]

    def assemble(small, win, wkv_, wout_):
        return [small[0], win[None], small[1], small[2], small[3], small[4], wkv_[None], wout_[None], small[5]]

    return (loss, grad_x.reshape(x.shape),
            *assemble(gs, g_win, g_wkv, g_wout), *assemble(ds, d_win, d_wkv, d_wout),
            *assemble(nms, m_win, m_wkv, m_wout), *assemble(nvs, v_win, v_wkv, v_wout))
```

```python
import functools

import jax
import jax.numpy as jnp
from jax import lax
from jax.experimental import pallas as pl
from jax.experimental.pallas import tpu as pltpu

F32 = jnp.float32
BF16 = jnp.bfloat16
SDS = jax.ShapeDtypeStruct
MESH = pl.DeviceIdType.MESH

N_DEV = 8
D_MODEL = 1024
SEQ = 2048
B_LOC = 2
T_LOC = B_LOC * SEQ
N_MEM = 256
HEAD = 64
ATTN_W = 512
SGU_W = 256
MEM_W = 256
IN_COLS = 3328
W_IN_SHARD = IN_COLS // N_DEV
ROW_SHARD = D_MODEL // N_DEV
CHUNK = 128
DILATIONS = ((1, 2048), (4, 512), (16, 128))
RADIUS = 64
EPS = 1e-6
NEG = -1e30
SCALE = HEAD ** -0.5
C_QA, C_KA, C_VA, C_ZA, C_UB, C_VB, C_ZB, C_QM, C_ZM = 0, 512, 1024, 1536, 2048, 2304, 2560, 2816, 3072
QKV_W = 1536
REST_W = IN_COLS - QKV_W

ADAM_LR, ADAM_B1, ADAM_B2, ADAM_EPS, ADAM_WD, ADAM_STEP = 0.001, 0.9, 0.999, 1e-08, 0.01, 10

V7X_VMEM_MIB = 64


def _params(vmem_mib, sem=None, **kw):
    assert vmem_mib < V7X_VMEM_MIB
    return pltpu.CompilerParams(vmem_limit_bytes=vmem_mib << 20, dimension_semantics=sem, **kw)


def _dot(a, b):
    return jnp.dot(a.astype(BF16), b.astype(BF16), preferred_element_type=F32)


def _dot_nt(a, b):
    return lax.dot_general(a.astype(BF16), b.astype(BF16), (((1,), (1,)), ((), ())), preferred_element_type=F32)


def _dot_tn(a, b):
    return lax.dot_general(a.astype(BF16), b.astype(BF16), (((0,), (0,)), ((), ())), preferred_element_type=F32)


def _rstd(v):
    return lax.rsqrt(jnp.mean(v * v, axis=-1, keepdims=True) + EPS)


def _rms_bwd(v, r, g, dy):
    gdy = g * dy
    return r * gdy - v * (r * r * r * jnp.mean(gdy * v, axis=-1, keepdims=True))


def _sigmoid(z):
    return 1.0 / (1.0 + jnp.exp(-z))


def _silu_and_grad(z):
    s = _sigmoid(z)
    return z * s, s * (1.0 + z * (1.0 - s))


_G_C = 0.7978845608028654
_G_K = 0.044715


def _gelu_and_grad(v):
    t = jnp.tanh(_G_C * (v + _G_K * (v * v * v)))
    cdf = 0.5 * (1.0 + t)
    return v * cdf, cdf + 0.5 * v * (1.0 - t * t) * (_G_C * (1.0 + 3.0 * _G_K * v * v))


def _cast_rows(src_ref, dst_ref, rows, step=256):
    def one(i, carry):
        r = pl.ds(pl.multiple_of(i * step, step), step)
        dst_ref[r, :] = src_ref[r, :].astype(dst_ref.dtype)
        return carry
    lax.fori_loop(0, rows // step, one, 0)


def _left_lanes(rows):
    return lax.broadcasted_iota(jnp.int32, (rows, 128), 1) < HEAD


def _mesh_pos():
    return lax.axis_index("x"), lax.axis_index("y"), lax.axis_index("c")


def _peer(pos, k):
    x, y, c = pos
    return (1 - x if k & 4 else x, 1 - y if k & 2 else y, 1 - c if k & 1 else c)


def _flat(pos):
    return 4 * pos[0] + 2 * pos[1] + pos[2]


def _allgather_weights(w_in_t, w_kv, w_out):
    def body(win_ref, wkv_ref, wout_ref, wint_o, wkv_o, wout_o, send_sems, recv_sems):
        x, y, c = _mesh_pos()
        me, sib = (x, y, c), (x, y, 1 - c)
        chips = [(1 - x, y), (x, 1 - y), (1 - x, 1 - y)]

        def rows(p):
            return wint_o.at[pl.ds(pl.multiple_of(_flat(p) * W_IN_SHARD, 16), W_IN_SHARD), :]

        rows(me)[...] = win_ref[...].astype(BF16)

        def copy(k, block, to):
            return pltpu.make_async_remote_copy(
                src_ref=rows(block), dst_ref=rows(block), send_sem=send_sems.at[k], recv_sem=recv_sems.at[k],
                device_id=to, device_id_type=MESH)

        first = [copy(0, me, sib)] + [copy(1 + j, me, (*chip, c)) for j, chip in enumerate(chips)]
        for cp in first:
            cp.start()
        wkv_o[...] = wkv_ref[...].astype(BF16)
        wout_o[...] = wout_ref[...].astype(BF16)
        passed = []
        for j, chip in enumerate(chips):
            copy(1 + j, (*chip, c), me).wait_recv()
            fwd = copy(4 + j, (*chip, c), sib)
            fwd.start()
            passed.append(fwd)
        copy(0, sib, me).wait_recv()
        for j, chip in enumerate(chips):
            copy(4 + j, (*chip, 1 - c), me).wait_recv()
        for cp in first + passed:
            cp.wait_send()

    vmem = pl.BlockSpec(memory_space=pltpu.VMEM)
    return pl.pallas_call(
        body, name="allgather_weights",
        out_shape=(SDS((IN_COLS, D_MODEL), BF16), SDS(w_kv.shape, BF16), SDS(w_out.shape, BF16)),
        in_specs=[vmem, vmem, vmem], out_specs=(vmem, vmem, vmem),
        scratch_shapes=[pltpu.SemaphoreType.DMA((7,)), pltpu.SemaphoreType.DMA((7,))],
        compiler_params=_params(40),
    )(w_in_t, w_kv, w_out)


def _proj_fwd(x2, g_norm, wint):
    tm = 256

    def body(x_ref, g_ref, w_ref, o_ref):
        xv = x_ref[...]
        h = xv * _rstd(xv) * g_ref[...]
        o_ref[...] = _dot_nt(h, w_ref[...])

    return pl.pallas_call(
        body, name="proj_fwd", grid=(T_LOC // tm,),
        in_specs=[pl.BlockSpec((tm, D_MODEL), lambda i: (i, 0)), pl.BlockSpec((1, D_MODEL), lambda i: (0, 0)),
                  pl.BlockSpec((IN_COLS, D_MODEL), lambda i: (0, 0))],
        out_specs=pl.BlockSpec((tm, IN_COLS), lambda i: (i, 0)),
        out_shape=SDS((T_LOC, IN_COLS), F32),
        compiler_params=_params(48, ("arbitrary",)),
    )(x2, g_norm, wint)


def _memkv_fwd(mem2, g_mem, wkv):
    def body(m_ref, g_ref, w_ref, o_ref):
        mv = m_ref[...]
        o_ref[...] = _dot(mv * _rstd(mv) * g_ref[...], w_ref[...])

    return pl.pallas_call(
        body, name="memkv_fwd", out_shape=SDS((B_LOC * N_MEM, 2 * MEM_W), F32), compiler_params=_params(32),
    )(mem2, g_mem, wkv)


N_BIAS = 7


def _fill_bias_tables(sl_ref, tab):
    for cfg, (d, length) in enumerate(DILATIONS):
        nk = min(length, 2 * CHUNK)
        r = lax.broadcasted_iota(jnp.int32, (CHUNK, nk), 0)
        c = lax.broadcasted_iota(jnp.int32, (CHUNK, nk), 1)
        for var in range(3 if length > nk else 1):
            rel = jnp.abs(r - c + var * RADIUS)
            dist = rel.astype(F32) * float(d)
            for h in range(2):
                slope = sl_ref[0, 0:1, h * HEAD:h * HEAD + 1]
                tab[3 * cfg + var, h * CHUNK:(h + 1) * CHUNK, 0:nk] = jnp.where(rel <= RADIUS, -slope * dist, NEG)


def _attn_blocks(visit, unroll):
    def step(t, carry):
        for cfg, (d, length) in enumerate(DILATIONS):
            nblk = length // CHUNK
            if nblk == 1:
                visit(cfg, 0, t, t, length, t)
                continue
            rho, i = (0, t) if d == 1 else (t // nblk, t % nblk)
            ks = jnp.clip(i * CHUNK - RADIUS, 0, length - 2 * CHUNK)
            visit(cfg, (i * CHUNK - ks) // RADIUS, rho + d * (i * CHUNK), rho + d * ks, 2 * CHUNK, t)
        return carry
    lax.fori_loop(0, 16, step, 0, unroll=unroll)


def _stack_heads(v, left):
    return jnp.concatenate([jnp.where(left, v, 0.0), jnp.where(left, 0.0, v)], axis=0)


def _unstack_heads(v, left):
    return jnp.where(left, v[0:CHUNK], v[CHUNK:2 * CHUNK])


def _rows(start, n, d):
    return pl.ds(start, n) if d == 1 else pl.ds(start, n, stride=d)


def _attn_fwd(proj, slopes):
    def body(sl_ref, q_ref, k_ref, v_ref, a_ref, lse_ref, *scr):
        o_c, m_c, l_c, tab = scr[0:3], scr[3:6], scr[6:9], scr[9]
        left = _left_lanes(CHUNK)
        _fill_bias_tables(sl_ref, tab)

        def block(cfg, var, q0, k0, nk, t):
            d = DILATIONS[cfg][0]
            rq, rk = _rows(q0, CHUNK, d), _rows(k0, nk, d)
            qs = _stack_heads(q_ref[rq, :] * SCALE, left)
            s = _dot_nt(qs, k_ref[rk, :]) + tab[3 * cfg + var, :, 0:nk]
            m = jnp.max(s, axis=-1, keepdims=True)
            p = jnp.exp(s - m)
            o_c[cfg][rq, :] = _unstack_heads(_dot(p, v_ref[rk, :]), left)
            m_c[cfg][rq, :] = _unstack_heads(m, left)
            l_c[cfg][rq, :] = _unstack_heads(jnp.sum(p, axis=-1, keepdims=True), left)
        _attn_blocks(block, 4)

        def merge(j, carry):
            rows = pl.ds(pl.multiple_of(j * 256, 256), 256)
            ms = [m_c[i][rows, :] for i in range(3)]
            top = jnp.maximum(jnp.maximum(ms[0], ms[1]), ms[2])
            ws = [jnp.exp(m - top) for m in ms]
            den = l_c[0][rows, :] * ws[0] + l_c[1][rows, :] * ws[1] + l_c[2][rows, :] * ws[2]
            num = o_c[0][rows, :] * ws[0] + o_c[1][rows, :] * ws[1] + o_c[2][rows, :] * ws[2]
            a_ref[rows, :] = num / den
            lse_ref[rows, :] = top + jnp.log(den)
            return carry
        lax.fori_loop(0, SEQ // 256, merge, 0)

    blk = lambda col0: pl.BlockSpec((SEQ, 128), lambda b, hp: (b, col0 // 128 + hp))
    out = pl.BlockSpec((SEQ, 128), lambda b, hp: (b, hp))
    return pl.pallas_call(
        body, name="attn_fwd", grid=(B_LOC, 4),
        in_specs=[pl.BlockSpec((1, 8, 128), lambda b, hp: (hp, 0, 0)), blk(C_QA), blk(C_KA), blk(C_VA)],
        out_specs=(out, out),
        out_shape=(SDS((T_LOC, ATTN_W), F32), SDS((T_LOC, ATTN_W), F32)),
        scratch_shapes=[pltpu.VMEM((SEQ, 128), F32)] * 9 + [pltpu.VMEM((N_BIAS, 2 * CHUNK, 2 * CHUNK), F32)],
        compiler_params=_params(40, ("arbitrary", "arbitrary")),
    )(slopes, proj, proj, proj)


def _sgu_mix(vn, ws_ref, dst_ref, tm):
    left = _left_lanes(CHUNK)
    for ch in range(tm // CHUNK):
        for pr in range(2):
            vp = vn[ch * CHUNK:(ch + 1) * CHUNK, pr * 128:(pr + 1) * 128]
            dst_ref[ch * CHUNK:(ch + 1) * CHUNK, pr * 128:(pr + 1) * 128] = jnp.where(
                left, _dot(ws_ref[2 * pr], vp), _dot(ws_ref[2 * pr + 1], vp))


def _mem_attn_head(qp, kp, h, left):
    qh = jnp.where(left if h == 0 else ~left, qp, 0.0)
    s = _dot_nt(qh, kp) * SCALE
    e = jnp.exp(s - jnp.max(s, axis=-1, keepdims=True))
    return e * (1.0 / jnp.sum(e, axis=-1, keepdims=True)), qh


def _branch_blocks(tm):
    col = lambda w, c0: pl.BlockSpec((tm, w), lambda i: (i, c0 // w))
    return [col(512, C_ZA), col(256, C_UB), col(256, C_VB), col(256, C_ZB), col(256, C_QM), col(256, C_ZM)]


def _branch_fwd(proj, a, kv, w_s, b_exp, g_v):
    tm = 256
    per_ex = SEQ // tm

    def body(za_ref, ub_ref, vb_ref, zb_ref, qm_ref, zm_ref, a_ref, kv_ref, ws_ref, be_ref, gv_ref, o_ref, mix):
        left = _left_lanes(tm)
        o_ref[:, 0:ATTN_W] = (_silu_and_grad(za_ref[...])[0] * a_ref[...]).astype(BF16)
        gu = _gelu_and_grad(ub_ref[...])[0]
        gv = _gelu_and_grad(vb_ref[...])[0]
        vn = gv * _rstd(gv) * gv_ref[...]
        _sgu_mix(vn.astype(BF16), ws_ref, mix, tm)
        sg = gu * (mix[...] + be_ref[...])
        o_ref[:, ATTN_W:ATTN_W + SGU_W] = (_silu_and_grad(zb_ref[...])[0] * sg).astype(BF16)
        szm = _silu_and_grad(zm_ref[...])[0]
        for hp in range(2):
            cols = slice(hp * 128, (hp + 1) * 128)
            qp, kp, vp = qm_ref[:, cols], kv_ref[:, cols], kv_ref[:, MEM_W + hp * 128:MEM_W + (hp + 1) * 128]
            o = [_dot(_mem_attn_head(qp, kp, h, left)[0], vp) for h in range(2)]
            c0 = ATTN_W + SGU_W + hp * 128
            o_ref[:, c0:c0 + 128] = (szm[:, cols] * jnp.where(left, o[0], o[1])).astype(BF16)

    full = lambda shape: pl.BlockSpec(shape, lambda i: (0,) * len(shape))
    return pl.pallas_call(
        body, name="branch_fwd", grid=(T_LOC // tm,),
        in_specs=_branch_blocks(tm) + [
            pl.BlockSpec((tm, ATTN_W), lambda i: (i, 0)), pl.BlockSpec((N_MEM, 2 * MEM_W), lambda i: (i // per_ex, 0)),
            full((4, CHUNK, CHUNK)), full((tm, SGU_W)), full((1, SGU_W))],
        out_specs=pl.BlockSpec((tm, D_MODEL), lambda i: (i, 0)),
        out_shape=SDS((T_LOC, D_MODEL), BF16),
        scratch_shapes=[pltpu.VMEM((tm, SGU_W), F32)],
        compiler_params=_params(40, ("arbitrary",)),
    )(proj, proj, proj, proj, proj, proj, a, kv, w_s, b_exp, g_v)


def _outproj_loss(gated, wout, x2, tgt2, g_final):
    tm = 512

    def body(g_ref, w_ref, x_ref, t_ref, gf_ref, dh2_ref, loss_ref, dgf_ref):
        @pl.when(pl.program_id(0) == 0)
        def _():
            loss_ref[...] = jnp.zeros_like(loss_ref)
            dgf_ref[...] = jnp.zeros_like(dgf_ref)
        h2 = x_ref[...] + _dot(g_ref[...], w_ref[...])
        r = _rstd(h2)
        gf = gf_ref[...]
        err = h2 * r * gf - t_ref[...]
        loss_ref[...] += 0.5 * jnp.sum(jnp.mean(err * err, axis=-1, keepdims=True))
        dy = err * (1.0 / D_MODEL)
        dh2_ref[...] = _rms_bwd(h2, r, gf, dy)
        dgf_ref[...] += jnp.sum(dy * (h2 * r), axis=0, keepdims=True)

    row = pl.BlockSpec((tm, D_MODEL), lambda i: (i, 0))
    vec = pl.BlockSpec((1, D_MODEL), lambda i: (0, 0))
    return pl.pallas_call(
        body, name="outproj_loss", grid=(T_LOC // tm,),
        in_specs=[row, pl.BlockSpec((D_MODEL, D_MODEL), lambda i: (0, 0)), row, row, vec],
        out_specs=(row, pl.BlockSpec((8, 128), lambda i: (0, 0)), vec),
        out_shape=(SDS((T_LOC, D_MODEL), F32), SDS((8, 128), F32), SDS((1, D_MODEL), F32)),
        compiler_params=_params(40, ("arbitrary",)),
    )(gated, wout, x2, tgt2, g_final)


def _branch_bwd(dh2, wout, gated, proj, a, kv, w_s, b_exp, g_v):
    tm = 256
    per_ex = SEQ // tm

    def body(dh2_ref, w_ref, g_ref, za_ref, ub_ref, vb_ref, zb_ref, qm_ref, zm_ref, a_ref, kv_ref, ws_ref,
             be_ref, gv_ref, da_ref, dr_ref, dkv_ref, dwo_ref, dws_ref, db_ref, dgv_ref, mix, dvn, dmsum, dwo_acc):
        i = pl.program_id(0)
        left = _left_lanes(tm)
        leftc = _left_lanes(CHUNK)

        @pl.when(i == 0)
        def _():
            dwo_acc[...] = jnp.zeros_like(dwo_acc)
            dws_ref[...] = jnp.zeros_like(dws_ref)
            dgv_ref[...] = jnp.zeros_like(dgv_ref)
            dmsum[...] = jnp.zeros_like(dmsum)

        @pl.when(i % per_ex == 0)
        def _():
            dkv_ref[...] = jnp.zeros_like(dkv_ref)

        dh2 = dh2_ref[...].astype(BF16)
        dwo_acc[...] += _dot_tn(g_ref[...], dh2)
        dg = _dot_nt(dh2, w_ref[...])

        sa, dsa = _silu_and_grad(za_ref[...])
        dga = dg[:, 0:ATTN_W]
        da_ref[...] = dga * sa
        dr_ref[:, 0:512] = (dga * a_ref[...] * dsa).astype(BF16)

        ub, vb = ub_ref[...], vb_ref[...]
        gu, dgu = _gelu_and_grad(ub)
        gv, dgv = _gelu_and_grad(vb)
        rv = _rstd(gv)
        gain = gv_ref[...]
        vn = (gv * rv * gain).astype(BF16)
        _sgu_mix(vn, ws_ref, mix, tm)
        mixed = mix[...] + be_ref[...]
        sb, dsb = _silu_and_grad(zb_ref[...])
        dgb = dg[:, ATTN_W:ATTN_W + SGU_W]
        dsg = dgb * sb
        dr_ref[:, 512:768] = (dsg * mixed * dgu).astype(BF16)
        dr_ref[:, 1024:1280] = (dgb * (gu * mixed) * dsb).astype(BF16)
        dmix = dsg * gu
        for ch in range(tm // CHUNK):
            rows = slice(ch * CHUNK, (ch + 1) * CHUNK)
            dmsum[...] += dmix[rows, :]
            for pr in range(2):
                cols = slice(pr * 128, (pr + 1) * 128)
                dmp, vp = dmix[rows, cols], vn[rows, cols]
                for h in range(2):
                    g = 2 * pr + h
                    dws_ref[g] += _dot_nt(jnp.where(leftc if h == 0 else ~leftc, dmp, 0.0), vp)
                dvn[rows, cols] = jnp.where(leftc, _dot_tn(ws_ref[2 * pr], dmp), _dot_tn(ws_ref[2 * pr + 1], dmp))
        dvn_v = dvn[...]
        dgv_ref[...] += jnp.sum(dvn_v * (gv * rv), axis=0, keepdims=True)
        dr_ref[:, 768:1024] = (_rms_bwd(gv, rv, gain, dvn_v) * dgv).astype(BF16)

        szm, dszm = _silu_and_grad(zm_ref[...])
        dgm = dg[:, ATTN_W + SGU_W:D_MODEL]
        dmo = dgm * szm
        for hp in range(2):
            cols = slice(hp * 128, (hp + 1) * 128)
            vcols = slice(MEM_W + hp * 128, MEM_W + (hp + 1) * 128)
            qp, kp, vp, dmop = qm_ref[:, cols], kv_ref[:, cols], kv_ref[:, vcols], dmo[:, cols]
            o, dq = [], []
            dk = jnp.zeros((N_MEM, 128), F32)
            dv = jnp.zeros((N_MEM, 128), F32)
            for h in range(2):
                p, qh = _mem_attn_head(qp, kp, h, left)
                dmoh = jnp.where(left if h == 0 else ~left, dmop, 0.0)
                o.append(_dot(p, vp))
                dp = _dot_nt(dmoh, vp)
                ds = p * (dp - jnp.sum(p * dp, axis=-1, keepdims=True)) * SCALE
                dq.append(_dot(ds, kp))
                dk += _dot_tn(ds, qh)
                dv += _dot_tn(p, dmoh)
            dr_ref[:, 1280 + hp * 128:1280 + (hp + 1) * 128] = jnp.where(left, dq[0], dq[1]).astype(BF16)
            dr_ref[:, 1536 + hp * 128:1536 + (hp + 1) * 128] = (
                dgm[:, cols] * jnp.where(left, o[0], o[1]) * dszm[:, cols]).astype(BF16)
            dkv_ref[:, cols] += dk
            dkv_ref[:, vcols] += dv

        @pl.when(i == pl.num_programs(0) - 1)
        def _():
            tot = dmsum[...]
            hi = tot.astype(BF16)
            lo = (tot - hi.astype(F32)).astype(BF16)
            grp = (lax.broadcasted_iota(jnp.int32, (SGU_W, 128), 0) // HEAD
                   == lax.broadcasted_iota(jnp.int32, (SGU_W, 128), 1)).astype(BF16)
            db_ref[...] = (_dot(hi, grp) + _dot(lo, grp)).T[0:4, :]
            _cast_rows(dwo_acc, dwo_ref, D_MODEL)

    full = lambda shape: pl.BlockSpec(shape, lambda i: (0,) * len(shape))
    row = lambda w: pl.BlockSpec((tm, w), lambda i: (i, 0))
    return pl.pallas_call(
        body, name="branch_bwd", grid=(T_LOC // tm,),
        in_specs=[row(D_MODEL), full((D_MODEL, D_MODEL)), row(D_MODEL)] + _branch_blocks(tm) + [
            row(ATTN_W), pl.BlockSpec((N_MEM, 2 * MEM_W), lambda i: (i // per_ex, 0)),
            full((4, CHUNK, CHUNK)), full((tm, SGU_W)), full((1, SGU_W))],
        out_specs=(row(ATTN_W), row(REST_W), pl.BlockSpec((N_MEM, 2 * MEM_W), lambda i: (i // per_ex, 0)),
                   full((D_MODEL, D_MODEL)), full((4, CHUNK, CHUNK)), full((4, CHUNK)), full((1, SGU_W))),
        out_shape=(SDS((T_LOC, ATTN_W), F32), SDS((T_LOC, REST_W), BF16), SDS((B_LOC * N_MEM, 2 * MEM_W), F32),
                   SDS((D_MODEL, D_MODEL), BF16), SDS((4, CHUNK, CHUNK), F32), SDS((4, CHUNK), F32), SDS((1, SGU_W), F32)),
        scratch_shapes=[pltpu.VMEM((tm, SGU_W), F32), pltpu.VMEM((tm, SGU_W), F32), pltpu.VMEM((CHUNK, SGU_W), F32),
                        pltpu.VMEM((D_MODEL, D_MODEL), F32)],
        compiler_params=_params(56, ("arbitrary",)),
    )(dh2, wout, gated, proj, proj, proj, proj, proj, proj, a, kv, w_s, b_exp, g_v)


def _attn_bwd(proj, slopes, da, a, lse):
    def body(sl_ref, q_ref, k_ref, v_ref, da_ref, a_ref, lse_ref, dq_ref, dk_ref, dv_ref, *scr):
        dq_s, dk_s, dv_s, tab = scr[0:3], scr[3:6], scr[6:9], scr[9]
        lse_h, delta_h = scr[10:12], scr[12:14]
        p_all, ds_all = scr[14], scr[15]
        left = _left_lanes(CHUNK)
        _fill_bias_tables(sl_ref, tab)

        def prep(j, carry):
            rows = pl.ds(pl.multiple_of(j * 256, 256), 256)
            l256 = _left_lanes(256)
            prod = da_ref[rows, :] * a_ref[rows, :]
            delta_h[0][rows, :] = jnp.broadcast_to(jnp.sum(jnp.where(l256, prod, 0.0), axis=-1, keepdims=True), (256, 128))
            delta_h[1][rows, :] = jnp.broadcast_to(jnp.sum(jnp.where(l256, 0.0, prod), axis=-1, keepdims=True), (256, 128))
            pair = lse_ref[rows, :]
            other = pltpu.roll(pair, HEAD, axis=1)
            lse_h[0][rows, :] = jnp.where(l256, pair, other)
            lse_h[1][rows, :] = jnp.where(l256, other, pair)
            zero = jnp.zeros((256, 128), F32)
            for cfg in range(3):
                dk_s[cfg][rows, :] = zero
                dv_s[cfg][rows, :] = zero
            return carry
        lax.fori_loop(0, SEQ // 256, prep, 0)

        def per_row(halves, rq, nk):
            v = jnp.concatenate([halves[0][rq, :], halves[1][rq, :]], axis=0)
            return v if nk == 128 else jnp.concatenate([v, v], axis=1)

        def probs(cfg, var, q0, k0, nk, t):
            d = DILATIONS[cfg][0]
            rq, rk = _rows(q0, CHUNK, d), _rows(k0, nk, d)
            qs = _stack_heads(q_ref[rq, :] * SCALE, left)
            das = _stack_heads(da_ref[rq, :], left)
            s = _dot_nt(qs, k_ref[rk, :]) + tab[3 * cfg + var, :, 0:nk]
            p = jnp.exp(s - per_row(lse_h, rq, nk))
            p_all[16 * cfg + t, :, 0:nk] = p.astype(BF16)
            ds_all[16 * cfg + t, :, 0:nk] = (p * (_dot_nt(das, v_ref[rk, :]) - per_row(delta_h, rq, nk))).astype(BF16)
        _attn_blocks(probs, 4)

        def grads(cfg, var, q0, k0, nk, t):
            d = DILATIONS[cfg][0]
            rq, rk = _rows(q0, CHUNK, d), _rows(k0, nk, d)
            qs = _stack_heads(q_ref[rq, :] * SCALE, left).astype(BF16)
            das = _stack_heads(da_ref[rq, :], left).astype(BF16)
            p, ds = p_all[16 * cfg + t, :, 0:nk], ds_all[16 * cfg + t, :, 0:nk]
            dq_s[cfg][rq, :] = _unstack_heads(_dot(ds, k_ref[rk, :]), left) * SCALE
            dk_s[cfg][rk, :] += _dot_tn(ds, qs)
            dv_s[cfg][rk, :] += _dot_tn(p, das)
        _attn_blocks(grads, 4)

        def flush(j, carry):
            rows = pl.ds(pl.multiple_of(j * 256, 256), 256)
            for acc, dst in ((dq_s, dq_ref), (dk_s, dk_ref), (dv_s, dv_ref)):
                dst[rows, :] = (acc[0][rows, :] + acc[1][rows, :] + acc[2][rows, :]).astype(BF16)
            return carry
        lax.fori_loop(0, SEQ // 256, flush, 0)

    blk = lambda col0: pl.BlockSpec((SEQ, 128), lambda b, hp: (b, col0 // 128 + hp))
    own = pl.BlockSpec((SEQ, 128), lambda b, hp: (b, hp))
    return pl.pallas_call(
        body, name="attn_bwd", grid=(B_LOC, 4),
        in_specs=[pl.BlockSpec((1, 8, 128), lambda b, hp: (hp, 0, 0)), blk(C_QA), blk(C_KA), blk(C_VA), own, own, own],
        out_specs=(own, own, own),
        out_shape=(SDS((T_LOC, ATTN_W), BF16),) * 3,
        scratch_shapes=[pltpu.VMEM((SEQ, 128), F32)] * 9 + [pltpu.VMEM((N_BIAS, 2 * CHUNK, 2 * CHUNK), F32)]
        + [pltpu.VMEM((SEQ, 128), F32)] * 4 + [pltpu.VMEM((48, 2 * CHUNK, 2 * CHUNK), BF16)] * 2,
        compiler_params=_params(52, ("arbitrary", "arbitrary")),
    )(slopes, proj, proj, proj, da, a, lse)


def _dproj_specs(tm):
    third = pl.BlockSpec((tm, ATTN_W), lambda i: (i, 0))
    return [third, third, third, pl.BlockSpec((tm, REST_W), lambda i: (i, 0))]


def _dx(dq, dk, dv, dr, wint, x2, dh2, g_norm):
    tm = 256

    def body(dq_ref, dk_ref, dv_ref, dr_ref, w_ref, x_ref, dh2_ref, g_ref, gx_ref, dgn_ref):
        @pl.when(pl.program_id(0) == 0)
        def _():
            dgn_ref[...] = jnp.zeros_like(dgn_ref)
        dh = (_dot(dq_ref[...], w_ref[C_QA:C_KA, :]) + _dot(dk_ref[...], w_ref[C_KA:C_VA, :])
              + _dot(dv_ref[...], w_ref[C_VA:C_ZA, :]) + _dot(dr_ref[...], w_ref[C_ZA:IN_COLS, :]))
        xv = x_ref[...]
        r = _rstd(xv)
        gx_ref[...] = dh2_ref[...] + _rms_bwd(xv, r, g_ref[...], dh)
        dgn_ref[...] += jnp.sum(dh * (xv * r), axis=0, keepdims=True)

    row = pl.BlockSpec((tm, D_MODEL), lambda i: (i, 0))
    vec = pl.BlockSpec((1, D_MODEL), lambda i: (0, 0))
    return pl.pallas_call(
        body, name="dx", grid=(T_LOC // tm,),
        in_specs=_dproj_specs(tm) + [pl.BlockSpec((IN_COLS, D_MODEL), lambda i: (0, 0)), row, row, vec],
        out_specs=(row, vec),
        out_shape=(SDS((T_LOC, D_MODEL), F32), SDS((1, D_MODEL), F32)),
        compiler_params=_params(48, ("arbitrary",)),
    )(dq, dk, dv, dr, wint, x2, dh2, g_norm)


def _dwin(dq, dk, dv, dr, x2, g_norm):
    tm = 512

    def body(dq_ref, dk_ref, dv_ref, dr_ref, x_ref, g_ref, o_ref, acc):
        @pl.when(pl.program_id(0) == 0)
        def _():
            acc[...] = jnp.zeros_like(acc)
        xv = x_ref[...]
        h = (xv * _rstd(xv) * g_ref[...]).astype(BF16)
        acc[C_QA:C_KA, :] += _dot_tn(dq_ref[...], h)
        acc[C_KA:C_VA, :] += _dot_tn(dk_ref[...], h)
        acc[C_VA:C_ZA, :] += _dot_tn(dv_ref[...], h)
        acc[C_ZA:IN_COLS, :] += _dot_tn(dr_ref[...], h)

        @pl.when(pl.program_id(0) == pl.num_programs(0) - 1)
        def _():
            _cast_rows(acc, o_ref, IN_COLS)

    return pl.pallas_call(
        body, name="dwin", grid=(T_LOC // tm,),
        in_specs=_dproj_specs(tm) + [pl.BlockSpec((tm, D_MODEL), lambda i: (i, 0)), pl.BlockSpec((1, D_MODEL), lambda i: (0, 0))],
        out_specs=pl.BlockSpec((IN_COLS, D_MODEL), lambda i: (0, 0)),
        out_shape=SDS((IN_COLS, D_MODEL), BF16),
        scratch_shapes=[pltpu.VMEM((IN_COLS, D_MODEL), F32)],
        compiler_params=_params(56, ("arbitrary",)),
    )(dq, dk, dv, dr, x2, g_norm)


def _memkv_bwd(dkv, mem2, g_mem, wkv):
    def body(dkv_ref, m_ref, g_ref, w_ref, dw_ref, dg_ref):
        mv = m_ref[...]
        r = _rstd(mv)
        dkv_v = dkv_ref[...].astype(BF16)
        dw_ref[...] = _dot_tn(mv * r * g_ref[...], dkv_v).astype(BF16)
        dg_ref[...] = jnp.sum(_dot_nt(dkv_v, w_ref[...]) * (mv * r), axis=0, keepdims=True)

    return pl.pallas_call(
        body, name="memkv_bwd", out_shape=(SDS((D_MODEL, 2 * MEM_W), BF16), SDS((1, D_MODEL), F32)),
        compiler_params=_params(32),
    )(dkv, mem2, g_mem, wkv)


def _allreduce_small(parts):
    n = len(parts)

    def body(*refs):
        ins, outs, bufs = refs[0:n], refs[n:2 * n], refs[2 * n:3 * n]
        send_sems, recv_sems = refs[3 * n], refs[3 * n + 1]
        pos = _mesh_pos()
        me = _flat(pos)
        for a in range(n):
            bufs[a][me] = ins[a][...]

        def copy(a, k, slot):
            return pltpu.make_async_remote_copy(
                src_ref=ins[a], dst_ref=bufs[a].at[slot],
                send_sem=send_sems.at[7 * a + k - 1], recv_sem=recv_sems.at[7 * a + k - 1],
                device_id=_peer(pos, k), device_id_type=MESH)

        sent = [copy(a, k, me) for a in range(n) for k in range(1, N_DEV)]
        for cp in sent:
            cp.start()
        for a in range(n):
            for k in range(1, N_DEV):
                copy(a, k, _flat(_peer(pos, k))).wait_recv()
        for cp in sent:
            cp.wait_send()
        for a in range(n):
            acc = bufs[a][0]
            for s in range(1, N_DEV):
                acc = acc + bufs[a][s]
            outs[a][...] = acc

    vmem = pl.BlockSpec(memory_space=pltpu.VMEM)
    return pl.pallas_call(
        body, name="allreduce_small",
        out_shape=tuple(SDS(p.shape, F32) for p in parts),
        in_specs=[vmem] * n, out_specs=(vmem,) * n,
        scratch_shapes=[pltpu.VMEM((N_DEV,) + p.shape, F32) for p in parts]
        + [pltpu.SemaphoreType.DMA((7 * n,)), pltpu.SemaphoreType.DMA((7 * n,))],
        compiler_params=_params(16),
    )(*parts)


_HBM = pl.BlockSpec(memory_space=pltpu.HBM)
_SEM = pl.BlockSpec(memory_space=pltpu.SEMAPHORE)
_SIDE_EFFECT = pltpu.SideEffectType.DATAFLOW_SIDE_EFFECTING


def _exchange_copies(src_refs, land_refs, scatter, send_sems, recv_sems):
    pos = _mesh_pos()
    copies = []
    for a, (src, land) in enumerate(zip(src_refs, land_refs)):
        n = land.shape[1]
        for k in range(1, N_DEV):
            peer = _peer(pos, k)
            piece = src.at[pl.ds(pl.multiple_of(_flat(peer) * n, 16), n), :] if scatter[a] else src
            copies.append(pltpu.make_async_remote_copy(
                src_ref=piece, dst_ref=land.at[_flat(pos)],
                send_sem=send_sems.at[7 * a + k - 1], recv_sem=recv_sems.at[7 * a + k - 1],
                device_id=peer, device_id_type=MESH))
    return copies


def _exchange_start(name, srcs, scatter, lands):
    n = len(srcs)

    def body(*refs):
        for cp in _exchange_copies(refs[0:n], refs[n:2 * n], scatter, refs[2 * n], refs[2 * n + 1]):
            cp.start()
        refs[-1][...] = jnp.zeros_like(refs[-1])

    ops = [pltpu.with_memory_space_constraint(t, pltpu.HBM) for t in (*srcs, *lands)]
    out = pl.pallas_call(
        body, name=name,
        out_shape=(pltpu.SemaphoreType.DMA((7 * n,)), pltpu.SemaphoreType.DMA((7 * n,)),
                   *[pltpu.HBM(t.shape, t.dtype) for t in ops], SDS((8, 128), F32)),
        in_specs=[_HBM] * (2 * n),
        out_specs=(_SEM, _SEM, *[_HBM] * (2 * n), pl.BlockSpec(memory_space=pltpu.VMEM)),
        input_output_aliases={i: 2 + i for i in range(2 * n)},
        compiler_params=pltpu.CompilerParams(has_side_effects=_SIDE_EFFECT),
    )(*ops)
    return out[0], out[1], out[2:2 + n], out[2 + n:2 + 2 * n], out[-1]


def _exchange_wait(name, started, scatter, after):
    send_sems, recv_sems, srcs, lands, _ = started
    n = len(srcs)

    def body(*refs):
        for cp in _exchange_copies(refs[0:n], refs[n:2 * n], scatter, refs[2 * n], refs[2 * n + 1]):
            cp.wait_send()
            cp.wait_recv()

    out = pl.pallas_call(
        body, name=name,
        out_shape=tuple(pltpu.HBM(t.shape, t.dtype) for t in (*srcs, *lands)),
        in_specs=[_HBM] * (2 * n) + [_SEM, _SEM, pl.BlockSpec(memory_space=pl.ANY)],
        out_specs=(_HBM,) * (2 * n),
        input_output_aliases={i: i for i in range(2 * n)},
        compiler_params=pltpu.CompilerParams(has_side_effects=_SIDE_EFFECT),
    )(*srcs, *lands, send_sems, recv_sems, after)
    return out[n:]


def _landing(own, me):
    return lax.dynamic_update_slice(lax.empty((N_DEV,) + own.shape, own.dtype), own[None], (me,) + (0,) * own.ndim)


def _adamw(w, g, m, v):
    m = ADAM_B1 * m + (1.0 - ADAM_B1) * g
    v = ADAM_B2 * v + (1.0 - ADAM_B2) * (g * g)
    m_hat = m / (1.0 - ADAM_B1 ** ADAM_STEP)
    v_hat = v / (1.0 - ADAM_B2 ** ADAM_STEP)
    return -ADAM_LR * (m_hat / (jnp.sqrt(v_hat) + ADAM_EPS) + ADAM_WD * w), m, v


def _adam_slots(name, slots, w, m, v):
    _, rows, cols = slots.shape

    def body(s_ref, w_ref, m_ref, v_ref, g_o, d_o, m_o, v_o, acc):
        s = pl.program_id(0)

        @pl.when(s == 0)
        def _():
            acc[...] = s_ref[0].astype(F32)

        @pl.when(s > 0)
        def _():
            acc[...] += s_ref[0].astype(F32)

        @pl.when(s == N_DEV - 1)
        def _():
            g = acc[...]
            g_o[...] = g
            d_o[...], m_o[...], v_o[...] = _adamw(w_ref[...], g, m_ref[...], v_ref[...])

    full = pl.BlockSpec((rows, cols), lambda s: (0, 0))
    return pl.pallas_call(
        body, name="adam_" + name, grid=(N_DEV,),
        in_specs=[pl.BlockSpec((1, rows, cols), lambda s: (s, 0, 0)), full, full, full],
        out_specs=(full,) * 4, out_shape=(SDS((rows, cols), F32),) * 4,
        scratch_shapes=[pltpu.VMEM((rows, cols), F32)],
        compiler_params=_params(40, ("arbitrary",)),
    )(slots, w, m, v)


def _adam_small(ws, gs, ms, vs, loss_slots):
    n = len(ws)

    def total(ref, like):
        if len(ref.shape) == len(like.shape):
            return ref[...]
        acc = ref[0]
        for s in range(1, N_DEV):
            acc = acc + ref[s]
        return acc

    def body(*refs):
        w_r, g_r, m_r, v_r = refs[0:n], refs[n:2 * n], refs[2 * n:3 * n], refs[3 * n:4 * n]
        loss_r, outs = refs[4 * n], refs[4 * n + 1:]
        for a in range(n):
            g = total(g_r[a], w_r[a])
            outs[a][...] = g
            outs[n + 1 + 3 * a][...], outs[n + 2 + 3 * a][...], outs[n + 3 + 3 * a][...] = _adamw(
                w_r[a][...], g, m_r[a][...], v_r[a][...])
        outs[n][...] = total(loss_r, outs[n])

    out = pl.pallas_call(
        body, name="adam_small",
        out_shape=tuple(SDS(w.shape, F32) for w in ws) + (SDS(loss_slots.shape[1:], F32),)
        + tuple(SDS(w.shape, F32) for w in ws for _ in range(3)),
        compiler_params=_params(16),
    )(*ws, *gs, *ms, *vs, loss_slots)
    return out[0:n], out[n], out[n + 1:]


def kernel(x, mem, g_norm, w_in, w_sgu_spatial, b_sgu_spatial, g_sgu_v, g_mem, w_mem_kv, w_out, g_final, loss_target, m_g_norm, m_w_in, m_w_sgu_spatial, m_b_sgu_spatial, m_g_sgu_v, m_g_mem, m_w_mem_kv, m_w_out, m_g_final, v_g_norm, v_w_in, v_w_sgu_spatial, v_b_sgu_spatial, v_g_sgu_v, v_g_mem, v_w_mem_kv, v_w_out, v_g_final):
    x2 = x.reshape(T_LOC, D_MODEL)
    tgt2 = loss_target.reshape(T_LOC, D_MODEL)
    mem2 = mem.reshape(B_LOC * N_MEM, D_MODEL)
    w_s = w_sgu_spatial[0]
    b_exp = jnp.tile(jnp.repeat(b_sgu_spatial[0].T, HEAD, axis=1), (2, 1))
    slope = jnp.power(2.0, -8.0 * (jnp.arange(8, dtype=F32) + 1.0) / 8)
    slopes = jnp.broadcast_to(jnp.repeat(slope.reshape(4, 2), HEAD, axis=1)[:, None, :], (4, 8, 128))

    tr = lambda t: jnp.transpose(t[0])
    hbm = lambda t: pltpu.with_memory_space_constraint(t, pltpu.HBM)

    me = _flat(_mesh_pos())
    own_rows = lambda t: lax.dynamic_slice_in_dim(t, me * (t.shape[0] // N_DEV), t.shape[0] // N_DEV)

    wint, wkv_own, wout_own = _allgather_weights(tr(w_in), w_mem_kv[0], w_out[0])
    started0 = _exchange_start("exchange0_start", [wkv_own, wout_own], [False, False],
                               [_landing(wkv_own, me), _landing(wout_own, me)])
    proj = hbm(_proj_fwd(x2, g_norm + started0[4][0:1, 0:1], wint))
    wkv, wout = _exchange_wait("exchange0_wait", started0, [False, False], proj)
    wkv, wout = wkv.reshape(D_MODEL, 2 * MEM_W), wout.reshape(D_MODEL, D_MODEL)
    kv = _memkv_fwd(mem2, g_mem, wkv)
    a, lse = map(hbm, _attn_fwd(proj, slopes))
    gated = hbm(_branch_fwd(proj, a, kv, w_s, b_exp, g_sgu_v))
    dh2, loss8, dgf = _outproj_loss(gated, wout, x2, tgt2, g_final.reshape(1, D_MODEL))
    dh2 = hbm(dh2)

    da, dr, dkv, dwout, dws, dbs, dgv = _branch_bwd(dh2, wout, gated, proj, a, kv, w_s, b_exp, g_sgu_v)
    da, dr = hbm(da), hbm(dr)
    dwkv, dgm = _memkv_bwd(dkv, mem2, g_mem, wkv)

    early = [dws.reshape(4 * CHUNK, CHUNK), dbs, dgv, dgm, dgf, loss8]
    scatter1 = [True, True] + [False] * len(early)
    started1 = _exchange_start(
        "exchange1_start", [dwkv, dwout] + early, scatter1,
        [_landing(own_rows(dwkv), me), _landing(own_rows(dwout), me)] + [_landing(t, me) for t in early])
    dq, dk, dv = map(hbm, _attn_bwd(proj, slopes + started1[4][0:1, 0:1], da, a, lse))
    s_wkv, s_wout, s_ws, s_bs, s_gv, s_gm, s_gf, s_loss = _exchange_wait("exchange1_wait", started1, scatter1, dq)

    dwint = _dwin(dq, dk, dv, dr, x2, g_norm)
    started2 = _exchange_start("exchange2_start", [dwint], [True], [_landing(own_rows(dwint), me)])
    grad_x, dgn = _dx(dq, dk, dv, dr, wint, x2, dh2, g_norm + started2[4][0:1, 0:1])
    dgn_sum, = _allreduce_small([dgn])
    s_win, = _exchange_wait("exchange2_wait", started2, [True], dgn_sum)

    g_win, d_win, m_win, v_win = map(
        jnp.transpose, _adam_slots("w_in", s_win, tr(w_in), tr(m_w_in), tr(v_w_in)))
    g_wkv, d_wkv, m_wkv, v_wkv = _adam_slots("w_mem_kv", s_wkv, w_mem_kv[0], m_w_mem_kv[0], v_w_mem_kv[0])
    g_wout, d_wout, m_wout, v_wout = _adam_slots("w_out", s_wout, w_out[0], m_w_out[0], v_w_out[0])

    small_shapes = [(1, D_MODEL), (4 * CHUNK, CHUNK), (4, CHUNK), (1, SGU_W), (1, D_MODEL), (1, D_MODEL)]
    pack = lambda arrs: [t.reshape(s) for t, s in zip(arrs, small_shapes)]
    g_small, loss_sum, upd = _adam_small(
        pack([g_norm, w_sgu_spatial, b_sgu_spatial, g_sgu_v, g_mem, g_final]),
        [dgn_sum, s_ws, s_bs, s_gv, s_gm, s_gf],
        pack([m_g_norm, m_w_sgu_spatial, m_b_sgu_spatial, m_g_sgu_v, m_g_mem, m_g_final]),
        pack([v_g_norm, v_w_sgu_spatial, v_b_sgu_spatial, v_g_sgu_v, v_g_mem, v_g_final]), s_loss)
    out_shapes = [g_norm.shape, w_sgu_spatial.shape, b_sgu_spatial.shape, g_sgu_v.shape, g_mem.shape, g_final.shape]
    unpack = lambda arrs: [t.reshape(s) for t, s in zip(arrs, out_shapes)]
    gs = unpack(g_small)
    ds, nms, nvs = unpack(upd[0::3]), unpack(upd[1::3]), unpack(upd[2::3])

    loss = loss_sum[0, 0]

    def assemble(small, win, wkv_, wout_):
        return [small[0], win[None], small[1], small[2], small[3], small[4], wkv_[None], wout_[None], small[5]]

    return (loss, grad_x.reshape(x.shape),
            *assemble(gs, g_win, g_wkv, g_wout), *assemble(ds, d_win, d_wkv, d_wout),
            *assemble(nms, m_win, m_wkv, m_wout), *assemble(nvs, v_win, v_wkv, v_wout))
```

```python
import functools

import jax
import jax.numpy as jnp
from jax import lax
from jax.experimental import pallas as pl
from jax.experimental.pallas import tpu as pltpu

F32 = jnp.float32
BF16 = jnp.bfloat16
SDS = jax.ShapeDtypeStruct
MESH = pl.DeviceIdType.MESH

N_DEV = 8
D_MODEL = 1024
SEQ = 2048
B_LOC = 2
T_LOC = B_LOC * SEQ
N_MEM = 256
HEAD = 64
ATTN_W = 512
SGU_W = 256
MEM_W = 256
IN_COLS = 3328
W_IN_SHARD = IN_COLS // N_DEV
ROW_SHARD = D_MODEL // N_DEV
CHUNK = 128
DILATIONS = ((1, 2048), (4, 512), (16, 128))
RADIUS = 64
EPS = 1e-6
NEG = -1e30
SCALE = HEAD ** -0.5
C_QA, C_KA, C_VA, C_ZA, C_UB, C_VB, C_ZB, C_QM, C_ZM = 0, 512, 1024, 1536, 2048, 2304, 2560, 2816, 3072
QKV_W = 1536
REST_W = IN_COLS - QKV_W

ADAM_LR, ADAM_B1, ADAM_B2, ADAM_EPS, ADAM_WD, ADAM_STEP = 0.001, 0.9, 0.999, 1e-08, 0.01, 10

V7X_VMEM_MIB = 64
VMEM_LIMIT_MIB = V7X_VMEM_MIB - 6


def _params(sem=None):
    return pltpu.CompilerParams(vmem_limit_bytes=VMEM_LIMIT_MIB << 20, dimension_semantics=sem)


def _dot(a, b):
    return jnp.dot(a.astype(BF16), b.astype(BF16), preferred_element_type=F32)


def _dot_nt(a, b):
    return lax.dot_general(a.astype(BF16), b.astype(BF16), (((1,), (1,)), ((), ())), preferred_element_type=F32)


def _dot_tn(a, b):
    return lax.dot_general(a.astype(BF16), b.astype(BF16), (((0,), (0,)), ((), ())), preferred_element_type=F32)


def _rstd(v):
    return lax.rsqrt(jnp.mean(v * v, axis=-1, keepdims=True) + EPS)


def _rms_bwd(v, r, g, dy):
    gdy = g * dy
    return r * gdy - v * (r * r * r * jnp.mean(gdy * v, axis=-1, keepdims=True))


def _sigmoid(z):
    return 1.0 / (1.0 + jnp.exp(-z))


def _silu_and_grad(z):
    s = _sigmoid(z)
    return z * s, s * (1.0 + z * (1.0 - s))


_G_C = 0.7978845608028654
_G_K = 0.044715


def _gelu_and_grad(v):
    t = jnp.tanh(_G_C * (v + _G_K * (v * v * v)))
    cdf = 0.5 * (1.0 + t)
    return v * cdf, cdf + 0.5 * v * (1.0 - t * t) * (_G_C * (1.0 + 3.0 * _G_K * v * v))


def _cast_rows(src_ref, dst_ref, rows, step=256):
    def one(i, carry):
        r = pl.ds(pl.multiple_of(i * step, step), step)
        dst_ref[r, :] = src_ref[r, :].astype(dst_ref.dtype)
        return carry
    lax.fori_loop(0, rows // step, one, 0)


def _left_lanes(rows):
    return lax.broadcasted_iota(jnp.int32, (rows, 128), 1) < HEAD


def _mesh_pos():
    return lax.axis_index("x"), lax.axis_index("y"), lax.axis_index("c")


def _peer(pos, k):
    x, y, c = pos
    return (1 - x if k & 4 else x, 1 - y if k & 2 else y, 1 - c if k & 1 else c)


def _flat(pos):
    return 4 * pos[0] + 2 * pos[1] + pos[2]


def _allgather_weights(w_in_t, w_kv, w_out):
    def body(win_ref, wkv_ref, wout_ref, wint_o, wkv_o, wout_o, send_sems, recv_sems):
        x, y, c = _mesh_pos()
        me, sib = (x, y, c), (x, y, 1 - c)
        chips = [(1 - x, y), (x, 1 - y), (1 - x, 1 - y)]

        def rows(p):
            return wint_o.at[pl.ds(pl.multiple_of(_flat(p) * W_IN_SHARD, 16), W_IN_SHARD), :]

        rows(me)[...] = win_ref[...].astype(BF16)

        def copy(k, block, to):
            return pltpu.make_async_remote_copy(
                src_ref=rows(block), dst_ref=rows(block), send_sem=send_sems.at[k], recv_sem=recv_sems.at[k],
                device_id=to, device_id_type=MESH)

        first = [copy(0, me, sib)] + [copy(1 + j, me, (*chip, c)) for j, chip in enumerate(chips)]
        for cp in first:
            cp.start()
        wkv_o[...] = wkv_ref[...].astype(BF16)
        wout_o[...] = wout_ref[...].astype(BF16)
        passed = []
        for j, chip in enumerate(chips):
            copy(1 + j, (*chip, c), me).wait_recv()
            fwd = copy(4 + j, (*chip, c), sib)
            fwd.start()
            passed.append(fwd)
        copy(0, sib, me).wait_recv()
        for j, chip in enumerate(chips):
            copy(4 + j, (*chip, 1 - c), me).wait_recv()
        for cp in first + passed:
            cp.wait_send()

    vmem = pl.BlockSpec(memory_space=pltpu.VMEM)
    return pl.pallas_call(
        body, name="allgather_weights",
        out_shape=(SDS((IN_COLS, D_MODEL), BF16), SDS(w_kv.shape, BF16), SDS(w_out.shape, BF16)),
        in_specs=[vmem, vmem, vmem], out_specs=(vmem, vmem, vmem),
        scratch_shapes=[pltpu.SemaphoreType.DMA((7,)), pltpu.SemaphoreType.DMA((7,))],
        compiler_params=_params(),
    )(w_in_t, w_kv, w_out)


def _proj_fwd(x2, g_norm, wint):
    tm = 256

    def body(x_ref, g_ref, w_ref, o_ref):
        xv = x_ref[...]
        h = xv * _rstd(xv) * g_ref[...]
        o_ref[...] = _dot_nt(h, w_ref[...])

    return pl.pallas_call(
        body, name="proj_fwd", grid=(T_LOC // tm,),
        in_specs=[pl.BlockSpec((tm, D_MODEL), lambda i: (i, 0)), pl.BlockSpec((1, D_MODEL), lambda i: (0, 0)),
                  pl.BlockSpec((IN_COLS, D_MODEL), lambda i: (0, 0))],
        out_specs=pl.BlockSpec((tm, IN_COLS), lambda i: (i, 0)),
        out_shape=SDS((T_LOC, IN_COLS), F32),
        compiler_params=_params(("arbitrary",)),
    )(x2, g_norm, wint)


def _memkv_fwd(mem2, g_mem, wkv):
    def body(m_ref, g_ref, w_ref, o_ref):
        mv = m_ref[...]
        o_ref[...] = _dot(mv * _rstd(mv) * g_ref[...], w_ref[...])

    return pl.pallas_call(
        body, name="memkv_fwd", out_shape=SDS((B_LOC * N_MEM, 2 * MEM_W), F32), compiler_params=_params(),
    )(mem2, g_mem, wkv)


N_BIAS = 7


def _fill_bias_tables(sl_ref, tab):
    for cfg, (d, length) in enumerate(DILATIONS):
        nk = min(length, 2 * CHUNK)
        r = lax.broadcasted_iota(jnp.int32, (CHUNK, nk), 0)
        c = lax.broadcasted_iota(jnp.int32, (CHUNK, nk), 1)
        for var in range(3 if length > nk else 1):
            rel = jnp.abs(r - c + var * RADIUS)
            dist = rel.astype(F32) * float(d)
            for h in range(2):
                slope = sl_ref[0, 0:1, h * HEAD:h * HEAD + 1]
                tab[3 * cfg + var, h * CHUNK:(h + 1) * CHUNK, 0:nk] = jnp.where(rel <= RADIUS, -slope * dist, NEG)


def _attn_blocks(visit, unroll):
    def step(t, carry):
        for cfg, (d, length) in enumerate(DILATIONS):
            nblk = length // CHUNK
            if nblk == 1:
                visit(cfg, 0, t, t, length, t)
                continue
            rho, i = (0, t) if d == 1 else (t // nblk, t % nblk)
            ks = jnp.clip(i * CHUNK - RADIUS, 0, length - 2 * CHUNK)
            visit(cfg, (i * CHUNK - ks) // RADIUS, rho + d * (i * CHUNK), rho + d * ks, 2 * CHUNK, t)
        return carry
    lax.fori_loop(0, 16, step, 0, unroll=unroll)


def _stack_heads(v, left):
    return jnp.concatenate([jnp.where(left, v, 0.0), jnp.where(left, 0.0, v)], axis=0)


def _unstack_heads(v, left):
    return jnp.where(left, v[0:CHUNK], v[CHUNK:2 * CHUNK])


def _rows(start, n, d):
    return pl.ds(start, n) if d == 1 else pl.ds(start, n, stride=d)


def _attn_fwd(proj, slopes):
    def body(sl_ref, q_ref, k_ref, v_ref, a_ref, lse_ref, *scr):
        o_c, m_c, l_c, tab = scr[0:3], scr[3:6], scr[6:9], scr[9]
        left = _left_lanes(CHUNK)
        _fill_bias_tables(sl_ref, tab)

        def block(cfg, var, q0, k0, nk, t):
            d = DILATIONS[cfg][0]
            rq, rk = _rows(q0, CHUNK, d), _rows(k0, nk, d)
            qs = _stack_heads(q_ref[rq, :] * SCALE, left)
            s = _dot_nt(qs, k_ref[rk, :]) + tab[3 * cfg + var, :, 0:nk]
            m = jnp.max(s, axis=-1, keepdims=True)
            p = jnp.exp(s - m)
            o_c[cfg][rq, :] = _unstack_heads(_dot(p, v_ref[rk, :]), left)
            m_c[cfg][rq, :] = _unstack_heads(m, left)
            l_c[cfg][rq, :] = _unstack_heads(jnp.sum(p, axis=-1, keepdims=True), left)
        _attn_blocks(block, 4)

        def merge(j, carry):
            rows = pl.ds(pl.multiple_of(j * 256, 256), 256)
            ms = [m_c[i][rows, :] for i in range(3)]
            top = jnp.maximum(jnp.maximum(ms[0], ms[1]), ms[2])
            ws = [jnp.exp(m - top) for m in ms]
            den = l_c[0][rows, :] * ws[0] + l_c[1][rows, :] * ws[1] + l_c[2][rows, :] * ws[2]
            num = o_c[0][rows, :] * ws[0] + o_c[1][rows, :] * ws[1] + o_c[2][rows, :] * ws[2]
            a_ref[rows, :] = num / den
            lse_ref[rows, :] = top + jnp.log(den)
            return carry
        lax.fori_loop(0, SEQ // 256, merge, 0)

    blk = lambda col0: pl.BlockSpec((SEQ, 128), lambda b, hp: (b, col0 // 128 + hp))
    out = pl.BlockSpec((SEQ, 128), lambda b, hp: (b, hp))
    return pl.pallas_call(
        body, name="attn_fwd", grid=(B_LOC, 4),
        in_specs=[pl.BlockSpec((1, 8, 128), lambda b, hp: (hp, 0, 0)), blk(C_QA), blk(C_KA), blk(C_VA)],
        out_specs=(out, out),
        out_shape=(SDS((T_LOC, ATTN_W), F32), SDS((T_LOC, ATTN_W), F32)),
        scratch_shapes=[pltpu.VMEM((SEQ, 128), F32)] * 9 + [pltpu.VMEM((N_BIAS, 2 * CHUNK, 2 * CHUNK), F32)],
        compiler_params=_params(("arbitrary", "arbitrary")),
    )(slopes, proj, proj, proj)


def _sgu_mix(vn, ws_ref, dst_ref, tm):
    left = _left_lanes(CHUNK)
    for ch in range(tm // CHUNK):
        for pr in range(2):
            vp = vn[ch * CHUNK:(ch + 1) * CHUNK, pr * 128:(pr + 1) * 128]
            dst_ref[ch * CHUNK:(ch + 1) * CHUNK, pr * 128:(pr + 1) * 128] = jnp.where(
                left, _dot(ws_ref[2 * pr], vp), _dot(ws_ref[2 * pr + 1], vp))


def _mem_attn_head(qp, kp, h, left):
    qh = jnp.where(left if h == 0 else ~left, qp, 0.0)
    s = _dot_nt(qh, kp) * SCALE
    e = jnp.exp(s - jnp.max(s, axis=-1, keepdims=True))
    return e * (1.0 / jnp.sum(e, axis=-1, keepdims=True)), qh


def _branch_blocks(tm):
    col = lambda w, c0: pl.BlockSpec((tm, w), lambda i: (i, c0 // w))
    return [col(512, C_ZA), col(256, C_UB), col(256, C_VB), col(256, C_ZB), col(256, C_QM), col(256, C_ZM)]


def _branch_fwd(proj, a, kv, w_s, b_exp, g_v):
    tm = 256
    per_ex = SEQ // tm

    def body(za_ref, ub_ref, vb_ref, zb_ref, qm_ref, zm_ref, a_ref, kv_ref, ws_ref, be_ref, gv_ref, o_ref, mix):
        left = _left_lanes(tm)
        o_ref[:, 0:ATTN_W] = (_silu_and_grad(za_ref[...])[0] * a_ref[...]).astype(BF16)
        gu = _gelu_and_grad(ub_ref[...])[0]
        gv = _gelu_and_grad(vb_ref[...])[0]
        vn = gv * _rstd(gv) * gv_ref[...]
        _sgu_mix(vn.astype(BF16), ws_ref, mix, tm)
        sg = gu * (mix[...] + be_ref[...])
        o_ref[:, ATTN_W:ATTN_W + SGU_W] = (_silu_and_grad(zb_ref[...])[0] * sg).astype(BF16)
        szm = _silu_and_grad(zm_ref[...])[0]
        for hp in range(2):
            cols = slice(hp * 128, (hp + 1) * 128)
            qp, kp, vp = qm_ref[:, cols], kv_ref[:, cols], kv_ref[:, MEM_W + hp * 128:MEM_W + (hp + 1) * 128]
            o = [_dot(_mem_attn_head(qp, kp, h, left)[0], vp) for h in range(2)]
            c0 = ATTN_W + SGU_W + hp * 128
            o_ref[:, c0:c0 + 128] = (szm[:, cols] * jnp.where(left, o[0], o[1])).astype(BF16)

    full = lambda shape: pl.BlockSpec(shape, lambda i: (0,) * len(shape))
    return pl.pallas_call(
        body, name="branch_fwd", grid=(T_LOC // tm,),
        in_specs=_branch_blocks(tm) + [
            pl.BlockSpec((tm, ATTN_W), lambda i: (i, 0)), pl.BlockSpec((N_MEM, 2 * MEM_W), lambda i: (i // per_ex, 0)),
            full((4, CHUNK, CHUNK)), full((tm, SGU_W)), full((1, SGU_W))],
        out_specs=pl.BlockSpec((tm, D_MODEL), lambda i: (i, 0)),
        out_shape=SDS((T_LOC, D_MODEL), BF16),
        scratch_shapes=[pltpu.VMEM((tm, SGU_W), F32)],
        compiler_params=_params(("arbitrary",)),
    )(proj, proj, proj, proj, proj, proj, a, kv, w_s, b_exp, g_v)


def _outproj_loss(gated, wout, x2, tgt2, g_final):
    tm = 512

    def body(g_ref, w_ref, x_ref, t_ref, gf_ref, dh2_ref, loss_ref, dgf_ref):
        @pl.when(pl.program_id(0) == 0)
        def _():
            loss_ref[...] = jnp.zeros_like(loss_ref)
            dgf_ref[...] = jnp.zeros_like(dgf_ref)
        h2 = x_ref[...] + _dot(g_ref[...], w_ref[...])
        r = _rstd(h2)
        gf = gf_ref[...]
        err = h2 * r * gf - t_ref[...]
        loss_ref[...] += 0.5 * jnp.sum(jnp.mean(err * err, axis=-1, keepdims=True))
        dy = err * (1.0 / D_MODEL)
        dh2_ref[...] = _rms_bwd(h2, r, gf, dy)
        dgf_ref[...] += jnp.sum(dy * (h2 * r), axis=0, keepdims=True)

    row = pl.BlockSpec((tm, D_MODEL), lambda i: (i, 0))
    vec = pl.BlockSpec((1, D_MODEL), lambda i: (0, 0))
    return pl.pallas_call(
        body, name="outproj_loss", grid=(T_LOC // tm,),
        in_specs=[row, pl.BlockSpec((D_MODEL, D_MODEL), lambda i: (0, 0)), row, row, vec],
        out_specs=(row, pl.BlockSpec((8, 128), lambda i: (0, 0)), vec),
        out_shape=(SDS((T_LOC, D_MODEL), F32), SDS((8, 128), F32), SDS((1, D_MODEL), F32)),
        compiler_params=_params(("arbitrary",)),
    )(gated, wout, x2, tgt2, g_final)


def _branch_bwd(dh2, wout, gated, proj, a, kv, w_s, b_exp, g_v):
    tm = 256
    per_ex = SEQ // tm

    def body(dh2_ref, w_ref, g_ref, za_ref, ub_ref, vb_ref, zb_ref, qm_ref, zm_ref, a_ref, kv_ref, ws_ref,
             be_ref, gv_ref, da_ref, dr_ref, dkv_ref, dwo_ref, dws_ref, db_ref, dgv_ref, mix, dvn, dmsum, dwo_acc):
        i = pl.program_id(0)
        left = _left_lanes(tm)
        leftc = _left_lanes(CHUNK)

        @pl.when(i == 0)
        def _():
            dwo_acc[...] = jnp.zeros_like(dwo_acc)
            dws_ref[...] = jnp.zeros_like(dws_ref)
            dgv_ref[...] = jnp.zeros_like(dgv_ref)
            dmsum[...] = jnp.zeros_like(dmsum)

        @pl.when(i % per_ex == 0)
        def _():
            dkv_ref[...] = jnp.zeros_like(dkv_ref)

        dh2 = dh2_ref[...].astype(BF16)
        dwo_acc[...] += _dot_tn(g_ref[...], dh2)
        dg = _dot_nt(dh2, w_ref[...])

        sa, dsa = _silu_and_grad(za_ref[...])
        dga = dg[:, 0:ATTN_W]
        da_ref[...] = dga * sa
        dr_ref[:, 0:512] = (dga * a_ref[...] * dsa).astype(BF16)

        ub, vb = ub_ref[...], vb_ref[...]
        gu, dgu = _gelu_and_grad(ub)
        gv, dgv = _gelu_and_grad(vb)
        rv = _rstd(gv)
        gain = gv_ref[...]
        vn = (gv * rv * gain).astype(BF16)
        _sgu_mix(vn, ws_ref, mix, tm)
        mixed = mix[...] + be_ref[...]
        sb, dsb = _silu_and_grad(zb_ref[...])
        dgb = dg[:, ATTN_W:ATTN_W + SGU_W]
        dsg = dgb * sb
        dr_ref[:, 512:768] = (dsg * mixed * dgu).astype(BF16)
        dr_ref[:, 1024:1280] = (dgb * (gu * mixed) * dsb).astype(BF16)
        dmix = dsg * gu
        for ch in range(tm // CHUNK):
            rows = slice(ch * CHUNK, (ch + 1) * CHUNK)
            dmsum[...] += dmix[rows, :]
            for pr in range(2):
                cols = slice(pr * 128, (pr + 1) * 128)
                dmp, vp = dmix[rows, cols], vn[rows, cols]
                for h in range(2):
                    g = 2 * pr + h
                    dws_ref[g] += _dot_nt(jnp.where(leftc if h == 0 else ~leftc, dmp, 0.0), vp)
                dvn[rows, cols] = jnp.where(leftc, _dot_tn(ws_ref[2 * pr], dmp), _dot_tn(ws_ref[2 * pr + 1], dmp))
        dvn_v = dvn[...]
        dgv_ref[...] += jnp.sum(dvn_v * (gv * rv), axis=0, keepdims=True)
        dr_ref[:, 768:1024] = (_rms_bwd(gv, rv, gain, dvn_v) * dgv).astype(BF16)

        szm, dszm = _silu_and_grad(zm_ref[...])
        dgm = dg[:, ATTN_W + SGU_W:D_MODEL]
        dmo = dgm * szm
        for hp in range(2):
            cols = slice(hp * 128, (hp + 1) * 128)
            vcols = slice(MEM_W + hp * 128, MEM_W + (hp + 1) * 128)
            qp, kp, vp, dmop = qm_ref[:, cols], kv_ref[:, cols], kv_ref[:, vcols], dmo[:, cols]
            o, dq = [], []
            dk = jnp.zeros((N_MEM, 128), F32)
            dv = jnp.zeros((N_MEM, 128), F32)
            for h in range(2):
                p, qh = _mem_attn_head(qp, kp, h, left)
                dmoh = jnp.where(left if h == 0 else ~left, dmop, 0.0)
                o.append(_dot(p, vp))
                dp = _dot_nt(dmoh, vp)
                ds = p * (dp - jnp.sum(p * dp, axis=-1, keepdims=True)) * SCALE
                dq.append(_dot(ds, kp))
                dk += _dot_tn(ds, qh)
                dv += _dot_tn(p, dmoh)
            dr_ref[:, 1280 + hp * 128:1280 + (hp + 1) * 128] = jnp.where(left, dq[0], dq[1]).astype(BF16)
            dr_ref[:, 1536 + hp * 128:1536 + (hp + 1) * 128] = (
                dgm[:, cols] * jnp.where(left, o[0], o[1]) * dszm[:, cols]).astype(BF16)
            dkv_ref[:, cols] += dk
            dkv_ref[:, vcols] += dv

        @pl.when(i == pl.num_programs(0) - 1)
        def _():
            tot = dmsum[...]
            hi = tot.astype(BF16)
            lo = (tot - hi.astype(F32)).astype(BF16)
            grp = (lax.broadcasted_iota(jnp.int32, (SGU_W, 128), 0) // HEAD
                   == lax.broadcasted_iota(jnp.int32, (SGU_W, 128), 1)).astype(BF16)
            db_ref[...] = (_dot(hi, grp) + _dot(lo, grp)).T[0:4, :]
            _cast_rows(dwo_acc, dwo_ref, D_MODEL)

    full = lambda shape: pl.BlockSpec(shape, lambda i: (0,) * len(shape))
    row = lambda w: pl.BlockSpec((tm, w), lambda i: (i, 0))
    return pl.pallas_call(
        body, name="branch_bwd", grid=(T_LOC // tm,),
        in_specs=[row(D_MODEL), full((D_MODEL, D_MODEL)), row(D_MODEL)] + _branch_blocks(tm) + [
            row(ATTN_W), pl.BlockSpec((N_MEM, 2 * MEM_W), lambda i: (i // per_ex, 0)),
            full((4, CHUNK, CHUNK)), full((tm, SGU_W)), full((1, SGU_W))],
        out_specs=(row(ATTN_W), row(REST_W), pl.BlockSpec((N_MEM, 2 * MEM_W), lambda i: (i // per_ex, 0)),
                   full((D_MODEL, D_MODEL)), full((4, CHUNK, CHUNK)), full((4, CHUNK)), full((1, SGU_W))),
        out_shape=(SDS((T_LOC, ATTN_W), F32), SDS((T_LOC, REST_W), BF16), SDS((B_LOC * N_MEM, 2 * MEM_W), F32),
                   SDS((D_MODEL, D_MODEL), BF16), SDS((4, CHUNK, CHUNK), F32), SDS((4, CHUNK), F32), SDS((1, SGU_W), F32)),
        scratch_shapes=[pltpu.VMEM((tm, SGU_W), F32), pltpu.VMEM((tm, SGU_W), F32), pltpu.VMEM((CHUNK, SGU_W), F32),
                        pltpu.VMEM((D_MODEL, D_MODEL), F32)],
        compiler_params=_params(("arbitrary",)),
    )(dh2, wout, gated, proj, proj, proj, proj, proj, proj, a, kv, w_s, b_exp, g_v)


def _attn_bwd(proj, slopes, da, a, lse):
    def body(sl_ref, q_ref, k_ref, v_ref, da_ref, a_ref, lse_ref, dq_ref, dk_ref, dv_ref, *scr):
        dq_s, dk_s, dv_s, tab = scr[0:3], scr[3:6], scr[6:9], scr[9]
        lse_h, delta_h = scr[10:12], scr[12:14]
        p_all, ds_all = scr[14], scr[15]
        left = _left_lanes(CHUNK)
        _fill_bias_tables(sl_ref, tab)

        def prep(j, carry):
            rows = pl.ds(pl.multiple_of(j * 256, 256), 256)
            l256 = _left_lanes(256)
            prod = da_ref[rows, :] * a_ref[rows, :]
            delta_h[0][rows, :] = jnp.broadcast_to(jnp.sum(jnp.where(l256, prod, 0.0), axis=-1, keepdims=True), (256, 128))
            delta_h[1][rows, :] = jnp.broadcast_to(jnp.sum(jnp.where(l256, 0.0, prod), axis=-1, keepdims=True), (256, 128))
            pair = lse_ref[rows, :]
            other = pltpu.roll(pair, HEAD, axis=1)
            lse_h[0][rows, :] = jnp.where(l256, pair, other)
            lse_h[1][rows, :] = jnp.where(l256, other, pair)
            zero = jnp.zeros((256, 128), F32)
            for cfg in range(3):
                dk_s[cfg][rows, :] = zero
                dv_s[cfg][rows, :] = zero
            return carry
        lax.fori_loop(0, SEQ // 256, prep, 0)

        def per_row(halves, rq, nk):
            v = jnp.concatenate([halves[0][rq, :], halves[1][rq, :]], axis=0)
            return v if nk == 128 else jnp.concatenate([v, v], axis=1)

        def probs(cfg, var, q0, k0, nk, t):
            d = DILATIONS[cfg][0]
            rq, rk = _rows(q0, CHUNK, d), _rows(k0, nk, d)
            qs = _stack_heads(q_ref[rq, :] * SCALE, left)
            das = _stack_heads(da_ref[rq, :], left)
            s = _dot_nt(qs, k_ref[rk, :]) + tab[3 * cfg + var, :, 0:nk]
            p = jnp.exp(s - per_row(lse_h, rq, nk))
            p_all[16 * cfg + t, :, 0:nk] = p.astype(BF16)
            ds_all[16 * cfg + t, :, 0:nk] = (p * (_dot_nt(das, v_ref[rk, :]) - per_row(delta_h, rq, nk))).astype(BF16)
        _attn_blocks(probs, 4)

        def grads(cfg, var, q0, k0, nk, t):
            d = DILATIONS[cfg][0]
            rq, rk = _rows(q0, CHUNK, d), _rows(k0, nk, d)
            qs = _stack_heads(q_ref[rq, :] * SCALE, left).astype(BF16)
            das = _stack_heads(da_ref[rq, :], left).astype(BF16)
            p, ds = p_all[16 * cfg + t, :, 0:nk], ds_all[16 * cfg + t, :, 0:nk]
            dq_s[cfg][rq, :] = _unstack_heads(_dot(ds, k_ref[rk, :]), left) * SCALE
            dk_s[cfg][rk, :] += _dot_tn(ds, qs)
            dv_s[cfg][rk, :] += _dot_tn(p, das)
        _attn_blocks(grads, 4)

        def flush(j, carry):
            rows = pl.ds(pl.multiple_of(j * 256, 256), 256)
            for acc, dst in ((dq_s, dq_ref), (dk_s, dk_ref), (dv_s, dv_ref)):
                dst[rows, :] = (acc[0][rows, :] + acc[1][rows, :] + acc[2][rows, :]).astype(BF16)
            return carry
        lax.fori_loop(0, SEQ // 256, flush, 0)

    blk = lambda col0: pl.BlockSpec((SEQ, 128), lambda b, hp: (b, col0 // 128 + hp))
    own = pl.BlockSpec((SEQ, 128), lambda b, hp: (b, hp))
    return pl.pallas_call(
        body, name="attn_bwd", grid=(B_LOC, 4),
        in_specs=[pl.BlockSpec((1, 8, 128), lambda b, hp: (hp, 0, 0)), blk(C_QA), blk(C_KA), blk(C_VA), own, own, own],
        out_specs=(own, own, own),
        out_shape=(SDS((T_LOC, ATTN_W), BF16),) * 3,
        scratch_shapes=[pltpu.VMEM((SEQ, 128), F32)] * 9 + [pltpu.VMEM((N_BIAS, 2 * CHUNK, 2 * CHUNK), F32)]
        + [pltpu.VMEM((SEQ, 128), F32)] * 4 + [pltpu.VMEM((48, 2 * CHUNK, 2 * CHUNK), BF16)] * 2,
        compiler_params=_params(("arbitrary", "arbitrary")),
    )(slopes, proj, proj, proj, da, a, lse)


def _dproj_specs(tm):
    third = pl.BlockSpec((tm, ATTN_W), lambda i: (i, 0))
    return [third, third, third, pl.BlockSpec((tm, REST_W), lambda i: (i, 0))]


def _dx(dq, dk, dv, dr, wint, x2, dh2, g_norm):
    tm = 256

    def body(dq_ref, dk_ref, dv_ref, dr_ref, w_ref, x_ref, dh2_ref, g_ref, gx_ref, dgn_ref):
        @pl.when(pl.program_id(0) == 0)
        def _():
            dgn_ref[...] = jnp.zeros_like(dgn_ref)
        dh = (_dot(dq_ref[...], w_ref[C_QA:C_KA, :]) + _dot(dk_ref[...], w_ref[C_KA:C_VA, :])
              + _dot(dv_ref[...], w_ref[C_VA:C_ZA, :]) + _dot(dr_ref[...], w_ref[C_ZA:IN_COLS, :]))
        xv = x_ref[...]
        r = _rstd(xv)
        gx_ref[...] = dh2_ref[...] + _rms_bwd(xv, r, g_ref[...], dh)
        dgn_ref[...] += jnp.sum(dh * (xv * r), axis=0, keepdims=True)

    row = pl.BlockSpec((tm, D_MODEL), lambda i: (i, 0))
    vec = pl.BlockSpec((1, D_MODEL), lambda i: (0, 0))
    return pl.pallas_call(
        body, name="dx", grid=(T_LOC // tm,),
        in_specs=_dproj_specs(tm) + [pl.BlockSpec((IN_COLS, D_MODEL), lambda i: (0, 0)), row, row, vec],
        out_specs=(row, vec),
        out_shape=(SDS((T_LOC, D_MODEL), F32), SDS((1, D_MODEL), F32)),
        compiler_params=_params(("arbitrary",)),
    )(dq, dk, dv, dr, wint, x2, dh2, g_norm)


def _dwin(dq, dk, dv, dr, x2, g_norm):
    tm = 512

    def body(dq_ref, dk_ref, dv_ref, dr_ref, x_ref, g_ref, o_ref, acc):
        @pl.when(pl.program_id(0) == 0)
        def _():
            acc[...] = jnp.zeros_like(acc)
        xv = x_ref[...]
        h = (xv * _rstd(xv) * g_ref[...]).astype(BF16)
        acc[C_QA:C_KA, :] += _dot_tn(dq_ref[...], h)
        acc[C_KA:C_VA, :] += _dot_tn(dk_ref[...], h)
        acc[C_VA:C_ZA, :] += _dot_tn(dv_ref[...], h)
        acc[C_ZA:IN_COLS, :] += _dot_tn(dr_ref[...], h)

        @pl.when(pl.program_id(0) == pl.num_programs(0) - 1)
        def _():
            _cast_rows(acc, o_ref, IN_COLS)

    return pl.pallas_call(
        body, name="dwin", grid=(T_LOC // tm,),
        in_specs=_dproj_specs(tm) + [pl.BlockSpec((tm, D_MODEL), lambda i: (i, 0)), pl.BlockSpec((1, D_MODEL), lambda i: (0, 0))],
        out_specs=pl.BlockSpec((IN_COLS, D_MODEL), lambda i: (0, 0)),
        out_shape=SDS((IN_COLS, D_MODEL), BF16),
        scratch_shapes=[pltpu.VMEM((IN_COLS, D_MODEL), F32)],
        compiler_params=_params(("arbitrary",)),
    )(dq, dk, dv, dr, x2, g_norm)


def _memkv_bwd(dkv, mem2, g_mem, wkv):
    def body(dkv_ref, m_ref, g_ref, w_ref, dw_ref, dg_ref):
        mv = m_ref[...]
        r = _rstd(mv)
        dkv_v = dkv_ref[...].astype(BF16)
        dw_ref[...] = _dot_tn(mv * r * g_ref[...], dkv_v).astype(BF16)
        dg_ref[...] = jnp.sum(_dot_nt(dkv_v, w_ref[...]) * (mv * r), axis=0, keepdims=True)

    return pl.pallas_call(
        body, name="memkv_bwd", out_shape=(SDS((D_MODEL, 2 * MEM_W), BF16), SDS((1, D_MODEL), F32)),
        compiler_params=_params(),
    )(dkv, mem2, g_mem, wkv)


def _allreduce_small(parts):
    n = len(parts)

    def body(*refs):
        ins, outs, bufs = refs[0:n], refs[n:2 * n], refs[2 * n:3 * n]
        send_sems, recv_sems = refs[3 * n], refs[3 * n + 1]
        pos = _mesh_pos()
        me = _flat(pos)
        for a in range(n):
            bufs[a][me] = ins[a][...]

        def copy(a, k, slot):
            return pltpu.make_async_remote_copy(
                src_ref=ins[a], dst_ref=bufs[a].at[slot],
                send_sem=send_sems.at[7 * a + k - 1], recv_sem=recv_sems.at[7 * a + k - 1],
                device_id=_peer(pos, k), device_id_type=MESH)

        sent = [copy(a, k, me) for a in range(n) for k in range(1, N_DEV)]
        for cp in sent:
            cp.start()
        for a in range(n):
            for k in range(1, N_DEV):
                copy(a, k, _flat(_peer(pos, k))).wait_recv()
        for cp in sent:
            cp.wait_send()
        for a in range(n):
            acc = bufs[a][0]
            for s in range(1, N_DEV):
                acc = acc + bufs[a][s]
            outs[a][...] = acc

    vmem = pl.BlockSpec(memory_space=pltpu.VMEM)
    return pl.pallas_call(
        body, name="allreduce_small",
        out_shape=tuple(SDS(p.shape, F32) for p in parts),
        in_specs=[vmem] * n, out_specs=(vmem,) * n,
        scratch_shapes=[pltpu.VMEM((N_DEV,) + p.shape, F32) for p in parts]
        + [pltpu.SemaphoreType.DMA((7 * n,)), pltpu.SemaphoreType.DMA((7 * n,))],
        compiler_params=_params(),
    )(*parts)


_HBM = pl.BlockSpec(memory_space=pltpu.HBM)
_SEM = pl.BlockSpec(memory_space=pltpu.SEMAPHORE)
_SIDE_EFFECT = pltpu.SideEffectType.DATAFLOW_SIDE_EFFECTING


def _exchange_copies(src_refs, land_refs, scatter, send_sems, recv_sems):
    pos = _mesh_pos()
    copies = []
    for a, (src, land) in enumerate(zip(src_refs, land_refs)):
        n = land.shape[1]
        for k in range(1, N_DEV):
            peer = _peer(pos, k)
            piece = src.at[pl.ds(pl.multiple_of(_flat(peer) * n, 16), n), :] if scatter[a] else src
            copies.append(pltpu.make_async_remote_copy(
                src_ref=piece, dst_ref=land.at[_flat(pos)],
                send_sem=send_sems.at[7 * a + k - 1], recv_sem=recv_sems.at[7 * a + k - 1],
                device_id=peer, device_id_type=MESH))
    return copies


def _exchange_start(name, srcs, scatter, lands):
    n = len(srcs)

    def body(*refs):
        for cp in _exchange_copies(refs[0:n], refs[n:2 * n], scatter, refs[2 * n], refs[2 * n + 1]):
            cp.start()
        refs[-1][...] = jnp.zeros_like(refs[-1])

    ops = [pltpu.with_memory_space_constraint(t, pltpu.HBM) for t in (*srcs, *lands)]
    out = pl.pallas_call(
        body, name=name,
        out_shape=(pltpu.SemaphoreType.DMA((7 * n,)), pltpu.SemaphoreType.DMA((7 * n,)),
                   *[pltpu.HBM(t.shape, t.dtype) for t in ops], SDS((8, 128), F32)),
        in_specs=[_HBM] * (2 * n),
        out_specs=(_SEM, _SEM, *[_HBM] * (2 * n), pl.BlockSpec(memory_space=pltpu.VMEM)),
        input_output_aliases={i: 2 + i for i in range(2 * n)},
        compiler_params=pltpu.CompilerParams(has_side_effects=_SIDE_EFFECT),
    )(*ops)
    return out[0], out[1], out[2:2 + n], out[2 + n:2 + 2 * n], out[-1]


def _exchange_wait(name, started, scatter, after):
    send_sems, recv_sems, srcs, lands, _ = started
    n = len(srcs)

    def body(*refs):
        for cp in _exchange_copies(refs[0:n], refs[n:2 * n], scatter, refs[2 * n], refs[2 * n + 1]):
            cp.wait_send()
            cp.wait_recv()

    out = pl.pallas_call(
        body, name=name,
        out_shape=tuple(pltpu.HBM(t.shape, t.dtype) for t in (*srcs, *lands)),
        in_specs=[_HBM] * (2 * n) + [_SEM, _SEM, pl.BlockSpec(memory_space=pl.ANY)],
        out_specs=(_HBM,) * (2 * n),
        input_output_aliases={i: i for i in range(2 * n)},
        compiler_params=pltpu.CompilerParams(has_side_effects=_SIDE_EFFECT),
    )(*srcs, *lands, send_sems, recv_sems, after)
    return out[n:]


def _landing(own, me):
    return lax.dynamic_update_slice(lax.empty((N_DEV,) + own.shape, own.dtype), own[None], (me,) + (0,) * own.ndim)


def _adamw(w, g, m, v):
    m = ADAM_B1 * m + (1.0 - ADAM_B1) * g
    v = ADAM_B2 * v + (1.0 - ADAM_B2) * (g * g)
    m_hat = m / (1.0 - ADAM_B1 ** ADAM_STEP)
    v_hat = v / (1.0 - ADAM_B2 ** ADAM_STEP)
    return -ADAM_LR * (m_hat / (jnp.sqrt(v_hat) + ADAM_EPS) + ADAM_WD * w), m, v


def _adam_slots(name, slots, w, m, v):
    _, rows, cols = slots.shape

    def body(s_ref, w_ref, m_ref, v_ref, g_o, d_o, m_o, v_o, acc):
        s = pl.program_id(0)

        @pl.when(s == 0)
        def _():
            acc[...] = s_ref[0].astype(F32)

        @pl.when(s > 0)
        def _():
            acc[...] += s_ref[0].astype(F32)

        @pl.when(s == N_DEV - 1)
        def _():
            g = acc[...]
            g_o[...] = g
            d_o[...], m_o[...], v_o[...] = _adamw(w_ref[...], g, m_ref[...], v_ref[...])

    full = pl.BlockSpec((rows, cols), lambda s: (0, 0))
    return pl.pallas_call(
        body, name="adam_" + name, grid=(N_DEV,),
        in_specs=[pl.BlockSpec((1, rows, cols), lambda s: (s, 0, 0)), full, full, full],
        out_specs=(full,) * 4, out_shape=(SDS((rows, cols), F32),) * 4,
        scratch_shapes=[pltpu.VMEM((rows, cols), F32)],
        compiler_params=_params(("arbitrary",)),
    )(slots, w, m, v)


def _adam_small(ws, gs, ms, vs, loss_slots):
    n = len(ws)

    def total(ref, like):
        if len(ref.shape) == len(like.shape):
            return ref[...]
        acc = ref[0]
        for s in range(1, N_DEV):
            acc = acc + ref[s]
        return acc

    def body(*refs):
        w_r, g_r, m_r, v_r = refs[0:n], refs[n:2 * n], refs[2 * n:3 * n], refs[3 * n:4 * n]
        loss_r, outs = refs[4 * n], refs[4 * n + 1:]
        for a in range(n):
            g = total(g_r[a], w_r[a])
            outs[a][...] = g
            outs[n + 1 + 3 * a][...], outs[n + 2 + 3 * a][...], outs[n + 3 + 3 * a][...] = _adamw(
                w_r[a][...], g, m_r[a][...], v_r[a][...])
        outs[n][...] = total(loss_r, outs[n])

    out = pl.pallas_call(
        body, name="adam_small",
        out_shape=tuple(SDS(w.shape, F32) for w in ws) + (SDS(loss_slots.shape[1:], F32),)
        + tuple(SDS(w.shape, F32) for w in ws for _ in range(3)),
        compiler_params=_params(),
    )(*ws, *gs, *ms, *vs, loss_slots)
    return out[0:n], out[n], out[n + 1:]


def kernel(x, mem, g_norm, w_in, w_sgu_spatial, b_sgu_spatial, g_sgu_v, g_mem, w_mem_kv, w_out, g_final, loss_target, m_g_norm, m_w_in, m_w_sgu_spatial, m_b_sgu_spatial, m_g_sgu_v, m_g_mem, m_w_mem_kv, m_w_out, m_g_final, v_g_norm, v_w_in, v_w_sgu_spatial, v_b_sgu_spatial, v_g_sgu_v, v_g_mem, v_w_mem_kv, v_w_out, v_g_final):
    x2 = x.reshape(T_LOC, D_MODEL)
    tgt2 = loss_target.reshape(T_LOC, D_MODEL)
    mem2 = mem.reshape(B_LOC * N_MEM, D_MODEL)
    w_s = w_sgu_spatial[0]
    b_exp = jnp.tile(jnp.repeat(b_sgu_spatial[0].T, HEAD, axis=1), (2, 1))
    slope = jnp.power(2.0, -8.0 * (jnp.arange(8, dtype=F32) + 1.0) / 8)
    slopes = jnp.broadcast_to(jnp.repeat(slope.reshape(4, 2), HEAD, axis=1)[:, None, :], (4, 8, 128))

    tr = lambda t: jnp.transpose(t[0])

    me = _flat(_mesh_pos())
    own_rows = lambda t: lax.dynamic_slice_in_dim(t, me * (t.shape[0] // N_DEV), t.shape[0] // N_DEV)

    wint, wkv_own, wout_own = _allgather_weights(tr(w_in), w_mem_kv[0], w_out[0])
    started0 = _exchange_start("exchange0_start", [wkv_own, wout_own], [False, False],
                               [_landing(wkv_own, me), _landing(wout_own, me)])
    proj = _proj_fwd(x2, g_norm + started0[4][0:1, 0:1], wint)
    wkv, wout = _exchange_wait("exchange0_wait", started0, [False, False], proj)
    wkv, wout = wkv.reshape(D_MODEL, 2 * MEM_W), wout.reshape(D_MODEL, D_MODEL)
    kv = _memkv_fwd(mem2, g_mem, wkv)
    a, lse = _attn_fwd(proj, slopes)
    gated = _branch_fwd(proj, a, kv, w_s, b_exp, g_sgu_v)
    dh2, loss8, dgf = _outproj_loss(gated, wout, x2, tgt2, g_final.reshape(1, D_MODEL))

    da, dr, dkv, dwout, dws, dbs, dgv = _branch_bwd(dh2, wout, gated, proj, a, kv, w_s, b_exp, g_sgu_v)
    dwkv, dgm = _memkv_bwd(dkv, mem2, g_mem, wkv)

    early = [dws.reshape(4 * CHUNK, CHUNK), dbs, dgv, dgm, dgf, loss8]
    scatter1 = [True, True] + [False] * len(early)
    started1 = _exchange_start(
        "exchange1_start", [dwkv, dwout] + early, scatter1,
        [_landing(own_rows(dwkv), me), _landing(own_rows(dwout), me)] + [_landing(t, me) for t in early])
    dq, dk, dv = _attn_bwd(proj, slopes + started1[4][0:1, 0:1], da, a, lse)
    s_wkv, s_wout, s_ws, s_bs, s_gv, s_gm, s_gf, s_loss = _exchange_wait("exchange1_wait", started1, scatter1, dq)

    dwint = _dwin(dq, dk, dv, dr, x2, g_norm)
    started2 = _exchange_start("exchange2_start", [dwint], [True], [_landing(own_rows(dwint), me)])
    grad_x, dgn = _dx(dq, dk, dv, dr, wint, x2, dh2, g_norm + started2[4][0:1, 0:1])
    dgn_sum, = _allreduce_small([dgn])
    s_win, = _exchange_wait("exchange2_wait", started2, [True], dgn_sum)

    g_win, d_win, m_win, v_win = map(
        jnp.transpose, _adam_slots("w_in", s_win, tr(w_in), tr(m_w_in), tr(v_w_in)))
    g_wkv, d_wkv, m_wkv, v_wkv = _adam_slots("w_mem_kv", s_wkv, w_mem_kv[0], m_w_mem_kv[0], v_w_mem_kv[0])
    g_wout, d_wout, m_wout, v_wout = _adam_slots("w_out", s_wout, w_out[0], m_w_out[0], v_w_out[0])

    small_shapes = [(1, D_MODEL), (4 * CHUNK, CHUNK), (4, CHUNK), (1, SGU_W), (1, D_MODEL), (1, D_MODEL)]
    pack = lambda arrs: [t.reshape(s) for t, s in zip(arrs, small_shapes)]
    g_small, loss_sum, upd = _adam_small(
        pack([g_norm, w_sgu_spatial, b_sgu_spatial, g_sgu_v, g_mem, g_final]),
        [dgn_sum, s_ws, s_bs, s_gv, s_gm, s_gf],
        pack([m_g_norm, m_w_sgu_spatial, m_b_sgu_spatial, m_g_sgu_v, m_g_mem, m_g_final]),
        pack([v_g_norm, v_w_sgu_spatial, v_b_sgu_spatial, v_g_sgu_v, v_g_mem, v_g_final]), s_loss)
    out_shapes = [g_norm.shape, w_sgu_spatial.shape, b_sgu_spatial.shape, g_sgu_v.shape, g_mem.shape, g_final.shape]
    unpack = lambda arrs: [t.reshape(s) for t, s in zip(arrs, out_shapes)]
    gs = unpack(g_small)
    ds, nms, nvs = unpack(upd[0::3]), unpack(upd[1::3]), unpack(upd[2::3])

    loss = loss_sum[0, 0]

    def assemble(small, win, wkv_, wout_):
        return [small[0], win[None], small[1], small[2], small[3], small[4], wkv_[None], wout_[None], small[5]]

    return (loss, grad_x.reshape(x.shape),
            *assemble(gs, g_win, g_wkv, g_wout), *assemble(ds, d_win, d_wkv, d_wout),
            *assemble(nms, m_win, m_wkv, m_wout), *assemble(nvs, v_win, v_wkv, v_wout))
```

```python
import functools

import jax
import jax.numpy as jnp
from jax import lax
from jax.experimental import pallas as pl
from jax.experimental.pallas import tpu as pltpu

F32 = jnp.float32
BF16 = jnp.bfloat16
SDS = jax.ShapeDtypeStruct
MESH = pl.DeviceIdType.MESH

N_DEV = 8
D_MODEL = 1024
SEQ = 2048
B_LOC = 2
T_LOC = B_LOC * SEQ
N_MEM = 256
HEAD = 64
ATTN_W = 512
SGU_W = 256
MEM_W = 256
IN_COLS = 3328
W_IN_SHARD = IN_COLS // N_DEV
ROW_SHARD = D_MODEL // N_DEV
CHUNK = 128
DILATIONS = ((1, 2048), (4, 512), (16, 128))
RADIUS = 64
EPS = 1e-6
NEG = -1e30
SCALE = HEAD ** -0.5
C_QA, C_KA, C_VA, C_ZA, C_UB, C_VB, C_ZB, C_QM, C_ZM = 0, 512, 1024, 1536, 2048, 2304, 2560, 2816, 3072
QKV_W = 1536
REST_W = IN_COLS - QKV_W

ADAM_LR, ADAM_B1, ADAM_B2, ADAM_EPS, ADAM_WD, ADAM_STEP = 0.001, 0.9, 0.999, 1e-08, 0.01, 10

V7X_VMEM_MIB = 64
VMEM_STREAMING_MIB = V7X_VMEM_MIB - 6


def _params(sem=None):
    if sem is None:
        return pltpu.CompilerParams()
    return pltpu.CompilerParams(vmem_limit_bytes=VMEM_STREAMING_MIB << 20, dimension_semantics=sem)


def _dot(a, b):
    return jnp.dot(a.astype(BF16), b.astype(BF16), preferred_element_type=F32)


def _dot_nt(a, b):
    return lax.dot_general(a.astype(BF16), b.astype(BF16), (((1,), (1,)), ((), ())), preferred_element_type=F32)


def _dot_tn(a, b):
    return lax.dot_general(a.astype(BF16), b.astype(BF16), (((0,), (0,)), ((), ())), preferred_element_type=F32)


def _rstd(v):
    return lax.rsqrt(jnp.mean(v * v, axis=-1, keepdims=True) + EPS)


def _rms_bwd(v, r, g, dy):
    gdy = g * dy
    return r * gdy - v * (r * r * r * jnp.mean(gdy * v, axis=-1, keepdims=True))


def _sigmoid(z):
    return 1.0 / (1.0 + jnp.exp(-z))


def _silu_and_grad(z):
    s = _sigmoid(z)
    return z * s, s * (1.0 + z * (1.0 - s))


_G_C = 0.7978845608028654
_G_K = 0.044715


def _gelu_and_grad(v):
    t = jnp.tanh(_G_C * (v + _G_K * (v * v * v)))
    cdf = 0.5 * (1.0 + t)
    return v * cdf, cdf + 0.5 * v * (1.0 - t * t) * (_G_C * (1.0 + 3.0 * _G_K * v * v))


def _cast_rows(src_ref, dst_ref, rows, step=256):
    def one(i, carry):
        r = pl.ds(pl.multiple_of(i * step, step), step)
        dst_ref[r, :] = src_ref[r, :].astype(dst_ref.dtype)
        return carry
    lax.fori_loop(0, rows // step, one, 0)


def _left_lanes(rows):
    return lax.broadcasted_iota(jnp.int32, (rows, 128), 1) < HEAD


def _mesh_pos():
    return lax.axis_index("x"), lax.axis_index("y"), lax.axis_index("c")


def _peer(pos, k):
    x, y, c = pos
    return (1 - x if k & 4 else x, 1 - y if k & 2 else y, 1 - c if k & 1 else c)


def _flat(pos):
    return 4 * pos[0] + 2 * pos[1] + pos[2]


def _allgather_weights(w_in_t, w_kv, w_out):
    def body(win_ref, wkv_ref, wout_ref, wint_o, wkv_o, wout_o, send_sems, recv_sems):
        x, y, c = _mesh_pos()
        me, sib = (x, y, c), (x, y, 1 - c)
        chips = [(1 - x, y), (x, 1 - y), (1 - x, 1 - y)]

        def rows(p):
            return wint_o.at[pl.ds(pl.multiple_of(_flat(p) * W_IN_SHARD, 16), W_IN_SHARD), :]

        rows(me)[...] = win_ref[...].astype(BF16)

        def copy(k, block, to):
            return pltpu.make_async_remote_copy(
                src_ref=rows(block), dst_ref=rows(block), send_sem=send_sems.at[k], recv_sem=recv_sems.at[k],
                device_id=to, device_id_type=MESH)

        first = [copy(0, me, sib)] + [copy(1 + j, me, (*chip, c)) for j, chip in enumerate(chips)]
        for cp in first:
            cp.start()
        wkv_o[...] = wkv_ref[...].astype(BF16)
        wout_o[...] = wout_ref[...].astype(BF16)
        passed = []
        for j, chip in enumerate(chips):
            copy(1 + j, (*chip, c), me).wait_recv()
            fwd = copy(4 + j, (*chip, c), sib)
            fwd.start()
            passed.append(fwd)
        copy(0, sib, me).wait_recv()
        for j, chip in enumerate(chips):
            copy(4 + j, (*chip, 1 - c), me).wait_recv()
        for cp in first + passed:
            cp.wait_send()

    vmem = pl.BlockSpec(memory_space=pltpu.VMEM)
    return pl.pallas_call(
        body, name="allgather_weights",
        out_shape=(SDS((IN_COLS, D_MODEL), BF16), SDS(w_kv.shape, BF16), SDS(w_out.shape, BF16)),
        in_specs=[vmem, vmem, vmem], out_specs=(vmem, vmem, vmem),
        scratch_shapes=[pltpu.SemaphoreType.DMA((7,)), pltpu.SemaphoreType.DMA((7,))],
        compiler_params=_params(),
    )(w_in_t, w_kv, w_out)


def _proj_fwd(x2, g_norm, wint):
    tm = 256

    def body(x_ref, g_ref, w_ref, o_ref):
        xv = x_ref[...]
        h = xv * _rstd(xv) * g_ref[...]
        o_ref[...] = _dot_nt(h, w_ref[...])

    return pl.pallas_call(
        body, name="proj_fwd", grid=(T_LOC // tm,),
        in_specs=[pl.BlockSpec((tm, D_MODEL), lambda i: (i, 0)), pl.BlockSpec((1, D_MODEL), lambda i: (0, 0)),
                  pl.BlockSpec((IN_COLS, D_MODEL), lambda i: (0, 0))],
        out_specs=pl.BlockSpec((tm, IN_COLS), lambda i: (i, 0)),
        out_shape=SDS((T_LOC, IN_COLS), F32),
        compiler_params=_params(("arbitrary",)),
    )(x2, g_norm, wint)


def _memkv_fwd(mem2, g_mem, wkv):
    def body(m_ref, g_ref, w_ref, o_ref):
        mv = m_ref[...]
        o_ref[...] = _dot(mv * _rstd(mv) * g_ref[...], w_ref[...])

    return pl.pallas_call(
        body, name="memkv_fwd", out_shape=SDS((B_LOC * N_MEM, 2 * MEM_W), F32), compiler_params=_params(),
    )(mem2, g_mem, wkv)


N_BIAS = 7


def _fill_bias_tables(sl_ref, tab):
    for cfg, (d, length) in enumerate(DILATIONS):
        nk = min(length, 2 * CHUNK)
        r = lax.broadcasted_iota(jnp.int32, (CHUNK, nk), 0)
        c = lax.broadcasted_iota(jnp.int32, (CHUNK, nk), 1)
        for var in range(3 if length > nk else 1):
            rel = jnp.abs(r - c + var * RADIUS)
            dist = rel.astype(F32) * float(d)
            for h in range(2):
                slope = sl_ref[0, 0:1, h * HEAD:h * HEAD + 1]
                tab[3 * cfg + var, h * CHUNK:(h + 1) * CHUNK, 0:nk] = jnp.where(rel <= RADIUS, -slope * dist, NEG)


def _attn_blocks(visit, unroll):
    def step(t, carry):
        for cfg, (d, length) in enumerate(DILATIONS):
            nblk = length // CHUNK
            if nblk == 1:
                visit(cfg, 0, t, t, length, t)
                continue
            rho, i = (0, t) if d == 1 else (t // nblk, t % nblk)
            ks = jnp.clip(i * CHUNK - RADIUS, 0, length - 2 * CHUNK)
            visit(cfg, (i * CHUNK - ks) // RADIUS, rho + d * (i * CHUNK), rho + d * ks, 2 * CHUNK, t)
        return carry
    lax.fori_loop(0, 16, step, 0, unroll=unroll)


def _stack_heads(v, left):
    return jnp.concatenate([jnp.where(left, v, 0.0), jnp.where(left, 0.0, v)], axis=0)


def _unstack_heads(v, left):
    return jnp.where(left, v[0:CHUNK], v[CHUNK:2 * CHUNK])


def _rows(start, n, d):
    return pl.ds(start, n) if d == 1 else pl.ds(start, n, stride=d)


def _attn_fwd(proj, slopes):
    def body(sl_ref, q_ref, k_ref, v_ref, a_ref, lse_ref, *scr):
        o_c, m_c, l_c, tab = scr[0:3], scr[3:6], scr[6:9], scr[9]
        left = _left_lanes(CHUNK)
        _fill_bias_tables(sl_ref, tab)

        def block(cfg, var, q0, k0, nk, t):
            d = DILATIONS[cfg][0]
            rq, rk = _rows(q0, CHUNK, d), _rows(k0, nk, d)
            qs = _stack_heads(q_ref[rq, :] * SCALE, left)
            s = _dot_nt(qs, k_ref[rk, :]) + tab[3 * cfg + var, :, 0:nk]
            m = jnp.max(s, axis=-1, keepdims=True)
            p = jnp.exp(s - m)
            o_c[cfg][rq, :] = _unstack_heads(_dot(p, v_ref[rk, :]), left)
            m_c[cfg][rq, :] = _unstack_heads(m, left)
            l_c[cfg][rq, :] = _unstack_heads(jnp.sum(p, axis=-1, keepdims=True), left)
        _attn_blocks(block, 4)

        def merge(j, carry):
            rows = pl.ds(pl.multiple_of(j * 256, 256), 256)
            ms = [m_c[i][rows, :] for i in range(3)]
            top = jnp.maximum(jnp.maximum(ms[0], ms[1]), ms[2])
            ws = [jnp.exp(m - top) for m in ms]
            den = l_c[0][rows, :] * ws[0] + l_c[1][rows, :] * ws[1] + l_c[2][rows, :] * ws[2]
            num = o_c[0][rows, :] * ws[0] + o_c[1][rows, :] * ws[1] + o_c[2][rows, :] * ws[2]
            a_ref[rows, :] = num / den
            lse_ref[rows, :] = top + jnp.log(den)
            return carry
        lax.fori_loop(0, SEQ // 256, merge, 0)

    blk = lambda col0: pl.BlockSpec((SEQ, 128), lambda b, hp: (b, col0 // 128 + hp))
    out = pl.BlockSpec((SEQ, 128), lambda b, hp: (b, hp))
    return pl.pallas_call(
        body, name="attn_fwd", grid=(B_LOC, 4),
        in_specs=[pl.BlockSpec((1, 8, 128), lambda b, hp: (hp, 0, 0)), blk(C_QA), blk(C_KA), blk(C_VA)],
        out_specs=(out, out),
        out_shape=(SDS((T_LOC, ATTN_W), F32), SDS((T_LOC, ATTN_W), F32)),
        scratch_shapes=[pltpu.VMEM((SEQ, 128), F32)] * 9 + [pltpu.VMEM((N_BIAS, 2 * CHUNK, 2 * CHUNK), F32)],
        compiler_params=_params(("arbitrary", "arbitrary")),
    )(slopes, proj, proj, proj)


def _sgu_mix(vn, ws_ref, dst_ref, tm):
    left = _left_lanes(CHUNK)
    for ch in range(tm // CHUNK):
        for pr in range(2):
            vp = vn[ch * CHUNK:(ch + 1) * CHUNK, pr * 128:(pr + 1) * 128]
            dst_ref[ch * CHUNK:(ch + 1) * CHUNK, pr * 128:(pr + 1) * 128] = jnp.where(
                left, _dot(ws_ref[2 * pr], vp), _dot(ws_ref[2 * pr + 1], vp))


def _mem_attn_head(qp, kp, h, left):
    qh = jnp.where(left if h == 0 else ~left, qp, 0.0)
    s = _dot_nt(qh, kp) * SCALE
    e = jnp.exp(s - jnp.max(s, axis=-1, keepdims=True))
    return e * (1.0 / jnp.sum(e, axis=-1, keepdims=True)), qh


def _branch_blocks(tm):
    col = lambda w, c0: pl.BlockSpec((tm, w), lambda i: (i, c0 // w))
    return [col(512, C_ZA), col(256, C_UB), col(256, C_VB), col(256, C_ZB), col(256, C_QM), col(256, C_ZM)]


def _branch_fwd(proj, a, kv, w_s, b_exp, g_v):
    tm = 256
    per_ex = SEQ // tm

    def body(za_ref, ub_ref, vb_ref, zb_ref, qm_ref, zm_ref, a_ref, kv_ref, ws_ref, be_ref, gv_ref, o_ref, mix):
        left = _left_lanes(tm)
        o_ref[:, 0:ATTN_W] = (_silu_and_grad(za_ref[...])[0] * a_ref[...]).astype(BF16)
        gu = _gelu_and_grad(ub_ref[...])[0]
        gv = _gelu_and_grad(vb_ref[...])[0]
        vn = gv * _rstd(gv) * gv_ref[...]
        _sgu_mix(vn.astype(BF16), ws_ref, mix, tm)
        sg = gu * (mix[...] + be_ref[...])
        o_ref[:, ATTN_W:ATTN_W + SGU_W] = (_silu_and_grad(zb_ref[...])[0] * sg).astype(BF16)
        szm = _silu_and_grad(zm_ref[...])[0]
        for hp in range(2):
            cols = slice(hp * 128, (hp + 1) * 128)
            qp, kp, vp = qm_ref[:, cols], kv_ref[:, cols], kv_ref[:, MEM_W + hp * 128:MEM_W + (hp + 1) * 128]
            o = [_dot(_mem_attn_head(qp, kp, h, left)[0], vp) for h in range(2)]
            c0 = ATTN_W + SGU_W + hp * 128
            o_ref[:, c0:c0 + 128] = (szm[:, cols] * jnp.where(left, o[0], o[1])).astype(BF16)

    full = lambda shape: pl.BlockSpec(shape, lambda i: (0,) * len(shape))
    return pl.pallas_call(
        body, name="branch_fwd", grid=(T_LOC // tm,),
        in_specs=_branch_blocks(tm) + [
            pl.BlockSpec((tm, ATTN_W), lambda i: (i, 0)), pl.BlockSpec((N_MEM, 2 * MEM_W), lambda i: (i // per_ex, 0)),
            full((4, CHUNK, CHUNK)), full((tm, SGU_W)), full((1, SGU_W))],
        out_specs=pl.BlockSpec((tm, D_MODEL), lambda i: (i, 0)),
        out_shape=SDS((T_LOC, D_MODEL), BF16),
        scratch_shapes=[pltpu.VMEM((tm, SGU_W), F32)],
        compiler_params=_params(("arbitrary",)),
    )(proj, proj, proj, proj, proj, proj, a, kv, w_s, b_exp, g_v)


def _outproj_loss(gated, wout, x2, tgt2, g_final):
    tm = 512

    def body(g_ref, w_ref, x_ref, t_ref, gf_ref, dh2_ref, loss_ref, dgf_ref):
        @pl.when(pl.program_id(0) == 0)
        def _():
            loss_ref[...] = jnp.zeros_like(loss_ref)
            dgf_ref[...] = jnp.zeros_like(dgf_ref)
        h2 = x_ref[...] + _dot(g_ref[...], w_ref[...])
        r = _rstd(h2)
        gf = gf_ref[...]
        err = h2 * r * gf - t_ref[...]
        loss_ref[...] += 0.5 * jnp.sum(jnp.mean(err * err, axis=-1, keepdims=True))
        dy = err * (1.0 / D_MODEL)
        dh2_ref[...] = _rms_bwd(h2, r, gf, dy)
        dgf_ref[...] += jnp.sum(dy * (h2 * r), axis=0, keepdims=True)

    row = pl.BlockSpec((tm, D_MODEL), lambda i: (i, 0))
    vec = pl.BlockSpec((1, D_MODEL), lambda i: (0, 0))
    return pl.pallas_call(
        body, name="outproj_loss", grid=(T_LOC // tm,),
        in_specs=[row, pl.BlockSpec((D_MODEL, D_MODEL), lambda i: (0, 0)), row, row, vec],
        out_specs=(row, pl.BlockSpec((8, 128), lambda i: (0, 0)), vec),
        out_shape=(SDS((T_LOC, D_MODEL), F32), SDS((8, 128), F32), SDS((1, D_MODEL), F32)),
        compiler_params=_params(("arbitrary",)),
    )(gated, wout, x2, tgt2, g_final)


def _branch_bwd(dh2, wout, gated, proj, a, kv, w_s, b_exp, g_v):
    tm = 256
    per_ex = SEQ // tm

    def body(dh2_ref, w_ref, g_ref, za_ref, ub_ref, vb_ref, zb_ref, qm_ref, zm_ref, a_ref, kv_ref, ws_ref,
             be_ref, gv_ref, da_ref, dr_ref, dkv_ref, dwo_ref, dws_ref, db_ref, dgv_ref, mix, dvn, dmsum, dwo_acc):
        i = pl.program_id(0)
        left = _left_lanes(tm)
        leftc = _left_lanes(CHUNK)

        @pl.when(i == 0)
        def _():
            dwo_acc[...] = jnp.zeros_like(dwo_acc)
            dws_ref[...] = jnp.zeros_like(dws_ref)
            dgv_ref[...] = jnp.zeros_like(dgv_ref)
            dmsum[...] = jnp.zeros_like(dmsum)

        @pl.when(i % per_ex == 0)
        def _():
            dkv_ref[...] = jnp.zeros_like(dkv_ref)

        dh2 = dh2_ref[...].astype(BF16)
        dwo_acc[...] += _dot_tn(g_ref[...], dh2)
        dg = _dot_nt(dh2, w_ref[...])

        sa, dsa = _silu_and_grad(za_ref[...])
        dga = dg[:, 0:ATTN_W]
        da_ref[...] = dga * sa
        dr_ref[:, 0:512] = (dga * a_ref[...] * dsa).astype(BF16)

        ub, vb = ub_ref[...], vb_ref[...]
        gu, dgu = _gelu_and_grad(ub)
        gv, dgv = _gelu_and_grad(vb)
        rv = _rstd(gv)
        gain = gv_ref[...]
        vn = (gv * rv * gain).astype(BF16)
        _sgu_mix(vn, ws_ref, mix, tm)
        mixed = mix[...] + be_ref[...]
        sb, dsb = _silu_and_grad(zb_ref[...])
        dgb = dg[:, ATTN_W:ATTN_W + SGU_W]
        dsg = dgb * sb
        dr_ref[:, 512:768] = (dsg * mixed * dgu).astype(BF16)
        dr_ref[:, 1024:1280] = (dgb * (gu * mixed) * dsb).astype(BF16)
        dmix = dsg * gu
        for ch in range(tm // CHUNK):
            rows = slice(ch * CHUNK, (ch + 1) * CHUNK)
            dmsum[...] += dmix[rows, :]
            for pr in range(2):
                cols = slice(pr * 128, (pr + 1) * 128)
                dmp, vp = dmix[rows, cols], vn[rows, cols]
                for h in range(2):
                    g = 2 * pr + h
                    dws_ref[g] += _dot_nt(jnp.where(leftc if h == 0 else ~leftc, dmp, 0.0), vp)
                dvn[rows, cols] = jnp.where(leftc, _dot_tn(ws_ref[2 * pr], dmp), _dot_tn(ws_ref[2 * pr + 1], dmp))
        dvn_v = dvn[...]
        dgv_ref[...] += jnp.sum(dvn_v * (gv * rv), axis=0, keepdims=True)
        dr_ref[:, 768:1024] = (_rms_bwd(gv, rv, gain, dvn_v) * dgv).astype(BF16)

        szm, dszm = _silu_and_grad(zm_ref[...])
        dgm = dg[:, ATTN_W + SGU_W:D_MODEL]
        dmo = dgm * szm
        for hp in range(2):
            cols = slice(hp * 128, (hp + 1) * 128)
            vcols = slice(MEM_W + hp * 128, MEM_W + (hp + 1) * 128)
            qp, kp, vp, dmop = qm_ref[:, cols], kv_ref[:, cols], kv_ref[:, vcols], dmo[:, cols]
            o, dq = [], []
            dk = jnp.zeros((N_MEM, 128), F32)
            dv = jnp.zeros((N_MEM, 128), F32)
            for h in range(2):
                p, qh = _mem_attn_head(qp, kp, h, left)
                dmoh = jnp.where(left if h == 0 else ~left, dmop, 0.0)
                o.append(_dot(p, vp))
                dp = _dot_nt(dmoh, vp)
                ds = p * (dp - jnp.sum(p * dp, axis=-1, keepdims=True)) * SCALE
                dq.append(_dot(ds, kp))
                dk += _dot_tn(ds, qh)
                dv += _dot_tn(p, dmoh)
            dr_ref[:, 1280 + hp * 128:1280 + (hp + 1) * 128] = jnp.where(left, dq[0], dq[1]).astype(BF16)
            dr_ref[:, 1536 + hp * 128:1536 + (hp + 1) * 128] = (
                dgm[:, cols] * jnp.where(left, o[0], o[1]) * dszm[:, cols]).astype(BF16)
            dkv_ref[:, cols] += dk
            dkv_ref[:, vcols] += dv

        @pl.when(i == pl.num_programs(0) - 1)
        def _():
            tot = dmsum[...]
            hi = tot.astype(BF16)
            lo = (tot - hi.astype(F32)).astype(BF16)
            grp = (lax.broadcasted_iota(jnp.int32, (SGU_W, 128), 0) // HEAD
                   == lax.broadcasted_iota(jnp.int32, (SGU_W, 128), 1)).astype(BF16)
            db_ref[...] = (_dot(hi, grp) + _dot(lo, grp)).T[0:4, :]
            _cast_rows(dwo_acc, dwo_ref, D_MODEL)

    full = lambda shape: pl.BlockSpec(shape, lambda i: (0,) * len(shape))
    row = lambda w: pl.BlockSpec((tm, w), lambda i: (i, 0))
    return pl.pallas_call(
        body, name="branch_bwd", grid=(T_LOC // tm,),
        in_specs=[row(D_MODEL), full((D_MODEL, D_MODEL)), row(D_MODEL)] + _branch_blocks(tm) + [
            row(ATTN_W), pl.BlockSpec((N_MEM, 2 * MEM_W), lambda i: (i // per_ex, 0)),
            full((4, CHUNK, CHUNK)), full((tm, SGU_W)), full((1, SGU_W))],
        out_specs=(row(ATTN_W), row(REST_W), pl.BlockSpec((N_MEM, 2 * MEM_W), lambda i: (i // per_ex, 0)),
                   full((D_MODEL, D_MODEL)), full((4, CHUNK, CHUNK)), full((4, CHUNK)), full((1, SGU_W))),
        out_shape=(SDS((T_LOC, ATTN_W), F32), SDS((T_LOC, REST_W), BF16), SDS((B_LOC * N_MEM, 2 * MEM_W), F32),
                   SDS((D_MODEL, D_MODEL), BF16), SDS((4, CHUNK, CHUNK), F32), SDS((4, CHUNK), F32), SDS((1, SGU_W), F32)),
        scratch_shapes=[pltpu.VMEM((tm, SGU_W), F32), pltpu.VMEM((tm, SGU_W), F32), pltpu.VMEM((CHUNK, SGU_W), F32),
                        pltpu.VMEM((D_MODEL, D_MODEL), F32)],
        compiler_params=_params(("arbitrary",)),
    )(dh2, wout, gated, proj, proj, proj, proj, proj, proj, a, kv, w_s, b_exp, g_v)


def _attn_bwd(proj, slopes, da, a, lse):
    def body(sl_ref, q_ref, k_ref, v_ref, da_ref, a_ref, lse_ref, dq_ref, dk_ref, dv_ref, *scr):
        dq_s, dk_s, dv_s, tab = scr[0:3], scr[3:6], scr[6:9], scr[9]
        lse_h, delta_h = scr[10:12], scr[12:14]
        p_all, ds_all = scr[14], scr[15]
        left = _left_lanes(CHUNK)
        _fill_bias_tables(sl_ref, tab)

        def prep(j, carry):
            rows = pl.ds(pl.multiple_of(j * 256, 256), 256)
            l256 = _left_lanes(256)
            prod = da_ref[rows, :] * a_ref[rows, :]
            delta_h[0][rows, :] = jnp.broadcast_to(jnp.sum(jnp.where(l256, prod, 0.0), axis=-1, keepdims=True), (256, 128))
            delta_h[1][rows, :] = jnp.broadcast_to(jnp.sum(jnp.where(l256, 0.0, prod), axis=-1, keepdims=True), (256, 128))
            pair = lse_ref[rows, :]
            other = pltpu.roll(pair, HEAD, axis=1)
            lse_h[0][rows, :] = jnp.where(l256, pair, other)
            lse_h[1][rows, :] = jnp.where(l256, other, pair)
            zero = jnp.zeros((256, 128), F32)
            for cfg in range(3):
                dk_s[cfg][rows, :] = zero
                dv_s[cfg][rows, :] = zero
            return carry
        lax.fori_loop(0, SEQ // 256, prep, 0)

        def per_row(halves, rq, nk):
            v = jnp.concatenate([halves[0][rq, :], halves[1][rq, :]], axis=0)
            return v if nk == 128 else jnp.concatenate([v, v], axis=1)

        def probs(cfg, var, q0, k0, nk, t):
            d = DILATIONS[cfg][0]
            rq, rk = _rows(q0, CHUNK, d), _rows(k0, nk, d)
            qs = _stack_heads(q_ref[rq, :] * SCALE, left)
            das = _stack_heads(da_ref[rq, :], left)
            s = _dot_nt(qs, k_ref[rk, :]) + tab[3 * cfg + var, :, 0:nk]
            p = jnp.exp(s - per_row(lse_h, rq, nk))
            p_all[16 * cfg + t, :, 0:nk] = p.astype(BF16)
            ds_all[16 * cfg + t, :, 0:nk] = (p * (_dot_nt(das, v_ref[rk, :]) - per_row(delta_h, rq, nk))).astype(BF16)
        _attn_blocks(probs, 4)

        def grads(cfg, var, q0, k0, nk, t):
            d = DILATIONS[cfg][0]
            rq, rk = _rows(q0, CHUNK, d), _rows(k0, nk, d)
            qs = _stack_heads(q_ref[rq, :] * SCALE, left).astype(BF16)
            das = _stack_heads(da_ref[rq, :], left).astype(BF16)
            p, ds = p_all[16 * cfg + t, :, 0:nk], ds_all[16 * cfg + t, :, 0:nk]
            dq_s[cfg][rq, :] = _unstack_heads(_dot(ds, k_ref[rk, :]), left) * SCALE
            dk_s[cfg][rk, :] += _dot_tn(ds, qs)
            dv_s[cfg][rk, :] += _dot_tn(p, das)
        _attn_blocks(grads, 4)

        def flush(j, carry):
            rows = pl.ds(pl.multiple_of(j * 256, 256), 256)
            for acc, dst in ((dq_s, dq_ref), (dk_s, dk_ref), (dv_s, dv_ref)):
                dst[rows, :] = (acc[0][rows, :] + acc[1][rows, :] + acc[2][rows, :]).astype(BF16)
            return carry
        lax.fori_loop(0, SEQ // 256, flush, 0)

    blk = lambda col0: pl.BlockSpec((SEQ, 128), lambda b, hp: (b, col0 // 128 + hp))
    own = pl.BlockSpec((SEQ, 128), lambda b, hp: (b, hp))
    return pl.pallas_call(
        body, name="attn_bwd", grid=(B_LOC, 4),
        in_specs=[pl.BlockSpec((1, 8, 128), lambda b, hp: (hp, 0, 0)), blk(C_QA), blk(C_KA), blk(C_VA), own, own, own],
        out_specs=(own, own, own),
        out_shape=(SDS((T_LOC, ATTN_W), BF16),) * 3,
        scratch_shapes=[pltpu.VMEM((SEQ, 128), F32)] * 9 + [pltpu.VMEM((N_BIAS, 2 * CHUNK, 2 * CHUNK), F32)]
        + [pltpu.VMEM((SEQ, 128), F32)] * 4 + [pltpu.VMEM((48, 2 * CHUNK, 2 * CHUNK), BF16)] * 2,
        compiler_params=_params(("arbitrary", "arbitrary")),
    )(slopes, proj, proj, proj, da, a, lse)


def _dproj_specs(tm):
    third = pl.BlockSpec((tm, ATTN_W), lambda i: (i, 0))
    return [third, third, third, pl.BlockSpec((tm, REST_W), lambda i: (i, 0))]


def _dx(dq, dk, dv, dr, wint, x2, dh2, g_norm):
    tm = 256

    def body(dq_ref, dk_ref, dv_ref, dr_ref, w_ref, x_ref, dh2_ref, g_ref, gx_ref, dgn_ref):
        @pl.when(pl.program_id(0) == 0)
        def _():
            dgn_ref[...] = jnp.zeros_like(dgn_ref)
        dh = (_dot(dq_ref[...], w_ref[C_QA:C_KA, :]) + _dot(dk_ref[...], w_ref[C_KA:C_VA, :])
              + _dot(dv_ref[...], w_ref[C_VA:C_ZA, :]) + _dot(dr_ref[...], w_ref[C_ZA:IN_COLS, :]))
        xv = x_ref[...]
        r = _rstd(xv)
        gx_ref[...] = dh2_ref[...] + _rms_bwd(xv, r, g_ref[...], dh)
        dgn_ref[...] += jnp.sum(dh * (xv * r), axis=0, keepdims=True)

    row = pl.BlockSpec((tm, D_MODEL), lambda i: (i, 0))
    vec = pl.BlockSpec((1, D_MODEL), lambda i: (0, 0))
    return pl.pallas_call(
        body, name="dx", grid=(T_LOC // tm,),
        in_specs=_dproj_specs(tm) + [pl.BlockSpec((IN_COLS, D_MODEL), lambda i: (0, 0)), row, row, vec],
        out_specs=(row, vec),
        out_shape=(SDS((T_LOC, D_MODEL), F32), SDS((1, D_MODEL), F32)),
        compiler_params=_params(("arbitrary",)),
    )(dq, dk, dv, dr, wint, x2, dh2, g_norm)


def _dwin(dq, dk, dv, dr, x2, g_norm):
    tm = 512

    def body(dq_ref, dk_ref, dv_ref, dr_ref, x_ref, g_ref, o_ref, acc):
        @pl.when(pl.program_id(0) == 0)
        def _():
            acc[...] = jnp.zeros_like(acc)
        xv = x_ref[...]
        h = (xv * _rstd(xv) * g_ref[...]).astype(BF16)
        acc[C_QA:C_KA, :] += _dot_tn(dq_ref[...], h)
        acc[C_KA:C_VA, :] += _dot_tn(dk_ref[...], h)
        acc[C_VA:C_ZA, :] += _dot_tn(dv_ref[...], h)
        acc[C_ZA:IN_COLS, :] += _dot_tn(dr_ref[...], h)

        @pl.when(pl.program_id(0) == pl.num_programs(0) - 1)
        def _():
            _cast_rows(acc, o_ref, IN_COLS)

    return pl.pallas_call(
        body, name="dwin", grid=(T_LOC // tm,),
        in_specs=_dproj_specs(tm) + [pl.BlockSpec((tm, D_MODEL), lambda i: (i, 0)), pl.BlockSpec((1, D_MODEL), lambda i: (0, 0))],
        out_specs=pl.BlockSpec((IN_COLS, D_MODEL), lambda i: (0, 0)),
        out_shape=SDS((IN_COLS, D_MODEL), BF16),
        scratch_shapes=[pltpu.VMEM((IN_COLS, D_MODEL), F32)],
        compiler_params=_params(("arbitrary",)),
    )(dq, dk, dv, dr, x2, g_norm)


def _memkv_bwd(dkv, mem2, g_mem, wkv):
    def body(dkv_ref, m_ref, g_ref, w_ref, dw_ref, dg_ref):
        mv = m_ref[...]
        r = _rstd(mv)
        dkv_v = dkv_ref[...].astype(BF16)
        dw_ref[...] = _dot_tn(mv * r * g_ref[...], dkv_v).astype(BF16)
        dg_ref[...] = jnp.sum(_dot_nt(dkv_v, w_ref[...]) * (mv * r), axis=0, keepdims=True)

    return pl.pallas_call(
        body, name="memkv_bwd", out_shape=(SDS((D_MODEL, 2 * MEM_W), BF16), SDS((1, D_MODEL), F32)),
        compiler_params=_params(),
    )(dkv, mem2, g_mem, wkv)


def _allreduce_small(parts):
    n = len(parts)

    def body(*refs):
        ins, outs, bufs = refs[0:n], refs[n:2 * n], refs[2 * n:3 * n]
        send_sems, recv_sems = refs[3 * n], refs[3 * n + 1]
        pos = _mesh_pos()
        me = _flat(pos)
        for a in range(n):
            bufs[a][me] = ins[a][...]

        def copy(a, k, slot):
            return pltpu.make_async_remote_copy(
                src_ref=ins[a], dst_ref=bufs[a].at[slot],
                send_sem=send_sems.at[7 * a + k - 1], recv_sem=recv_sems.at[7 * a + k - 1],
                device_id=_peer(pos, k), device_id_type=MESH)

        sent = [copy(a, k, me) for a in range(n) for k in range(1, N_DEV)]
        for cp in sent:
            cp.start()
        for a in range(n):
            for k in range(1, N_DEV):
                copy(a, k, _flat(_peer(pos, k))).wait_recv()
        for cp in sent:
            cp.wait_send()
        for a in range(n):
            acc = bufs[a][0]
            for s in range(1, N_DEV):
                acc = acc + bufs[a][s]
            outs[a][...] = acc

    vmem = pl.BlockSpec(memory_space=pltpu.VMEM)
    return pl.pallas_call(
        body, name="allreduce_small",
        out_shape=tuple(SDS(p.shape, F32) for p in parts),
        in_specs=[vmem] * n, out_specs=(vmem,) * n,
        scratch_shapes=[pltpu.VMEM((N_DEV,) + p.shape, F32) for p in parts]
        + [pltpu.SemaphoreType.DMA((7 * n,)), pltpu.SemaphoreType.DMA((7 * n,))],
        compiler_params=_params(),
    )(*parts)


_HBM = pl.BlockSpec(memory_space=pltpu.HBM)
_SEM = pl.BlockSpec(memory_space=pltpu.SEMAPHORE)
_SIDE_EFFECT = pltpu.SideEffectType.DATAFLOW_SIDE_EFFECTING


def _exchange_copies(src_refs, land_refs, scatter, send_sems, recv_sems):
    pos = _mesh_pos()
    copies = []
    for a, (src, land) in enumerate(zip(src_refs, land_refs)):
        n = land.shape[1]
        for k in range(1, N_DEV):
            peer = _peer(pos, k)
            piece = src.at[pl.ds(pl.multiple_of(_flat(peer) * n, 16), n), :] if scatter[a] else src
            copies.append(pltpu.make_async_remote_copy(
                src_ref=piece, dst_ref=land.at[_flat(pos)],
                send_sem=send_sems.at[7 * a + k - 1], recv_sem=recv_sems.at[7 * a + k - 1],
                device_id=peer, device_id_type=MESH))
    return copies


def _exchange_start(name, srcs, scatter, lands):
    n = len(srcs)

    def body(*refs):
        for cp in _exchange_copies(refs[0:n], refs[n:2 * n], scatter, refs[2 * n], refs[2 * n + 1]):
            cp.start()
        refs[-1][...] = jnp.zeros_like(refs[-1])

    ops = [pltpu.with_memory_space_constraint(t, pltpu.HBM) for t in (*srcs, *lands)]
    out = pl.pallas_call(
        body, name=name,
        out_shape=(pltpu.SemaphoreType.DMA((7 * n,)), pltpu.SemaphoreType.DMA((7 * n,)),
                   *[pltpu.HBM(t.shape, t.dtype) for t in ops], SDS((8, 128), F32)),
        in_specs=[_HBM] * (2 * n),
        out_specs=(_SEM, _SEM, *[_HBM] * (2 * n), pl.BlockSpec(memory_space=pltpu.VMEM)),
        input_output_aliases={i: 2 + i for i in range(2 * n)},
        compiler_params=pltpu.CompilerParams(has_side_effects=_SIDE_EFFECT),
    )(*ops)
    return out[0], out[1], out[2:2 + n], out[2 + n:2 + 2 * n], out[-1]


def _exchange_wait(name, started, scatter, after):
    send_sems, recv_sems, srcs, lands, _ = started
    n = len(srcs)

    def body(*refs):
        for cp in _exchange_copies(refs[0:n], refs[n:2 * n], scatter, refs[2 * n], refs[2 * n + 1]):
            cp.wait_send()
            cp.wait_recv()

    out = pl.pallas_call(
        body, name=name,
        out_shape=tuple(pltpu.HBM(t.shape, t.dtype) for t in (*srcs, *lands)),
        in_specs=[_HBM] * (2 * n) + [_SEM, _SEM, pl.BlockSpec(memory_space=pl.ANY)],
        out_specs=(_HBM,) * (2 * n),
        input_output_aliases={i: i for i in range(2 * n)},
        compiler_params=pltpu.CompilerParams(has_side_effects=_SIDE_EFFECT),
    )(*srcs, *lands, send_sems, recv_sems, after)
    return out[n:]


def _landing(own, me):
    return lax.dynamic_update_slice(lax.empty((N_DEV,) + own.shape, own.dtype), own[None], (me,) + (0,) * own.ndim)


def _adamw(w, g, m, v):
    m = ADAM_B1 * m + (1.0 - ADAM_B1) * g
    v = ADAM_B2 * v + (1.0 - ADAM_B2) * (g * g)
    m_hat = m / (1.0 - ADAM_B1 ** ADAM_STEP)
    v_hat = v / (1.0 - ADAM_B2 ** ADAM_STEP)
    return -ADAM_LR * (m_hat / (jnp.sqrt(v_hat) + ADAM_EPS) + ADAM_WD * w), m, v


def _adam_slots(name, slots, w, m, v):
    _, rows, cols = slots.shape
    rc = rows // 2

    def body(s_ref, w_ref, m_ref, v_ref, g_o, d_o, m_o, v_o, acc):
        s = pl.program_id(1)

        @pl.when(s == 0)
        def _():
            acc[...] = s_ref[0].astype(F32)

        @pl.when(s > 0)
        def _():
            acc[...] += s_ref[0].astype(F32)

        @pl.when(s == N_DEV - 1)
        def _():
            g = acc[...]
            g_o[...] = g
            d_o[...], m_o[...], v_o[...] = _adamw(w_ref[...], g, m_ref[...], v_ref[...])

    chunk = pl.BlockSpec((rc, cols), lambda i, s: (i, 0))
    return pl.pallas_call(
        body, name="adam_" + name, grid=(rows // rc, N_DEV),
        in_specs=[pl.BlockSpec((1, rc, cols), lambda i, s: (s, i, 0)), chunk, chunk, chunk],
        out_specs=(chunk,) * 4, out_shape=(SDS((rows, cols), F32),) * 4,
        scratch_shapes=[pltpu.VMEM((rc, cols), F32)],
        compiler_params=_params(("arbitrary", "arbitrary")),
    )(slots, w, m, v)


def _adam_small(ws, gs, ms, vs, loss_slots):
    n = len(ws)

    def total(ref, like):
        if len(ref.shape) == len(like.shape):
            return ref[...]
        acc = ref[0]
        for s in range(1, N_DEV):
            acc = acc + ref[s]
        return acc

    def body(*refs):
        w_r, g_r, m_r, v_r = refs[0:n], refs[n:2 * n], refs[2 * n:3 * n], refs[3 * n:4 * n]
        loss_r, outs = refs[4 * n], refs[4 * n + 1:]
        for a in range(n):
            g = total(g_r[a], w_r[a])
            outs[a][...] = g
            outs[n + 1 + 3 * a][...], outs[n + 2 + 3 * a][...], outs[n + 3 + 3 * a][...] = _adamw(
                w_r[a][...], g, m_r[a][...], v_r[a][...])
        outs[n][...] = total(loss_r, outs[n])

    out = pl.pallas_call(
        body, name="adam_small",
        out_shape=tuple(SDS(w.shape, F32) for w in ws) + (SDS(loss_slots.shape[1:], F32),)
        + tuple(SDS(w.shape, F32) for w in ws for _ in range(3)),
        compiler_params=_params(),
    )(*ws, *gs, *ms, *vs, loss_slots)
    return out[0:n], out[n], out[n + 1:]


def kernel(x, mem, g_norm, w_in, w_sgu_spatial, b_sgu_spatial, g_sgu_v, g_mem, w_mem_kv, w_out, g_final, loss_target, m_g_norm, m_w_in, m_w_sgu_spatial, m_b_sgu_spatial, m_g_sgu_v, m_g_mem, m_w_mem_kv, m_w_out, m_g_final, v_g_norm, v_w_in, v_w_sgu_spatial, v_b_sgu_spatial, v_g_sgu_v, v_g_mem, v_w_mem_kv, v_w_out, v_g_final):
    x2 = x.reshape(T_LOC, D_MODEL)
    tgt2 = loss_target.reshape(T_LOC, D_MODEL)
    mem2 = mem.reshape(B_LOC * N_MEM, D_MODEL)
    w_s = w_sgu_spatial[0]
    b_exp = jnp.tile(jnp.repeat(b_sgu_spatial[0].T, HEAD, axis=1), (2, 1))
    slope = jnp.power(2.0, -8.0 * (jnp.arange(8, dtype=F32) + 1.0) / 8)
    slopes = jnp.broadcast_to(jnp.repeat(slope.reshape(4, 2), HEAD, axis=1)[:, None, :], (4, 8, 128))

    tr = lambda t: jnp.transpose(t[0])

    me = _flat(_mesh_pos())
    own_rows = lambda t: lax.dynamic_slice_in_dim(t, me * (t.shape[0] // N_DEV), t.shape[0] // N_DEV)

    wint, wkv_own, wout_own = _allgather_weights(tr(w_in), w_mem_kv[0], w_out[0])
    started0 = _exchange_start("exchange0_start", [wkv_own, wout_own], [False, False],
                               [_landing(wkv_own, me), _landing(wout_own, me)])
    proj = _proj_fwd(x2, g_norm + started0[4][0:1, 0:1], wint)
    wkv, wout = _exchange_wait("exchange0_wait", started0, [False, False], proj)
    wkv, wout = wkv.reshape(D_MODEL, 2 * MEM_W), wout.reshape(D_MODEL, D_MODEL)
    kv = _memkv_fwd(mem2, g_mem, wkv)
    a, lse = _attn_fwd(proj, slopes)
    gated = _branch_fwd(proj, a, kv, w_s, b_exp, g_sgu_v)
    dh2, loss8, dgf = _outproj_loss(gated, wout, x2, tgt2, g_final.reshape(1, D_MODEL))

    da, dr, dkv, dwout, dws, dbs, dgv = _branch_bwd(dh2, wout, gated, proj, a, kv, w_s, b_exp, g_sgu_v)
    dwkv, dgm = _memkv_bwd(dkv, mem2, g_mem, wkv)

    early = [dws.reshape(4 * CHUNK, CHUNK), dbs, dgv, dgm, dgf, loss8]
    scatter1 = [True, True] + [False] * len(early)
    started1 = _exchange_start(
        "exchange1_start", [dwkv, dwout] + early, scatter1,
        [_landing(own_rows(dwkv), me), _landing(own_rows(dwout), me)] + [_landing(t, me) for t in early])
    dq, dk, dv = _attn_bwd(proj, slopes + started1[4][0:1, 0:1], da, a, lse)
    s_wkv, s_wout, s_ws, s_bs, s_gv, s_gm, s_gf, s_loss = _exchange_wait("exchange1_wait", started1, scatter1, dq)

    dwint = _dwin(dq, dk, dv, dr, x2, g_norm)
    started2 = _exchange_start("exchange2_start", [dwint], [True], [_landing(own_rows(dwint), me)])
    grad_x, dgn = _dx(dq, dk, dv, dr, wint, x2, dh2, g_norm + started2[4][0:1, 0:1])
    dgn_sum, = _allreduce_small([dgn])
    s_win, = _exchange_wait("exchange2_wait", started2, [True], dgn_sum)

    g_win, d_win, m_win, v_win = map(
        jnp.transpose, _adam_slots("w_in", s_win, tr(w_in), tr(m_w_in), tr(v_w_in)))
    g_wkv, d_wkv, m_wkv, v_wkv = _adam_slots("w_mem_kv", s_wkv, w_mem_kv[0], m_w_mem_kv[0], v_w_mem_kv[0])
    g_wout, d_wout, m_wout, v_wout = _adam_slots("w_out", s_wout, w_out[0], m_w_out[0], v_w_out[0])

    small_shapes = [(1, D_MODEL), (4 * CHUNK, CHUNK), (4, CHUNK), (1, SGU_W), (1, D_MODEL), (1, D_MODEL)]
    pack = lambda arrs: [t.reshape(s) for t, s in zip(arrs, small_shapes)]
    g_small, loss_sum, upd = _adam_small(
        pack([g_norm, w_sgu_spatial, b_sgu_spatial, g_sgu_v, g_mem, g_final]),
        [dgn_sum, s_ws, s_bs, s_gv, s_gm, s_gf],
        pack([m_g_norm, m_w_sgu_spatial, m_b_sgu_spatial, m_g_sgu_v, m_g_mem, m_g_final]),
        pack([v_g_norm, v_w_sgu_spatial, v_b_sgu_spatial, v_g_sgu_v, v_g_mem, v_g_final]), s_loss)
    out_shapes = [g_norm.shape, w_sgu_spatial.shape, b_sgu_spatial.shape, g_sgu_v.shape, g_mem.shape, g_final.shape]
    unpack = lambda arrs: [t.reshape(s) for t, s in zip(arrs, out_shapes)]
    gs = unpack(g_small)
    ds, nms, nvs = unpack(upd[0::3]), unpack(upd[1::3]), unpack(upd[2::3])

    loss = loss_sum[0, 0]

    def assemble(small, win, wkv_, wout_):
        return [small[0], win[None], small[1], small[2], small[3], small[4], wkv_[None], wout_[None], small[5]]

    return (loss, grad_x.reshape(x.shape),
            *assemble(gs, g_win, g_wkv, g_wout), *assemble(ds, d_win, d_wkv, d_wout),
            *assemble(nms, m_win, m_wkv, m_wout), *assemble(nvs, v_win, v_wkv, v_wout))
```

```python
import functools

import jax
import jax.numpy as jnp
from jax import lax
from jax.experimental import pallas as pl
from jax.experimental.pallas import tpu as pltpu

F32 = jnp.float32
BF16 = jnp.bfloat16
SDS = jax.ShapeDtypeStruct
MESH = pl.DeviceIdType.MESH

N_DEV = 8
D_MODEL = 1024
SEQ = 2048
B_LOC = 2
T_LOC = B_LOC * SEQ
N_MEM = 256
HEAD = 64
ATTN_W = 512
SGU_W = 256
MEM_W = 256
IN_COLS = 3328
W_IN_SHARD = IN_COLS // N_DEV
ROW_SHARD = D_MODEL // N_DEV
CHUNK = 128
DILATIONS = ((1, 2048), (4, 512), (16, 128))
RADIUS = 64
EPS = 1e-6
NEG = -1e30
SCALE = HEAD ** -0.5
C_QA, C_KA, C_VA, C_ZA, C_UB, C_VB, C_ZB, C_QM, C_ZM = 0, 512, 1024, 1536, 2048, 2304, 2560, 2816, 3072
QKV_W = 1536
REST_W = IN_COLS - QKV_W

ADAM_LR, ADAM_B1, ADAM_B2, ADAM_EPS, ADAM_WD, ADAM_STEP = 0.001, 0.9, 0.999, 1e-08, 0.01, 10

V7X_VMEM_MIB = 64
VMEM_NO_STAGING_MIB = V7X_VMEM_MIB - 6


def _params(vmem_mib, sem=None):
    assert vmem_mib < V7X_VMEM_MIB
    return pltpu.CompilerParams(vmem_limit_bytes=vmem_mib << 20, dimension_semantics=sem)


def _dot(a, b):
    return jnp.dot(a.astype(BF16), b.astype(BF16), preferred_element_type=F32)


def _dot_nt(a, b):
    return lax.dot_general(a.astype(BF16), b.astype(BF16), (((1,), (1,)), ((), ())), preferred_element_type=F32)


def _dot_tn(a, b):
    return lax.dot_general(a.astype(BF16), b.astype(BF16), (((0,), (0,)), ((), ())), preferred_element_type=F32)


def _rstd(v):
    return lax.rsqrt(jnp.mean(v * v, axis=-1, keepdims=True) + EPS)


def _rms_bwd(v, r, g, dy):
    gdy = g * dy
    return r * gdy - v * (r * r * r * jnp.mean(gdy * v, axis=-1, keepdims=True))


def _sigmoid(z):
    return 1.0 / (1.0 + jnp.exp(-z))


def _silu_and_grad(z):
    s = _sigmoid(z)
    return z * s, s * (1.0 + z * (1.0 - s))


_G_C = 0.7978845608028654
_G_K = 0.044715


def _gelu_and_grad(v):
    t = jnp.tanh(_G_C * (v + _G_K * (v * v * v)))
    cdf = 0.5 * (1.0 + t)
    return v * cdf, cdf + 0.5 * v * (1.0 - t * t) * (_G_C * (1.0 + 3.0 * _G_K * v * v))


def _cast_rows(src_ref, dst_ref, rows, step=256):
    def one(i, carry):
        r = pl.ds(pl.multiple_of(i * step, step), step)
        dst_ref[r, :] = src_ref[r, :].astype(dst_ref.dtype)
        return carry
    lax.fori_loop(0, rows // step, one, 0)


def _left_lanes(rows):
    return lax.broadcasted_iota(jnp.int32, (rows, 128), 1) < HEAD


def _mesh_pos():
    return lax.axis_index("x"), lax.axis_index("y"), lax.axis_index("c")


def _peer(pos, k):
    x, y, c = pos
    return (1 - x if k & 4 else x, 1 - y if k & 2 else y, 1 - c if k & 1 else c)


def _flat(pos):
    return 4 * pos[0] + 2 * pos[1] + pos[2]


def _allgather_weights(w_in_t, w_kv, w_out):
    def body(win_ref, wkv_ref, wout_ref, wint_o, wkv_o, wout_o, send_sems, recv_sems):
        x, y, c = _mesh_pos()
        me, sib = (x, y, c), (x, y, 1 - c)
        chips = [(1 - x, y), (x, 1 - y), (1 - x, 1 - y)]

        def rows(p):
            return wint_o.at[pl.ds(pl.multiple_of(_flat(p) * W_IN_SHARD, 16), W_IN_SHARD), :]

        rows(me)[...] = win_ref[...].astype(BF16)

        def copy(k, block, to):
            return pltpu.make_async_remote_copy(
                src_ref=rows(block), dst_ref=rows(block), send_sem=send_sems.at[k], recv_sem=recv_sems.at[k],
                device_id=to, device_id_type=MESH)

        first = [copy(0, me, sib)] + [copy(1 + j, me, (*chip, c)) for j, chip in enumerate(chips)]
        for cp in first:
            cp.start()
        wkv_o[...] = wkv_ref[...].astype(BF16)
        wout_o[...] = wout_ref[...].astype(BF16)
        passed = []
        for j, chip in enumerate(chips):
            copy(1 + j, (*chip, c), me).wait_recv()
            fwd = copy(4 + j, (*chip, c), sib)
            fwd.start()
            passed.append(fwd)
        copy(0, sib, me).wait_recv()
        for j, chip in enumerate(chips):
            copy(4 + j, (*chip, 1 - c), me).wait_recv()
        for cp in first + passed:
            cp.wait_send()

    vmem = pl.BlockSpec(memory_space=pltpu.VMEM)
    return pl.pallas_call(
        body, name="allgather_weights",
        out_shape=(SDS((IN_COLS, D_MODEL), BF16), SDS(w_kv.shape, BF16), SDS(w_out.shape, BF16)),
        in_specs=[vmem, vmem, vmem], out_specs=(vmem, vmem, vmem),
        scratch_shapes=[pltpu.SemaphoreType.DMA((7,)), pltpu.SemaphoreType.DMA((7,))],
        compiler_params=_params(40),
    )(w_in_t, w_kv, w_out)


def _proj_fwd(x2, g_norm, wint):
    tm = 256

    def body(x_ref, g_ref, w_ref, o_ref):
        xv = x_ref[...]
        h = xv * _rstd(xv) * g_ref[...]
        o_ref[...] = _dot_nt(h, w_ref[...])

    return pl.pallas_call(
        body, name="proj_fwd", grid=(T_LOC // tm,),
        in_specs=[pl.BlockSpec((tm, D_MODEL), lambda i: (i, 0)), pl.BlockSpec((1, D_MODEL), lambda i: (0, 0)),
                  pl.BlockSpec((IN_COLS, D_MODEL), lambda i: (0, 0))],
        out_specs=pl.BlockSpec((tm, IN_COLS), lambda i: (i, 0)),
        out_shape=SDS((T_LOC, IN_COLS), F32),
        compiler_params=_params(48, ("arbitrary",)),
    )(x2, g_norm, wint)


def _memkv_fwd(mem2, g_mem, wkv):
    def body(m_ref, g_ref, w_ref, o_ref):
        mv = m_ref[...]
        o_ref[...] = _dot(mv * _rstd(mv) * g_ref[...], w_ref[...])

    return pl.pallas_call(
        body, name="memkv_fwd", out_shape=SDS((B_LOC * N_MEM, 2 * MEM_W), F32), compiler_params=_params(32),
    )(mem2, g_mem, wkv)


N_BIAS = 7


def _fill_bias_tables(sl_ref, tab):
    for cfg, (d, length) in enumerate(DILATIONS):
        nk = min(length, 2 * CHUNK)
        r = lax.broadcasted_iota(jnp.int32, (CHUNK, nk), 0)
        c = lax.broadcasted_iota(jnp.int32, (CHUNK, nk), 1)
        for var in range(3 if length > nk else 1):
            rel = jnp.abs(r - c + var * RADIUS)
            dist = rel.astype(F32) * float(d)
            for h in range(2):
                slope = sl_ref[0, 0:1, h * HEAD:h * HEAD + 1]
                tab[3 * cfg + var, h * CHUNK:(h + 1) * CHUNK, 0:nk] = jnp.where(rel <= RADIUS, -slope * dist, NEG)


def _attn_blocks(visit, unroll):
    def step(t, carry):
        for cfg, (d, length) in enumerate(DILATIONS):
            nblk = length // CHUNK
            if nblk == 1:
                visit(cfg, 0, t, t, length, t)
                continue
            rho, i = (0, t) if d == 1 else (t // nblk, t % nblk)
            ks = jnp.clip(i * CHUNK - RADIUS, 0, length - 2 * CHUNK)
            visit(cfg, (i * CHUNK - ks) // RADIUS, rho + d * (i * CHUNK), rho + d * ks, 2 * CHUNK, t)
        return carry
    lax.fori_loop(0, 16, step, 0, unroll=unroll)


def _stack_heads(v, left):
    return jnp.concatenate([jnp.where(left, v, 0.0), jnp.where(left, 0.0, v)], axis=0)


def _unstack_heads(v, left):
    return jnp.where(left, v[0:CHUNK], v[CHUNK:2 * CHUNK])


def _rows(start, n, d):
    return pl.ds(start, n) if d == 1 else pl.ds(start, n, stride=d)


def _attn_fwd(proj, slopes):
    def body(sl_ref, q_ref, k_ref, v_ref, a_ref, lse_ref, *scr):
        o_c, m_c, l_c, tab = scr[0:3], scr[3:6], scr[6:9], scr[9]
        left = _left_lanes(CHUNK)
        _fill_bias_tables(sl_ref, tab)

        def block(cfg, var, q0, k0, nk, t):
            d = DILATIONS[cfg][0]
            rq, rk = _rows(q0, CHUNK, d), _rows(k0, nk, d)
            qs = _stack_heads(q_ref[rq, :] * SCALE, left)
            s = _dot_nt(qs, k_ref[rk, :]) + tab[3 * cfg + var, :, 0:nk]
            m = jnp.max(s, axis=-1, keepdims=True)
            p = jnp.exp(s - m)
            o_c[cfg][rq, :] = _unstack_heads(_dot(p, v_ref[rk, :]), left)
            m_c[cfg][rq, :] = _unstack_heads(m, left)
            l_c[cfg][rq, :] = _unstack_heads(jnp.sum(p, axis=-1, keepdims=True), left)
        _attn_blocks(block, 4)

        def merge(j, carry):
            rows = pl.ds(pl.multiple_of(j * 256, 256), 256)
            ms = [m_c[i][rows, :] for i in range(3)]
            top = jnp.maximum(jnp.maximum(ms[0], ms[1]), ms[2])
            ws = [jnp.exp(m - top) for m in ms]
            den = l_c[0][rows, :] * ws[0] + l_c[1][rows, :] * ws[1] + l_c[2][rows, :] * ws[2]
            num = o_c[0][rows, :] * ws[0] + o_c[1][rows, :] * ws[1] + o_c[2][rows, :] * ws[2]
            a_ref[rows, :] = num / den
            lse_ref[rows, :] = top + jnp.log(den)
            return carry
        lax.fori_loop(0, SEQ // 256, merge, 0)

    blk = lambda col0: pl.BlockSpec((SEQ, 128), lambda b, hp: (b, col0 // 128 + hp))
    out = pl.BlockSpec((SEQ, 128), lambda b, hp: (b, hp))
    return pl.pallas_call(
        body, name="attn_fwd", grid=(B_LOC, 4),
        in_specs=[pl.BlockSpec((1, 8, 128), lambda b, hp: (hp, 0, 0)), blk(C_QA), blk(C_KA), blk(C_VA)],
        out_specs=(out, out),
        out_shape=(SDS((T_LOC, ATTN_W), F32), SDS((T_LOC, ATTN_W), F32)),
        scratch_shapes=[pltpu.VMEM((SEQ, 128), F32)] * 9 + [pltpu.VMEM((N_BIAS, 2 * CHUNK, 2 * CHUNK), F32)],
        compiler_params=_params(40, ("arbitrary", "arbitrary")),
    )(slopes, proj, proj, proj)


def _chunks_side_by_side(v, pr, tm):
    return jnp.concatenate([v[ch * CHUNK:(ch + 1) * CHUNK, pr * 128:(pr + 1) * 128] for ch in range(tm // CHUNK)], axis=1)


def _first_group_lanes(tm):
    return lax.broadcasted_iota(jnp.int32, (CHUNK, tm), 1) % 128 < HEAD


def _store_chunks(dst_ref, pr, val, tm):
    for ch in range(tm // CHUNK):
        dst_ref[ch * CHUNK:(ch + 1) * CHUNK, pr * 128:(pr + 1) * 128] = val[:, ch * CHUNK:(ch + 1) * CHUNK]


def _sgu_mix(vn, ws_ref, dst_ref, tm):
    first = _first_group_lanes(tm)
    for pr in range(2):
        vp = _chunks_side_by_side(vn, pr, tm)
        _store_chunks(dst_ref, pr, jnp.where(first, _dot(ws_ref[2 * pr], vp), _dot(ws_ref[2 * pr + 1], vp)), tm)


def _mem_head_of_lane(rows):
    return lax.broadcasted_iota(jnp.int32, (rows, MEM_W), 1) // HEAD


def _stack_mem_heads(v, rows):
    head = _mem_head_of_lane(rows)
    return jnp.concatenate([jnp.where(head == h, v, 0.0) for h in range(4)], axis=0)


def _unstack_mem_heads(v, rows):
    head = _mem_head_of_lane(rows)
    out = v[0:rows]
    for h in range(1, 4):
        out = jnp.where(head == h, v[h * rows:(h + 1) * rows], out)
    return out


def _mem_attn_probs(q, kmem, rows):
    qs = _stack_mem_heads(q, rows).astype(BF16)
    s = _dot_nt(qs, kmem) * SCALE
    e = jnp.exp(s - jnp.max(s, axis=-1, keepdims=True))
    return e * (1.0 / jnp.sum(e, axis=-1, keepdims=True)), qs


def _branch_blocks(tm):
    col = lambda w, c0: pl.BlockSpec((tm, w), lambda i: (i, c0 // w))
    return [col(512, C_ZA), col(256, C_UB), col(256, C_VB), col(256, C_ZB), col(256, C_QM), col(256, C_ZM)]


def _branch_fwd(proj, a, kv, w_s, b_exp, g_v):
    tm = 256
    per_ex = SEQ // tm

    def body(za_ref, ub_ref, vb_ref, zb_ref, qm_ref, zm_ref, a_ref, kv_ref, ws_ref, be_ref, gv_ref, o_ref, mix):
        o_ref[:, 0:ATTN_W] = (_silu_and_grad(za_ref[...])[0] * a_ref[...]).astype(BF16)
        gu = _gelu_and_grad(ub_ref[...])[0]
        gv = _gelu_and_grad(vb_ref[...])[0]
        vn = gv * _rstd(gv) * gv_ref[...]
        _sgu_mix(vn.astype(BF16), ws_ref, mix, tm)
        sg = gu * (mix[...] + be_ref[...])
        o_ref[:, ATTN_W:ATTN_W + SGU_W] = (_silu_and_grad(zb_ref[...])[0] * sg).astype(BF16)
        p = _mem_attn_probs(qm_ref[...], kv_ref[:, 0:MEM_W], tm)[0]
        mo = _unstack_mem_heads(_dot(p, kv_ref[:, MEM_W:2 * MEM_W]), tm)
        o_ref[:, ATTN_W + SGU_W:D_MODEL] = (_silu_and_grad(zm_ref[...])[0] * mo).astype(BF16)

    full = lambda shape: pl.BlockSpec(shape, lambda i: (0,) * len(shape))
    return pl.pallas_call(
        body, name="branch_fwd", grid=(T_LOC // tm,),
        in_specs=_branch_blocks(tm) + [
            pl.BlockSpec((tm, ATTN_W), lambda i: (i, 0)), pl.BlockSpec((N_MEM, 2 * MEM_W), lambda i: (i // per_ex, 0)),
            full((4, CHUNK, CHUNK)), full((tm, SGU_W)), full((1, SGU_W))],
        out_specs=pl.BlockSpec((tm, D_MODEL), lambda i: (i, 0)),
        out_shape=SDS((T_LOC, D_MODEL), BF16),
        scratch_shapes=[pltpu.VMEM((tm, SGU_W), F32)],
        compiler_params=_params(VMEM_NO_STAGING_MIB, ("arbitrary",)),
    )(proj, proj, proj, proj, proj, proj, a, kv, w_s, b_exp, g_v)


def _outproj_loss(gated, wout, x2, tgt2, g_final):
    tm = 512

    def body(g_ref, w_ref, x_ref, t_ref, gf_ref, dh2_ref, loss_ref, dgf_ref):
        @pl.when(pl.program_id(0) == 0)
        def _():
            loss_ref[...] = jnp.zeros_like(loss_ref)
            dgf_ref[...] = jnp.zeros_like(dgf_ref)
        h2 = x_ref[...] + _dot(g_ref[...], w_ref[...])
        r = _rstd(h2)
        gf = gf_ref[...]
        err = h2 * r * gf - t_ref[...]
        loss_ref[...] += 0.5 * jnp.sum(jnp.mean(err * err, axis=-1, keepdims=True))
        dy = err * (1.0 / D_MODEL)
        dh2_ref[...] = _rms_bwd(h2, r, gf, dy)
        dgf_ref[...] += jnp.sum(dy * (h2 * r), axis=0, keepdims=True)

    row = pl.BlockSpec((tm, D_MODEL), lambda i: (i, 0))
    vec = pl.BlockSpec((1, D_MODEL), lambda i: (0, 0))
    return pl.pallas_call(
        body, name="outproj_loss", grid=(T_LOC // tm,),
        in_specs=[row, pl.BlockSpec((D_MODEL, D_MODEL), lambda i: (0, 0)), row, row, vec],
        out_specs=(row, pl.BlockSpec((8, 128), lambda i: (0, 0)), vec),
        out_shape=(SDS((T_LOC, D_MODEL), F32), SDS((8, 128), F32), SDS((1, D_MODEL), F32)),
        compiler_params=_params(VMEM_NO_STAGING_MIB, ("arbitrary",)),
    )(gated, wout, x2, tgt2, g_final)


def _branch_bwd(dh2, wout, gated, proj, a, kv, w_s, b_exp, g_v):
    tm = 256
    per_ex = SEQ // tm

    def body(dh2_ref, w_ref, g_ref, za_ref, ub_ref, vb_ref, zb_ref, qm_ref, zm_ref, a_ref, kv_ref, ws_ref,
             be_ref, gv_ref, da_ref, dr_ref, dkv_ref, dwo_ref, dws_ref, db_ref, dgv_ref, mix, dvn, dmsum, dwo_acc):
        i = pl.program_id(0)

        @pl.when(i == 0)
        def _():
            dwo_acc[...] = jnp.zeros_like(dwo_acc)
            dws_ref[...] = jnp.zeros_like(dws_ref)
            dgv_ref[...] = jnp.zeros_like(dgv_ref)
            dmsum[...] = jnp.zeros_like(dmsum)

        @pl.when(i % per_ex == 0)
        def _():
            dkv_ref[...] = jnp.zeros_like(dkv_ref)

        dh2 = dh2_ref[...].astype(BF16)
        dwo_acc[...] += _dot_tn(g_ref[...], dh2)
        dg = _dot_nt(dh2, w_ref[...])

        sa, dsa = _silu_and_grad(za_ref[...])
        dga = dg[:, 0:ATTN_W]
        da_ref[...] = dga * sa
        dr_ref[:, 0:512] = (dga * a_ref[...] * dsa).astype(BF16)

        ub, vb = ub_ref[...], vb_ref[...]
        gu, dgu = _gelu_and_grad(ub)
        gv, dgv = _gelu_and_grad(vb)
        rv = _rstd(gv)
        gain = gv_ref[...]
        vn = (gv * rv * gain).astype(BF16)
        _sgu_mix(vn, ws_ref, mix, tm)
        mixed = mix[...] + be_ref[...]
        sb, dsb = _silu_and_grad(zb_ref[...])
        dgb = dg[:, ATTN_W:ATTN_W + SGU_W]
        dsg = dgb * sb
        dr_ref[:, 512:768] = (dsg * mixed * dgu).astype(BF16)
        dr_ref[:, 1024:1280] = (dgb * (gu * mixed) * dsb).astype(BF16)
        dmix = dsg * gu
        for ch in range(tm // CHUNK):
            dmsum[...] += dmix[ch * CHUNK:(ch + 1) * CHUNK, :]
        first = _first_group_lanes(tm)
        for pr in range(2):
            dmp, vp = _chunks_side_by_side(dmix, pr, tm), _chunks_side_by_side(vn, pr, tm)
            dws_ref[2 * pr] += _dot_nt(jnp.where(first, dmp, 0.0), vp)
            dws_ref[2 * pr + 1] += _dot_nt(jnp.where(first, 0.0, dmp), vp)
            _store_chunks(dvn, pr, jnp.where(first, _dot_tn(ws_ref[2 * pr], dmp), _dot_tn(ws_ref[2 * pr + 1], dmp)), tm)
        dvn_v = dvn[...]
        dgv_ref[...] += jnp.sum(dvn_v * (gv * rv), axis=0, keepdims=True)
        dr_ref[:, 768:1024] = (_rms_bwd(gv, rv, gain, dvn_v) * dgv).astype(BF16)

        szm, dszm = _silu_and_grad(zm_ref[...])
        dgm = dg[:, ATTN_W + SGU_W:D_MODEL]
        kmem, vmem_ = kv_ref[:, 0:MEM_W].astype(BF16), kv_ref[:, MEM_W:2 * MEM_W].astype(BF16)
        p, qs = _mem_attn_probs(qm_ref[...], kmem, tm)
        dmos = _stack_mem_heads(dgm * szm, tm).astype(BF16)
        dr_ref[:, 1536:1792] = (dgm * _unstack_mem_heads(_dot(p, vmem_), tm) * dszm).astype(BF16)
        dp = _dot_nt(dmos, vmem_)
        ds = (p * (dp - jnp.sum(p * dp, axis=-1, keepdims=True)) * SCALE).astype(BF16)
        dr_ref[:, 1280:1536] = _unstack_mem_heads(_dot(ds, kmem), tm).astype(BF16)
        dkv_ref[:, 0:MEM_W] += _dot_tn(ds, qs)
        dkv_ref[:, MEM_W:2 * MEM_W] += _dot_tn(p, dmos)

        @pl.when(i == pl.num_programs(0) - 1)
        def _():
            tot = dmsum[...]
            hi = tot.astype(BF16)
            lo = (tot - hi.astype(F32)).astype(BF16)
            grp = (lax.broadcasted_iota(jnp.int32, (SGU_W, 128), 0) // HEAD
                   == lax.broadcasted_iota(jnp.int32, (SGU_W, 128), 1)).astype(BF16)
            db_ref[...] = (_dot(hi, grp) + _dot(lo, grp)).T[0:4, :]
            _cast_rows(dwo_acc, dwo_ref, D_MODEL)

    full = lambda shape: pl.BlockSpec(shape, lambda i: (0,) * len(shape))
    row = lambda w: pl.BlockSpec((tm, w), lambda i: (i, 0))
    return pl.pallas_call(
        body, name="branch_bwd", grid=(T_LOC // tm,),
        in_specs=[row(D_MODEL), full((D_MODEL, D_MODEL)), row(D_MODEL)] + _branch_blocks(tm) + [
            row(ATTN_W), pl.BlockSpec((N_MEM, 2 * MEM_W), lambda i: (i // per_ex, 0)),
            full((4, CHUNK, CHUNK)), full((tm, SGU_W)), full((1, SGU_W))],
        out_specs=(row(ATTN_W), row(REST_W), pl.BlockSpec((N_MEM, 2 * MEM_W), lambda i: (i // per_ex, 0)),
                   full((D_MODEL, D_MODEL)), full((4, CHUNK, CHUNK)), full((4, CHUNK)), full((1, SGU_W))),
        out_shape=(SDS((T_LOC, ATTN_W), F32), SDS((T_LOC, REST_W), BF16), SDS((B_LOC * N_MEM, 2 * MEM_W), F32),
                   SDS((D_MODEL, D_MODEL), BF16), SDS((4, CHUNK, CHUNK), F32), SDS((4, CHUNK), F32), SDS((1, SGU_W), F32)),
        scratch_shapes=[pltpu.VMEM((tm, SGU_W), F32), pltpu.VMEM((tm, SGU_W), F32), pltpu.VMEM((CHUNK, SGU_W), F32),
                        pltpu.VMEM((D_MODEL, D_MODEL), F32)],
        compiler_params=_params(56, ("arbitrary",)),
    )(dh2, wout, gated, proj, proj, proj, proj, proj, proj, a, kv, w_s, b_exp, g_v)


def _attn_bwd(proj, slopes, da, a, lse):
    def body(sl_ref, q_ref, k_ref, v_ref, da_ref, a_ref, lse_ref, dq_ref, dk_ref, dv_ref, *scr):
        dq_s, dk_s, dv_s, tab = scr[0:3], scr[3:6], scr[6:9], scr[9]
        lse_h, delta_h = scr[10:12], scr[12:14]
        p_all, ds_all = scr[14], scr[15]
        left = _left_lanes(CHUNK)
        _fill_bias_tables(sl_ref, tab)

        def prep(j, carry):
            rows = pl.ds(pl.multiple_of(j * 256, 256), 256)
            l256 = _left_lanes(256)
            prod = da_ref[rows, :] * a_ref[rows, :]
            delta_h[0][rows, :] = jnp.broadcast_to(jnp.sum(jnp.where(l256, prod, 0.0), axis=-1, keepdims=True), (256, 128))
            delta_h[1][rows, :] = jnp.broadcast_to(jnp.sum(jnp.where(l256, 0.0, prod), axis=-1, keepdims=True), (256, 128))
            pair = lse_ref[rows, :]
            other = pltpu.roll(pair, HEAD, axis=1)
            lse_h[0][rows, :] = jnp.where(l256, pair, other)
            lse_h[1][rows, :] = jnp.where(l256, other, pair)
            zero = jnp.zeros((256, 128), F32)
            for cfg in range(3):
                dk_s[cfg][rows, :] = zero
                dv_s[cfg][rows, :] = zero
            return carry
        lax.fori_loop(0, SEQ // 256, prep, 0)

        def per_row(halves, rq, nk):
            v = jnp.concatenate([halves[0][rq, :], halves[1][rq, :]], axis=0)
            return v if nk == 128 else jnp.concatenate([v, v], axis=1)

        def probs(cfg, var, q0, k0, nk, t):
            d = DILATIONS[cfg][0]
            rq, rk = _rows(q0, CHUNK, d), _rows(k0, nk, d)
            qs = _stack_heads(q_ref[rq, :] * SCALE, left)
            das = _stack_heads(da_ref[rq, :], left)
            s = _dot_nt(qs, k_ref[rk, :]) + tab[3 * cfg + var, :, 0:nk]
            p = jnp.exp(s - per_row(lse_h, rq, nk))
            p_all[16 * cfg + t, :, 0:nk] = p.astype(BF16)
            ds_all[16 * cfg + t, :, 0:nk] = (p * (_dot_nt(das, v_ref[rk, :]) - per_row(delta_h, rq, nk))).astype(BF16)
        _attn_blocks(probs, 4)

        def grads(cfg, var, q0, k0, nk, t):
            d = DILATIONS[cfg][0]
            rq, rk = _rows(q0, CHUNK, d), _rows(k0, nk, d)
            qs = _stack_heads(q_ref[rq, :] * SCALE, left).astype(BF16)
            das = _stack_heads(da_ref[rq, :], left).astype(BF16)
            p, ds = p_all[16 * cfg + t, :, 0:nk], ds_all[16 * cfg + t, :, 0:nk]
            dq_s[cfg][rq, :] = _unstack_heads(_dot(ds, k_ref[rk, :]), left) * SCALE
            dk_s[cfg][rk, :] += _dot_tn(ds, qs)
            dv_s[cfg][rk, :] += _dot_tn(p, das)
        _attn_blocks(grads, 4)

        def flush(j, carry):
            rows = pl.ds(pl.multiple_of(j * 256, 256), 256)
            for acc, dst in ((dq_s, dq_ref), (dk_s, dk_ref), (dv_s, dv_ref)):
                dst[rows, :] = (acc[0][rows, :] + acc[1][rows, :] + acc[2][rows, :]).astype(BF16)
            return carry
        lax.fori_loop(0, SEQ // 256, flush, 0)

    blk = lambda col0: pl.BlockSpec((SEQ, 128), lambda b, hp: (b, col0 // 128 + hp))
    own = pl.BlockSpec((SEQ, 128), lambda b, hp: (b, hp))
    return pl.pallas_call(
        body, name="attn_bwd", grid=(B_LOC, 4),
        in_specs=[pl.BlockSpec((1, 8, 128), lambda b, hp: (hp, 0, 0)), blk(C_QA), blk(C_KA), blk(C_VA), own, own, own],
        out_specs=(own, own, own),
        out_shape=(SDS((T_LOC, ATTN_W), BF16),) * 3,
        scratch_shapes=[pltpu.VMEM((SEQ, 128), F32)] * 9 + [pltpu.VMEM((N_BIAS, 2 * CHUNK, 2 * CHUNK), F32)]
        + [pltpu.VMEM((SEQ, 128), F32)] * 4 + [pltpu.VMEM((48, 2 * CHUNK, 2 * CHUNK), BF16)] * 2,
        compiler_params=_params(52, ("arbitrary", "arbitrary")),
    )(slopes, proj, proj, proj, da, a, lse)


def _dproj_specs(tm):
    third = pl.BlockSpec((tm, ATTN_W), lambda i: (i, 0))
    return [third, third, third, pl.BlockSpec((tm, REST_W), lambda i: (i, 0))]


def _dx(dq, dk, dv, dr, wint, x2, dh2, g_norm):
    tm = 256

    def body(dq_ref, dk_ref, dv_ref, dr_ref, w_ref, x_ref, dh2_ref, g_ref, gx_ref, dgn_ref):
        @pl.when(pl.program_id(0) == 0)
        def _():
            dgn_ref[...] = jnp.zeros_like(dgn_ref)
        dh = (_dot(dq_ref[...], w_ref[C_QA:C_KA, :]) + _dot(dk_ref[...], w_ref[C_KA:C_VA, :])
              + _dot(dv_ref[...], w_ref[C_VA:C_ZA, :]) + _dot(dr_ref[...], w_ref[C_ZA:IN_COLS, :]))
        xv = x_ref[...]
        r = _rstd(xv)
        gx_ref[...] = dh2_ref[...] + _rms_bwd(xv, r, g_ref[...], dh)
        dgn_ref[...] += jnp.sum(dh * (xv * r), axis=0, keepdims=True)

    row = pl.BlockSpec((tm, D_MODEL), lambda i: (i, 0))
    vec = pl.BlockSpec((1, D_MODEL), lambda i: (0, 0))
    return pl.pallas_call(
        body, name="dx", grid=(T_LOC // tm,),
        in_specs=_dproj_specs(tm) + [pl.BlockSpec((IN_COLS, D_MODEL), lambda i: (0, 0)), row, row, vec],
        out_specs=(row, vec),
        out_shape=(SDS((T_LOC, D_MODEL), F32), SDS((1, D_MODEL), F32)),
        compiler_params=_params(48, ("arbitrary",)),
    )(dq, dk, dv, dr, wint, x2, dh2, g_norm)


def _dwin(dq, dk, dv, dr, x2, g_norm):
    tm = 512

    def body(dq_ref, dk_ref, dv_ref, dr_ref, x_ref, g_ref, o_ref, acc):
        @pl.when(pl.program_id(0) == 0)
        def _():
            acc[...] = jnp.zeros_like(acc)
        xv = x_ref[...]
        h = (xv * _rstd(xv) * g_ref[...]).astype(BF16)
        acc[C_QA:C_KA, :] += _dot_tn(dq_ref[...], h)
        acc[C_KA:C_VA, :] += _dot_tn(dk_ref[...], h)
        acc[C_VA:C_ZA, :] += _dot_tn(dv_ref[...], h)
        acc[C_ZA:IN_COLS, :] += _dot_tn(dr_ref[...], h)

        @pl.when(pl.program_id(0) == pl.num_programs(0) - 1)
        def _():
            _cast_rows(acc, o_ref, IN_COLS)

    return pl.pallas_call(
        body, name="dwin", grid=(T_LOC // tm,),
        in_specs=_dproj_specs(tm) + [pl.BlockSpec((tm, D_MODEL), lambda i: (i, 0)), pl.BlockSpec((1, D_MODEL), lambda i: (0, 0))],
        out_specs=pl.BlockSpec((IN_COLS, D_MODEL), lambda i: (0, 0)),
        out_shape=SDS((IN_COLS, D_MODEL), BF16),
        scratch_shapes=[pltpu.VMEM((IN_COLS, D_MODEL), F32)],
        compiler_params=_params(56, ("arbitrary",)),
    )(dq, dk, dv, dr, x2, g_norm)


def _memkv_bwd(dkv, mem2, g_mem, wkv):
    def body(dkv_ref, m_ref, g_ref, w_ref, dw_ref, dg_ref):
        mv = m_ref[...]
        r = _rstd(mv)
        dkv_v = dkv_ref[...].astype(BF16)
        dw_ref[...] = _dot_tn(mv * r * g_ref[...], dkv_v).astype(BF16)
        dg_ref[...] = jnp.sum(_dot_nt(dkv_v, w_ref[...]) * (mv * r), axis=0, keepdims=True)

    return pl.pallas_call(
        body, name="memkv_bwd", out_shape=(SDS((D_MODEL, 2 * MEM_W), BF16), SDS((1, D_MODEL), F32)),
        compiler_params=_params(32),
    )(dkv, mem2, g_mem, wkv)


def _allreduce_small(parts):
    n = len(parts)

    def body(*refs):
        ins, outs, bufs = refs[0:n], refs[n:2 * n], refs[2 * n:3 * n]
        send_sems, recv_sems = refs[3 * n], refs[3 * n + 1]
        pos = _mesh_pos()
        me = _flat(pos)
        for a in range(n):
            bufs[a][me] = ins[a][...]

        def copy(a, k, slot):
            return pltpu.make_async_remote_copy(
                src_ref=ins[a], dst_ref=bufs[a].at[slot],
                send_sem=send_sems.at[7 * a + k - 1], recv_sem=recv_sems.at[7 * a + k - 1],
                device_id=_peer(pos, k), device_id_type=MESH)

        sent = [copy(a, k, me) for a in range(n) for k in range(1, N_DEV)]
        for cp in sent:
            cp.start()
        for a in range(n):
            for k in range(1, N_DEV):
                copy(a, k, _flat(_peer(pos, k))).wait_recv()
        for cp in sent:
            cp.wait_send()
        for a in range(n):
            acc = bufs[a][0]
            for s in range(1, N_DEV):
                acc = acc + bufs[a][s]
            outs[a][...] = acc

    vmem = pl.BlockSpec(memory_space=pltpu.VMEM)
    return pl.pallas_call(
        body, name="allreduce_small",
        out_shape=tuple(SDS(p.shape, F32) for p in parts),
        in_specs=[vmem] * n, out_specs=(vmem,) * n,
        scratch_shapes=[pltpu.VMEM((N_DEV,) + p.shape, F32) for p in parts]
        + [pltpu.SemaphoreType.DMA((7 * n,)), pltpu.SemaphoreType.DMA((7 * n,))],
        compiler_params=_params(16),
    )(*parts)


_HBM = pl.BlockSpec(memory_space=pltpu.HBM)
_SEM = pl.BlockSpec(memory_space=pltpu.SEMAPHORE)
_SIDE_EFFECT = pltpu.SideEffectType.DATAFLOW_SIDE_EFFECTING


def _exchange_copies(src_refs, land_refs, scatter, send_sems, recv_sems):
    pos = _mesh_pos()
    copies = []
    for a, (src, land) in enumerate(zip(src_refs, land_refs)):
        n = land.shape[1]
        for k in range(1, N_DEV):
            peer = _peer(pos, k)
            piece = src.at[pl.ds(pl.multiple_of(_flat(peer) * n, 16), n), :] if scatter[a] else src
            copies.append(pltpu.make_async_remote_copy(
                src_ref=piece, dst_ref=land.at[_flat(pos)],
                send_sem=send_sems.at[7 * a + k - 1], recv_sem=recv_sems.at[7 * a + k - 1],
                device_id=peer, device_id_type=MESH))
    return copies


def _exchange_start(name, srcs, scatter, lands):
    n = len(srcs)

    def body(*refs):
        for cp in _exchange_copies(refs[0:n], refs[n:2 * n], scatter, refs[2 * n], refs[2 * n + 1]):
            cp.start()
        refs[-1][...] = jnp.zeros_like(refs[-1])

    ops = [pltpu.with_memory_space_constraint(t, pltpu.HBM) for t in (*srcs, *lands)]
    out = pl.pallas_call(
        body, name=name,
        out_shape=(pltpu.SemaphoreType.DMA((7 * n,)), pltpu.SemaphoreType.DMA((7 * n,)),
                   *[pltpu.HBM(t.shape, t.dtype) for t in ops], SDS((8, 128), F32)),
        in_specs=[_HBM] * (2 * n),
        out_specs=(_SEM, _SEM, *[_HBM] * (2 * n), pl.BlockSpec(memory_space=pltpu.VMEM)),
        input_output_aliases={i: 2 + i for i in range(2 * n)},
        compiler_params=pltpu.CompilerParams(has_side_effects=_SIDE_EFFECT),
    )(*ops)
    return out[0], out[1], out[2:2 + n], out[2 + n:2 + 2 * n], out[-1]


def _exchange_wait(name, started, scatter, after):
    send_sems, recv_sems, srcs, lands, _ = started
    n = len(srcs)

    def body(*refs):
        for cp in _exchange_copies(refs[0:n], refs[n:2 * n], scatter, refs[2 * n], refs[2 * n + 1]):
            cp.wait_send()
            cp.wait_recv()

    out = pl.pallas_call(
        body, name=name,
        out_shape=tuple(pltpu.HBM(t.shape, t.dtype) for t in (*srcs, *lands)),
        in_specs=[_HBM] * (2 * n) + [_SEM, _SEM, pl.BlockSpec(memory_space=pl.ANY)],
        out_specs=(_HBM,) * (2 * n),
        input_output_aliases={i: i for i in range(2 * n)},
        compiler_params=pltpu.CompilerParams(has_side_effects=_SIDE_EFFECT),
    )(*srcs, *lands, send_sems, recv_sems, after)
    return out[n:]


def _landing(own, me):
    return lax.dynamic_update_slice(lax.empty((N_DEV,) + own.shape, own.dtype), own[None], (me,) + (0,) * own.ndim)


def _adamw(w, g, m, v):
    m = ADAM_B1 * m + (1.0 - ADAM_B1) * g
    v = ADAM_B2 * v + (1.0 - ADAM_B2) * (g * g)
    m_hat = m / (1.0 - ADAM_B1 ** ADAM_STEP)
    v_hat = v / (1.0 - ADAM_B2 ** ADAM_STEP)
    return -ADAM_LR * (m_hat / (jnp.sqrt(v_hat) + ADAM_EPS) + ADAM_WD * w), m, v


def _adam_slots(name, slots, w, m, v):
    _, rows, cols = slots.shape

    def body(s_ref, w_ref, m_ref, v_ref, g_o, d_o, m_o, v_o, acc):
        s = pl.program_id(0)

        @pl.when(s == 0)
        def _():
            acc[...] = s_ref[0].astype(F32)

        @pl.when(s > 0)
        def _():
            acc[...] += s_ref[0].astype(F32)

        @pl.when(s == N_DEV - 1)
        def _():
            g = acc[...]
            g_o[...] = g
            d_o[...], m_o[...], v_o[...] = _adamw(w_ref[...], g, m_ref[...], v_ref[...])

    full = pl.BlockSpec((rows, cols), lambda s: (0, 0))
    return pl.pallas_call(
        body, name="adam_" + name, grid=(N_DEV,),
        in_specs=[pl.BlockSpec((1, rows, cols), lambda s: (s, 0, 0)), full, full, full],
        out_specs=(full,) * 4, out_shape=(SDS((rows, cols), F32),) * 4,
        scratch_shapes=[pltpu.VMEM((rows, cols), F32)],
        compiler_params=_params(40, ("arbitrary",)),
    )(slots, w, m, v)


def _adam_small(ws, gs, ms, vs, loss_slots):
    n = len(ws)

    def total(ref, like):
        if len(ref.shape) == len(like.shape):
            return ref[...]
        acc = ref[0]
        for s in range(1, N_DEV):
            acc = acc + ref[s]
        return acc

    def body(*refs):
        w_r, g_r, m_r, v_r = refs[0:n], refs[n:2 * n], refs[2 * n:3 * n], refs[3 * n:4 * n]
        loss_r, outs = refs[4 * n], refs[4 * n + 1:]
        for a in range(n):
            g = total(g_r[a], w_r[a])
            outs[a][...] = g
            outs[n + 1 + 3 * a][...], outs[n + 2 + 3 * a][...], outs[n + 3 + 3 * a][...] = _adamw(
                w_r[a][...], g, m_r[a][...], v_r[a][...])
        outs[n][...] = total(loss_r, outs[n])

    out = pl.pallas_call(
        body, name="adam_small",
        out_shape=tuple(SDS(w.shape, F32) for w in ws) + (SDS(loss_slots.shape[1:], F32),)
        + tuple(SDS(w.shape, F32) for w in ws for _ in range(3)),
        compiler_params=_params(16),
    )(*ws, *gs, *ms, *vs, loss_slots)
    return out[0:n], out[n], out[n + 1:]


def kernel(x, mem, g_norm, w_in, w_sgu_spatial, b_sgu_spatial, g_sgu_v, g_mem, w_mem_kv, w_out, g_final, loss_target, m_g_norm, m_w_in, m_w_sgu_spatial, m_b_sgu_spatial, m_g_sgu_v, m_g_mem, m_w_mem_kv, m_w_out, m_g_final, v_g_norm, v_w_in, v_w_sgu_spatial, v_b_sgu_spatial, v_g_sgu_v, v_g_mem, v_w_mem_kv, v_w_out, v_g_final):
    x2 = x.reshape(T_LOC, D_MODEL)
    tgt2 = loss_target.reshape(T_LOC, D_MODEL)
    mem2 = mem.reshape(B_LOC * N_MEM, D_MODEL)
    w_s = w_sgu_spatial[0]
    b_exp = jnp.tile(jnp.repeat(b_sgu_spatial[0].T, HEAD, axis=1), (2, 1))
    slope = jnp.power(2.0, -8.0 * (jnp.arange(8, dtype=F32) + 1.0) / 8)
    slopes = jnp.broadcast_to(jnp.repeat(slope.reshape(4, 2), HEAD, axis=1)[:, None, :], (4, 8, 128))

    tr = lambda t: jnp.transpose(t[0])

    me = _flat(_mesh_pos())
    own_rows = lambda t: lax.dynamic_slice_in_dim(t, me * (t.shape[0] // N_DEV), t.shape[0] // N_DEV)

    wint, wkv_own, wout_own = _allgather_weights(tr(w_in), w_mem_kv[0], w_out[0])
    started0 = _exchange_start("exchange0_start", [wkv_own, wout_own], [False, False],
                               [_landing(wkv_own, me), _landing(wout_own, me)])
    proj = _proj_fwd(x2, g_norm + started0[4][0:1, 0:1], wint)
    wkv, wout = _exchange_wait("exchange0_wait", started0, [False, False], proj)
    wkv, wout = wkv.reshape(D_MODEL, 2 * MEM_W), wout.reshape(D_MODEL, D_MODEL)
    kv = _memkv_fwd(mem2, g_mem, wkv)
    a, lse = _attn_fwd(proj, slopes)
    gated = _branch_fwd(proj, a, kv, w_s, b_exp, g_sgu_v)
    dh2, loss8, dgf = _outproj_loss(gated, wout, x2, tgt2, g_final.reshape(1, D_MODEL))

    da, dr, dkv, dwout, dws, dbs, dgv = _branch_bwd(dh2, wout, gated, proj, a, kv, w_s, b_exp, g_sgu_v)
    dwkv, dgm = _memkv_bwd(dkv, mem2, g_mem, wkv)

    early = [dws.reshape(4 * CHUNK, CHUNK), dbs, dgv, dgm, dgf, loss8]
    scatter1 = [True, True] + [False] * len(early)
    started1 = _exchange_start(
        "exchange1_start", [dwkv, dwout] + early, scatter1,
        [_landing(own_rows(dwkv), me), _landing(own_rows(dwout), me)] + [_landing(t, me) for t in early])
    dq, dk, dv = _attn_bwd(proj, slopes + started1[4][0:1, 0:1], da, a, lse)
    s_wkv, s_wout, s_ws, s_bs, s_gv, s_gm, s_gf, s_loss = _exchange_wait("exchange1_wait", started1, scatter1, dq)

    dwint = _dwin(dq, dk, dv, dr, x2, g_norm)
    started2 = _exchange_start("exchange2_start", [dwint], [True], [_landing(own_rows(dwint), me)])
    grad_x, dgn = _dx(dq, dk, dv, dr, wint, x2, dh2, g_norm + started2[4][0:1, 0:1])
    dgn_sum, = _allreduce_small([dgn])
    s_win, = _exchange_wait("exchange2_wait", started2, [True], dgn_sum)

    g_win, d_win, m_win, v_win = map(
        jnp.transpose, _adam_slots("w_in", s_win, tr(w_in), tr(m_w_in), tr(v_w_in)))
    g_wkv, d_wkv, m_wkv, v_wkv = _adam_slots("w_mem_kv", s_wkv, w_mem_kv[0], m_w_mem_kv[0], v_w_mem_kv[0])
    g_wout, d_wout, m_wout, v_wout = _adam_slots("w_out", s_wout, w_out[0], m_w_out[0], v_w_out[0])

    small_shapes = [(1, D_MODEL), (4 * CHUNK, CHUNK), (4, CHUNK), (1, SGU_W), (1, D_MODEL), (1, D_MODEL)]
    pack = lambda arrs: [t.reshape(s) for t, s in zip(arrs, small_shapes)]
    g_small, loss_sum, upd = _adam_small(
        pack([g_norm, w_sgu_spatial, b_sgu_spatial, g_sgu_v, g_mem, g_final]),
        [dgn_sum, s_ws, s_bs, s_gv, s_gm, s_gf],
        pack([m_g_norm, m_w_sgu_spatial, m_b_sgu_spatial, m_g_sgu_v, m_g_mem, m_g_final]),
        pack([v_g_norm, v_w_sgu_spatial, v_b_sgu_spatial, v_g_sgu_v, v_g_mem, v_g_final]), s_loss)
    out_shapes = [g_norm.shape, w_sgu_spatial.shape, b_sgu_spatial.shape, g_sgu_v.shape, g_mem.shape, g_final.shape]
    unpack = lambda arrs: [t.reshape(s) for t, s in zip(arrs, out_shapes)]
    gs = unpack(g_small)
    ds, nms, nvs = unpack(upd[0::3]), unpack(upd[1::3]), unpack(upd[2::3])

    loss = loss_sum[0, 0]

    def assemble(small, win, wkv_, wout_):
        return [small[0], win[None], small[1], small[2], small[3], small[4], wkv_[None], wout_[None], small[5]]

    return (loss, grad_x.reshape(x.shape),
            *assemble(gs, g_win, g_wkv, g_wout), *assemble(ds, d_win, d_wkv, d_wout),
            *assemble(nms, m_win, m_wkv, m_wout), *assemble(nvs, v_win, v_wkv, v_wout))
```

```python
import functools

import jax
import jax.numpy as jnp
from jax import lax
from jax.experimental import pallas as pl
from jax.experimental.pallas import tpu as pltpu

F32 = jnp.float32
BF16 = jnp.bfloat16
SDS = jax.ShapeDtypeStruct
MESH = pl.DeviceIdType.MESH

N_DEV = 8
D_MODEL = 1024
SEQ = 2048
B_LOC = 2
T_LOC = B_LOC * SEQ
N_MEM = 256
HEAD = 64
ATTN_W = 512
SGU_W = 256
MEM_W = 256
IN_COLS = 3328
W_IN_SHARD = IN_COLS // N_DEV
ROW_SHARD = D_MODEL // N_DEV
CHUNK = 128
DILATIONS = ((1, 2048), (4, 512), (16, 128))
RADIUS = 64
EPS = 1e-6
NEG = -1e30
SCALE = HEAD ** -0.5
C_QA, C_KA, C_VA, C_ZA, C_UB, C_VB, C_ZB, C_QM, C_ZM = 0, 512, 1024, 1536, 2048, 2304, 2560, 2816, 3072
QKV_W = 1536
REST_W = IN_COLS - QKV_W

ADAM_LR, ADAM_B1, ADAM_B2, ADAM_EPS, ADAM_WD, ADAM_STEP = 0.001, 0.9, 0.999, 1e-08, 0.01, 10

V7X_VMEM_MIB = 64
VMEM_NO_STAGING_MIB = V7X_VMEM_MIB - 6


def _params(vmem_mib, sem=None):
    assert vmem_mib < V7X_VMEM_MIB
    return pltpu.CompilerParams(vmem_limit_bytes=vmem_mib << 20, dimension_semantics=sem)


def _dot(a, b):
    return jnp.dot(a.astype(BF16), b.astype(BF16), preferred_element_type=F32)


def _dot_nt(a, b):
    return lax.dot_general(a.astype(BF16), b.astype(BF16), (((1,), (1,)), ((), ())), preferred_element_type=F32)


def _dot_tn(a, b):
    return lax.dot_general(a.astype(BF16), b.astype(BF16), (((0,), (0,)), ((), ())), preferred_element_type=F32)


def _rstd(v):
    return lax.rsqrt(jnp.mean(v * v, axis=-1, keepdims=True) + EPS)


def _rms_bwd(v, r, g, dy):
    gdy = g * dy
    return r * gdy - v * (r * r * r * jnp.mean(gdy * v, axis=-1, keepdims=True))


def _sigmoid(z):
    return 1.0 / (1.0 + jnp.exp(-z))


def _silu_and_grad(z):
    s = _sigmoid(z)
    return z * s, s * (1.0 + z * (1.0 - s))


_G_C = 0.7978845608028654
_G_K = 0.044715


def _gelu_and_grad(v):
    t = jnp.tanh(_G_C * (v + _G_K * (v * v * v)))
    cdf = 0.5 * (1.0 + t)
    return v * cdf, cdf + 0.5 * v * (1.0 - t * t) * (_G_C * (1.0 + 3.0 * _G_K * v * v))


def _cast_rows(src_ref, dst_ref, rows, step=256):
    def one(i, carry):
        r = pl.ds(pl.multiple_of(i * step, step), step)
        dst_ref[r, :] = src_ref[r, :].astype(dst_ref.dtype)
        return carry
    lax.fori_loop(0, rows // step, one, 0)


def _left_lanes(rows):
    return lax.broadcasted_iota(jnp.int32, (rows, 128), 1) < HEAD


def _mesh_pos():
    return lax.axis_index("x"), lax.axis_index("y"), lax.axis_index("c")


def _peer(pos, k):
    x, y, c = pos
    return (1 - x if k & 4 else x, 1 - y if k & 2 else y, 1 - c if k & 1 else c)


def _flat(pos):
    return 4 * pos[0] + 2 * pos[1] + pos[2]


def _allgather_weights(w_in_t, w_kv, w_out):
    def body(win_ref, wkv_ref, wout_ref, wint_o, wkv_o, wout_o, send_sems, recv_sems):
        x, y, c = _mesh_pos()
        me, sib = (x, y, c), (x, y, 1 - c)
        chips = [(1 - x, y), (x, 1 - y), (1 - x, 1 - y)]

        def rows(p):
            return wint_o.at[pl.ds(pl.multiple_of(_flat(p) * W_IN_SHARD, 16), W_IN_SHARD), :]

        rows(me)[...] = win_ref[...].astype(BF16)

        def copy(k, block, to):
            return pltpu.make_async_remote_copy(
                src_ref=rows(block), dst_ref=rows(block), send_sem=send_sems.at[k], recv_sem=recv_sems.at[k],
                device_id=to, device_id_type=MESH)

        first = [copy(0, me, sib)] + [copy(1 + j, me, (*chip, c)) for j, chip in enumerate(chips)]
        for cp in first:
            cp.start()
        wkv_o[...] = wkv_ref[...].astype(BF16)
        wout_o[...] = wout_ref[...].astype(BF16)
        passed = []
        for j, chip in enumerate(chips):
            copy(1 + j, (*chip, c), me).wait_recv()
            fwd = copy(4 + j, (*chip, c), sib)
            fwd.start()
            passed.append(fwd)
        copy(0, sib, me).wait_recv()
        for j, chip in enumerate(chips):
            copy(4 + j, (*chip, 1 - c), me).wait_recv()
        for cp in first + passed:
            cp.wait_send()

    vmem = pl.BlockSpec(memory_space=pltpu.VMEM)
    return pl.pallas_call(
        body, name="allgather_weights",
        out_shape=(SDS((IN_COLS, D_MODEL), BF16), SDS(w_kv.shape, BF16), SDS(w_out.shape, BF16)),
        in_specs=[vmem, vmem, vmem], out_specs=(vmem, vmem, vmem),
        scratch_shapes=[pltpu.SemaphoreType.DMA((7,)), pltpu.SemaphoreType.DMA((7,))],
        compiler_params=_params(40),
    )(w_in_t, w_kv, w_out)


def _proj_fwd(x2, g_norm, wint):
    tm = 256

    def body(x_ref, g_ref, w_ref, o_ref):
        xv = x_ref[...]
        h = xv * _rstd(xv) * g_ref[...]
        o_ref[...] = _dot_nt(h, w_ref[...])

    return pl.pallas_call(
        body, name="proj_fwd", grid=(T_LOC // tm,),
        in_specs=[pl.BlockSpec((tm, D_MODEL), lambda i: (i, 0)), pl.BlockSpec((1, D_MODEL), lambda i: (0, 0)),
                  pl.BlockSpec((IN_COLS, D_MODEL), lambda i: (0, 0))],
        out_specs=pl.BlockSpec((tm, IN_COLS), lambda i: (i, 0)),
        out_shape=SDS((T_LOC, IN_COLS), F32),
        compiler_params=_params(48, ("arbitrary",)),
    )(x2, g_norm, wint)


def _memkv_fwd(mem2, g_mem, wkv):
    def body(m_ref, g_ref, w_ref, o_ref):
        mv = m_ref[...]
        o_ref[...] = _dot(mv * _rstd(mv) * g_ref[...], w_ref[...])

    return pl.pallas_call(
        body, name="memkv_fwd", out_shape=SDS((B_LOC * N_MEM, 2 * MEM_W), F32), compiler_params=_params(32),
    )(mem2, g_mem, wkv)


N_BIAS = 7


def _fill_bias_tables(sl_ref, tab):
    for cfg, (d, length) in enumerate(DILATIONS):
        nk = min(length, 2 * CHUNK)
        r = lax.broadcasted_iota(jnp.int32, (CHUNK, nk), 0)
        c = lax.broadcasted_iota(jnp.int32, (CHUNK, nk), 1)
        for var in range(3 if length > nk else 1):
            rel = jnp.abs(r - c + var * RADIUS)
            dist = rel.astype(F32) * float(d)
            for h in range(2):
                slope = sl_ref[0, 0:1, h * HEAD:h * HEAD + 1]
                tab[3 * cfg + var, h * CHUNK:(h + 1) * CHUNK, 0:nk] = jnp.where(rel <= RADIUS, -slope * dist, NEG)


def _attn_blocks(visit, unroll):
    def step(t, carry):
        for cfg, (d, length) in enumerate(DILATIONS):
            nblk = length // CHUNK
            if nblk == 1:
                visit(cfg, 0, t, t, length, t)
                continue
            rho, i = (0, t) if d == 1 else (t // nblk, t % nblk)
            ks = jnp.clip(i * CHUNK - RADIUS, 0, length - 2 * CHUNK)
            visit(cfg, (i * CHUNK - ks) // RADIUS, rho + d * (i * CHUNK), rho + d * ks, 2 * CHUNK, t)
        return carry
    lax.fori_loop(0, 16, step, 0, unroll=unroll)


def _stack_heads(v, left):
    return jnp.concatenate([jnp.where(left, v, 0.0), jnp.where(left, 0.0, v)], axis=0)


def _unstack_heads(v, left):
    return jnp.where(left, v[0:CHUNK], v[CHUNK:2 * CHUNK])


def _rows(start, n, d):
    return pl.ds(start, n) if d == 1 else pl.ds(start, n, stride=d)


def _attn_fwd(proj, slopes):
    def body(sl_ref, q_ref, k_ref, v_ref, a_ref, lse_ref, *scr):
        o_c, m_c, l_c, tab = scr[0:3], scr[3:6], scr[6:9], scr[9]
        left = _left_lanes(CHUNK)
        _fill_bias_tables(sl_ref, tab)

        def block(cfg, var, q0, k0, nk, t):
            d = DILATIONS[cfg][0]
            rq, rk = _rows(q0, CHUNK, d), _rows(k0, nk, d)
            qs = _stack_heads(q_ref[rq, :] * SCALE, left)
            s = _dot_nt(qs, k_ref[rk, :]) + tab[3 * cfg + var, :, 0:nk]
            m = jnp.max(s, axis=-1, keepdims=True)
            p = jnp.exp(s - m)
            o_c[cfg][rq, :] = _unstack_heads(_dot(p, v_ref[rk, :]), left)
            m_c[cfg][rq, :] = _unstack_heads(m, left)
            l_c[cfg][rq, :] = _unstack_heads(jnp.sum(p, axis=-1, keepdims=True), left)
        _attn_blocks(block, 4)

        def merge(j, carry):
            rows = pl.ds(pl.multiple_of(j * 256, 256), 256)
            ms = [m_c[i][rows, :] for i in range(3)]
            top = jnp.maximum(jnp.maximum(ms[0], ms[1]), ms[2])
            ws = [jnp.exp(m - top) for m in ms]
            den = l_c[0][rows, :] * ws[0] + l_c[1][rows, :] * ws[1] + l_c[2][rows, :] * ws[2]
            num = o_c[0][rows, :] * ws[0] + o_c[1][rows, :] * ws[1] + o_c[2][rows, :] * ws[2]
            a_ref[rows, :] = num / den
            lse_ref[rows, :] = top + jnp.log(den)
            return carry
        lax.fori_loop(0, SEQ // 256, merge, 0)

    blk = lambda col0: pl.BlockSpec((SEQ, 128), lambda b, hp: (b, col0 // 128 + hp))
    out = pl.BlockSpec((SEQ, 128), lambda b, hp: (b, hp))
    return pl.pallas_call(
        body, name="attn_fwd", grid=(B_LOC, 4),
        in_specs=[pl.BlockSpec((1, 8, 128), lambda b, hp: (hp, 0, 0)), blk(C_QA), blk(C_KA), blk(C_VA)],
        out_specs=(out, out),
        out_shape=(SDS((T_LOC, ATTN_W), F32), SDS((T_LOC, ATTN_W), F32)),
        scratch_shapes=[pltpu.VMEM((SEQ, 128), F32)] * 9 + [pltpu.VMEM((N_BIAS, 2 * CHUNK, 2 * CHUNK), F32)],
        compiler_params=_params(40, ("arbitrary", "arbitrary")),
    )(slopes, proj, proj, proj)


def _chunks_side_by_side(v, pr, tm):
    return jnp.concatenate([v[ch * CHUNK:(ch + 1) * CHUNK, pr * 128:(pr + 1) * 128] for ch in range(tm // CHUNK)], axis=1)


def _first_group_lanes(tm):
    return lax.broadcasted_iota(jnp.int32, (CHUNK, tm), 1) % 128 < HEAD


def _store_chunks(dst_ref, pr, val, tm):
    for ch in range(tm // CHUNK):
        dst_ref[ch * CHUNK:(ch + 1) * CHUNK, pr * 128:(pr + 1) * 128] = val[:, ch * CHUNK:(ch + 1) * CHUNK]


def _sgu_mix(vn, ws_ref, dst_ref, tm):
    first = _first_group_lanes(tm)
    for pr in range(2):
        vp = _chunks_side_by_side(vn, pr, tm)
        _store_chunks(dst_ref, pr, jnp.where(first, _dot(ws_ref[2 * pr], vp), _dot(ws_ref[2 * pr + 1], vp)), tm)


def _mem_head_of_lane(rows):
    return lax.broadcasted_iota(jnp.int32, (rows, MEM_W), 1) // HEAD


def _stack_mem_heads(v, rows):
    head = _mem_head_of_lane(rows)
    return jnp.concatenate([jnp.where(head == h, v, 0.0) for h in range(4)], axis=0)


def _unstack_mem_heads(v, rows):
    head = _mem_head_of_lane(rows)
    out = v[0:rows]
    for h in range(1, 4):
        out = jnp.where(head == h, v[h * rows:(h + 1) * rows], out)
    return out


def _mem_attn_probs(q, kmem, rows):
    qs = _stack_mem_heads(q, rows).astype(BF16)
    s = _dot_nt(qs, kmem) * SCALE
    e = jnp.exp(s - jnp.max(s, axis=-1, keepdims=True))
    return e * (1.0 / jnp.sum(e, axis=-1, keepdims=True)), qs


def _branch_blocks(tm):
    col = lambda w, c0: pl.BlockSpec((tm, w), lambda i: (i, c0 // w))
    return [col(512, C_ZA), col(256, C_UB), col(256, C_VB), col(256, C_ZB), col(256, C_QM), col(256, C_ZM)]


def _branch_fwd(proj, a, kv, w_s, b_exp, g_v):
    tm = 256
    per_ex = SEQ // tm

    def body(za_ref, ub_ref, vb_ref, zb_ref, qm_ref, zm_ref, a_ref, kv_ref, ws_ref, be_ref, gv_ref, o_ref, mix):
        o_ref[:, 0:ATTN_W] = (_silu_and_grad(za_ref[...])[0] * a_ref[...]).astype(BF16)
        gu = _gelu_and_grad(ub_ref[...])[0]
        gv = _gelu_and_grad(vb_ref[...])[0]
        vn = gv * _rstd(gv) * gv_ref[...]
        _sgu_mix(vn.astype(BF16), ws_ref, mix, tm)
        sg = gu * (mix[...] + be_ref[...])
        o_ref[:, ATTN_W:ATTN_W + SGU_W] = (_silu_and_grad(zb_ref[...])[0] * sg).astype(BF16)
        p = _mem_attn_probs(qm_ref[...], kv_ref[:, 0:MEM_W], tm)[0]
        mo = _unstack_mem_heads(_dot(p, kv_ref[:, MEM_W:2 * MEM_W]), tm)
        o_ref[:, ATTN_W + SGU_W:D_MODEL] = (_silu_and_grad(zm_ref[...])[0] * mo).astype(BF16)

    full = lambda shape: pl.BlockSpec(shape, lambda i: (0,) * len(shape))
    return pl.pallas_call(
        body, name="branch_fwd", grid=(T_LOC // tm,),
        in_specs=_branch_blocks(tm) + [
            pl.BlockSpec((tm, ATTN_W), lambda i: (i, 0)), pl.BlockSpec((N_MEM, 2 * MEM_W), lambda i: (i // per_ex, 0)),
            full((4, CHUNK, CHUNK)), full((tm, SGU_W)), full((1, SGU_W))],
        out_specs=pl.BlockSpec((tm, D_MODEL), lambda i: (i, 0)),
        out_shape=SDS((T_LOC, D_MODEL), BF16),
        scratch_shapes=[pltpu.VMEM((tm, SGU_W), F32)],
        compiler_params=_params(VMEM_NO_STAGING_MIB, ("arbitrary",)),
    )(proj, proj, proj, proj, proj, proj, a, kv, w_s, b_exp, g_v)


def _outproj_loss(gated, wout, x2, tgt2, g_final):
    tm = 512

    def body(g_ref, w_ref, x_ref, t_ref, gf_ref, dh2_ref, loss_ref, dgf_ref):
        @pl.when(pl.program_id(0) == 0)
        def _():
            loss_ref[...] = jnp.zeros_like(loss_ref)
            dgf_ref[...] = jnp.zeros_like(dgf_ref)
        h2 = x_ref[...] + _dot(g_ref[...], w_ref[...])
        r = _rstd(h2)
        gf = gf_ref[...]
        err = h2 * r * gf - t_ref[...]
        loss_ref[...] += 0.5 * jnp.sum(jnp.mean(err * err, axis=-1, keepdims=True))
        dy = err * (1.0 / D_MODEL)
        dh2_ref[...] = _rms_bwd(h2, r, gf, dy)
        dgf_ref[...] += jnp.sum(dy * (h2 * r), axis=0, keepdims=True)

    row = pl.BlockSpec((tm, D_MODEL), lambda i: (i, 0))
    vec = pl.BlockSpec((1, D_MODEL), lambda i: (0, 0))
    return pl.pallas_call(
        body, name="outproj_loss", grid=(T_LOC // tm,),
        in_specs=[row, pl.BlockSpec((D_MODEL, D_MODEL), lambda i: (0, 0)), row, row, vec],
        out_specs=(row, pl.BlockSpec((8, 128), lambda i: (0, 0)), vec),
        out_shape=(SDS((T_LOC, D_MODEL), F32), SDS((8, 128), F32), SDS((1, D_MODEL), F32)),
        compiler_params=_params(VMEM_NO_STAGING_MIB, ("arbitrary",)),
    )(gated, wout, x2, tgt2, g_final)


def _branch_bwd(dh2, wout, gated, proj, a, kv, w_s, b_exp, g_v):
    tm = 256
    per_ex = SEQ // tm

    def body(dh2_ref, w_ref, g_ref, za_ref, ub_ref, vb_ref, zb_ref, qm_ref, zm_ref, a_ref, kv_ref, ws_ref,
             be_ref, gv_ref, da_ref, dr_ref, dkv_ref, dwo_ref, dws_ref, db_ref, dgv_ref, mix, dvn, dmsum, dwo_acc):
        i = pl.program_id(0)

        @pl.when(i == 0)
        def _():
            dwo_acc[...] = jnp.zeros_like(dwo_acc)
            dws_ref[...] = jnp.zeros_like(dws_ref)
            dgv_ref[...] = jnp.zeros_like(dgv_ref)
            dmsum[...] = jnp.zeros_like(dmsum)

        @pl.when(i % per_ex == 0)
        def _():
            dkv_ref[...] = jnp.zeros_like(dkv_ref)

        dh2 = dh2_ref[...].astype(BF16)
        dwo_acc[...] += _dot_tn(g_ref[...], dh2)
        dg = _dot_nt(dh2, w_ref[...])

        sa, dsa = _silu_and_grad(za_ref[...])
        dga = dg[:, 0:ATTN_W]
        da_ref[...] = dga * sa
        dr_ref[:, 0:512] = (dga * a_ref[...] * dsa).astype(BF16)

        ub, vb = ub_ref[...], vb_ref[...]
        gu, dgu = _gelu_and_grad(ub)
        gv, dgv = _gelu_and_grad(vb)
        rv = _rstd(gv)
        gain = gv_ref[...]
        vn = (gv * rv * gain).astype(BF16)
        _sgu_mix(vn, ws_ref, mix, tm)
        mixed = mix[...] + be_ref[...]
        sb, dsb = _silu_and_grad(zb_ref[...])
        dgb = dg[:, ATTN_W:ATTN_W + SGU_W]
        dsg = dgb * sb
        dr_ref[:, 512:768] = (dsg * mixed * dgu).astype(BF16)
        dr_ref[:, 1024:1280] = (dgb * (gu * mixed) * dsb).astype(BF16)
        dmix = dsg * gu
        for ch in range(tm // CHUNK):
            dmsum[...] += dmix[ch * CHUNK:(ch + 1) * CHUNK, :]
        first = _first_group_lanes(tm)
        for pr in range(2):
            dmp, vp = _chunks_side_by_side(dmix, pr, tm), _chunks_side_by_side(vn, pr, tm)
            dws_ref[2 * pr] += _dot_nt(jnp.where(first, dmp, 0.0), vp)
            dws_ref[2 * pr + 1] += _dot_nt(jnp.where(first, 0.0, dmp), vp)
            _store_chunks(dvn, pr, jnp.where(first, _dot_tn(ws_ref[2 * pr], dmp), _dot_tn(ws_ref[2 * pr + 1], dmp)), tm)
        dvn_v = dvn[...]
        dgv_ref[...] += jnp.sum(dvn_v * (gv * rv), axis=0, keepdims=True)
        dr_ref[:, 768:1024] = (_rms_bwd(gv, rv, gain, dvn_v) * dgv).astype(BF16)

        szm, dszm = _silu_and_grad(zm_ref[...])
        dgm = dg[:, ATTN_W + SGU_W:D_MODEL]
        kmem, vmem_ = kv_ref[:, 0:MEM_W].astype(BF16), kv_ref[:, MEM_W:2 * MEM_W].astype(BF16)
        p, qs = _mem_attn_probs(qm_ref[...], kmem, tm)
        dmos = _stack_mem_heads(dgm * szm, tm).astype(BF16)
        dr_ref[:, 1536:1792] = (dgm * _unstack_mem_heads(_dot(p, vmem_), tm) * dszm).astype(BF16)
        dp = _dot_nt(dmos, vmem_)
        ds = (p * (dp - jnp.sum(p * dp, axis=-1, keepdims=True)) * SCALE).astype(BF16)
        dr_ref[:, 1280:1536] = _unstack_mem_heads(_dot(ds, kmem), tm).astype(BF16)
        dkv_ref[:, 0:MEM_W] += _dot_tn(ds, qs)
        dkv_ref[:, MEM_W:2 * MEM_W] += _dot_tn(p, dmos)

        @pl.when(i == pl.num_programs(0) - 1)
        def _():
            tot = dmsum[...]
            hi = tot.astype(BF16)
            lo = (tot - hi.astype(F32)).astype(BF16)
            grp = (lax.broadcasted_iota(jnp.int32, (SGU_W, 128), 0) // HEAD
                   == lax.broadcasted_iota(jnp.int32, (SGU_W, 128), 1)).astype(BF16)
            db_ref[...] = (_dot(hi, grp) + _dot(lo, grp)).T[0:4, :]
            _cast_rows(dwo_acc, dwo_ref, D_MODEL)

    full = lambda shape: pl.BlockSpec(shape, lambda i: (0,) * len(shape))
    row = lambda w: pl.BlockSpec((tm, w), lambda i: (i, 0))
    return pl.pallas_call(
        body, name="branch_bwd", grid=(T_LOC // tm,),
        in_specs=[row(D_MODEL), full((D_MODEL, D_MODEL)), row(D_MODEL)] + _branch_blocks(tm) + [
            row(ATTN_W), pl.BlockSpec((N_MEM, 2 * MEM_W), lambda i: (i // per_ex, 0)),
            full((4, CHUNK, CHUNK)), full((tm, SGU_W)), full((1, SGU_W))],
        out_specs=(row(ATTN_W), row(REST_W), pl.BlockSpec((N_MEM, 2 * MEM_W), lambda i: (i // per_ex, 0)),
                   full((D_MODEL, D_MODEL)), full((4, CHUNK, CHUNK)), full((4, CHUNK)), full((1, SGU_W))),
        out_shape=(SDS((T_LOC, ATTN_W), F32), SDS((T_LOC, REST_W), BF16), SDS((B_LOC * N_MEM, 2 * MEM_W), F32),
                   SDS((D_MODEL, D_MODEL), BF16), SDS((4, CHUNK, CHUNK), F32), SDS((4, CHUNK), F32), SDS((1, SGU_W), F32)),
        scratch_shapes=[pltpu.VMEM((tm, SGU_W), F32), pltpu.VMEM((tm, SGU_W), F32), pltpu.VMEM((CHUNK, SGU_W), F32),
                        pltpu.VMEM((D_MODEL, D_MODEL), F32)],
        compiler_params=_params(56, ("arbitrary",)),
    )(dh2, wout, gated, proj, proj, proj, proj, proj, proj, a, kv, w_s, b_exp, g_v)


def _attn_bwd(proj, slopes, da, a, lse):
    def body(sl_ref, q_ref, k_ref, v_ref, da_ref, a_ref, lse_ref, dq_ref, dk_ref, dv_ref, *scr):
        dq_s, dk_s, dv_s, tab = scr[0:3], scr[3:6], scr[6:9], scr[9]
        lse_h, delta_h = scr[10:12], scr[12:14]
        p_all, ds_all = scr[14], scr[15]
        left = _left_lanes(CHUNK)
        _fill_bias_tables(sl_ref, tab)

        def prep(j, carry):
            rows = pl.ds(pl.multiple_of(j * 256, 256), 256)
            l256 = _left_lanes(256)
            prod = da_ref[rows, :] * a_ref[rows, :]
            delta_h[0][rows, :] = jnp.broadcast_to(jnp.sum(jnp.where(l256, prod, 0.0), axis=-1, keepdims=True), (256, 128))
            delta_h[1][rows, :] = jnp.broadcast_to(jnp.sum(jnp.where(l256, 0.0, prod), axis=-1, keepdims=True), (256, 128))
            pair = lse_ref[rows, :]
            other = pltpu.roll(pair, HEAD, axis=1)
            lse_h[0][rows, :] = jnp.where(l256, pair, other)
            lse_h[1][rows, :] = jnp.where(l256, other, pair)
            zero = jnp.zeros((256, 128), F32)
            for cfg in range(3):
                dk_s[cfg][rows, :] = zero
                dv_s[cfg][rows, :] = zero
            return carry
        lax.fori_loop(0, SEQ // 256, prep, 0)

        def per_row(halves, rq, nk):
            v = jnp.concatenate([halves[0][rq, :], halves[1][rq, :]], axis=0)
            return v if nk == 128 else jnp.concatenate([v, v], axis=1)

        def probs(cfg, var, q0, k0, nk, t):
            d = DILATIONS[cfg][0]
            rq, rk = _rows(q0, CHUNK, d), _rows(k0, nk, d)
            qs = _stack_heads(q_ref[rq, :] * SCALE, left)
            das = _stack_heads(da_ref[rq, :], left)
            s = _dot_nt(qs, k_ref[rk, :]) + tab[3 * cfg + var, :, 0:nk]
            p = jnp.exp(s - per_row(lse_h, rq, nk))
            p_all[16 * cfg + t, :, 0:nk] = p.astype(BF16)
            ds_all[16 * cfg + t, :, 0:nk] = (p * (_dot_nt(das, v_ref[rk, :]) - per_row(delta_h, rq, nk))).astype(BF16)
        _attn_blocks(probs, 4)

        def grads(cfg, var, q0, k0, nk, t):
            d = DILATIONS[cfg][0]
            rq, rk = _rows(q0, CHUNK, d), _rows(k0, nk, d)
            qs = _stack_heads(q_ref[rq, :] * SCALE, left).astype(BF16)
            das = _stack_heads(da_ref[rq, :], left).astype(BF16)
            p, ds = p_all[16 * cfg + t, :, 0:nk], ds_all[16 * cfg + t, :, 0:nk]
            dq_s[cfg][rq, :] = _unstack_heads(_dot(ds, k_ref[rk, :]), left) * SCALE
            dk_s[cfg][rk, :] += _dot_tn(ds, qs)
            dv_s[cfg][rk, :] += _dot_tn(p, das)
        _attn_blocks(grads, 4)

        def flush(j, carry):
            rows = pl.ds(pl.multiple_of(j * 256, 256), 256)
            for acc, dst in ((dq_s, dq_ref), (dk_s, dk_ref), (dv_s, dv_ref)):
                dst[rows, :] = (acc[0][rows, :] + acc[1][rows, :] + acc[2][rows, :]).astype(BF16)
            return carry
        lax.fori_loop(0, SEQ // 256, flush, 0)

    blk = lambda col0: pl.BlockSpec((SEQ, 128), lambda b, hp: (b, col0 // 128 + hp))
    own = pl.BlockSpec((SEQ, 128), lambda b, hp: (b, hp))
    return pl.pallas_call(
        body, name="attn_bwd", grid=(B_LOC, 4),
        in_specs=[pl.BlockSpec((1, 8, 128), lambda b, hp: (hp, 0, 0)), blk(C_QA), blk(C_KA), blk(C_VA), own, own, own],
        out_specs=(own, own, own),
        out_shape=(SDS((T_LOC, ATTN_W), BF16),) * 3,
        scratch_shapes=[pltpu.VMEM((SEQ, 128), F32)] * 9 + [pltpu.VMEM((N_BIAS, 2 * CHUNK, 2 * CHUNK), F32)]
        + [pltpu.VMEM((SEQ, 128), F32)] * 4 + [pltpu.VMEM((48, 2 * CHUNK, 2 * CHUNK), BF16)] * 2,
        compiler_params=_params(52, ("arbitrary", "arbitrary")),
    )(slopes, proj, proj, proj, da, a, lse)


def _dproj_specs(tm):
    third = pl.BlockSpec((tm, ATTN_W), lambda i: (i, 0))
    return [third, third, third, pl.BlockSpec((tm, REST_W), lambda i: (i, 0))]


def _dx(dq, dk, dv, dr, wint, x2, dh2, g_norm):
    tm = 256

    def body(dq_ref, dk_ref, dv_ref, dr_ref, w_ref, x_ref, dh2_ref, g_ref, gx_ref, dgn_ref):
        @pl.when(pl.program_id(0) == 0)
        def _():
            dgn_ref[...] = jnp.zeros_like(dgn_ref)
        dh = (_dot(dq_ref[...], w_ref[C_QA:C_KA, :]) + _dot(dk_ref[...], w_ref[C_KA:C_VA, :])
              + _dot(dv_ref[...], w_ref[C_VA:C_ZA, :]) + _dot(dr_ref[...], w_ref[C_ZA:IN_COLS, :]))
        xv = x_ref[...]
        r = _rstd(xv)
        gx_ref[...] = dh2_ref[...] + _rms_bwd(xv, r, g_ref[...], dh)
        dgn_ref[...] += jnp.sum(dh * (xv * r), axis=0, keepdims=True)

    row = pl.BlockSpec((tm, D_MODEL), lambda i: (i, 0))
    vec = pl.BlockSpec((1, D_MODEL), lambda i: (0, 0))
    return pl.pallas_call(
        body, name="dx", grid=(T_LOC // tm,),
        in_specs=_dproj_specs(tm) + [pl.BlockSpec((IN_COLS, D_MODEL), lambda i: (0, 0)), row, row, vec],
        out_specs=(row, vec),
        out_shape=(SDS((T_LOC, D_MODEL), F32), SDS((1, D_MODEL), F32)),
        compiler_params=_params(48, ("arbitrary",)),
    )(dq, dk, dv, dr, wint, x2, dh2, g_norm)


def _dwin(dq, dk, dv, dr, x2, g_norm, half):
    tm = 512
    width = D_MODEL // 2
    cols = slice(half * width, (half + 1) * width)

    def body(dq_ref, dk_ref, dv_ref, dr_ref, x_ref, g_ref, o_ref, acc):
        @pl.when(pl.program_id(0) == 0)
        def _():
            acc[...] = jnp.zeros_like(acc)
        xv = x_ref[...]
        h = (xv[:, cols] * _rstd(xv) * g_ref[:, cols]).astype(BF16)
        acc[C_QA:C_KA, :] += _dot_tn(dq_ref[...], h)
        acc[C_KA:C_VA, :] += _dot_tn(dk_ref[...], h)
        acc[C_VA:C_ZA, :] += _dot_tn(dv_ref[...], h)
        acc[C_ZA:IN_COLS, :] += _dot_tn(dr_ref[...], h)

        @pl.when(pl.program_id(0) == pl.num_programs(0) - 1)
        def _():
            _cast_rows(acc, o_ref, IN_COLS)

    return pl.pallas_call(
        body, name="dwin%d" % half, grid=(T_LOC // tm,),
        in_specs=_dproj_specs(tm) + [pl.BlockSpec((tm, D_MODEL), lambda i: (i, 0)), pl.BlockSpec((1, D_MODEL), lambda i: (0, 0))],
        out_specs=pl.BlockSpec((IN_COLS, width), lambda i: (0, 0)),
        out_shape=SDS((IN_COLS, width), BF16),
        scratch_shapes=[pltpu.VMEM((IN_COLS, width), F32)],
        compiler_params=_params(48, ("arbitrary",)),
    )(dq, dk, dv, dr, x2, g_norm)


def _memkv_bwd(dkv, mem2, g_mem, wkv):
    def body(dkv_ref, m_ref, g_ref, w_ref, dw_ref, dg_ref):
        mv = m_ref[...]
        r = _rstd(mv)
        dkv_v = dkv_ref[...].astype(BF16)
        dw_ref[...] = _dot_tn(mv * r * g_ref[...], dkv_v).astype(BF16)
        dg_ref[...] = jnp.sum(_dot_nt(dkv_v, w_ref[...]) * (mv * r), axis=0, keepdims=True)

    return pl.pallas_call(
        body, name="memkv_bwd", out_shape=(SDS((D_MODEL, 2 * MEM_W), BF16), SDS((1, D_MODEL), F32)),
        compiler_params=_params(32),
    )(dkv, mem2, g_mem, wkv)


def _allreduce_small(parts):
    n = len(parts)

    def body(*refs):
        ins, outs, bufs = refs[0:n], refs[n:2 * n], refs[2 * n:3 * n]
        send_sems, recv_sems = refs[3 * n], refs[3 * n + 1]
        pos = _mesh_pos()
        me = _flat(pos)
        for a in range(n):
            bufs[a][me] = ins[a][...]

        def copy(a, k, slot):
            return pltpu.make_async_remote_copy(
                src_ref=ins[a], dst_ref=bufs[a].at[slot],
                send_sem=send_sems.at[7 * a + k - 1], recv_sem=recv_sems.at[7 * a + k - 1],
                device_id=_peer(pos, k), device_id_type=MESH)

        sent = [copy(a, k, me) for a in range(n) for k in range(1, N_DEV)]
        for cp in sent:
            cp.start()
        for a in range(n):
            for k in range(1, N_DEV):
                copy(a, k, _flat(_peer(pos, k))).wait_recv()
        for cp in sent:
            cp.wait_send()
        for a in range(n):
            acc = bufs[a][0]
            for s in range(1, N_DEV):
                acc = acc + bufs[a][s]
            outs[a][...] = acc

    vmem = pl.BlockSpec(memory_space=pltpu.VMEM)
    return pl.pallas_call(
        body, name="allreduce_small",
        out_shape=tuple(SDS(p.shape, F32) for p in parts),
        in_specs=[vmem] * n, out_specs=(vmem,) * n,
        scratch_shapes=[pltpu.VMEM((N_DEV,) + p.shape, F32) for p in parts]
        + [pltpu.SemaphoreType.DMA((7 * n,)), pltpu.SemaphoreType.DMA((7 * n,))],
        compiler_params=_params(16),
    )(*parts)


_HBM = pl.BlockSpec(memory_space=pltpu.HBM)
_SEM = pl.BlockSpec(memory_space=pltpu.SEMAPHORE)
_SIDE_EFFECT = pltpu.SideEffectType.DATAFLOW_SIDE_EFFECTING


def _exchange_copies(src_refs, land_refs, scatter, send_sems, recv_sems):
    pos = _mesh_pos()
    copies = []
    for a, (src, land) in enumerate(zip(src_refs, land_refs)):
        n = land.shape[1]
        for k in range(1, N_DEV):
            peer = _peer(pos, k)
            piece = src.at[pl.ds(pl.multiple_of(_flat(peer) * n, 16), n), :] if scatter[a] else src
            copies.append(pltpu.make_async_remote_copy(
                src_ref=piece, dst_ref=land.at[_flat(pos)],
                send_sem=send_sems.at[7 * a + k - 1], recv_sem=recv_sems.at[7 * a + k - 1],
                device_id=peer, device_id_type=MESH))
    return copies


def _exchange_start(name, srcs, scatter, lands):
    n = len(srcs)

    def body(*refs):
        for cp in _exchange_copies(refs[0:n], refs[n:2 * n], scatter, refs[2 * n], refs[2 * n + 1]):
            cp.start()
        refs[-1][...] = jnp.zeros_like(refs[-1])

    ops = [pltpu.with_memory_space_constraint(t, pltpu.HBM) for t in (*srcs, *lands)]
    out = pl.pallas_call(
        body, name=name,
        out_shape=(pltpu.SemaphoreType.DMA((7 * n,)), pltpu.SemaphoreType.DMA((7 * n,)),
                   *[pltpu.HBM(t.shape, t.dtype) for t in ops], SDS((8, 128), F32)),
        in_specs=[_HBM] * (2 * n),
        out_specs=(_SEM, _SEM, *[_HBM] * (2 * n), pl.BlockSpec(memory_space=pltpu.VMEM)),
        input_output_aliases={i: 2 + i for i in range(2 * n)},
        compiler_params=pltpu.CompilerParams(has_side_effects=_SIDE_EFFECT),
    )(*ops)
    return out[0], out[1], out[2:2 + n], out[2 + n:2 + 2 * n], out[-1]


def _exchange_wait(name, started, scatter, after):
    send_sems, recv_sems, srcs, lands, _ = started
    n = len(srcs)

    def body(*refs):
        for cp in _exchange_copies(refs[0:n], refs[n:2 * n], scatter, refs[2 * n], refs[2 * n + 1]):
            cp.wait_send()
            cp.wait_recv()

    out = pl.pallas_call(
        body, name=name,
        out_shape=tuple(pltpu.HBM(t.shape, t.dtype) for t in (*srcs, *lands)),
        in_specs=[_HBM] * (2 * n) + [_SEM, _SEM, pl.BlockSpec(memory_space=pl.ANY)],
        out_specs=(_HBM,) * (2 * n),
        input_output_aliases={i: i for i in range(2 * n)},
        compiler_params=pltpu.CompilerParams(has_side_effects=_SIDE_EFFECT),
    )(*srcs, *lands, send_sems, recv_sems, after)
    return out[n:]


def _landing(own, me):
    return lax.dynamic_update_slice(lax.empty((N_DEV,) + own.shape, own.dtype), own[None], (me,) + (0,) * own.ndim)


def _adamw(w, g, m, v):
    m = ADAM_B1 * m + (1.0 - ADAM_B1) * g
    v = ADAM_B2 * v + (1.0 - ADAM_B2) * (g * g)
    m_hat = m / (1.0 - ADAM_B1 ** ADAM_STEP)
    v_hat = v / (1.0 - ADAM_B2 ** ADAM_STEP)
    return -ADAM_LR * (m_hat / (jnp.sqrt(v_hat) + ADAM_EPS) + ADAM_WD * w), m, v


def _adam_slots(name, pieces, w, m, v):
    rows, cols = w.shape
    starts = [sum(p.shape[2] for p in pieces[:i]) for i in range(len(pieces) + 1)]
    assert starts[-1] == cols and all(p.shape[1] == rows for p in pieces)

    def body(*refs):
        s_refs, (w_ref, m_ref, v_ref, g_o, d_o, m_o, v_o, acc) = refs[:len(pieces)], refs[len(pieces):]
        s = pl.program_id(0)

        @pl.when(s == 0)
        def _():
            for i, s_ref in enumerate(s_refs):
                acc[:, starts[i]:starts[i + 1]] = s_ref[0].astype(F32)

        @pl.when(s > 0)
        def _():
            for i, s_ref in enumerate(s_refs):
                acc[:, starts[i]:starts[i + 1]] += s_ref[0].astype(F32)

        @pl.when(s == N_DEV - 1)
        def _():
            g = acc[...]
            g_o[...] = g
            d_o[...], m_o[...], v_o[...] = _adamw(w_ref[...], g, m_ref[...], v_ref[...])

    full = pl.BlockSpec((rows, cols), lambda s: (0, 0))
    return pl.pallas_call(
        body, name="adam_" + name, grid=(N_DEV,),
        in_specs=[pl.BlockSpec((1, rows, p.shape[2]), lambda s: (s, 0, 0)) for p in pieces] + [full, full, full],
        out_specs=(full,) * 4, out_shape=(SDS((rows, cols), F32),) * 4,
        scratch_shapes=[pltpu.VMEM((rows, cols), F32)],
        compiler_params=_params(40, ("arbitrary",)),
    )(*pieces, w, m, v)


def _adam_small(ws, gs, ms, vs, loss_slots):
    n = len(ws)

    def total(ref, like):
        if len(ref.shape) == len(like.shape):
            return ref[...]
        acc = ref[0]
        for s in range(1, N_DEV):
            acc = acc + ref[s]
        return acc

    def body(*refs):
        w_r, g_r, m_r, v_r = refs[0:n], refs[n:2 * n], refs[2 * n:3 * n], refs[3 * n:4 * n]
        loss_r, outs = refs[4 * n], refs[4 * n + 1:]
        for a in range(n):
            g = total(g_r[a], w_r[a])
            outs[a][...] = g
            outs[n + 1 + 3 * a][...], outs[n + 2 + 3 * a][...], outs[n + 3 + 3 * a][...] = _adamw(
                w_r[a][...], g, m_r[a][...], v_r[a][...])
        outs[n][...] = total(loss_r, outs[n])

    out = pl.pallas_call(
        body, name="adam_small",
        out_shape=tuple(SDS(w.shape, F32) for w in ws) + (SDS(loss_slots.shape[1:], F32),)
        + tuple(SDS(w.shape, F32) for w in ws for _ in range(3)),
        compiler_params=_params(16),
    )(*ws, *gs, *ms, *vs, loss_slots)
    return out[0:n], out[n], out[n + 1:]


def kernel(x, mem, g_norm, w_in, w_sgu_spatial, b_sgu_spatial, g_sgu_v, g_mem, w_mem_kv, w_out, g_final, loss_target, m_g_norm, m_w_in, m_w_sgu_spatial, m_b_sgu_spatial, m_g_sgu_v, m_g_mem, m_w_mem_kv, m_w_out, m_g_final, v_g_norm, v_w_in, v_w_sgu_spatial, v_b_sgu_spatial, v_g_sgu_v, v_g_mem, v_w_mem_kv, v_w_out, v_g_final):
    x2 = x.reshape(T_LOC, D_MODEL)
    tgt2 = loss_target.reshape(T_LOC, D_MODEL)
    mem2 = mem.reshape(B_LOC * N_MEM, D_MODEL)
    w_s = w_sgu_spatial[0]
    b_exp = jnp.tile(jnp.repeat(b_sgu_spatial[0].T, HEAD, axis=1), (2, 1))
    slope = jnp.power(2.0, -8.0 * (jnp.arange(8, dtype=F32) + 1.0) / 8)
    slopes = jnp.broadcast_to(jnp.repeat(slope.reshape(4, 2), HEAD, axis=1)[:, None, :], (4, 8, 128))

    tr = lambda t: jnp.transpose(t[0])

    me = _flat(_mesh_pos())
    own_rows = lambda t: lax.dynamic_slice_in_dim(t, me * (t.shape[0] // N_DEV), t.shape[0] // N_DEV)

    wint, wkv_own, wout_own = _allgather_weights(tr(w_in), w_mem_kv[0], w_out[0])
    started0 = _exchange_start("exchange0_start", [wkv_own, wout_own], [False, False],
                               [_landing(wkv_own, me), _landing(wout_own, me)])
    proj = _proj_fwd(x2, g_norm + started0[4][0:1, 0:1], wint)
    wkv, wout = _exchange_wait("exchange0_wait", started0, [False, False], proj)
    wkv, wout = wkv.reshape(D_MODEL, 2 * MEM_W), wout.reshape(D_MODEL, D_MODEL)
    kv = _memkv_fwd(mem2, g_mem, wkv)
    a, lse = _attn_fwd(proj, slopes)
    gated = _branch_fwd(proj, a, kv, w_s, b_exp, g_sgu_v)
    dh2, loss8, dgf = _outproj_loss(gated, wout, x2, tgt2, g_final.reshape(1, D_MODEL))

    da, dr, dkv, dwout, dws, dbs, dgv = _branch_bwd(dh2, wout, gated, proj, a, kv, w_s, b_exp, g_sgu_v)
    dwkv, dgm = _memkv_bwd(dkv, mem2, g_mem, wkv)

    early = [dws.reshape(4 * CHUNK, CHUNK), dbs, dgv, dgm, dgf, loss8]
    scatter1 = [True, True] + [False] * len(early)
    started1 = _exchange_start(
        "exchange1_start", [dwkv, dwout] + early, scatter1,
        [_landing(own_rows(dwkv), me), _landing(own_rows(dwout), me)] + [_landing(t, me) for t in early])
    dq, dk, dv = _attn_bwd(proj, slopes + started1[4][0:1, 0:1], da, a, lse)
    s_wkv, s_wout, s_ws, s_bs, s_gv, s_gm, s_gf, s_loss = _exchange_wait("exchange1_wait", started1, scatter1, dq)

    dwint0 = _dwin(dq, dk, dv, dr, x2, g_norm, 0)
    started2 = _exchange_start("exchange2_start", [dwint0], [True], [_landing(own_rows(dwint0), me)])
    dwint1 = _dwin(dq, dk, dv, dr, x2, g_norm + started2[4][0:1, 0:1], 1)
    started3 = _exchange_start("exchange3_start", [dwint1], [True], [_landing(own_rows(dwint1), me)])
    grad_x, dgn = _dx(dq, dk, dv, dr, wint, x2, dh2, g_norm + started3[4][0:1, 0:1])
    s_win0, = _exchange_wait("exchange2_wait", started2, [True], grad_x)
    dgn_sum, = _allreduce_small([dgn])
    s_win1, = _exchange_wait("exchange3_wait", started3, [True], dgn_sum)

    g_win, d_win, m_win, v_win = map(
        jnp.transpose, _adam_slots("w_in", [s_win0, s_win1], tr(w_in), tr(m_w_in), tr(v_w_in)))
    g_wkv, d_wkv, m_wkv, v_wkv = _adam_slots("w_mem_kv", [s_wkv], w_mem_kv[0], m_w_mem_kv[0], v_w_mem_kv[0])
    g_wout, d_wout, m_wout, v_wout = _adam_slots("w_out", [s_wout], w_out[0], m_w_out[0], v_w_out[0])

    small_shapes = [(1, D_MODEL), (4 * CHUNK, CHUNK), (4, CHUNK), (1, SGU_W), (1, D_MODEL), (1, D_MODEL)]
    pack = lambda arrs: [t.reshape(s) for t, s in zip(arrs, small_shapes)]
    g_small, loss_sum, upd = _adam_small(
        pack([g_norm, w_sgu_spatial, b_sgu_spatial, g_sgu_v, g_mem, g_final]),
        [dgn_sum, s_ws, s_bs, s_gv, s_gm, s_gf],
        pack([m_g_norm, m_w_sgu_spatial, m_b_sgu_spatial, m_g_sgu_v, m_g_mem, m_g_final]),
        pack([v_g_norm, v_w_sgu_spatial, v_b_sgu_spatial, v_g_sgu_v, v_g_mem, v_g_final]), s_loss)
    out_shapes = [g_norm.shape, w_sgu_spatial.shape, b_sgu_spatial.shape, g_sgu_v.shape, g_mem.shape, g_final.shape]
    unpack = lambda arrs: [t.reshape(s) for t, s in zip(arrs, out_shapes)]
    gs = unpack(g_small)
    ds, nms, nvs = unpack(upd[0::3]), unpack(upd[1::3]), unpack(upd[2::3])

    loss = loss_sum[0, 0]

    def assemble(small, win, wkv_, wout_):
        return [small[0], win[None], small[1], small[2], small[3], small[4], wkv_[None], wout_[None], small[5]]

    return (loss, grad_x.reshape(x.shape),
            *assemble(gs, g_win, g_wkv, g_wout), *assemble(ds, d_win, d_wkv, d_wout),
            *assemble(nms, m_win, m_wkv, m_wout), *assemble(nvs, v_win, v_wkv, v_wout))
```

```python
import jax
import jax.numpy as jnp
from jax import lax
from jax.experimental import pallas as pl
from jax.experimental.pallas import tpu as pltpu

F32 = jnp.float32
BF16 = jnp.bfloat16
SDS = jax.ShapeDtypeStruct
MESH = pl.DeviceIdType.MESH

N_DEV = 8
D_MODEL = 1024
SEQ = 2048
B_LOC = 2
T_LOC = B_LOC * SEQ
N_MEM = 256
HEAD = 64
ATTN_W = 512
SGU_W = 256
MEM_W = 256
IN_COLS = 3328
W_IN_SHARD = IN_COLS // N_DEV
CHUNK = 128
DILATIONS = ((1, 2048), (4, 512), (16, 128))
RADIUS = 64
EPS = 1e-6
NEG = -1e30
SCALE = HEAD ** -0.5
C_QA, C_KA, C_VA, C_ZA, C_UB, C_VB, C_ZB, C_QM, C_ZM = 0, 512, 1024, 1536, 2048, 2304, 2560, 2816, 3072
QKV_W = 1536
REST_W = IN_COLS - QKV_W

ADAM_LR, ADAM_B1, ADAM_B2, ADAM_EPS, ADAM_WD, ADAM_STEP = 0.001, 0.9, 0.999, 1e-08, 0.01, 10

V7X_VMEM_MIB = 64
VMEM_NO_STAGING_MIB = V7X_VMEM_MIB - 6


def _params(vmem_mib, sem=None):
    assert vmem_mib < V7X_VMEM_MIB
    return pltpu.CompilerParams(vmem_limit_bytes=vmem_mib << 20, dimension_semantics=sem)


def _dot(a, b):
    return jnp.dot(a.astype(BF16), b.astype(BF16), preferred_element_type=F32)


def _dot_nt(a, b):
    return lax.dot_general(a.astype(BF16), b.astype(BF16), (((1,), (1,)), ((), ())), preferred_element_type=F32)


def _dot_tn(a, b):
    return lax.dot_general(a.astype(BF16), b.astype(BF16), (((0,), (0,)), ((), ())), preferred_element_type=F32)


def _rstd(v):
    return lax.rsqrt(jnp.mean(v * v, axis=-1, keepdims=True) + EPS)


def _rms_bwd(v, r, g, dy):
    gdy = g * dy
    return r * gdy - v * (r * r * r * jnp.mean(gdy * v, axis=-1, keepdims=True))


def _sigmoid(z):
    return 1.0 / (1.0 + jnp.exp(-z))


def _silu_and_grad(z):
    s = _sigmoid(z)
    return z * s, s * (1.0 + z * (1.0 - s))


_G_C = 0.7978845608028654
_G_K = 0.044715


def _gelu_and_grad(v):
    t = jnp.tanh(_G_C * (v + _G_K * (v * v * v)))
    cdf = 0.5 * (1.0 + t)
    return v * cdf, cdf + 0.5 * v * (1.0 - t * t) * (_G_C * (1.0 + 3.0 * _G_K * v * v))


def _cast_rows(src_ref, dst_ref, rows, step=256):
    def one(i, carry):
        r = pl.ds(pl.multiple_of(i * step, step), step)
        dst_ref[r, :] = src_ref[r, :].astype(dst_ref.dtype)
        return carry
    lax.fori_loop(0, rows // step, one, 0)


def _left_lanes(rows):
    return lax.broadcasted_iota(jnp.int32, (rows, 128), 1) < HEAD


def _mesh_pos():
    return lax.axis_index("x"), lax.axis_index("y"), lax.axis_index("c")


def _peer(pos, k):
    x, y, c = pos
    return (1 - x if k & 4 else x, 1 - y if k & 2 else y, 1 - c if k & 1 else c)


def _flat(pos):
    return 4 * pos[0] + 2 * pos[1] + pos[2]


def _allgather_weights(w_in_t, w_kv, w_out):
    def body(win_ref, wkv_ref, wout_ref, wint_o, wkv_o, wout_o, send_sems, recv_sems):
        x, y, c = _mesh_pos()
        me, sib = (x, y, c), (x, y, 1 - c)
        chips = [(1 - x, y), (x, 1 - y), (1 - x, 1 - y)]

        def rows(p):
            return wint_o.at[pl.ds(pl.multiple_of(_flat(p) * W_IN_SHARD, 16), W_IN_SHARD), :]

        rows(me)[...] = win_ref[...].astype(BF16)

        def copy(k, block, to):
            return pltpu.make_async_remote_copy(
                src_ref=rows(block), dst_ref=rows(block), send_sem=send_sems.at[k], recv_sem=recv_sems.at[k],
                device_id=to, device_id_type=MESH)

        first = [copy(0, me, sib)] + [copy(1 + j, me, (*chip, c)) for j, chip in enumerate(chips)]
        for cp in first:
            cp.start()
        wkv_o[...] = wkv_ref[...].astype(BF16)
        wout_o[...] = wout_ref[...].astype(BF16)
        passed = []
        for j, chip in enumerate(chips):
            copy(1 + j, (*chip, c), me).wait_recv()
            fwd = copy(4 + j, (*chip, c), sib)
            fwd.start()
            passed.append(fwd)
        copy(0, sib, me).wait_recv()
        for j, chip in enumerate(chips):
            copy(4 + j, (*chip, 1 - c), me).wait_recv()
        for cp in first + passed:
            cp.wait_send()

    vmem = pl.BlockSpec(memory_space=pltpu.VMEM)
    return pl.pallas_call(
        body, name="allgather_weights",
        out_shape=(SDS((IN_COLS, D_MODEL), BF16), SDS(w_kv.shape, BF16), SDS(w_out.shape, BF16)),
        in_specs=[vmem, vmem, vmem], out_specs=(vmem, vmem, vmem),
        scratch_shapes=[pltpu.SemaphoreType.DMA((7,)), pltpu.SemaphoreType.DMA((7,))],
        compiler_params=_params(40),
    )(w_in_t, w_kv, w_out)


def _proj_fwd(x2, g_norm, wint):
    tm = 256

    def body(x_ref, g_ref, w_ref, o_ref):
        xv = x_ref[...]
        h = xv * _rstd(xv) * g_ref[...]
        o_ref[...] = _dot_nt(h, w_ref[...])

    return pl.pallas_call(
        body, name="proj_fwd", grid=(T_LOC // tm,),
        in_specs=[pl.BlockSpec((tm, D_MODEL), lambda i: (i, 0)), pl.BlockSpec((1, D_MODEL), lambda i: (0, 0)),
                  pl.BlockSpec((IN_COLS, D_MODEL), lambda i: (0, 0))],
        out_specs=pl.BlockSpec((tm, IN_COLS), lambda i: (i, 0)),
        out_shape=SDS((T_LOC, IN_COLS), F32),
        compiler_params=_params(48, ("arbitrary",)),
    )(x2, g_norm, wint)


def _memkv_fwd(mem2, g_mem, wkv):
    def body(m_ref, g_ref, w_ref, o_ref):
        mv = m_ref[...]
        o_ref[...] = _dot(mv * _rstd(mv) * g_ref[...], w_ref[...])

    return pl.pallas_call(
        body, name="memkv_fwd", out_shape=SDS((B_LOC * N_MEM, 2 * MEM_W), F32), compiler_params=_params(32),
    )(mem2, g_mem, wkv)


N_BIAS = 7


def _fill_bias_tables(sl_ref, tab):
    for cfg, (d, length) in enumerate(DILATIONS):
        nk = min(length, 2 * CHUNK)
        r = lax.broadcasted_iota(jnp.int32, (CHUNK, nk), 0)
        c = lax.broadcasted_iota(jnp.int32, (CHUNK, nk), 1)
        for var in range(3 if length > nk else 1):
            rel = jnp.abs(r - c + var * RADIUS)
            dist = rel.astype(F32) * float(d)
            for h in range(2):
                slope = sl_ref[0, 0:1, h * HEAD:h * HEAD + 1]
                tab[3 * cfg + var, h * CHUNK:(h + 1) * CHUNK, 0:nk] = jnp.where(rel <= RADIUS, -slope * dist, NEG)


def _attn_blocks(visit, unroll):
    def step(t, carry):
        for cfg, (d, length) in enumerate(DILATIONS):
            nblk = length // CHUNK
            if nblk == 1:
                visit(cfg, 0, t, t, length, t)
                continue
            rho, i = (0, t) if d == 1 else (t // nblk, t % nblk)
            ks = jnp.clip(i * CHUNK - RADIUS, 0, length - 2 * CHUNK)
            visit(cfg, (i * CHUNK - ks) // RADIUS, rho + d * (i * CHUNK), rho + d * ks, 2 * CHUNK, t)
        return carry
    lax.fori_loop(0, 16, step, 0, unroll=unroll)


def _stack_heads(v, left):
    return jnp.concatenate([jnp.where(left, v, 0.0), jnp.where(left, 0.0, v)], axis=0)


def _unstack_heads(v, left):
    return jnp.where(left, v[0:CHUNK], v[CHUNK:2 * CHUNK])


def _rows(start, n, d):
    return pl.ds(start, n) if d == 1 else pl.ds(start, n, stride=d)


def _attn_fwd(proj, slopes):
    def body(sl_ref, q_ref, k_ref, v_ref, a_ref, lse_ref, *scr):
        o_c, m_c, l_c, tab = scr[0:3], scr[3:6], scr[6:9], scr[9]
        left = _left_lanes(CHUNK)
        _fill_bias_tables(sl_ref, tab)

        def block(cfg, var, q0, k0, nk, t):
            d = DILATIONS[cfg][0]
            rq, rk = _rows(q0, CHUNK, d), _rows(k0, nk, d)
            qs = _stack_heads(q_ref[rq, :] * SCALE, left)
            s = _dot_nt(qs, k_ref[rk, :]) + tab[3 * cfg + var, :, 0:nk]
            m = jnp.max(s, axis=-1, keepdims=True)
            p = jnp.exp(s - m)
            o_c[cfg][rq, :] = _unstack_heads(_dot(p, v_ref[rk, :]), left)
            m_c[cfg][rq, :] = _unstack_heads(m, left)
            l_c[cfg][rq, :] = _unstack_heads(jnp.sum(p, axis=-1, keepdims=True), left)
        _attn_blocks(block, 8)

        def merge(j, carry):
            rows = pl.ds(pl.multiple_of(j * 256, 256), 256)
            ms = [m_c[i][rows, :] for i in range(3)]
            top = jnp.maximum(jnp.maximum(ms[0], ms[1]), ms[2])
            ws = [jnp.exp(m - top) for m in ms]
            den = l_c[0][rows, :] * ws[0] + l_c[1][rows, :] * ws[1] + l_c[2][rows, :] * ws[2]
            num = o_c[0][rows, :] * ws[0] + o_c[1][rows, :] * ws[1] + o_c[2][rows, :] * ws[2]
            a_ref[rows, :] = num / den
            lse_ref[rows, :] = top + jnp.log(den)
            return carry
        lax.fori_loop(0, SEQ // 256, merge, 0)

    blk = lambda col0: pl.BlockSpec((SEQ, 128), lambda b, hp: (b, col0 // 128 + hp))
    out = pl.BlockSpec((SEQ, 128), lambda b, hp: (b, hp))
    return pl.pallas_call(
        body, name="attn_fwd", grid=(B_LOC, 4),
        in_specs=[pl.BlockSpec((1, 8, 128), lambda b, hp: (hp, 0, 0)), blk(C_QA), blk(C_KA), blk(C_VA)],
        out_specs=(out, out),
        out_shape=(SDS((T_LOC, ATTN_W), F32), SDS((T_LOC, ATTN_W), F32)),
        scratch_shapes=[pltpu.VMEM((SEQ, 128), F32)] * 9 + [pltpu.VMEM((N_BIAS, 2 * CHUNK, 2 * CHUNK), F32)],
        compiler_params=_params(40, ("arbitrary", "arbitrary")),
    )(slopes, proj, proj, proj)


def _chunks_side_by_side(v, pr, tm):
    return jnp.concatenate([v[ch * CHUNK:(ch + 1) * CHUNK, pr * 128:(pr + 1) * 128] for ch in range(tm // CHUNK)], axis=1)


def _first_group_lanes(tm):
    return lax.broadcasted_iota(jnp.int32, (CHUNK, tm), 1) % 128 < HEAD


def _store_chunks(dst_ref, pr, val, tm):
    for ch in range(tm // CHUNK):
        dst_ref[ch * CHUNK:(ch + 1) * CHUNK, pr * 128:(pr + 1) * 128] = val[:, ch * CHUNK:(ch + 1) * CHUNK]


def _sgu_mix(vn, ws_ref, dst_ref, tm):
    first = _first_group_lanes(tm)
    for pr in range(2):
        vp = _chunks_side_by_side(vn, pr, tm)
        _store_chunks(dst_ref, pr, jnp.where(first, _dot(ws_ref[2 * pr], vp), _dot(ws_ref[2 * pr + 1], vp)), tm)


def _mem_head_of_lane(rows):
    return lax.broadcasted_iota(jnp.int32, (rows, MEM_W), 1) // HEAD


def _stack_mem_heads(v, rows):
    head = _mem_head_of_lane(rows)
    return jnp.concatenate([jnp.where(head == h, v, 0.0) for h in range(4)], axis=0)


def _unstack_mem_heads(v, rows):
    head = _mem_head_of_lane(rows)
    out = v[0:rows]
    for h in range(1, 4):
        out = jnp.where(head == h, v[h * rows:(h + 1) * rows], out)
    return out


def _mem_attn_probs(q, kmem, rows):
    qs = _stack_mem_heads(q, rows).astype(BF16)
    s = _dot_nt(qs, kmem) * SCALE
    e = jnp.exp(s - jnp.max(s, axis=-1, keepdims=True))
    return e * (1.0 / jnp.sum(e, axis=-1, keepdims=True)), qs


def _branch_blocks(tm):
    col = lambda w, c0: pl.BlockSpec((tm, w), lambda i: (i, c0 // w))
    return [col(512, C_ZA), col(256, C_UB), col(256, C_VB), col(256, C_ZB), col(256, C_QM), col(256, C_ZM)]


def _branch_fwd(proj, a, kv, w_s, b_exp, g_v):
    tm = 256
    per_ex = SEQ // tm

    def body(za_ref, ub_ref, vb_ref, zb_ref, qm_ref, zm_ref, a_ref, kv_ref, ws_ref, be_ref, gv_ref, o_ref, mix):
        o_ref[:, 0:ATTN_W] = (_silu_and_grad(za_ref[...])[0] * a_ref[...]).astype(BF16)
        gu = _gelu_and_grad(ub_ref[...])[0]
        gv = _gelu_and_grad(vb_ref[...])[0]
        vn = gv * _rstd(gv) * gv_ref[...]
        _sgu_mix(vn.astype(BF16), ws_ref, mix, tm)
        sg = gu * (mix[...] + be_ref[...])
        o_ref[:, ATTN_W:ATTN_W + SGU_W] = (_silu_and_grad(zb_ref[...])[0] * sg).astype(BF16)
        p = _mem_attn_probs(qm_ref[...], kv_ref[:, 0:MEM_W], tm)[0]
        mo = _unstack_mem_heads(_dot(p, kv_ref[:, MEM_W:2 * MEM_W]), tm)
        o_ref[:, ATTN_W + SGU_W:D_MODEL] = (_silu_and_grad(zm_ref[...])[0] * mo).astype(BF16)

    full = lambda shape: pl.BlockSpec(shape, lambda i: (0,) * len(shape))
    return pl.pallas_call(
        body, name="branch_fwd", grid=(T_LOC // tm,),
        in_specs=_branch_blocks(tm) + [
            pl.BlockSpec((tm, ATTN_W), lambda i: (i, 0)), pl.BlockSpec((N_MEM, 2 * MEM_W), lambda i: (i // per_ex, 0)),
            full((4, CHUNK, CHUNK)), full((tm, SGU_W)), full((1, SGU_W))],
        out_specs=pl.BlockSpec((tm, D_MODEL), lambda i: (i, 0)),
        out_shape=SDS((T_LOC, D_MODEL), BF16),
        scratch_shapes=[pltpu.VMEM((tm, SGU_W), F32)],
        compiler_params=_params(VMEM_NO_STAGING_MIB, ("arbitrary",)),
    )(proj, proj, proj, proj, proj, proj, a, kv, w_s, b_exp, g_v)


def _outproj_loss(gated, wout, x2, tgt2, g_final):
    tm = 512

    def body(g_ref, w_ref, x_ref, t_ref, gf_ref, dh2_ref, loss_ref, dgf_ref):
        @pl.when(pl.program_id(0) == 0)
        def _():
            loss_ref[...] = jnp.zeros_like(loss_ref)
            dgf_ref[...] = jnp.zeros_like(dgf_ref)
        h2 = x_ref[...] + _dot(g_ref[...], w_ref[...])
        r = _rstd(h2)
        gf = gf_ref[...]
        err = h2 * r * gf - t_ref[...]
        loss_ref[...] += 0.5 * jnp.sum(jnp.mean(err * err, axis=-1, keepdims=True))
        dy = err * (1.0 / D_MODEL)
        dh2_ref[...] = _rms_bwd(h2, r, gf, dy)
        dgf_ref[...] += jnp.sum(dy * (h2 * r), axis=0, keepdims=True)

    row = pl.BlockSpec((tm, D_MODEL), lambda i: (i, 0))
    vec = pl.BlockSpec((1, D_MODEL), lambda i: (0, 0))
    return pl.pallas_call(
        body, name="outproj_loss", grid=(T_LOC // tm,),
        in_specs=[row, pl.BlockSpec((D_MODEL, D_MODEL), lambda i: (0, 0)), row, row, vec],
        out_specs=(row, pl.BlockSpec((8, 128), lambda i: (0, 0)), vec),
        out_shape=(SDS((T_LOC, D_MODEL), F32), SDS((8, 128), F32), SDS((1, D_MODEL), F32)),
        compiler_params=_params(VMEM_NO_STAGING_MIB, ("arbitrary",)),
    )(gated, wout, x2, tgt2, g_final)


def _branch_bwd(dh2, wout, gated, proj, a, kv, w_s, b_exp, g_v):
    tm = 256
    per_ex = SEQ // tm

    def body(dh2_ref, w_ref, g_ref, za_ref, ub_ref, vb_ref, zb_ref, qm_ref, zm_ref, a_ref, kv_ref, ws_ref,
             be_ref, gv_ref, da_ref, dr_ref, dkv_ref, dwo_ref, dws_ref, db_ref, dgv_ref, mix, dvn, dmsum, dwo_acc):
        i = pl.program_id(0)

        @pl.when(i == 0)
        def _():
            dwo_acc[...] = jnp.zeros_like(dwo_acc)
            dws_ref[...] = jnp.zeros_like(dws_ref)
            dgv_ref[...] = jnp.zeros_like(dgv_ref)
            dmsum[...] = jnp.zeros_like(dmsum)

        @pl.when(i % per_ex == 0)
        def _():
            dkv_ref[...] = jnp.zeros_like(dkv_ref)

        dh2 = dh2_ref[...].astype(BF16)
        dwo_acc[...] += _dot_tn(g_ref[...], dh2)
        dg = _dot_nt(dh2, w_ref[...])

        sa, dsa = _silu_and_grad(za_ref[...])
        dga = dg[:, 0:ATTN_W]
        da_ref[...] = dga * sa
        dr_ref[:, 0:512] = (dga * a_ref[...] * dsa).astype(BF16)

        ub, vb = ub_ref[...], vb_ref[...]
        gu, dgu = _gelu_and_grad(ub)
        gv, dgv = _gelu_and_grad(vb)
        rv = _rstd(gv)
        gain = gv_ref[...]
        vn = (gv * rv * gain).astype(BF16)
        _sgu_mix(vn, ws_ref, mix, tm)
        mixed = mix[...] + be_ref[...]
        sb, dsb = _silu_and_grad(zb_ref[...])
        dgb = dg[:, ATTN_W:ATTN_W + SGU_W]
        dsg = dgb * sb
        dr_ref[:, 512:768] = (dsg * mixed * dgu).astype(BF16)
        dr_ref[:, 1024:1280] = (dgb * (gu * mixed) * dsb).astype(BF16)
        dmix = dsg * gu
        for ch in range(tm // CHUNK):
            dmsum[...] += dmix[ch * CHUNK:(ch + 1) * CHUNK, :]
        first = _first_group_lanes(tm)
        for pr in range(2):
            dmp, vp = _chunks_side_by_side(dmix, pr, tm), _chunks_side_by_side(vn, pr, tm)
            dws_ref[2 * pr] += _dot_nt(jnp.where(first, dmp, 0.0), vp)
            dws_ref[2 * pr + 1] += _dot_nt(jnp.where(first, 0.0, dmp), vp)
            _store_chunks(dvn, pr, jnp.where(first, _dot_tn(ws_ref[2 * pr], dmp), _dot_tn(ws_ref[2 * pr + 1], dmp)), tm)
        dvn_v = dvn[...]
        dgv_ref[...] += jnp.sum(dvn_v * (gv * rv), axis=0, keepdims=True)
        dr_ref[:, 768:1024] = (_rms_bwd(gv, rv, gain, dvn_v) * dgv).astype(BF16)

        szm, dszm = _silu_and_grad(zm_ref[...])
        dgm = dg[:, ATTN_W + SGU_W:D_MODEL]
        kmem, vmem_ = kv_ref[:, 0:MEM_W].astype(BF16), kv_ref[:, MEM_W:2 * MEM_W].astype(BF16)
        p, qs = _mem_attn_probs(qm_ref[...], kmem, tm)
        dmos = _stack_mem_heads(dgm * szm, tm).astype(BF16)
        dr_ref[:, 1536:1792] = (dgm * _unstack_mem_heads(_dot(p, vmem_), tm) * dszm).astype(BF16)
        dp = _dot_nt(dmos, vmem_)
        ds = (p * (dp - jnp.sum(p * dp, axis=-1, keepdims=True)) * SCALE).astype(BF16)
        dr_ref[:, 1280:1536] = _unstack_mem_heads(_dot(ds, kmem), tm).astype(BF16)
        dkv_ref[:, 0:MEM_W] += _dot_tn(ds, qs)
        dkv_ref[:, MEM_W:2 * MEM_W] += _dot_tn(p, dmos)

        @pl.when(i == pl.num_programs(0) - 1)
        def _():
            tot = dmsum[...]
            hi = tot.astype(BF16)
            lo = (tot - hi.astype(F32)).astype(BF16)
            grp = (lax.broadcasted_iota(jnp.int32, (SGU_W, 128), 0) // HEAD
                   == lax.broadcasted_iota(jnp.int32, (SGU_W, 128), 1)).astype(BF16)
            db_ref[...] = (_dot(hi, grp) + _dot(lo, grp)).T[0:4, :]
            _cast_rows(dwo_acc, dwo_ref, D_MODEL)

    full = lambda shape: pl.BlockSpec(shape, lambda i: (0,) * len(shape))
    row = lambda w: pl.BlockSpec((tm, w), lambda i: (i, 0))
    return pl.pallas_call(
        body, name="branch_bwd", grid=(T_LOC // tm,),
        in_specs=[row(D_MODEL), full((D_MODEL, D_MODEL)), row(D_MODEL)] + _branch_blocks(tm) + [
            row(ATTN_W), pl.BlockSpec((N_MEM, 2 * MEM_W), lambda i: (i // per_ex, 0)),
            full((4, CHUNK, CHUNK)), full((tm, SGU_W)), full((1, SGU_W))],
        out_specs=(row(ATTN_W), row(REST_W), pl.BlockSpec((N_MEM, 2 * MEM_W), lambda i: (i // per_ex, 0)),
                   full((D_MODEL, D_MODEL)), full((4, CHUNK, CHUNK)), full((4, CHUNK)), full((1, SGU_W))),
        out_shape=(SDS((T_LOC, ATTN_W), F32), SDS((T_LOC, REST_W), BF16), SDS((B_LOC * N_MEM, 2 * MEM_W), F32),
                   SDS((D_MODEL, D_MODEL), BF16), SDS((4, CHUNK, CHUNK), F32), SDS((4, CHUNK), F32), SDS((1, SGU_W), F32)),
        scratch_shapes=[pltpu.VMEM((tm, SGU_W), F32), pltpu.VMEM((tm, SGU_W), F32), pltpu.VMEM((CHUNK, SGU_W), F32),
                        pltpu.VMEM((D_MODEL, D_MODEL), F32)],
        compiler_params=_params(56, ("arbitrary",)),
    )(dh2, wout, gated, proj, proj, proj, proj, proj, proj, a, kv, w_s, b_exp, g_v)


def _attn_bwd(proj, slopes, da, a, lse):
    def body(sl_ref, q_ref, k_ref, v_ref, da_ref, a_ref, lse_ref, dq_ref, dk_ref, dv_ref, *scr):
        dq_s, dk_s, dv_s, tab = scr[0:3], scr[3:6], scr[6:9], scr[9]
        lse_h, delta_h = scr[10:12], scr[12:14]
        p_all, ds_all = scr[14], scr[15]
        left = _left_lanes(CHUNK)
        _fill_bias_tables(sl_ref, tab)

        def prep(j, carry):
            rows = pl.ds(pl.multiple_of(j * 256, 256), 256)
            l256 = _left_lanes(256)
            prod = da_ref[rows, :] * a_ref[rows, :]
            delta_h[0][rows, :] = jnp.broadcast_to(jnp.sum(jnp.where(l256, prod, 0.0), axis=-1, keepdims=True), (256, 128))
            delta_h[1][rows, :] = jnp.broadcast_to(jnp.sum(jnp.where(l256, 0.0, prod), axis=-1, keepdims=True), (256, 128))
            pair = lse_ref[rows, :]
            other = pltpu.roll(pair, HEAD, axis=1)
            lse_h[0][rows, :] = jnp.where(l256, pair, other)
            lse_h[1][rows, :] = jnp.where(l256, other, pair)
            zero = jnp.zeros((256, 128), F32)
            for cfg in range(3):
                dk_s[cfg][rows, :] = zero
                dv_s[cfg][rows, :] = zero
            return carry
        lax.fori_loop(0, SEQ // 256, prep, 0)

        def per_row(halves, rq, nk):
            v = jnp.concatenate([halves[0][rq, :], halves[1][rq, :]], axis=0)
            return v if nk == 128 else jnp.concatenate([v, v], axis=1)

        def probs(cfg, var, q0, k0, nk, t):
            d = DILATIONS[cfg][0]
            rq, rk = _rows(q0, CHUNK, d), _rows(k0, nk, d)
            qs = _stack_heads(q_ref[rq, :] * SCALE, left)
            das = _stack_heads(da_ref[rq, :], left)
            s = _dot_nt(qs, k_ref[rk, :]) + tab[3 * cfg + var, :, 0:nk]
            p = jnp.exp(s - per_row(lse_h, rq, nk))
            p_all[16 * cfg + t, :, 0:nk] = p.astype(BF16)
            ds_all[16 * cfg + t, :, 0:nk] = (p * (_dot_nt(das, v_ref[rk, :]) - per_row(delta_h, rq, nk))).astype(BF16)
        _attn_blocks(probs, 4)

        def grads(cfg, var, q0, k0, nk, t):
            d = DILATIONS[cfg][0]
            rq, rk = _rows(q0, CHUNK, d), _rows(k0, nk, d)
            qs = _stack_heads(q_ref[rq, :] * SCALE, left).astype(BF16)
            das = _stack_heads(da_ref[rq, :], left).astype(BF16)
            p, ds = p_all[16 * cfg + t, :, 0:nk], ds_all[16 * cfg + t, :, 0:nk]
            dq_s[cfg][rq, :] = _unstack_heads(_dot(ds, k_ref[rk, :]), left) * SCALE
            dk_s[cfg][rk, :] += _dot_tn(ds, qs)
            dv_s[cfg][rk, :] += _dot_tn(p, das)
        _attn_blocks(grads, 4)

        def flush(j, carry):
            rows = pl.ds(pl.multiple_of(j * 256, 256), 256)
            for acc, dst in ((dq_s, dq_ref), (dk_s, dk_ref), (dv_s, dv_ref)):
                dst[rows, :] = (acc[0][rows, :] + acc[1][rows, :] + acc[2][rows, :]).astype(BF16)
            return carry
        lax.fori_loop(0, SEQ // 256, flush, 0)

    blk = lambda col0: pl.BlockSpec((SEQ, 128), lambda b, hp: (b, col0 // 128 + hp))
    own = pl.BlockSpec((SEQ, 128), lambda b, hp: (b, hp))
    return pl.pallas_call(
        body, name="attn_bwd", grid=(B_LOC, 4),
        in_specs=[pl.BlockSpec((1, 8, 128), lambda b, hp: (hp, 0, 0)), blk(C_QA), blk(C_KA), blk(C_VA), own, own, own],
        out_specs=(own, own, own),
        out_shape=(SDS((T_LOC, ATTN_W), BF16),) * 3,
        scratch_shapes=[pltpu.VMEM((SEQ, 128), F32)] * 9 + [pltpu.VMEM((N_BIAS, 2 * CHUNK, 2 * CHUNK), F32)]
        + [pltpu.VMEM((SEQ, 128), F32)] * 4 + [pltpu.VMEM((48, 2 * CHUNK, 2 * CHUNK), BF16)] * 2,
        compiler_params=_params(52, ("arbitrary", "arbitrary")),
    )(slopes, proj, proj, proj, da, a, lse)


def _dproj_specs(tm):
    third = pl.BlockSpec((tm, ATTN_W), lambda i: (i, 0))
    return [third, third, third, pl.BlockSpec((tm, REST_W), lambda i: (i, 0))]


def _dx(dq, dk, dv, dr, wint, x2, dh2, g_norm):
    tm = 256

    def body(dq_ref, dk_ref, dv_ref, dr_ref, w_ref, x_ref, dh2_ref, g_ref, gx_ref, dgn_ref):
        @pl.when(pl.program_id(0) == 0)
        def _():
            dgn_ref[...] = jnp.zeros_like(dgn_ref)
        dh = (_dot(dq_ref[...], w_ref[C_QA:C_KA, :]) + _dot(dk_ref[...], w_ref[C_KA:C_VA, :])
              + _dot(dv_ref[...], w_ref[C_VA:C_ZA, :]) + _dot(dr_ref[...], w_ref[C_ZA:IN_COLS, :]))
        xv = x_ref[...]
        r = _rstd(xv)
        gx_ref[...] = dh2_ref[...] + _rms_bwd(xv, r, g_ref[...], dh)
        dgn_ref[...] += jnp.sum(dh * (xv * r), axis=0, keepdims=True)

    row = pl.BlockSpec((tm, D_MODEL), lambda i: (i, 0))
    vec = pl.BlockSpec((1, D_MODEL), lambda i: (0, 0))
    return pl.pallas_call(
        body, name="dx", grid=(T_LOC // tm,),
        in_specs=_dproj_specs(tm) + [pl.BlockSpec((IN_COLS, D_MODEL), lambda i: (0, 0)), row, row, vec],
        out_specs=(row, vec),
        out_shape=(SDS((T_LOC, D_MODEL), F32), SDS((1, D_MODEL), F32)),
        compiler_params=_params(48, ("arbitrary",)),
    )(dq, dk, dv, dr, wint, x2, dh2, g_norm)


def _dwin(dq, dk, dv, dr, x2, g_norm, half):
    tm = 512
    width = D_MODEL // 2
    cols = slice(half * width, (half + 1) * width)

    def body(dq_ref, dk_ref, dv_ref, dr_ref, x_ref, g_ref, o_ref, acc):
        @pl.when(pl.program_id(0) == 0)
        def _():
            acc[...] = jnp.zeros_like(acc)
        xv = x_ref[...]
        h = (xv[:, cols] * _rstd(xv) * g_ref[:, cols]).astype(BF16)
        acc[C_QA:C_KA, :] += _dot_tn(dq_ref[...], h)
        acc[C_KA:C_VA, :] += _dot_tn(dk_ref[...], h)
        acc[C_VA:C_ZA, :] += _dot_tn(dv_ref[...], h)
        acc[C_ZA:IN_COLS, :] += _dot_tn(dr_ref[...], h)

        @pl.when(pl.program_id(0) == pl.num_programs(0) - 1)
        def _():
            _cast_rows(acc, o_ref, IN_COLS)

    return pl.pallas_call(
        body, name="dwin%d" % half, grid=(T_LOC // tm,),
        in_specs=_dproj_specs(tm) + [pl.BlockSpec((tm, D_MODEL), lambda i: (i, 0)), pl.BlockSpec((1, D_MODEL), lambda i: (0, 0))],
        out_specs=pl.BlockSpec((IN_COLS, width), lambda i: (0, 0)),
        out_shape=SDS((IN_COLS, width), BF16),
        scratch_shapes=[pltpu.VMEM((IN_COLS, width), F32)],
        compiler_params=_params(48, ("arbitrary",)),
    )(dq, dk, dv, dr, x2, g_norm)


def _memkv_bwd(dkv, mem2, g_mem, wkv):
    def body(dkv_ref, m_ref, g_ref, w_ref, dw_ref, dg_ref):
        mv = m_ref[...]
        r = _rstd(mv)
        dkv_v = dkv_ref[...].astype(BF16)
        dw_ref[...] = _dot_tn(mv * r * g_ref[...], dkv_v).astype(BF16)
        dg_ref[...] = jnp.sum(_dot_nt(dkv_v, w_ref[...]) * (mv * r), axis=0, keepdims=True)

    return pl.pallas_call(
        body, name="memkv_bwd", out_shape=(SDS((D_MODEL, 2 * MEM_W), BF16), SDS((1, D_MODEL), F32)),
        compiler_params=_params(32),
    )(dkv, mem2, g_mem, wkv)


def _allreduce_small(parts):
    n = len(parts)

    def body(*refs):
        ins, outs, bufs = refs[0:n], refs[n:2 * n], refs[2 * n:3 * n]
        send_sems, recv_sems = refs[3 * n], refs[3 * n + 1]
        pos = _mesh_pos()
        me = _flat(pos)
        for a in range(n):
            bufs[a][me] = ins[a][...]

        def copy(a, k, slot):
            return pltpu.make_async_remote_copy(
                src_ref=ins[a], dst_ref=bufs[a].at[slot],
                send_sem=send_sems.at[7 * a + k - 1], recv_sem=recv_sems.at[7 * a + k - 1],
                device_id=_peer(pos, k), device_id_type=MESH)

        sent = [copy(a, k, me) for a in range(n) for k in range(1, N_DEV)]
        for cp in sent:
            cp.start()
        for a in range(n):
            for k in range(1, N_DEV):
                copy(a, k, _flat(_peer(pos, k))).wait_recv()
        for cp in sent:
            cp.wait_send()
        for a in range(n):
            acc = bufs[a][0]
            for s in range(1, N_DEV):
                acc = acc + bufs[a][s]
            outs[a][...] = acc

    vmem = pl.BlockSpec(memory_space=pltpu.VMEM)
    return pl.pallas_call(
        body, name="allreduce_small",
        out_shape=tuple(SDS(p.shape, F32) for p in parts),
        in_specs=[vmem] * n, out_specs=(vmem,) * n,
        scratch_shapes=[pltpu.VMEM((N_DEV,) + p.shape, F32) for p in parts]
        + [pltpu.SemaphoreType.DMA((7 * n,)), pltpu.SemaphoreType.DMA((7 * n,))],
        compiler_params=_params(16),
    )(*parts)


_HBM = pl.BlockSpec(memory_space=pltpu.HBM)
_SEM = pl.BlockSpec(memory_space=pltpu.SEMAPHORE)
_SIDE_EFFECT = pltpu.SideEffectType.DATAFLOW_SIDE_EFFECTING


def _exchange_copies(src_refs, land_refs, scatter, send_sems, recv_sems):
    pos = _mesh_pos()
    copies = []
    for a, (src, land) in enumerate(zip(src_refs, land_refs)):
        n = land.shape[1]
        for k in range(1, N_DEV):
            peer = _peer(pos, k)
            piece = src.at[pl.ds(pl.multiple_of(_flat(peer) * n, 16), n), :] if scatter[a] else src
            copies.append(pltpu.make_async_remote_copy(
                src_ref=piece, dst_ref=land.at[_flat(pos)],
                send_sem=send_sems.at[7 * a + k - 1], recv_sem=recv_sems.at[7 * a + k - 1],
                device_id=peer, device_id_type=MESH))
    return copies


def _exchange_start(name, srcs, scatter, lands):
    n = len(srcs)

    def body(*refs):
        for cp in _exchange_copies(refs[0:n], refs[n:2 * n], scatter, refs[2 * n], refs[2 * n + 1]):
            cp.start()
        refs[-1][...] = jnp.zeros_like(refs[-1])

    ops = [pltpu.with_memory_space_constraint(t, pltpu.HBM) for t in (*srcs, *lands)]
    out = pl.pallas_call(
        body, name=name,
        out_shape=(pltpu.SemaphoreType.DMA((7 * n,)), pltpu.SemaphoreType.DMA((7 * n,)),
                   *[pltpu.HBM(t.shape, t.dtype) for t in ops], SDS((8, 128), F32)),
        in_specs=[_HBM] * (2 * n),
        out_specs=(_SEM, _SEM, *[_HBM] * (2 * n), pl.BlockSpec(memory_space=pltpu.VMEM)),
        input_output_aliases={i: 2 + i for i in range(2 * n)},
        compiler_params=pltpu.CompilerParams(has_side_effects=_SIDE_EFFECT),
    )(*ops)
    return out[0], out[1], out[2:2 + n], out[2 + n:2 + 2 * n], out[-1]


def _exchange_wait(name, started, scatter, after):
    send_sems, recv_sems, srcs, lands, _ = started
    n = len(srcs)

    def body(*refs):
        for cp in _exchange_copies(refs[0:n], refs[n:2 * n], scatter, refs[2 * n], refs[2 * n + 1]):
            cp.wait_send()
            cp.wait_recv()

    out = pl.pallas_call(
        body, name=name,
        out_shape=tuple(pltpu.HBM(t.shape, t.dtype) for t in (*srcs, *lands)),
        in_specs=[_HBM] * (2 * n) + [_SEM, _SEM, pl.BlockSpec(memory_space=pl.ANY)],
        out_specs=(_HBM,) * (2 * n),
        input_output_aliases={i: i for i in range(2 * n)},
        compiler_params=pltpu.CompilerParams(has_side_effects=_SIDE_EFFECT),
    )(*srcs, *lands, send_sems, recv_sems, after)
    return out[n:]


def _landing(own, me):
    return lax.dynamic_update_slice(lax.empty((N_DEV,) + own.shape, own.dtype), own[None], (me,) + (0,) * own.ndim)


def _adamw(w, g, m, v):
    m = ADAM_B1 * m + (1.0 - ADAM_B1) * g
    v = ADAM_B2 * v + (1.0 - ADAM_B2) * (g * g)
    m_hat = m / (1.0 - ADAM_B1 ** ADAM_STEP)
    v_hat = v / (1.0 - ADAM_B2 ** ADAM_STEP)
    return -ADAM_LR * (m_hat / (jnp.sqrt(v_hat) + ADAM_EPS) + ADAM_WD * w), m, v


def _adam_slots(name, pieces, w, m, v):
    rows, cols = w.shape
    starts = [sum(p.shape[2] for p in pieces[:i]) for i in range(len(pieces) + 1)]
    assert starts[-1] == cols and all(p.shape[1] == rows for p in pieces)

    def body(*refs):
        s_refs, (w_ref, m_ref, v_ref, g_o, d_o, m_o, v_o, acc) = refs[:len(pieces)], refs[len(pieces):]
        s = pl.program_id(0)

        @pl.when(s == 0)
        def _():
            for i, s_ref in enumerate(s_refs):
                acc[:, starts[i]:starts[i + 1]] = s_ref[0].astype(F32)

        @pl.when(s > 0)
        def _():
            for i, s_ref in enumerate(s_refs):
                acc[:, starts[i]:starts[i + 1]] += s_ref[0].astype(F32)

        @pl.when(s == N_DEV - 1)
        def _():
            g = acc[...]
            g_o[...] = g
            d_o[...], m_o[...], v_o[...] = _adamw(w_ref[...], g, m_ref[...], v_ref[...])

    full = pl.BlockSpec((rows, cols), lambda s: (0, 0))
    return pl.pallas_call(
        body, name="adam_" + name, grid=(N_DEV,),
        in_specs=[pl.BlockSpec((1, rows, p.shape[2]), lambda s: (s, 0, 0)) for p in pieces] + [full, full, full],
        out_specs=(full,) * 4, out_shape=(SDS((rows, cols), F32),) * 4,
        scratch_shapes=[pltpu.VMEM((rows, cols), F32)],
        compiler_params=_params(40, ("arbitrary",)),
    )(*pieces, w, m, v)


def _adam_small(ws, gs, ms, vs, loss_slots):
    n = len(ws)

    def total(ref, like):
        if len(ref.shape) == len(like.shape):
            return ref[...]
        acc = ref[0]
        for s in range(1, N_DEV):
            acc = acc + ref[s]
        return acc

    def body(*refs):
        w_r, g_r, m_r, v_r = refs[0:n], refs[n:2 * n], refs[2 * n:3 * n], refs[3 * n:4 * n]
        loss_r, outs = refs[4 * n], refs[4 * n + 1:]
        for a in range(n):
            g = total(g_r[a], w_r[a])
            outs[a][...] = g
            outs[n + 1 + 3 * a][...], outs[n + 2 + 3 * a][...], outs[n + 3 + 3 * a][...] = _adamw(
                w_r[a][...], g, m_r[a][...], v_r[a][...])
        outs[n][...] = total(loss_r, outs[n])

    out = pl.pallas_call(
        body, name="adam_small",
        out_shape=tuple(SDS(w.shape, F32) for w in ws) + (SDS(loss_slots.shape[1:], F32),)
        + tuple(SDS(w.shape, F32) for w in ws for _ in range(3)),
        compiler_params=_params(16),
    )(*ws, *gs, *ms, *vs, loss_slots)
    return out[0:n], out[n], out[n + 1:]


def kernel(x, mem, g_norm, w_in, w_sgu_spatial, b_sgu_spatial, g_sgu_v, g_mem, w_mem_kv, w_out, g_final, loss_target, m_g_norm, m_w_in, m_w_sgu_spatial, m_b_sgu_spatial, m_g_sgu_v, m_g_mem, m_w_mem_kv, m_w_out, m_g_final, v_g_norm, v_w_in, v_w_sgu_spatial, v_b_sgu_spatial, v_g_sgu_v, v_g_mem, v_w_mem_kv, v_w_out, v_g_final):
    x2 = x.reshape(T_LOC, D_MODEL)
    tgt2 = loss_target.reshape(T_LOC, D_MODEL)
    mem2 = mem.reshape(B_LOC * N_MEM, D_MODEL)
    w_s = w_sgu_spatial[0]
    b_exp = jnp.tile(jnp.repeat(b_sgu_spatial[0].T, HEAD, axis=1), (2, 1))
    slope = jnp.power(2.0, -8.0 * (jnp.arange(8, dtype=F32) + 1.0) / 8)
    slopes = jnp.broadcast_to(jnp.repeat(slope.reshape(4, 2), HEAD, axis=1)[:, None, :], (4, 8, 128))

    tr = lambda t: jnp.transpose(t[0])

    me = _flat(_mesh_pos())
    own_rows = lambda t: lax.dynamic_slice_in_dim(t, me * (t.shape[0] // N_DEV), t.shape[0] // N_DEV)

    wint, wkv_own, wout_own = _allgather_weights(tr(w_in), w_mem_kv[0], w_out[0])
    started0 = _exchange_start("exchange0_start", [wkv_own, wout_own], [False, False],
                               [_landing(wkv_own, me), _landing(wout_own, me)])
    proj = _proj_fwd(x2, g_norm + started0[4][0:1, 0:1], wint)
    wkv, wout = _exchange_wait("exchange0_wait", started0, [False, False], proj)
    wkv, wout = wkv.reshape(D_MODEL, 2 * MEM_W), wout.reshape(D_MODEL, D_MODEL)
    kv = _memkv_fwd(mem2, g_mem, wkv)
    a, lse = _attn_fwd(proj, slopes)
    gated = _branch_fwd(proj, a, kv, w_s, b_exp, g_sgu_v)
    dh2, loss8, dgf = _outproj_loss(gated, wout, x2, tgt2, g_final.reshape(1, D_MODEL))

    da, dr, dkv, dwout, dws, dbs, dgv = _branch_bwd(dh2, wout, gated, proj, a, kv, w_s, b_exp, g_sgu_v)
    dwkv, dgm = _memkv_bwd(dkv, mem2, g_mem, wkv)

    early = [dws.reshape(4 * CHUNK, CHUNK), dbs, dgv, dgm, dgf, loss8]
    scatter1 = [True, True] + [False] * len(early)
    started1 = _exchange_start(
        "exchange1_start", [dwkv, dwout] + early, scatter1,
        [_landing(own_rows(dwkv), me), _landing(own_rows(dwout), me)] + [_landing(t, me) for t in early])
    dq, dk, dv = _attn_bwd(proj, slopes + started1[4][0:1, 0:1], da, a, lse)
    s_wkv, s_wout, s_ws, s_bs, s_gv, s_gm, s_gf, s_loss = _exchange_wait("exchange1_wait", started1, scatter1, dq)

    dwint0 = _dwin(dq, dk, dv, dr, x2, g_norm, 0)
    started2 = _exchange_start("exchange2_start", [dwint0], [True], [_landing(own_rows(dwint0), me)])
    dwint1 = _dwin(dq, dk, dv, dr, x2, g_norm + started2[4][0:1, 0:1], 1)
    started3 = _exchange_start("exchange3_start", [dwint1], [True], [_landing(own_rows(dwint1), me)])
    grad_x, dgn = _dx(dq, dk, dv, dr, wint, x2, dh2, g_norm + started3[4][0:1, 0:1])
    s_win0, = _exchange_wait("exchange2_wait", started2, [True], grad_x)
    dgn_sum, = _allreduce_small([dgn])
    s_win1, = _exchange_wait("exchange3_wait", started3, [True], dgn_sum)

    g_win, d_win, m_win, v_win = map(
        jnp.transpose, _adam_slots("w_in", [s_win0, s_win1], tr(w_in), tr(m_w_in), tr(v_w_in)))
    g_wkv, d_wkv, m_wkv, v_wkv = _adam_slots("w_mem_kv", [s_wkv], w_mem_kv[0], m_w_mem_kv[0], v_w_mem_kv[0])
    g_wout, d_wout, m_wout, v_wout = _adam_slots("w_out", [s_wout], w_out[0], m_w_out[0], v_w_out[0])

    small_shapes = [(1, D_MODEL), (4 * CHUNK, CHUNK), (4, CHUNK), (1, SGU_W), (1, D_MODEL), (1, D_MODEL)]
    pack = lambda arrs: [t.reshape(s) for t, s in zip(arrs, small_shapes)]
    g_small, loss_sum, upd = _adam_small(
        pack([g_norm, w_sgu_spatial, b_sgu_spatial, g_sgu_v, g_mem, g_final]),
        [dgn_sum, s_ws, s_bs, s_gv, s_gm, s_gf],
        pack([m_g_norm, m_w_sgu_spatial, m_b_sgu_spatial, m_g_sgu_v, m_g_mem, m_g_final]),
        pack([v_g_norm, v_w_sgu_spatial, v_b_sgu_spatial, v_g_sgu_v, v_g_mem, v_g_final]), s_loss)
    out_shapes = [g_norm.shape, w_sgu_spatial.shape, b_sgu_spatial.shape, g_sgu_v.shape, g_mem.shape, g_final.shape]
    unpack = lambda arrs: [t.reshape(s) for t, s in zip(arrs, out_shapes)]
    gs = unpack(g_small)
    ds, nms, nvs = unpack(upd[0::3]), unpack(upd[1::3]), unpack(upd[2::3])

    loss = loss_sum[0, 0]

    def assemble(small, win, wkv_, wout_):
        return [small[0], win[None], small[1], small[2], small[3], small[4], wkv_[None], wout_[None], small[5]]

    return (loss, grad_x.reshape(x.shape),
            *assemble(gs, g_win, g_wkv, g_wout), *assemble(ds, d_win, d_wkv, d_wout),
            *assemble(nms, m_win, m_wkv, m_wout), *assemble(nvs, v_win, v_wkv, v_wout))
```

```python
import jax
import jax.numpy as jnp
from jax import lax
from jax.experimental import pallas as pl
from jax.experimental.pallas import tpu as pltpu

F32 = jnp.float32
BF16 = jnp.bfloat16
SDS = jax.ShapeDtypeStruct
MESH = pl.DeviceIdType.MESH

N_DEV = 8
D_MODEL = 1024
SEQ = 2048
B_LOC = 2
T_LOC = B_LOC * SEQ
N_MEM = 256
HEAD = 64
ATTN_W = 512
SGU_W = 256
MEM_W = 256
IN_COLS = 3328
W_IN_SHARD = IN_COLS // N_DEV
CHUNK = 128
DILATIONS = ((1, 2048), (4, 512), (16, 128))
RADIUS = 64
EPS = 1e-6
NEG = -1e30
SCALE = HEAD ** -0.5
C_QA, C_KA, C_VA, C_ZA, C_UB, C_VB, C_ZB, C_QM, C_ZM = 0, 512, 1024, 1536, 2048, 2304, 2560, 2816, 3072
QKV_W = 1536
REST_W = IN_COLS - QKV_W

ADAM_LR, ADAM_B1, ADAM_B2, ADAM_EPS, ADAM_WD, ADAM_STEP = 0.001, 0.9, 0.999, 1e-08, 0.01, 10

V7X_VMEM_MIB = 64
VMEM_NO_STAGING_MIB = V7X_VMEM_MIB - 6


def _params(vmem_mib, sem=None):
    assert vmem_mib < V7X_VMEM_MIB
    return pltpu.CompilerParams(vmem_limit_bytes=vmem_mib << 20, dimension_semantics=sem)


def _dot(a, b):
    return jnp.dot(a.astype(BF16), b.astype(BF16), preferred_element_type=F32)


def _dot_nt(a, b):
    return lax.dot_general(a.astype(BF16), b.astype(BF16), (((1,), (1,)), ((), ())), preferred_element_type=F32)


def _dot_tn(a, b):
    return lax.dot_general(a.astype(BF16), b.astype(BF16), (((0,), (0,)), ((), ())), preferred_element_type=F32)


def _rstd(v):
    return lax.rsqrt(jnp.mean(v * v, axis=-1, keepdims=True) + EPS)


def _rms_bwd(v, r, g, dy):
    gdy = g * dy
    return r * gdy - v * (r * r * r * jnp.mean(gdy * v, axis=-1, keepdims=True))


def _sigmoid(z):
    return 1.0 / (1.0 + jnp.exp(-z))


def _silu_and_grad(z):
    s = _sigmoid(z)
    return z * s, s * (1.0 + z * (1.0 - s))


_G_C = 0.7978845608028654
_G_K = 0.044715


def _gelu_and_grad(v):
    t = jnp.tanh(_G_C * (v + _G_K * (v * v * v)))
    cdf = 0.5 * (1.0 + t)
    return v * cdf, cdf + 0.5 * v * (1.0 - t * t) * (_G_C * (1.0 + 3.0 * _G_K * v * v))


def _cast_rows(src_ref, dst_ref, rows, step=256):
    def one(i, carry):
        r = pl.ds(pl.multiple_of(i * step, step), step)
        dst_ref[r, :] = src_ref[r, :].astype(dst_ref.dtype)
        return carry
    lax.fori_loop(0, rows // step, one, 0)


def _left_lanes(rows):
    return lax.broadcasted_iota(jnp.int32, (rows, 128), 1) < HEAD


def _mesh_pos():
    return lax.axis_index("x"), lax.axis_index("y"), lax.axis_index("c")


def _peer(pos, k):
    x, y, c = pos
    return (1 - x if k & 4 else x, 1 - y if k & 2 else y, 1 - c if k & 1 else c)


def _flat(pos):
    return 4 * pos[0] + 2 * pos[1] + pos[2]


def _allgather_weights(w_in_t, w_kv, w_out):
    def body(win_ref, wkv_ref, wout_ref, wint_o, wkv_o, wout_o, send_sems, recv_sems):
        x, y, c = _mesh_pos()
        me, sib = (x, y, c), (x, y, 1 - c)
        chips = [(1 - x, y), (x, 1 - y), (1 - x, 1 - y)]

        def rows(p):
            return wint_o.at[pl.ds(pl.multiple_of(_flat(p) * W_IN_SHARD, 16), W_IN_SHARD), :]

        rows(me)[...] = win_ref[...].astype(BF16)

        def copy(k, block, to):
            return pltpu.make_async_remote_copy(
                src_ref=rows(block), dst_ref=rows(block), send_sem=send_sems.at[k], recv_sem=recv_sems.at[k],
                device_id=to, device_id_type=MESH)

        first = [copy(0, me, sib)] + [copy(1 + j, me, (*chip, c)) for j, chip in enumerate(chips)]
        for cp in first:
            cp.start()
        wkv_o[...] = wkv_ref[...].astype(BF16)
        wout_o[...] = wout_ref[...].astype(BF16)
        passed = []
        for j, chip in enumerate(chips):
            copy(1 + j, (*chip, c), me).wait_recv()
            fwd = copy(4 + j, (*chip, c), sib)
            fwd.start()
            passed.append(fwd)
        copy(0, sib, me).wait_recv()
        for j, chip in enumerate(chips):
            copy(4 + j, (*chip, 1 - c), me).wait_recv()
        for cp in first + passed:
            cp.wait_send()

    vmem = pl.BlockSpec(memory_space=pltpu.VMEM)
    return pl.pallas_call(
        body, name="allgather_weights",
        out_shape=(SDS((IN_COLS, D_MODEL), BF16), SDS(w_kv.shape, BF16), SDS(w_out.shape, BF16)),
        in_specs=[vmem, vmem, vmem], out_specs=(vmem, vmem, vmem),
        scratch_shapes=[pltpu.SemaphoreType.DMA((7,)), pltpu.SemaphoreType.DMA((7,))],
        compiler_params=_params(40),
    )(w_in_t, w_kv, w_out)


def _proj_fwd(x2, g_norm, wint):
    tm = 256
    d = DILATIONS[2][0]
    per_ex = SEQ // tm

    def body(x_ref, g_ref, w_ref, o_ref, o16_ref):
        xv = x_ref[...]
        h = xv * _rstd(xv) * g_ref[...]
        res = _dot_nt(h, w_ref[...])
        o_ref[...] = res
        r_out = lax.broadcasted_iota(jnp.int32, (tm, tm), 0)
        r_in = lax.broadcasted_iota(jnp.int32, (tm, tm), 1)
        pick = (r_in == d * (r_out % (tm // d)) + r_out // (tm // d)).astype(BF16)
        grouped = _dot(pick, res[:, 0:QKV_W]).astype(BF16)
        for rho in range(d):
            o16_ref[0, rho] = grouped[rho * (tm // d):(rho + 1) * (tm // d), :]

    return pl.pallas_call(
        body, name="proj_fwd", grid=(T_LOC // tm,),
        in_specs=[pl.BlockSpec((tm, D_MODEL), lambda i: (i, 0)), pl.BlockSpec((1, D_MODEL), lambda i: (0, 0)),
                  pl.BlockSpec((IN_COLS, D_MODEL), lambda i: (0, 0))],
        out_specs=(pl.BlockSpec((tm, IN_COLS), lambda i: (i, 0)),
                   pl.BlockSpec((1, d, tm // d, QKV_W), lambda i: (i // per_ex, 0, i % per_ex, 0))),
        out_shape=(SDS((T_LOC, IN_COLS), F32), SDS((B_LOC, d, SEQ // d, QKV_W), BF16)),
        compiler_params=_params(48, ("arbitrary",)),
    )(x2, g_norm, wint)


def _memkv_fwd(mem2, g_mem, wkv):
    def body(m_ref, g_ref, w_ref, o_ref):
        mv = m_ref[...]
        o_ref[...] = _dot(mv * _rstd(mv) * g_ref[...], w_ref[...])

    return pl.pallas_call(
        body, name="memkv_fwd", out_shape=SDS((B_LOC * N_MEM, 2 * MEM_W), F32), compiler_params=_params(32),
    )(mem2, g_mem, wkv)


N_BIAS = 7


def _fill_bias_tables(sl_ref, tab):
    for cfg, (d, length) in enumerate(DILATIONS):
        nk = min(length, 2 * CHUNK)
        r = lax.broadcasted_iota(jnp.int32, (CHUNK, nk), 0)
        c = lax.broadcasted_iota(jnp.int32, (CHUNK, nk), 1)
        for var in range(3 if length > nk else 1):
            rel = jnp.abs(r - c + var * RADIUS)
            dist = rel.astype(F32) * float(d)
            for h in range(2):
                slope = sl_ref[0, 0:1, h * HEAD:h * HEAD + 1]
                tab[3 * cfg + var, h * CHUNK:(h + 1) * CHUNK, 0:nk] = jnp.where(rel <= RADIUS, -slope * dist, NEG)


def _attn_blocks(visit, unroll):
    def step(t, carry):
        for cfg, (d, length) in enumerate(DILATIONS):
            nblk = length // CHUNK
            if nblk == 1:
                visit(cfg, 0, t, t, length, t)
                continue
            rho, i = (0, t) if d == 1 else (t // nblk, t % nblk)
            ks = jnp.clip(i * CHUNK - RADIUS, 0, length - 2 * CHUNK)
            visit(cfg, (i * CHUNK - ks) // RADIUS, rho + d * (i * CHUNK), rho + d * ks, 2 * CHUNK, t)
        return carry
    lax.fori_loop(0, 16, step, 0, unroll=unroll)


def _stack_heads(v, left):
    return jnp.concatenate([jnp.where(left, v, 0.0), jnp.where(left, 0.0, v)], axis=0)


def _unstack_heads(v, left):
    return jnp.where(left, v[0:CHUNK], v[CHUNK:2 * CHUNK])


def _rows(start, n, d):
    return pl.ds(start, n) if d == 1 else pl.ds(start, n, stride=d)


def _blk16(col0):
    d, length = DILATIONS[2]
    return pl.BlockSpec((1, d, length, 128), lambda b, hp: (b, 0, 0, col0 // 128 + hp))


def _attn_fwd(proj, qkv16, slopes):
    def body(sl_ref, q_ref, k_ref, v_ref, q16_ref, k16_ref, v16_ref, a_ref, lse_ref, *scr):
        o_c, m_c, l_c, tab = scr[0:3], scr[3:6], scr[6:9], scr[9]
        left = _left_lanes(CHUNK)
        _fill_bias_tables(sl_ref, tab)

        def block(cfg, var, q0, k0, nk, t):
            d = DILATIONS[cfg][0]
            rq, rk = _rows(q0, CHUNK, d), _rows(k0, nk, d)
            if cfg == 2:
                qb, kw, vw = q16_ref[0, t].astype(F32), k16_ref[0, t], v16_ref[0, t]
            else:
                qb, kw, vw = q_ref[rq, :], k_ref[rk, :], v_ref[rk, :]
            qs = _stack_heads(qb * SCALE, left)
            s = _dot_nt(qs, kw) + tab[3 * cfg + var, :, 0:nk]
            m = jnp.max(s, axis=-1, keepdims=True)
            p = jnp.exp(s - m)
            o_c[cfg][rq, :] = _unstack_heads(_dot(p, vw), left)
            m_c[cfg][rq, :] = _unstack_heads(m, left)
            l_c[cfg][rq, :] = _unstack_heads(jnp.sum(p, axis=-1, keepdims=True), left)
        _attn_blocks(block, 8)

        def merge(j, carry):
            rows = pl.ds(pl.multiple_of(j * 256, 256), 256)
            ms = [m_c[i][rows, :] for i in range(3)]
            top = jnp.maximum(jnp.maximum(ms[0], ms[1]), ms[2])
            ws = [jnp.exp(m - top) for m in ms]
            den = l_c[0][rows, :] * ws[0] + l_c[1][rows, :] * ws[1] + l_c[2][rows, :] * ws[2]
            num = o_c[0][rows, :] * ws[0] + o_c[1][rows, :] * ws[1] + o_c[2][rows, :] * ws[2]
            a_ref[rows, :] = num / den
            lse_ref[rows, :] = top + jnp.log(den)
            return carry
        lax.fori_loop(0, SEQ // 256, merge, 0)

    blk = lambda col0: pl.BlockSpec((SEQ, 128), lambda b, hp: (b, col0 // 128 + hp))
    out = pl.BlockSpec((SEQ, 128), lambda b, hp: (b, hp))
    return pl.pallas_call(
        body, name="attn_fwd", grid=(B_LOC, 4),
        in_specs=[pl.BlockSpec((1, 8, 128), lambda b, hp: (hp, 0, 0)), blk(C_QA), blk(C_KA), blk(C_VA),
                  _blk16(C_QA), _blk16(C_KA), _blk16(C_VA)],
        out_specs=(out, out),
        out_shape=(SDS((T_LOC, ATTN_W), F32), SDS((T_LOC, ATTN_W), F32)),
        scratch_shapes=[pltpu.VMEM((SEQ, 128), F32)] * 9 + [pltpu.VMEM((N_BIAS, 2 * CHUNK, 2 * CHUNK), F32)],
        compiler_params=_params(40, ("arbitrary", "arbitrary")),
    )(slopes, proj, proj, proj, qkv16, qkv16, qkv16)


def _chunks_side_by_side(v, pr, tm):
    return jnp.concatenate([v[ch * CHUNK:(ch + 1) * CHUNK, pr * 128:(pr + 1) * 128] for ch in range(tm // CHUNK)], axis=1)


def _first_group_lanes(tm):
    return lax.broadcasted_iota(jnp.int32, (CHUNK, tm), 1) % 128 < HEAD


def _store_chunks(dst_ref, pr, val, tm):
    for ch in range(tm // CHUNK):
        dst_ref[ch * CHUNK:(ch + 1) * CHUNK, pr * 128:(pr + 1) * 128] = val[:, ch * CHUNK:(ch + 1) * CHUNK]


def _sgu_mix(vn, ws_ref, dst_ref, tm):
    first = _first_group_lanes(tm)
    for pr in range(2):
        vp = _chunks_side_by_side(vn, pr, tm)
        _store_chunks(dst_ref, pr, jnp.where(first, _dot(ws_ref[2 * pr], vp), _dot(ws_ref[2 * pr + 1], vp)), tm)


def _mem_head_of_lane(rows):
    return lax.broadcasted_iota(jnp.int32, (rows, MEM_W), 1) // HEAD


def _stack_mem_heads(v, rows):
    head = _mem_head_of_lane(rows)
    return jnp.concatenate([jnp.where(head == h, v, 0.0) for h in range(4)], axis=0)


def _unstack_mem_heads(v, rows):
    head = _mem_head_of_lane(rows)
    out = v[0:rows]
    for h in range(1, 4):
        out = jnp.where(head == h, v[h * rows:(h + 1) * rows], out)
    return out


def _mem_attn_probs(q, kmem, rows):
    qs = _stack_mem_heads(q, rows).astype(BF16)
    s = _dot_nt(qs, kmem) * SCALE
    e = jnp.exp(s - jnp.max(s, axis=-1, keepdims=True))
    return e * (1.0 / jnp.sum(e, axis=-1, keepdims=True)), qs


def _branch_blocks(tm):
    col = lambda w, c0: pl.BlockSpec((tm, w), lambda i: (i, c0 // w))
    return [col(512, C_ZA), col(256, C_UB), col(256, C_VB), col(256, C_ZB), col(256, C_QM), col(256, C_ZM)]


def _branch_fwd(proj, a, kv, w_s, b_exp, g_v):
    tm = 256
    per_ex = SEQ // tm

    def body(za_ref, ub_ref, vb_ref, zb_ref, qm_ref, zm_ref, a_ref, kv_ref, ws_ref, be_ref, gv_ref, o_ref, mix):
        o_ref[:, 0:ATTN_W] = (_silu_and_grad(za_ref[...])[0] * a_ref[...]).astype(BF16)
        gu = _gelu_and_grad(ub_ref[...])[0]
        gv = _gelu_and_grad(vb_ref[...])[0]
        vn = gv * _rstd(gv) * gv_ref[...]
        _sgu_mix(vn.astype(BF16), ws_ref, mix, tm)
        sg = gu * (mix[...] + be_ref[...])
        o_ref[:, ATTN_W:ATTN_W + SGU_W] = (_silu_and_grad(zb_ref[...])[0] * sg).astype(BF16)
        p = _mem_attn_probs(qm_ref[...], kv_ref[:, 0:MEM_W], tm)[0]
        mo = _unstack_mem_heads(_dot(p, kv_ref[:, MEM_W:2 * MEM_W]), tm)
        o_ref[:, ATTN_W + SGU_W:D_MODEL] = (_silu_and_grad(zm_ref[...])[0] * mo).astype(BF16)

    full = lambda shape: pl.BlockSpec(shape, lambda i: (0,) * len(shape))
    return pl.pallas_call(
        body, name="branch_fwd", grid=(T_LOC // tm,),
        in_specs=_branch_blocks(tm) + [
            pl.BlockSpec((tm, ATTN_W), lambda i: (i, 0)), pl.BlockSpec((N_MEM, 2 * MEM_W), lambda i: (i // per_ex, 0)),
            full((4, CHUNK, CHUNK)), full((tm, SGU_W)), full((1, SGU_W))],
        out_specs=pl.BlockSpec((tm, D_MODEL), lambda i: (i, 0)),
        out_shape=SDS((T_LOC, D_MODEL), BF16),
        scratch_shapes=[pltpu.VMEM((tm, SGU_W), F32)],
        compiler_params=_params(VMEM_NO_STAGING_MIB, ("arbitrary",)),
    )(proj, proj, proj, proj, proj, proj, a, kv, w_s, b_exp, g_v)


def _outproj_loss(gated, wout, x2, tgt2, g_final):
    tm = 512

    def body(g_ref, w_ref, x_ref, t_ref, gf_ref, dh2_ref, loss_ref, dgf_ref):
        @pl.when(pl.program_id(0) == 0)
        def _():
            loss_ref[...] = jnp.zeros_like(loss_ref)
            dgf_ref[...] = jnp.zeros_like(dgf_ref)
        h2 = x_ref[...] + _dot(g_ref[...], w_ref[...])
        r = _rstd(h2)
        gf = gf_ref[...]
        err = h2 * r * gf - t_ref[...]
        loss_ref[...] += 0.5 * jnp.sum(jnp.mean(err * err, axis=-1, keepdims=True))
        dy = err * (1.0 / D_MODEL)
        dh2_ref[...] = _rms_bwd(h2, r, gf, dy)
        dgf_ref[...] += jnp.sum(dy * (h2 * r), axis=0, keepdims=True)

    row = pl.BlockSpec((tm, D_MODEL), lambda i: (i, 0))
    vec = pl.BlockSpec((1, D_MODEL), lambda i: (0, 0))
    return pl.pallas_call(
        body, name="outproj_loss", grid=(T_LOC // tm,),
        in_specs=[row, pl.BlockSpec((D_MODEL, D_MODEL), lambda i: (0, 0)), row, row, vec],
        out_specs=(row, pl.BlockSpec((8, 128), lambda i: (0, 0)), vec),
        out_shape=(SDS((T_LOC, D_MODEL), F32), SDS((8, 128), F32), SDS((1, D_MODEL), F32)),
        compiler_params=_params(VMEM_NO_STAGING_MIB, ("arbitrary",)),
    )(gated, wout, x2, tgt2, g_final)


def _branch_bwd(dh2, wout, gated, proj, a, kv, w_s, b_exp, g_v):
    tm = 256
    per_ex = SEQ // tm

    def body(dh2_ref, w_ref, g_ref, za_ref, ub_ref, vb_ref, zb_ref, qm_ref, zm_ref, a_ref, kv_ref, ws_ref,
             be_ref, gv_ref, da_ref, dr_ref, dkv_ref, dwo_ref, dws_ref, db_ref, dgv_ref, mix, dvn, dmsum, dwo_acc):
        i = pl.program_id(0)

        @pl.when(i == 0)
        def _():
            dwo_acc[...] = jnp.zeros_like(dwo_acc)
            dws_ref[...] = jnp.zeros_like(dws_ref)
            dgv_ref[...] = jnp.zeros_like(dgv_ref)
            dmsum[...] = jnp.zeros_like(dmsum)

        @pl.when(i % per_ex == 0)
        def _():
            dkv_ref[...] = jnp.zeros_like(dkv_ref)

        dh2 = dh2_ref[...].astype(BF16)
        dwo_acc[...] += _dot_tn(g_ref[...], dh2)
        dg = _dot_nt(dh2, w_ref[...])

        sa, dsa = _silu_and_grad(za_ref[...])
        dga = dg[:, 0:ATTN_W]
        da_ref[...] = dga * sa
        dr_ref[:, 0:512] = (dga * a_ref[...] * dsa).astype(BF16)

        ub, vb = ub_ref[...], vb_ref[...]
        gu, dgu = _gelu_and_grad(ub)
        gv, dgv = _gelu_and_grad(vb)
        rv = _rstd(gv)
        gain = gv_ref[...]
        vn = (gv * rv * gain).astype(BF16)
        _sgu_mix(vn, ws_ref, mix, tm)
        mixed = mix[...] + be_ref[...]
        sb, dsb = _silu_and_grad(zb_ref[...])
        dgb = dg[:, ATTN_W:ATTN_W + SGU_W]
        dsg = dgb * sb
        dr_ref[:, 512:768] = (dsg * mixed * dgu).astype(BF16)
        dr_ref[:, 1024:1280] = (dgb * (gu * mixed) * dsb).astype(BF16)
        dmix = dsg * gu
        for ch in range(tm // CHUNK):
            dmsum[...] += dmix[ch * CHUNK:(ch + 1) * CHUNK, :]
        first = _first_group_lanes(tm)
        for pr in range(2):
            dmp, vp = _chunks_side_by_side(dmix, pr, tm), _chunks_side_by_side(vn, pr, tm)
            dws_ref[2 * pr] += _dot_nt(jnp.where(first, dmp, 0.0), vp)
            dws_ref[2 * pr + 1] += _dot_nt(jnp.where(first, 0.0, dmp), vp)
            _store_chunks(dvn, pr, jnp.where(first, _dot_tn(ws_ref[2 * pr], dmp), _dot_tn(ws_ref[2 * pr + 1], dmp)), tm)
        dvn_v = dvn[...]
        dgv_ref[...] += jnp.sum(dvn_v * (gv * rv), axis=0, keepdims=True)
        dr_ref[:, 768:1024] = (_rms_bwd(gv, rv, gain, dvn_v) * dgv).astype(BF16)

        szm, dszm = _silu_and_grad(zm_ref[...])
        dgm = dg[:, ATTN_W + SGU_W:D_MODEL]
        kmem, vmem_ = kv_ref[:, 0:MEM_W].astype(BF16), kv_ref[:, MEM_W:2 * MEM_W].astype(BF16)
        p, qs = _mem_attn_probs(qm_ref[...], kmem, tm)
        dmos = _stack_mem_heads(dgm * szm, tm).astype(BF16)
        dr_ref[:, 1536:1792] = (dgm * _unstack_mem_heads(_dot(p, vmem_), tm) * dszm).astype(BF16)
        dp = _dot_nt(dmos, vmem_)
        ds = (p * (dp - jnp.sum(p * dp, axis=-1, keepdims=True)) * SCALE).astype(BF16)
        dr_ref[:, 1280:1536] = _unstack_mem_heads(_dot(ds, kmem), tm).astype(BF16)
        dkv_ref[:, 0:MEM_W] += _dot_tn(ds, qs)
        dkv_ref[:, MEM_W:2 * MEM_W] += _dot_tn(p, dmos)

        @pl.when(i == pl.num_programs(0) - 1)
        def _():
            tot = dmsum[...]
            hi = tot.astype(BF16)
            lo = (tot - hi.astype(F32)).astype(BF16)
            grp = (lax.broadcasted_iota(jnp.int32, (SGU_W, 128), 0) // HEAD
                   == lax.broadcasted_iota(jnp.int32, (SGU_W, 128), 1)).astype(BF16)
            db_ref[...] = (_dot(hi, grp) + _dot(lo, grp)).T[0:4, :]
            _cast_rows(dwo_acc, dwo_ref, D_MODEL)

    full = lambda shape: pl.BlockSpec(shape, lambda i: (0,) * len(shape))
    row = lambda w: pl.BlockSpec((tm, w), lambda i: (i, 0))
    return pl.pallas_call(
        body, name="branch_bwd", grid=(T_LOC // tm,),
        in_specs=[row(D_MODEL), full((D_MODEL, D_MODEL)), row(D_MODEL)] + _branch_blocks(tm) + [
            row(ATTN_W), pl.BlockSpec((N_MEM, 2 * MEM_W), lambda i: (i // per_ex, 0)),
            full((4, CHUNK, CHUNK)), full((tm, SGU_W)), full((1, SGU_W))],
        out_specs=(row(ATTN_W), row(REST_W), pl.BlockSpec((N_MEM, 2 * MEM_W), lambda i: (i // per_ex, 0)),
                   full((D_MODEL, D_MODEL)), full((4, CHUNK, CHUNK)), full((4, CHUNK)), full((1, SGU_W))),
        out_shape=(SDS((T_LOC, ATTN_W), F32), SDS((T_LOC, REST_W), BF16), SDS((B_LOC * N_MEM, 2 * MEM_W), F32),
                   SDS((D_MODEL, D_MODEL), BF16), SDS((4, CHUNK, CHUNK), F32), SDS((4, CHUNK), F32), SDS((1, SGU_W), F32)),
        scratch_shapes=[pltpu.VMEM((tm, SGU_W), F32), pltpu.VMEM((tm, SGU_W), F32), pltpu.VMEM((CHUNK, SGU_W), F32),
                        pltpu.VMEM((D_MODEL, D_MODEL), F32)],
        compiler_params=_params(56, ("arbitrary",)),
    )(dh2, wout, gated, proj, proj, proj, proj, proj, proj, a, kv, w_s, b_exp, g_v)


def _attn_bwd(proj, qkv16, slopes, da, a, lse):
    def body(sl_ref, q_ref, k_ref, v_ref, q16_ref, k16_ref, v16_ref, da_ref, a_ref, lse_ref,
             dq_ref, dk_ref, dv_ref, *scr):
        dq_s, dk_s, dv_s, tab = scr[0:3], scr[3:6], scr[6:9], scr[9]
        lse_h, delta_h = scr[10:12], scr[12:14]
        p_all, ds_all = scr[14], scr[15]
        left = _left_lanes(CHUNK)
        _fill_bias_tables(sl_ref, tab)

        def prep(j, carry):
            rows = pl.ds(pl.multiple_of(j * 256, 256), 256)
            l256 = _left_lanes(256)
            prod = da_ref[rows, :] * a_ref[rows, :]
            delta_h[0][rows, :] = jnp.broadcast_to(jnp.sum(jnp.where(l256, prod, 0.0), axis=-1, keepdims=True), (256, 128))
            delta_h[1][rows, :] = jnp.broadcast_to(jnp.sum(jnp.where(l256, 0.0, prod), axis=-1, keepdims=True), (256, 128))
            pair = lse_ref[rows, :]
            other = pltpu.roll(pair, HEAD, axis=1)
            lse_h[0][rows, :] = jnp.where(l256, pair, other)
            lse_h[1][rows, :] = jnp.where(l256, other, pair)
            zero = jnp.zeros((256, 128), F32)
            for cfg in range(3):
                dk_s[cfg][rows, :] = zero
                dv_s[cfg][rows, :] = zero
            return carry
        lax.fori_loop(0, SEQ // 256, prep, 0)

        def per_row(halves, rq, nk):
            v = jnp.concatenate([halves[0][rq, :], halves[1][rq, :]], axis=0)
            return v if nk == 128 else jnp.concatenate([v, v], axis=1)

        def qkv(cfg, rq, rk, t):
            if cfg == 2:
                return q16_ref[0, t].astype(F32), k16_ref[0, t], v16_ref[0, t]
            return q_ref[rq, :], k_ref[rk, :], v_ref[rk, :]

        def probs(cfg, var, q0, k0, nk, t):
            d = DILATIONS[cfg][0]
            rq, rk = _rows(q0, CHUNK, d), _rows(k0, nk, d)
            qb, kw, vw = qkv(cfg, rq, rk, t)
            qs = _stack_heads(qb * SCALE, left)
            das = _stack_heads(da_ref[rq, :], left)
            s = _dot_nt(qs, kw) + tab[3 * cfg + var, :, 0:nk]
            p = jnp.exp(s - per_row(lse_h, rq, nk))
            p_all[16 * cfg + t, :, 0:nk] = p.astype(BF16)
            ds_all[16 * cfg + t, :, 0:nk] = (p * (_dot_nt(das, vw) - per_row(delta_h, rq, nk))).astype(BF16)
        _attn_blocks(probs, 4)

        def grads(cfg, var, q0, k0, nk, t):
            d = DILATIONS[cfg][0]
            rq, rk = _rows(q0, CHUNK, d), _rows(k0, nk, d)
            qb, kw, _ = qkv(cfg, rq, rk, t)
            qs = _stack_heads(qb * SCALE, left).astype(BF16)
            das = _stack_heads(da_ref[rq, :], left).astype(BF16)
            p, ds = p_all[16 * cfg + t, :, 0:nk], ds_all[16 * cfg + t, :, 0:nk]
            dq_s[cfg][rq, :] = _unstack_heads(_dot(ds, kw), left) * SCALE
            dk_s[cfg][rk, :] += _dot_tn(ds, qs)
            dv_s[cfg][rk, :] += _dot_tn(p, das)
        _attn_blocks(grads, 4)

        def flush(j, carry):
            rows = pl.ds(pl.multiple_of(j * 256, 256), 256)
            for acc, dst in ((dq_s, dq_ref), (dk_s, dk_ref), (dv_s, dv_ref)):
                dst[rows, :] = (acc[0][rows, :] + acc[1][rows, :] + acc[2][rows, :]).astype(BF16)
            return carry
        lax.fori_loop(0, SEQ // 256, flush, 0)

    blk = lambda col0: pl.BlockSpec((SEQ, 128), lambda b, hp: (b, col0 // 128 + hp))
    own = pl.BlockSpec((SEQ, 128), lambda b, hp: (b, hp))
    return pl.pallas_call(
        body, name="attn_bwd", grid=(B_LOC, 4),
        in_specs=[pl.BlockSpec((1, 8, 128), lambda b, hp: (hp, 0, 0)), blk(C_QA), blk(C_KA), blk(C_VA),
                  _blk16(C_QA), _blk16(C_KA), _blk16(C_VA), own, own, own],
        out_specs=(own, own, own),
        out_shape=(SDS((T_LOC, ATTN_W), BF16),) * 3,
        scratch_shapes=[pltpu.VMEM((SEQ, 128), F32)] * 9 + [pltpu.VMEM((N_BIAS, 2 * CHUNK, 2 * CHUNK), F32)]
        + [pltpu.VMEM((SEQ, 128), F32)] * 4 + [pltpu.VMEM((48, 2 * CHUNK, 2 * CHUNK), BF16)] * 2,
        compiler_params=_params(52, ("arbitrary", "arbitrary")),
    )(slopes, proj, proj, proj, qkv16, qkv16, qkv16, da, a, lse)


def _dproj_specs(tm):
    third = pl.BlockSpec((tm, ATTN_W), lambda i: (i, 0))
    return [third, third, third, pl.BlockSpec((tm, REST_W), lambda i: (i, 0))]


def _dx(dq, dk, dv, dr, wint, x2, dh2, g_norm):
    tm = 256

    def body(dq_ref, dk_ref, dv_ref, dr_ref, w_ref, x_ref, dh2_ref, g_ref, gx_ref, dgn_ref):
        @pl.when(pl.program_id(0) == 0)
        def _():
            dgn_ref[...] = jnp.zeros_like(dgn_ref)
        dh = (_dot(dq_ref[...], w_ref[C_QA:C_KA, :]) + _dot(dk_ref[...], w_ref[C_KA:C_VA, :])
              + _dot(dv_ref[...], w_ref[C_VA:C_ZA, :]) + _dot(dr_ref[...], w_ref[C_ZA:IN_COLS, :]))
        xv = x_ref[...]
        r = _rstd(xv)
        gx_ref[...] = dh2_ref[...] + _rms_bwd(xv, r, g_ref[...], dh)
        dgn_ref[...] += jnp.sum(dh * (xv * r), axis=0, keepdims=True)

    row = pl.BlockSpec((tm, D_MODEL), lambda i: (i, 0))
    vec = pl.BlockSpec((1, D_MODEL), lambda i: (0, 0))
    return pl.pallas_call(
        body, name="dx", grid=(T_LOC // tm,),
        in_specs=_dproj_specs(tm) + [pl.BlockSpec((IN_COLS, D_MODEL), lambda i: (0, 0)), row, row, vec],
        out_specs=(row, vec),
        out_shape=(SDS((T_LOC, D_MODEL), F32), SDS((1, D_MODEL), F32)),
        compiler_params=_params(48, ("arbitrary",)),
    )(dq, dk, dv, dr, wint, x2, dh2, g_norm)


def _dwin(dq, dk, dv, dr, x2, g_norm, half):
    tm = 512
    width = D_MODEL // 2
    cols = slice(half * width, (half + 1) * width)

    def body(dq_ref, dk_ref, dv_ref, dr_ref, x_ref, g_ref, o_ref, acc):
        @pl.when(pl.program_id(0) == 0)
        def _():
            acc[...] = jnp.zeros_like(acc)
        xv = x_ref[...]
        h = (xv[:, cols] * _rstd(xv) * g_ref[:, cols]).astype(BF16)
        acc[C_QA:C_KA, :] += _dot_tn(dq_ref[...], h)
        acc[C_KA:C_VA, :] += _dot_tn(dk_ref[...], h)
        acc[C_VA:C_ZA, :] += _dot_tn(dv_ref[...], h)
        acc[C_ZA:IN_COLS, :] += _dot_tn(dr_ref[...], h)

        @pl.when(pl.program_id(0) == pl.num_programs(0) - 1)
        def _():
            _cast_rows(acc, o_ref, IN_COLS)

    return pl.pallas_call(
        body, name="dwin%d" % half, grid=(T_LOC // tm,),
        in_specs=_dproj_specs(tm) + [pl.BlockSpec((tm, D_MODEL), lambda i: (i, 0)), pl.BlockSpec((1, D_MODEL), lambda i: (0, 0))],
        out_specs=pl.BlockSpec((IN_COLS, width), lambda i: (0, 0)),
        out_shape=SDS((IN_COLS, width), BF16),
        scratch_shapes=[pltpu.VMEM((IN_COLS, width), F32)],
        compiler_params=_params(48, ("arbitrary",)),
    )(dq, dk, dv, dr, x2, g_norm)


def _memkv_bwd(dkv, mem2, g_mem, wkv):
    def body(dkv_ref, m_ref, g_ref, w_ref, dw_ref, dg_ref):
        mv = m_ref[...]
        r = _rstd(mv)
        dkv_v = dkv_ref[...].astype(BF16)
        dw_ref[...] = _dot_tn(mv * r * g_ref[...], dkv_v).astype(BF16)
        dg_ref[...] = jnp.sum(_dot_nt(dkv_v, w_ref[...]) * (mv * r), axis=0, keepdims=True)

    return pl.pallas_call(
        body, name="memkv_bwd", out_shape=(SDS((D_MODEL, 2 * MEM_W), BF16), SDS((1, D_MODEL), F32)),
        compiler_params=_params(32),
    )(dkv, mem2, g_mem, wkv)


def _allreduce_small(parts):
    n = len(parts)

    def body(*refs):
        ins, outs, bufs = refs[0:n], refs[n:2 * n], refs[2 * n:3 * n]
        send_sems, recv_sems = refs[3 * n], refs[3 * n + 1]
        pos = _mesh_pos()
        me = _flat(pos)
        for a in range(n):
            bufs[a][me] = ins[a][...]

        def copy(a, k, slot):
            return pltpu.make_async_remote_copy(
                src_ref=ins[a], dst_ref=bufs[a].at[slot],
                send_sem=send_sems.at[7 * a + k - 1], recv_sem=recv_sems.at[7 * a + k - 1],
                device_id=_peer(pos, k), device_id_type=MESH)

        sent = [copy(a, k, me) for a in range(n) for k in range(1, N_DEV)]
        for cp in sent:
            cp.start()
        for a in range(n):
            for k in range(1, N_DEV):
                copy(a, k, _flat(_peer(pos, k))).wait_recv()
        for cp in sent:
            cp.wait_send()
        for a in range(n):
            acc = bufs[a][0]
            for s in range(1, N_DEV):
                acc = acc + bufs[a][s]
            outs[a][...] = acc

    vmem = pl.BlockSpec(memory_space=pltpu.VMEM)
    return pl.pallas_call(
        body, name="allreduce_small",
        out_shape=tuple(SDS(p.shape, F32) for p in parts),
        in_specs=[vmem] * n, out_specs=(vmem,) * n,
        scratch_shapes=[pltpu.VMEM((N_DEV,) + p.shape, F32) for p in parts]
        + [pltpu.SemaphoreType.DMA((7 * n,)), pltpu.SemaphoreType.DMA((7 * n,))],
        compiler_params=_params(16),
    )(*parts)


_HBM = pl.BlockSpec(memory_space=pltpu.HBM)
_SEM = pl.BlockSpec(memory_space=pltpu.SEMAPHORE)
_SIDE_EFFECT = pltpu.SideEffectType.DATAFLOW_SIDE_EFFECTING


def _exchange_copies(src_refs, land_refs, scatter, send_sems, recv_sems):
    pos = _mesh_pos()
    copies = []
    for a, (src, land) in enumerate(zip(src_refs, land_refs)):
        n = land.shape[1]
        for k in range(1, N_DEV):
            peer = _peer(pos, k)
            piece = src.at[pl.ds(pl.multiple_of(_flat(peer) * n, 16), n), :] if scatter[a] else src
            copies.append(pltpu.make_async_remote_copy(
                src_ref=piece, dst_ref=land.at[_flat(pos)],
                send_sem=send_sems.at[7 * a + k - 1], recv_sem=recv_sems.at[7 * a + k - 1],
                device_id=peer, device_id_type=MESH))
    return copies


def _exchange_start(name, srcs, scatter, lands):
    n = len(srcs)

    def body(*refs):
        for cp in _exchange_copies(refs[0:n], refs[n:2 * n], scatter, refs[2 * n], refs[2 * n + 1]):
            cp.start()
        refs[-1][...] = jnp.zeros_like(refs[-1])

    ops = [pltpu.with_memory_space_constraint(t, pltpu.HBM) for t in (*srcs, *lands)]
    out = pl.pallas_call(
        body, name=name,
        out_shape=(pltpu.SemaphoreType.DMA((7 * n,)), pltpu.SemaphoreType.DMA((7 * n,)),
                   *[pltpu.HBM(t.shape, t.dtype) for t in ops], SDS((8, 128), F32)),
        in_specs=[_HBM] * (2 * n),
        out_specs=(_SEM, _SEM, *[_HBM] * (2 * n), pl.BlockSpec(memory_space=pltpu.VMEM)),
        input_output_aliases={i: 2 + i for i in range(2 * n)},
        compiler_params=pltpu.CompilerParams(has_side_effects=_SIDE_EFFECT),
    )(*ops)
    return out[0], out[1], out[2:2 + n], out[2 + n:2 + 2 * n], out[-1]


def _exchange_wait(name, started, scatter, after):
    send_sems, recv_sems, srcs, lands, _ = started
    n = len(srcs)

    def body(*refs):
        for cp in _exchange_copies(refs[0:n], refs[n:2 * n], scatter, refs[2 * n], refs[2 * n + 1]):
            cp.wait_send()
            cp.wait_recv()

    out = pl.pallas_call(
        body, name=name,
        out_shape=tuple(pltpu.HBM(t.shape, t.dtype) for t in (*srcs, *lands)),
        in_specs=[_HBM] * (2 * n) + [_SEM, _SEM, pl.BlockSpec(memory_space=pl.ANY)],
        out_specs=(_HBM,) * (2 * n),
        input_output_aliases={i: i for i in range(2 * n)},
        compiler_params=pltpu.CompilerParams(has_side_effects=_SIDE_EFFECT),
    )(*srcs, *lands, send_sems, recv_sems, after)
    return out[n:]


def _landing(own, me):
    return lax.dynamic_update_slice(lax.empty((N_DEV,) + own.shape, own.dtype), own[None], (me,) + (0,) * own.ndim)


def _adamw(w, g, m, v):
    m = ADAM_B1 * m + (1.0 - ADAM_B1) * g
    v = ADAM_B2 * v + (1.0 - ADAM_B2) * (g * g)
    m_hat = m / (1.0 - ADAM_B1 ** ADAM_STEP)
    v_hat = v / (1.0 - ADAM_B2 ** ADAM_STEP)
    return -ADAM_LR * (m_hat / (jnp.sqrt(v_hat) + ADAM_EPS) + ADAM_WD * w), m, v


def _adam_slots(name, pieces, w, m, v):
    rows, cols = w.shape
    starts = [sum(p.shape[2] for p in pieces[:i]) for i in range(len(pieces) + 1)]
    assert starts[-1] == cols and all(p.shape[1] == rows for p in pieces)

    def body(*refs):
        s_refs, (w_ref, m_ref, v_ref, g_o, d_o, m_o, v_o, acc) = refs[:len(pieces)], refs[len(pieces):]
        s = pl.program_id(0)

        @pl.when(s == 0)
        def _():
            for i, s_ref in enumerate(s_refs):
                acc[:, starts[i]:starts[i + 1]] = s_ref[0].astype(F32)

        @pl.when(s > 0)
        def _():
            for i, s_ref in enumerate(s_refs):
                acc[:, starts[i]:starts[i + 1]] += s_ref[0].astype(F32)

        @pl.when(s == N_DEV - 1)
        def _():
            g = acc[...]
            g_o[...] = g
            d_o[...], m_o[...], v_o[...] = _adamw(w_ref[...], g, m_ref[...], v_ref[...])

    full = pl.BlockSpec((rows, cols), lambda s: (0, 0))
    return pl.pallas_call(
        body, name="adam_" + name, grid=(N_DEV,),
        in_specs=[pl.BlockSpec((1, rows, p.shape[2]), lambda s: (s, 0, 0)) for p in pieces] + [full, full, full],
        out_specs=(full,) * 4, out_shape=(SDS((rows, cols), F32),) * 4,
        scratch_shapes=[pltpu.VMEM((rows, cols), F32)],
        compiler_params=_params(40, ("arbitrary",)),
    )(*pieces, w, m, v)


def _adam_small(ws, gs, ms, vs, loss_slots):
    n = len(ws)

    def total(ref, like):
        if len(ref.shape) == len(like.shape):
            return ref[...]
        acc = ref[0]
        for s in range(1, N_DEV):
            acc = acc + ref[s]
        return acc

    def body(*refs):
        w_r, g_r, m_r, v_r = refs[0:n], refs[n:2 * n], refs[2 * n:3 * n], refs[3 * n:4 * n]
        loss_r, outs = refs[4 * n], refs[4 * n + 1:]
        for a in range(n):
            g = total(g_r[a], w_r[a])
            outs[a][...] = g
            outs[n + 1 + 3 * a][...], outs[n + 2 + 3 * a][...], outs[n + 3 + 3 * a][...] = _adamw(
                w_r[a][...], g, m_r[a][...], v_r[a][...])
        outs[n][...] = total(loss_r, outs[n])

    out = pl.pallas_call(
        body, name="adam_small",
        out_shape=tuple(SDS(w.shape, F32) for w in ws) + (SDS(loss_slots.shape[1:], F32),)
        + tuple(SDS(w.shape, F32) for w in ws for _ in range(3)),
        compiler_params=_params(16),
    )(*ws, *gs, *ms, *vs, loss_slots)
    return out[0:n], out[n], out[n + 1:]


def kernel(x, mem, g_norm, w_in, w_sgu_spatial, b_sgu_spatial, g_sgu_v, g_mem, w_mem_kv, w_out, g_final, loss_target, m_g_norm, m_w_in, m_w_sgu_spatial, m_b_sgu_spatial, m_g_sgu_v, m_g_mem, m_w_mem_kv, m_w_out, m_g_final, v_g_norm, v_w_in, v_w_sgu_spatial, v_b_sgu_spatial, v_g_sgu_v, v_g_mem, v_w_mem_kv, v_w_out, v_g_final):
    x2 = x.reshape(T_LOC, D_MODEL)
    tgt2 = loss_target.reshape(T_LOC, D_MODEL)
    mem2 = mem.reshape(B_LOC * N_MEM, D_MODEL)
    w_s = w_sgu_spatial[0]
    b_exp = jnp.tile(jnp.repeat(b_sgu_spatial[0].T, HEAD, axis=1), (2, 1))
    slope = jnp.power(2.0, -8.0 * (jnp.arange(8, dtype=F32) + 1.0) / 8)
    slopes = jnp.broadcast_to(jnp.repeat(slope.reshape(4, 2), HEAD, axis=1)[:, None, :], (4, 8, 128))

    tr = lambda t: jnp.transpose(t[0])

    me = _flat(_mesh_pos())
    own_rows = lambda t: lax.dynamic_slice_in_dim(t, me * (t.shape[0] // N_DEV), t.shape[0] // N_DEV)

    wint, wkv_own, wout_own = _allgather_weights(tr(w_in), w_mem_kv[0], w_out[0])
    started0 = _exchange_start("exchange0_start", [wkv_own, wout_own], [False, False],
                               [_landing(wkv_own, me), _landing(wout_own, me)])
    proj, qkv16 = _proj_fwd(x2, g_norm + started0[4][0:1, 0:1], wint)
    wkv, wout = _exchange_wait("exchange0_wait", started0, [False, False], proj)
    wkv, wout = wkv.reshape(D_MODEL, 2 * MEM_W), wout.reshape(D_MODEL, D_MODEL)
    kv = _memkv_fwd(mem2, g_mem, wkv)
    a, lse = _attn_fwd(proj, qkv16, slopes)
    gated = _branch_fwd(proj, a, kv, w_s, b_exp, g_sgu_v)
    dh2, loss8, dgf = _outproj_loss(gated, wout, x2, tgt2, g_final.reshape(1, D_MODEL))

    da, dr, dkv, dwout, dws, dbs, dgv = _branch_bwd(dh2, wout, gated, proj, a, kv, w_s, b_exp, g_sgu_v)
    dwkv, dgm = _memkv_bwd(dkv, mem2, g_mem, wkv)

    early = [dws.reshape(4 * CHUNK, CHUNK), dbs, dgv, dgm, dgf, loss8]
    scatter1 = [True, True] + [False] * len(early)
    started1 = _exchange_start(
        "exchange1_start", [dwkv, dwout] + early, scatter1,
        [_landing(own_rows(dwkv), me), _landing(own_rows(dwout), me)] + [_landing(t, me) for t in early])
    dq, dk, dv = _attn_bwd(proj, qkv16, slopes + started1[4][0:1, 0:1], da, a, lse)
    s_wkv, s_wout, s_ws, s_bs, s_gv, s_gm, s_gf, s_loss = _exchange_wait("exchange1_wait", started1, scatter1, dq)

    dwint0 = _dwin(dq, dk, dv, dr, x2, g_norm, 0)
    started2 = _exchange_start("exchange2_start", [dwint0], [True], [_landing(own_rows(dwint0), me)])
    dwint1 = _dwin(dq, dk, dv, dr, x2, g_norm + started2[4][0:1, 0:1], 1)
    started3 = _exchange_start("exchange3_start", [dwint1], [True], [_landing(own_rows(dwint1), me)])
    grad_x, dgn = _dx(dq, dk, dv, dr, wint, x2, dh2, g_norm + started3[4][0:1, 0:1])
    s_win0, = _exchange_wait("exchange2_wait", started2, [True], grad_x)
    dgn_sum, = _allreduce_small([dgn])
    s_win1, = _exchange_wait("exchange3_wait", started3, [True], dgn_sum)

    g_win, d_win, m_win, v_win = map(
        jnp.transpose, _adam_slots("w_in", [s_win0, s_win1], tr(w_in), tr(m_w_in), tr(v_w_in)))
    g_wkv, d_wkv, m_wkv, v_wkv = _adam_slots("w_mem_kv", [s_wkv], w_mem_kv[0], m_w_mem_kv[0], v_w_mem_kv[0])
    g_wout, d_wout, m_wout, v_wout = _adam_slots("w_out", [s_wout], w_out[0], m_w_out[0], v_w_out[0])

    small_shapes = [(1, D_MODEL), (4 * CHUNK, CHUNK), (4, CHUNK), (1, SGU_W), (1, D_MODEL), (1, D_MODEL)]
    pack = lambda arrs: [t.reshape(s) for t, s in zip(arrs, small_shapes)]
    g_small, loss_sum, upd = _adam_small(
        pack([g_norm, w_sgu_spatial, b_sgu_spatial, g_sgu_v, g_mem, g_final]),
        [dgn_sum, s_ws, s_bs, s_gv, s_gm, s_gf],
        pack([m_g_norm, m_w_sgu_spatial, m_b_sgu_spatial, m_g_sgu_v, m_g_mem, m_g_final]),
        pack([v_g_norm, v_w_sgu_spatial, v_b_sgu_spatial, v_g_sgu_v, v_g_mem, v_g_final]), s_loss)
    out_shapes = [g_norm.shape, w_sgu_spatial.shape, b_sgu_spatial.shape, g_sgu_v.shape, g_mem.shape, g_final.shape]
    unpack = lambda arrs: [t.reshape(s) for t, s in zip(arrs, out_shapes)]
    gs = unpack(g_small)
    ds, nms, nvs = unpack(upd[0::3]), unpack(upd[1::3]), unpack(upd[2::3])

    loss = loss_sum[0, 0]

    def assemble(small, win, wkv_, wout_):
        return [small[0], win[None], small[1], small[2], small[3], small[4], wkv_[None], wout_[None], small[5]]

    return (loss, grad_x.reshape(x.shape),
            *assemble(gs, g_win, g_wkv, g_wout), *assemble(ds, d_win, d_wkv, d_wout),
            *assemble(nms, m_win, m_wkv, m_wout), *assemble(nvs, v_win, v_wkv, v_wout))
```

```python
import jax
import jax.numpy as jnp
from jax import lax
from jax.experimental import pallas as pl
from jax.experimental.pallas import tpu as pltpu

F32 = jnp.float32
BF16 = jnp.bfloat16
SDS = jax.ShapeDtypeStruct
MESH = pl.DeviceIdType.MESH

N_DEV = 8
D_MODEL = 1024
SEQ = 2048
B_LOC = 2
T_LOC = B_LOC * SEQ
N_MEM = 256
HEAD = 64
ATTN_W = 512
SGU_W = 256
MEM_W = 256
IN_COLS = 3328
W_IN_SHARD = IN_COLS // N_DEV
CHUNK = 128
DILATIONS = ((1, 2048), (4, 512), (16, 128))
RADIUS = 64
EPS = 1e-6
NEG = -1e30
SCALE = HEAD ** -0.5
C_QA, C_KA, C_VA, C_ZA, C_UB, C_VB, C_ZB, C_QM, C_ZM = 0, 512, 1024, 1536, 2048, 2304, 2560, 2816, 3072
QKV_W = 1536
REST_W = IN_COLS - QKV_W

ADAM_LR, ADAM_B1, ADAM_B2, ADAM_EPS, ADAM_WD, ADAM_STEP = 0.001, 0.9, 0.999, 1e-08, 0.01, 10

V7X_VMEM_MIB = 64
VMEM_NO_STAGING_MIB = V7X_VMEM_MIB - 6


def _params(vmem_mib, sem=None):
    assert vmem_mib < V7X_VMEM_MIB
    return pltpu.CompilerParams(vmem_limit_bytes=vmem_mib << 20, dimension_semantics=sem)


def _dot(a, b):
    return jnp.dot(a.astype(BF16), b.astype(BF16), preferred_element_type=F32)


def _dot_nt(a, b):
    return lax.dot_general(a.astype(BF16), b.astype(BF16), (((1,), (1,)), ((), ())), preferred_element_type=F32)


def _dot_tn(a, b):
    return lax.dot_general(a.astype(BF16), b.astype(BF16), (((0,), (0,)), ((), ())), preferred_element_type=F32)


def _rstd(v):
    return lax.rsqrt(jnp.mean(v * v, axis=-1, keepdims=True) + EPS)


def _rms_bwd(v, r, g, dy):
    gdy = g * dy
    return r * gdy - v * (r * r * r * jnp.mean(gdy * v, axis=-1, keepdims=True))


def _sigmoid(z):
    return 1.0 / (1.0 + jnp.exp(-z))


def _silu_and_grad(z):
    s = _sigmoid(z)
    return z * s, s * (1.0 + z * (1.0 - s))


_G_C = 0.7978845608028654
_G_K = 0.044715


def _gelu_and_grad(v):
    t = jnp.tanh(_G_C * (v + _G_K * (v * v * v)))
    cdf = 0.5 * (1.0 + t)
    return v * cdf, cdf + 0.5 * v * (1.0 - t * t) * (_G_C * (1.0 + 3.0 * _G_K * v * v))


def _cast_rows(src_ref, dst_ref, rows, step=256):
    def one(i, carry):
        r = pl.ds(pl.multiple_of(i * step, step), step)
        dst_ref[r, :] = src_ref[r, :].astype(dst_ref.dtype)
        return carry
    lax.fori_loop(0, rows // step, one, 0)


def _left_lanes(rows):
    return lax.broadcasted_iota(jnp.int32, (rows, 128), 1) < HEAD


def _mesh_pos():
    return lax.axis_index("x"), lax.axis_index("y"), lax.axis_index("c")


def _peer(pos, k):
    x, y, c = pos
    return (1 - x if k & 4 else x, 1 - y if k & 2 else y, 1 - c if k & 1 else c)


def _flat(pos):
    return 4 * pos[0] + 2 * pos[1] + pos[2]


def _allgather_weights(w_in_t, w_kv, w_out):
    def body(win_ref, wkv_ref, wout_ref, wint_o, wkv_o, wout_o, send_sems, recv_sems):
        x, y, c = _mesh_pos()
        me, sib = (x, y, c), (x, y, 1 - c)
        chips = [(1 - x, y), (x, 1 - y), (1 - x, 1 - y)]

        def rows(p):
            return wint_o.at[pl.ds(pl.multiple_of(_flat(p) * W_IN_SHARD, 16), W_IN_SHARD), :]

        rows(me)[...] = win_ref[...].astype(BF16)

        def copy(k, block, to):
            return pltpu.make_async_remote_copy(
                src_ref=rows(block), dst_ref=rows(block), send_sem=send_sems.at[k], recv_sem=recv_sems.at[k],
                device_id=to, device_id_type=MESH)

        first = [copy(0, me, sib)] + [copy(1 + j, me, (*chip, c)) for j, chip in enumerate(chips)]
        for cp in first:
            cp.start()
        wkv_o[...] = wkv_ref[...].astype(BF16)
        wout_o[...] = wout_ref[...].astype(BF16)
        passed = []
        for j, chip in enumerate(chips):
            copy(1 + j, (*chip, c), me).wait_recv()
            fwd = copy(4 + j, (*chip, c), sib)
            fwd.start()
            passed.append(fwd)
        copy(0, sib, me).wait_recv()
        for j, chip in enumerate(chips):
            copy(4 + j, (*chip, 1 - c), me).wait_recv()
        for cp in first + passed:
            cp.wait_send()

    vmem = pl.BlockSpec(memory_space=pltpu.VMEM)
    return pl.pallas_call(
        body, name="allgather_weights",
        out_shape=(SDS((IN_COLS, D_MODEL), BF16), SDS(w_kv.shape, BF16), SDS(w_out.shape, BF16)),
        in_specs=[vmem, vmem, vmem], out_specs=(vmem, vmem, vmem),
        scratch_shapes=[pltpu.SemaphoreType.DMA((7,)), pltpu.SemaphoreType.DMA((7,))],
        compiler_params=_params(40),
    )(w_in_t, w_kv, w_out)


def _proj_fwd(x2, g_norm, wint):
    tm = 256
    d = DILATIONS[2][0]
    per_ex = SEQ // tm

    def body(x_ref, g_ref, w_ref, o_ref, o16_ref):
        xv = x_ref[...]
        h = xv * _rstd(xv) * g_ref[...]
        res = _dot_nt(h, w_ref[...])
        o_ref[...] = res
        r_out = lax.broadcasted_iota(jnp.int32, (tm, tm), 0)
        r_in = lax.broadcasted_iota(jnp.int32, (tm, tm), 1)
        pick = (r_in == d * (r_out % (tm // d)) + r_out // (tm // d)).astype(BF16)
        grouped = _dot(pick, res[:, 0:QKV_W]).astype(BF16)
        for rho in range(d):
            o16_ref[0, rho] = grouped[rho * (tm // d):(rho + 1) * (tm // d), :]

    return pl.pallas_call(
        body, name="proj_fwd", grid=(T_LOC // tm,),
        in_specs=[pl.BlockSpec((tm, D_MODEL), lambda i: (i, 0)), pl.BlockSpec((1, D_MODEL), lambda i: (0, 0)),
                  pl.BlockSpec((IN_COLS, D_MODEL), lambda i: (0, 0))],
        out_specs=(pl.BlockSpec((tm, IN_COLS), lambda i: (i, 0)),
                   pl.BlockSpec((1, d, tm // d, QKV_W), lambda i: (i // per_ex, 0, i % per_ex, 0))),
        out_shape=(SDS((T_LOC, IN_COLS), F32), SDS((B_LOC, d, SEQ // d, QKV_W), BF16)),
        compiler_params=_params(48, ("arbitrary",)),
    )(x2, g_norm, wint)


def _memkv_fwd(mem2, g_mem, wkv):
    def body(m_ref, g_ref, w_ref, o_ref):
        mv = m_ref[...]
        o_ref[...] = _dot(mv * _rstd(mv) * g_ref[...], w_ref[...])

    return pl.pallas_call(
        body, name="memkv_fwd", out_shape=SDS((B_LOC * N_MEM, 2 * MEM_W), F32), compiler_params=_params(32),
    )(mem2, g_mem, wkv)


N_BIAS = 7


def _fill_bias_tables(sl_ref, tab):
    for cfg, (d, length) in enumerate(DILATIONS):
        nk = min(length, 2 * CHUNK)
        r = lax.broadcasted_iota(jnp.int32, (CHUNK, nk), 0)
        c = lax.broadcasted_iota(jnp.int32, (CHUNK, nk), 1)
        for var in range(3 if length > nk else 1):
            rel = jnp.abs(r - c + var * RADIUS)
            dist = rel.astype(F32) * float(d)
            for h in range(2):
                slope = sl_ref[0, 0:1, h * HEAD:h * HEAD + 1]
                tab[3 * cfg + var, h * CHUNK:(h + 1) * CHUNK, 0:nk] = jnp.where(rel <= RADIUS, -slope * dist, NEG)


def _attn_blocks(visit, unroll):
    def step(t, carry):
        for cfg, (d, length) in enumerate(DILATIONS):
            nblk = length // CHUNK
            if nblk == 1:
                visit(cfg, 0, t, t, length, t)
                continue
            rho, i = (0, t) if d == 1 else (t // nblk, t % nblk)
            ks = jnp.clip(i * CHUNK - RADIUS, 0, length - 2 * CHUNK)
            visit(cfg, (i * CHUNK - ks) // RADIUS, rho + d * (i * CHUNK), rho + d * ks, 2 * CHUNK, t)
        return carry
    lax.fori_loop(0, 16, step, 0, unroll=unroll)


def _stack_heads(v, left):
    return jnp.concatenate([jnp.where(left, v, 0.0), jnp.where(left, 0.0, v)], axis=0)


def _unstack_heads(v, left):
    return jnp.where(left, v[0:CHUNK], v[CHUNK:2 * CHUNK])


def _rows(start, n, d):
    return pl.ds(start, n) if d == 1 else pl.ds(start, n, stride=d)


def _blk16(col0):
    d, length = DILATIONS[2]
    return pl.BlockSpec((1, d, length, 128), lambda b, hp: (b, 0, 0, col0 // 128 + hp))


def _attn_fwd(proj, qkv16, slopes):
    def body(sl_ref, q_ref, k_ref, v_ref, q16_ref, k16_ref, v16_ref, a_ref, lse_ref, *scr):
        o_c, m_c, l_c, tab = scr[0:3], scr[3:6], scr[6:9], scr[9]
        left = _left_lanes(CHUNK)
        _fill_bias_tables(sl_ref, tab)

        def block(cfg, var, q0, k0, nk, t):
            d = DILATIONS[cfg][0]
            rq, rk = _rows(q0, CHUNK, d), _rows(k0, nk, d)
            if cfg == 2:
                qb, kw, vw = q16_ref[0, t].astype(F32), k16_ref[0, t], v16_ref[0, t]
            else:
                qb, kw, vw = q_ref[rq, :], k_ref[rk, :], v_ref[rk, :]
            qs = _stack_heads(qb * SCALE, left)
            s = _dot_nt(qs, kw) + tab[3 * cfg + var, :, 0:nk]
            m = jnp.max(s, axis=-1, keepdims=True)
            p = jnp.exp(s - m)
            o_c[cfg][rq, :] = _unstack_heads(_dot(p, vw), left)
            m_c[cfg][rq, :] = _unstack_heads(m, left)
            l_c[cfg][rq, :] = _unstack_heads(jnp.sum(p, axis=-1, keepdims=True), left)
        _attn_blocks(block, 8)

        def merge(j, carry):
            rows = pl.ds(pl.multiple_of(j * 256, 256), 256)
            ms = [m_c[i][rows, :] for i in range(3)]
            top = jnp.maximum(jnp.maximum(ms[0], ms[1]), ms[2])
            ws = [jnp.exp(m - top) for m in ms]
            den = l_c[0][rows, :] * ws[0] + l_c[1][rows, :] * ws[1] + l_c[2][rows, :] * ws[2]
            num = o_c[0][rows, :] * ws[0] + o_c[1][rows, :] * ws[1] + o_c[2][rows, :] * ws[2]
            a_ref[rows, :] = num / den
            lse_ref[rows, :] = top + jnp.log(den)
            return carry
        lax.fori_loop(0, SEQ // 256, merge, 0)

    blk = lambda col0: pl.BlockSpec((SEQ, 128), lambda b, hp: (b, col0 // 128 + hp))
    out = pl.BlockSpec((SEQ, 128), lambda b, hp: (b, hp))
    return pl.pallas_call(
        body, name="attn_fwd", grid=(B_LOC, 4),
        in_specs=[pl.BlockSpec((1, 8, 128), lambda b, hp: (hp, 0, 0)), blk(C_QA), blk(C_KA), blk(C_VA),
                  _blk16(C_QA), _blk16(C_KA), _blk16(C_VA)],
        out_specs=(out, out),
        out_shape=(SDS((T_LOC, ATTN_W), F32), SDS((T_LOC, ATTN_W), F32)),
        scratch_shapes=[pltpu.VMEM((SEQ, 128), F32)] * 9 + [pltpu.VMEM((N_BIAS, 2 * CHUNK, 2 * CHUNK), F32)],
        compiler_params=_params(40, ("arbitrary", "arbitrary")),
    )(slopes, proj, proj, proj, qkv16, qkv16, qkv16)


def _chunks_side_by_side(v, pr, tm):
    return jnp.concatenate([v[ch * CHUNK:(ch + 1) * CHUNK, pr * 128:(pr + 1) * 128] for ch in range(tm // CHUNK)], axis=1)


def _first_group_lanes(tm):
    return lax.broadcasted_iota(jnp.int32, (CHUNK, tm), 1) % 128 < HEAD


def _store_chunks(dst_ref, pr, val, tm):
    for ch in range(tm // CHUNK):
        dst_ref[ch * CHUNK:(ch + 1) * CHUNK, pr * 128:(pr + 1) * 128] = val[:, ch * CHUNK:(ch + 1) * CHUNK]


def _sgu_mix(vn, ws_ref, dst_ref, tm):
    first = _first_group_lanes(tm)
    for pr in range(2):
        vp = _chunks_side_by_side(vn, pr, tm)
        _store_chunks(dst_ref, pr, jnp.where(first, _dot(ws_ref[2 * pr], vp), _dot(ws_ref[2 * pr + 1], vp)), tm)


def _mem_head_of_lane(rows):
    return lax.broadcasted_iota(jnp.int32, (rows, MEM_W), 1) // HEAD


def _stack_mem_heads(v, rows):
    head = _mem_head_of_lane(rows)
    return jnp.concatenate([jnp.where(head == h, v, 0.0) for h in range(4)], axis=0)


def _unstack_mem_heads(v, rows):
    head = _mem_head_of_lane(rows)
    out = v[0:rows]
    for h in range(1, 4):
        out = jnp.where(head == h, v[h * rows:(h + 1) * rows], out)
    return out


def _mem_attn_probs(q, kmem, rows):
    qs = _stack_mem_heads(q, rows).astype(BF16)
    s = _dot_nt(qs, kmem) * SCALE
    e = jnp.exp(s - jnp.max(s, axis=-1, keepdims=True))
    return e * (1.0 / jnp.sum(e, axis=-1, keepdims=True)), qs


def _branch_blocks(tm):
    col = lambda w, c0: pl.BlockSpec((tm, w), lambda i: (i, c0 // w))
    return [col(512, C_ZA), col(256, C_UB), col(256, C_VB), col(256, C_ZB), col(256, C_QM), col(256, C_ZM)]


def _branch_fwd(proj, a, kv, w_s, b_exp, g_v):
    tm = 256
    per_ex = SEQ // tm

    def body(za_ref, ub_ref, vb_ref, zb_ref, qm_ref, zm_ref, a_ref, kv_ref, ws_ref, be_ref, gv_ref, o_ref, mix):
        o_ref[:, 0:ATTN_W] = (_silu_and_grad(za_ref[...])[0] * a_ref[...]).astype(BF16)
        gu = _gelu_and_grad(ub_ref[...])[0]
        gv = _gelu_and_grad(vb_ref[...])[0]
        vn = gv * _rstd(gv) * gv_ref[...]
        _sgu_mix(vn.astype(BF16), ws_ref, mix, tm)
        sg = gu * (mix[...] + be_ref[...])
        o_ref[:, ATTN_W:ATTN_W + SGU_W] = (_silu_and_grad(zb_ref[...])[0] * sg).astype(BF16)
        p = _mem_attn_probs(qm_ref[...], kv_ref[:, 0:MEM_W], tm)[0]
        mo = _unstack_mem_heads(_dot(p, kv_ref[:, MEM_W:2 * MEM_W]), tm)
        o_ref[:, ATTN_W + SGU_W:D_MODEL] = (_silu_and_grad(zm_ref[...])[0] * mo).astype(BF16)

    full = lambda shape: pl.BlockSpec(shape, lambda i: (0,) * len(shape))
    return pl.pallas_call(
        body, name="branch_fwd", grid=(T_LOC // tm,),
        in_specs=_branch_blocks(tm) + [
            pl.BlockSpec((tm, ATTN_W), lambda i: (i, 0)), pl.BlockSpec((N_MEM, 2 * MEM_W), lambda i: (i // per_ex, 0)),
            full((4, CHUNK, CHUNK)), full((tm, SGU_W)), full((1, SGU_W))],
        out_specs=pl.BlockSpec((tm, D_MODEL), lambda i: (i, 0)),
        out_shape=SDS((T_LOC, D_MODEL), BF16),
        scratch_shapes=[pltpu.VMEM((tm, SGU_W), F32)],
        compiler_params=_params(VMEM_NO_STAGING_MIB, ("arbitrary",)),
    )(proj, proj, proj, proj, proj, proj, a, kv, w_s, b_exp, g_v)


def _outproj_loss(gated, wout, x2, tgt2, g_final):
    tm = 512

    def body(g_ref, w_ref, x_ref, t_ref, gf_ref, dh2_ref, loss_ref, dgf_ref, dwo_ref, dwo_acc):
        @pl.when(pl.program_id(0) == 0)
        def _():
            loss_ref[...] = jnp.zeros_like(loss_ref)
            dgf_ref[...] = jnp.zeros_like(dgf_ref)
            dwo_acc[...] = jnp.zeros_like(dwo_acc)
        gated = g_ref[...]
        h2 = x_ref[...] + _dot(gated, w_ref[...])
        r = _rstd(h2)
        gf = gf_ref[...]
        err = h2 * r * gf - t_ref[...]
        loss_ref[...] += 0.5 * jnp.sum(jnp.mean(err * err, axis=-1, keepdims=True))
        dy = err * (1.0 / D_MODEL)
        dh2 = _rms_bwd(h2, r, gf, dy)
        dh2_ref[...] = dh2
        dgf_ref[...] += jnp.sum(dy * (h2 * r), axis=0, keepdims=True)
        dwo_acc[...] += _dot_tn(gated, dh2)

        @pl.when(pl.program_id(0) == pl.num_programs(0) - 1)
        def _():
            _cast_rows(dwo_acc, dwo_ref, D_MODEL)

    row = pl.BlockSpec((tm, D_MODEL), lambda i: (i, 0))
    vec = pl.BlockSpec((1, D_MODEL), lambda i: (0, 0))
    square = pl.BlockSpec((D_MODEL, D_MODEL), lambda i: (0, 0))
    return pl.pallas_call(
        body, name="outproj_loss", grid=(T_LOC // tm,),
        in_specs=[row, square, row, row, vec],
        out_specs=(row, pl.BlockSpec((8, 128), lambda i: (0, 0)), vec, square),
        out_shape=(SDS((T_LOC, D_MODEL), F32), SDS((8, 128), F32), SDS((1, D_MODEL), F32), SDS((D_MODEL, D_MODEL), BF16)),
        scratch_shapes=[pltpu.VMEM((D_MODEL, D_MODEL), F32)],
        compiler_params=_params(VMEM_NO_STAGING_MIB, ("arbitrary",)),
    )(gated, wout, x2, tgt2, g_final)


def _branch_bwd(dh2, wout, proj, a, kv, w_s, b_exp, g_v):
    tm = 256
    per_ex = SEQ // tm

    def body(dh2_ref, w_ref, za_ref, ub_ref, vb_ref, zb_ref, qm_ref, zm_ref, a_ref, kv_ref, ws_ref,
             be_ref, gv_ref, da_ref, dr_ref, dkv_ref, dws_ref, db_ref, dgv_ref, mix, dvn, dmsum):
        i = pl.program_id(0)

        @pl.when(i == 0)
        def _():
            dws_ref[...] = jnp.zeros_like(dws_ref)
            dgv_ref[...] = jnp.zeros_like(dgv_ref)
            dmsum[...] = jnp.zeros_like(dmsum)

        @pl.when(i % per_ex == 0)
        def _():
            dkv_ref[...] = jnp.zeros_like(dkv_ref)

        dg = _dot_nt(dh2_ref[...], w_ref[...])

        sa, dsa = _silu_and_grad(za_ref[...])
        dga = dg[:, 0:ATTN_W]
        da_ref[...] = dga * sa
        dr_ref[:, 0:512] = (dga * a_ref[...] * dsa).astype(BF16)

        ub, vb = ub_ref[...], vb_ref[...]
        gu, dgu = _gelu_and_grad(ub)
        gv, dgv = _gelu_and_grad(vb)
        rv = _rstd(gv)
        gain = gv_ref[...]
        vn = (gv * rv * gain).astype(BF16)
        _sgu_mix(vn, ws_ref, mix, tm)
        mixed = mix[...] + be_ref[...]
        sb, dsb = _silu_and_grad(zb_ref[...])
        dgb = dg[:, ATTN_W:ATTN_W + SGU_W]
        dsg = dgb * sb
        dr_ref[:, 512:768] = (dsg * mixed * dgu).astype(BF16)
        dr_ref[:, 1024:1280] = (dgb * (gu * mixed) * dsb).astype(BF16)
        dmix = dsg * gu
        for ch in range(tm // CHUNK):
            dmsum[...] += dmix[ch * CHUNK:(ch + 1) * CHUNK, :]
        first = _first_group_lanes(tm)
        for pr in range(2):
            dmp, vp = _chunks_side_by_side(dmix, pr, tm), _chunks_side_by_side(vn, pr, tm)
            dws_ref[2 * pr] += _dot_nt(jnp.where(first, dmp, 0.0), vp)
            dws_ref[2 * pr + 1] += _dot_nt(jnp.where(first, 0.0, dmp), vp)
            _store_chunks(dvn, pr, jnp.where(first, _dot_tn(ws_ref[2 * pr], dmp), _dot_tn(ws_ref[2 * pr + 1], dmp)), tm)
        dvn_v = dvn[...]
        dgv_ref[...] += jnp.sum(dvn_v * (gv * rv), axis=0, keepdims=True)
        dr_ref[:, 768:1024] = (_rms_bwd(gv, rv, gain, dvn_v) * dgv).astype(BF16)

        szm, dszm = _silu_and_grad(zm_ref[...])
        dgm = dg[:, ATTN_W + SGU_W:D_MODEL]
        kmem, vmem_ = kv_ref[:, 0:MEM_W].astype(BF16), kv_ref[:, MEM_W:2 * MEM_W].astype(BF16)
        p, qs = _mem_attn_probs(qm_ref[...], kmem, tm)
        dmos = _stack_mem_heads(dgm * szm, tm).astype(BF16)
        dr_ref[:, 1536:1792] = (dgm * _unstack_mem_heads(_dot(p, vmem_), tm) * dszm).astype(BF16)
        dp = _dot_nt(dmos, vmem_)
        ds = (p * (dp - jnp.sum(p * dp, axis=-1, keepdims=True)) * SCALE).astype(BF16)
        dr_ref[:, 1280:1536] = _unstack_mem_heads(_dot(ds, kmem), tm).astype(BF16)
        dkv_ref[:, 0:MEM_W] += _dot_tn(ds, qs)
        dkv_ref[:, MEM_W:2 * MEM_W] += _dot_tn(p, dmos)

        @pl.when(i == pl.num_programs(0) - 1)
        def _():
            tot = dmsum[...]
            hi = tot.astype(BF16)
            lo = (tot - hi.astype(F32)).astype(BF16)
            grp = (lax.broadcasted_iota(jnp.int32, (SGU_W, 128), 0) // HEAD
                   == lax.broadcasted_iota(jnp.int32, (SGU_W, 128), 1)).astype(BF16)
            db_ref[...] = (_dot(hi, grp) + _dot(lo, grp)).T[0:4, :]

    full = lambda shape: pl.BlockSpec(shape, lambda i: (0,) * len(shape))
    row = lambda w: pl.BlockSpec((tm, w), lambda i: (i, 0))
    return pl.pallas_call(
        body, name="branch_bwd", grid=(T_LOC // tm,),
        in_specs=[row(D_MODEL), full((D_MODEL, D_MODEL))] + _branch_blocks(tm) + [
            row(ATTN_W), pl.BlockSpec((N_MEM, 2 * MEM_W), lambda i: (i // per_ex, 0)),
            full((4, CHUNK, CHUNK)), full((tm, SGU_W)), full((1, SGU_W))],
        out_specs=(row(ATTN_W), row(REST_W), pl.BlockSpec((N_MEM, 2 * MEM_W), lambda i: (i // per_ex, 0)),
                   full((4, CHUNK, CHUNK)), full((4, CHUNK)), full((1, SGU_W))),
        out_shape=(SDS((T_LOC, ATTN_W), F32), SDS((T_LOC, REST_W), BF16), SDS((B_LOC * N_MEM, 2 * MEM_W), F32),
                   SDS((4, CHUNK, CHUNK), F32), SDS((4, CHUNK), F32), SDS((1, SGU_W), F32)),
        scratch_shapes=[pltpu.VMEM((tm, SGU_W), F32), pltpu.VMEM((tm, SGU_W), F32), pltpu.VMEM((CHUNK, SGU_W), F32)],
        compiler_params=_params(56, ("arbitrary",)),
    )(dh2, wout, proj, proj, proj, proj, proj, proj, a, kv, w_s, b_exp, g_v)


def _attn_bwd(proj, qkv16, slopes, da, a, lse):
    def body(sl_ref, q_ref, k_ref, v_ref, q16_ref, k16_ref, v16_ref, da_ref, a_ref, lse_ref,
             dq_ref, dk_ref, dv_ref, *scr):
        dq_s, dk_s, dv_s, tab = scr[0:3], scr[3:6], scr[6:9], scr[9]
        lse_h, delta_h = scr[10:12], scr[12:14]
        p_all, ds_all = scr[14], scr[15]
        left = _left_lanes(CHUNK)
        _fill_bias_tables(sl_ref, tab)

        def prep(j, carry):
            rows = pl.ds(pl.multiple_of(j * 256, 256), 256)
            l256 = _left_lanes(256)
            prod = da_ref[rows, :] * a_ref[rows, :]
            delta_h[0][rows, :] = jnp.broadcast_to(jnp.sum(jnp.where(l256, prod, 0.0), axis=-1, keepdims=True), (256, 128))
            delta_h[1][rows, :] = jnp.broadcast_to(jnp.sum(jnp.where(l256, 0.0, prod), axis=-1, keepdims=True), (256, 128))
            pair = lse_ref[rows, :]
            other = pltpu.roll(pair, HEAD, axis=1)
            lse_h[0][rows, :] = jnp.where(l256, pair, other)
            lse_h[1][rows, :] = jnp.where(l256, other, pair)
            zero = jnp.zeros((256, 128), F32)
            for cfg in range(3):
                dk_s[cfg][rows, :] = zero
                dv_s[cfg][rows, :] = zero
            return carry
        lax.fori_loop(0, SEQ // 256, prep, 0)

        def per_row(halves, rq, nk):
            v = jnp.concatenate([halves[0][rq, :], halves[1][rq, :]], axis=0)
            return v if nk == 128 else jnp.concatenate([v, v], axis=1)

        def qkv(cfg, rq, rk, t):
            if cfg == 2:
                return q16_ref[0, t].astype(F32), k16_ref[0, t], v16_ref[0, t]
            return q_ref[rq, :], k_ref[rk, :], v_ref[rk, :]

        def probs(cfg, var, q0, k0, nk, t):
            d = DILATIONS[cfg][0]
            rq, rk = _rows(q0, CHUNK, d), _rows(k0, nk, d)
            qb, kw, vw = qkv(cfg, rq, rk, t)
            qs = _stack_heads(qb * SCALE, left)
            das = _stack_heads(da_ref[rq, :], left)
            s = _dot_nt(qs, kw) + tab[3 * cfg + var, :, 0:nk]
            p = jnp.exp(s - per_row(lse_h, rq, nk))
            p_all[16 * cfg + t, :, 0:nk] = p.astype(BF16)
            ds_all[16 * cfg + t, :, 0:nk] = (p * (_dot_nt(das, vw) - per_row(delta_h, rq, nk))).astype(BF16)
        _attn_blocks(probs, 4)

        def grads(cfg, var, q0, k0, nk, t):
            d = DILATIONS[cfg][0]
            rq, rk = _rows(q0, CHUNK, d), _rows(k0, nk, d)
            qb, kw, _ = qkv(cfg, rq, rk, t)
            qs = _stack_heads(qb * SCALE, left).astype(BF16)
            das = _stack_heads(da_ref[rq, :], left).astype(BF16)
            p, ds = p_all[16 * cfg + t, :, 0:nk], ds_all[16 * cfg + t, :, 0:nk]
            dq_s[cfg][rq, :] = _unstack_heads(_dot(ds, kw), left) * SCALE
            dk_s[cfg][rk, :] += _dot_tn(ds, qs)
            dv_s[cfg][rk, :] += _dot_tn(p, das)
        _attn_blocks(grads, 4)

        def flush(j, carry):
            rows = pl.ds(pl.multiple_of(j * 256, 256), 256)
            for acc, dst in ((dq_s, dq_ref), (dk_s, dk_ref), (dv_s, dv_ref)):
                dst[rows, :] = (acc[0][rows, :] + acc[1][rows, :] + acc[2][rows, :]).astype(BF16)
            return carry
        lax.fori_loop(0, SEQ // 256, flush, 0)

    blk = lambda col0: pl.BlockSpec((SEQ, 128), lambda b, hp: (b, col0 // 128 + hp))
    own = pl.BlockSpec((SEQ, 128), lambda b, hp: (b, hp))
    return pl.pallas_call(
        body, name="attn_bwd", grid=(B_LOC, 4),
        in_specs=[pl.BlockSpec((1, 8, 128), lambda b, hp: (hp, 0, 0)), blk(C_QA), blk(C_KA), blk(C_VA),
                  _blk16(C_QA), _blk16(C_KA), _blk16(C_VA), own, own, own],
        out_specs=(own, own, own),
        out_shape=(SDS((T_LOC, ATTN_W), BF16),) * 3,
        scratch_shapes=[pltpu.VMEM((SEQ, 128), F32)] * 9 + [pltpu.VMEM((N_BIAS, 2 * CHUNK, 2 * CHUNK), F32)]
        + [pltpu.VMEM((SEQ, 128), F32)] * 4 + [pltpu.VMEM((48, 2 * CHUNK, 2 * CHUNK), BF16)] * 2,
        compiler_params=_params(52, ("arbitrary", "arbitrary")),
    )(slopes, proj, proj, proj, qkv16, qkv16, qkv16, da, a, lse)


def _dproj_specs(tm):
    third = pl.BlockSpec((tm, ATTN_W), lambda i: (i, 0))
    return [third, third, third, pl.BlockSpec((tm, REST_W), lambda i: (i, 0))]


def _dx(dq, dk, dv, dr, wint, x2, dh2, g_norm):
    tm = 256

    def body(dq_ref, dk_ref, dv_ref, dr_ref, w_ref, x_ref, dh2_ref, g_ref, gx_ref, dgn_ref):
        @pl.when(pl.program_id(0) == 0)
        def _():
            dgn_ref[...] = jnp.zeros_like(dgn_ref)
        dh = (_dot(dq_ref[...], w_ref[C_QA:C_KA, :]) + _dot(dk_ref[...], w_ref[C_KA:C_VA, :])
              + _dot(dv_ref[...], w_ref[C_VA:C_ZA, :]) + _dot(dr_ref[...], w_ref[C_ZA:IN_COLS, :]))
        xv = x_ref[...]
        r = _rstd(xv)
        gx_ref[...] = dh2_ref[...] + _rms_bwd(xv, r, g_ref[...], dh)
        dgn_ref[...] += jnp.sum(dh * (xv * r), axis=0, keepdims=True)

    row = pl.BlockSpec((tm, D_MODEL), lambda i: (i, 0))
    vec = pl.BlockSpec((1, D_MODEL), lambda i: (0, 0))
    return pl.pallas_call(
        body, name="dx", grid=(T_LOC // tm,),
        in_specs=_dproj_specs(tm) + [pl.BlockSpec((IN_COLS, D_MODEL), lambda i: (0, 0)), row, row, vec],
        out_specs=(row, vec),
        out_shape=(SDS((T_LOC, D_MODEL), F32), SDS((1, D_MODEL), F32)),
        compiler_params=_params(48, ("arbitrary",)),
    )(dq, dk, dv, dr, wint, x2, dh2, g_norm)


def _dwin(dq, dk, dv, dr, x2, g_norm, half):
    tm = 512
    width = D_MODEL // 2
    cols = slice(half * width, (half + 1) * width)

    def body(dq_ref, dk_ref, dv_ref, dr_ref, x_ref, g_ref, o_ref, acc):
        @pl.when(pl.program_id(0) == 0)
        def _():
            acc[...] = jnp.zeros_like(acc)
        xv = x_ref[...]
        h = (xv[:, cols] * _rstd(xv) * g_ref[:, cols]).astype(BF16)
        acc[C_QA:C_KA, :] += _dot_tn(dq_ref[...], h)
        acc[C_KA:C_VA, :] += _dot_tn(dk_ref[...], h)
        acc[C_VA:C_ZA, :] += _dot_tn(dv_ref[...], h)
        acc[C_ZA:IN_COLS, :] += _dot_tn(dr_ref[...], h)

        @pl.when(pl.program_id(0) == pl.num_programs(0) - 1)
        def _():
            _cast_rows(acc, o_ref, IN_COLS)

    return pl.pallas_call(
        body, name="dwin%d" % half, grid=(T_LOC // tm,),
        in_specs=_dproj_specs(tm) + [pl.BlockSpec((tm, D_MODEL), lambda i: (i, 0)), pl.BlockSpec((1, D_MODEL), lambda i: (0, 0))],
        out_specs=pl.BlockSpec((IN_COLS, width), lambda i: (0, 0)),
        out_shape=SDS((IN_COLS, width), BF16),
        scratch_shapes=[pltpu.VMEM((IN_COLS, width), F32)],
        compiler_params=_params(48, ("arbitrary",)),
    )(dq, dk, dv, dr, x2, g_norm)


def _memkv_bwd(dkv, mem2, g_mem, wkv):
    def body(dkv_ref, m_ref, g_ref, w_ref, dw_ref, dg_ref):
        mv = m_ref[...]
        r = _rstd(mv)
        dkv_v = dkv_ref[...].astype(BF16)
        dw_ref[...] = _dot_tn(mv * r * g_ref[...], dkv_v).astype(BF16)
        dg_ref[...] = jnp.sum(_dot_nt(dkv_v, w_ref[...]) * (mv * r), axis=0, keepdims=True)

    return pl.pallas_call(
        body, name="memkv_bwd", out_shape=(SDS((D_MODEL, 2 * MEM_W), BF16), SDS((1, D_MODEL), F32)),
        compiler_params=_params(32),
    )(dkv, mem2, g_mem, wkv)


def _allreduce_small(parts):
    n = len(parts)

    def body(*refs):
        ins, outs, bufs = refs[0:n], refs[n:2 * n], refs[2 * n:3 * n]
        send_sems, recv_sems = refs[3 * n], refs[3 * n + 1]
        pos = _mesh_pos()
        me = _flat(pos)
        for a in range(n):
            bufs[a][me] = ins[a][...]

        def copy(a, k, slot):
            return pltpu.make_async_remote_copy(
                src_ref=ins[a], dst_ref=bufs[a].at[slot],
                send_sem=send_sems.at[7 * a + k - 1], recv_sem=recv_sems.at[7 * a + k - 1],
                device_id=_peer(pos, k), device_id_type=MESH)

        sent = [copy(a, k, me) for a in range(n) for k in range(1, N_DEV)]
        for cp in sent:
            cp.start()
        for a in range(n):
            for k in range(1, N_DEV):
                copy(a, k, _flat(_peer(pos, k))).wait_recv()
        for cp in sent:
            cp.wait_send()
        for a in range(n):
            acc = bufs[a][0]
            for s in range(1, N_DEV):
                acc = acc + bufs[a][s]
            outs[a][...] = acc

    vmem = pl.BlockSpec(memory_space=pltpu.VMEM)
    return pl.pallas_call(
        body, name="allreduce_small",
        out_shape=tuple(SDS(p.shape, F32) for p in parts),
        in_specs=[vmem] * n, out_specs=(vmem,) * n,
        scratch_shapes=[pltpu.VMEM((N_DEV,) + p.shape, F32) for p in parts]
        + [pltpu.SemaphoreType.DMA((7 * n,)), pltpu.SemaphoreType.DMA((7 * n,))],
        compiler_params=_params(16),
    )(*parts)


_HBM = pl.BlockSpec(memory_space=pltpu.HBM)
_SEM = pl.BlockSpec(memory_space=pltpu.SEMAPHORE)
_SIDE_EFFECT = pltpu.SideEffectType.DATAFLOW_SIDE_EFFECTING


def _exchange_copies(src_refs, land_refs, scatter, send_sems, recv_sems):
    pos = _mesh_pos()
    copies = []
    for a, (src, land) in enumerate(zip(src_refs, land_refs)):
        n = land.shape[1]
        for k in range(1, N_DEV):
            peer = _peer(pos, k)
            piece = src.at[pl.ds(pl.multiple_of(_flat(peer) * n, 16), n), :] if scatter[a] else src
            copies.append(pltpu.make_async_remote_copy(
                src_ref=piece, dst_ref=land.at[_flat(pos)],
                send_sem=send_sems.at[7 * a + k - 1], recv_sem=recv_sems.at[7 * a + k - 1],
                device_id=peer, device_id_type=MESH))
    return copies


def _exchange_start(name, srcs, scatter, lands):
    n = len(srcs)

    def body(*refs):
        for cp in _exchange_copies(refs[0:n], refs[n:2 * n], scatter, refs[2 * n], refs[2 * n + 1]):
            cp.start()
        refs[-1][...] = jnp.zeros_like(refs[-1])

    ops = [pltpu.with_memory_space_constraint(t, pltpu.HBM) for t in (*srcs, *lands)]
    out = pl.pallas_call(
        body, name=name,
        out_shape=(pltpu.SemaphoreType.DMA((7 * n,)), pltpu.SemaphoreType.DMA((7 * n,)),
                   *[pltpu.HBM(t.shape, t.dtype) for t in ops], SDS((8, 128), F32)),
        in_specs=[_HBM] * (2 * n),
        out_specs=(_SEM, _SEM, *[_HBM] * (2 * n), pl.BlockSpec(memory_space=pltpu.VMEM)),
        input_output_aliases={i: 2 + i for i in range(2 * n)},
        compiler_params=pltpu.CompilerParams(has_side_effects=_SIDE_EFFECT),
    )(*ops)
    return out[0], out[1], out[2:2 + n], out[2 + n:2 + 2 * n], out[-1]


def _exchange_wait(name, started, scatter, after):
    send_sems, recv_sems, srcs, lands, _ = started
    n = len(srcs)

    def body(*refs):
        for cp in _exchange_copies(refs[0:n], refs[n:2 * n], scatter, refs[2 * n], refs[2 * n + 1]):
            cp.wait_send()
            cp.wait_recv()

    out = pl.pallas_call(
        body, name=name,
        out_shape=tuple(pltpu.HBM(t.shape, t.dtype) for t in (*srcs, *lands)),
        in_specs=[_HBM] * (2 * n) + [_SEM, _SEM, pl.BlockSpec(memory_space=pl.ANY)],
        out_specs=(_HBM,) * (2 * n),
        input_output_aliases={i: i for i in range(2 * n)},
        compiler_params=pltpu.CompilerParams(has_side_effects=_SIDE_EFFECT),
    )(*srcs, *lands, send_sems, recv_sems, after)
    return out[n:]


def _landing(own, me):
    return lax.dynamic_update_slice(lax.empty((N_DEV,) + own.shape, own.dtype), own[None], (me,) + (0,) * own.ndim)


def _adamw(w, g, m, v):
    m = ADAM_B1 * m + (1.0 - ADAM_B1) * g
    v = ADAM_B2 * v + (1.0 - ADAM_B2) * (g * g)
    m_hat = m / (1.0 - ADAM_B1 ** ADAM_STEP)
    v_hat = v / (1.0 - ADAM_B2 ** ADAM_STEP)
    return -ADAM_LR * (m_hat / (jnp.sqrt(v_hat) + ADAM_EPS) + ADAM_WD * w), m, v


def _adam_slots(name, pieces, w, m, v):
    rows, cols = w.shape
    starts = [sum(p.shape[2] for p in pieces[:i]) for i in range(len(pieces) + 1)]
    assert starts[-1] == cols and all(p.shape[1] == rows for p in pieces)

    def body(*refs):
        s_refs, (w_ref, m_ref, v_ref, g_o, d_o, m_o, v_o, acc) = refs[:len(pieces)], refs[len(pieces):]
        s = pl.program_id(0)

        @pl.when(s == 0)
        def _():
            for i, s_ref in enumerate(s_refs):
                acc[:, starts[i]:starts[i + 1]] = s_ref[0].astype(F32)

        @pl.when(s > 0)
        def _():
            for i, s_ref in enumerate(s_refs):
                acc[:, starts[i]:starts[i + 1]] += s_ref[0].astype(F32)

        @pl.when(s == N_DEV - 1)
        def _():
            g = acc[...]
            g_o[...] = g
            d_o[...], m_o[...], v_o[...] = _adamw(w_ref[...], g, m_ref[...], v_ref[...])

    full = pl.BlockSpec((rows, cols), lambda s: (0, 0))
    return pl.pallas_call(
        body, name="adam_" + name, grid=(N_DEV,),
        in_specs=[pl.BlockSpec((1, rows, p.shape[2]), lambda s: (s, 0, 0)) for p in pieces] + [full, full, full],
        out_specs=(full,) * 4, out_shape=(SDS((rows, cols), F32),) * 4,
        scratch_shapes=[pltpu.VMEM((rows, cols), F32)],
        compiler_params=_params(40, ("arbitrary",)),
    )(*pieces, w, m, v)


def _adam_small(ws, gs, ms, vs, loss_slots):
    n = len(ws)

    def total(ref, like):
        if len(ref.shape) == len(like.shape):
            return ref[...]
        acc = ref[0]
        for s in range(1, N_DEV):
            acc = acc + ref[s]
        return acc

    def body(*refs):
        w_r, g_r, m_r, v_r = refs[0:n], refs[n:2 * n], refs[2 * n:3 * n], refs[3 * n:4 * n]
        loss_r, outs = refs[4 * n], refs[4 * n + 1:]
        for a in range(n):
            g = total(g_r[a], w_r[a])
            outs[a][...] = g
            outs[n + 1 + 3 * a][...], outs[n + 2 + 3 * a][...], outs[n + 3 + 3 * a][...] = _adamw(
                w_r[a][...], g, m_r[a][...], v_r[a][...])
        outs[n][...] = total(loss_r, outs[n])

    out = pl.pallas_call(
        body, name="adam_small",
        out_shape=tuple(SDS(w.shape, F32) for w in ws) + (SDS(loss_slots.shape[1:], F32),)
        + tuple(SDS(w.shape, F32) for w in ws for _ in range(3)),
        compiler_params=_params(16),
    )(*ws, *gs, *ms, *vs, loss_slots)
    return out[0:n], out[n], out[n + 1:]


def kernel(x, mem, g_norm, w_in, w_sgu_spatial, b_sgu_spatial, g_sgu_v, g_mem, w_mem_kv, w_out, g_final, loss_target, m_g_norm, m_w_in, m_w_sgu_spatial, m_b_sgu_spatial, m_g_sgu_v, m_g_mem, m_w_mem_kv, m_w_out, m_g_final, v_g_norm, v_w_in, v_w_sgu_spatial, v_b_sgu_spatial, v_g_sgu_v, v_g_mem, v_w_mem_kv, v_w_out, v_g_final):
    x2 = x.reshape(T_LOC, D_MODEL)
    tgt2 = loss_target.reshape(T_LOC, D_MODEL)
    mem2 = mem.reshape(B_LOC * N_MEM, D_MODEL)
    w_s = w_sgu_spatial[0]
    b_exp = jnp.tile(jnp.repeat(b_sgu_spatial[0].T, HEAD, axis=1), (2, 1))
    slope = jnp.power(2.0, -8.0 * (jnp.arange(8, dtype=F32) + 1.0) / 8)
    slopes = jnp.broadcast_to(jnp.repeat(slope.reshape(4, 2), HEAD, axis=1)[:, None, :], (4, 8, 128))

    tr = lambda t: jnp.transpose(t[0])

    me = _flat(_mesh_pos())
    own_rows = lambda t: lax.dynamic_slice_in_dim(t, me * (t.shape[0] // N_DEV), t.shape[0] // N_DEV)

    wint, wkv_own, wout_own = _allgather_weights(tr(w_in), w_mem_kv[0], w_out[0])
    started0 = _exchange_start("exchange0_start", [wkv_own, wout_own], [False, False],
                               [_landing(wkv_own, me), _landing(wout_own, me)])
    proj, qkv16 = _proj_fwd(x2, g_norm + started0[4][0:1, 0:1], wint)
    wkv, wout = _exchange_wait("exchange0_wait", started0, [False, False], proj)
    wkv, wout = wkv.reshape(D_MODEL, 2 * MEM_W), wout.reshape(D_MODEL, D_MODEL)
    kv = _memkv_fwd(mem2, g_mem, wkv)
    a, lse = _attn_fwd(proj, qkv16, slopes)
    gated = _branch_fwd(proj, a, kv, w_s, b_exp, g_sgu_v)
    dh2, loss8, dgf, dwout = _outproj_loss(gated, wout, x2, tgt2, g_final.reshape(1, D_MODEL))

    da, dr, dkv, dws, dbs, dgv = _branch_bwd(dh2, wout, proj, a, kv, w_s, b_exp, g_sgu_v)
    dwkv, dgm = _memkv_bwd(dkv, mem2, g_mem, wkv)

    early = [dws.reshape(4 * CHUNK, CHUNK), dbs, dgv, dgm, dgf, loss8]
    scatter1 = [True, True] + [False] * len(early)
    started1 = _exchange_start(
        "exchange1_start", [dwkv, dwout] + early, scatter1,
        [_landing(own_rows(dwkv), me), _landing(own_rows(dwout), me)] + [_landing(t, me) for t in early])
    dq, dk, dv = _attn_bwd(proj, qkv16, slopes + started1[4][0:1, 0:1], da, a, lse)
    s_wkv, s_wout, s_ws, s_bs, s_gv, s_gm, s_gf, s_loss = _exchange_wait("exchange1_wait", started1, scatter1, dq)

    dwint0 = _dwin(dq, dk, dv, dr, x2, g_norm, 0)
    started2 = _exchange_start("exchange2_start", [dwint0], [True], [_landing(own_rows(dwint0), me)])
    dwint1 = _dwin(dq, dk, dv, dr, x2, g_norm + started2[4][0:1, 0:1], 1)
    started3 = _exchange_start("exchange3_start", [dwint1], [True], [_landing(own_rows(dwint1), me)])
    grad_x, dgn = _dx(dq, dk, dv, dr, wint, x2, dh2, g_norm + started3[4][0:1, 0:1])
    s_win0, = _exchange_wait("exchange2_wait", started2, [True], grad_x)
    dgn_sum, = _allreduce_small([dgn])
    s_win1, = _exchange_wait("exchange3_wait", started3, [True], dgn_sum)

    g_win, d_win, m_win, v_win = map(
        jnp.transpose, _adam_slots("w_in", [s_win0, s_win1], tr(w_in), tr(m_w_in), tr(v_w_in)))
    g_wkv, d_wkv, m_wkv, v_wkv = _adam_slots("w_mem_kv", [s_wkv], w_mem_kv[0], m_w_mem_kv[0], v_w_mem_kv[0])
    g_wout, d_wout, m_wout, v_wout = _adam_slots("w_out", [s_wout], w_out[0], m_w_out[0], v_w_out[0])

    small_shapes = [(1, D_MODEL), (4 * CHUNK, CHUNK), (4, CHUNK), (1, SGU_W), (1, D_MODEL), (1, D_MODEL)]
    pack = lambda arrs: [t.reshape(s) for t, s in zip(arrs, small_shapes)]
    g_small, loss_sum, upd = _adam_small(
        pack([g_norm, w_sgu_spatial, b_sgu_spatial, g_sgu_v, g_mem, g_final]),
        [dgn_sum, s_ws, s_bs, s_gv, s_gm, s_gf],
        pack([m_g_norm, m_w_sgu_spatial, m_b_sgu_spatial, m_g_sgu_v, m_g_mem, m_g_final]),
        pack([v_g_norm, v_w_sgu_spatial, v_b_sgu_spatial, v_g_sgu_v, v_g_mem, v_g_final]), s_loss)
    out_shapes = [g_norm.shape, w_sgu_spatial.shape, b_sgu_spatial.shape, g_sgu_v.shape, g_mem.shape, g_final.shape]
    unpack = lambda arrs: [t.reshape(s) for t, s in zip(arrs, out_shapes)]
    gs = unpack(g_small)
    ds, nms, nvs = unpack(upd[0::3]), unpack(upd[1::3]), unpack(upd[2::3])

    loss = loss_sum[0, 0]

    def assemble(small, win, wkv_, wout_):
        return [small[0], win[None], small[1], small[2], small[3], small[4], wkv_[None], wout_[None], small[5]]

    return (loss, grad_x.reshape(x.shape),
            *assemble(gs, g_win, g_wkv, g_wout), *assemble(ds, d_win, d_wkv, d_wout),
            *assemble(nms, m_win, m_wkv, m_wout), *assemble(nvs, v_win, v_wkv, v_wout))
```

```python
import jax
import jax.numpy as jnp
from jax import lax
from jax.experimental import pallas as pl
from jax.experimental.pallas import tpu as pltpu

F32 = jnp.float32
BF16 = jnp.bfloat16
SDS = jax.ShapeDtypeStruct
MESH = pl.DeviceIdType.MESH

N_DEV = 8
D_MODEL = 1024
SEQ = 2048
B_LOC = 2
T_LOC = B_LOC * SEQ
N_MEM = 256
HEAD = 64
ATTN_W = 512
SGU_W = 256
MEM_W = 256
IN_COLS = 3328
W_IN_SHARD = IN_COLS // N_DEV
CHUNK = 128
DILATIONS = ((1, 2048), (4, 512), (16, 128))
RADIUS = 64
EPS = 1e-6
NEG = -1e30
SCALE = HEAD ** -0.5
C_QA, C_KA, C_VA, C_ZA, C_UB, C_VB, C_ZB, C_QM, C_ZM = 0, 512, 1024, 1536, 2048, 2304, 2560, 2816, 3072
QKV_W = 1536
REST_W = IN_COLS - QKV_W

ADAM_LR, ADAM_B1, ADAM_B2, ADAM_EPS, ADAM_WD, ADAM_STEP = 0.001, 0.9, 0.999, 1e-08, 0.01, 10

V7X_VMEM_MIB = 64
VMEM_NO_STAGING_MIB = V7X_VMEM_MIB - 6


def _params(vmem_mib, sem=None):
    assert vmem_mib < V7X_VMEM_MIB
    return pltpu.CompilerParams(vmem_limit_bytes=vmem_mib << 20, dimension_semantics=sem)


def _dot(a, b):
    return jnp.dot(a.astype(BF16), b.astype(BF16), preferred_element_type=F32)


def _dot_nt(a, b):
    return lax.dot_general(a.astype(BF16), b.astype(BF16), (((1,), (1,)), ((), ())), preferred_element_type=F32)


def _dot_tn(a, b):
    return lax.dot_general(a.astype(BF16), b.astype(BF16), (((0,), (0,)), ((), ())), preferred_element_type=F32)


def _rstd(v):
    return lax.rsqrt(jnp.mean(v * v, axis=-1, keepdims=True) + EPS)


def _rms_bwd(v, r, g, dy):
    gdy = g * dy
    return r * gdy - v * (r * r * r * jnp.mean(gdy * v, axis=-1, keepdims=True))


def _sigmoid(z):
    return 1.0 / (1.0 + jnp.exp(-z))


def _silu_and_grad(z):
    s = _sigmoid(z)
    return z * s, s * (1.0 + z * (1.0 - s))


_G_C = 0.7978845608028654
_G_K = 0.044715


def _gelu_and_grad(v):
    t = jnp.tanh(_G_C * (v + _G_K * (v * v * v)))
    cdf = 0.5 * (1.0 + t)
    return v * cdf, cdf + 0.5 * v * (1.0 - t * t) * (_G_C * (1.0 + 3.0 * _G_K * v * v))


def _cast_rows(src_ref, dst_ref, rows, step=256):
    def one(i, carry):
        r = pl.ds(pl.multiple_of(i * step, step), step)
        dst_ref[r, :] = src_ref[r, :].astype(dst_ref.dtype)
        return carry
    lax.fori_loop(0, rows // step, one, 0)


def _left_lanes(rows):
    return lax.broadcasted_iota(jnp.int32, (rows, 128), 1) < HEAD


def _mesh_pos():
    return lax.axis_index("x"), lax.axis_index("y"), lax.axis_index("c")


def _peer(pos, k):
    x, y, c = pos
    return (1 - x if k & 4 else x, 1 - y if k & 2 else y, 1 - c if k & 1 else c)


def _flat(pos):
    return 4 * pos[0] + 2 * pos[1] + pos[2]


def _allgather_weights(w_in_t, w_kv, w_out):
    def body(win_ref, wkv_ref, wout_ref, wint_o, wkv_o, wout_o, send_sems, recv_sems):
        x, y, c = _mesh_pos()
        me, sib = (x, y, c), (x, y, 1 - c)
        chips = [(1 - x, y), (x, 1 - y), (1 - x, 1 - y)]

        def rows(p):
            return wint_o.at[pl.ds(pl.multiple_of(_flat(p) * W_IN_SHARD, 16), W_IN_SHARD), :]

        rows(me)[...] = win_ref[...].astype(BF16)

        def copy(k, block, to):
            return pltpu.make_async_remote_copy(
                src_ref=rows(block), dst_ref=rows(block), send_sem=send_sems.at[k], recv_sem=recv_sems.at[k],
                device_id=to, device_id_type=MESH)

        first = [copy(0, me, sib)] + [copy(1 + j, me, (*chip, c)) for j, chip in enumerate(chips)]
        for cp in first:
            cp.start()
        wkv_o[...] = wkv_ref[...].astype(BF16)
        wout_o[...] = wout_ref[...].astype(BF16)
        passed = []
        for j, chip in enumerate(chips):
            copy(1 + j, (*chip, c), me).wait_recv()
            fwd = copy(4 + j, (*chip, c), sib)
            fwd.start()
            passed.append(fwd)
        copy(0, sib, me).wait_recv()
        for j, chip in enumerate(chips):
            copy(4 + j, (*chip, 1 - c), me).wait_recv()
        for cp in first + passed:
            cp.wait_send()

    vmem = pl.BlockSpec(memory_space=pltpu.VMEM)
    return pl.pallas_call(
        body, name="allgather_weights",
        out_shape=(SDS((IN_COLS, D_MODEL), BF16), SDS(w_kv.shape, BF16), SDS(w_out.shape, BF16)),
        in_specs=[vmem, vmem, vmem], out_specs=(vmem, vmem, vmem),
        scratch_shapes=[pltpu.SemaphoreType.DMA((7,)), pltpu.SemaphoreType.DMA((7,))],
        compiler_params=_params(40),
    )(w_in_t, w_kv, w_out)


def _proj_fwd(x2, g_norm, wint):
    tm = 256
    d = DILATIONS[2][0]
    per_ex = SEQ // tm

    def body(x_ref, g_ref, w_ref, o_ref, o16_ref):
        xv = x_ref[...]
        h = xv * _rstd(xv) * g_ref[...]
        res = _dot_nt(h, w_ref[...])
        o_ref[...] = res
        r_out = lax.broadcasted_iota(jnp.int32, (tm, tm), 0)
        r_in = lax.broadcasted_iota(jnp.int32, (tm, tm), 1)
        pick = (r_in == d * (r_out % (tm // d)) + r_out // (tm // d)).astype(BF16)
        grouped = _dot(pick, res[:, 0:QKV_W]).astype(BF16)
        for rho in range(d):
            o16_ref[0, rho] = grouped[rho * (tm // d):(rho + 1) * (tm // d), :]

    return pl.pallas_call(
        body, name="proj_fwd", grid=(T_LOC // tm,),
        in_specs=[pl.BlockSpec((tm, D_MODEL), lambda i: (i, 0)), pl.BlockSpec((1, D_MODEL), lambda i: (0, 0)),
                  pl.BlockSpec((IN_COLS, D_MODEL), lambda i: (0, 0))],
        out_specs=(pl.BlockSpec((tm, IN_COLS), lambda i: (i, 0)),
                   pl.BlockSpec((1, d, tm // d, QKV_W), lambda i: (i // per_ex, 0, i % per_ex, 0))),
        out_shape=(SDS((T_LOC, IN_COLS), F32), SDS((B_LOC, d, SEQ // d, QKV_W), BF16)),
        compiler_params=_params(48, ("arbitrary",)),
    )(x2, g_norm, wint)


def _memkv_fwd(mem2, g_mem, wkv):
    def body(m_ref, g_ref, w_ref, o_ref):
        mv = m_ref[...]
        o_ref[...] = _dot(mv * _rstd(mv) * g_ref[...], w_ref[...])

    return pl.pallas_call(
        body, name="memkv_fwd", out_shape=SDS((B_LOC * N_MEM, 2 * MEM_W), F32), compiler_params=_params(32),
    )(mem2, g_mem, wkv)


N_BIAS = 7


def _fill_bias_tables(sl_ref, tab):
    for cfg, (d, length) in enumerate(DILATIONS):
        nk = min(length, 2 * CHUNK)
        r = lax.broadcasted_iota(jnp.int32, (CHUNK, nk), 0)
        c = lax.broadcasted_iota(jnp.int32, (CHUNK, nk), 1)
        for var in range(3 if length > nk else 1):
            rel = jnp.abs(r - c + var * RADIUS)
            dist = rel.astype(F32) * float(d)
            for h in range(2):
                slope = sl_ref[0, 0:1, h * HEAD:h * HEAD + 1]
                tab[3 * cfg + var, h * CHUNK:(h + 1) * CHUNK, 0:nk] = jnp.where(rel <= RADIUS, -slope * dist, NEG)


def _attn_blocks(visit, unroll):
    def step(t, carry):
        for cfg, (d, length) in enumerate(DILATIONS):
            nblk = length // CHUNK
            if nblk == 1:
                visit(cfg, 0, t, t, length, t)
                continue
            rho, i = (0, t) if d == 1 else (t // nblk, t % nblk)
            ks = jnp.clip(i * CHUNK - RADIUS, 0, length - 2 * CHUNK)
            visit(cfg, (i * CHUNK - ks) // RADIUS, rho + d * (i * CHUNK), rho + d * ks, 2 * CHUNK, t)
        return carry
    lax.fori_loop(0, 16, step, 0, unroll=unroll)


def _stack_heads(v, left):
    return jnp.concatenate([jnp.where(left, v, 0.0), jnp.where(left, 0.0, v)], axis=0)


def _unstack_heads(v, left):
    return jnp.where(left, v[0:CHUNK], v[CHUNK:2 * CHUNK])


def _rows(start, n, d):
    return pl.ds(start, n) if d == 1 else pl.ds(start, n, stride=d)


def _blk16(col0):
    d, length = DILATIONS[2]
    return pl.BlockSpec((1, d, length, 128), lambda b, hp: (b, 0, 0, col0 // 128 + hp))


def _attn_fwd(proj, qkv16, slopes):
    def body(sl_ref, q_ref, k_ref, v_ref, q16_ref, k16_ref, v16_ref, a_ref, lse_ref, *scr):
        o_c, m_c, l_c, tab = scr[0:3], scr[3:6], scr[6:9], scr[9]
        left = _left_lanes(CHUNK)
        _fill_bias_tables(sl_ref, tab)

        def block(cfg, var, q0, k0, nk, t):
            d = DILATIONS[cfg][0]
            rq, rk = _rows(q0, CHUNK, d), _rows(k0, nk, d)
            if cfg == 2:
                qb, kw, vw = q16_ref[0, t].astype(F32), k16_ref[0, t], v16_ref[0, t]
            else:
                qb, kw, vw = q_ref[rq, :], k_ref[rk, :], v_ref[rk, :]
            qs = _stack_heads(qb * SCALE, left)
            s = _dot_nt(qs, kw) + tab[3 * cfg + var, :, 0:nk]
            m = jnp.max(s, axis=-1, keepdims=True)
            p = jnp.exp(s - m)
            o_c[cfg][rq, :] = _unstack_heads(_dot(p, vw), left)
            m_c[cfg][rq, :] = _unstack_heads(m, left)
            l_c[cfg][rq, :] = _unstack_heads(jnp.sum(p, axis=-1, keepdims=True), left)
        _attn_blocks(block, 8)

        def merge(j, carry):
            rows = pl.ds(pl.multiple_of(j * 256, 256), 256)
            ms = [m_c[i][rows, :] for i in range(3)]
            top = jnp.maximum(jnp.maximum(ms[0], ms[1]), ms[2])
            ws = [jnp.exp(m - top) for m in ms]
            den = l_c[0][rows, :] * ws[0] + l_c[1][rows, :] * ws[1] + l_c[2][rows, :] * ws[2]
            num = o_c[0][rows, :] * ws[0] + o_c[1][rows, :] * ws[1] + o_c[2][rows, :] * ws[2]
            a_ref[rows, :] = num / den
            lse_ref[rows, :] = top + jnp.log(den)
            return carry
        lax.fori_loop(0, SEQ // 256, merge, 0)

    blk = lambda col0: pl.BlockSpec((SEQ, 128), lambda b, hp: (b, col0 // 128 + hp))
    out = pl.BlockSpec((SEQ, 128), lambda b, hp: (b, hp))
    return pl.pallas_call(
        body, name="attn_fwd", grid=(B_LOC, 4),
        in_specs=[pl.BlockSpec((1, 8, 128), lambda b, hp: (hp, 0, 0)), blk(C_QA), blk(C_KA), blk(C_VA),
                  _blk16(C_QA), _blk16(C_KA), _blk16(C_VA)],
        out_specs=(out, out),
        out_shape=(SDS((T_LOC, ATTN_W), F32), SDS((T_LOC, ATTN_W), F32)),
        scratch_shapes=[pltpu.VMEM((SEQ, 128), F32)] * 9 + [pltpu.VMEM((N_BIAS, 2 * CHUNK, 2 * CHUNK), F32)],
        compiler_params=_params(40, ("arbitrary", "arbitrary")),
    )(slopes, proj, proj, proj, qkv16, qkv16, qkv16)


def _chunks_side_by_side(v, pr, tm):
    return jnp.concatenate([v[ch * CHUNK:(ch + 1) * CHUNK, pr * 128:(pr + 1) * 128] for ch in range(tm // CHUNK)], axis=1)


def _first_group_lanes(tm):
    return lax.broadcasted_iota(jnp.int32, (CHUNK, tm), 1) % 128 < HEAD


def _store_chunks(dst_ref, pr, val, tm):
    for ch in range(tm // CHUNK):
        dst_ref[ch * CHUNK:(ch + 1) * CHUNK, pr * 128:(pr + 1) * 128] = val[:, ch * CHUNK:(ch + 1) * CHUNK]


def _sgu_mix(vn, ws_ref, dst_ref, tm):
    first = _first_group_lanes(tm)
    for pr in range(2):
        vp = _chunks_side_by_side(vn, pr, tm)
        _store_chunks(dst_ref, pr, jnp.where(first, _dot(ws_ref[2 * pr], vp), _dot(ws_ref[2 * pr + 1], vp)), tm)


def _mem_head_of_lane(rows):
    return lax.broadcasted_iota(jnp.int32, (rows, MEM_W), 1) // HEAD


def _stack_mem_heads(v, rows):
    head = _mem_head_of_lane(rows)
    return jnp.concatenate([jnp.where(head == h, v, 0.0) for h in range(4)], axis=0)


def _unstack_mem_heads(v, rows):
    head = _mem_head_of_lane(rows)
    out = v[0:rows]
    for h in range(1, 4):
        out = jnp.where(head == h, v[h * rows:(h + 1) * rows], out)
    return out


def _mem_attn_probs(q, kmem, rows):
    qs = _stack_mem_heads(q, rows).astype(BF16)
    s = _dot_nt(qs, kmem) * SCALE
    e = jnp.exp(s - jnp.max(s, axis=-1, keepdims=True))
    return e * (1.0 / jnp.sum(e, axis=-1, keepdims=True)), qs


def _branch_blocks(tm):
    col = lambda w, c0: pl.BlockSpec((tm, w), lambda i: (i, c0 // w))
    return [col(512, C_ZA), col(256, C_UB), col(256, C_VB), col(256, C_ZB), col(256, C_QM), col(256, C_ZM)]


def _branch_fwd(proj, a, kv, w_s, b_exp, g_v):
    tm = 512
    per_ex = SEQ // tm

    def body(za_ref, ub_ref, vb_ref, zb_ref, qm_ref, zm_ref, a_ref, kv_ref, ws_ref, be_ref, gv_ref, o_ref, mix):
        o_ref[:, 0:ATTN_W] = (_silu_and_grad(za_ref[...])[0] * a_ref[...]).astype(BF16)
        gu = _gelu_and_grad(ub_ref[...])[0]
        gv = _gelu_and_grad(vb_ref[...])[0]
        vn = gv * _rstd(gv) * gv_ref[...]
        _sgu_mix(vn.astype(BF16), ws_ref, mix, tm)
        sg = gu * (mix[...] + jnp.concatenate([be_ref[...]] * (tm // CHUNK), axis=0))
        o_ref[:, ATTN_W:ATTN_W + SGU_W] = (_silu_and_grad(zb_ref[...])[0] * sg).astype(BF16)
        p = _mem_attn_probs(qm_ref[...], kv_ref[:, 0:MEM_W], tm)[0]
        mo = _unstack_mem_heads(_dot(p, kv_ref[:, MEM_W:2 * MEM_W]), tm)
        o_ref[:, ATTN_W + SGU_W:D_MODEL] = (_silu_and_grad(zm_ref[...])[0] * mo).astype(BF16)

    full = lambda shape: pl.BlockSpec(shape, lambda i: (0,) * len(shape))
    return pl.pallas_call(
        body, name="branch_fwd", grid=(T_LOC // tm,),
        in_specs=_branch_blocks(tm) + [
            pl.BlockSpec((tm, ATTN_W), lambda i: (i, 0)), pl.BlockSpec((N_MEM, 2 * MEM_W), lambda i: (i // per_ex, 0)),
            full((4, CHUNK, CHUNK)), full((CHUNK, SGU_W)), full((1, SGU_W))],
        out_specs=pl.BlockSpec((tm, D_MODEL), lambda i: (i, 0)),
        out_shape=SDS((T_LOC, D_MODEL), BF16),
        scratch_shapes=[pltpu.VMEM((tm, SGU_W), F32)],
        compiler_params=_params(VMEM_NO_STAGING_MIB, ("arbitrary",)),
    )(proj, proj, proj, proj, proj, proj, a, kv, w_s, b_exp, g_v)


def _outproj_loss(gated, wout, x2, tgt2, g_final):
    tm = 512

    def body(g_ref, w_ref, x_ref, t_ref, gf_ref, dh2_ref, loss_ref, dgf_ref, dwo_ref, dwo_acc):
        @pl.when(pl.program_id(0) == 0)
        def _():
            loss_ref[...] = jnp.zeros_like(loss_ref)
            dgf_ref[...] = jnp.zeros_like(dgf_ref)
            dwo_acc[...] = jnp.zeros_like(dwo_acc)
        gated = g_ref[...]
        h2 = x_ref[...] + _dot(gated, w_ref[...])
        r = _rstd(h2)
        gf = gf_ref[...]
        err = h2 * r * gf - t_ref[...]
        loss_ref[...] += 0.5 * jnp.sum(jnp.mean(err * err, axis=-1, keepdims=True))
        dy = err * (1.0 / D_MODEL)
        dh2 = _rms_bwd(h2, r, gf, dy)
        dh2_ref[...] = dh2
        dgf_ref[...] += jnp.sum(dy * (h2 * r), axis=0, keepdims=True)
        dwo_acc[...] += _dot_tn(gated, dh2)

        @pl.when(pl.program_id(0) == pl.num_programs(0) - 1)
        def _():
            _cast_rows(dwo_acc, dwo_ref, D_MODEL)

    row = pl.BlockSpec((tm, D_MODEL), lambda i: (i, 0))
    vec = pl.BlockSpec((1, D_MODEL), lambda i: (0, 0))
    square = pl.BlockSpec((D_MODEL, D_MODEL), lambda i: (0, 0))
    return pl.pallas_call(
        body, name="outproj_loss", grid=(T_LOC // tm,),
        in_specs=[row, square, row, row, vec],
        out_specs=(row, pl.BlockSpec((8, 128), lambda i: (0, 0)), vec, square),
        out_shape=(SDS((T_LOC, D_MODEL), F32), SDS((8, 128), F32), SDS((1, D_MODEL), F32), SDS((D_MODEL, D_MODEL), BF16)),
        scratch_shapes=[pltpu.VMEM((D_MODEL, D_MODEL), F32)],
        compiler_params=_params(VMEM_NO_STAGING_MIB, ("arbitrary",)),
    )(gated, wout, x2, tgt2, g_final)


def _branch_bwd(dh2, wout, proj, a, kv, w_s, b_exp, g_v):
    tm = 512
    per_ex = SEQ // tm

    def body(dh2_ref, w_ref, za_ref, ub_ref, vb_ref, zb_ref, qm_ref, zm_ref, a_ref, kv_ref, ws_ref,
             be_ref, gv_ref, da_ref, dr_ref, dkv_ref, dws_ref, db_ref, dgv_ref, mix, dvn, dmsum):
        i = pl.program_id(0)

        @pl.when(i == 0)
        def _():
            dws_ref[...] = jnp.zeros_like(dws_ref)
            dgv_ref[...] = jnp.zeros_like(dgv_ref)
            dmsum[...] = jnp.zeros_like(dmsum)

        @pl.when(i % per_ex == 0)
        def _():
            dkv_ref[...] = jnp.zeros_like(dkv_ref)

        dg = _dot_nt(dh2_ref[...], w_ref[...])

        sa, dsa = _silu_and_grad(za_ref[...])
        dga = dg[:, 0:ATTN_W]
        da_ref[...] = dga * sa
        dr_ref[:, 0:512] = (dga * a_ref[...] * dsa).astype(BF16)

        ub, vb = ub_ref[...], vb_ref[...]
        gu, dgu = _gelu_and_grad(ub)
        gv, dgv = _gelu_and_grad(vb)
        rv = _rstd(gv)
        gain = gv_ref[...]
        vn = (gv * rv * gain).astype(BF16)
        _sgu_mix(vn, ws_ref, mix, tm)
        mixed = mix[...] + jnp.concatenate([be_ref[...]] * (tm // CHUNK), axis=0)
        sb, dsb = _silu_and_grad(zb_ref[...])
        dgb = dg[:, ATTN_W:ATTN_W + SGU_W]
        dsg = dgb * sb
        dr_ref[:, 512:768] = (dsg * mixed * dgu).astype(BF16)
        dr_ref[:, 1024:1280] = (dgb * (gu * mixed) * dsb).astype(BF16)
        dmix = dsg * gu
        for ch in range(tm // CHUNK):
            dmsum[...] += dmix[ch * CHUNK:(ch + 1) * CHUNK, :]
        first = _first_group_lanes(tm)
        for pr in range(2):
            dmp, vp = _chunks_side_by_side(dmix, pr, tm), _chunks_side_by_side(vn, pr, tm)
            dws_ref[2 * pr] += _dot_nt(jnp.where(first, dmp, 0.0), vp)
            dws_ref[2 * pr + 1] += _dot_nt(jnp.where(first, 0.0, dmp), vp)
            _store_chunks(dvn, pr, jnp.where(first, _dot_tn(ws_ref[2 * pr], dmp), _dot_tn(ws_ref[2 * pr + 1], dmp)), tm)
        dvn_v = dvn[...]
        dgv_ref[...] += jnp.sum(dvn_v * (gv * rv), axis=0, keepdims=True)
        dr_ref[:, 768:1024] = (_rms_bwd(gv, rv, gain, dvn_v) * dgv).astype(BF16)

        szm, dszm = _silu_and_grad(zm_ref[...])
        dgm = dg[:, ATTN_W + SGU_W:D_MODEL]
        kmem, vmem_ = kv_ref[:, 0:MEM_W].astype(BF16), kv_ref[:, MEM_W:2 * MEM_W].astype(BF16)
        p, qs = _mem_attn_probs(qm_ref[...], kmem, tm)
        dmos = _stack_mem_heads(dgm * szm, tm).astype(BF16)
        dr_ref[:, 1536:1792] = (dgm * _unstack_mem_heads(_dot(p, vmem_), tm) * dszm).astype(BF16)
        dp = _dot_nt(dmos, vmem_)
        ds = (p * (dp - jnp.sum(p * dp, axis=-1, keepdims=True)) * SCALE).astype(BF16)
        dr_ref[:, 1280:1536] = _unstack_mem_heads(_dot(ds, kmem), tm).astype(BF16)
        dkv_ref[:, 0:MEM_W] += _dot_tn(ds, qs)
        dkv_ref[:, MEM_W:2 * MEM_W] += _dot_tn(p, dmos)

        @pl.when(i == pl.num_programs(0) - 1)
        def _():
            tot = dmsum[...]
            hi = tot.astype(BF16)
            lo = (tot - hi.astype(F32)).astype(BF16)
            grp = (lax.broadcasted_iota(jnp.int32, (SGU_W, 128), 0) // HEAD
                   == lax.broadcasted_iota(jnp.int32, (SGU_W, 128), 1)).astype(BF16)
            db_ref[...] = (_dot(hi, grp) + _dot(lo, grp)).T[0:4, :]

    full = lambda shape: pl.BlockSpec(shape, lambda i: (0,) * len(shape))
    row = lambda w: pl.BlockSpec((tm, w), lambda i: (i, 0))
    return pl.pallas_call(
        body, name="branch_bwd", grid=(T_LOC // tm,),
        in_specs=[row(D_MODEL), full((D_MODEL, D_MODEL))] + _branch_blocks(tm) + [
            row(ATTN_W), pl.BlockSpec((N_MEM, 2 * MEM_W), lambda i: (i // per_ex, 0)),
            full((4, CHUNK, CHUNK)), full((CHUNK, SGU_W)), full((1, SGU_W))],
        out_specs=(row(ATTN_W), row(REST_W), pl.BlockSpec((N_MEM, 2 * MEM_W), lambda i: (i // per_ex, 0)),
                   full((4, CHUNK, CHUNK)), full((4, CHUNK)), full((1, SGU_W))),
        out_shape=(SDS((T_LOC, ATTN_W), F32), SDS((T_LOC, REST_W), BF16), SDS((B_LOC * N_MEM, 2 * MEM_W), F32),
                   SDS((4, CHUNK, CHUNK), F32), SDS((4, CHUNK), F32), SDS((1, SGU_W), F32)),
        scratch_shapes=[pltpu.VMEM((tm, SGU_W), F32), pltpu.VMEM((tm, SGU_W), F32), pltpu.VMEM((CHUNK, SGU_W), F32)],
        compiler_params=_params(56, ("arbitrary",)),
    )(dh2, wout, proj, proj, proj, proj, proj, proj, a, kv, w_s, b_exp, g_v)


def _attn_bwd(proj, qkv16, slopes, da, a, lse):
    def body(sl_ref, q_ref, k_ref, v_ref, q16_ref, k16_ref, v16_ref, da_ref, a_ref, lse_ref,
             dq_ref, dk_ref, dv_ref, *scr):
        dq_s, dk_s, dv_s, tab = scr[0:3], scr[3:6], scr[6:9], scr[9]
        lse_h, delta_h = scr[10:12], scr[12:14]
        p_all, ds_all = scr[14], scr[15]
        left = _left_lanes(CHUNK)
        _fill_bias_tables(sl_ref, tab)

        def prep(j, carry):
            rows = pl.ds(pl.multiple_of(j * 256, 256), 256)
            l256 = _left_lanes(256)
            prod = da_ref[rows, :] * a_ref[rows, :]
            delta_h[0][rows, :] = jnp.broadcast_to(jnp.sum(jnp.where(l256, prod, 0.0), axis=-1, keepdims=True), (256, 128))
            delta_h[1][rows, :] = jnp.broadcast_to(jnp.sum(jnp.where(l256, 0.0, prod), axis=-1, keepdims=True), (256, 128))
            pair = lse_ref[rows, :]
            other = pltpu.roll(pair, HEAD, axis=1)
            lse_h[0][rows, :] = jnp.where(l256, pair, other)
            lse_h[1][rows, :] = jnp.where(l256, other, pair)
            zero = jnp.zeros((256, 128), F32)
            for cfg in range(3):
                dk_s[cfg][rows, :] = zero
                dv_s[cfg][rows, :] = zero
            return carry
        lax.fori_loop(0, SEQ // 256, prep, 0)

        def per_row(halves, rq, nk):
            v = jnp.concatenate([halves[0][rq, :], halves[1][rq, :]], axis=0)
            return v if nk == 128 else jnp.concatenate([v, v], axis=1)

        def qkv(cfg, rq, rk, t):
            if cfg == 2:
                return q16_ref[0, t].astype(F32), k16_ref[0, t], v16_ref[0, t]
            return q_ref[rq, :], k_ref[rk, :], v_ref[rk, :]

        def probs(cfg, var, q0, k0, nk, t):
            d = DILATIONS[cfg][0]
            rq, rk = _rows(q0, CHUNK, d), _rows(k0, nk, d)
            qb, kw, vw = qkv(cfg, rq, rk, t)
            qs = _stack_heads(qb * SCALE, left)
            das = _stack_heads(da_ref[rq, :], left)
            s = _dot_nt(qs, kw) + tab[3 * cfg + var, :, 0:nk]
            p = jnp.exp(s - per_row(lse_h, rq, nk))
            p_all[16 * cfg + t, :, 0:nk] = p.astype(BF16)
            ds_all[16 * cfg + t, :, 0:nk] = (p * (_dot_nt(das, vw) - per_row(delta_h, rq, nk))).astype(BF16)
        _attn_blocks(probs, 4)

        def grads(cfg, var, q0, k0, nk, t):
            d = DILATIONS[cfg][0]
            rq, rk = _rows(q0, CHUNK, d), _rows(k0, nk, d)
            qb, kw, _ = qkv(cfg, rq, rk, t)
            qs = _stack_heads(qb * SCALE, left).astype(BF16)
            das = _stack_heads(da_ref[rq, :], left).astype(BF16)
            p, ds = p_all[16 * cfg + t, :, 0:nk], ds_all[16 * cfg + t, :, 0:nk]
            dq_s[cfg][rq, :] = _unstack_heads(_dot(ds, kw), left) * SCALE
            dk_s[cfg][rk, :] += _dot_tn(ds, qs)
            dv_s[cfg][rk, :] += _dot_tn(p, das)
        _attn_blocks(grads, 4)

        def flush(j, carry):
            rows = pl.ds(pl.multiple_of(j * 256, 256), 256)
            for acc, dst in ((dq_s, dq_ref), (dk_s, dk_ref), (dv_s, dv_ref)):
                dst[rows, :] = (acc[0][rows, :] + acc[1][rows, :] + acc[2][rows, :]).astype(BF16)
            return carry
        lax.fori_loop(0, SEQ // 256, flush, 0)

    blk = lambda col0: pl.BlockSpec((SEQ, 128), lambda b, hp: (b, col0 // 128 + hp))
    own = pl.BlockSpec((SEQ, 128), lambda b, hp: (b, hp))
    return pl.pallas_call(
        body, name="attn_bwd", grid=(B_LOC, 4),
        in_specs=[pl.BlockSpec((1, 8, 128), lambda b, hp: (hp, 0, 0)), blk(C_QA), blk(C_KA), blk(C_VA),
                  _blk16(C_QA), _blk16(C_KA), _blk16(C_VA), own, own, own],
        out_specs=(own, own, own),
        out_shape=(SDS((T_LOC, ATTN_W), BF16),) * 3,
        scratch_shapes=[pltpu.VMEM((SEQ, 128), F32)] * 9 + [pltpu.VMEM((N_BIAS, 2 * CHUNK, 2 * CHUNK), F32)]
        + [pltpu.VMEM((SEQ, 128), F32)] * 4 + [pltpu.VMEM((48, 2 * CHUNK, 2 * CHUNK), BF16)] * 2,
        compiler_params=_params(52, ("arbitrary", "arbitrary")),
    )(slopes, proj, proj, proj, qkv16, qkv16, qkv16, da, a, lse)


def _dproj_specs(tm):
    third = pl.BlockSpec((tm, ATTN_W), lambda i: (i, 0))
    return [third, third, third, pl.BlockSpec((tm, REST_W), lambda i: (i, 0))]


def _dx(dq, dk, dv, dr, wint, x2, dh2, g_norm):
    tm = 256

    def body(dq_ref, dk_ref, dv_ref, dr_ref, w_ref, x_ref, dh2_ref, g_ref, gx_ref, dgn_ref):
        @pl.when(pl.program_id(0) == 0)
        def _():
            dgn_ref[...] = jnp.zeros_like(dgn_ref)
        dh = (_dot(dq_ref[...], w_ref[C_QA:C_KA, :]) + _dot(dk_ref[...], w_ref[C_KA:C_VA, :])
              + _dot(dv_ref[...], w_ref[C_VA:C_ZA, :]) + _dot(dr_ref[...], w_ref[C_ZA:IN_COLS, :]))
        xv = x_ref[...]
        r = _rstd(xv)
        gx_ref[...] = dh2_ref[...] + _rms_bwd(xv, r, g_ref[...], dh)
        dgn_ref[...] += jnp.sum(dh * (xv * r), axis=0, keepdims=True)

    row = pl.BlockSpec((tm, D_MODEL), lambda i: (i, 0))
    vec = pl.BlockSpec((1, D_MODEL), lambda i: (0, 0))
    return pl.pallas_call(
        body, name="dx", grid=(T_LOC // tm,),
        in_specs=_dproj_specs(tm) + [pl.BlockSpec((IN_COLS, D_MODEL), lambda i: (0, 0)), row, row, vec],
        out_specs=(row, vec),
        out_shape=(SDS((T_LOC, D_MODEL), F32), SDS((1, D_MODEL), F32)),
        compiler_params=_params(48, ("arbitrary",)),
    )(dq, dk, dv, dr, wint, x2, dh2, g_norm)


def _dwin(dq, dk, dv, dr, x2, g_norm, half):
    tm = 512
    width = D_MODEL // 2
    cols = slice(half * width, (half + 1) * width)

    def body(dq_ref, dk_ref, dv_ref, dr_ref, x_ref, g_ref, o_ref, acc):
        @pl.when(pl.program_id(0) == 0)
        def _():
            acc[...] = jnp.zeros_like(acc)
        xv = x_ref[...]
        h = (xv[:, cols] * _rstd(xv) * g_ref[:, cols]).astype(BF16)
        acc[C_QA:C_KA, :] += _dot_tn(dq_ref[...], h)
        acc[C_KA:C_VA, :] += _dot_tn(dk_ref[...], h)
        acc[C_VA:C_ZA, :] += _dot_tn(dv_ref[...], h)
        acc[C_ZA:IN_COLS, :] += _dot_tn(dr_ref[...], h)

        @pl.when(pl.program_id(0) == pl.num_programs(0) - 1)
        def _():
            _cast_rows(acc, o_ref, IN_COLS)

    return pl.pallas_call(
        body, name="dwin%d" % half, grid=(T_LOC // tm,),
        in_specs=_dproj_specs(tm) + [pl.BlockSpec((tm, D_MODEL), lambda i: (i, 0)), pl.BlockSpec((1, D_MODEL), lambda i: (0, 0))],
        out_specs=pl.BlockSpec((IN_COLS, width), lambda i: (0, 0)),
        out_shape=SDS((IN_COLS, width), BF16),
        scratch_shapes=[pltpu.VMEM((IN_COLS, width), F32)],
        compiler_params=_params(48, ("arbitrary",)),
    )(dq, dk, dv, dr, x2, g_norm)


def _memkv_bwd(dkv, mem2, g_mem, wkv):
    def body(dkv_ref, m_ref, g_ref, w_ref, dw_ref, dg_ref):
        mv = m_ref[...]
        r = _rstd(mv)
        dkv_v = dkv_ref[...].astype(BF16)
        dw_ref[...] = _dot_tn(mv * r * g_ref[...], dkv_v).astype(BF16)
        dg_ref[...] = jnp.sum(_dot_nt(dkv_v, w_ref[...]) * (mv * r), axis=0, keepdims=True)

    return pl.pallas_call(
        body, name="memkv_bwd", out_shape=(SDS((D_MODEL, 2 * MEM_W), BF16), SDS((1, D_MODEL), F32)),
        compiler_params=_params(32),
    )(dkv, mem2, g_mem, wkv)


def _allreduce_small(parts):
    n = len(parts)

    def body(*refs):
        ins, outs, bufs = refs[0:n], refs[n:2 * n], refs[2 * n:3 * n]
        send_sems, recv_sems = refs[3 * n], refs[3 * n + 1]
        pos = _mesh_pos()
        me = _flat(pos)
        for a in range(n):
            bufs[a][me] = ins[a][...]

        def copy(a, k, slot):
            return pltpu.make_async_remote_copy(
                src_ref=ins[a], dst_ref=bufs[a].at[slot],
                send_sem=send_sems.at[7 * a + k - 1], recv_sem=recv_sems.at[7 * a + k - 1],
                device_id=_peer(pos, k), device_id_type=MESH)

        sent = [copy(a, k, me) for a in range(n) for k in range(1, N_DEV)]
        for cp in sent:
            cp.start()
        for a in range(n):
            for k in range(1, N_DEV):
                copy(a, k, _flat(_peer(pos, k))).wait_recv()
        for cp in sent:
            cp.wait_send()
        for a in range(n):
            acc = bufs[a][0]
            for s in range(1, N_DEV):
                acc = acc + bufs[a][s]
            outs[a][...] = acc

    vmem = pl.BlockSpec(memory_space=pltpu.VMEM)
    return pl.pallas_call(
        body, name="allreduce_small",
        out_shape=tuple(SDS(p.shape, F32) for p in parts),
        in_specs=[vmem] * n, out_specs=(vmem,) * n,
        scratch_shapes=[pltpu.VMEM((N_DEV,) + p.shape, F32) for p in parts]
        + [pltpu.SemaphoreType.DMA((7 * n,)), pltpu.SemaphoreType.DMA((7 * n,))],
        compiler_params=_params(16),
    )(*parts)


_HBM = pl.BlockSpec(memory_space=pltpu.HBM)
_SEM = pl.BlockSpec(memory_space=pltpu.SEMAPHORE)
_SIDE_EFFECT = pltpu.SideEffectType.DATAFLOW_SIDE_EFFECTING


def _exchange_copies(src_refs, land_refs, scatter, send_sems, recv_sems):
    pos = _mesh_pos()
    copies = []
    for a, (src, land) in enumerate(zip(src_refs, land_refs)):
        n = land.shape[1]
        for k in range(1, N_DEV):
            peer = _peer(pos, k)
            piece = src.at[pl.ds(pl.multiple_of(_flat(peer) * n, 16), n), :] if scatter[a] else src
            copies.append(pltpu.make_async_remote_copy(
                src_ref=piece, dst_ref=land.at[_flat(pos)],
                send_sem=send_sems.at[7 * a + k - 1], recv_sem=recv_sems.at[7 * a + k - 1],
                device_id=peer, device_id_type=MESH))
    return copies


def _exchange_start(name, srcs, scatter, lands):
    n = len(srcs)

    def body(*refs):
        for cp in _exchange_copies(refs[0:n], refs[n:2 * n], scatter, refs[2 * n], refs[2 * n + 1]):
            cp.start()
        refs[-1][...] = jnp.zeros_like(refs[-1])

    ops = [pltpu.with_memory_space_constraint(t, pltpu.HBM) for t in (*srcs, *lands)]
    out = pl.pallas_call(
        body, name=name,
        out_shape=(pltpu.SemaphoreType.DMA((7 * n,)), pltpu.SemaphoreType.DMA((7 * n,)),
                   *[pltpu.HBM(t.shape, t.dtype) for t in ops], SDS((8, 128), F32)),
        in_specs=[_HBM] * (2 * n),
        out_specs=(_SEM, _SEM, *[_HBM] * (2 * n), pl.BlockSpec(memory_space=pltpu.VMEM)),
        input_output_aliases={i: 2 + i for i in range(2 * n)},
        compiler_params=pltpu.CompilerParams(has_side_effects=_SIDE_EFFECT),
    )(*ops)
    return out[0], out[1], out[2:2 + n], out[2 + n:2 + 2 * n], out[-1]


def _exchange_wait(name, started, scatter, after):
    send_sems, recv_sems, srcs, lands, _ = started
    n = len(srcs)

    def body(*refs):
        for cp in _exchange_copies(refs[0:n], refs[n:2 * n], scatter, refs[2 * n], refs[2 * n + 1]):
            cp.wait_send()
            cp.wait_recv()

    out = pl.pallas_call(
        body, name=name,
        out_shape=tuple(pltpu.HBM(t.shape, t.dtype) for t in (*srcs, *lands)),
        in_specs=[_HBM] * (2 * n) + [_SEM, _SEM, pl.BlockSpec(memory_space=pl.ANY)],
        out_specs=(_HBM,) * (2 * n),
        input_output_aliases={i: i for i in range(2 * n)},
        compiler_params=pltpu.CompilerParams(has_side_effects=_SIDE_EFFECT),
    )(*srcs, *lands, send_sems, recv_sems, after)
    return out[n:]


def _landing(own, me):
    return lax.dynamic_update_slice(lax.empty((N_DEV,) + own.shape, own.dtype), own[None], (me,) + (0,) * own.ndim)


def _adamw(w, g, m, v):
    m = ADAM_B1 * m + (1.0 - ADAM_B1) * g
    v = ADAM_B2 * v + (1.0 - ADAM_B2) * (g * g)
    m_hat = m / (1.0 - ADAM_B1 ** ADAM_STEP)
    v_hat = v / (1.0 - ADAM_B2 ** ADAM_STEP)
    return -ADAM_LR * (m_hat / (jnp.sqrt(v_hat) + ADAM_EPS) + ADAM_WD * w), m, v


def _adam_slots(name, pieces, w, m, v):
    rows, cols = w.shape
    starts = [sum(p.shape[2] for p in pieces[:i]) for i in range(len(pieces) + 1)]
    assert starts[-1] == cols and all(p.shape[1] == rows for p in pieces)

    def body(*refs):
        s_refs, (w_ref, m_ref, v_ref, g_o, d_o, m_o, v_o, acc) = refs[:len(pieces)], refs[len(pieces):]
        s = pl.program_id(0)

        @pl.when(s == 0)
        def _():
            for i, s_ref in enumerate(s_refs):
                acc[:, starts[i]:starts[i + 1]] = s_ref[0].astype(F32)

        @pl.when(s > 0)
        def _():
            for i, s_ref in enumerate(s_refs):
                acc[:, starts[i]:starts[i + 1]] += s_ref[0].astype(F32)

        @pl.when(s == N_DEV - 1)
        def _():
            g = acc[...]
            g_o[...] = g
            d_o[...], m_o[...], v_o[...] = _adamw(w_ref[...], g, m_ref[...], v_ref[...])

    full = pl.BlockSpec((rows, cols), lambda s: (0, 0))
    return pl.pallas_call(
        body, name="adam_" + name, grid=(N_DEV,),
        in_specs=[pl.BlockSpec((1, rows, p.shape[2]), lambda s: (s, 0, 0)) for p in pieces] + [full, full, full],
        out_specs=(full,) * 4, out_shape=(SDS((rows, cols), F32),) * 4,
        scratch_shapes=[pltpu.VMEM((rows, cols), F32)],
        compiler_params=_params(40, ("arbitrary",)),
    )(*pieces, w, m, v)


def _adam_small(ws, gs, ms, vs, loss_slots):
    n = len(ws)

    def total(ref, like):
        if len(ref.shape) == len(like.shape):
            return ref[...]
        acc = ref[0]
        for s in range(1, N_DEV):
            acc = acc + ref[s]
        return acc

    def body(*refs):
        w_r, g_r, m_r, v_r = refs[0:n], refs[n:2 * n], refs[2 * n:3 * n], refs[3 * n:4 * n]
        loss_r, outs = refs[4 * n], refs[4 * n + 1:]
        for a in range(n):
            g = total(g_r[a], w_r[a])
            outs[a][...] = g
            outs[n + 1 + 3 * a][...], outs[n + 2 + 3 * a][...], outs[n + 3 + 3 * a][...] = _adamw(
                w_r[a][...], g, m_r[a][...], v_r[a][...])
        outs[n][...] = total(loss_r, outs[n])

    out = pl.pallas_call(
        body, name="adam_small",
        out_shape=tuple(SDS(w.shape, F32) for w in ws) + (SDS(loss_slots.shape[1:], F32),)
        + tuple(SDS(w.shape, F32) for w in ws for _ in range(3)),
        compiler_params=_params(16),
    )(*ws, *gs, *ms, *vs, loss_slots)
    return out[0:n], out[n], out[n + 1:]


def kernel(x, mem, g_norm, w_in, w_sgu_spatial, b_sgu_spatial, g_sgu_v, g_mem, w_mem_kv, w_out, g_final, loss_target, m_g_norm, m_w_in, m_w_sgu_spatial, m_b_sgu_spatial, m_g_sgu_v, m_g_mem, m_w_mem_kv, m_w_out, m_g_final, v_g_norm, v_w_in, v_w_sgu_spatial, v_b_sgu_spatial, v_g_sgu_v, v_g_mem, v_w_mem_kv, v_w_out, v_g_final):
    x2 = x.reshape(T_LOC, D_MODEL)
    tgt2 = loss_target.reshape(T_LOC, D_MODEL)
    mem2 = mem.reshape(B_LOC * N_MEM, D_MODEL)
    w_s = w_sgu_spatial[0]
    b_exp = jnp.repeat(b_sgu_spatial[0].T, HEAD, axis=1)
    slope = jnp.power(2.0, -8.0 * (jnp.arange(8, dtype=F32) + 1.0) / 8)
    slopes = jnp.broadcast_to(jnp.repeat(slope.reshape(4, 2), HEAD, axis=1)[:, None, :], (4, 8, 128))

    tr = lambda t: jnp.transpose(t[0])

    me = _flat(_mesh_pos())
    own_rows = lambda t: lax.dynamic_slice_in_dim(t, me * (t.shape[0] // N_DEV), t.shape[0] // N_DEV)

    wint, wkv_own, wout_own = _allgather_weights(tr(w_in), w_mem_kv[0], w_out[0])
    started0 = _exchange_start("exchange0_start", [wkv_own, wout_own], [False, False],
                               [_landing(wkv_own, me), _landing(wout_own, me)])
    proj, qkv16 = _proj_fwd(x2, g_norm + started0[4][0:1, 0:1], wint)
    wkv, wout = _exchange_wait("exchange0_wait", started0, [False, False], proj)
    wkv, wout = wkv.reshape(D_MODEL, 2 * MEM_W), wout.reshape(D_MODEL, D_MODEL)
    kv = _memkv_fwd(mem2, g_mem, wkv)
    a, lse = _attn_fwd(proj, qkv16, slopes)
    gated = _branch_fwd(proj, a, kv, w_s, b_exp, g_sgu_v)
    dh2, loss8, dgf, dwout = _outproj_loss(gated, wout, x2, tgt2, g_final.reshape(1, D_MODEL))

    da, dr, dkv, dws, dbs, dgv = _branch_bwd(dh2, wout, proj, a, kv, w_s, b_exp, g_sgu_v)
    dwkv, dgm = _memkv_bwd(dkv, mem2, g_mem, wkv)

    early = [dws.reshape(4 * CHUNK, CHUNK), dbs, dgv, dgm, dgf, loss8]
    scatter1 = [True, True] + [False] * len(early)
    started1 = _exchange_start(
        "exchange1_start", [dwkv, dwout] + early, scatter1,
        [_landing(own_rows(dwkv), me), _landing(own_rows(dwout), me)] + [_landing(t, me) for t in early])
    dq, dk, dv = _attn_bwd(proj, qkv16, slopes + started1[4][0:1, 0:1], da, a, lse)
    s_wkv, s_wout, s_ws, s_bs, s_gv, s_gm, s_gf, s_loss = _exchange_wait("exchange1_wait", started1, scatter1, dq)

    dwint0 = _dwin(dq, dk, dv, dr, x2, g_norm, 0)
    started2 = _exchange_start("exchange2_start", [dwint0], [True], [_landing(own_rows(dwint0), me)])
    dwint1 = _dwin(dq, dk, dv, dr, x2, g_norm + started2[4][0:1, 0:1], 1)
    started3 = _exchange_start("exchange3_start", [dwint1], [True], [_landing(own_rows(dwint1), me)])
    grad_x, dgn = _dx(dq, dk, dv, dr, wint, x2, dh2, g_norm + started3[4][0:1, 0:1])
    s_win0, = _exchange_wait("exchange2_wait", started2, [True], grad_x)
    dgn_sum, = _allreduce_small([dgn])
    s_win1, = _exchange_wait("exchange3_wait", started3, [True], dgn_sum)

    g_win, d_win, m_win, v_win = map(
        jnp.transpose, _adam_slots("w_in", [s_win0, s_win1], tr(w_in), tr(m_w_in), tr(v_w_in)))
    g_wkv, d_wkv, m_wkv, v_wkv = _adam_slots("w_mem_kv", [s_wkv], w_mem_kv[0], m_w_mem_kv[0], v_w_mem_kv[0])
    g_wout, d_wout, m_wout, v_wout = _adam_slots("w_out", [s_wout], w_out[0], m_w_out[0], v_w_out[0])

    small_shapes = [(1, D_MODEL), (4 * CHUNK, CHUNK), (4, CHUNK), (1, SGU_W), (1, D_MODEL), (1, D_MODEL)]
    pack = lambda arrs: [t.reshape(s) for t, s in zip(arrs, small_shapes)]
    g_small, loss_sum, upd = _adam_small(
        pack([g_norm, w_sgu_spatial, b_sgu_spatial, g_sgu_v, g_mem, g_final]),
        [dgn_sum, s_ws, s_bs, s_gv, s_gm, s_gf],
        pack([m_g_norm, m_w_sgu_spatial, m_b_sgu_spatial, m_g_sgu_v, m_g_mem, m_g_final]),
        pack([v_g_norm, v_w_sgu_spatial, v_b_sgu_spatial, v_g_sgu_v, v_g_mem, v_g_final]), s_loss)
    out_shapes = [g_norm.shape, w_sgu_spatial.shape, b_sgu_spatial.shape, g_sgu_v.shape, g_mem.shape, g_final.shape]
    unpack = lambda arrs: [t.reshape(s) for t, s in zip(arrs, out_shapes)]
    gs = unpack(g_small)
    ds, nms, nvs = unpack(upd[0::3]), unpack(upd[1::3]), unpack(upd[2::3])

    loss = loss_sum[0, 0]

    def assemble(small, win, wkv_, wout_):
        return [small[0], win[None], small[1], small[2], small[3], small[4], wkv_[None], wout_[None], small[5]]

    return (loss, grad_x.reshape(x.shape),
            *assemble(gs, g_win, g_wkv, g_wout), *assemble(ds, d_win, d_wkv, d_wout),
            *assemble(nms, m_win, m_wkv, m_wout), *assemble(nvs, v_win, v_wkv, v_wout))
```

```python
import jax
import jax.numpy as jnp
from jax import lax
from jax.experimental import pallas as pl
from jax.experimental.pallas import tpu as pltpu

F32 = jnp.float32
BF16 = jnp.bfloat16
SDS = jax.ShapeDtypeStruct
MESH = pl.DeviceIdType.MESH

N_DEV = 8
D_MODEL = 1024
SEQ = 2048
B_LOC = 2
T_LOC = B_LOC * SEQ
N_MEM = 256
HEAD = 64
ATTN_W = 512
SGU_W = 256
MEM_W = 256
IN_COLS = 3328
W_IN_SHARD = IN_COLS // N_DEV
CHUNK = 128
DILATIONS = ((1, 2048), (4, 512), (16, 128))
RADIUS = 64
EPS = 1e-6
NEG = -1e30
SCALE = HEAD ** -0.5
C_QA, C_KA, C_VA, C_ZA, C_UB, C_VB, C_ZB, C_QM, C_ZM = 0, 512, 1024, 1536, 2048, 2304, 2560, 2816, 3072
QKV_W = 1536
REST_W = IN_COLS - QKV_W

ADAM_LR, ADAM_B1, ADAM_B2, ADAM_EPS, ADAM_WD, ADAM_STEP = 0.001, 0.9, 0.999, 1e-08, 0.01, 10

V7X_VMEM_MIB = 64
VMEM_NO_STAGING_MIB = V7X_VMEM_MIB - 6


def _params(vmem_mib, sem=None):
    assert vmem_mib < V7X_VMEM_MIB
    return pltpu.CompilerParams(vmem_limit_bytes=vmem_mib << 20, dimension_semantics=sem)


def _dot(a, b):
    return jnp.dot(a.astype(BF16), b.astype(BF16), preferred_element_type=F32)


def _dot_nt(a, b):
    return lax.dot_general(a.astype(BF16), b.astype(BF16), (((1,), (1,)), ((), ())), preferred_element_type=F32)


def _dot_tn(a, b):
    return lax.dot_general(a.astype(BF16), b.astype(BF16), (((0,), (0,)), ((), ())), preferred_element_type=F32)


def _rstd(v):
    return lax.rsqrt(jnp.mean(v * v, axis=-1, keepdims=True) + EPS)


def _rms_bwd(v, r, g, dy):
    gdy = g * dy
    return r * gdy - v * (r * r * r * jnp.mean(gdy * v, axis=-1, keepdims=True))


def _sigmoid(z):
    return 1.0 / (1.0 + jnp.exp(-z))


def _silu_and_grad(z):
    s = _sigmoid(z)
    return z * s, s * (1.0 + z * (1.0 - s))


_G_C = 0.7978845608028654
_G_K = 0.044715


def _gelu_and_grad(v):
    t = jnp.tanh(_G_C * (v + _G_K * (v * v * v)))
    cdf = 0.5 * (1.0 + t)
    return v * cdf, cdf + 0.5 * v * (1.0 - t * t) * (_G_C * (1.0 + 3.0 * _G_K * v * v))


def _cast_rows(src_ref, dst_ref, rows, step=256):
    def one(i, carry):
        r = pl.ds(pl.multiple_of(i * step, step), step)
        dst_ref[r, :] = src_ref[r, :].astype(dst_ref.dtype)
        return carry
    lax.fori_loop(0, rows // step, one, 0)


def _left_lanes(rows):
    return lax.broadcasted_iota(jnp.int32, (rows, 128), 1) < HEAD


def _mesh_pos():
    return lax.axis_index("x"), lax.axis_index("y"), lax.axis_index("c")


def _peer(pos, k):
    x, y, c = pos
    return (1 - x if k & 4 else x, 1 - y if k & 2 else y, 1 - c if k & 1 else c)


def _flat(pos):
    return 4 * pos[0] + 2 * pos[1] + pos[2]


def _allgather_weights(w_in_t, w_kv, w_out):
    def body(win_ref, wkv_ref, wout_ref, wint_o, wkv_o, wout_o, send_sems, recv_sems):
        x, y, c = _mesh_pos()
        me, sib = (x, y, c), (x, y, 1 - c)
        chips = [(1 - x, y), (x, 1 - y), (1 - x, 1 - y)]

        def rows(p):
            return wint_o.at[pl.ds(pl.multiple_of(_flat(p) * W_IN_SHARD, 16), W_IN_SHARD), :]

        rows(me)[...] = win_ref[...].astype(BF16)

        def copy(k, block, to):
            return pltpu.make_async_remote_copy(
                src_ref=rows(block), dst_ref=rows(block), send_sem=send_sems.at[k], recv_sem=recv_sems.at[k],
                device_id=to, device_id_type=MESH)

        first = [copy(0, me, sib)] + [copy(1 + j, me, (*chip, c)) for j, chip in enumerate(chips)]
        for cp in first:
            cp.start()
        wkv_o[...] = wkv_ref[...].astype(BF16)
        wout_o[...] = wout_ref[...].astype(BF16)
        passed = []
        for j, chip in enumerate(chips):
            copy(1 + j, (*chip, c), me).wait_recv()
            fwd = copy(4 + j, (*chip, c), sib)
            fwd.start()
            passed.append(fwd)
        copy(0, sib, me).wait_recv()
        for j, chip in enumerate(chips):
            copy(4 + j, (*chip, 1 - c), me).wait_recv()
        for cp in first + passed:
            cp.wait_send()

    vmem = pl.BlockSpec(memory_space=pltpu.VMEM)
    return pl.pallas_call(
        body, name="allgather_weights",
        out_shape=(SDS((IN_COLS, D_MODEL), BF16), SDS(w_kv.shape, BF16), SDS(w_out.shape, BF16)),
        in_specs=[vmem, vmem, vmem], out_specs=(vmem, vmem, vmem),
        scratch_shapes=[pltpu.SemaphoreType.DMA((7,)), pltpu.SemaphoreType.DMA((7,))],
        compiler_params=_params(40),
    )(w_in_t, w_kv, w_out)


def _proj_fwd(x2, g_norm, wint):
    tm = 256
    d = DILATIONS[2][0]
    per_ex = SEQ // tm

    def body(x_ref, g_ref, w_ref, o_ref, o16_ref):
        xv = x_ref[...]
        h = xv * _rstd(xv) * g_ref[...]
        res = _dot_nt(h, w_ref[...])
        o_ref[...] = res
        r_out = lax.broadcasted_iota(jnp.int32, (tm, tm), 0)
        r_in = lax.broadcasted_iota(jnp.int32, (tm, tm), 1)
        pick = (r_in == d * (r_out % (tm // d)) + r_out // (tm // d)).astype(BF16)
        grouped = _dot(pick, res[:, 0:QKV_W]).astype(BF16)
        for rho in range(d):
            o16_ref[0, rho] = grouped[rho * (tm // d):(rho + 1) * (tm // d), :]

    return pl.pallas_call(
        body, name="proj_fwd", grid=(T_LOC // tm,),
        in_specs=[pl.BlockSpec((tm, D_MODEL), lambda i: (i, 0)), pl.BlockSpec((1, D_MODEL), lambda i: (0, 0)),
                  pl.BlockSpec((IN_COLS, D_MODEL), lambda i: (0, 0))],
        out_specs=(pl.BlockSpec((tm, IN_COLS), lambda i: (i, 0)),
                   pl.BlockSpec((1, d, tm // d, QKV_W), lambda i: (i // per_ex, 0, i % per_ex, 0))),
        out_shape=(SDS((T_LOC, IN_COLS), F32), SDS((B_LOC, d, SEQ // d, QKV_W), BF16)),
        compiler_params=_params(48, ("arbitrary",)),
    )(x2, g_norm, wint)


def _memkv_fwd(mem2, g_mem, wkv):
    def body(m_ref, g_ref, w_ref, o_ref):
        mv = m_ref[...]
        o_ref[...] = _dot(mv * _rstd(mv) * g_ref[...], w_ref[...])

    return pl.pallas_call(
        body, name="memkv_fwd", out_shape=SDS((B_LOC * N_MEM, 2 * MEM_W), F32), compiler_params=_params(32),
    )(mem2, g_mem, wkv)


N_BIAS = 7


def _fill_bias_tables(sl_ref, tab):
    for cfg, (d, length) in enumerate(DILATIONS):
        nk = min(length, 2 * CHUNK)
        r = lax.broadcasted_iota(jnp.int32, (CHUNK, nk), 0)
        c = lax.broadcasted_iota(jnp.int32, (CHUNK, nk), 1)
        for var in range(3 if length > nk else 1):
            rel = jnp.abs(r - c + var * RADIUS)
            dist = rel.astype(F32) * float(d)
            for h in range(2):
                slope = sl_ref[0, 0:1, h * HEAD:h * HEAD + 1]
                tab[3 * cfg + var, h * CHUNK:(h + 1) * CHUNK, 0:nk] = jnp.where(rel <= RADIUS, -slope * dist, NEG)


def _attn_blocks(visit, unroll):
    def step(t, carry):
        for cfg, (d, length) in enumerate(DILATIONS):
            nblk = length // CHUNK
            if nblk == 1:
                visit(cfg, 0, t, t, length, t)
                continue
            rho, i = (0, t) if d == 1 else (t // nblk, t % nblk)
            ks = jnp.clip(i * CHUNK - RADIUS, 0, length - 2 * CHUNK)
            visit(cfg, (i * CHUNK - ks) // RADIUS, rho + d * (i * CHUNK), rho + d * ks, 2 * CHUNK, t)
        return carry
    lax.fori_loop(0, 16, step, 0, unroll=unroll)


def _stack_heads(v, left):
    return jnp.concatenate([jnp.where(left, v, 0.0), jnp.where(left, 0.0, v)], axis=0)


def _unstack_heads(v, left):
    return jnp.where(left, v[0:CHUNK], v[CHUNK:2 * CHUNK])


def _rows(start, n, d):
    return pl.ds(start, n) if d == 1 else pl.ds(start, n, stride=d)


def _blk16(col0):
    d, length = DILATIONS[2]
    return pl.BlockSpec((1, d, length, 128), lambda b, hp: (b, 0, 0, col0 // 128 + hp))


def _attn_fwd(proj, qkv16, slopes):
    def body(sl_ref, q_ref, k_ref, v_ref, q16_ref, k16_ref, v16_ref, a_ref, lse_ref, *scr):
        o_c, m_c, l_c, tab = scr[0:3], scr[3:6], scr[6:9], scr[9]
        left = _left_lanes(CHUNK)
        _fill_bias_tables(sl_ref, tab)

        def block(cfg, var, q0, k0, nk, t):
            d = DILATIONS[cfg][0]
            rq, rk = _rows(q0, CHUNK, d), _rows(k0, nk, d)
            if cfg == 2:
                qb, kw, vw = q16_ref[0, t].astype(F32), k16_ref[0, t], v16_ref[0, t]
            else:
                qb, kw, vw = q_ref[rq, :], k_ref[rk, :], v_ref[rk, :]
            qs = _stack_heads(qb * SCALE, left)
            s = _dot_nt(qs, kw) + tab[3 * cfg + var, :, 0:nk]
            m = jnp.max(s, axis=-1, keepdims=True)
            p = jnp.exp(s - m)
            o_c[cfg][rq, :] = _unstack_heads(_dot(p, vw), left)
            m_c[cfg][rq, :] = _unstack_heads(m, left)
            l_c[cfg][rq, :] = _unstack_heads(jnp.sum(p, axis=-1, keepdims=True), left)
        _attn_blocks(block, 8)

        def merge(j, carry):
            rows = pl.ds(pl.multiple_of(j * 256, 256), 256)
            ms = [m_c[i][rows, :] for i in range(3)]
            top = jnp.maximum(jnp.maximum(ms[0], ms[1]), ms[2])
            ws = [jnp.exp(m - top) for m in ms]
            den = l_c[0][rows, :] * ws[0] + l_c[1][rows, :] * ws[1] + l_c[2][rows, :] * ws[2]
            num = o_c[0][rows, :] * ws[0] + o_c[1][rows, :] * ws[1] + o_c[2][rows, :] * ws[2]
            a_ref[rows, :] = num / den
            lse_ref[rows, :] = top + jnp.log(den)
            return carry
        lax.fori_loop(0, SEQ // 256, merge, 0)

    blk = lambda col0: pl.BlockSpec((SEQ, 128), lambda b, hp: (b, col0 // 128 + hp))
    out = pl.BlockSpec((SEQ, 128), lambda b, hp: (b, hp))
    return pl.pallas_call(
        body, name="attn_fwd", grid=(B_LOC, 4),
        in_specs=[pl.BlockSpec((1, 8, 128), lambda b, hp: (hp, 0, 0)), blk(C_QA), blk(C_KA), blk(C_VA),
                  _blk16(C_QA), _blk16(C_KA), _blk16(C_VA)],
        out_specs=(out, out),
        out_shape=(SDS((T_LOC, ATTN_W), F32), SDS((T_LOC, ATTN_W), F32)),
        scratch_shapes=[pltpu.VMEM((SEQ, 128), F32)] * 9 + [pltpu.VMEM((N_BIAS, 2 * CHUNK, 2 * CHUNK), F32)],
        compiler_params=_params(40, ("arbitrary", "arbitrary")),
    )(slopes, proj, proj, proj, qkv16, qkv16, qkv16)


def _chunks_side_by_side(v, pr, tm):
    return jnp.concatenate([v[ch * CHUNK:(ch + 1) * CHUNK, pr * 128:(pr + 1) * 128] for ch in range(tm // CHUNK)], axis=1)


def _first_group_lanes(tm):
    return lax.broadcasted_iota(jnp.int32, (CHUNK, tm), 1) % 128 < HEAD


def _store_chunks(dst_ref, pr, val, tm):
    for ch in range(tm // CHUNK):
        dst_ref[ch * CHUNK:(ch + 1) * CHUNK, pr * 128:(pr + 1) * 128] = val[:, ch * CHUNK:(ch + 1) * CHUNK]


def _sgu_mix(vn, ws_ref, dst_ref, tm):
    first = _first_group_lanes(tm)
    for pr in range(2):
        vp = _chunks_side_by_side(vn, pr, tm)
        _store_chunks(dst_ref, pr, jnp.where(first, _dot(ws_ref[2 * pr], vp), _dot(ws_ref[2 * pr + 1], vp)), tm)


def _mem_head_of_lane(rows):
    return lax.broadcasted_iota(jnp.int32, (rows, MEM_W), 1) // HEAD


def _stack_mem_heads(v, rows):
    head = _mem_head_of_lane(rows)
    return jnp.concatenate([jnp.where(head == h, v, 0.0) for h in range(4)], axis=0)


def _unstack_mem_heads(v, rows):
    head = _mem_head_of_lane(rows)
    out = v[0:rows]
    for h in range(1, 4):
        out = jnp.where(head == h, v[h * rows:(h + 1) * rows], out)
    return out


def _mem_attn_probs(q, kmem, rows):
    qs = _stack_mem_heads(q, rows).astype(BF16)
    s = _dot_nt(qs, kmem) * SCALE
    e = jnp.exp(s - jnp.max(s, axis=-1, keepdims=True))
    return e * (1.0 / jnp.sum(e, axis=-1, keepdims=True)), qs


def _branch_blocks(tm):
    col = lambda w, c0: pl.BlockSpec((tm, w), lambda i: (i, c0 // w))
    return [col(512, C_ZA), col(256, C_UB), col(256, C_VB), col(256, C_ZB), col(256, C_QM), col(256, C_ZM)]


def _branch_fwd(proj, a, kv, w_s, b_exp, g_v):
    tm = 512
    per_ex = SEQ // tm

    def body(za_ref, ub_ref, vb_ref, zb_ref, qm_ref, zm_ref, a_ref, kv_ref, ws_ref, be_ref, gv_ref, o_ref, mix):
        o_ref[:, 0:ATTN_W] = (_silu_and_grad(za_ref[...])[0] * a_ref[...]).astype(BF16)
        gu = _gelu_and_grad(ub_ref[...])[0]
        gv = _gelu_and_grad(vb_ref[...])[0]
        vn = gv * _rstd(gv) * gv_ref[...]
        _sgu_mix(vn.astype(BF16), ws_ref, mix, tm)
        sg = gu * (mix[...] + jnp.concatenate([be_ref[...]] * (tm // CHUNK), axis=0))
        o_ref[:, ATTN_W:ATTN_W + SGU_W] = (_silu_and_grad(zb_ref[...])[0] * sg).astype(BF16)
        p = _mem_attn_probs(qm_ref[...], kv_ref[:, 0:MEM_W], tm)[0]
        mo = _unstack_mem_heads(_dot(p, kv_ref[:, MEM_W:2 * MEM_W]), tm)
        o_ref[:, ATTN_W + SGU_W:D_MODEL] = (_silu_and_grad(zm_ref[...])[0] * mo).astype(BF16)

    full = lambda shape: pl.BlockSpec(shape, lambda i: (0,) * len(shape))
    return pl.pallas_call(
        body, name="branch_fwd", grid=(T_LOC // tm,),
        in_specs=_branch_blocks(tm) + [
            pl.BlockSpec((tm, ATTN_W), lambda i: (i, 0)), pl.BlockSpec((N_MEM, 2 * MEM_W), lambda i: (i // per_ex, 0)),
            full((4, CHUNK, CHUNK)), full((CHUNK, SGU_W)), full((1, SGU_W))],
        out_specs=pl.BlockSpec((tm, D_MODEL), lambda i: (i, 0)),
        out_shape=SDS((T_LOC, D_MODEL), BF16),
        scratch_shapes=[pltpu.VMEM((tm, SGU_W), F32)],
        compiler_params=_params(VMEM_NO_STAGING_MIB, ("arbitrary",)),
    )(proj, proj, proj, proj, proj, proj, a, kv, w_s, b_exp, g_v)


def _outproj_loss(gated, wout, x2, tgt2, g_final):
    tm = 512

    def body(g_ref, w_ref, x_ref, t_ref, gf_ref, dh2_ref, loss_ref, dgf_ref, dwo_ref, dwo_acc):
        @pl.when(pl.program_id(0) == 0)
        def _():
            loss_ref[...] = jnp.zeros_like(loss_ref)
            dgf_ref[...] = jnp.zeros_like(dgf_ref)
            dwo_acc[...] = jnp.zeros_like(dwo_acc)
        gated = g_ref[...]
        h2 = x_ref[...] + _dot(gated, w_ref[...])
        r = _rstd(h2)
        gf = gf_ref[...]
        err = h2 * r * gf - t_ref[...]
        loss_ref[...] += 0.5 * jnp.sum(jnp.mean(err * err, axis=-1, keepdims=True))
        dy = err * (1.0 / D_MODEL)
        dh2 = _rms_bwd(h2, r, gf, dy)
        dh2_ref[...] = dh2
        dgf_ref[...] += jnp.sum(dy * (h2 * r), axis=0, keepdims=True)
        dwo_acc[...] += _dot_tn(gated, dh2)

        @pl.when(pl.program_id(0) == pl.num_programs(0) - 1)
        def _():
            _cast_rows(dwo_acc, dwo_ref, D_MODEL)

    row = pl.BlockSpec((tm, D_MODEL), lambda i: (i, 0))
    vec = pl.BlockSpec((1, D_MODEL), lambda i: (0, 0))
    square = pl.BlockSpec((D_MODEL, D_MODEL), lambda i: (0, 0))
    return pl.pallas_call(
        body, name="outproj_loss", grid=(T_LOC // tm,),
        in_specs=[row, square, row, row, vec],
        out_specs=(row, pl.BlockSpec((8, 128), lambda i: (0, 0)), vec, square),
        out_shape=(SDS((T_LOC, D_MODEL), F32), SDS((8, 128), F32), SDS((1, D_MODEL), F32), SDS((D_MODEL, D_MODEL), BF16)),
        scratch_shapes=[pltpu.VMEM((D_MODEL, D_MODEL), F32)],
        compiler_params=_params(VMEM_NO_STAGING_MIB, ("arbitrary",)),
    )(gated, wout, x2, tgt2, g_final)


def _branch_bwd(dh2, wout, proj, a, kv, w_s, b_exp, g_v):
    tm = 512
    per_ex = SEQ // tm

    def body(dh2_ref, w_ref, za_ref, ub_ref, vb_ref, zb_ref, qm_ref, zm_ref, a_ref, kv_ref, ws_ref,
             be_ref, gv_ref, da_ref, dr_ref, dkv_ref, dws_ref, db_ref, dgv_ref, mix, dvn, dmsum):
        i = pl.program_id(0)

        @pl.when(i == 0)
        def _():
            dws_ref[...] = jnp.zeros_like(dws_ref)
            dgv_ref[...] = jnp.zeros_like(dgv_ref)
            dmsum[...] = jnp.zeros_like(dmsum)

        @pl.when(i % per_ex == 0)
        def _():
            dkv_ref[...] = jnp.zeros_like(dkv_ref)

        dg = _dot_nt(dh2_ref[...], w_ref[...])

        sa, dsa = _silu_and_grad(za_ref[...])
        dga = dg[:, 0:ATTN_W]
        da_ref[...] = dga * sa
        dr_ref[:, 0:512] = (dga * a_ref[...] * dsa).astype(BF16)

        ub, vb = ub_ref[...], vb_ref[...]
        gu, dgu = _gelu_and_grad(ub)
        gv, dgv = _gelu_and_grad(vb)
        rv = _rstd(gv)
        gain = gv_ref[...]
        vn = (gv * rv * gain).astype(BF16)
        _sgu_mix(vn, ws_ref, mix, tm)
        mixed = mix[...] + jnp.concatenate([be_ref[...]] * (tm // CHUNK), axis=0)
        sb, dsb = _silu_and_grad(zb_ref[...])
        dgb = dg[:, ATTN_W:ATTN_W + SGU_W]
        dsg = dgb * sb
        dr_ref[:, 512:768] = (dsg * mixed * dgu).astype(BF16)
        dr_ref[:, 1024:1280] = (dgb * (gu * mixed) * dsb).astype(BF16)
        dmix = dsg * gu
        for ch in range(tm // CHUNK):
            dmsum[...] += dmix[ch * CHUNK:(ch + 1) * CHUNK, :]
        first = _first_group_lanes(tm)
        for pr in range(2):
            dmp, vp = _chunks_side_by_side(dmix, pr, tm), _chunks_side_by_side(vn, pr, tm)
            dws_ref[2 * pr] += _dot_nt(jnp.where(first, dmp, 0.0), vp)
            dws_ref[2 * pr + 1] += _dot_nt(jnp.where(first, 0.0, dmp), vp)
            _store_chunks(dvn, pr, jnp.where(first, _dot_tn(ws_ref[2 * pr], dmp), _dot_tn(ws_ref[2 * pr + 1], dmp)), tm)
        dvn_v = dvn[...]
        dgv_ref[...] += jnp.sum(dvn_v * (gv * rv), axis=0, keepdims=True)
        dr_ref[:, 768:1024] = (_rms_bwd(gv, rv, gain, dvn_v) * dgv).astype(BF16)

        szm, dszm = _silu_and_grad(zm_ref[...])
        dgm = dg[:, ATTN_W + SGU_W:D_MODEL]
        kmem, vmem_ = kv_ref[:, 0:MEM_W].astype(BF16), kv_ref[:, MEM_W:2 * MEM_W].astype(BF16)
        p, qs = _mem_attn_probs(qm_ref[...], kmem, tm)
        dmos = _stack_mem_heads(dgm * szm, tm).astype(BF16)
        dr_ref[:, 1536:1792] = (dgm * _unstack_mem_heads(_dot(p, vmem_), tm) * dszm).astype(BF16)
        dp = _dot_nt(dmos, vmem_)
        ds = (p * (dp - jnp.sum(p * dp, axis=-1, keepdims=True)) * SCALE).astype(BF16)
        dr_ref[:, 1280:1536] = _unstack_mem_heads(_dot(ds, kmem), tm).astype(BF16)
        dkv_ref[:, 0:MEM_W] += _dot_tn(ds, qs)
        dkv_ref[:, MEM_W:2 * MEM_W] += _dot_tn(p, dmos)

        @pl.when(i == pl.num_programs(0) - 1)
        def _():
            tot = dmsum[...]
            hi = tot.astype(BF16)
            lo = (tot - hi.astype(F32)).astype(BF16)
            grp = (lax.broadcasted_iota(jnp.int32, (SGU_W, 128), 0) // HEAD
                   == lax.broadcasted_iota(jnp.int32, (SGU_W, 128), 1)).astype(BF16)
            db_ref[...] = (_dot(hi, grp) + _dot(lo, grp)).T[0:4, :]

    full = lambda shape: pl.BlockSpec(shape, lambda i: (0,) * len(shape))
    row = lambda w: pl.BlockSpec((tm, w), lambda i: (i, 0))
    return pl.pallas_call(
        body, name="branch_bwd", grid=(T_LOC // tm,),
        in_specs=[row(D_MODEL), full((D_MODEL, D_MODEL))] + _branch_blocks(tm) + [
            row(ATTN_W), pl.BlockSpec((N_MEM, 2 * MEM_W), lambda i: (i // per_ex, 0)),
            full((4, CHUNK, CHUNK)), full((CHUNK, SGU_W)), full((1, SGU_W))],
        out_specs=(row(ATTN_W), row(REST_W), pl.BlockSpec((N_MEM, 2 * MEM_W), lambda i: (i // per_ex, 0)),
                   full((4, CHUNK, CHUNK)), full((4, CHUNK)), full((1, SGU_W))),
        out_shape=(SDS((T_LOC, ATTN_W), F32), SDS((T_LOC, REST_W), BF16), SDS((B_LOC * N_MEM, 2 * MEM_W), F32),
                   SDS((4, CHUNK, CHUNK), F32), SDS((4, CHUNK), F32), SDS((1, SGU_W), F32)),
        scratch_shapes=[pltpu.VMEM((tm, SGU_W), F32), pltpu.VMEM((tm, SGU_W), F32), pltpu.VMEM((CHUNK, SGU_W), F32)],
        compiler_params=_params(56, ("arbitrary",)),
    )(dh2, wout, proj, proj, proj, proj, proj, proj, a, kv, w_s, b_exp, g_v)


def _attn_bwd(proj, qkv16, slopes, da, a, lse):
    def body(sl_ref, q_ref, k_ref, v_ref, q16_ref, k16_ref, v16_ref, da_ref, a_ref, lse_ref,
             dq_ref, dk_ref, dv_ref, *scr):
        dq_s, dk_s, dv_s, tab = scr[0:3], scr[3:6], scr[6:9], scr[9]
        lse_h, delta_h = scr[10:12], scr[12:14]
        p_all, ds_all = scr[14], scr[15]
        left = _left_lanes(CHUNK)
        _fill_bias_tables(sl_ref, tab)

        def prep(j, carry):
            rows = pl.ds(pl.multiple_of(j * 256, 256), 256)
            l256 = _left_lanes(256)
            prod = da_ref[rows, :] * a_ref[rows, :]
            delta_h[0][rows, :] = jnp.broadcast_to(jnp.sum(jnp.where(l256, prod, 0.0), axis=-1, keepdims=True), (256, 128))
            delta_h[1][rows, :] = jnp.broadcast_to(jnp.sum(jnp.where(l256, 0.0, prod), axis=-1, keepdims=True), (256, 128))
            pair = lse_ref[rows, :]
            other = pltpu.roll(pair, HEAD, axis=1)
            lse_h[0][rows, :] = jnp.where(l256, pair, other)
            lse_h[1][rows, :] = jnp.where(l256, other, pair)
            zero = jnp.zeros((256, 128), F32)
            for cfg in range(3):
                dk_s[cfg][rows, :] = zero
                dv_s[cfg][rows, :] = zero
            return carry
        lax.fori_loop(0, SEQ // 256, prep, 0)

        def per_row(halves, rq, nk):
            v = jnp.concatenate([halves[0][rq, :], halves[1][rq, :]], axis=0)
            return v if nk == 128 else jnp.concatenate([v, v], axis=1)

        def qkv(cfg, rq, rk, t):
            if cfg == 2:
                return q16_ref[0, t].astype(F32), k16_ref[0, t], v16_ref[0, t]
            return q_ref[rq, :], k_ref[rk, :], v_ref[rk, :]

        def probs(cfg, var, q0, k0, nk, t):
            d = DILATIONS[cfg][0]
            rq, rk = _rows(q0, CHUNK, d), _rows(k0, nk, d)
            qb, kw, vw = qkv(cfg, rq, rk, t)
            qs = _stack_heads(qb * SCALE, left)
            das = _stack_heads(da_ref[rq, :], left)
            s = _dot_nt(qs, kw) + tab[3 * cfg + var, :, 0:nk]
            p = jnp.exp(s - per_row(lse_h, rq, nk))
            p_all[16 * cfg + t, :, 0:nk] = p.astype(BF16)
            ds_all[16 * cfg + t, :, 0:nk] = (p * (_dot_nt(das, vw) - per_row(delta_h, rq, nk))).astype(BF16)
        _attn_blocks(probs, 4)

        def grads(cfg, var, q0, k0, nk, t):
            d = DILATIONS[cfg][0]
            rq, rk = _rows(q0, CHUNK, d), _rows(k0, nk, d)
            qb, kw, _ = qkv(cfg, rq, rk, t)
            qs = _stack_heads(qb * SCALE, left).astype(BF16)
            das = _stack_heads(da_ref[rq, :], left).astype(BF16)
            p, ds = p_all[16 * cfg + t, :, 0:nk], ds_all[16 * cfg + t, :, 0:nk]
            dq_s[cfg][rq, :] = _unstack_heads(_dot(ds, kw), left) * SCALE
            dk_s[cfg][rk, :] += _dot_tn(ds, qs)
            dv_s[cfg][rk, :] += _dot_tn(p, das)
        _attn_blocks(grads, 4)

        def flush(j, carry):
            rows = pl.ds(pl.multiple_of(j * 256, 256), 256)
            for acc, dst in ((dq_s, dq_ref), (dk_s, dk_ref), (dv_s, dv_ref)):
                dst[rows, :] = (acc[0][rows, :] + acc[1][rows, :] + acc[2][rows, :]).astype(BF16)
            return carry
        lax.fori_loop(0, SEQ // 256, flush, 0)

    blk = lambda col0: pl.BlockSpec((SEQ, 128), lambda b, hp: (b, col0 // 128 + hp))
    own = pl.BlockSpec((SEQ, 128), lambda b, hp: (b, hp))
    return pl.pallas_call(
        body, name="attn_bwd", grid=(B_LOC, 4),
        in_specs=[pl.BlockSpec((1, 8, 128), lambda b, hp: (hp, 0, 0)), blk(C_QA), blk(C_KA), blk(C_VA),
                  _blk16(C_QA), _blk16(C_KA), _blk16(C_VA), own, own, own],
        out_specs=(own, own, own),
        out_shape=(SDS((T_LOC, ATTN_W), BF16),) * 3,
        scratch_shapes=[pltpu.VMEM((SEQ, 128), F32)] * 9 + [pltpu.VMEM((N_BIAS, 2 * CHUNK, 2 * CHUNK), F32)]
        + [pltpu.VMEM((SEQ, 128), F32)] * 4 + [pltpu.VMEM((48, 2 * CHUNK, 2 * CHUNK), BF16)] * 2,
        compiler_params=_params(52, ("arbitrary", "arbitrary")),
    )(slopes, proj, proj, proj, qkv16, qkv16, qkv16, da, a, lse)


def _dproj_specs(tm):
    third = pl.BlockSpec((tm, ATTN_W), lambda i: (i, 0))
    return [third, third, third, pl.BlockSpec((tm, REST_W), lambda i: (i, 0))]


def _dx(dq, dk, dv, dr, wint, x2, dh2, g_norm):
    tm = 512

    def body(dq_ref, dk_ref, dv_ref, dr_ref, w_ref, x_ref, dh2_ref, g_ref, gx_ref, dgn_ref):
        @pl.when(pl.program_id(0) == 0)
        def _():
            dgn_ref[...] = jnp.zeros_like(dgn_ref)
        dh = (_dot(dq_ref[...], w_ref[C_QA:C_KA, :]) + _dot(dk_ref[...], w_ref[C_KA:C_VA, :])
              + _dot(dv_ref[...], w_ref[C_VA:C_ZA, :]) + _dot(dr_ref[...], w_ref[C_ZA:IN_COLS, :]))
        xv = x_ref[...]
        r = _rstd(xv)
        gx_ref[...] = dh2_ref[...] + _rms_bwd(xv, r, g_ref[...], dh)
        dgn_ref[...] += jnp.sum(dh * (xv * r), axis=0, keepdims=True)

    row = pl.BlockSpec((tm, D_MODEL), lambda i: (i, 0))
    vec = pl.BlockSpec((1, D_MODEL), lambda i: (0, 0))
    return pl.pallas_call(
        body, name="dx", grid=(T_LOC // tm,),
        in_specs=_dproj_specs(tm) + [pl.BlockSpec((IN_COLS, D_MODEL), lambda i: (0, 0)), row, row, vec],
        out_specs=(row, vec),
        out_shape=(SDS((T_LOC, D_MODEL), F32), SDS((1, D_MODEL), F32)),
        compiler_params=_params(48, ("arbitrary",)),
    )(dq, dk, dv, dr, wint, x2, dh2, g_norm)


def _dwin(dq, dk, dv, dr, x2, g_norm, half):
    tm = 1024
    width = D_MODEL // 2
    cols = slice(half * width, (half + 1) * width)

    def body(dq_ref, dk_ref, dv_ref, dr_ref, x_ref, g_ref, o_ref, acc):
        @pl.when(pl.program_id(0) == 0)
        def _():
            acc[...] = jnp.zeros_like(acc)
        xv = x_ref[...]
        h = (xv[:, cols] * _rstd(xv) * g_ref[:, cols]).astype(BF16)
        acc[C_QA:C_KA, :] += _dot_tn(dq_ref[...], h)
        acc[C_KA:C_VA, :] += _dot_tn(dk_ref[...], h)
        acc[C_VA:C_ZA, :] += _dot_tn(dv_ref[...], h)
        acc[C_ZA:IN_COLS, :] += _dot_tn(dr_ref[...], h)

        @pl.when(pl.program_id(0) == pl.num_programs(0) - 1)
        def _():
            _cast_rows(acc, o_ref, IN_COLS)

    return pl.pallas_call(
        body, name="dwin%d" % half, grid=(T_LOC // tm,),
        in_specs=_dproj_specs(tm) + [pl.BlockSpec((tm, D_MODEL), lambda i: (i, 0)), pl.BlockSpec((1, D_MODEL), lambda i: (0, 0))],
        out_specs=pl.BlockSpec((IN_COLS, width), lambda i: (0, 0)),
        out_shape=SDS((IN_COLS, width), BF16),
        scratch_shapes=[pltpu.VMEM((IN_COLS, width), F32)],
        compiler_params=_params(48, ("arbitrary",)),
    )(dq, dk, dv, dr, x2, g_norm)


def _memkv_bwd(dkv, mem2, g_mem, wkv):
    def body(dkv_ref, m_ref, g_ref, w_ref, dw_ref, dg_ref):
        mv = m_ref[...]
        r = _rstd(mv)
        dkv_v = dkv_ref[...].astype(BF16)
        dw_ref[...] = _dot_tn(mv * r * g_ref[...], dkv_v).astype(BF16)
        dg_ref[...] = jnp.sum(_dot_nt(dkv_v, w_ref[...]) * (mv * r), axis=0, keepdims=True)

    return pl.pallas_call(
        body, name="memkv_bwd", out_shape=(SDS((D_MODEL, 2 * MEM_W), BF16), SDS((1, D_MODEL), F32)),
        compiler_params=_params(32),
    )(dkv, mem2, g_mem, wkv)


def _allreduce_small(parts):
    n = len(parts)

    def body(*refs):
        ins, outs, bufs = refs[0:n], refs[n:2 * n], refs[2 * n:3 * n]
        send_sems, recv_sems = refs[3 * n], refs[3 * n + 1]
        pos = _mesh_pos()
        me = _flat(pos)
        for a in range(n):
            bufs[a][me] = ins[a][...]

        def copy(a, k, slot):
            return pltpu.make_async_remote_copy(
                src_ref=ins[a], dst_ref=bufs[a].at[slot],
                send_sem=send_sems.at[7 * a + k - 1], recv_sem=recv_sems.at[7 * a + k - 1],
                device_id=_peer(pos, k), device_id_type=MESH)

        sent = [copy(a, k, me) for a in range(n) for k in range(1, N_DEV)]
        for cp in sent:
            cp.start()
        for a in range(n):
            for k in range(1, N_DEV):
                copy(a, k, _flat(_peer(pos, k))).wait_recv()
        for cp in sent:
            cp.wait_send()
        for a in range(n):
            acc = bufs[a][0]
            for s in range(1, N_DEV):
                acc = acc + bufs[a][s]
            outs[a][...] = acc

    vmem = pl.BlockSpec(memory_space=pltpu.VMEM)
    return pl.pallas_call(
        body, name="allreduce_small",
        out_shape=tuple(SDS(p.shape, F32) for p in parts),
        in_specs=[vmem] * n, out_specs=(vmem,) * n,
        scratch_shapes=[pltpu.VMEM((N_DEV,) + p.shape, F32) for p in parts]
        + [pltpu.SemaphoreType.DMA((7 * n,)), pltpu.SemaphoreType.DMA((7 * n,))],
        compiler_params=_params(16),
    )(*parts)


_HBM = pl.BlockSpec(memory_space=pltpu.HBM)
_SEM = pl.BlockSpec(memory_space=pltpu.SEMAPHORE)
_SIDE_EFFECT = pltpu.SideEffectType.DATAFLOW_SIDE_EFFECTING


def _exchange_copies(src_refs, land_refs, scatter, send_sems, recv_sems):
    pos = _mesh_pos()
    copies = []
    for a, (src, land) in enumerate(zip(src_refs, land_refs)):
        n = land.shape[1]
        for k in range(1, N_DEV):
            peer = _peer(pos, k)
            piece = src.at[pl.ds(pl.multiple_of(_flat(peer) * n, 16), n), :] if scatter[a] else src
            copies.append(pltpu.make_async_remote_copy(
                src_ref=piece, dst_ref=land.at[_flat(pos)],
                send_sem=send_sems.at[7 * a + k - 1], recv_sem=recv_sems.at[7 * a + k - 1],
                device_id=peer, device_id_type=MESH))
    return copies


def _exchange_start(name, srcs, scatter, lands):
    n = len(srcs)

    def body(*refs):
        for cp in _exchange_copies(refs[0:n], refs[n:2 * n], scatter, refs[2 * n], refs[2 * n + 1]):
            cp.start()
        refs[-1][...] = jnp.zeros_like(refs[-1])

    ops = [pltpu.with_memory_space_constraint(t, pltpu.HBM) for t in (*srcs, *lands)]
    out = pl.pallas_call(
        body, name=name,
        out_shape=(pltpu.SemaphoreType.DMA((7 * n,)), pltpu.SemaphoreType.DMA((7 * n,)),
                   *[pltpu.HBM(t.shape, t.dtype) for t in ops], SDS((8, 128), F32)),
        in_specs=[_HBM] * (2 * n),
        out_specs=(_SEM, _SEM, *[_HBM] * (2 * n), pl.BlockSpec(memory_space=pltpu.VMEM)),
        input_output_aliases={i: 2 + i for i in range(2 * n)},
        compiler_params=pltpu.CompilerParams(has_side_effects=_SIDE_EFFECT),
    )(*ops)
    return out[0], out[1], out[2:2 + n], out[2 + n:2 + 2 * n], out[-1]


def _exchange_wait(name, started, scatter, after):
    send_sems, recv_sems, srcs, lands, _ = started
    n = len(srcs)

    def body(*refs):
        for cp in _exchange_copies(refs[0:n], refs[n:2 * n], scatter, refs[2 * n], refs[2 * n + 1]):
            cp.wait_send()
            cp.wait_recv()

    out = pl.pallas_call(
        body, name=name,
        out_shape=tuple(pltpu.HBM(t.shape, t.dtype) for t in (*srcs, *lands)),
        in_specs=[_HBM] * (2 * n) + [_SEM, _SEM, pl.BlockSpec(memory_space=pl.ANY)],
        out_specs=(_HBM,) * (2 * n),
        input_output_aliases={i: i for i in range(2 * n)},
        compiler_params=pltpu.CompilerParams(has_side_effects=_SIDE_EFFECT),
    )(*srcs, *lands, send_sems, recv_sems, after)
    return out[n:]


def _landing(own, me):
    return lax.dynamic_update_slice(lax.empty((N_DEV,) + own.shape, own.dtype), own[None], (me,) + (0,) * own.ndim)


def _adamw(w, g, m, v):
    m = ADAM_B1 * m + (1.0 - ADAM_B1) * g
    v = ADAM_B2 * v + (1.0 - ADAM_B2) * (g * g)
    m_hat = m / (1.0 - ADAM_B1 ** ADAM_STEP)
    v_hat = v / (1.0 - ADAM_B2 ** ADAM_STEP)
    return -ADAM_LR * (m_hat / (jnp.sqrt(v_hat) + ADAM_EPS) + ADAM_WD * w), m, v


def _adam_slots(name, pieces, w, m, v):
    rows, cols = w.shape
    starts = [sum(p.shape[2] for p in pieces[:i]) for i in range(len(pieces) + 1)]
    assert starts[-1] == cols and all(p.shape[1] == rows for p in pieces)

    def body(*refs):
        s_refs, (w_ref, m_ref, v_ref, g_o, d_o, m_o, v_o, acc) = refs[:len(pieces)], refs[len(pieces):]
        s = pl.program_id(0)

        @pl.when(s == 0)
        def _():
            for i, s_ref in enumerate(s_refs):
                acc[:, starts[i]:starts[i + 1]] = s_ref[0].astype(F32)

        @pl.when(s > 0)
        def _():
            for i, s_ref in enumerate(s_refs):
                acc[:, starts[i]:starts[i + 1]] += s_ref[0].astype(F32)

        @pl.when(s == N_DEV - 1)
        def _():
            g = acc[...]
            g_o[...] = g
            d_o[...], m_o[...], v_o[...] = _adamw(w_ref[...], g, m_ref[...], v_ref[...])

    full = pl.BlockSpec((rows, cols), lambda s: (0, 0))
    return pl.pallas_call(
        body, name="adam_" + name, grid=(N_DEV,),
        in_specs=[pl.BlockSpec((1, rows, p.shape[2]), lambda s: (s, 0, 0)) for p in pieces] + [full, full, full],
        out_specs=(full,) * 4, out_shape=(SDS((rows, cols), F32),) * 4,
        scratch_shapes=[pltpu.VMEM((rows, cols), F32)],
        compiler_params=_params(40, ("arbitrary",)),
    )(*pieces, w, m, v)


def _adam_small(ws, gs, ms, vs, loss_slots):
    n = len(ws)

    def total(ref, like):
        if len(ref.shape) == len(like.shape):
            return ref[...]
        acc = ref[0]
        for s in range(1, N_DEV):
            acc = acc + ref[s]
        return acc

    def body(*refs):
        w_r, g_r, m_r, v_r = refs[0:n], refs[n:2 * n], refs[2 * n:3 * n], refs[3 * n:4 * n]
        loss_r, outs = refs[4 * n], refs[4 * n + 1:]
        for a in range(n):
            g = total(g_r[a], w_r[a])
            outs[a][...] = g
            outs[n + 1 + 3 * a][...], outs[n + 2 + 3 * a][...], outs[n + 3 + 3 * a][...] = _adamw(
                w_r[a][...], g, m_r[a][...], v_r[a][...])
        outs[n][...] = total(loss_r, outs[n])

    out = pl.pallas_call(
        body, name="adam_small",
        out_shape=tuple(SDS(w.shape, F32) for w in ws) + (SDS(loss_slots.shape[1:], F32),)
        + tuple(SDS(w.shape, F32) for w in ws for _ in range(3)),
        compiler_params=_params(16),
    )(*ws, *gs, *ms, *vs, loss_slots)
    return out[0:n], out[n], out[n + 1:]


def kernel(x, mem, g_norm, w_in, w_sgu_spatial, b_sgu_spatial, g_sgu_v, g_mem, w_mem_kv, w_out, g_final, loss_target, m_g_norm, m_w_in, m_w_sgu_spatial, m_b_sgu_spatial, m_g_sgu_v, m_g_mem, m_w_mem_kv, m_w_out, m_g_final, v_g_norm, v_w_in, v_w_sgu_spatial, v_b_sgu_spatial, v_g_sgu_v, v_g_mem, v_w_mem_kv, v_w_out, v_g_final):
    x2 = x.reshape(T_LOC, D_MODEL)
    tgt2 = loss_target.reshape(T_LOC, D_MODEL)
    mem2 = mem.reshape(B_LOC * N_MEM, D_MODEL)
    w_s = w_sgu_spatial[0]
    b_exp = jnp.repeat(b_sgu_spatial[0].T, HEAD, axis=1)
    slope = jnp.power(2.0, -8.0 * (jnp.arange(8, dtype=F32) + 1.0) / 8)
    slopes = jnp.broadcast_to(jnp.repeat(slope.reshape(4, 2), HEAD, axis=1)[:, None, :], (4, 8, 128))

    tr = lambda t: jnp.transpose(t[0])

    me = _flat(_mesh_pos())
    own_rows = lambda t: lax.dynamic_slice_in_dim(t, me * (t.shape[0] // N_DEV), t.shape[0] // N_DEV)

    wint, wkv_own, wout_own = _allgather_weights(tr(w_in), w_mem_kv[0], w_out[0])
    started0 = _exchange_start("exchange0_start", [wkv_own, wout_own], [False, False],
                               [_landing(wkv_own, me), _landing(wout_own, me)])
    proj, qkv16 = _proj_fwd(x2, g_norm + started0[4][0:1, 0:1], wint)
    wkv, wout = _exchange_wait("exchange0_wait", started0, [False, False], proj)
    wkv, wout = wkv.reshape(D_MODEL, 2 * MEM_W), wout.reshape(D_MODEL, D_MODEL)
    kv = _memkv_fwd(mem2, g_mem, wkv)
    a, lse = _attn_fwd(proj, qkv16, slopes)
    gated = _branch_fwd(proj, a, kv, w_s, b_exp, g_sgu_v)
    dh2, loss8, dgf, dwout = _outproj_loss(gated, wout, x2, tgt2, g_final.reshape(1, D_MODEL))

    da, dr, dkv, dws, dbs, dgv = _branch_bwd(dh2, wout, proj, a, kv, w_s, b_exp, g_sgu_v)
    dwkv, dgm = _memkv_bwd(dkv, mem2, g_mem, wkv)

    early = [dws.reshape(4 * CHUNK, CHUNK), dbs, dgv, dgm, dgf, loss8]
    scatter1 = [True, True] + [False] * len(early)
    started1 = _exchange_start(
        "exchange1_start", [dwkv, dwout] + early, scatter1,
        [_landing(own_rows(dwkv), me), _landing(own_rows(dwout), me)] + [_landing(t, me) for t in early])
    dq, dk, dv = _attn_bwd(proj, qkv16, slopes + started1[4][0:1, 0:1], da, a, lse)
    s_wkv, s_wout, s_ws, s_bs, s_gv, s_gm, s_gf, s_loss = _exchange_wait("exchange1_wait", started1, scatter1, dq)

    dwint0 = _dwin(dq, dk, dv, dr, x2, g_norm, 0)
    started2 = _exchange_start("exchange2_start", [dwint0], [True], [_landing(own_rows(dwint0), me)])
    dwint1 = _dwin(dq, dk, dv, dr, x2, g_norm + started2[4][0:1, 0:1], 1)
    started3 = _exchange_start("exchange3_start", [dwint1], [True], [_landing(own_rows(dwint1), me)])
    grad_x, dgn = _dx(dq, dk, dv, dr, wint, x2, dh2, g_norm + started3[4][0:1, 0:1])
    s_win0, = _exchange_wait("exchange2_wait", started2, [True], grad_x)
    dgn_sum, = _allreduce_small([dgn])
    s_win1, = _exchange_wait("exchange3_wait", started3, [True], dgn_sum)

    g_win, d_win, m_win, v_win = map(
        jnp.transpose, _adam_slots("w_in", [s_win0, s_win1], tr(w_in), tr(m_w_in), tr(v_w_in)))
    g_wkv, d_wkv, m_wkv, v_wkv = _adam_slots("w_mem_kv", [s_wkv], w_mem_kv[0], m_w_mem_kv[0], v_w_mem_kv[0])
    g_wout, d_wout, m_wout, v_wout = _adam_slots("w_out", [s_wout], w_out[0], m_w_out[0], v_w_out[0])

    small_shapes = [(1, D_MODEL), (4 * CHUNK, CHUNK), (4, CHUNK), (1, SGU_W), (1, D_MODEL), (1, D_MODEL)]
    pack = lambda arrs: [t.reshape(s) for t, s in zip(arrs, small_shapes)]
    g_small, loss_sum, upd = _adam_small(
        pack([g_norm, w_sgu_spatial, b_sgu_spatial, g_sgu_v, g_mem, g_final]),
        [dgn_sum, s_ws, s_bs, s_gv, s_gm, s_gf],
        pack([m_g_norm, m_w_sgu_spatial, m_b_sgu_spatial, m_g_sgu_v, m_g_mem, m_g_final]),
        pack([v_g_norm, v_w_sgu_spatial, v_b_sgu_spatial, v_g_sgu_v, v_g_mem, v_g_final]), s_loss)
    out_shapes = [g_norm.shape, w_sgu_spatial.shape, b_sgu_spatial.shape, g_sgu_v.shape, g_mem.shape, g_final.shape]
    unpack = lambda arrs: [t.reshape(s) for t, s in zip(arrs, out_shapes)]
    gs = unpack(g_small)
    ds, nms, nvs = unpack(upd[0::3]), unpack(upd[1::3]), unpack(upd[2::3])

    loss = loss_sum[0, 0]

    def assemble(small, win, wkv_, wout_):
        return [small[0], win[None], small[1], small[2], small[3], small[4], wkv_[None], wout_[None], small[5]]

    return (loss, grad_x.reshape(x.shape),
            *assemble(gs, g_win, g_wkv, g_wout), *assemble(ds, d_win, d_wkv, d_wout),
            *assemble(nms, m_win, m_wkv, m_wout), *assemble(nvs, v_win, v_wkv, v_wout))
```

```python
import jax
import jax.numpy as jnp
from jax import lax
from jax.experimental import pallas as pl
from jax.experimental.pallas import tpu as pltpu

F32 = jnp.float32
BF16 = jnp.bfloat16
SDS = jax.ShapeDtypeStruct
MESH = pl.DeviceIdType.MESH

N_DEV = 8
D_MODEL = 1024
SEQ = 2048
B_LOC = 2
T_LOC = B_LOC * SEQ
N_MEM = 256
HEAD = 64
ATTN_W = 512
SGU_W = 256
MEM_W = 256
IN_COLS = 3328
W_IN_SHARD = IN_COLS // N_DEV
CHUNK = 128
DILATIONS = ((1, 2048), (4, 512), (16, 128))
RADIUS = 64
EPS = 1e-6
NEG = -1e30
SCALE = HEAD ** -0.5
C_QA, C_KA, C_VA, C_ZA, C_UB, C_VB, C_ZB, C_QM, C_ZM = 0, 512, 1024, 1536, 2048, 2304, 2560, 2816, 3072
QKV_W = 1536
REST_W = IN_COLS - QKV_W

ADAM_LR, ADAM_B1, ADAM_B2, ADAM_EPS, ADAM_WD, ADAM_STEP = 0.001, 0.9, 0.999, 1e-08, 0.01, 10

V7X_VMEM_MIB = 64
VMEM_NO_STAGING_MIB = V7X_VMEM_MIB - 6


def _params(vmem_mib, sem=None):
    assert vmem_mib < V7X_VMEM_MIB
    return pltpu.CompilerParams(vmem_limit_bytes=vmem_mib << 20, dimension_semantics=sem)


def _dot(a, b):
    return jnp.dot(a.astype(BF16), b.astype(BF16), preferred_element_type=F32)


def _dot_nt(a, b):
    return lax.dot_general(a.astype(BF16), b.astype(BF16), (((1,), (1,)), ((), ())), preferred_element_type=F32)


def _dot_tn(a, b):
    return lax.dot_general(a.astype(BF16), b.astype(BF16), (((0,), (0,)), ((), ())), preferred_element_type=F32)


def _rstd(v):
    return lax.rsqrt(jnp.mean(v * v, axis=-1, keepdims=True) + EPS)


def _rms_bwd(v, r, g, dy):
    gdy = g * dy
    return r * gdy - v * (r * r * r * jnp.mean(gdy * v, axis=-1, keepdims=True))


def _sigmoid(z):
    return 1.0 / (1.0 + jnp.exp(-z))


def _silu_and_grad(z):
    s = _sigmoid(z)
    return z * s, s * (1.0 + z * (1.0 - s))


_G_C = 0.7978845608028654
_G_K = 0.044715


def _gelu_and_grad(v):
    t = jnp.tanh(_G_C * (v + _G_K * (v * v * v)))
    cdf = 0.5 * (1.0 + t)
    return v * cdf, cdf + 0.5 * v * (1.0 - t * t) * (_G_C * (1.0 + 3.0 * _G_K * v * v))


def _cast_rows(src_ref, dst_ref, rows, step=256):
    def one(i, carry):
        r = pl.ds(pl.multiple_of(i * step, step), step)
        dst_ref[r, :] = src_ref[r, :].astype(dst_ref.dtype)
        return carry
    lax.fori_loop(0, rows // step, one, 0)


def _left_lanes(rows):
    return lax.broadcasted_iota(jnp.int32, (rows, 128), 1) < HEAD


def _mesh_pos():
    return lax.axis_index("x"), lax.axis_index("y"), lax.axis_index("c")


def _peer(pos, k):
    x, y, c = pos
    return (1 - x if k & 4 else x, 1 - y if k & 2 else y, 1 - c if k & 1 else c)


def _flat(pos):
    return 4 * pos[0] + 2 * pos[1] + pos[2]


def _allgather_weights(w_in_t, w_kv, w_out):
    def body(win_ref, wkv_ref, wout_ref, wint_o, wkv_o, wout_o, send_sems, recv_sems):
        x, y, c = _mesh_pos()
        me, sib = (x, y, c), (x, y, 1 - c)
        chips = [(1 - x, y), (x, 1 - y), (1 - x, 1 - y)]

        def rows(p):
            return wint_o.at[pl.ds(pl.multiple_of(_flat(p) * W_IN_SHARD, 16), W_IN_SHARD), :]

        rows(me)[...] = win_ref[...].astype(BF16)

        def copy(k, block, to):
            return pltpu.make_async_remote_copy(
                src_ref=rows(block), dst_ref=rows(block), send_sem=send_sems.at[k], recv_sem=recv_sems.at[k],
                device_id=to, device_id_type=MESH)

        first = [copy(0, me, sib)] + [copy(1 + j, me, (*chip, c)) for j, chip in enumerate(chips)]
        for cp in first:
            cp.start()
        wkv_o[...] = wkv_ref[...].astype(BF16)
        wout_o[...] = wout_ref[...].astype(BF16)
        passed = []
        for j, chip in enumerate(chips):
            copy(1 + j, (*chip, c), me).wait_recv()
            fwd = copy(4 + j, (*chip, c), sib)
            fwd.start()
            passed.append(fwd)
        copy(0, sib, me).wait_recv()
        for j, chip in enumerate(chips):
            copy(4 + j, (*chip, 1 - c), me).wait_recv()
        for cp in first + passed:
            cp.wait_send()

    vmem = pl.BlockSpec(memory_space=pltpu.VMEM)
    return pl.pallas_call(
        body, name="allgather_weights",
        out_shape=(SDS((IN_COLS, D_MODEL), BF16), SDS(w_kv.shape, BF16), SDS(w_out.shape, BF16)),
        in_specs=[vmem, vmem, vmem], out_specs=(vmem, vmem, vmem),
        scratch_shapes=[pltpu.SemaphoreType.DMA((7,)), pltpu.SemaphoreType.DMA((7,))],
        compiler_params=_params(40),
    )(w_in_t, w_kv, w_out)


def _proj_fwd(x2, g_norm, wint):
    tm = 256
    d = DILATIONS[2][0]
    per_ex = SEQ // tm

    def body(x_ref, g_ref, w_ref, o_ref, o16_ref):
        xv = x_ref[...]
        h = xv * _rstd(xv) * g_ref[...]
        res = _dot_nt(h, w_ref[...])
        o_ref[...] = res
        r_out = lax.broadcasted_iota(jnp.int32, (tm, tm), 0)
        r_in = lax.broadcasted_iota(jnp.int32, (tm, tm), 1)
        pick = (r_in == d * (r_out % (tm // d)) + r_out // (tm // d)).astype(BF16)
        grouped = _dot(pick, res[:, 0:QKV_W]).astype(BF16)
        for rho in range(d):
            o16_ref[0, rho] = grouped[rho * (tm // d):(rho + 1) * (tm // d), :]

    return pl.pallas_call(
        body, name="proj_fwd", grid=(T_LOC // tm,),
        in_specs=[pl.BlockSpec((tm, D_MODEL), lambda i: (i, 0)), pl.BlockSpec((1, D_MODEL), lambda i: (0, 0)),
                  pl.BlockSpec((IN_COLS, D_MODEL), lambda i: (0, 0))],
        out_specs=(pl.BlockSpec((tm, IN_COLS), lambda i: (i, 0)),
                   pl.BlockSpec((1, d, tm // d, QKV_W), lambda i: (i // per_ex, 0, i % per_ex, 0))),
        out_shape=(SDS((T_LOC, IN_COLS), F32), SDS((B_LOC, d, SEQ // d, QKV_W), BF16)),
        compiler_params=_params(48, ("arbitrary",)),
    )(x2, g_norm, wint)


def _memkv_fwd(mem2, g_mem, wkv):
    def body(m_ref, g_ref, w_ref, o_ref):
        mv = m_ref[...]
        o_ref[...] = _dot(mv * _rstd(mv) * g_ref[...], w_ref[...])

    return pl.pallas_call(
        body, name="memkv_fwd", out_shape=SDS((B_LOC * N_MEM, 2 * MEM_W), F32), compiler_params=_params(32),
    )(mem2, g_mem, wkv)


N_BIAS = 7


def _fill_bias_tables(sl_ref, tab):
    for cfg, (d, length) in enumerate(DILATIONS):
        nk = min(length, 2 * CHUNK)
        r = lax.broadcasted_iota(jnp.int32, (CHUNK, nk), 0)
        c = lax.broadcasted_iota(jnp.int32, (CHUNK, nk), 1)
        for var in range(3 if length > nk else 1):
            rel = jnp.abs(r - c + var * RADIUS)
            dist = rel.astype(F32) * float(d)
            for h in range(2):
                slope = sl_ref[0, 0:1, h * HEAD:h * HEAD + 1]
                tab[3 * cfg + var, h * CHUNK:(h + 1) * CHUNK, 0:nk] = jnp.where(rel <= RADIUS, -slope * dist, NEG)


def _attn_blocks(visit, unroll):
    def step(t, carry):
        for cfg, (d, length) in enumerate(DILATIONS):
            nblk = length // CHUNK
            if nblk == 1:
                visit(cfg, 0, t, t, length, t)
                continue
            rho, i = (0, t) if d == 1 else (t // nblk, t % nblk)
            ks = jnp.clip(i * CHUNK - RADIUS, 0, length - 2 * CHUNK)
            visit(cfg, (i * CHUNK - ks) // RADIUS, rho + d * (i * CHUNK), rho + d * ks, 2 * CHUNK, t)
        return carry
    lax.fori_loop(0, 16, step, 0, unroll=unroll)


def _stack_heads(v, left):
    return jnp.concatenate([jnp.where(left, v, 0.0), jnp.where(left, 0.0, v)], axis=0)


def _unstack_heads(v, left):
    return jnp.where(left, v[0:CHUNK], v[CHUNK:2 * CHUNK])


def _rows(start, n, d):
    return pl.ds(start, n) if d == 1 else pl.ds(start, n, stride=d)


def _blk16(col0):
    d, length = DILATIONS[2]
    return pl.BlockSpec((1, d, length, 128), lambda b, hp: (b, 0, 0, col0 // 128 + hp))


def _attn_fwd(proj, qkv16, slopes):
    def body(sl_ref, q_ref, k_ref, v_ref, q16_ref, k16_ref, v16_ref, a_ref, lse_ref, *scr):
        o_c, m_c, l_c, tab = scr[0:3], scr[3:6], scr[6:9], scr[9]
        left = _left_lanes(CHUNK)
        _fill_bias_tables(sl_ref, tab)

        def block(cfg, var, q0, k0, nk, t):
            d = DILATIONS[cfg][0]
            rq, rk = _rows(q0, CHUNK, d), _rows(k0, nk, d)
            if cfg == 2:
                qb, kw, vw = q16_ref[0, t].astype(F32), k16_ref[0, t], v16_ref[0, t]
            else:
                qb, kw, vw = q_ref[rq, :], k_ref[rk, :], v_ref[rk, :]
            qs = _stack_heads(qb * SCALE, left)
            s = _dot_nt(qs, kw) + tab[3 * cfg + var, :, 0:nk]
            m = jnp.max(s, axis=-1, keepdims=True)
            p = jnp.exp(s - m)
            o_c[cfg][rq, :] = _unstack_heads(_dot(p, vw), left)
            m_c[cfg][rq, :] = _unstack_heads(m, left)
            l_c[cfg][rq, :] = _unstack_heads(jnp.sum(p, axis=-1, keepdims=True), left)
        _attn_blocks(block, 8)

        def merge(j, carry):
            rows = pl.ds(pl.multiple_of(j * 256, 256), 256)
            ms = [m_c[i][rows, :] for i in range(3)]
            top = jnp.maximum(jnp.maximum(ms[0], ms[1]), ms[2])
            ws = [jnp.exp(m - top) for m in ms]
            den = l_c[0][rows, :] * ws[0] + l_c[1][rows, :] * ws[1] + l_c[2][rows, :] * ws[2]
            num = o_c[0][rows, :] * ws[0] + o_c[1][rows, :] * ws[1] + o_c[2][rows, :] * ws[2]
            a_ref[rows, :] = num / den
            lse_ref[rows, :] = top + jnp.log(den)
            return carry
        lax.fori_loop(0, SEQ // 256, merge, 0)

    blk = lambda col0: pl.BlockSpec((SEQ, 128), lambda b, hp: (b, col0 // 128 + hp))
    out = pl.BlockSpec((SEQ, 128), lambda b, hp: (b, hp))
    return pl.pallas_call(
        body, name="attn_fwd", grid=(B_LOC, 4),
        in_specs=[pl.BlockSpec((1, 8, 128), lambda b, hp: (hp, 0, 0)), blk(C_QA), blk(C_KA), blk(C_VA),
                  _blk16(C_QA), _blk16(C_KA), _blk16(C_VA)],
        out_specs=(out, out),
        out_shape=(SDS((T_LOC, ATTN_W), F32), SDS((T_LOC, ATTN_W), F32)),
        scratch_shapes=[pltpu.VMEM((SEQ, 128), F32)] * 9 + [pltpu.VMEM((N_BIAS, 2 * CHUNK, 2 * CHUNK), F32)],
        compiler_params=_params(40, ("arbitrary", "arbitrary")),
    )(slopes, proj, proj, proj, qkv16, qkv16, qkv16)


def _chunks_side_by_side(v, pr, tm):
    return jnp.concatenate([v[ch * CHUNK:(ch + 1) * CHUNK, pr * 128:(pr + 1) * 128] for ch in range(tm // CHUNK)], axis=1)


def _first_group_lanes(tm):
    return lax.broadcasted_iota(jnp.int32, (CHUNK, tm), 1) % 128 < HEAD


def _store_chunks(dst_ref, pr, val, tm):
    for ch in range(tm // CHUNK):
        dst_ref[ch * CHUNK:(ch + 1) * CHUNK, pr * 128:(pr + 1) * 128] = val[:, ch * CHUNK:(ch + 1) * CHUNK]


def _sgu_mix(vn, ws_ref, dst_ref, tm):
    first = _first_group_lanes(tm)
    for pr in range(2):
        vp = _chunks_side_by_side(vn, pr, tm)
        _store_chunks(dst_ref, pr, jnp.where(first, _dot(ws_ref[2 * pr], vp), _dot(ws_ref[2 * pr + 1], vp)), tm)


def _mem_head_of_lane(rows):
    return lax.broadcasted_iota(jnp.int32, (rows, MEM_W), 1) // HEAD


def _stack_mem_heads(v, rows):
    head = _mem_head_of_lane(rows)
    return jnp.concatenate([jnp.where(head == h, v, 0.0) for h in range(4)], axis=0)


def _unstack_mem_heads(v, rows):
    head = _mem_head_of_lane(rows)
    out = v[0:rows]
    for h in range(1, 4):
        out = jnp.where(head == h, v[h * rows:(h + 1) * rows], out)
    return out


def _mem_attn_probs(q, kmem, rows):
    qs = _stack_mem_heads(q, rows).astype(BF16)
    s = _dot_nt(qs, kmem) * SCALE
    e = jnp.exp(s - jnp.max(s, axis=-1, keepdims=True))
    return e * (1.0 / jnp.sum(e, axis=-1, keepdims=True)), qs


def _branch_blocks(tm):
    col = lambda w, c0: pl.BlockSpec((tm, w), lambda i: (i, c0 // w))
    return [col(512, C_ZA), col(256, C_UB), col(256, C_VB), col(256, C_ZB), col(256, C_QM), col(256, C_ZM)]


def _branch_fwd(proj, a, kv, w_s, b_exp, g_v):
    tm = 512
    per_ex = SEQ // tm

    def body(za_ref, ub_ref, vb_ref, zb_ref, qm_ref, zm_ref, a_ref, kv_ref, ws_ref, be_ref, gv_ref, o_ref, mix):
        o_ref[:, 0:ATTN_W] = (_silu_and_grad(za_ref[...])[0] * a_ref[...]).astype(BF16)
        gu = _gelu_and_grad(ub_ref[...])[0]
        gv = _gelu_and_grad(vb_ref[...])[0]
        vn = gv * _rstd(gv) * gv_ref[...]
        _sgu_mix(vn.astype(BF16), ws_ref, mix, tm)
        sg = gu * (mix[...] + jnp.concatenate([be_ref[...]] * (tm // CHUNK), axis=0))
        o_ref[:, ATTN_W:ATTN_W + SGU_W] = (_silu_and_grad(zb_ref[...])[0] * sg).astype(BF16)
        p = _mem_attn_probs(qm_ref[...], kv_ref[:, 0:MEM_W], tm)[0]
        mo = _unstack_mem_heads(_dot(p, kv_ref[:, MEM_W:2 * MEM_W]), tm)
        o_ref[:, ATTN_W + SGU_W:D_MODEL] = (_silu_and_grad(zm_ref[...])[0] * mo).astype(BF16)

    full = lambda shape: pl.BlockSpec(shape, lambda i: (0,) * len(shape))
    return pl.pallas_call(
        body, name="branch_fwd", grid=(T_LOC // tm,),
        in_specs=_branch_blocks(tm) + [
            pl.BlockSpec((tm, ATTN_W), lambda i: (i, 0)), pl.BlockSpec((N_MEM, 2 * MEM_W), lambda i: (i // per_ex, 0)),
            full((4, CHUNK, CHUNK)), full((CHUNK, SGU_W)), full((1, SGU_W))],
        out_specs=pl.BlockSpec((tm, D_MODEL), lambda i: (i, 0)),
        out_shape=SDS((T_LOC, D_MODEL), BF16),
        scratch_shapes=[pltpu.VMEM((tm, SGU_W), F32)],
        compiler_params=_params(VMEM_NO_STAGING_MIB, ("arbitrary",)),
    )(proj, proj, proj, proj, proj, proj, a, kv, w_s, b_exp, g_v)


def _outproj_loss(gated, wout, x2, tgt2, g_final):
    tm = 512

    def body(g_ref, w_ref, x_ref, t_ref, gf_ref, dh2_ref, loss_ref, dgf_ref, dwo_ref, dwo_acc):
        @pl.when(pl.program_id(0) == 0)
        def _():
            loss_ref[...] = jnp.zeros_like(loss_ref)
            dgf_ref[...] = jnp.zeros_like(dgf_ref)
            dwo_acc[...] = jnp.zeros_like(dwo_acc)
        gated = g_ref[...]
        h2 = x_ref[...] + _dot(gated, w_ref[...])
        r = _rstd(h2)
        gf = gf_ref[...]
        err = h2 * r * gf - t_ref[...]
        loss_ref[...] += 0.5 * jnp.sum(jnp.mean(err * err, axis=-1, keepdims=True))
        dy = err * (1.0 / D_MODEL)
        dh2 = _rms_bwd(h2, r, gf, dy)
        dh2_ref[...] = dh2
        dgf_ref[...] += jnp.sum(dy * (h2 * r), axis=0, keepdims=True)
        dwo_acc[...] += _dot_tn(gated, dh2)

        @pl.when(pl.program_id(0) == pl.num_programs(0) - 1)
        def _():
            _cast_rows(dwo_acc, dwo_ref, D_MODEL)

    row = pl.BlockSpec((tm, D_MODEL), lambda i: (i, 0))
    vec = pl.BlockSpec((1, D_MODEL), lambda i: (0, 0))
    square = pl.BlockSpec((D_MODEL, D_MODEL), lambda i: (0, 0))
    return pl.pallas_call(
        body, name="outproj_loss", grid=(T_LOC // tm,),
        in_specs=[row, square, row, row, vec],
        out_specs=(row, pl.BlockSpec((8, 128), lambda i: (0, 0)), vec, square),
        out_shape=(SDS((T_LOC, D_MODEL), F32), pltpu.HBM((8, 128), F32), pltpu.HBM((1, D_MODEL), F32),
                   pltpu.HBM((D_MODEL, D_MODEL), BF16)),
        scratch_shapes=[pltpu.VMEM((D_MODEL, D_MODEL), F32)],
        compiler_params=_params(VMEM_NO_STAGING_MIB, ("arbitrary",)),
    )(gated, wout, x2, tgt2, g_final)


def _branch_bwd(dh2, wout, proj, a, kv, w_s, b_exp, g_v):
    tm = 512
    per_ex = SEQ // tm

    def body(dh2_ref, w_ref, za_ref, ub_ref, vb_ref, zb_ref, qm_ref, zm_ref, a_ref, kv_ref, ws_ref,
             be_ref, gv_ref, da_ref, dr_ref, dkv_ref, dws_ref, db_ref, dgv_ref, mix, dvn, dmsum):
        i = pl.program_id(0)

        @pl.when(i == 0)
        def _():
            dws_ref[...] = jnp.zeros_like(dws_ref)
            dgv_ref[...] = jnp.zeros_like(dgv_ref)
            dmsum[...] = jnp.zeros_like(dmsum)

        @pl.when(i % per_ex == 0)
        def _():
            dkv_ref[...] = jnp.zeros_like(dkv_ref)

        dg = _dot_nt(dh2_ref[...], w_ref[...])

        sa, dsa = _silu_and_grad(za_ref[...])
        dga = dg[:, 0:ATTN_W]
        da_ref[...] = dga * sa
        dr_ref[:, 0:512] = (dga * a_ref[...] * dsa).astype(BF16)

        ub, vb = ub_ref[...], vb_ref[...]
        gu, dgu = _gelu_and_grad(ub)
        gv, dgv = _gelu_and_grad(vb)
        rv = _rstd(gv)
        gain = gv_ref[...]
        vn = (gv * rv * gain).astype(BF16)
        _sgu_mix(vn, ws_ref, mix, tm)
        mixed = mix[...] + jnp.concatenate([be_ref[...]] * (tm // CHUNK), axis=0)
        sb, dsb = _silu_and_grad(zb_ref[...])
        dgb = dg[:, ATTN_W:ATTN_W + SGU_W]
        dsg = dgb * sb
        dr_ref[:, 512:768] = (dsg * mixed * dgu).astype(BF16)
        dr_ref[:, 1024:1280] = (dgb * (gu * mixed) * dsb).astype(BF16)
        dmix = dsg * gu
        for ch in range(tm // CHUNK):
            dmsum[...] += dmix[ch * CHUNK:(ch + 1) * CHUNK, :]
        first = _first_group_lanes(tm)
        for pr in range(2):
            dmp, vp = _chunks_side_by_side(dmix, pr, tm), _chunks_side_by_side(vn, pr, tm)
            dws_ref[2 * pr] += _dot_nt(jnp.where(first, dmp, 0.0), vp)
            dws_ref[2 * pr + 1] += _dot_nt(jnp.where(first, 0.0, dmp), vp)
            _store_chunks(dvn, pr, jnp.where(first, _dot_tn(ws_ref[2 * pr], dmp), _dot_tn(ws_ref[2 * pr + 1], dmp)), tm)
        dvn_v = dvn[...]
        dgv_ref[...] += jnp.sum(dvn_v * (gv * rv), axis=0, keepdims=True)
        dr_ref[:, 768:1024] = (_rms_bwd(gv, rv, gain, dvn_v) * dgv).astype(BF16)

        szm, dszm = _silu_and_grad(zm_ref[...])
        dgm = dg[:, ATTN_W + SGU_W:D_MODEL]
        kmem, vmem_ = kv_ref[:, 0:MEM_W].astype(BF16), kv_ref[:, MEM_W:2 * MEM_W].astype(BF16)
        p, qs = _mem_attn_probs(qm_ref[...], kmem, tm)
        dmos = _stack_mem_heads(dgm * szm, tm).astype(BF16)
        dr_ref[:, 1536:1792] = (dgm * _unstack_mem_heads(_dot(p, vmem_), tm) * dszm).astype(BF16)
        dp = _dot_nt(dmos, vmem_)
        ds = (p * (dp - jnp.sum(p * dp, axis=-1, keepdims=True)) * SCALE).astype(BF16)
        dr_ref[:, 1280:1536] = _unstack_mem_heads(_dot(ds, kmem), tm).astype(BF16)
        dkv_ref[:, 0:MEM_W] += _dot_tn(ds, qs)
        dkv_ref[:, MEM_W:2 * MEM_W] += _dot_tn(p, dmos)

        @pl.when(i == pl.num_programs(0) - 1)
        def _():
            tot = dmsum[...]
            hi = tot.astype(BF16)
            lo = (tot - hi.astype(F32)).astype(BF16)
            grp = (lax.broadcasted_iota(jnp.int32, (SGU_W, 128), 0) // HEAD
                   == lax.broadcasted_iota(jnp.int32, (SGU_W, 128), 1)).astype(BF16)
            db_ref[...] = (_dot(hi, grp) + _dot(lo, grp)).T[0:4, :]

    full = lambda shape: pl.BlockSpec(shape, lambda i: (0,) * len(shape))
    row = lambda w: pl.BlockSpec((tm, w), lambda i: (i, 0))
    return pl.pallas_call(
        body, name="branch_bwd", grid=(T_LOC // tm,),
        in_specs=[row(D_MODEL), full((D_MODEL, D_MODEL))] + _branch_blocks(tm) + [
            row(ATTN_W), pl.BlockSpec((N_MEM, 2 * MEM_W), lambda i: (i // per_ex, 0)),
            full((4, CHUNK, CHUNK)), full((CHUNK, SGU_W)), full((1, SGU_W))],
        out_specs=(row(ATTN_W), row(REST_W), pl.BlockSpec((N_MEM, 2 * MEM_W), lambda i: (i // per_ex, 0)),
                   full((4, CHUNK, CHUNK)), full((4, CHUNK)), full((1, SGU_W))),
        out_shape=(SDS((T_LOC, ATTN_W), F32), SDS((T_LOC, REST_W), BF16), SDS((B_LOC * N_MEM, 2 * MEM_W), F32),
                   pltpu.HBM((4, CHUNK, CHUNK), F32), pltpu.HBM((4, CHUNK), F32), pltpu.HBM((1, SGU_W), F32)),
        scratch_shapes=[pltpu.VMEM((tm, SGU_W), F32), pltpu.VMEM((tm, SGU_W), F32), pltpu.VMEM((CHUNK, SGU_W), F32)],
        compiler_params=_params(56, ("arbitrary",)),
    )(dh2, wout, proj, proj, proj, proj, proj, proj, a, kv, w_s, b_exp, g_v)


def _attn_bwd(proj, qkv16, slopes, da, a, lse):
    def body(sl_ref, q_ref, k_ref, v_ref, q16_ref, k16_ref, v16_ref, da_ref, a_ref, lse_ref,
             dq_ref, dk_ref, dv_ref, *scr):
        dq_s, dk_s, dv_s, tab = scr[0:3], scr[3:6], scr[6:9], scr[9]
        lse_h, delta_h = scr[10:12], scr[12:14]
        p_all, ds_all = scr[14], scr[15]
        left = _left_lanes(CHUNK)
        _fill_bias_tables(sl_ref, tab)

        def prep(j, carry):
            rows = pl.ds(pl.multiple_of(j * 256, 256), 256)
            l256 = _left_lanes(256)
            prod = da_ref[rows, :] * a_ref[rows, :]
            delta_h[0][rows, :] = jnp.broadcast_to(jnp.sum(jnp.where(l256, prod, 0.0), axis=-1, keepdims=True), (256, 128))
            delta_h[1][rows, :] = jnp.broadcast_to(jnp.sum(jnp.where(l256, 0.0, prod), axis=-1, keepdims=True), (256, 128))
            pair = lse_ref[rows, :]
            other = pltpu.roll(pair, HEAD, axis=1)
            lse_h[0][rows, :] = jnp.where(l256, pair, other)
            lse_h[1][rows, :] = jnp.where(l256, other, pair)
            zero = jnp.zeros((256, 128), F32)
            for cfg in range(3):
                dk_s[cfg][rows, :] = zero
                dv_s[cfg][rows, :] = zero
            return carry
        lax.fori_loop(0, SEQ // 256, prep, 0)

        def per_row(halves, rq, nk):
            v = jnp.concatenate([halves[0][rq, :], halves[1][rq, :]], axis=0)
            return v if nk == 128 else jnp.concatenate([v, v], axis=1)

        def qkv(cfg, rq, rk, t):
            if cfg == 2:
                return q16_ref[0, t].astype(F32), k16_ref[0, t], v16_ref[0, t]
            return q_ref[rq, :], k_ref[rk, :], v_ref[rk, :]

        def probs(cfg, var, q0, k0, nk, t):
            d = DILATIONS[cfg][0]
            rq, rk = _rows(q0, CHUNK, d), _rows(k0, nk, d)
            qb, kw, vw = qkv(cfg, rq, rk, t)
            qs = _stack_heads(qb * SCALE, left)
            das = _stack_heads(da_ref[rq, :], left)
            s = _dot_nt(qs, kw) + tab[3 * cfg + var, :, 0:nk]
            p = jnp.exp(s - per_row(lse_h, rq, nk))
            p_all[16 * cfg + t, :, 0:nk] = p.astype(BF16)
            ds_all[16 * cfg + t, :, 0:nk] = (p * (_dot_nt(das, vw) - per_row(delta_h, rq, nk))).astype(BF16)
        _attn_blocks(probs, 4)

        def grads(cfg, var, q0, k0, nk, t):
            d = DILATIONS[cfg][0]
            rq, rk = _rows(q0, CHUNK, d), _rows(k0, nk, d)
            qb, kw, _ = qkv(cfg, rq, rk, t)
            qs = _stack_heads(qb * SCALE, left).astype(BF16)
            das = _stack_heads(da_ref[rq, :], left).astype(BF16)
            p, ds = p_all[16 * cfg + t, :, 0:nk], ds_all[16 * cfg + t, :, 0:nk]
            dq_s[cfg][rq, :] = _unstack_heads(_dot(ds, kw), left) * SCALE
            dk_s[cfg][rk, :] += _dot_tn(ds, qs)
            dv_s[cfg][rk, :] += _dot_tn(p, das)
        _attn_blocks(grads, 4)

        def flush(j, carry):
            rows = pl.ds(pl.multiple_of(j * 256, 256), 256)
            for acc, dst in ((dq_s, dq_ref), (dk_s, dk_ref), (dv_s, dv_ref)):
                dst[rows, :] = (acc[0][rows, :] + acc[1][rows, :] + acc[2][rows, :]).astype(BF16)
            return carry
        lax.fori_loop(0, SEQ // 256, flush, 0)

    blk = lambda col0: pl.BlockSpec((SEQ, 128), lambda b, hp: (b, col0 // 128 + hp))
    own = pl.BlockSpec((SEQ, 128), lambda b, hp: (b, hp))
    return pl.pallas_call(
        body, name="attn_bwd", grid=(B_LOC, 4),
        in_specs=[pl.BlockSpec((1, 8, 128), lambda b, hp: (hp, 0, 0)), blk(C_QA), blk(C_KA), blk(C_VA),
                  _blk16(C_QA), _blk16(C_KA), _blk16(C_VA), own, own, own],
        out_specs=(own, own, own),
        out_shape=(SDS((T_LOC, ATTN_W), BF16),) * 3,
        scratch_shapes=[pltpu.VMEM((SEQ, 128), F32)] * 9 + [pltpu.VMEM((N_BIAS, 2 * CHUNK, 2 * CHUNK), F32)]
        + [pltpu.VMEM((SEQ, 128), F32)] * 4 + [pltpu.VMEM((48, 2 * CHUNK, 2 * CHUNK), BF16)] * 2,
        compiler_params=_params(VMEM_NO_STAGING_MIB, ("arbitrary", "arbitrary")),
    )(slopes, proj, proj, proj, qkv16, qkv16, qkv16, da, a, lse)


def _dproj_specs(tm):
    third = pl.BlockSpec((tm, ATTN_W), lambda i: (i, 0))
    return [third, third, third, pl.BlockSpec((tm, REST_W), lambda i: (i, 0))]


def _dx(dq, dk, dv, dr, wint, x2, dh2, g_norm):
    tm = 512

    def body(dq_ref, dk_ref, dv_ref, dr_ref, w_ref, x_ref, dh2_ref, g_ref, gx_ref, dgn_ref):
        @pl.when(pl.program_id(0) == 0)
        def _():
            dgn_ref[...] = jnp.zeros_like(dgn_ref)
        dh = (_dot(dq_ref[...], w_ref[C_QA:C_KA, :]) + _dot(dk_ref[...], w_ref[C_KA:C_VA, :])
              + _dot(dv_ref[...], w_ref[C_VA:C_ZA, :]) + _dot(dr_ref[...], w_ref[C_ZA:IN_COLS, :]))
        xv = x_ref[...]
        r = _rstd(xv)
        gx_ref[...] = dh2_ref[...] + _rms_bwd(xv, r, g_ref[...], dh)
        dgn_ref[...] += jnp.sum(dh * (xv * r), axis=0, keepdims=True)

    row = pl.BlockSpec((tm, D_MODEL), lambda i: (i, 0))
    vec = pl.BlockSpec((1, D_MODEL), lambda i: (0, 0))
    return pl.pallas_call(
        body, name="dx", grid=(T_LOC // tm,),
        in_specs=_dproj_specs(tm) + [pl.BlockSpec((IN_COLS, D_MODEL), lambda i: (0, 0)), row, row, vec],
        out_specs=(row, vec),
        out_shape=(SDS((T_LOC, D_MODEL), F32), SDS((1, D_MODEL), F32)),
        compiler_params=_params(48, ("arbitrary",)),
    )(dq, dk, dv, dr, wint, x2, dh2, g_norm)


def _dwin(dq, dk, dv, dr, x2, g_norm, half):
    tm = 1024
    width = D_MODEL // 2
    cols = slice(half * width, (half + 1) * width)

    def body(dq_ref, dk_ref, dv_ref, dr_ref, x_ref, g_ref, o_ref, acc):
        @pl.when(pl.program_id(0) == 0)
        def _():
            acc[...] = jnp.zeros_like(acc)
        xv = x_ref[...]
        h = (xv[:, cols] * _rstd(xv) * g_ref[:, cols]).astype(BF16)
        acc[C_QA:C_KA, :] += _dot_tn(dq_ref[...], h)
        acc[C_KA:C_VA, :] += _dot_tn(dk_ref[...], h)
        acc[C_VA:C_ZA, :] += _dot_tn(dv_ref[...], h)
        acc[C_ZA:IN_COLS, :] += _dot_tn(dr_ref[...], h)

        @pl.when(pl.program_id(0) == pl.num_programs(0) - 1)
        def _():
            _cast_rows(acc, o_ref, IN_COLS)

    return pl.pallas_call(
        body, name="dwin%d" % half, grid=(T_LOC // tm,),
        in_specs=_dproj_specs(tm) + [pl.BlockSpec((tm, D_MODEL), lambda i: (i, 0)), pl.BlockSpec((1, D_MODEL), lambda i: (0, 0))],
        out_specs=pl.BlockSpec((IN_COLS, width), lambda i: (0, 0)),
        out_shape=pltpu.HBM((IN_COLS, width), BF16),
        scratch_shapes=[pltpu.VMEM((IN_COLS, width), F32)],
        compiler_params=_params(VMEM_NO_STAGING_MIB, ("arbitrary",)),
    )(dq, dk, dv, dr, x2, g_norm)


def _memkv_bwd(dkv, mem2, g_mem, wkv):
    def body(dkv_ref, m_ref, g_ref, w_ref, dw_ref, dg_ref):
        mv = m_ref[...]
        r = _rstd(mv)
        dkv_v = dkv_ref[...].astype(BF16)
        dw_ref[...] = _dot_tn(mv * r * g_ref[...], dkv_v).astype(BF16)
        dg_ref[...] = jnp.sum(_dot_nt(dkv_v, w_ref[...]) * (mv * r), axis=0, keepdims=True)

    return pl.pallas_call(
        body, name="memkv_bwd", out_shape=(SDS((D_MODEL, 2 * MEM_W), BF16), SDS((1, D_MODEL), F32)),
        compiler_params=_params(32),
    )(dkv, mem2, g_mem, wkv)


def _allreduce_small(parts):
    n = len(parts)

    def body(*refs):
        ins, outs, bufs = refs[0:n], refs[n:2 * n], refs[2 * n:3 * n]
        send_sems, recv_sems = refs[3 * n], refs[3 * n + 1]
        pos = _mesh_pos()
        me = _flat(pos)
        for a in range(n):
            bufs[a][me] = ins[a][...]

        def copy(a, k, slot):
            return pltpu.make_async_remote_copy(
                src_ref=ins[a], dst_ref=bufs[a].at[slot],
                send_sem=send_sems.at[7 * a + k - 1], recv_sem=recv_sems.at[7 * a + k - 1],
                device_id=_peer(pos, k), device_id_type=MESH)

        sent = [copy(a, k, me) for a in range(n) for k in range(1, N_DEV)]
        for cp in sent:
            cp.start()
        for a in range(n):
            for k in range(1, N_DEV):
                copy(a, k, _flat(_peer(pos, k))).wait_recv()
        for cp in sent:
            cp.wait_send()
        for a in range(n):
            acc = bufs[a][0]
            for s in range(1, N_DEV):
                acc = acc + bufs[a][s]
            outs[a][...] = acc

    vmem = pl.BlockSpec(memory_space=pltpu.VMEM)
    return pl.pallas_call(
        body, name="allreduce_small",
        out_shape=tuple(SDS(p.shape, F32) for p in parts),
        in_specs=[vmem] * n, out_specs=(vmem,) * n,
        scratch_shapes=[pltpu.VMEM((N_DEV,) + p.shape, F32) for p in parts]
        + [pltpu.SemaphoreType.DMA((7 * n,)), pltpu.SemaphoreType.DMA((7 * n,))],
        compiler_params=_params(16),
    )(*parts)


_HBM = pl.BlockSpec(memory_space=pltpu.HBM)
_SEM = pl.BlockSpec(memory_space=pltpu.SEMAPHORE)
_SIDE_EFFECT = pltpu.SideEffectType.DATAFLOW_SIDE_EFFECTING


def _exchange_copies(src_refs, land_refs, scatter, send_sems, recv_sems):
    pos = _mesh_pos()
    copies = []
    for a, (src, land) in enumerate(zip(src_refs, land_refs)):
        n = land.shape[1]
        for k in range(1, N_DEV):
            peer = _peer(pos, k)
            piece = src.at[pl.ds(pl.multiple_of(_flat(peer) * n, 16), n), :] if scatter[a] else src
            copies.append(pltpu.make_async_remote_copy(
                src_ref=piece, dst_ref=land.at[_flat(pos)],
                send_sem=send_sems.at[7 * a + k - 1], recv_sem=recv_sems.at[7 * a + k - 1],
                device_id=peer, device_id_type=MESH))
    return copies


def _exchange_start(name, srcs, scatter, lands):
    n = len(srcs)

    def body(*refs):
        for cp in _exchange_copies(refs[0:n], refs[n:2 * n], scatter, refs[2 * n], refs[2 * n + 1]):
            cp.start()
        refs[-1][...] = jnp.zeros_like(refs[-1])

    ops = [pltpu.with_memory_space_constraint(t, pltpu.HBM) for t in (*srcs, *lands)]
    out = pl.pallas_call(
        body, name=name,
        out_shape=(pltpu.SemaphoreType.DMA((7 * n,)), pltpu.SemaphoreType.DMA((7 * n,)),
                   *[pltpu.HBM(t.shape, t.dtype) for t in ops], SDS((8, 128), F32)),
        in_specs=[_HBM] * (2 * n),
        out_specs=(_SEM, _SEM, *[_HBM] * (2 * n), pl.BlockSpec(memory_space=pltpu.VMEM)),
        input_output_aliases={i: 2 + i for i in range(2 * n)},
        compiler_params=pltpu.CompilerParams(has_side_effects=_SIDE_EFFECT),
    )(*ops)
    return out[0], out[1], out[2:2 + n], out[2 + n:2 + 2 * n], out[-1]


def _exchange_wait(name, started, scatter, after):
    send_sems, recv_sems, srcs, lands, _ = started
    n = len(srcs)

    def body(*refs):
        for cp in _exchange_copies(refs[0:n], refs[n:2 * n], scatter, refs[2 * n], refs[2 * n + 1]):
            cp.wait_send()
            cp.wait_recv()

    out = pl.pallas_call(
        body, name=name,
        out_shape=tuple(pltpu.HBM(t.shape, t.dtype) for t in (*srcs, *lands)),
        in_specs=[_HBM] * (2 * n) + [_SEM, _SEM, pl.BlockSpec(memory_space=pl.ANY)],
        out_specs=(_HBM,) * (2 * n),
        input_output_aliases={i: i for i in range(2 * n)},
        compiler_params=pltpu.CompilerParams(has_side_effects=_SIDE_EFFECT),
    )(*srcs, *lands, send_sems, recv_sems, after)
    return out[n:]


def _landing(own, me):
    return lax.dynamic_update_slice(lax.empty((N_DEV,) + own.shape, own.dtype), own[None], (me,) + (0,) * own.ndim)


def _adamw(w, g, m, v):
    m = ADAM_B1 * m + (1.0 - ADAM_B1) * g
    v = ADAM_B2 * v + (1.0 - ADAM_B2) * (g * g)
    m_hat = m / (1.0 - ADAM_B1 ** ADAM_STEP)
    v_hat = v / (1.0 - ADAM_B2 ** ADAM_STEP)
    return -ADAM_LR * (m_hat / (jnp.sqrt(v_hat) + ADAM_EPS) + ADAM_WD * w), m, v


def _adam_slots(name, me, pieces, partials, w, m, v):
    rows, cols = w.shape
    k = len(pieces)
    starts = [sum(p.shape[2] for p in pieces[:i]) for i in range(k + 1)]
    assert starts[-1] == cols and all(p.shape[1] == rows for p in pieces)

    def body(me_ref, *refs):
        s_refs, own_refs, (w_ref, m_ref, v_ref, g_o, d_o, m_o, v_o, acc) = refs[:k], refs[k:2 * k], refs[2 * k:]
        s = pl.program_id(0)
        terms = [jnp.where(s == me_ref[0], own[...], slot[0]).astype(F32) for slot, own in zip(s_refs, own_refs)]

        @pl.when(s == 0)
        def _():
            for i, t in enumerate(terms):
                acc[:, starts[i]:starts[i + 1]] = t

        @pl.when(s > 0)
        def _():
            for i, t in enumerate(terms):
                acc[:, starts[i]:starts[i + 1]] += t

        @pl.when(s == N_DEV - 1)
        def _():
            g = acc[...]
            g_o[...] = g
            d_o[...], m_o[...], v_o[...] = _adamw(w_ref[...], g, m_ref[...], v_ref[...])

    full = pl.BlockSpec((rows, cols), lambda s, me_ref: (0, 0))
    return pl.pallas_call(
        body, name="adam_" + name,
        grid_spec=pltpu.PrefetchScalarGridSpec(
            num_scalar_prefetch=1, grid=(N_DEV,),
            in_specs=[pl.BlockSpec((1, rows, p.shape[2]), lambda s, me_ref: (s, 0, 0)) for p in pieces]
            + [pl.BlockSpec((rows, p.shape[2]), lambda s, me_ref: (me_ref[0], 0)) for p in pieces] + [full, full, full],
            out_specs=(full,) * 4,
            scratch_shapes=[pltpu.VMEM((rows, cols), F32)]),
        out_shape=(SDS((rows, cols), F32),) * 4,
        compiler_params=_params(40, ("arbitrary",)),
    )(me, *pieces, *partials, w, m, v)


def _adam_small(me, ws, gs, ms, vs, loss):
    n = len(ws)
    flat, where_ = [], []
    for g in list(gs) + [loss]:
        where_.append((len(flat), isinstance(g, tuple)))
        flat += list(g) if isinstance(g, tuple) else [g]

    def body(me_ref, *refs):
        w_r, m_r, v_r = refs[0:n], refs[n:2 * n], refs[2 * n:3 * n]
        g_r, outs = refs[3 * n:3 * n + len(flat)], refs[3 * n + len(flat):]

        def total(i):
            at, slotted = where_[i]
            if not slotted:
                return g_r[at][...]
            own = g_r[at + 1][...]
            acc = jnp.where(me_ref[0] == 0, own, g_r[at][0])
            for s in range(1, N_DEV):
                acc = acc + jnp.where(me_ref[0] == s, own, g_r[at][s])
            return acc

        for a in range(n):
            g = total(a)
            outs[a][...] = g
            outs[n + 1 + 3 * a][...], outs[n + 2 + 3 * a][...], outs[n + 3 + 3 * a][...] = _adamw(
                w_r[a][...], g, m_r[a][...], v_r[a][...])
        outs[n][...] = total(n)

    vmem = pl.BlockSpec(memory_space=pltpu.VMEM)
    out = pl.pallas_call(
        body, name="adam_small",
        in_specs=[pl.BlockSpec(memory_space=pltpu.SMEM)] + [vmem] * (3 * n + len(flat)),
        out_shape=tuple(SDS(w.shape, F32) for w in ws) + (SDS(loss[1].shape, F32),)
        + tuple(SDS(w.shape, F32) for w in ws for _ in range(3)),
        compiler_params=_params(16),
    )(me, *ws, *ms, *vs, *flat)
    return out[0:n], out[n], out[n + 1:]


def kernel(x, mem, g_norm, w_in, w_sgu_spatial, b_sgu_spatial, g_sgu_v, g_mem, w_mem_kv, w_out, g_final, loss_target, m_g_norm, m_w_in, m_w_sgu_spatial, m_b_sgu_spatial, m_g_sgu_v, m_g_mem, m_w_mem_kv, m_w_out, m_g_final, v_g_norm, v_w_in, v_w_sgu_spatial, v_b_sgu_spatial, v_g_sgu_v, v_g_mem, v_w_mem_kv, v_w_out, v_g_final):
    x2 = x.reshape(T_LOC, D_MODEL)
    tgt2 = loss_target.reshape(T_LOC, D_MODEL)
    mem2 = mem.reshape(B_LOC * N_MEM, D_MODEL)
    w_s = w_sgu_spatial[0]
    b_exp = jnp.repeat(b_sgu_spatial[0].T, HEAD, axis=1)
    slope = jnp.power(2.0, -8.0 * (jnp.arange(8, dtype=F32) + 1.0) / 8)
    slopes = jnp.broadcast_to(jnp.repeat(slope.reshape(4, 2), HEAD, axis=1)[:, None, :], (4, 8, 128))

    tr = lambda t: jnp.transpose(t[0])

    me = _flat(_mesh_pos())

    wint, wkv_own, wout_own = _allgather_weights(tr(w_in), w_mem_kv[0], w_out[0])
    started0 = _exchange_start("exchange0_start", [wkv_own, wout_own], [False, False],
                               [_landing(wkv_own, me), _landing(wout_own, me)])
    proj, qkv16 = _proj_fwd(x2, g_norm + started0[4][0:1, 0:1], wint)
    wkv, wout = _exchange_wait("exchange0_wait", started0, [False, False], proj)
    wkv, wout = wkv.reshape(D_MODEL, 2 * MEM_W), wout.reshape(D_MODEL, D_MODEL)
    kv = _memkv_fwd(mem2, g_mem, wkv)
    a, lse = _attn_fwd(proj, qkv16, slopes)
    gated = _branch_fwd(proj, a, kv, w_s, b_exp, g_sgu_v)
    dh2, loss8, dgf, dwout = _outproj_loss(gated, wout, x2, tgt2, g_final.reshape(1, D_MODEL))

    da, dr, dkv, dws, dbs, dgv = _branch_bwd(dh2, wout, proj, a, kv, w_s, b_exp, g_sgu_v)
    dwkv, dgm = _memkv_bwd(dkv, mem2, g_mem, wkv)

    slots_for = lambda t, scattered: lax.empty(
        (N_DEV, t.shape[0] // N_DEV if scattered else t.shape[0]) + t.shape[1:], t.dtype)
    early = [dws.reshape(4 * CHUNK, CHUNK), dbs, dgv, dgm, dgf, loss8]
    scatter1 = [True, True] + [False] * len(early)
    srcs1 = [dwkv, dwout] + early
    started1 = _exchange_start("exchange1_start", srcs1, scatter1, [slots_for(t, s) for t, s in zip(srcs1, scatter1)])
    dq, dk, dv = _attn_bwd(proj, qkv16, slopes + started1[4][0:1, 0:1], da, a, lse)
    s_wkv, s_wout, s_ws, s_bs, s_gv, s_gm, s_gf, s_loss = _exchange_wait("exchange1_wait", started1, scatter1, dq)

    dwint0 = _dwin(dq, dk, dv, dr, x2, g_norm, 0)
    started2 = _exchange_start("exchange2_start", [dwint0], [True], [slots_for(dwint0, True)])
    dwint1 = _dwin(dq, dk, dv, dr, x2, g_norm + started2[4][0:1, 0:1], 1)
    started3 = _exchange_start("exchange3_start", [dwint1], [True], [slots_for(dwint1, True)])
    grad_x, dgn = _dx(dq, dk, dv, dr, wint, x2, dh2, g_norm + started3[4][0:1, 0:1])
    s_win0, = _exchange_wait("exchange2_wait", started2, [True], grad_x)
    dgn_sum, = _allreduce_small([dgn])
    s_win1, = _exchange_wait("exchange3_wait", started3, [True], dgn_sum)

    me1 = me.reshape(1).astype(jnp.int32)
    g_win, d_win, m_win, v_win = map(jnp.transpose, _adam_slots(
        "w_in", me1, [s_win0, s_win1], [dwint0, dwint1], tr(w_in), tr(m_w_in), tr(v_w_in)))
    g_wkv, d_wkv, m_wkv, v_wkv = _adam_slots(
        "w_mem_kv", me1, [s_wkv], [dwkv], w_mem_kv[0], m_w_mem_kv[0], v_w_mem_kv[0])
    g_wout, d_wout, m_wout, v_wout = _adam_slots("w_out", me1, [s_wout], [dwout], w_out[0], m_w_out[0], v_w_out[0])

    small_shapes = [(1, D_MODEL), (4 * CHUNK, CHUNK), (4, CHUNK), (1, SGU_W), (1, D_MODEL), (1, D_MODEL)]
    pack = lambda arrs: [t.reshape(s) for t, s in zip(arrs, small_shapes)]
    g_small, loss_sum, upd = _adam_small(
        me1, pack([g_norm, w_sgu_spatial, b_sgu_spatial, g_sgu_v, g_mem, g_final]),
        [dgn_sum] + list(zip([s_ws, s_bs, s_gv, s_gm, s_gf], early[0:5])),
        pack([m_g_norm, m_w_sgu_spatial, m_b_sgu_spatial, m_g_sgu_v, m_g_mem, m_g_final]),
        pack([v_g_norm, v_w_sgu_spatial, v_b_sgu_spatial, v_g_sgu_v, v_g_mem, v_g_final]), (s_loss, loss8))
    out_shapes = [g_norm.shape, w_sgu_spatial.shape, b_sgu_spatial.shape, g_sgu_v.shape, g_mem.shape, g_final.shape]
    unpack = lambda arrs: [t.reshape(s) for t, s in zip(arrs, out_shapes)]
    gs = unpack(g_small)
    ds, nms, nvs = unpack(upd[0::3]), unpack(upd[1::3]), unpack(upd[2::3])

    loss = loss_sum[0, 0]

    def assemble(small, win, wkv_, wout_):
        return [small[0], win[None], small[1], small[2], small[3], small[4], wkv_[None], wout_[None], small[5]]

    return (loss, grad_x.reshape(x.shape),
            *assemble(gs, g_win, g_wkv, g_wout), *assemble(ds, d_win, d_wkv, d_wout),
            *assemble(nms, m_win, m_wkv, m_wout), *assemble(nvs, v_win, v_wkv, v_wout))
```

```python
import jax
import jax.numpy as jnp
from jax import lax
from jax.experimental import pallas as pl
from jax.experimental.pallas import tpu as pltpu

F32 = jnp.float32
BF16 = jnp.bfloat16
SDS = jax.ShapeDtypeStruct
MESH = pl.DeviceIdType.MESH

N_DEV = 8
D_MODEL = 1024
SEQ = 2048
B_LOC = 2
T_LOC = B_LOC * SEQ
N_MEM = 256
HEAD = 64
ATTN_W = 512
SGU_W = 256
MEM_W = 256
IN_COLS = 3328
W_IN_SHARD = IN_COLS // N_DEV
CHUNK = 128
DILATIONS = ((1, 2048), (4, 512), (16, 128))
RADIUS = 64
EPS = 1e-6
NEG = -1e30
SCALE = HEAD ** -0.5
C_QA, C_KA, C_VA, C_ZA, C_UB, C_VB, C_ZB, C_QM, C_ZM = 0, 512, 1024, 1536, 2048, 2304, 2560, 2816, 3072
QKV_W = 1536
REST_W = IN_COLS - QKV_W

ADAM_LR, ADAM_B1, ADAM_B2, ADAM_EPS, ADAM_WD, ADAM_STEP = 0.001, 0.9, 0.999, 1e-08, 0.01, 10

V7X_VMEM_MIB = 64
VMEM_NO_STAGING_MIB = V7X_VMEM_MIB - 6


def _params(vmem_mib, sem=None):
    assert vmem_mib < V7X_VMEM_MIB
    return pltpu.CompilerParams(vmem_limit_bytes=vmem_mib << 20, dimension_semantics=sem)


def _dot(a, b):
    return jnp.dot(a.astype(BF16), b.astype(BF16), preferred_element_type=F32)


def _dot_nt(a, b):
    return lax.dot_general(a.astype(BF16), b.astype(BF16), (((1,), (1,)), ((), ())), preferred_element_type=F32)


def _dot_tn(a, b):
    return lax.dot_general(a.astype(BF16), b.astype(BF16), (((0,), (0,)), ((), ())), preferred_element_type=F32)


def _rstd(v):
    return lax.rsqrt(jnp.mean(v * v, axis=-1, keepdims=True) + EPS)


def _rms_bwd(v, r, g, dy):
    gdy = g * dy
    return r * gdy - v * (r * r * r * jnp.mean(gdy * v, axis=-1, keepdims=True))


def _sigmoid(z):
    return 1.0 / (1.0 + jnp.exp(-z))


def _silu_and_grad(z):
    s = _sigmoid(z)
    return z * s, s * (1.0 + z * (1.0 - s))


_G_C = 0.7978845608028654
_G_K = 0.044715


def _gelu_and_grad(v):
    t = jnp.tanh(_G_C * (v + _G_K * (v * v * v)))
    cdf = 0.5 * (1.0 + t)
    return v * cdf, cdf + 0.5 * v * (1.0 - t * t) * (_G_C * (1.0 + 3.0 * _G_K * v * v))


def _cast_rows(src_ref, dst_ref, rows, step=256):
    def one(i, carry):
        r = pl.ds(pl.multiple_of(i * step, step), step)
        dst_ref[r, :] = src_ref[r, :].astype(dst_ref.dtype)
        return carry
    lax.fori_loop(0, rows // step, one, 0)


def _left_lanes(rows):
    return lax.broadcasted_iota(jnp.int32, (rows, 128), 1) < HEAD


def _mesh_pos():
    return lax.axis_index("x"), lax.axis_index("y"), lax.axis_index("c")


def _peer(pos, k):
    x, y, c = pos
    return (1 - x if k & 4 else x, 1 - y if k & 2 else y, 1 - c if k & 1 else c)


def _flat(pos):
    return 4 * pos[0] + 2 * pos[1] + pos[2]


def _allgather_weights(w_in_t, w_kv, w_out):
    def body(win_ref, wkv_ref, wout_ref, wint_o, wkv_o, wout_o, send_sems, recv_sems):
        x, y, c = _mesh_pos()
        me, sib = (x, y, c), (x, y, 1 - c)
        chips = [(1 - x, y), (x, 1 - y), (1 - x, 1 - y)]

        def rows(p):
            return wint_o.at[pl.ds(pl.multiple_of(_flat(p) * W_IN_SHARD, 16), W_IN_SHARD), :]

        rows(me)[...] = win_ref[...].astype(BF16)

        def copy(k, block, to):
            return pltpu.make_async_remote_copy(
                src_ref=rows(block), dst_ref=rows(block), send_sem=send_sems.at[k], recv_sem=recv_sems.at[k],
                device_id=to, device_id_type=MESH)

        first = [copy(0, me, sib)] + [copy(1 + j, me, (*chip, c)) for j, chip in enumerate(chips)]
        for cp in first:
            cp.start()
        wkv_o[...] = wkv_ref[...].astype(BF16)
        wout_o[...] = wout_ref[...].astype(BF16)
        passed = []
        for j, chip in enumerate(chips):
            copy(1 + j, (*chip, c), me).wait_recv()
            fwd = copy(4 + j, (*chip, c), sib)
            fwd.start()
            passed.append(fwd)
        copy(0, sib, me).wait_recv()
        for j, chip in enumerate(chips):
            copy(4 + j, (*chip, 1 - c), me).wait_recv()
        for cp in first + passed:
            cp.wait_send()

    vmem = pl.BlockSpec(memory_space=pltpu.VMEM)
    return pl.pallas_call(
        body, name="allgather_weights",
        out_shape=(SDS((IN_COLS, D_MODEL), BF16), SDS(w_kv.shape, BF16), SDS(w_out.shape, BF16)),
        in_specs=[vmem, vmem, vmem], out_specs=(vmem, vmem, vmem),
        scratch_shapes=[pltpu.SemaphoreType.DMA((7,)), pltpu.SemaphoreType.DMA((7,))],
        compiler_params=_params(40),
    )(w_in_t, w_kv, w_out)


def _proj_fwd(x2, g_norm, wint):
    tm = 512
    sub = 256
    d = DILATIONS[2][0]
    per_ex = SEQ // tm

    def body(x_ref, g_ref, w_ref, o_ref, o16_ref):
        xv = x_ref[...]
        h = xv * _rstd(xv) * g_ref[...]
        res = _dot_nt(h, w_ref[...])
        o_ref[...] = res
        r_out = lax.broadcasted_iota(jnp.int32, (sub, sub), 0)
        r_in = lax.broadcasted_iota(jnp.int32, (sub, sub), 1)
        pick = (r_in == d * (r_out % (sub // d)) + r_out // (sub // d)).astype(BF16)
        for part in range(tm // sub):
            grouped = _dot(pick, res[part * sub:(part + 1) * sub, 0:QKV_W]).astype(BF16)
            for rho in range(d):
                o16_ref[0, rho, part * (sub // d):(part + 1) * (sub // d), :] = (
                    grouped[rho * (sub // d):(rho + 1) * (sub // d), :])

    return pl.pallas_call(
        body, name="proj_fwd", grid=(T_LOC // tm,),
        in_specs=[pl.BlockSpec((tm, D_MODEL), lambda i: (i, 0)), pl.BlockSpec((1, D_MODEL), lambda i: (0, 0)),
                  pl.BlockSpec((IN_COLS, D_MODEL), lambda i: (0, 0))],
        out_specs=(pl.BlockSpec((tm, IN_COLS), lambda i: (i, 0)),
                   pl.BlockSpec((1, d, tm // d, QKV_W), lambda i: (i // per_ex, 0, i % per_ex, 0))),
        out_shape=(SDS((T_LOC, IN_COLS), F32), SDS((B_LOC, d, SEQ // d, QKV_W), BF16)),
        compiler_params=_params(48, ("arbitrary",)),
    )(x2, g_norm, wint)


def _memkv_fwd(mem2, g_mem, wkv):
    def body(m_ref, g_ref, w_ref, o_ref):
        mv = m_ref[...]
        o_ref[...] = _dot(mv * _rstd(mv) * g_ref[...], w_ref[...])

    return pl.pallas_call(
        body, name="memkv_fwd", out_shape=SDS((B_LOC * N_MEM, 2 * MEM_W), F32), compiler_params=_params(32),
    )(mem2, g_mem, wkv)


N_BIAS = 7


def _fill_bias_tables(sl_ref, tab):
    for cfg, (d, length) in enumerate(DILATIONS):
        nk = min(length, 2 * CHUNK)
        r = lax.broadcasted_iota(jnp.int32, (CHUNK, nk), 0)
        c = lax.broadcasted_iota(jnp.int32, (CHUNK, nk), 1)
        for var in range(3 if length > nk else 1):
            rel = jnp.abs(r - c + var * RADIUS)
            dist = rel.astype(F32) * float(d)
            for h in range(2):
                slope = sl_ref[0, 0:1, h * HEAD:h * HEAD + 1]
                tab[3 * cfg + var, h * CHUNK:(h + 1) * CHUNK, 0:nk] = jnp.where(rel <= RADIUS, -slope * dist, NEG)


def _attn_blocks(visit, unroll):
    def step(t, carry):
        for cfg, (d, length) in enumerate(DILATIONS):
            nblk = length // CHUNK
            if nblk == 1:
                visit(cfg, 0, t, t, length, t)
                continue
            rho, i = (0, t) if d == 1 else (t // nblk, t % nblk)
            ks = jnp.clip(i * CHUNK - RADIUS, 0, length - 2 * CHUNK)
            visit(cfg, (i * CHUNK - ks) // RADIUS, rho + d * (i * CHUNK), rho + d * ks, 2 * CHUNK, t)
        return carry
    lax.fori_loop(0, 16, step, 0, unroll=unroll)


def _stack_heads(v, left):
    return jnp.concatenate([jnp.where(left, v, 0.0), jnp.where(left, 0.0, v)], axis=0)


def _unstack_heads(v, left):
    return jnp.where(left, v[0:CHUNK], v[CHUNK:2 * CHUNK])


def _rows(start, n, d):
    return pl.ds(start, n) if d == 1 else pl.ds(start, n, stride=d)


def _blk16(col0):
    d, length = DILATIONS[2]
    return pl.BlockSpec((1, d, length, 128), lambda b, hp: (b, 0, 0, col0 // 128 + hp))


def _attn_fwd(proj, qkv16, slopes):
    def body(sl_ref, q_ref, k_ref, v_ref, q16_ref, k16_ref, v16_ref, a_ref, lse_ref, *scr):
        o_c, m_c, l_c, tab = scr[0:3], scr[3:6], scr[6:9], scr[9]
        left = _left_lanes(CHUNK)
        _fill_bias_tables(sl_ref, tab)

        def block(cfg, var, q0, k0, nk, t):
            d = DILATIONS[cfg][0]
            rq, rk = _rows(q0, CHUNK, d), _rows(k0, nk, d)
            if cfg == 2:
                qb, kw, vw = q16_ref[0, t].astype(F32), k16_ref[0, t], v16_ref[0, t]
            else:
                qb, kw, vw = q_ref[rq, :], k_ref[rk, :], v_ref[rk, :]
            qs = _stack_heads(qb * SCALE, left)
            s = _dot_nt(qs, kw) + tab[3 * cfg + var, :, 0:nk]
            m = jnp.max(s, axis=-1, keepdims=True)
            p = jnp.exp(s - m)
            o_c[cfg][rq, :] = _unstack_heads(_dot(p, vw), left)
            m_c[cfg][rq, :] = _unstack_heads(m, left)
            l_c[cfg][rq, :] = _unstack_heads(jnp.sum(p, axis=-1, keepdims=True), left)
        _attn_blocks(block, 8)

        def merge(j, carry):
            rows = pl.ds(pl.multiple_of(j * 256, 256), 256)
            ms = [m_c[i][rows, :] for i in range(3)]
            top = jnp.maximum(jnp.maximum(ms[0], ms[1]), ms[2])
            ws = [jnp.exp(m - top) for m in ms]
            den = l_c[0][rows, :] * ws[0] + l_c[1][rows, :] * ws[1] + l_c[2][rows, :] * ws[2]
            num = o_c[0][rows, :] * ws[0] + o_c[1][rows, :] * ws[1] + o_c[2][rows, :] * ws[2]
            a_ref[rows, :] = num / den
            lse_ref[rows, :] = top + jnp.log(den)
            return carry
        lax.fori_loop(0, SEQ // 256, merge, 0)

    blk = lambda col0: pl.BlockSpec((SEQ, 128), lambda b, hp: (b, col0 // 128 + hp))
    out = pl.BlockSpec((SEQ, 128), lambda b, hp: (b, hp))
    return pl.pallas_call(
        body, name="attn_fwd", grid=(B_LOC, 4),
        in_specs=[pl.BlockSpec((1, 8, 128), lambda b, hp: (hp, 0, 0)), blk(C_QA), blk(C_KA), blk(C_VA),
                  _blk16(C_QA), _blk16(C_KA), _blk16(C_VA)],
        out_specs=(out, out),
        out_shape=(SDS((T_LOC, ATTN_W), F32), SDS((T_LOC, ATTN_W), F32)),
        scratch_shapes=[pltpu.VMEM((SEQ, 128), F32)] * 9 + [pltpu.VMEM((N_BIAS, 2 * CHUNK, 2 * CHUNK), F32)],
        compiler_params=_params(40, ("arbitrary", "arbitrary")),
    )(slopes, proj, proj, proj, qkv16, qkv16, qkv16)


def _chunks_side_by_side(v, pr, tm):
    return jnp.concatenate([v[ch * CHUNK:(ch + 1) * CHUNK, pr * 128:(pr + 1) * 128] for ch in range(tm // CHUNK)], axis=1)


def _first_group_lanes(tm):
    return lax.broadcasted_iota(jnp.int32, (CHUNK, tm), 1) % 128 < HEAD


def _store_chunks(dst_ref, pr, val, tm):
    for ch in range(tm // CHUNK):
        dst_ref[ch * CHUNK:(ch + 1) * CHUNK, pr * 128:(pr + 1) * 128] = val[:, ch * CHUNK:(ch + 1) * CHUNK]


def _sgu_mix(vn, ws_ref, dst_ref, tm):
    first = _first_group_lanes(tm)
    for pr in range(2):
        vp = _chunks_side_by_side(vn, pr, tm)
        _store_chunks(dst_ref, pr, jnp.where(first, _dot(ws_ref[2 * pr], vp), _dot(ws_ref[2 * pr + 1], vp)), tm)


def _mem_head_of_lane(rows):
    return lax.broadcasted_iota(jnp.int32, (rows, MEM_W), 1) // HEAD


def _stack_mem_heads(v, rows):
    head = _mem_head_of_lane(rows)
    return jnp.concatenate([jnp.where(head == h, v, 0.0) for h in range(4)], axis=0)


def _unstack_mem_heads(v, rows):
    head = _mem_head_of_lane(rows)
    out = v[0:rows]
    for h in range(1, 4):
        out = jnp.where(head == h, v[h * rows:(h + 1) * rows], out)
    return out


def _mem_attn_probs(q, kmem, rows):
    qs = _stack_mem_heads(q, rows).astype(BF16)
    s = _dot_nt(qs, kmem) * SCALE
    e = jnp.exp(s - jnp.max(s, axis=-1, keepdims=True))
    return e * (1.0 / jnp.sum(e, axis=-1, keepdims=True)), qs


def _branch_blocks(tm):
    col = lambda w, c0: pl.BlockSpec((tm, w), lambda i: (i, c0 // w))
    return [col(512, C_ZA), col(256, C_UB), col(256, C_VB), col(256, C_ZB), col(256, C_QM), col(256, C_ZM)]


def _branch_fwd(proj, a, kv, w_s, b_exp, g_v):
    tm = 512
    per_ex = SEQ // tm

    def body(za_ref, ub_ref, vb_ref, zb_ref, qm_ref, zm_ref, a_ref, kv_ref, ws_ref, be_ref, gv_ref, o_ref, mix):
        o_ref[:, 0:ATTN_W] = (_silu_and_grad(za_ref[...])[0] * a_ref[...]).astype(BF16)
        gu = _gelu_and_grad(ub_ref[...])[0]
        gv = _gelu_and_grad(vb_ref[...])[0]
        vn = gv * _rstd(gv) * gv_ref[...]
        _sgu_mix(vn.astype(BF16), ws_ref, mix, tm)
        sg = gu * (mix[...] + jnp.concatenate([be_ref[...]] * (tm // CHUNK), axis=0))
        o_ref[:, ATTN_W:ATTN_W + SGU_W] = (_silu_and_grad(zb_ref[...])[0] * sg).astype(BF16)
        p = _mem_attn_probs(qm_ref[...], kv_ref[:, 0:MEM_W], tm)[0]
        mo = _unstack_mem_heads(_dot(p, kv_ref[:, MEM_W:2 * MEM_W]), tm)
        o_ref[:, ATTN_W + SGU_W:D_MODEL] = (_silu_and_grad(zm_ref[...])[0] * mo).astype(BF16)

    full = lambda shape: pl.BlockSpec(shape, lambda i: (0,) * len(shape))
    return pl.pallas_call(
        body, name="branch_fwd", grid=(T_LOC // tm,),
        in_specs=_branch_blocks(tm) + [
            pl.BlockSpec((tm, ATTN_W), lambda i: (i, 0)), pl.BlockSpec((N_MEM, 2 * MEM_W), lambda i: (i // per_ex, 0)),
            full((4, CHUNK, CHUNK)), full((CHUNK, SGU_W)), full((1, SGU_W))],
        out_specs=pl.BlockSpec((tm, D_MODEL), lambda i: (i, 0)),
        out_shape=SDS((T_LOC, D_MODEL), BF16),
        scratch_shapes=[pltpu.VMEM((tm, SGU_W), F32)],
        compiler_params=_params(VMEM_NO_STAGING_MIB, ("arbitrary",)),
    )(proj, proj, proj, proj, proj, proj, a, kv, w_s, b_exp, g_v)


def _outproj_loss(gated, wout, x2, tgt2, g_final):
    tm = 1024

    def body(g_ref, w_ref, x_ref, t_ref, gf_ref, dh2_ref, loss_ref, dgf_ref, dwo_ref, dwo_acc):
        @pl.when(pl.program_id(0) == 0)
        def _():
            loss_ref[...] = jnp.zeros_like(loss_ref)
            dgf_ref[...] = jnp.zeros_like(dgf_ref)
            dwo_acc[...] = jnp.zeros_like(dwo_acc)
        gated = g_ref[...]
        h2 = x_ref[...] + _dot(gated, w_ref[...])
        r = _rstd(h2)
        gf = gf_ref[...]
        err = h2 * r * gf - t_ref[...]
        loss_ref[...] += 0.5 * jnp.sum(jnp.mean(err * err, axis=-1, keepdims=True))
        dy = err * (1.0 / D_MODEL)
        dh2 = _rms_bwd(h2, r, gf, dy)
        dh2_ref[...] = dh2
        dgf_ref[...] += jnp.sum(dy * (h2 * r), axis=0, keepdims=True)
        dwo_acc[...] += _dot_tn(gated, dh2)

        @pl.when(pl.program_id(0) == pl.num_programs(0) - 1)
        def _():
            _cast_rows(dwo_acc, dwo_ref, D_MODEL)

    row = pl.BlockSpec((tm, D_MODEL), lambda i: (i, 0))
    vec = pl.BlockSpec((1, D_MODEL), lambda i: (0, 0))
    square = pl.BlockSpec((D_MODEL, D_MODEL), lambda i: (0, 0))
    return pl.pallas_call(
        body, name="outproj_loss", grid=(T_LOC // tm,),
        in_specs=[row, square, row, row, vec],
        out_specs=(row, pl.BlockSpec((8, 128), lambda i: (0, 0)), vec, square),
        out_shape=(SDS((T_LOC, D_MODEL), F32), SDS((8, 128), F32), SDS((1, D_MODEL), F32), SDS((D_MODEL, D_MODEL), BF16)),
        scratch_shapes=[pltpu.VMEM((D_MODEL, D_MODEL), F32)],
        compiler_params=_params(VMEM_NO_STAGING_MIB, ("arbitrary",)),
    )(gated, wout, x2, tgt2, g_final)


def _branch_bwd(dh2, wout, proj, a, kv, w_s, b_exp, g_v):
    tm = 512
    per_ex = SEQ // tm

    def body(dh2_ref, w_ref, za_ref, ub_ref, vb_ref, zb_ref, qm_ref, zm_ref, a_ref, kv_ref, ws_ref,
             be_ref, gv_ref, da_ref, dr_ref, dkv_ref, dws_ref, db_ref, dgv_ref, mix, dvn, dmsum):
        i = pl.program_id(0)

        @pl.when(i == 0)
        def _():
            dws_ref[...] = jnp.zeros_like(dws_ref)
            dgv_ref[...] = jnp.zeros_like(dgv_ref)
            dmsum[...] = jnp.zeros_like(dmsum)

        @pl.when(i % per_ex == 0)
        def _():
            dkv_ref[...] = jnp.zeros_like(dkv_ref)

        dg = _dot_nt(dh2_ref[...], w_ref[...])

        sa, dsa = _silu_and_grad(za_ref[...])
        dga = dg[:, 0:ATTN_W]
        da_ref[...] = dga * sa
        dr_ref[:, 0:512] = (dga * a_ref[...] * dsa).astype(BF16)

        ub, vb = ub_ref[...], vb_ref[...]
        gu, dgu = _gelu_and_grad(ub)
        gv, dgv = _gelu_and_grad(vb)
        rv = _rstd(gv)
        gain = gv_ref[...]
        vn = (gv * rv * gain).astype(BF16)
        _sgu_mix(vn, ws_ref, mix, tm)
        mixed = mix[...] + jnp.concatenate([be_ref[...]] * (tm // CHUNK), axis=0)
        sb, dsb = _silu_and_grad(zb_ref[...])
        dgb = dg[:, ATTN_W:ATTN_W + SGU_W]
        dsg = dgb * sb
        dr_ref[:, 512:768] = (dsg * mixed * dgu).astype(BF16)
        dr_ref[:, 1024:1280] = (dgb * (gu * mixed) * dsb).astype(BF16)
        dmix = dsg * gu
        for ch in range(tm // CHUNK):
            dmsum[...] += dmix[ch * CHUNK:(ch + 1) * CHUNK, :]
        first = _first_group_lanes(tm)
        for pr in range(2):
            dmp, vp = _chunks_side_by_side(dmix, pr, tm), _chunks_side_by_side(vn, pr, tm)
            dws_ref[2 * pr] += _dot_nt(jnp.where(first, dmp, 0.0), vp)
            dws_ref[2 * pr + 1] += _dot_nt(jnp.where(first, 0.0, dmp), vp)
            _store_chunks(dvn, pr, jnp.where(first, _dot_tn(ws_ref[2 * pr], dmp), _dot_tn(ws_ref[2 * pr + 1], dmp)), tm)
        dvn_v = dvn[...]
        dgv_ref[...] += jnp.sum(dvn_v * (gv * rv), axis=0, keepdims=True)
        dr_ref[:, 768:1024] = (_rms_bwd(gv, rv, gain, dvn_v) * dgv).astype(BF16)

        szm, dszm = _silu_and_grad(zm_ref[...])
        dgm = dg[:, ATTN_W + SGU_W:D_MODEL]
        kmem, vmem_ = kv_ref[:, 0:MEM_W].astype(BF16), kv_ref[:, MEM_W:2 * MEM_W].astype(BF16)
        p, qs = _mem_attn_probs(qm_ref[...], kmem, tm)
        dmos = _stack_mem_heads(dgm * szm, tm).astype(BF16)
        dr_ref[:, 1536:1792] = (dgm * _unstack_mem_heads(_dot(p, vmem_), tm) * dszm).astype(BF16)
        dp = _dot_nt(dmos, vmem_)
        ds = (p * (dp - jnp.sum(p * dp, axis=-1, keepdims=True)) * SCALE).astype(BF16)
        dr_ref[:, 1280:1536] = _unstack_mem_heads(_dot(ds, kmem), tm).astype(BF16)
        dkv_ref[:, 0:MEM_W] += _dot_tn(ds, qs)
        dkv_ref[:, MEM_W:2 * MEM_W] += _dot_tn(p, dmos)

        @pl.when(i == pl.num_programs(0) - 1)
        def _():
            tot = dmsum[...]
            hi = tot.astype(BF16)
            lo = (tot - hi.astype(F32)).astype(BF16)
            grp = (lax.broadcasted_iota(jnp.int32, (SGU_W, 128), 0) // HEAD
                   == lax.broadcasted_iota(jnp.int32, (SGU_W, 128), 1)).astype(BF16)
            db_ref[...] = (_dot(hi, grp) + _dot(lo, grp)).T[0:4, :]

    full = lambda shape: pl.BlockSpec(shape, lambda i: (0,) * len(shape))
    row = lambda w: pl.BlockSpec((tm, w), lambda i: (i, 0))
    return pl.pallas_call(
        body, name="branch_bwd", grid=(T_LOC // tm,),
        in_specs=[row(D_MODEL), full((D_MODEL, D_MODEL))] + _branch_blocks(tm) + [
            row(ATTN_W), pl.BlockSpec((N_MEM, 2 * MEM_W), lambda i: (i // per_ex, 0)),
            full((4, CHUNK, CHUNK)), full((CHUNK, SGU_W)), full((1, SGU_W))],
        out_specs=(row(ATTN_W), row(REST_W), pl.BlockSpec((N_MEM, 2 * MEM_W), lambda i: (i // per_ex, 0)),
                   full((4, CHUNK, CHUNK)), full((4, CHUNK)), full((1, SGU_W))),
        out_shape=(SDS((T_LOC, ATTN_W), F32), SDS((T_LOC, REST_W), BF16), SDS((B_LOC * N_MEM, 2 * MEM_W), F32),
                   SDS((4, CHUNK, CHUNK), F32), SDS((4, CHUNK), F32), SDS((1, SGU_W), F32)),
        scratch_shapes=[pltpu.VMEM((tm, SGU_W), F32), pltpu.VMEM((tm, SGU_W), F32), pltpu.VMEM((CHUNK, SGU_W), F32)],
        compiler_params=_params(56, ("arbitrary",)),
    )(dh2, wout, proj, proj, proj, proj, proj, proj, a, kv, w_s, b_exp, g_v)


def _attn_bwd(proj, qkv16, slopes, da, a, lse):
    def body(sl_ref, q_ref, k_ref, v_ref, q16_ref, k16_ref, v16_ref, da_ref, a_ref, lse_ref,
             dq_ref, dk_ref, dv_ref, *scr):
        dq_s, dk_s, dv_s, tab = scr[0:3], scr[3:6], scr[6:9], scr[9]
        lse_h, delta_h = scr[10:12], scr[12:14]
        p_all, ds_all = scr[14], scr[15]
        left = _left_lanes(CHUNK)
        _fill_bias_tables(sl_ref, tab)

        def prep(j, carry):
            rows = pl.ds(pl.multiple_of(j * 256, 256), 256)
            l256 = _left_lanes(256)
            prod = da_ref[rows, :] * a_ref[rows, :]
            delta_h[0][rows, :] = jnp.broadcast_to(jnp.sum(jnp.where(l256, prod, 0.0), axis=-1, keepdims=True), (256, 128))
            delta_h[1][rows, :] = jnp.broadcast_to(jnp.sum(jnp.where(l256, 0.0, prod), axis=-1, keepdims=True), (256, 128))
            pair = lse_ref[rows, :]
            other = pltpu.roll(pair, HEAD, axis=1)
            lse_h[0][rows, :] = jnp.where(l256, pair, other)
            lse_h[1][rows, :] = jnp.where(l256, other, pair)
            zero = jnp.zeros((256, 128), F32)
            for cfg in range(3):
                dk_s[cfg][rows, :] = zero
                dv_s[cfg][rows, :] = zero
            return carry
        lax.fori_loop(0, SEQ // 256, prep, 0)

        def per_row(halves, rq, nk):
            v = jnp.concatenate([halves[0][rq, :], halves[1][rq, :]], axis=0)
            return v if nk == 128 else jnp.concatenate([v, v], axis=1)

        def qkv(cfg, rq, rk, t):
            if cfg == 2:
                return q16_ref[0, t].astype(F32), k16_ref[0, t], v16_ref[0, t]
            return q_ref[rq, :], k_ref[rk, :], v_ref[rk, :]

        def probs(cfg, var, q0, k0, nk, t):
            d = DILATIONS[cfg][0]
            rq, rk = _rows(q0, CHUNK, d), _rows(k0, nk, d)
            qb, kw, vw = qkv(cfg, rq, rk, t)
            qs = _stack_heads(qb * SCALE, left)
            das = _stack_heads(da_ref[rq, :], left)
            s = _dot_nt(qs, kw) + tab[3 * cfg + var, :, 0:nk]
            p = jnp.exp(s - per_row(lse_h, rq, nk))
            p_all[16 * cfg + t, :, 0:nk] = p.astype(BF16)
            ds_all[16 * cfg + t, :, 0:nk] = (p * (_dot_nt(das, vw) - per_row(delta_h, rq, nk))).astype(BF16)
        _attn_blocks(probs, 4)

        def grads(cfg, var, q0, k0, nk, t):
            d = DILATIONS[cfg][0]
            rq, rk = _rows(q0, CHUNK, d), _rows(k0, nk, d)
            qb, kw, _ = qkv(cfg, rq, rk, t)
            qs = _stack_heads(qb * SCALE, left).astype(BF16)
            das = _stack_heads(da_ref[rq, :], left).astype(BF16)
            p, ds = p_all[16 * cfg + t, :, 0:nk], ds_all[16 * cfg + t, :, 0:nk]
            dq_s[cfg][rq, :] = _unstack_heads(_dot(ds, kw), left) * SCALE
            dk_s[cfg][rk, :] += _dot_tn(ds, qs)
            dv_s[cfg][rk, :] += _dot_tn(p, das)
        _attn_blocks(grads, 4)

        def flush(j, carry):
            rows = pl.ds(pl.multiple_of(j * 256, 256), 256)
            for acc, dst in ((dq_s, dq_ref), (dk_s, dk_ref), (dv_s, dv_ref)):
                dst[rows, :] = (acc[0][rows, :] + acc[1][rows, :] + acc[2][rows, :]).astype(BF16)
            return carry
        lax.fori_loop(0, SEQ // 256, flush, 0)

    blk = lambda col0: pl.BlockSpec((SEQ, 128), lambda b, hp: (b, col0 // 128 + hp))
    own = pl.BlockSpec((SEQ, 128), lambda b, hp: (b, hp))
    return pl.pallas_call(
        body, name="attn_bwd", grid=(B_LOC, 4),
        in_specs=[pl.BlockSpec((1, 8, 128), lambda b, hp: (hp, 0, 0)), blk(C_QA), blk(C_KA), blk(C_VA),
                  _blk16(C_QA), _blk16(C_KA), _blk16(C_VA), own, own, own],
        out_specs=(own, own, own),
        out_shape=(SDS((T_LOC, ATTN_W), BF16),) * 3,
        scratch_shapes=[pltpu.VMEM((SEQ, 128), F32)] * 9 + [pltpu.VMEM((N_BIAS, 2 * CHUNK, 2 * CHUNK), F32)]
        + [pltpu.VMEM((SEQ, 128), F32)] * 4 + [pltpu.VMEM((48, 2 * CHUNK, 2 * CHUNK), BF16)] * 2,
        compiler_params=_params(52, ("arbitrary", "arbitrary")),
    )(slopes, proj, proj, proj, qkv16, qkv16, qkv16, da, a, lse)


def _dproj_specs(tm):
    third = pl.BlockSpec((tm, ATTN_W), lambda i: (i, 0))
    return [third, third, third, pl.BlockSpec((tm, REST_W), lambda i: (i, 0))]


def _dx(dq, dk, dv, dr, wint, x2, dh2, g_norm):
    tm = 512

    def body(dq_ref, dk_ref, dv_ref, dr_ref, w_ref, x_ref, dh2_ref, g_ref, gx_ref, dgn_ref):
        @pl.when(pl.program_id(0) == 0)
        def _():
            dgn_ref[...] = jnp.zeros_like(dgn_ref)
        dh = (_dot(dq_ref[...], w_ref[C_QA:C_KA, :]) + _dot(dk_ref[...], w_ref[C_KA:C_VA, :])
              + _dot(dv_ref[...], w_ref[C_VA:C_ZA, :]) + _dot(dr_ref[...], w_ref[C_ZA:IN_COLS, :]))
        xv = x_ref[...]
        r = _rstd(xv)
        gx_ref[...] = dh2_ref[...] + _rms_bwd(xv, r, g_ref[...], dh)
        dgn_ref[...] += jnp.sum(dh * (xv * r), axis=0, keepdims=True)

    row = pl.BlockSpec((tm, D_MODEL), lambda i: (i, 0))
    vec = pl.BlockSpec((1, D_MODEL), lambda i: (0, 0))
    return pl.pallas_call(
        body, name="dx", grid=(T_LOC // tm,),
        in_specs=_dproj_specs(tm) + [pl.BlockSpec((IN_COLS, D_MODEL), lambda i: (0, 0)), row, row, vec],
        out_specs=(row, vec),
        out_shape=(SDS((T_LOC, D_MODEL), F32), SDS((1, D_MODEL), F32)),
        compiler_params=_params(48, ("arbitrary",)),
    )(dq, dk, dv, dr, wint, x2, dh2, g_norm)


def _dwin(dq, dk, dv, dr, x2, g_norm, half):
    tm = 1024
    width = D_MODEL // 2
    cols = slice(half * width, (half + 1) * width)

    def body(dq_ref, dk_ref, dv_ref, dr_ref, x_ref, g_ref, o_ref, acc):
        @pl.when(pl.program_id(0) == 0)
        def _():
            acc[...] = jnp.zeros_like(acc)
        xv = x_ref[...]
        h = (xv[:, cols] * _rstd(xv) * g_ref[:, cols]).astype(BF16)
        acc[C_QA:C_KA, :] += _dot_tn(dq_ref[...], h)
        acc[C_KA:C_VA, :] += _dot_tn(dk_ref[...], h)
        acc[C_VA:C_ZA, :] += _dot_tn(dv_ref[...], h)
        acc[C_ZA:IN_COLS, :] += _dot_tn(dr_ref[...], h)

        @pl.when(pl.program_id(0) == pl.num_programs(0) - 1)
        def _():
            _cast_rows(acc, o_ref, IN_COLS)

    return pl.pallas_call(
        body, name="dwin%d" % half, grid=(T_LOC // tm,),
        in_specs=_dproj_specs(tm) + [pl.BlockSpec((tm, D_MODEL), lambda i: (i, 0)), pl.BlockSpec((1, D_MODEL), lambda i: (0, 0))],
        out_specs=pl.BlockSpec((IN_COLS, width), lambda i: (0, 0)),
        out_shape=SDS((IN_COLS, width), BF16),
        scratch_shapes=[pltpu.VMEM((IN_COLS, width), F32)],
        compiler_params=_params(48, ("arbitrary",)),
    )(dq, dk, dv, dr, x2, g_norm)


def _memkv_bwd(dkv, mem2, g_mem, wkv):
    def body(dkv_ref, m_ref, g_ref, w_ref, dw_ref, dg_ref):
        mv = m_ref[...]
        r = _rstd(mv)
        dkv_v = dkv_ref[...].astype(BF16)
        dw_ref[...] = _dot_tn(mv * r * g_ref[...], dkv_v).astype(BF16)
        dg_ref[...] = jnp.sum(_dot_nt(dkv_v, w_ref[...]) * (mv * r), axis=0, keepdims=True)

    return pl.pallas_call(
        body, name="memkv_bwd", out_shape=(SDS((D_MODEL, 2 * MEM_W), BF16), SDS((1, D_MODEL), F32)),
        compiler_params=_params(32),
    )(dkv, mem2, g_mem, wkv)


def _allreduce_small(parts):
    n = len(parts)

    def body(*refs):
        ins, outs, bufs = refs[0:n], refs[n:2 * n], refs[2 * n:3 * n]
        send_sems, recv_sems = refs[3 * n], refs[3 * n + 1]
        pos = _mesh_pos()
        me = _flat(pos)
        for a in range(n):
            bufs[a][me] = ins[a][...]

        def copy(a, k, slot):
            return pltpu.make_async_remote_copy(
                src_ref=ins[a], dst_ref=bufs[a].at[slot],
                send_sem=send_sems.at[7 * a + k - 1], recv_sem=recv_sems.at[7 * a + k - 1],
                device_id=_peer(pos, k), device_id_type=MESH)

        sent = [copy(a, k, me) for a in range(n) for k in range(1, N_DEV)]
        for cp in sent:
            cp.start()
        for a in range(n):
            for k in range(1, N_DEV):
                copy(a, k, _flat(_peer(pos, k))).wait_recv()
        for cp in sent:
            cp.wait_send()
        for a in range(n):
            acc = bufs[a][0]
            for s in range(1, N_DEV):
                acc = acc + bufs[a][s]
            outs[a][...] = acc

    vmem = pl.BlockSpec(memory_space=pltpu.VMEM)
    return pl.pallas_call(
        body, name="allreduce_small",
        out_shape=tuple(SDS(p.shape, F32) for p in parts),
        in_specs=[vmem] * n, out_specs=(vmem,) * n,
        scratch_shapes=[pltpu.VMEM((N_DEV,) + p.shape, F32) for p in parts]
        + [pltpu.SemaphoreType.DMA((7 * n,)), pltpu.SemaphoreType.DMA((7 * n,))],
        compiler_params=_params(16),
    )(*parts)


_HBM = pl.BlockSpec(memory_space=pltpu.HBM)
_SEM = pl.BlockSpec(memory_space=pltpu.SEMAPHORE)
_SIDE_EFFECT = pltpu.SideEffectType.DATAFLOW_SIDE_EFFECTING


def _exchange_copies(src_refs, land_refs, scatter, send_sems, recv_sems):
    pos = _mesh_pos()
    copies = []
    for a, (src, land) in enumerate(zip(src_refs, land_refs)):
        n = land.shape[1]
        for k in range(1, N_DEV):
            peer = _peer(pos, k)
            piece = src.at[pl.ds(pl.multiple_of(_flat(peer) * n, 16), n), :] if scatter[a] else src
            copies.append(pltpu.make_async_remote_copy(
                src_ref=piece, dst_ref=land.at[_flat(pos)],
                send_sem=send_sems.at[7 * a + k - 1], recv_sem=recv_sems.at[7 * a + k - 1],
                device_id=peer, device_id_type=MESH))
    return copies


def _exchange_start(name, srcs, scatter, lands):
    n = len(srcs)

    def body(*refs):
        for cp in _exchange_copies(refs[0:n], refs[n:2 * n], scatter, refs[2 * n], refs[2 * n + 1]):
            cp.start()
        refs[-1][...] = jnp.zeros_like(refs[-1])

    ops = [pltpu.with_memory_space_constraint(t, pltpu.HBM) for t in (*srcs, *lands)]
    out = pl.pallas_call(
        body, name=name,
        out_shape=(pltpu.SemaphoreType.DMA((7 * n,)), pltpu.SemaphoreType.DMA((7 * n,)),
                   *[pltpu.HBM(t.shape, t.dtype) for t in ops], SDS((8, 128), F32)),
        in_specs=[_HBM] * (2 * n),
        out_specs=(_SEM, _SEM, *[_HBM] * (2 * n), pl.BlockSpec(memory_space=pltpu.VMEM)),
        input_output_aliases={i: 2 + i for i in range(2 * n)},
        compiler_params=pltpu.CompilerParams(has_side_effects=_SIDE_EFFECT),
    )(*ops)
    return out[0], out[1], out[2:2 + n], out[2 + n:2 + 2 * n], out[-1]


def _exchange_wait(name, started, scatter, after):
    send_sems, recv_sems, srcs, lands, _ = started
    n = len(srcs)

    def body(*refs):
        for cp in _exchange_copies(refs[0:n], refs[n:2 * n], scatter, refs[2 * n], refs[2 * n + 1]):
            cp.wait_send()
            cp.wait_recv()

    out = pl.pallas_call(
        body, name=name,
        out_shape=tuple(pltpu.HBM(t.shape, t.dtype) for t in (*srcs, *lands)),
        in_specs=[_HBM] * (2 * n) + [_SEM, _SEM, pl.BlockSpec(memory_space=pl.ANY)],
        out_specs=(_HBM,) * (2 * n),
        input_output_aliases={i: i for i in range(2 * n)},
        compiler_params=pltpu.CompilerParams(has_side_effects=_SIDE_EFFECT),
    )(*srcs, *lands, send_sems, recv_sems, after)
    return out[n:]


def _landing(own, me):
    return lax.dynamic_update_slice(lax.empty((N_DEV,) + own.shape, own.dtype), own[None], (me,) + (0,) * own.ndim)


def _adamw(w, g, m, v):
    m = ADAM_B1 * m + (1.0 - ADAM_B1) * g
    v = ADAM_B2 * v + (1.0 - ADAM_B2) * (g * g)
    m_hat = m / (1.0 - ADAM_B1 ** ADAM_STEP)
    v_hat = v / (1.0 - ADAM_B2 ** ADAM_STEP)
    return -ADAM_LR * (m_hat / (jnp.sqrt(v_hat) + ADAM_EPS) + ADAM_WD * w), m, v


def _adam_slots(name, pieces, w, m, v):
    rows, cols = w.shape
    starts = [sum(p.shape[2] for p in pieces[:i]) for i in range(len(pieces) + 1)]
    assert starts[-1] == cols and all(p.shape[1] == rows for p in pieces)

    def body(*refs):
        s_refs, (w_ref, m_ref, v_ref, g_o, d_o, m_o, v_o, acc) = refs[:len(pieces)], refs[len(pieces):]
        s = pl.program_id(0)

        @pl.when(s == 0)
        def _():
            for i, s_ref in enumerate(s_refs):
                acc[:, starts[i]:starts[i + 1]] = s_ref[0].astype(F32)

        @pl.when(s > 0)
        def _():
            for i, s_ref in enumerate(s_refs):
                acc[:, starts[i]:starts[i + 1]] += s_ref[0].astype(F32)

        @pl.when(s == N_DEV - 1)
        def _():
            g = acc[...]
            g_o[...] = g
            d_o[...], m_o[...], v_o[...] = _adamw(w_ref[...], g, m_ref[...], v_ref[...])

    full = pl.BlockSpec((rows, cols), lambda s: (0, 0))
    return pl.pallas_call(
        body, name="adam_" + name, grid=(N_DEV,),
        in_specs=[pl.BlockSpec((1, rows, p.shape[2]), lambda s: (s, 0, 0)) for p in pieces] + [full, full, full],
        out_specs=(full,) * 4, out_shape=(SDS((rows, cols), F32),) * 4,
        scratch_shapes=[pltpu.VMEM((rows, cols), F32)],
        compiler_params=_params(40, ("arbitrary",)),
    )(*pieces, w, m, v)


def _adam_small(ws, gs, ms, vs, loss_slots):
    n = len(ws)

    def total(ref, like):
        if len(ref.shape) == len(like.shape):
            return ref[...]
        acc = ref[0]
        for s in range(1, N_DEV):
            acc = acc + ref[s]
        return acc

    def body(*refs):
        w_r, g_r, m_r, v_r = refs[0:n], refs[n:2 * n], refs[2 * n:3 * n], refs[3 * n:4 * n]
        loss_r, outs = refs[4 * n], refs[4 * n + 1:]
        for a in range(n):
            g = total(g_r[a], w_r[a])
            outs[a][...] = g
            outs[n + 1 + 3 * a][...], outs[n + 2 + 3 * a][...], outs[n + 3 + 3 * a][...] = _adamw(
                w_r[a][...], g, m_r[a][...], v_r[a][...])
        outs[n][...] = total(loss_r, outs[n])

    out = pl.pallas_call(
        body, name="adam_small",
        out_shape=tuple(SDS(w.shape, F32) for w in ws) + (SDS(loss_slots.shape[1:], F32),)
        + tuple(SDS(w.shape, F32) for w in ws for _ in range(3)),
        compiler_params=_params(16),
    )(*ws, *gs, *ms, *vs, loss_slots)
    return out[0:n], out[n], out[n + 1:]


def kernel(x, mem, g_norm, w_in, w_sgu_spatial, b_sgu_spatial, g_sgu_v, g_mem, w_mem_kv, w_out, g_final, loss_target, m_g_norm, m_w_in, m_w_sgu_spatial, m_b_sgu_spatial, m_g_sgu_v, m_g_mem, m_w_mem_kv, m_w_out, m_g_final, v_g_norm, v_w_in, v_w_sgu_spatial, v_b_sgu_spatial, v_g_sgu_v, v_g_mem, v_w_mem_kv, v_w_out, v_g_final):
    x2 = x.reshape(T_LOC, D_MODEL)
    tgt2 = loss_target.reshape(T_LOC, D_MODEL)
    mem2 = mem.reshape(B_LOC * N_MEM, D_MODEL)
    w_s = w_sgu_spatial[0]
    b_exp = jnp.repeat(b_sgu_spatial[0].T, HEAD, axis=1)
    slope = jnp.power(2.0, -8.0 * (jnp.arange(8, dtype=F32) + 1.0) / 8)
    slopes = jnp.broadcast_to(jnp.repeat(slope.reshape(4, 2), HEAD, axis=1)[:, None, :], (4, 8, 128))

    tr = lambda t: jnp.transpose(t[0])

    me = _flat(_mesh_pos())
    own_rows = lambda t: lax.dynamic_slice_in_dim(t, me * (t.shape[0] // N_DEV), t.shape[0] // N_DEV)

    wint, wkv_own, wout_own = _allgather_weights(tr(w_in), w_mem_kv[0], w_out[0])
    started0 = _exchange_start("exchange0_start", [wkv_own, wout_own], [False, False],
                               [_landing(wkv_own, me), _landing(wout_own, me)])
    proj, qkv16 = _proj_fwd(x2, g_norm + started0[4][0:1, 0:1], wint)
    wkv, wout = _exchange_wait("exchange0_wait", started0, [False, False], proj)
    wkv, wout = wkv.reshape(D_MODEL, 2 * MEM_W), wout.reshape(D_MODEL, D_MODEL)
    kv = _memkv_fwd(mem2, g_mem, wkv)
    a, lse = _attn_fwd(proj, qkv16, slopes)
    gated = _branch_fwd(proj, a, kv, w_s, b_exp, g_sgu_v)
    dh2, loss8, dgf, dwout = _outproj_loss(gated, wout, x2, tgt2, g_final.reshape(1, D_MODEL))

    da, dr, dkv, dws, dbs, dgv = _branch_bwd(dh2, wout, proj, a, kv, w_s, b_exp, g_sgu_v)
    dwkv, dgm = _memkv_bwd(dkv, mem2, g_mem, wkv)

    early = [dws.reshape(4 * CHUNK, CHUNK), dbs, dgv, dgm, dgf, loss8]
    scatter1 = [True, True] + [False] * len(early)
    started1 = _exchange_start(
        "exchange1_start", [dwkv, dwout] + early, scatter1,
        [_landing(own_rows(dwkv), me), _landing(own_rows(dwout), me)] + [_landing(t, me) for t in early])
    dq, dk, dv = _attn_bwd(proj, qkv16, slopes + started1[4][0:1, 0:1], da, a, lse)
    s_wkv, s_wout, s_ws, s_bs, s_gv, s_gm, s_gf, s_loss = _exchange_wait("exchange1_wait", started1, scatter1, dq)

    dwint0 = _dwin(dq, dk, dv, dr, x2, g_norm, 0)
    started2 = _exchange_start("exchange2_start", [dwint0], [True], [_landing(own_rows(dwint0), me)])
    dwint1 = _dwin(dq, dk, dv, dr, x2, g_norm + started2[4][0:1, 0:1], 1)
    started3 = _exchange_start("exchange3_start", [dwint1], [True], [_landing(own_rows(dwint1), me)])
    grad_x, dgn = _dx(dq, dk, dv, dr, wint, x2, dh2, g_norm + started3[4][0:1, 0:1])
    s_win0, = _exchange_wait("exchange2_wait", started2, [True], grad_x)
    dgn_sum, = _allreduce_small([dgn])
    s_win1, = _exchange_wait("exchange3_wait", started3, [True], dgn_sum)

    g_win, d_win, m_win, v_win = map(
        jnp.transpose, _adam_slots("w_in", [s_win0, s_win1], tr(w_in), tr(m_w_in), tr(v_w_in)))
    g_wkv, d_wkv, m_wkv, v_wkv = _adam_slots("w_mem_kv", [s_wkv], w_mem_kv[0], m_w_mem_kv[0], v_w_mem_kv[0])
    g_wout, d_wout, m_wout, v_wout = _adam_slots("w_out", [s_wout], w_out[0], m_w_out[0], v_w_out[0])

    small_shapes = [(1, D_MODEL), (4 * CHUNK, CHUNK), (4, CHUNK), (1, SGU_W), (1, D_MODEL), (1, D_MODEL)]
    pack = lambda arrs: [t.reshape(s) for t, s in zip(arrs, small_shapes)]
    g_small, loss_sum, upd = _adam_small(
        pack([g_norm, w_sgu_spatial, b_sgu_spatial, g_sgu_v, g_mem, g_final]),
        [dgn_sum, s_ws, s_bs, s_gv, s_gm, s_gf],
        pack([m_g_norm, m_w_sgu_spatial, m_b_sgu_spatial, m_g_sgu_v, m_g_mem, m_g_final]),
        pack([v_g_norm, v_w_sgu_spatial, v_b_sgu_spatial, v_g_sgu_v, v_g_mem, v_g_final]), s_loss)
    out_shapes = [g_norm.shape, w_sgu_spatial.shape, b_sgu_spatial.shape, g_sgu_v.shape, g_mem.shape, g_final.shape]
    unpack = lambda arrs: [t.reshape(s) for t, s in zip(arrs, out_shapes)]
    gs = unpack(g_small)
    ds, nms, nvs = unpack(upd[0::3]), unpack(upd[1::3]), unpack(upd[2::3])

    loss = loss_sum[0, 0]

    def assemble(small, win, wkv_, wout_):
        return [small[0], win[None], small[1], small[2], small[3], small[4], wkv_[None], wout_[None], small[5]]

    return (loss, grad_x.reshape(x.shape),
            *assemble(gs, g_win, g_wkv, g_wout), *assemble(ds, d_win, d_wkv, d_wout),
            *assemble(nms, m_win, m_wkv, m_wout), *assemble(nvs, v_win, v_wkv, v_wout))
```

```python
import jax
import jax.numpy as jnp
from jax import lax
from jax.experimental import pallas as pl
from jax.experimental.pallas import tpu as pltpu

F32 = jnp.float32
BF16 = jnp.bfloat16
SDS = jax.ShapeDtypeStruct
MESH = pl.DeviceIdType.MESH

N_DEV = 8
D_MODEL = 1024
SEQ = 2048
B_LOC = 2
T_LOC = B_LOC * SEQ
N_MEM = 256
HEAD = 64
ATTN_W = 512
SGU_W = 256
MEM_W = 256
IN_COLS = 3328
W_IN_SHARD = IN_COLS // N_DEV
CHUNK = 128
DILATIONS = ((1, 2048), (4, 512), (16, 128))
RADIUS = 64
EPS = 1e-6
NEG = -1e30
SCALE = HEAD ** -0.5
C_QA, C_KA, C_VA, C_ZA, C_UB, C_VB, C_ZB, C_QM, C_ZM = 0, 512, 1024, 1536, 2048, 2304, 2560, 2816, 3072
QKV_W = 1536
REST_W = IN_COLS - QKV_W

ADAM_LR, ADAM_B1, ADAM_B2, ADAM_EPS, ADAM_WD, ADAM_STEP = 0.001, 0.9, 0.999, 1e-08, 0.01, 10

V7X_VMEM_MIB = 64
VMEM_NO_STAGING_MIB = V7X_VMEM_MIB - 6


def _params(vmem_mib, sem=None):
    assert vmem_mib < V7X_VMEM_MIB
    return pltpu.CompilerParams(vmem_limit_bytes=vmem_mib << 20, dimension_semantics=sem)


_TOKEN = pl.BlockSpec(memory_space=pl.ANY)


def _dot(a, b):
    return jnp.dot(a.astype(BF16), b.astype(BF16), preferred_element_type=F32)


def _dot_nt(a, b):
    return lax.dot_general(a.astype(BF16), b.astype(BF16), (((1,), (1,)), ((), ())), preferred_element_type=F32)


def _dot_tn(a, b):
    return lax.dot_general(a.astype(BF16), b.astype(BF16), (((0,), (0,)), ((), ())), preferred_element_type=F32)


def _rstd(v):
    return lax.rsqrt(jnp.mean(v * v, axis=-1, keepdims=True) + EPS)


def _rms_bwd(v, r, g, dy):
    gdy = g * dy
    return r * gdy - v * (r * r * r * jnp.mean(gdy * v, axis=-1, keepdims=True))


def _sigmoid(z):
    return 1.0 / (1.0 + jnp.exp(-z))


def _silu_and_grad(z):
    s = _sigmoid(z)
    return z * s, s * (1.0 + z * (1.0 - s))


_G_C = 0.7978845608028654
_G_K = 0.044715


def _gelu_and_grad(v):
    t = jnp.tanh(_G_C * (v + _G_K * (v * v * v)))
    cdf = 0.5 * (1.0 + t)
    return v * cdf, cdf + 0.5 * v * (1.0 - t * t) * (_G_C * (1.0 + 3.0 * _G_K * v * v))


def _cast_rows(src_ref, dst_ref, rows, step=256):
    def one(i, carry):
        r = pl.ds(pl.multiple_of(i * step, step), step)
        dst_ref[r, :] = src_ref[r, :].astype(dst_ref.dtype)
        return carry
    lax.fori_loop(0, rows // step, one, 0)


def _left_lanes(rows):
    return lax.broadcasted_iota(jnp.int32, (rows, 128), 1) < HEAD


def _mesh_pos():
    return lax.axis_index("x"), lax.axis_index("y"), lax.axis_index("c")


def _peer(pos, k):
    x, y, c = pos
    return (1 - x if k & 4 else x, 1 - y if k & 2 else y, 1 - c if k & 1 else c)


def _flat(pos):
    return 4 * pos[0] + 2 * pos[1] + pos[2]


def _allgather_weights(w_in_t, w_kv, w_out):
    def body(win_ref, wkv_ref, wout_ref, wint_o, wkv_o, wout_o, send_sems, recv_sems):
        x, y, c = _mesh_pos()
        me, sib = (x, y, c), (x, y, 1 - c)
        chips = [(1 - x, y), (x, 1 - y), (1 - x, 1 - y)]

        def rows(p):
            return wint_o.at[pl.ds(pl.multiple_of(_flat(p) * W_IN_SHARD, 16), W_IN_SHARD), :]

        rows(me)[...] = win_ref[...].astype(BF16)

        def copy(k, block, to):
            return pltpu.make_async_remote_copy(
                src_ref=rows(block), dst_ref=rows(block), send_sem=send_sems.at[k], recv_sem=recv_sems.at[k],
                device_id=to, device_id_type=MESH)

        first = [copy(0, me, sib)] + [copy(1 + j, me, (*chip, c)) for j, chip in enumerate(chips)]
        for cp in first:
            cp.start()
        wkv_o[...] = wkv_ref[...].astype(BF16)
        wout_o[...] = wout_ref[...].astype(BF16)
        passed = []
        for j, chip in enumerate(chips):
            copy(1 + j, (*chip, c), me).wait_recv()
            fwd = copy(4 + j, (*chip, c), sib)
            fwd.start()
            passed.append(fwd)
        copy(0, sib, me).wait_recv()
        for j, chip in enumerate(chips):
            copy(4 + j, (*chip, 1 - c), me).wait_recv()
        for cp in first + passed:
            cp.wait_send()

    vmem = pl.BlockSpec(memory_space=pltpu.VMEM)
    return pl.pallas_call(
        body, name="allgather_weights",
        out_shape=(SDS((IN_COLS, D_MODEL), BF16), SDS(w_kv.shape, BF16), SDS(w_out.shape, BF16)),
        in_specs=[vmem, vmem, vmem], out_specs=(vmem, vmem, vmem),
        scratch_shapes=[pltpu.SemaphoreType.DMA((7,)), pltpu.SemaphoreType.DMA((7,))],
        compiler_params=_params(40),
    )(w_in_t, w_kv, w_out)


def _proj_fwd(x2, g_norm, wint, token):
    tm = 512
    sub = 256
    d = DILATIONS[2][0]
    per_ex = SEQ // tm

    def body(x_ref, g_ref, w_ref, _, o_ref, o16_ref):
        xv = x_ref[...]
        h = xv * _rstd(xv) * g_ref[...]
        res = _dot_nt(h, w_ref[...])
        o_ref[...] = res
        r_out = lax.broadcasted_iota(jnp.int32, (sub, sub), 0)
        r_in = lax.broadcasted_iota(jnp.int32, (sub, sub), 1)
        pick = (r_in == d * (r_out % (sub // d)) + r_out // (sub // d)).astype(BF16)
        for part in range(tm // sub):
            grouped = _dot(pick, res[part * sub:(part + 1) * sub, 0:QKV_W]).astype(BF16)
            for rho in range(d):
                o16_ref[0, rho, part * (sub // d):(part + 1) * (sub // d), :] = (
                    grouped[rho * (sub // d):(rho + 1) * (sub // d), :])

    return pl.pallas_call(
        body, name="proj_fwd", grid=(T_LOC // tm,),
        in_specs=[pl.BlockSpec((tm, D_MODEL), lambda i: (i, 0)), pl.BlockSpec((1, D_MODEL), lambda i: (0, 0)),
                  pl.BlockSpec((IN_COLS, D_MODEL), lambda i: (0, 0)), _TOKEN],
        out_specs=(pl.BlockSpec((tm, IN_COLS), lambda i: (i, 0)),
                   pl.BlockSpec((1, d, tm // d, QKV_W), lambda i: (i // per_ex, 0, i % per_ex, 0))),
        out_shape=(SDS((T_LOC, IN_COLS), F32), SDS((B_LOC, d, SEQ // d, QKV_W), BF16)),
        compiler_params=_params(48, ("arbitrary",)),
    )(x2, g_norm, wint, token)


def _memkv_fwd(mem2, g_mem, wkv):
    def body(m_ref, g_ref, w_ref, o_ref):
        mv = m_ref[...]
        o_ref[...] = _dot(mv * _rstd(mv) * g_ref[...], w_ref[...])

    return pl.pallas_call(
        body, name="memkv_fwd", out_shape=SDS((B_LOC * N_MEM, 2 * MEM_W), F32), compiler_params=_params(32),
    )(mem2, g_mem, wkv)


N_BIAS = 7


def _fill_bias_tables(sl_ref, tab):
    for cfg, (d, length) in enumerate(DILATIONS):
        nk = min(length, 2 * CHUNK)
        r = lax.broadcasted_iota(jnp.int32, (CHUNK, nk), 0)
        c = lax.broadcasted_iota(jnp.int32, (CHUNK, nk), 1)
        for var in range(3 if length > nk else 1):
            rel = jnp.abs(r - c + var * RADIUS)
            dist = rel.astype(F32) * float(d)
            for h in range(2):
                slope = sl_ref[0, 0:1, h * HEAD:h * HEAD + 1]
                tab[3 * cfg + var, h * CHUNK:(h + 1) * CHUNK, 0:nk] = jnp.where(rel <= RADIUS, -slope * dist, NEG)


def _attn_blocks(visit, unroll):
    def step(t, carry):
        for cfg, (d, length) in enumerate(DILATIONS):
            nblk = length // CHUNK
            if nblk == 1:
                visit(cfg, 0, t, t, length, t)
                continue
            rho, i = (0, t) if d == 1 else (t // nblk, t % nblk)
            ks = jnp.clip(i * CHUNK - RADIUS, 0, length - 2 * CHUNK)
            visit(cfg, (i * CHUNK - ks) // RADIUS, rho + d * (i * CHUNK), rho + d * ks, 2 * CHUNK, t)
        return carry
    lax.fori_loop(0, 16, step, 0, unroll=unroll)


def _stack_heads(v, left):
    return jnp.concatenate([jnp.where(left, v, 0.0), jnp.where(left, 0.0, v)], axis=0)


def _unstack_heads(v, left):
    return jnp.where(left, v[0:CHUNK], v[CHUNK:2 * CHUNK])


def _rows(start, n, d):
    return pl.ds(start, n) if d == 1 else pl.ds(start, n, stride=d)


def _blk16(col0):
    d, length = DILATIONS[2]
    return pl.BlockSpec((1, d, length, 128), lambda b, hp: (b, 0, 0, col0 // 128 + hp))


def _attn_fwd(proj, qkv16, slopes):
    def body(sl_ref, q_ref, k_ref, v_ref, q16_ref, k16_ref, v16_ref, a_ref, lse_ref, *scr):
        o_c, m_c, l_c, tab = scr[0:3], scr[3:6], scr[6:9], scr[9]
        left = _left_lanes(CHUNK)
        _fill_bias_tables(sl_ref, tab)

        def block(cfg, var, q0, k0, nk, t):
            d = DILATIONS[cfg][0]
            rq, rk = _rows(q0, CHUNK, d), _rows(k0, nk, d)
            if cfg == 2:
                qb, kw, vw = q16_ref[0, t].astype(F32), k16_ref[0, t], v16_ref[0, t]
            else:
                qb, kw, vw = q_ref[rq, :], k_ref[rk, :], v_ref[rk, :]
            qs = _stack_heads(qb * SCALE, left)
            s = _dot_nt(qs, kw) + tab[3 * cfg + var, :, 0:nk]
            m = jnp.max(s, axis=-1, keepdims=True)
            p = jnp.exp(s - m)
            o_c[cfg][rq, :] = _unstack_heads(_dot(p, vw), left)
            m_c[cfg][rq, :] = _unstack_heads(m, left)
            l_c[cfg][rq, :] = _unstack_heads(jnp.sum(p, axis=-1, keepdims=True), left)
        _attn_blocks(block, 8)

        def merge(j, carry):
            rows = pl.ds(pl.multiple_of(j * 256, 256), 256)
            ms = [m_c[i][rows, :] for i in range(3)]
            top = jnp.maximum(jnp.maximum(ms[0], ms[1]), ms[2])
            ws = [jnp.exp(m - top) for m in ms]
            den = l_c[0][rows, :] * ws[0] + l_c[1][rows, :] * ws[1] + l_c[2][rows, :] * ws[2]
            num = o_c[0][rows, :] * ws[0] + o_c[1][rows, :] * ws[1] + o_c[2][rows, :] * ws[2]
            a_ref[rows, :] = num / den
            lse_ref[rows, :] = top + jnp.log(den)
            return carry
        lax.fori_loop(0, SEQ // 256, merge, 0)

    blk = lambda col0: pl.BlockSpec((SEQ, 128), lambda b, hp: (b, col0 // 128 + hp))
    out = pl.BlockSpec((SEQ, 128), lambda b, hp: (b, hp))
    return pl.pallas_call(
        body, name="attn_fwd", grid=(B_LOC, 4),
        in_specs=[pl.BlockSpec((1, 8, 128), lambda b, hp: (hp, 0, 0)), blk(C_QA), blk(C_KA), blk(C_VA),
                  _blk16(C_QA), _blk16(C_KA), _blk16(C_VA)],
        out_specs=(out, out),
        out_shape=(SDS((T_LOC, ATTN_W), F32), SDS((T_LOC, ATTN_W), F32)),
        scratch_shapes=[pltpu.VMEM((SEQ, 128), F32)] * 9 + [pltpu.VMEM((N_BIAS, 2 * CHUNK, 2 * CHUNK), F32)],
        compiler_params=_params(40, ("arbitrary", "arbitrary")),
    )(slopes, proj, proj, proj, qkv16, qkv16, qkv16)


def _chunks_side_by_side(v, pr, tm):
    return jnp.concatenate([v[ch * CHUNK:(ch + 1) * CHUNK, pr * 128:(pr + 1) * 128] for ch in range(tm // CHUNK)], axis=1)


def _first_group_lanes(tm):
    return lax.broadcasted_iota(jnp.int32, (CHUNK, tm), 1) % 128 < HEAD


def _store_chunks(dst_ref, pr, val, tm):
    for ch in range(tm // CHUNK):
        dst_ref[ch * CHUNK:(ch + 1) * CHUNK, pr * 128:(pr + 1) * 128] = val[:, ch * CHUNK:(ch + 1) * CHUNK]


def _sgu_mix(vn, ws_ref, dst_ref, tm):
    first = _first_group_lanes(tm)
    for pr in range(2):
        vp = _chunks_side_by_side(vn, pr, tm)
        _store_chunks(dst_ref, pr, jnp.where(first, _dot(ws_ref[2 * pr], vp), _dot(ws_ref[2 * pr + 1], vp)), tm)


def _mem_head_of_lane(rows):
    return lax.broadcasted_iota(jnp.int32, (rows, MEM_W), 1) // HEAD


def _stack_mem_heads(v, rows):
    head = _mem_head_of_lane(rows)
    return jnp.concatenate([jnp.where(head == h, v, 0.0) for h in range(4)], axis=0)


def _unstack_mem_heads(v, rows):
    head = _mem_head_of_lane(rows)
    out = v[0:rows]
    for h in range(1, 4):
        out = jnp.where(head == h, v[h * rows:(h + 1) * rows], out)
    return out


def _mem_attn_probs(q, kmem, rows):
    qs = _stack_mem_heads(q, rows).astype(BF16)
    s = _dot_nt(qs, kmem) * SCALE
    e = jnp.exp(s - jnp.max(s, axis=-1, keepdims=True))
    return e * (1.0 / jnp.sum(e, axis=-1, keepdims=True)), qs


def _branch_blocks(tm):
    col = lambda w, c0: pl.BlockSpec((tm, w), lambda i: (i, c0 // w))
    return [col(512, C_ZA), col(256, C_UB), col(256, C_VB), col(256, C_ZB), col(256, C_QM), col(256, C_ZM)]


def _branch_fwd(proj, a, kv, w_s, b_exp, g_v):
    tm = 512
    per_ex = SEQ // tm

    def body(za_ref, ub_ref, vb_ref, zb_ref, qm_ref, zm_ref, a_ref, kv_ref, ws_ref, be_ref, gv_ref, o_ref, mix):
        o_ref[:, 0:ATTN_W] = (_silu_and_grad(za_ref[...])[0] * a_ref[...]).astype(BF16)
        gu = _gelu_and_grad(ub_ref[...])[0]
        gv = _gelu_and_grad(vb_ref[...])[0]
        vn = gv * _rstd(gv) * gv_ref[...]
        _sgu_mix(vn.astype(BF16), ws_ref, mix, tm)
        sg = gu * (mix[...] + jnp.concatenate([be_ref[...]] * (tm // CHUNK), axis=0))
        o_ref[:, ATTN_W:ATTN_W + SGU_W] = (_silu_and_grad(zb_ref[...])[0] * sg).astype(BF16)
        p = _mem_attn_probs(qm_ref[...], kv_ref[:, 0:MEM_W], tm)[0]
        mo = _unstack_mem_heads(_dot(p, kv_ref[:, MEM_W:2 * MEM_W]), tm)
        o_ref[:, ATTN_W + SGU_W:D_MODEL] = (_silu_and_grad(zm_ref[...])[0] * mo).astype(BF16)

    full = lambda shape: pl.BlockSpec(shape, lambda i: (0,) * len(shape))
    return pl.pallas_call(
        body, name="branch_fwd", grid=(T_LOC // tm,),
        in_specs=_branch_blocks(tm) + [
            pl.BlockSpec((tm, ATTN_W), lambda i: (i, 0)), pl.BlockSpec((N_MEM, 2 * MEM_W), lambda i: (i // per_ex, 0)),
            full((4, CHUNK, CHUNK)), full((CHUNK, SGU_W)), full((1, SGU_W))],
        out_specs=pl.BlockSpec((tm, D_MODEL), lambda i: (i, 0)),
        out_shape=SDS((T_LOC, D_MODEL), BF16),
        scratch_shapes=[pltpu.VMEM((tm, SGU_W), F32)],
        compiler_params=_params(VMEM_NO_STAGING_MIB, ("arbitrary",)),
    )(proj, proj, proj, proj, proj, proj, a, kv, w_s, b_exp, g_v)


def _outproj_loss(gated, wout, x2, tgt2, g_final):
    tm = 512

    def body(g_ref, w_ref, x_ref, t_ref, gf_ref, dh2_ref, loss_ref, dgf_ref, dwo_ref, dwo_acc):
        @pl.when(pl.program_id(0) == 0)
        def _():
            loss_ref[...] = jnp.zeros_like(loss_ref)
            dgf_ref[...] = jnp.zeros_like(dgf_ref)
            dwo_acc[...] = jnp.zeros_like(dwo_acc)
        gated = g_ref[...]
        h2 = x_ref[...] + _dot(gated, w_ref[...])
        r = _rstd(h2)
        gf = gf_ref[...]
        err = h2 * r * gf - t_ref[...]
        loss_ref[...] += 0.5 * jnp.sum(jnp.mean(err * err, axis=-1, keepdims=True))
        dy = err * (1.0 / D_MODEL)
        dh2 = _rms_bwd(h2, r, gf, dy)
        dh2_ref[...] = dh2
        dgf_ref[...] += jnp.sum(dy * (h2 * r), axis=0, keepdims=True)
        dwo_acc[...] += _dot_tn(gated, dh2)

        @pl.when(pl.program_id(0) == pl.num_programs(0) - 1)
        def _():
            _cast_rows(dwo_acc, dwo_ref, D_MODEL)

    row = pl.BlockSpec((tm, D_MODEL), lambda i: (i, 0))
    vec = pl.BlockSpec((1, D_MODEL), lambda i: (0, 0))
    square = pl.BlockSpec((D_MODEL, D_MODEL), lambda i: (0, 0))
    return pl.pallas_call(
        body, name="outproj_loss", grid=(T_LOC // tm,),
        in_specs=[row, square, row, row, vec],
        out_specs=(row, pl.BlockSpec((8, 128), lambda i: (0, 0)), vec, square),
        out_shape=(SDS((T_LOC, D_MODEL), F32), SDS((8, 128), F32), SDS((1, D_MODEL), F32), SDS((D_MODEL, D_MODEL), BF16)),
        scratch_shapes=[pltpu.VMEM((D_MODEL, D_MODEL), F32)],
        compiler_params=_params(VMEM_NO_STAGING_MIB, ("arbitrary",)),
    )(gated, wout, x2, tgt2, g_final)


def _branch_bwd(dh2, wout, proj, a, kv, w_s, b_exp, g_v):
    tm = 512
    per_ex = SEQ // tm

    def body(dh2_ref, w_ref, za_ref, ub_ref, vb_ref, zb_ref, qm_ref, zm_ref, a_ref, kv_ref, ws_ref,
             be_ref, gv_ref, da_ref, dr_ref, dkv_ref, dws_ref, db_ref, dgv_ref, mix, dvn, dmsum):
        i = pl.program_id(0)

        @pl.when(i == 0)
        def _():
            dws_ref[...] = jnp.zeros_like(dws_ref)
            dgv_ref[...] = jnp.zeros_like(dgv_ref)
            dmsum[...] = jnp.zeros_like(dmsum)

        @pl.when(i % per_ex == 0)
        def _():
            dkv_ref[...] = jnp.zeros_like(dkv_ref)

        dg = _dot_nt(dh2_ref[...], w_ref[...])

        sa, dsa = _silu_and_grad(za_ref[...])
        dga = dg[:, 0:ATTN_W]
        da_ref[...] = dga * sa
        dr_ref[:, 0:512] = (dga * a_ref[...] * dsa).astype(BF16)

        ub, vb = ub_ref[...], vb_ref[...]
        gu, dgu = _gelu_and_grad(ub)
        gv, dgv = _gelu_and_grad(vb)
        rv = _rstd(gv)
        gain = gv_ref[...]
        vn = (gv * rv * gain).astype(BF16)
        _sgu_mix(vn, ws_ref, mix, tm)
        mixed = mix[...] + jnp.concatenate([be_ref[...]] * (tm // CHUNK), axis=0)
        sb, dsb = _silu_and_grad(zb_ref[...])
        dgb = dg[:, ATTN_W:ATTN_W + SGU_W]
        dsg = dgb * sb
        dr_ref[:, 512:768] = (dsg * mixed * dgu).astype(BF16)
        dr_ref[:, 1024:1280] = (dgb * (gu * mixed) * dsb).astype(BF16)
        dmix = dsg * gu
        for ch in range(tm // CHUNK):
            dmsum[...] += dmix[ch * CHUNK:(ch + 1) * CHUNK, :]
        first = _first_group_lanes(tm)
        for pr in range(2):
            dmp, vp = _chunks_side_by_side(dmix, pr, tm), _chunks_side_by_side(vn, pr, tm)
            dws_ref[2 * pr] += _dot_nt(jnp.where(first, dmp, 0.0), vp)
            dws_ref[2 * pr + 1] += _dot_nt(jnp.where(first, 0.0, dmp), vp)
            _store_chunks(dvn, pr, jnp.where(first, _dot_tn(ws_ref[2 * pr], dmp), _dot_tn(ws_ref[2 * pr + 1], dmp)), tm)
        dvn_v = dvn[...]
        dgv_ref[...] += jnp.sum(dvn_v * (gv * rv), axis=0, keepdims=True)
        dr_ref[:, 768:1024] = (_rms_bwd(gv, rv, gain, dvn_v) * dgv).astype(BF16)

        szm, dszm = _silu_and_grad(zm_ref[...])
        dgm = dg[:, ATTN_W + SGU_W:D_MODEL]
        kmem, vmem_ = kv_ref[:, 0:MEM_W].astype(BF16), kv_ref[:, MEM_W:2 * MEM_W].astype(BF16)
        p, qs = _mem_attn_probs(qm_ref[...], kmem, tm)
        dmos = _stack_mem_heads(dgm * szm, tm).astype(BF16)
        dr_ref[:, 1536:1792] = (dgm * _unstack_mem_heads(_dot(p, vmem_), tm) * dszm).astype(BF16)
        dp = _dot_nt(dmos, vmem_)
        ds = (p * (dp - jnp.sum(p * dp, axis=-1, keepdims=True)) * SCALE).astype(BF16)
        dr_ref[:, 1280:1536] = _unstack_mem_heads(_dot(ds, kmem), tm).astype(BF16)
        dkv_ref[:, 0:MEM_W] += _dot_tn(ds, qs)
        dkv_ref[:, MEM_W:2 * MEM_W] += _dot_tn(p, dmos)

        @pl.when(i == pl.num_programs(0) - 1)
        def _():
            tot = dmsum[...]
            hi = tot.astype(BF16)
            lo = (tot - hi.astype(F32)).astype(BF16)
            grp = (lax.broadcasted_iota(jnp.int32, (SGU_W, 128), 0) // HEAD
                   == lax.broadcasted_iota(jnp.int32, (SGU_W, 128), 1)).astype(BF16)
            db_ref[...] = (_dot(hi, grp) + _dot(lo, grp)).T[0:4, :]

    full = lambda shape: pl.BlockSpec(shape, lambda i: (0,) * len(shape))
    row = lambda w: pl.BlockSpec((tm, w), lambda i: (i, 0))
    return pl.pallas_call(
        body, name="branch_bwd", grid=(T_LOC // tm,),
        in_specs=[row(D_MODEL), full((D_MODEL, D_MODEL))] + _branch_blocks(tm) + [
            row(ATTN_W), pl.BlockSpec((N_MEM, 2 * MEM_W), lambda i: (i // per_ex, 0)),
            full((4, CHUNK, CHUNK)), full((CHUNK, SGU_W)), full((1, SGU_W))],
        out_specs=(row(ATTN_W), row(REST_W), pl.BlockSpec((N_MEM, 2 * MEM_W), lambda i: (i // per_ex, 0)),
                   full((4, CHUNK, CHUNK)), full((4, CHUNK)), full((1, SGU_W))),
        out_shape=(SDS((T_LOC, ATTN_W), F32), SDS((T_LOC, REST_W), BF16), SDS((B_LOC * N_MEM, 2 * MEM_W), F32),
                   SDS((4, CHUNK, CHUNK), F32), SDS((4, CHUNK), F32), SDS((1, SGU_W), F32)),
        scratch_shapes=[pltpu.VMEM((tm, SGU_W), F32), pltpu.VMEM((tm, SGU_W), F32), pltpu.VMEM((CHUNK, SGU_W), F32)],
        compiler_params=_params(56, ("arbitrary",)),
    )(dh2, wout, proj, proj, proj, proj, proj, proj, a, kv, w_s, b_exp, g_v)


def _attn_bwd(proj, qkv16, slopes, da, a, lse, token):
    def body(sl_ref, q_ref, k_ref, v_ref, q16_ref, k16_ref, v16_ref, da_ref, a_ref, lse_ref, _,
             dq_ref, dk_ref, dv_ref, *scr):
        dq_s, dk_s, dv_s, tab = scr[0:3], scr[3:6], scr[6:9], scr[9]
        lse_h, delta_h = scr[10:12], scr[12:14]
        p_all, ds_all = scr[14], scr[15]
        left = _left_lanes(CHUNK)
        _fill_bias_tables(sl_ref, tab)

        def prep(j, carry):
            rows = pl.ds(pl.multiple_of(j * 256, 256), 256)
            l256 = _left_lanes(256)
            prod = da_ref[rows, :] * a_ref[rows, :]
            delta_h[0][rows, :] = jnp.broadcast_to(jnp.sum(jnp.where(l256, prod, 0.0), axis=-1, keepdims=True), (256, 128))
            delta_h[1][rows, :] = jnp.broadcast_to(jnp.sum(jnp.where(l256, 0.0, prod), axis=-1, keepdims=True), (256, 128))
            pair = lse_ref[rows, :]
            other = pltpu.roll(pair, HEAD, axis=1)
            lse_h[0][rows, :] = jnp.where(l256, pair, other)
            lse_h[1][rows, :] = jnp.where(l256, other, pair)
            zero = jnp.zeros((256, 128), F32)
            for cfg in range(3):
                dk_s[cfg][rows, :] = zero
                dv_s[cfg][rows, :] = zero
            return carry
        lax.fori_loop(0, SEQ // 256, prep, 0)

        def per_row(halves, rq, nk):
            v = jnp.concatenate([halves[0][rq, :], halves[1][rq, :]], axis=0)
            return v if nk == 128 else jnp.concatenate([v, v], axis=1)

        def qkv(cfg, rq, rk, t):
            if cfg == 2:
                return q16_ref[0, t].astype(F32), k16_ref[0, t], v16_ref[0, t]
            return q_ref[rq, :], k_ref[rk, :], v_ref[rk, :]

        def probs(cfg, var, q0, k0, nk, t):
            d = DILATIONS[cfg][0]
            rq, rk = _rows(q0, CHUNK, d), _rows(k0, nk, d)
            qb, kw, vw = qkv(cfg, rq, rk, t)
            qs = _stack_heads(qb * SCALE, left)
            das = _stack_heads(da_ref[rq, :], left)
            s = _dot_nt(qs, kw) + tab[3 * cfg + var, :, 0:nk]
            p = jnp.exp(s - per_row(lse_h, rq, nk))
            p_all[16 * cfg + t, :, 0:nk] = p.astype(BF16)
            ds_all[16 * cfg + t, :, 0:nk] = (p * (_dot_nt(das, vw) - per_row(delta_h, rq, nk))).astype(BF16)
        _attn_blocks(probs, 4)

        def grads(cfg, var, q0, k0, nk, t):
            d = DILATIONS[cfg][0]
            rq, rk = _rows(q0, CHUNK, d), _rows(k0, nk, d)
            qb, kw, _ = qkv(cfg, rq, rk, t)
            qs = _stack_heads(qb * SCALE, left).astype(BF16)
            das = _stack_heads(da_ref[rq, :], left).astype(BF16)
            p, ds = p_all[16 * cfg + t, :, 0:nk], ds_all[16 * cfg + t, :, 0:nk]
            dq_s[cfg][rq, :] = _unstack_heads(_dot(ds, kw), left) * SCALE
            dk_s[cfg][rk, :] += _dot_tn(ds, qs)
            dv_s[cfg][rk, :] += _dot_tn(p, das)
        _attn_blocks(grads, 4)

        def flush(j, carry):
            rows = pl.ds(pl.multiple_of(j * 256, 256), 256)
            for acc, dst in ((dq_s, dq_ref), (dk_s, dk_ref), (dv_s, dv_ref)):
                dst[rows, :] = (acc[0][rows, :] + acc[1][rows, :] + acc[2][rows, :]).astype(BF16)
            return carry
        lax.fori_loop(0, SEQ // 256, flush, 0)

    blk = lambda col0: pl.BlockSpec((SEQ, 128), lambda b, hp: (b, col0 // 128 + hp))
    own = pl.BlockSpec((SEQ, 128), lambda b, hp: (b, hp))
    return pl.pallas_call(
        body, name="attn_bwd", grid=(B_LOC, 4),
        in_specs=[pl.BlockSpec((1, 8, 128), lambda b, hp: (hp, 0, 0)), blk(C_QA), blk(C_KA), blk(C_VA),
                  _blk16(C_QA), _blk16(C_KA), _blk16(C_VA), own, own, own, _TOKEN],
        out_specs=(own, own, own),
        out_shape=(SDS((T_LOC, ATTN_W), BF16),) * 3,
        scratch_shapes=[pltpu.VMEM((SEQ, 128), F32)] * 9 + [pltpu.VMEM((N_BIAS, 2 * CHUNK, 2 * CHUNK), F32)]
        + [pltpu.VMEM((SEQ, 128), F32)] * 4 + [pltpu.VMEM((48, 2 * CHUNK, 2 * CHUNK), BF16)] * 2,
        compiler_params=_params(52, ("arbitrary", "arbitrary")),
    )(slopes, proj, proj, proj, qkv16, qkv16, qkv16, da, a, lse, token)


def _dproj_specs(tm):
    third = pl.BlockSpec((tm, ATTN_W), lambda i: (i, 0))
    return [third, third, third, pl.BlockSpec((tm, REST_W), lambda i: (i, 0))]


def _dx(dq, dk, dv, dr, wint, x2, dh2, g_norm, token):
    tm = 512

    def body(dq_ref, dk_ref, dv_ref, dr_ref, w_ref, x_ref, dh2_ref, g_ref, _, gx_ref, dgn_ref):
        @pl.when(pl.program_id(0) == 0)
        def _():
            dgn_ref[...] = jnp.zeros_like(dgn_ref)
        dh = (_dot(dq_ref[...], w_ref[C_QA:C_KA, :]) + _dot(dk_ref[...], w_ref[C_KA:C_VA, :])
              + _dot(dv_ref[...], w_ref[C_VA:C_ZA, :]) + _dot(dr_ref[...], w_ref[C_ZA:IN_COLS, :]))
        xv = x_ref[...]
        r = _rstd(xv)
        gx_ref[...] = dh2_ref[...] + _rms_bwd(xv, r, g_ref[...], dh)
        dgn_ref[...] += jnp.sum(dh * (xv * r), axis=0, keepdims=True)

    row = pl.BlockSpec((tm, D_MODEL), lambda i: (i, 0))
    vec = pl.BlockSpec((1, D_MODEL), lambda i: (0, 0))
    return pl.pallas_call(
        body, name="dx", grid=(T_LOC // tm,),
        in_specs=_dproj_specs(tm) + [pl.BlockSpec((IN_COLS, D_MODEL), lambda i: (0, 0)), row, row, vec, _TOKEN],
        out_specs=(row, vec),
        out_shape=(SDS((T_LOC, D_MODEL), F32), SDS((1, D_MODEL), F32)),
        compiler_params=_params(48, ("arbitrary",)),
    )(dq, dk, dv, dr, wint, x2, dh2, g_norm, token)


def _dwin(dq, dk, dv, dr, x2, g_norm, half, token):
    tm = 1024
    width = D_MODEL // 2
    cols = slice(half * width, (half + 1) * width)

    def body(dq_ref, dk_ref, dv_ref, dr_ref, x_ref, g_ref, _, o_ref, acc):
        @pl.when(pl.program_id(0) == 0)
        def _():
            acc[...] = jnp.zeros_like(acc)
        xv = x_ref[...]
        h = (xv[:, cols] * _rstd(xv) * g_ref[:, cols]).astype(BF16)
        acc[C_QA:C_KA, :] += _dot_tn(dq_ref[...], h)
        acc[C_KA:C_VA, :] += _dot_tn(dk_ref[...], h)
        acc[C_VA:C_ZA, :] += _dot_tn(dv_ref[...], h)
        acc[C_ZA:IN_COLS, :] += _dot_tn(dr_ref[...], h)

        @pl.when(pl.program_id(0) == pl.num_programs(0) - 1)
        def _():
            _cast_rows(acc, o_ref, IN_COLS)

    return pl.pallas_call(
        body, name="dwin%d" % half, grid=(T_LOC // tm,),
        in_specs=_dproj_specs(tm) + [pl.BlockSpec((tm, D_MODEL), lambda i: (i, 0)),
                                     pl.BlockSpec((1, D_MODEL), lambda i: (0, 0)), _TOKEN],
        out_specs=pl.BlockSpec((IN_COLS, width), lambda i: (0, 0)),
        out_shape=SDS((IN_COLS, width), BF16),
        scratch_shapes=[pltpu.VMEM((IN_COLS, width), F32)],
        compiler_params=_params(48, ("arbitrary",)),
    )(dq, dk, dv, dr, x2, g_norm, token)


def _memkv_bwd(dkv, mem2, g_mem, wkv):
    def body(dkv_ref, m_ref, g_ref, w_ref, dw_ref, dg_ref):
        mv = m_ref[...]
        r = _rstd(mv)
        dkv_v = dkv_ref[...].astype(BF16)
        dw_ref[...] = _dot_tn(mv * r * g_ref[...], dkv_v).astype(BF16)
        dg_ref[...] = jnp.sum(_dot_nt(dkv_v, w_ref[...]) * (mv * r), axis=0, keepdims=True)

    return pl.pallas_call(
        body, name="memkv_bwd", out_shape=(SDS((D_MODEL, 2 * MEM_W), BF16), SDS((1, D_MODEL), F32)),
        compiler_params=_params(32),
    )(dkv, mem2, g_mem, wkv)


def _allreduce_small(parts):
    n = len(parts)

    def body(*refs):
        ins, outs, bufs = refs[0:n], refs[n:2 * n], refs[2 * n:3 * n]
        send_sems, recv_sems = refs[3 * n], refs[3 * n + 1]
        pos = _mesh_pos()
        me = _flat(pos)
        for a in range(n):
            bufs[a][me] = ins[a][...]

        def copy(a, k, slot):
            return pltpu.make_async_remote_copy(
                src_ref=ins[a], dst_ref=bufs[a].at[slot],
                send_sem=send_sems.at[7 * a + k - 1], recv_sem=recv_sems.at[7 * a + k - 1],
                device_id=_peer(pos, k), device_id_type=MESH)

        sent = [copy(a, k, me) for a in range(n) for k in range(1, N_DEV)]
        for cp in sent:
            cp.start()
        for a in range(n):
            for k in range(1, N_DEV):
                copy(a, k, _flat(_peer(pos, k))).wait_recv()
        for cp in sent:
            cp.wait_send()
        for a in range(n):
            acc = bufs[a][0]
            for s in range(1, N_DEV):
                acc = acc + bufs[a][s]
            outs[a][...] = acc

    vmem = pl.BlockSpec(memory_space=pltpu.VMEM)
    return pl.pallas_call(
        body, name="allreduce_small",
        out_shape=tuple(SDS(p.shape, F32) for p in parts),
        in_specs=[vmem] * n, out_specs=(vmem,) * n,
        scratch_shapes=[pltpu.VMEM((N_DEV,) + p.shape, F32) for p in parts]
        + [pltpu.SemaphoreType.DMA((7 * n,)), pltpu.SemaphoreType.DMA((7 * n,))],
        compiler_params=_params(16),
    )(*parts)


_HBM = pl.BlockSpec(memory_space=pltpu.HBM)
_SEM = pl.BlockSpec(memory_space=pltpu.SEMAPHORE)
_SIDE_EFFECT = pltpu.SideEffectType.DATAFLOW_SIDE_EFFECTING


def _exchange_copies(src_refs, land_refs, scatter, send_sems, recv_sems):
    pos = _mesh_pos()
    copies = []
    for a, (src, land) in enumerate(zip(src_refs, land_refs)):
        n = land.shape[1]
        for k in range(1, N_DEV):
            peer = _peer(pos, k)
            piece = src.at[pl.ds(pl.multiple_of(_flat(peer) * n, 16), n), :] if scatter[a] else src
            copies.append(pltpu.make_async_remote_copy(
                src_ref=piece, dst_ref=land.at[_flat(pos)],
                send_sem=send_sems.at[7 * a + k - 1], recv_sem=recv_sems.at[7 * a + k - 1],
                device_id=peer, device_id_type=MESH))
    return copies


def _exchange_start(name, srcs, scatter, lands):
    n = len(srcs)

    def body(*refs):
        for cp in _exchange_copies(refs[0:n], refs[n:2 * n], scatter, refs[2 * n], refs[2 * n + 1]):
            cp.start()
        refs[-1][...] = jnp.zeros_like(refs[-1])

    ops = [pltpu.with_memory_space_constraint(t, pltpu.HBM) for t in (*srcs, *lands)]
    out = pl.pallas_call(
        body, name=name,
        out_shape=(pltpu.SemaphoreType.DMA((7 * n,)), pltpu.SemaphoreType.DMA((7 * n,)),
                   *[pltpu.HBM(t.shape, t.dtype) for t in ops], SDS((8, 128), F32)),
        in_specs=[_HBM] * (2 * n),
        out_specs=(_SEM, _SEM, *[_HBM] * (2 * n), pl.BlockSpec(memory_space=pltpu.VMEM)),
        input_output_aliases={i: 2 + i for i in range(2 * n)},
        compiler_params=pltpu.CompilerParams(has_side_effects=_SIDE_EFFECT),
    )(*ops)
    return out[0], out[1], out[2:2 + n], out[2 + n:2 + 2 * n], out[-1]


def _exchange_wait(name, started, scatter, after):
    send_sems, recv_sems, srcs, lands, _ = started
    n = len(srcs)

    def body(*refs):
        for cp in _exchange_copies(refs[0:n], refs[n:2 * n], scatter, refs[2 * n], refs[2 * n + 1]):
            cp.wait_send()
            cp.wait_recv()

    out = pl.pallas_call(
        body, name=name,
        out_shape=tuple(pltpu.HBM(t.shape, t.dtype) for t in (*srcs, *lands)),
        in_specs=[_HBM] * (2 * n) + [_SEM, _SEM, pl.BlockSpec(memory_space=pl.ANY)],
        out_specs=(_HBM,) * (2 * n),
        input_output_aliases={i: i for i in range(2 * n)},
        compiler_params=pltpu.CompilerParams(has_side_effects=_SIDE_EFFECT),
    )(*srcs, *lands, send_sems, recv_sems, after)
    return out[n:]


def _landing(own, me):
    return lax.dynamic_update_slice(lax.empty((N_DEV,) + own.shape, own.dtype), own[None], (me,) + (0,) * own.ndim)


def _adamw(w, g, m, v):
    m = ADAM_B1 * m + (1.0 - ADAM_B1) * g
    v = ADAM_B2 * v + (1.0 - ADAM_B2) * (g * g)
    m_hat = m / (1.0 - ADAM_B1 ** ADAM_STEP)
    v_hat = v / (1.0 - ADAM_B2 ** ADAM_STEP)
    return -ADAM_LR * (m_hat / (jnp.sqrt(v_hat) + ADAM_EPS) + ADAM_WD * w), m, v


def _adam_slots(name, pieces, w, m, v):
    rows, cols = w.shape
    starts = [sum(p.shape[2] for p in pieces[:i]) for i in range(len(pieces) + 1)]
    assert starts[-1] == cols and all(p.shape[1] == rows for p in pieces)

    def body(*refs):
        s_refs, (w_ref, m_ref, v_ref, g_o, d_o, m_o, v_o, acc) = refs[:len(pieces)], refs[len(pieces):]
        s = pl.program_id(0)

        @pl.when(s == 0)
        def _():
            for i, s_ref in enumerate(s_refs):
                acc[:, starts[i]:starts[i + 1]] = s_ref[0].astype(F32)

        @pl.when(s > 0)
        def _():
            for i, s_ref in enumerate(s_refs):
                acc[:, starts[i]:starts[i + 1]] += s_ref[0].astype(F32)

        @pl.when(s == N_DEV - 1)
        def _():
            g = acc[...]
            g_o[...] = g
            d_o[...], m_o[...], v_o[...] = _adamw(w_ref[...], g, m_ref[...], v_ref[...])

    full = pl.BlockSpec((rows, cols), lambda s: (0, 0))
    return pl.pallas_call(
        body, name="adam_" + name, grid=(N_DEV,),
        in_specs=[pl.BlockSpec((1, rows, p.shape[2]), lambda s: (s, 0, 0)) for p in pieces] + [full, full, full],
        out_specs=(full,) * 4, out_shape=(SDS((rows, cols), F32),) * 4,
        scratch_shapes=[pltpu.VMEM((rows, cols), F32)],
        compiler_params=_params(40, ("arbitrary",)),
    )(*pieces, w, m, v)


def _adam_small(ws, gs, ms, vs, loss_slots):
    n = len(ws)

    def total(ref, like):
        if len(ref.shape) == len(like.shape):
            return ref[...]
        acc = ref[0]
        for s in range(1, N_DEV):
            acc = acc + ref[s]
        return acc

    def body(*refs):
        w_r, g_r, m_r, v_r = refs[0:n], refs[n:2 * n], refs[2 * n:3 * n], refs[3 * n:4 * n]
        loss_r, outs = refs[4 * n], refs[4 * n + 1:]
        for a in range(n):
            g = total(g_r[a], w_r[a])
            outs[a][...] = g
            outs[n + 1 + 3 * a][...], outs[n + 2 + 3 * a][...], outs[n + 3 + 3 * a][...] = _adamw(
                w_r[a][...], g, m_r[a][...], v_r[a][...])
        outs[n][...] = total(loss_r, outs[n])

    out = pl.pallas_call(
        body, name="adam_small",
        out_shape=tuple(SDS(w.shape, F32) for w in ws) + (SDS(loss_slots.shape[1:], F32),)
        + tuple(SDS(w.shape, F32) for w in ws for _ in range(3)),
        compiler_params=_params(16),
    )(*ws, *gs, *ms, *vs, loss_slots)
    return out[0:n], out[n], out[n + 1:]


def kernel(x, mem, g_norm, w_in, w_sgu_spatial, b_sgu_spatial, g_sgu_v, g_mem, w_mem_kv, w_out, g_final, loss_target, m_g_norm, m_w_in, m_w_sgu_spatial, m_b_sgu_spatial, m_g_sgu_v, m_g_mem, m_w_mem_kv, m_w_out, m_g_final, v_g_norm, v_w_in, v_w_sgu_spatial, v_b_sgu_spatial, v_g_sgu_v, v_g_mem, v_w_mem_kv, v_w_out, v_g_final):
    x2 = x.reshape(T_LOC, D_MODEL)
    tgt2 = loss_target.reshape(T_LOC, D_MODEL)
    mem2 = mem.reshape(B_LOC * N_MEM, D_MODEL)
    w_s = w_sgu_spatial[0]
    b_exp = jnp.repeat(b_sgu_spatial[0].T, HEAD, axis=1)
    slope = jnp.power(2.0, -8.0 * (jnp.arange(8, dtype=F32) + 1.0) / 8)
    slopes = jnp.broadcast_to(jnp.repeat(slope.reshape(4, 2), HEAD, axis=1)[:, None, :], (4, 8, 128))

    tr = lambda t: jnp.transpose(t[0])

    me = _flat(_mesh_pos())
    own_rows = lambda t: lax.dynamic_slice_in_dim(t, me * (t.shape[0] // N_DEV), t.shape[0] // N_DEV)

    wint, wkv_own, wout_own = _allgather_weights(tr(w_in), w_mem_kv[0], w_out[0])
    started0 = _exchange_start("exchange0_start", [wkv_own, wout_own], [False, False],
                               [_landing(wkv_own, me), _landing(wout_own, me)])
    proj, qkv16 = _proj_fwd(x2, g_norm, wint, started0[4])
    wkv, wout = _exchange_wait("exchange0_wait", started0, [False, False], proj)
    wkv, wout = wkv.reshape(D_MODEL, 2 * MEM_W), wout.reshape(D_MODEL, D_MODEL)
    kv = _memkv_fwd(mem2, g_mem, wkv)
    a, lse = _attn_fwd(proj, qkv16, slopes)
    gated = _branch_fwd(proj, a, kv, w_s, b_exp, g_sgu_v)
    dh2, loss8, dgf, dwout = _outproj_loss(gated, wout, x2, tgt2, g_final.reshape(1, D_MODEL))

    da, dr, dkv, dws, dbs, dgv = _branch_bwd(dh2, wout, proj, a, kv, w_s, b_exp, g_sgu_v)
    dwkv, dgm = _memkv_bwd(dkv, mem2, g_mem, wkv)

    early = [dws.reshape(4 * CHUNK, CHUNK), dbs, dgv, dgm, dgf, loss8]
    scatter1 = [True, True] + [False] * len(early)
    started1 = _exchange_start(
        "exchange1_start", [dwkv, dwout] + early, scatter1,
        [_landing(own_rows(dwkv), me), _landing(own_rows(dwout), me)] + [_landing(t, me) for t in early])
    dq, dk, dv = _attn_bwd(proj, qkv16, slopes, da, a, lse, started1[4])
    s_wkv, s_wout, s_ws, s_bs, s_gv, s_gm, s_gf, s_loss = _exchange_wait("exchange1_wait", started1, scatter1, dq)

    dwint0 = _dwin(dq, dk, dv, dr, x2, g_norm, 0, started1[4])
    started2 = _exchange_start("exchange2_start", [dwint0], [True], [_landing(own_rows(dwint0), me)])
    dwint1 = _dwin(dq, dk, dv, dr, x2, g_norm, 1, started2[4])
    started3 = _exchange_start("exchange3_start", [dwint1], [True], [_landing(own_rows(dwint1), me)])
    grad_x, dgn = _dx(dq, dk, dv, dr, wint, x2, dh2, g_norm, started3[4])
    s_win0, = _exchange_wait("exchange2_wait", started2, [True], grad_x)
    dgn_sum, = _allreduce_small([dgn])
    s_win1, = _exchange_wait("exchange3_wait", started3, [True], dgn_sum)

    g_win, d_win, m_win, v_win = map(
        jnp.transpose, _adam_slots("w_in", [s_win0, s_win1], tr(w_in), tr(m_w_in), tr(v_w_in)))
    g_wkv, d_wkv, m_wkv, v_wkv = _adam_slots("w_mem_kv", [s_wkv], w_mem_kv[0], m_w_mem_kv[0], v_w_mem_kv[0])
    g_wout, d_wout, m_wout, v_wout = _adam_slots("w_out", [s_wout], w_out[0], m_w_out[0], v_w_out[0])

    small_shapes = [(1, D_MODEL), (4 * CHUNK, CHUNK), (4, CHUNK), (1, SGU_W), (1, D_MODEL), (1, D_MODEL)]
    pack = lambda arrs: [t.reshape(s) for t, s in zip(arrs, small_shapes)]
    g_small, loss_sum, upd = _adam_small(
        pack([g_norm, w_sgu_spatial, b_sgu_spatial, g_sgu_v, g_mem, g_final]),
        [dgn_sum, s_ws, s_bs, s_gv, s_gm, s_gf],
        pack([m_g_norm, m_w_sgu_spatial, m_b_sgu_spatial, m_g_sgu_v, m_g_mem, m_g_final]),
        pack([v_g_norm, v_w_sgu_spatial, v_b_sgu_spatial, v_g_sgu_v, v_g_mem, v_g_final]), s_loss)
    out_shapes = [g_norm.shape, w_sgu_spatial.shape, b_sgu_spatial.shape, g_sgu_v.shape, g_mem.shape, g_final.shape]
    unpack = lambda arrs: [t.reshape(s) for t, s in zip(arrs, out_shapes)]
    gs = unpack(g_small)
    ds, nms, nvs = unpack(upd[0::3]), unpack(upd[1::3]), unpack(upd[2::3])

    loss = loss_sum[0, 0]

    def assemble(small, win, wkv_, wout_):
        return [small[0], win[None], small[1], small[2], small[3], small[4], wkv_[None], wout_[None], small[5]]

    return (loss, grad_x.reshape(x.shape),
            *assemble(gs, g_win, g_wkv, g_wout), *assemble(ds, d_win, d_wkv, d_wout),
            *assemble(nms, m_win, m_wkv, m_wout), *assemble(nvs, v_win, v_wkv, v_wout))
```

```python
import jax
import jax.numpy as jnp
from jax import lax
from jax.experimental import pallas as pl
from jax.experimental.pallas import tpu as pltpu

F32 = jnp.float32
BF16 = jnp.bfloat16
SDS = jax.ShapeDtypeStruct
MESH = pl.DeviceIdType.MESH

N_DEV = 8
D_MODEL = 1024
SEQ = 2048
B_LOC = 2
T_LOC = B_LOC * SEQ
N_MEM = 256
HEAD = 64
ATTN_W = 512
SGU_W = 256
MEM_W = 256
IN_COLS = 3328
W_IN_SHARD = IN_COLS // N_DEV
CHUNK = 128
DILATIONS = ((1, 2048), (4, 512), (16, 128))
RADIUS = 64
EPS = 1e-6
NEG = -1e30
SCALE = HEAD ** -0.5
C_QA, C_KA, C_VA, C_ZA, C_UB, C_VB, C_ZB, C_QM, C_ZM = 0, 512, 1024, 1536, 2048, 2304, 2560, 2816, 3072
QKV_W = 1536
REST_W = IN_COLS - QKV_W

ADAM_LR, ADAM_B1, ADAM_B2, ADAM_EPS, ADAM_WD, ADAM_STEP = 0.001, 0.9, 0.999, 1e-08, 0.01, 10

V7X_VMEM_MIB = 64
VMEM_NO_STAGING_MIB = V7X_VMEM_MIB - 6


def _params(vmem_mib, sem=None):
    assert vmem_mib < V7X_VMEM_MIB
    return pltpu.CompilerParams(vmem_limit_bytes=vmem_mib << 20, dimension_semantics=sem)


_TOKEN = pl.BlockSpec(memory_space=pl.ANY)


def _dot(a, b):
    return jnp.dot(a.astype(BF16), b.astype(BF16), preferred_element_type=F32)


def _dot_nt(a, b):
    return lax.dot_general(a.astype(BF16), b.astype(BF16), (((1,), (1,)), ((), ())), preferred_element_type=F32)


def _dot_tn(a, b):
    return lax.dot_general(a.astype(BF16), b.astype(BF16), (((0,), (0,)), ((), ())), preferred_element_type=F32)


def _rstd(v):
    return lax.rsqrt(jnp.mean(v * v, axis=-1, keepdims=True) + EPS)


def _rms_bwd(v, r, g, dy):
    gdy = g * dy
    return r * gdy - v * (r * r * r * jnp.mean(gdy * v, axis=-1, keepdims=True))


def _sigmoid(z):
    return 1.0 / (1.0 + jnp.exp(-z))


def _silu_and_grad(z):
    s = _sigmoid(z)
    return z * s, s * (1.0 + z * (1.0 - s))


_G_C = 0.7978845608028654
_G_K = 0.044715


def _gelu_and_grad(v):
    t = jnp.tanh(_G_C * (v + _G_K * (v * v * v)))
    cdf = 0.5 * (1.0 + t)
    return v * cdf, cdf + 0.5 * v * (1.0 - t * t) * (_G_C * (1.0 + 3.0 * _G_K * v * v))


def _cast_rows(src_ref, dst_ref, rows, step=256):
    def one(i, carry):
        r = pl.ds(pl.multiple_of(i * step, step), step)
        dst_ref[r, :] = src_ref[r, :].astype(dst_ref.dtype)
        return carry
    lax.fori_loop(0, rows // step, one, 0)


def _left_lanes(rows):
    return lax.broadcasted_iota(jnp.int32, (rows, 128), 1) < HEAD


def _mesh_pos():
    return lax.axis_index("x"), lax.axis_index("y"), lax.axis_index("c")


def _peer(pos, k):
    x, y, c = pos
    return (1 - x if k & 4 else x, 1 - y if k & 2 else y, 1 - c if k & 1 else c)


def _flat(pos):
    return 4 * pos[0] + 2 * pos[1] + pos[2]


def _allgather_weights(w_in_t, w_kv, w_out):
    def body(win_ref, wkv_ref, wout_ref, wint_o, wkv_o, wout_o, send_sems, recv_sems):
        x, y, c = _mesh_pos()
        me, sib = (x, y, c), (x, y, 1 - c)
        chips = [(1 - x, y), (x, 1 - y), (1 - x, 1 - y)]

        def rows(p):
            return wint_o.at[pl.ds(pl.multiple_of(_flat(p) * W_IN_SHARD, 16), W_IN_SHARD), :]

        rows(me)[...] = win_ref[...].astype(BF16)

        def copy(k, block, to):
            return pltpu.make_async_remote_copy(
                src_ref=rows(block), dst_ref=rows(block), send_sem=send_sems.at[k], recv_sem=recv_sems.at[k],
                device_id=to, device_id_type=MESH)

        first = [copy(0, me, sib)] + [copy(1 + j, me, (*chip, c)) for j, chip in enumerate(chips)]
        for cp in first:
            cp.start()
        wkv_o[...] = wkv_ref[...].astype(BF16)
        wout_o[...] = wout_ref[...].astype(BF16)
        passed = []
        for j, chip in enumerate(chips):
            copy(1 + j, (*chip, c), me).wait_recv()
            fwd = copy(4 + j, (*chip, c), sib)
            fwd.start()
            passed.append(fwd)
        copy(0, sib, me).wait_recv()
        for j, chip in enumerate(chips):
            copy(4 + j, (*chip, 1 - c), me).wait_recv()
        for cp in first + passed:
            cp.wait_send()

    vmem = pl.BlockSpec(memory_space=pltpu.VMEM)
    return pl.pallas_call(
        body, name="allgather_weights",
        out_shape=(SDS((IN_COLS, D_MODEL), BF16), SDS(w_kv.shape, BF16), SDS(w_out.shape, BF16)),
        in_specs=[vmem, vmem, vmem], out_specs=(vmem, vmem, vmem),
        scratch_shapes=[pltpu.SemaphoreType.DMA((7,)), pltpu.SemaphoreType.DMA((7,))],
        compiler_params=_params(40),
    )(w_in_t, w_kv, w_out)


def _proj_fwd(x2, g_norm, wint, token):
    tm = 512
    sub = 256
    d = DILATIONS[2][0]
    per_ex = SEQ // tm

    def body(x_ref, g_ref, w_ref, _, o_ref, o16_ref):
        xv = x_ref[...]
        h = xv * _rstd(xv) * g_ref[...]
        res = _dot_nt(h, w_ref[...])
        o_ref[...] = res
        r_out = lax.broadcasted_iota(jnp.int32, (sub, sub), 0)
        r_in = lax.broadcasted_iota(jnp.int32, (sub, sub), 1)
        pick = (r_in == d * (r_out % (sub // d)) + r_out // (sub // d)).astype(BF16)
        for part in range(tm // sub):
            grouped = _dot(pick, res[part * sub:(part + 1) * sub, 0:QKV_W]).astype(BF16)
            for rho in range(d):
                o16_ref[0, rho, part * (sub // d):(part + 1) * (sub // d), :] = (
                    grouped[rho * (sub // d):(rho + 1) * (sub // d), :])

    return pl.pallas_call(
        body, name="proj_fwd", grid=(T_LOC // tm,),
        in_specs=[pl.BlockSpec((tm, D_MODEL), lambda i: (i, 0)), pl.BlockSpec((1, D_MODEL), lambda i: (0, 0)),
                  pl.BlockSpec((IN_COLS, D_MODEL), lambda i: (0, 0)), _TOKEN],
        out_specs=(pl.BlockSpec((tm, IN_COLS), lambda i: (i, 0)),
                   pl.BlockSpec((1, d, tm // d, QKV_W), lambda i: (i // per_ex, 0, i % per_ex, 0))),
        out_shape=(SDS((T_LOC, IN_COLS), F32), SDS((B_LOC, d, SEQ // d, QKV_W), BF16)),
        compiler_params=_params(48, ("arbitrary",)),
    )(x2, g_norm, wint, token)


def _memkv_fwd(mem2, g_mem, wkv):
    def body(m_ref, g_ref, w_ref, o_ref):
        mv = m_ref[...]
        o_ref[...] = _dot(mv * _rstd(mv) * g_ref[...], w_ref[...])

    return pl.pallas_call(
        body, name="memkv_fwd", out_shape=SDS((B_LOC * N_MEM, 2 * MEM_W), F32), compiler_params=_params(32),
    )(mem2, g_mem, wkv)


N_BIAS = 7


def _fill_bias_tables(sl_ref, tab):
    for cfg, (d, length) in enumerate(DILATIONS):
        nk = min(length, 2 * CHUNK)
        r = lax.broadcasted_iota(jnp.int32, (CHUNK, nk), 0)
        c = lax.broadcasted_iota(jnp.int32, (CHUNK, nk), 1)
        for var in range(3 if length > nk else 1):
            rel = jnp.abs(r - c + var * RADIUS)
            dist = rel.astype(F32) * float(d)
            for h in range(2):
                slope = sl_ref[0, 0:1, h * HEAD:h * HEAD + 1]
                tab[3 * cfg + var, h * CHUNK:(h + 1) * CHUNK, 0:nk] = jnp.where(rel <= RADIUS, -slope * dist, NEG)


def _attn_blocks(visit, unroll):
    def step(t, carry):
        for cfg, (d, length) in enumerate(DILATIONS):
            nblk = length // CHUNK
            if nblk == 1:
                visit(cfg, 0, t, t, length, t)
                continue
            rho, i = (0, t) if d == 1 else (t // nblk, t % nblk)
            ks = jnp.clip(i * CHUNK - RADIUS, 0, length - 2 * CHUNK)
            visit(cfg, (i * CHUNK - ks) // RADIUS, rho + d * (i * CHUNK), rho + d * ks, 2 * CHUNK, t)
        return carry
    lax.fori_loop(0, 16, step, 0, unroll=unroll)


def _stack_heads(v, left):
    return jnp.concatenate([jnp.where(left, v, 0.0), jnp.where(left, 0.0, v)], axis=0)


def _unstack_heads(v, left):
    return jnp.where(left, v[0:CHUNK], v[CHUNK:2 * CHUNK])


def _rows(start, n, d):
    return pl.ds(start, n) if d == 1 else pl.ds(start, n, stride=d)


def _blk16(col0):
    d, length = DILATIONS[2]
    return pl.BlockSpec((1, d, length, 128), lambda b, hp: (b, 0, 0, col0 // 128 + hp))


def _attn_fwd(proj, qkv16, slopes):
    def body(sl_ref, q_ref, k_ref, v_ref, q16_ref, k16_ref, v16_ref, a_ref, lse_ref, *scr):
        o_c, m_c, l_c, tab = scr[0:3], scr[3:6], scr[6:9], scr[9]
        left = _left_lanes(CHUNK)
        _fill_bias_tables(sl_ref, tab)

        def block(cfg, var, q0, k0, nk, t):
            d = DILATIONS[cfg][0]
            rq, rk = _rows(q0, CHUNK, d), _rows(k0, nk, d)
            if cfg == 2:
                qb, kw, vw = q16_ref[0, t].astype(F32), k16_ref[0, t], v16_ref[0, t]
            else:
                qb, kw, vw = q_ref[rq, :], k_ref[rk, :], v_ref[rk, :]
            qs = _stack_heads(qb * SCALE, left)
            s = _dot_nt(qs, kw) + tab[3 * cfg + var, :, 0:nk]
            m = jnp.max(s, axis=-1, keepdims=True)
            p = jnp.exp(s - m)
            o_c[cfg][rq, :] = _unstack_heads(_dot(p, vw), left)
            m_c[cfg][rq, :] = _unstack_heads(m, left)
            l_c[cfg][rq, :] = _unstack_heads(jnp.sum(p, axis=-1, keepdims=True), left)
        _attn_blocks(block, 8)

        def merge(j, carry):
            rows = pl.ds(pl.multiple_of(j * 256, 256), 256)
            ms = [m_c[i][rows, :] for i in range(3)]
            top = jnp.maximum(jnp.maximum(ms[0], ms[1]), ms[2])
            ws = [jnp.exp(m - top) for m in ms]
            den = l_c[0][rows, :] * ws[0] + l_c[1][rows, :] * ws[1] + l_c[2][rows, :] * ws[2]
            num = o_c[0][rows, :] * ws[0] + o_c[1][rows, :] * ws[1] + o_c[2][rows, :] * ws[2]
            a_ref[rows, :] = num / den
            lse_ref[rows, :] = top + jnp.log(den)
            return carry
        lax.fori_loop(0, SEQ // 256, merge, 0)

    blk = lambda col0: pl.BlockSpec((SEQ, 128), lambda b, hp: (b, col0 // 128 + hp))
    out = pl.BlockSpec((SEQ, 128), lambda b, hp: (b, hp))
    return pl.pallas_call(
        body, name="attn_fwd", grid=(B_LOC, 4),
        in_specs=[pl.BlockSpec((1, 8, 128), lambda b, hp: (hp, 0, 0)), blk(C_QA), blk(C_KA), blk(C_VA),
                  _blk16(C_QA), _blk16(C_KA), _blk16(C_VA)],
        out_specs=(out, out),
        out_shape=(SDS((T_LOC, ATTN_W), F32), SDS((T_LOC, ATTN_W), F32)),
        scratch_shapes=[pltpu.VMEM((SEQ, 128), F32)] * 9 + [pltpu.VMEM((N_BIAS, 2 * CHUNK, 2 * CHUNK), F32)],
        compiler_params=_params(VMEM_NO_STAGING_MIB, ("arbitrary", "arbitrary")),
    )(slopes, proj, proj, proj, qkv16, qkv16, qkv16)


def _chunks_side_by_side(v, pr, tm):
    return jnp.concatenate([v[ch * CHUNK:(ch + 1) * CHUNK, pr * 128:(pr + 1) * 128] for ch in range(tm // CHUNK)], axis=1)


def _first_group_lanes(tm):
    return lax.broadcasted_iota(jnp.int32, (CHUNK, tm), 1) % 128 < HEAD


def _store_chunks(dst_ref, pr, val, tm):
    for ch in range(tm // CHUNK):
        dst_ref[ch * CHUNK:(ch + 1) * CHUNK, pr * 128:(pr + 1) * 128] = val[:, ch * CHUNK:(ch + 1) * CHUNK]


def _sgu_mix(vn, ws_ref, dst_ref, tm):
    first = _first_group_lanes(tm)
    for pr in range(2):
        vp = _chunks_side_by_side(vn, pr, tm)
        _store_chunks(dst_ref, pr, jnp.where(first, _dot(ws_ref[2 * pr], vp), _dot(ws_ref[2 * pr + 1], vp)), tm)


def _mem_head_of_lane(rows):
    return lax.broadcasted_iota(jnp.int32, (rows, MEM_W), 1) // HEAD


def _stack_mem_heads(v, rows):
    head = _mem_head_of_lane(rows)
    return jnp.concatenate([jnp.where(head == h, v, 0.0) for h in range(4)], axis=0)


def _unstack_mem_heads(v, rows):
    head = _mem_head_of_lane(rows)
    out = v[0:rows]
    for h in range(1, 4):
        out = jnp.where(head == h, v[h * rows:(h + 1) * rows], out)
    return out


def _mem_attn_probs(q, kmem, rows):
    qs = _stack_mem_heads(q, rows).astype(BF16)
    s = _dot_nt(qs, kmem) * SCALE
    e = jnp.exp(s - jnp.max(s, axis=-1, keepdims=True))
    return e * (1.0 / jnp.sum(e, axis=-1, keepdims=True)), qs


def _branch_blocks(tm):
    col = lambda w, c0: pl.BlockSpec((tm, w), lambda i: (i, c0 // w))
    return [col(512, C_ZA), col(256, C_UB), col(256, C_VB), col(256, C_ZB), col(256, C_QM), col(256, C_ZM)]


def _branch_fwd(proj, a, kv, w_s, b_exp, g_v):
    tm = 512
    per_ex = SEQ // tm

    def body(za_ref, ub_ref, vb_ref, zb_ref, qm_ref, zm_ref, a_ref, kv_ref, ws_ref, be_ref, gv_ref, o_ref, mix):
        o_ref[:, 0:ATTN_W] = (_silu_and_grad(za_ref[...])[0] * a_ref[...]).astype(BF16)
        gu = _gelu_and_grad(ub_ref[...])[0]
        gv = _gelu_and_grad(vb_ref[...])[0]
        vn = gv * _rstd(gv) * gv_ref[...]
        _sgu_mix(vn.astype(BF16), ws_ref, mix, tm)
        sg = gu * (mix[...] + jnp.concatenate([be_ref[...]] * (tm // CHUNK), axis=0))
        o_ref[:, ATTN_W:ATTN_W + SGU_W] = (_silu_and_grad(zb_ref[...])[0] * sg).astype(BF16)
        p = _mem_attn_probs(qm_ref[...], kv_ref[:, 0:MEM_W], tm)[0]
        mo = _unstack_mem_heads(_dot(p, kv_ref[:, MEM_W:2 * MEM_W]), tm)
        o_ref[:, ATTN_W + SGU_W:D_MODEL] = (_silu_and_grad(zm_ref[...])[0] * mo).astype(BF16)

    full = lambda shape: pl.BlockSpec(shape, lambda i: (0,) * len(shape))
    return pl.pallas_call(
        body, name="branch_fwd", grid=(T_LOC // tm,),
        in_specs=_branch_blocks(tm) + [
            pl.BlockSpec((tm, ATTN_W), lambda i: (i, 0)), pl.BlockSpec((N_MEM, 2 * MEM_W), lambda i: (i // per_ex, 0)),
            full((4, CHUNK, CHUNK)), full((CHUNK, SGU_W)), full((1, SGU_W))],
        out_specs=pl.BlockSpec((tm, D_MODEL), lambda i: (i, 0)),
        out_shape=SDS((T_LOC, D_MODEL), BF16),
        scratch_shapes=[pltpu.VMEM((tm, SGU_W), F32)],
        compiler_params=_params(VMEM_NO_STAGING_MIB, ("arbitrary",)),
    )(proj, proj, proj, proj, proj, proj, a, kv, w_s, b_exp, g_v)


def _outproj_loss(gated, wout, x2, tgt2, g_final):
    tm = 512

    def body(g_ref, w_ref, x_ref, t_ref, gf_ref, dh2_ref, loss_ref, dgf_ref, dwo_ref, dwo_acc):
        @pl.when(pl.program_id(0) == 0)
        def _():
            loss_ref[...] = jnp.zeros_like(loss_ref)
            dgf_ref[...] = jnp.zeros_like(dgf_ref)
            dwo_acc[...] = jnp.zeros_like(dwo_acc)
        gated = g_ref[...]
        h2 = x_ref[...] + _dot(gated, w_ref[...])
        r = _rstd(h2)
        gf = gf_ref[...]
        err = h2 * r * gf - t_ref[...]
        loss_ref[...] += 0.5 * jnp.sum(jnp.mean(err * err, axis=-1, keepdims=True))
        dy = err * (1.0 / D_MODEL)
        dh2 = _rms_bwd(h2, r, gf, dy)
        dh2_ref[...] = dh2
        dgf_ref[...] += jnp.sum(dy * (h2 * r), axis=0, keepdims=True)
        dwo_acc[...] += _dot_tn(gated, dh2)

        @pl.when(pl.program_id(0) == pl.num_programs(0) - 1)
        def _():
            _cast_rows(dwo_acc, dwo_ref, D_MODEL)

    row = pl.BlockSpec((tm, D_MODEL), lambda i: (i, 0))
    vec = pl.BlockSpec((1, D_MODEL), lambda i: (0, 0))
    square = pl.BlockSpec((D_MODEL, D_MODEL), lambda i: (0, 0))
    return pl.pallas_call(
        body, name="outproj_loss", grid=(T_LOC // tm,),
        in_specs=[row, square, row, row, vec],
        out_specs=(row, pl.BlockSpec((8, 128), lambda i: (0, 0)), vec, square),
        out_shape=(SDS((T_LOC, D_MODEL), F32), SDS((8, 128), F32), SDS((1, D_MODEL), F32), SDS((D_MODEL, D_MODEL), BF16)),
        scratch_shapes=[pltpu.VMEM((D_MODEL, D_MODEL), F32)],
        compiler_params=_params(VMEM_NO_STAGING_MIB, ("arbitrary",)),
    )(gated, wout, x2, tgt2, g_final)


def _branch_bwd(dh2, wout, proj, a, kv, w_s, b_exp, g_v):
    tm = 512
    per_ex = SEQ // tm

    def body(dh2_ref, w_ref, za_ref, ub_ref, vb_ref, zb_ref, qm_ref, zm_ref, a_ref, kv_ref, ws_ref,
             be_ref, gv_ref, da_ref, dr_ref, dkv_ref, dws_ref, db_ref, dgv_ref, mix, dvn, dmsum):
        i = pl.program_id(0)

        @pl.when(i == 0)
        def _():
            dws_ref[...] = jnp.zeros_like(dws_ref)
            dgv_ref[...] = jnp.zeros_like(dgv_ref)
            dmsum[...] = jnp.zeros_like(dmsum)

        @pl.when(i % per_ex == 0)
        def _():
            dkv_ref[...] = jnp.zeros_like(dkv_ref)

        dg = _dot_nt(dh2_ref[...], w_ref[...])

        sa, dsa = _silu_and_grad(za_ref[...])
        dga = dg[:, 0:ATTN_W]
        da_ref[...] = dga * sa
        dr_ref[:, 0:512] = (dga * a_ref[...] * dsa).astype(BF16)

        ub, vb = ub_ref[...], vb_ref[...]
        gu, dgu = _gelu_and_grad(ub)
        gv, dgv = _gelu_and_grad(vb)
        rv = _rstd(gv)
        gain = gv_ref[...]
        vn = (gv * rv * gain).astype(BF16)
        _sgu_mix(vn, ws_ref, mix, tm)
        mixed = mix[...] + jnp.concatenate([be_ref[...]] * (tm // CHUNK), axis=0)
        sb, dsb = _silu_and_grad(zb_ref[...])
        dgb = dg[:, ATTN_W:ATTN_W + SGU_W]
        dsg = dgb * sb
        dr_ref[:, 512:768] = (dsg * mixed * dgu).astype(BF16)
        dr_ref[:, 1024:1280] = (dgb * (gu * mixed) * dsb).astype(BF16)
        dmix = dsg * gu
        for ch in range(tm // CHUNK):
            dmsum[...] += dmix[ch * CHUNK:(ch + 1) * CHUNK, :]
        first = _first_group_lanes(tm)
        for pr in range(2):
            dmp, vp = _chunks_side_by_side(dmix, pr, tm), _chunks_side_by_side(vn, pr, tm)
            dws_ref[2 * pr] += _dot_nt(jnp.where(first, dmp, 0.0), vp)
            dws_ref[2 * pr + 1] += _dot_nt(jnp.where(first, 0.0, dmp), vp)
            _store_chunks(dvn, pr, jnp.where(first, _dot_tn(ws_ref[2 * pr], dmp), _dot_tn(ws_ref[2 * pr + 1], dmp)), tm)
        dvn_v = dvn[...]
        dgv_ref[...] += jnp.sum(dvn_v * (gv * rv), axis=0, keepdims=True)
        dr_ref[:, 768:1024] = (_rms_bwd(gv, rv, gain, dvn_v) * dgv).astype(BF16)

        szm, dszm = _silu_and_grad(zm_ref[...])
        dgm = dg[:, ATTN_W + SGU_W:D_MODEL]
        kmem, vmem_ = kv_ref[:, 0:MEM_W].astype(BF16), kv_ref[:, MEM_W:2 * MEM_W].astype(BF16)
        p, qs = _mem_attn_probs(qm_ref[...], kmem, tm)
        dmos = _stack_mem_heads(dgm * szm, tm).astype(BF16)
        dr_ref[:, 1536:1792] = (dgm * _unstack_mem_heads(_dot(p, vmem_), tm) * dszm).astype(BF16)
        dp = _dot_nt(dmos, vmem_)
        ds = (p * (dp - jnp.sum(p * dp, axis=-1, keepdims=True)) * SCALE).astype(BF16)
        dr_ref[:, 1280:1536] = _unstack_mem_heads(_dot(ds, kmem), tm).astype(BF16)
        dkv_ref[:, 0:MEM_W] += _dot_tn(ds, qs)
        dkv_ref[:, MEM_W:2 * MEM_W] += _dot_tn(p, dmos)

        @pl.when(i == pl.num_programs(0) - 1)
        def _():
            tot = dmsum[...]
            hi = tot.astype(BF16)
            lo = (tot - hi.astype(F32)).astype(BF16)
            grp = (lax.broadcasted_iota(jnp.int32, (SGU_W, 128), 0) // HEAD
                   == lax.broadcasted_iota(jnp.int32, (SGU_W, 128), 1)).astype(BF16)
            db_ref[...] = (_dot(hi, grp) + _dot(lo, grp)).T[0:4, :]

    full = lambda shape: pl.BlockSpec(shape, lambda i: (0,) * len(shape))
    row = lambda w: pl.BlockSpec((tm, w), lambda i: (i, 0))
    return pl.pallas_call(
        body, name="branch_bwd", grid=(T_LOC // tm,),
        in_specs=[row(D_MODEL), full((D_MODEL, D_MODEL))] + _branch_blocks(tm) + [
            row(ATTN_W), pl.BlockSpec((N_MEM, 2 * MEM_W), lambda i: (i // per_ex, 0)),
            full((4, CHUNK, CHUNK)), full((CHUNK, SGU_W)), full((1, SGU_W))],
        out_specs=(row(ATTN_W), row(REST_W), pl.BlockSpec((N_MEM, 2 * MEM_W), lambda i: (i // per_ex, 0)),
                   full((4, CHUNK, CHUNK)), full((4, CHUNK)), full((1, SGU_W))),
        out_shape=(SDS((T_LOC, ATTN_W), F32), SDS((T_LOC, REST_W), BF16), SDS((B_LOC * N_MEM, 2 * MEM_W), F32),
                   SDS((4, CHUNK, CHUNK), F32), SDS((4, CHUNK), F32), SDS((1, SGU_W), F32)),
        scratch_shapes=[pltpu.VMEM((tm, SGU_W), F32), pltpu.VMEM((tm, SGU_W), F32), pltpu.VMEM((CHUNK, SGU_W), F32)],
        compiler_params=_params(56, ("arbitrary",)),
    )(dh2, wout, proj, proj, proj, proj, proj, proj, a, kv, w_s, b_exp, g_v)


def _attn_bwd(proj, qkv16, slopes, da, a, lse, token):
    def body(sl_ref, q_ref, k_ref, v_ref, q16_ref, k16_ref, v16_ref, da_ref, a_ref, lse_ref, _,
             dq_ref, dk_ref, dv_ref, *scr):
        dq_s, dk_s, dv_s, tab = scr[0:3], scr[3:6], scr[6:9], scr[9]
        lse_h, delta_h = scr[10:12], scr[12:14]
        p_all, ds_all = scr[14], scr[15]
        left = _left_lanes(CHUNK)
        _fill_bias_tables(sl_ref, tab)

        def prep(j, carry):
            rows = pl.ds(pl.multiple_of(j * 256, 256), 256)
            l256 = _left_lanes(256)
            prod = da_ref[rows, :] * a_ref[rows, :]
            delta_h[0][rows, :] = jnp.broadcast_to(jnp.sum(jnp.where(l256, prod, 0.0), axis=-1, keepdims=True), (256, 128))
            delta_h[1][rows, :] = jnp.broadcast_to(jnp.sum(jnp.where(l256, 0.0, prod), axis=-1, keepdims=True), (256, 128))
            pair = lse_ref[rows, :]
            other = pltpu.roll(pair, HEAD, axis=1)
            lse_h[0][rows, :] = jnp.where(l256, pair, other)
            lse_h[1][rows, :] = jnp.where(l256, other, pair)
            zero = jnp.zeros((256, 128), F32)
            for cfg in range(3):
                dk_s[cfg][rows, :] = zero
                dv_s[cfg][rows, :] = zero
            return carry
        lax.fori_loop(0, SEQ // 256, prep, 0)

        def per_row(halves, rq, nk):
            v = jnp.concatenate([halves[0][rq, :], halves[1][rq, :]], axis=0)
            return v if nk == 128 else jnp.concatenate([v, v], axis=1)

        def qkv(cfg, rq, rk, t):
            if cfg == 2:
                return q16_ref[0, t].astype(F32), k16_ref[0, t], v16_ref[0, t]
            return q_ref[rq, :], k_ref[rk, :], v_ref[rk, :]

        def probs(cfg, var, q0, k0, nk, t):
            d = DILATIONS[cfg][0]
            rq, rk = _rows(q0, CHUNK, d), _rows(k0, nk, d)
            qb, kw, vw = qkv(cfg, rq, rk, t)
            qs = _stack_heads(qb * SCALE, left)
            das = _stack_heads(da_ref[rq, :], left)
            s = _dot_nt(qs, kw) + tab[3 * cfg + var, :, 0:nk]
            p = jnp.exp(s - per_row(lse_h, rq, nk))
            p_all[16 * cfg + t, :, 0:nk] = p.astype(BF16)
            ds_all[16 * cfg + t, :, 0:nk] = (p * (_dot_nt(das, vw) - per_row(delta_h, rq, nk))).astype(BF16)
        _attn_blocks(probs, 4)

        def grads(cfg, var, q0, k0, nk, t):
            d = DILATIONS[cfg][0]
            rq, rk = _rows(q0, CHUNK, d), _rows(k0, nk, d)
            qb, kw, _ = qkv(cfg, rq, rk, t)
            qs = _stack_heads(qb * SCALE, left).astype(BF16)
            das = _stack_heads(da_ref[rq, :], left).astype(BF16)
            p, ds = p_all[16 * cfg + t, :, 0:nk], ds_all[16 * cfg + t, :, 0:nk]
            dq_s[cfg][rq, :] = _unstack_heads(_dot(ds, kw), left) * SCALE
            dk_s[cfg][rk, :] += _dot_tn(ds, qs)
            dv_s[cfg][rk, :] += _dot_tn(p, das)
        _attn_blocks(grads, 4)

        def flush(j, carry):
            rows = pl.ds(pl.multiple_of(j * 256, 256), 256)
            for acc, dst in ((dq_s, dq_ref), (dk_s, dk_ref), (dv_s, dv_ref)):
                dst[rows, :] = (acc[0][rows, :] + acc[1][rows, :] + acc[2][rows, :]).astype(BF16)
            return carry
        lax.fori_loop(0, SEQ // 256, flush, 0)

    blk = lambda col0: pl.BlockSpec((SEQ, 128), lambda b, hp: (b, col0 // 128 + hp))
    own = pl.BlockSpec((SEQ, 128), lambda b, hp: (b, hp))
    return pl.pallas_call(
        body, name="attn_bwd", grid=(B_LOC, 4),
        in_specs=[pl.BlockSpec((1, 8, 128), lambda b, hp: (hp, 0, 0)), blk(C_QA), blk(C_KA), blk(C_VA),
                  _blk16(C_QA), _blk16(C_KA), _blk16(C_VA), own, own, own, _TOKEN],
        out_specs=(own, own, own),
        out_shape=(SDS((T_LOC, ATTN_W), BF16),) * 3,
        scratch_shapes=[pltpu.VMEM((SEQ, 128), F32)] * 9 + [pltpu.VMEM((N_BIAS, 2 * CHUNK, 2 * CHUNK), F32)]
        + [pltpu.VMEM((SEQ, 128), F32)] * 4 + [pltpu.VMEM((48, 2 * CHUNK, 2 * CHUNK), BF16)] * 2,
        compiler_params=_params(52, ("arbitrary", "arbitrary")),
    )(slopes, proj, proj, proj, qkv16, qkv16, qkv16, da, a, lse, token)


def _dproj_specs(tm):
    third = pl.BlockSpec((tm, ATTN_W), lambda i: (i, 0))
    return [third, third, third, pl.BlockSpec((tm, REST_W), lambda i: (i, 0))]


def _dx(dq, dk, dv, dr, wint, x2, dh2, g_norm, token):
    tm = 512

    def body(dq_ref, dk_ref, dv_ref, dr_ref, w_ref, x_ref, dh2_ref, g_ref, _, gx_ref, dgn_ref):
        @pl.when(pl.program_id(0) == 0)
        def _():
            dgn_ref[...] = jnp.zeros_like(dgn_ref)
        dh = (_dot(dq_ref[...], w_ref[C_QA:C_KA, :]) + _dot(dk_ref[...], w_ref[C_KA:C_VA, :])
              + _dot(dv_ref[...], w_ref[C_VA:C_ZA, :]) + _dot(dr_ref[...], w_ref[C_ZA:IN_COLS, :]))
        xv = x_ref[...]
        r = _rstd(xv)
        gx_ref[...] = dh2_ref[...] + _rms_bwd(xv, r, g_ref[...], dh)
        dgn_ref[...] += jnp.sum(dh * (xv * r), axis=0, keepdims=True)

    row = pl.BlockSpec((tm, D_MODEL), lambda i: (i, 0))
    vec = pl.BlockSpec((1, D_MODEL), lambda i: (0, 0))
    return pl.pallas_call(
        body, name="dx", grid=(T_LOC // tm,),
        in_specs=_dproj_specs(tm) + [pl.BlockSpec((IN_COLS, D_MODEL), lambda i: (0, 0)), row, row, vec, _TOKEN],
        out_specs=(row, vec),
        out_shape=(SDS((T_LOC, D_MODEL), F32), SDS((1, D_MODEL), F32)),
        compiler_params=_params(48, ("arbitrary",)),
    )(dq, dk, dv, dr, wint, x2, dh2, g_norm, token)


def _dwin(dq, dk, dv, dr, x2, g_norm, half, token):
    tm = 1024
    width = D_MODEL // 2
    cols = slice(half * width, (half + 1) * width)

    def body(dq_ref, dk_ref, dv_ref, dr_ref, x_ref, g_ref, _, o_ref, acc):
        @pl.when(pl.program_id(0) == 0)
        def _():
            acc[...] = jnp.zeros_like(acc)
        xv = x_ref[...]
        h = (xv[:, cols] * _rstd(xv) * g_ref[:, cols]).astype(BF16)
        acc[C_QA:C_KA, :] += _dot_tn(dq_ref[...], h)
        acc[C_KA:C_VA, :] += _dot_tn(dk_ref[...], h)
        acc[C_VA:C_ZA, :] += _dot_tn(dv_ref[...], h)
        acc[C_ZA:IN_COLS, :] += _dot_tn(dr_ref[...], h)

        @pl.when(pl.program_id(0) == pl.num_programs(0) - 1)
        def _():
            _cast_rows(acc, o_ref, IN_COLS)

    return pl.pallas_call(
        body, name="dwin%d" % half, grid=(T_LOC // tm,),
        in_specs=_dproj_specs(tm) + [pl.BlockSpec((tm, D_MODEL), lambda i: (i, 0)),
                                     pl.BlockSpec((1, D_MODEL), lambda i: (0, 0)), _TOKEN],
        out_specs=pl.BlockSpec((IN_COLS, width), lambda i: (0, 0)),
        out_shape=SDS((IN_COLS, width), BF16),
        scratch_shapes=[pltpu.VMEM((IN_COLS, width), F32)],
        compiler_params=_params(48, ("arbitrary",)),
    )(dq, dk, dv, dr, x2, g_norm, token)


def _memkv_bwd(dkv, mem2, g_mem, wkv):
    def body(dkv_ref, m_ref, g_ref, w_ref, dw_ref, dg_ref):
        mv = m_ref[...]
        r = _rstd(mv)
        dkv_v = dkv_ref[...].astype(BF16)
        dw_ref[...] = _dot_tn(mv * r * g_ref[...], dkv_v).astype(BF16)
        dg_ref[...] = jnp.sum(_dot_nt(dkv_v, w_ref[...]) * (mv * r), axis=0, keepdims=True)

    return pl.pallas_call(
        body, name="memkv_bwd", out_shape=(SDS((D_MODEL, 2 * MEM_W), BF16), SDS((1, D_MODEL), F32)),
        compiler_params=_params(32),
    )(dkv, mem2, g_mem, wkv)


def _allreduce_small(parts):
    n = len(parts)

    def body(*refs):
        ins, outs, bufs = refs[0:n], refs[n:2 * n], refs[2 * n:3 * n]
        send_sems, recv_sems = refs[3 * n], refs[3 * n + 1]
        pos = _mesh_pos()
        me = _flat(pos)
        for a in range(n):
            bufs[a][me] = ins[a][...]

        def copy(a, k, slot):
            return pltpu.make_async_remote_copy(
                src_ref=ins[a], dst_ref=bufs[a].at[slot],
                send_sem=send_sems.at[7 * a + k - 1], recv_sem=recv_sems.at[7 * a + k - 1],
                device_id=_peer(pos, k), device_id_type=MESH)

        sent = [copy(a, k, me) for a in range(n) for k in range(1, N_DEV)]
        for cp in sent:
            cp.start()
        for a in range(n):
            for k in range(1, N_DEV):
                copy(a, k, _flat(_peer(pos, k))).wait_recv()
        for cp in sent:
            cp.wait_send()
        for a in range(n):
            acc = bufs[a][0]
            for s in range(1, N_DEV):
                acc = acc + bufs[a][s]
            outs[a][...] = acc

    vmem = pl.BlockSpec(memory_space=pltpu.VMEM)
    return pl.pallas_call(
        body, name="allreduce_small",
        out_shape=tuple(SDS(p.shape, F32) for p in parts),
        in_specs=[vmem] * n, out_specs=(vmem,) * n,
        scratch_shapes=[pltpu.VMEM((N_DEV,) + p.shape, F32) for p in parts]
        + [pltpu.SemaphoreType.DMA((7 * n,)), pltpu.SemaphoreType.DMA((7 * n,))],
        compiler_params=_params(16),
    )(*parts)


_HBM = pl.BlockSpec(memory_space=pltpu.HBM)
_SEM = pl.BlockSpec(memory_space=pltpu.SEMAPHORE)
_SIDE_EFFECT = pltpu.SideEffectType.DATAFLOW_SIDE_EFFECTING


def _exchange_copies(src_refs, land_refs, scatter, send_sems, recv_sems):
    pos = _mesh_pos()
    copies = []
    for a, (src, land) in enumerate(zip(src_refs, land_refs)):
        n = land.shape[1]
        for k in range(1, N_DEV):
            peer = _peer(pos, k)
            piece = src.at[pl.ds(pl.multiple_of(_flat(peer) * n, 16), n), :] if scatter[a] else src
            copies.append(pltpu.make_async_remote_copy(
                src_ref=piece, dst_ref=land.at[_flat(pos)],
                send_sem=send_sems.at[7 * a + k - 1], recv_sem=recv_sems.at[7 * a + k - 1],
                device_id=peer, device_id_type=MESH))
    return copies


def _exchange_start(name, srcs, scatter):
    n = len(srcs)

    def body(*refs):
        src_refs, land_refs, own_sems = refs[0:n], refs[n:2 * n], refs[-1]
        me = _flat(_mesh_pos())
        own = []
        for a, (src, land) in enumerate(zip(src_refs, land_refs)):
            rows = land.shape[1]
            piece = src.at[pl.ds(pl.multiple_of(me * rows, 16), rows), :] if scatter[a] else src
            own.append(pltpu.make_async_copy(piece, land.at[me], own_sems.at[a]))
            own[-1].start()
        for cp in _exchange_copies(src_refs, land_refs, scatter, refs[2 * n], refs[2 * n + 1]):
            cp.start()
        refs[4 * n + 2][...] = jnp.zeros_like(refs[4 * n + 2])
        for cp in own:
            cp.wait()

    lands = [lax.empty((N_DEV,) + (t.shape[0] // N_DEV if sc else t.shape[0],) + t.shape[1:], t.dtype)
             for t, sc in zip(srcs, scatter)]
    ops = [pltpu.with_memory_space_constraint(t, pltpu.HBM) for t in (*srcs, *lands)]
    out = pl.pallas_call(
        body, name=name,
        out_shape=(pltpu.SemaphoreType.DMA((7 * n,)), pltpu.SemaphoreType.DMA((7 * n,)),
                   *[pltpu.HBM(t.shape, t.dtype) for t in ops], SDS((8, 128), F32)),
        in_specs=[_HBM] * (2 * n),
        out_specs=(_SEM, _SEM, *[_HBM] * (2 * n), pl.BlockSpec(memory_space=pltpu.VMEM)),
        input_output_aliases={i: 2 + i for i in range(2 * n)},
        scratch_shapes=[pltpu.SemaphoreType.DMA((n,))],
        compiler_params=pltpu.CompilerParams(has_side_effects=_SIDE_EFFECT),
    )(*ops)
    return out[0], out[1], out[2:2 + n], out[2 + n:2 + 2 * n], out[-1]


def _exchange_wait(name, started, scatter, after):
    send_sems, recv_sems, srcs, lands, _ = started
    n = len(srcs)

    def body(*refs):
        for cp in _exchange_copies(refs[0:n], refs[n:2 * n], scatter, refs[2 * n], refs[2 * n + 1]):
            cp.wait_send()
            cp.wait_recv()

    out = pl.pallas_call(
        body, name=name,
        out_shape=tuple(pltpu.HBM(t.shape, t.dtype) for t in (*srcs, *lands)),
        in_specs=[_HBM] * (2 * n) + [_SEM, _SEM, pl.BlockSpec(memory_space=pl.ANY)],
        out_specs=(_HBM,) * (2 * n),
        input_output_aliases={i: i for i in range(2 * n)},
        compiler_params=pltpu.CompilerParams(has_side_effects=_SIDE_EFFECT),
    )(*srcs, *lands, send_sems, recv_sems, after)
    return out[n:]


def _adamw(w, g, m, v):
    m = ADAM_B1 * m + (1.0 - ADAM_B1) * g
    v = ADAM_B2 * v + (1.0 - ADAM_B2) * (g * g)
    m_hat = m / (1.0 - ADAM_B1 ** ADAM_STEP)
    v_hat = v / (1.0 - ADAM_B2 ** ADAM_STEP)
    return -ADAM_LR * (m_hat / (jnp.sqrt(v_hat) + ADAM_EPS) + ADAM_WD * w), m, v


def _adam_slots(name, pieces, w, m, v):
    rows, cols = w.shape
    starts = [sum(p.shape[2] for p in pieces[:i]) for i in range(len(pieces) + 1)]
    assert starts[-1] == cols and all(p.shape[1] == rows for p in pieces)

    def body(*refs):
        s_refs, (w_ref, m_ref, v_ref, g_o, d_o, m_o, v_o, acc) = refs[:len(pieces)], refs[len(pieces):]
        s = pl.program_id(0)

        @pl.when(s == 0)
        def _():
            for i, s_ref in enumerate(s_refs):
                acc[:, starts[i]:starts[i + 1]] = s_ref[0].astype(F32)

        @pl.when(s > 0)
        def _():
            for i, s_ref in enumerate(s_refs):
                acc[:, starts[i]:starts[i + 1]] += s_ref[0].astype(F32)

        @pl.when(s == N_DEV - 1)
        def _():
            g = acc[...]
            g_o[...] = g
            d_o[...], m_o[...], v_o[...] = _adamw(w_ref[...], g, m_ref[...], v_ref[...])

    full = pl.BlockSpec((rows, cols), lambda s: (0, 0))
    return pl.pallas_call(
        body, name="adam_" + name, grid=(N_DEV,),
        in_specs=[pl.BlockSpec((1, rows, p.shape[2]), lambda s: (s, 0, 0)) for p in pieces] + [full, full, full],
        out_specs=(full,) * 4, out_shape=(SDS((rows, cols), F32),) * 4,
        scratch_shapes=[pltpu.VMEM((rows, cols), F32)],
        compiler_params=_params(40, ("arbitrary",)),
    )(*pieces, w, m, v)


def _adam_small(ws, gs, ms, vs, loss_slots):
    n = len(ws)

    def total(ref, like):
        if len(ref.shape) == len(like.shape):
            return ref[...]
        acc = ref[0]
        for s in range(1, N_DEV):
            acc = acc + ref[s]
        return acc

    def body(*refs):
        w_r, g_r, m_r, v_r = refs[0:n], refs[n:2 * n], refs[2 * n:3 * n], refs[3 * n:4 * n]
        loss_r, outs = refs[4 * n], refs[4 * n + 1:]
        for a in range(n):
            g = total(g_r[a], w_r[a])
            outs[a][...] = g
            outs[n + 1 + 3 * a][...], outs[n + 2 + 3 * a][...], outs[n + 3 + 3 * a][...] = _adamw(
                w_r[a][...], g, m_r[a][...], v_r[a][...])
        outs[n][...] = total(loss_r, outs[n])

    out = pl.pallas_call(
        body, name="adam_small",
        out_shape=tuple(SDS(w.shape, F32) for w in ws) + (SDS(loss_slots.shape[1:], F32),)
        + tuple(SDS(w.shape, F32) for w in ws for _ in range(3)),
        compiler_params=_params(16),
    )(*ws, *gs, *ms, *vs, loss_slots)
    return out[0:n], out[n], out[n + 1:]


def kernel(x, mem, g_norm, w_in, w_sgu_spatial, b_sgu_spatial, g_sgu_v, g_mem, w_mem_kv, w_out, g_final, loss_target, m_g_norm, m_w_in, m_w_sgu_spatial, m_b_sgu_spatial, m_g_sgu_v, m_g_mem, m_w_mem_kv, m_w_out, m_g_final, v_g_norm, v_w_in, v_w_sgu_spatial, v_b_sgu_spatial, v_g_sgu_v, v_g_mem, v_w_mem_kv, v_w_out, v_g_final):
    x2 = x.reshape(T_LOC, D_MODEL)
    tgt2 = loss_target.reshape(T_LOC, D_MODEL)
    mem2 = mem.reshape(B_LOC * N_MEM, D_MODEL)
    w_s = w_sgu_spatial[0]
    b_exp = jnp.repeat(b_sgu_spatial[0].T, HEAD, axis=1)
    slope = jnp.power(2.0, -8.0 * (jnp.arange(8, dtype=F32) + 1.0) / 8)
    slopes = jnp.broadcast_to(jnp.repeat(slope.reshape(4, 2), HEAD, axis=1)[:, None, :], (4, 8, 128))

    tr = lambda t: jnp.transpose(t[0])

    wint, wkv_own, wout_own = _allgather_weights(tr(w_in), w_mem_kv[0], w_out[0])
    started0 = _exchange_start("exchange0_start", [wkv_own, wout_own], [False, False])
    proj, qkv16 = _proj_fwd(x2, g_norm, wint, started0[4])
    wkv, wout = _exchange_wait("exchange0_wait", started0, [False, False], proj)
    wkv, wout = wkv.reshape(D_MODEL, 2 * MEM_W), wout.reshape(D_MODEL, D_MODEL)
    kv = _memkv_fwd(mem2, g_mem, wkv)
    a, lse = _attn_fwd(proj, qkv16, slopes)
    gated = _branch_fwd(proj, a, kv, w_s, b_exp, g_sgu_v)
    dh2, loss8, dgf, dwout = _outproj_loss(gated, wout, x2, tgt2, g_final.reshape(1, D_MODEL))

    da, dr, dkv, dws, dbs, dgv = _branch_bwd(dh2, wout, proj, a, kv, w_s, b_exp, g_sgu_v)
    dwkv, dgm = _memkv_bwd(dkv, mem2, g_mem, wkv)

    early = [dws.reshape(4 * CHUNK, CHUNK), dbs, dgv, dgm, dgf, loss8]
    scatter1 = [True, True] + [False] * len(early)
    started1 = _exchange_start("exchange1_start", [dwkv, dwout] + early, scatter1)
    dq, dk, dv = _attn_bwd(proj, qkv16, slopes, da, a, lse, started1[4])
    s_wkv, s_wout, s_ws, s_bs, s_gv, s_gm, s_gf, s_loss = _exchange_wait("exchange1_wait", started1, scatter1, dq)

    dwint0 = _dwin(dq, dk, dv, dr, x2, g_norm, 0, started1[4])
    started2 = _exchange_start("exchange2_start", [dwint0], [True])
    dwint1 = _dwin(dq, dk, dv, dr, x2, g_norm, 1, started2[4])
    started3 = _exchange_start("exchange3_start", [dwint1], [True])
    grad_x, dgn = _dx(dq, dk, dv, dr, wint, x2, dh2, g_norm, started3[4])
    s_win0, = _exchange_wait("exchange2_wait", started2, [True], grad_x)
    dgn_sum, = _allreduce_small([dgn])
    s_win1, = _exchange_wait("exchange3_wait", started3, [True], dgn_sum)

    g_win, d_win, m_win, v_win = map(
        jnp.transpose, _adam_slots("w_in", [s_win0, s_win1], tr(w_in), tr(m_w_in), tr(v_w_in)))
    g_wkv, d_wkv, m_wkv, v_wkv = _adam_slots("w_mem_kv", [s_wkv], w_mem_kv[0], m_w_mem_kv[0], v_w_mem_kv[0])
    g_wout, d_wout, m_wout, v_wout = _adam_slots("w_out", [s_wout], w_out[0], m_w_out[0], v_w_out[0])

    small_shapes = [(1, D_MODEL), (4 * CHUNK, CHUNK), (4, CHUNK), (1, SGU_W), (1, D_MODEL), (1, D_MODEL)]
    pack = lambda arrs: [t.reshape(s) for t, s in zip(arrs, small_shapes)]
    g_small, loss_sum, upd = _adam_small(
        pack([g_norm, w_sgu_spatial, b_sgu_spatial, g_sgu_v, g_mem, g_final]),
        [dgn_sum, s_ws, s_bs, s_gv, s_gm, s_gf],
        pack([m_g_norm, m_w_sgu_spatial, m_b_sgu_spatial, m_g_sgu_v, m_g_mem, m_g_final]),
        pack([v_g_norm, v_w_sgu_spatial, v_b_sgu_spatial, v_g_sgu_v, v_g_mem, v_g_final]), s_loss)
    out_shapes = [g_norm.shape, w_sgu_spatial.shape, b_sgu_spatial.shape, g_sgu_v.shape, g_mem.shape, g_final.shape]
    unpack = lambda arrs: [t.reshape(s) for t, s in zip(arrs, out_shapes)]
    gs = unpack(g_small)
    ds, nms, nvs = unpack(upd[0::3]), unpack(upd[1::3]), unpack(upd[2::3])

    loss = loss_sum[0, 0]

    def assemble(small, win, wkv_, wout_):
        return [small[0], win[None], small[1], small[2], small[3], small[4], wkv_[None], wout_[None], small[5]]

    return (loss, grad_x.reshape(x.shape),
            *assemble(gs, g_win, g_wkv, g_wout), *assemble(ds, d_win, d_wkv, d_wout),
            *assemble(nms, m_win, m_wkv, m_wout), *assemble(nvs, v_win, v_wkv, v_wout))
```

```python
import jax
import jax.numpy as jnp
from jax import lax
from jax.experimental import pallas as pl
from jax.experimental.pallas import tpu as pltpu

F32 = jnp.float32
BF16 = jnp.bfloat16
SDS = jax.ShapeDtypeStruct
MESH = pl.DeviceIdType.MESH

N_DEV = 8
D_MODEL = 1024
SEQ = 2048
B_LOC = 2
T_LOC = B_LOC * SEQ
N_MEM = 256
HEAD = 64
ATTN_W = 512
SGU_W = 256
MEM_W = 256
IN_COLS = 3328
W_IN_SHARD = IN_COLS // N_DEV
CHUNK = 128
DILATIONS = ((1, 2048), (4, 512), (16, 128))
RADIUS = 64
EPS = 1e-6
NEG = -1e30
SCALE = HEAD ** -0.5
C_QA, C_KA, C_VA, C_ZA, C_UB, C_VB, C_ZB, C_QM, C_ZM = 0, 512, 1024, 1536, 2048, 2304, 2560, 2816, 3072
QKV_W = 1536
REST_W = IN_COLS - QKV_W

ADAM_LR, ADAM_B1, ADAM_B2, ADAM_EPS, ADAM_WD, ADAM_STEP = 0.001, 0.9, 0.999, 1e-08, 0.01, 10

V7X_VMEM_MIB = 64
VMEM_NO_STAGING_MIB = V7X_VMEM_MIB - 6


def _params(vmem_mib, sem=None):
    assert vmem_mib < V7X_VMEM_MIB
    return pltpu.CompilerParams(vmem_limit_bytes=vmem_mib << 20, dimension_semantics=sem)


_TOKEN = pl.BlockSpec(memory_space=pl.ANY)


def _dot(a, b):
    return jnp.dot(a.astype(BF16), b.astype(BF16), preferred_element_type=F32)


def _dot_nt(a, b):
    return lax.dot_general(a.astype(BF16), b.astype(BF16), (((1,), (1,)), ((), ())), preferred_element_type=F32)


def _dot_tn(a, b):
    return lax.dot_general(a.astype(BF16), b.astype(BF16), (((0,), (0,)), ((), ())), preferred_element_type=F32)


def _rstd(v):
    return lax.rsqrt(jnp.mean(v * v, axis=-1, keepdims=True) + EPS)


def _rms_bwd(v, r, g, dy):
    gdy = g * dy
    return r * gdy - v * (r * r * r * jnp.mean(gdy * v, axis=-1, keepdims=True))


def _sigmoid(z):
    return 1.0 / (1.0 + jnp.exp(-z))


def _silu_and_grad(z):
    s = _sigmoid(z)
    return z * s, s * (1.0 + z * (1.0 - s))


_G_C = 0.7978845608028654
_G_K = 0.044715


def _gelu_and_grad(v):
    t = jnp.tanh(_G_C * (v + _G_K * (v * v * v)))
    cdf = 0.5 * (1.0 + t)
    return v * cdf, cdf + 0.5 * v * (1.0 - t * t) * (_G_C * (1.0 + 3.0 * _G_K * v * v))


def _cast_rows(src_ref, dst_ref, rows, step=256):
    def one(i, carry):
        r = pl.ds(pl.multiple_of(i * step, step), step)
        dst_ref[r, :] = src_ref[r, :].astype(dst_ref.dtype)
        return carry
    lax.fori_loop(0, rows // step, one, 0)


def _left_lanes(rows):
    return lax.broadcasted_iota(jnp.int32, (rows, 128), 1) < HEAD


def _mesh_pos():
    return lax.axis_index("x"), lax.axis_index("y"), lax.axis_index("c")


def _peer(pos, k):
    x, y, c = pos
    return (1 - x if k & 4 else x, 1 - y if k & 2 else y, 1 - c if k & 1 else c)


def _flat(pos):
    return 4 * pos[0] + 2 * pos[1] + pos[2]


def _allgather_weights(w_in_t, w_kv, w_out):
    def body(win_ref, wkv_ref, wout_ref, wint_o, wkv_o, wout_o, send_sems, recv_sems):
        x, y, c = _mesh_pos()
        me, sib = (x, y, c), (x, y, 1 - c)
        chips = [(1 - x, y), (x, 1 - y), (1 - x, 1 - y)]

        def rows(p):
            return wint_o.at[pl.ds(pl.multiple_of(_flat(p) * W_IN_SHARD, 16), W_IN_SHARD), :]

        rows(me)[...] = win_ref[...].astype(BF16)

        def copy(k, block, to):
            return pltpu.make_async_remote_copy(
                src_ref=rows(block), dst_ref=rows(block), send_sem=send_sems.at[k], recv_sem=recv_sems.at[k],
                device_id=to, device_id_type=MESH)

        first = [copy(0, me, sib)] + [copy(1 + j, me, (*chip, c)) for j, chip in enumerate(chips)]
        for cp in first:
            cp.start()
        wkv_o[...] = wkv_ref[...].astype(BF16)
        wout_o[...] = wout_ref[...].astype(BF16)
        passed = []
        for j, chip in enumerate(chips):
            copy(1 + j, (*chip, c), me).wait_recv()
            fwd = copy(4 + j, (*chip, c), sib)
            fwd.start()
            passed.append(fwd)
        copy(0, sib, me).wait_recv()
        for j, chip in enumerate(chips):
            copy(4 + j, (*chip, 1 - c), me).wait_recv()
        for cp in first + passed:
            cp.wait_send()

    vmem = pl.BlockSpec(memory_space=pltpu.VMEM)
    return pl.pallas_call(
        body, name="allgather_weights",
        out_shape=(SDS((IN_COLS, D_MODEL), BF16), SDS(w_kv.shape, BF16), SDS(w_out.shape, BF16)),
        in_specs=[vmem, vmem, vmem], out_specs=(vmem, vmem, vmem),
        scratch_shapes=[pltpu.SemaphoreType.DMA((7,)), pltpu.SemaphoreType.DMA((7,))],
        compiler_params=_params(40),
    )(w_in_t, w_kv, w_out)


def _proj_fwd(x2, g_norm, wint, token):
    tm = 512
    sub = 256
    d = DILATIONS[2][0]
    per_ex = SEQ // tm

    def body(x_ref, g_ref, w_ref, _, o_ref, o16_ref):
        xv = x_ref[...]
        h = xv * _rstd(xv) * g_ref[...]
        res = _dot_nt(h, w_ref[...])
        o_ref[...] = res
        r_out = lax.broadcasted_iota(jnp.int32, (sub, sub), 0)
        r_in = lax.broadcasted_iota(jnp.int32, (sub, sub), 1)
        pick = (r_in == d * (r_out % (sub // d)) + r_out // (sub // d)).astype(BF16)
        for part in range(tm // sub):
            grouped = _dot(pick, res[part * sub:(part + 1) * sub, 0:QKV_W]).astype(BF16)
            for rho in range(d):
                o16_ref[0, rho, part * (sub // d):(part + 1) * (sub // d), :] = (
                    grouped[rho * (sub // d):(rho + 1) * (sub // d), :])

    return pl.pallas_call(
        body, name="proj_fwd", grid=(T_LOC // tm,),
        in_specs=[pl.BlockSpec((tm, D_MODEL), lambda i: (i, 0)), pl.BlockSpec((1, D_MODEL), lambda i: (0, 0)),
                  pl.BlockSpec((IN_COLS, D_MODEL), lambda i: (0, 0)), _TOKEN],
        out_specs=(pl.BlockSpec((tm, IN_COLS), lambda i: (i, 0)),
                   pl.BlockSpec((1, d, tm // d, QKV_W), lambda i: (i // per_ex, 0, i % per_ex, 0))),
        out_shape=(SDS((T_LOC, IN_COLS), F32), SDS((B_LOC, d, SEQ // d, QKV_W), BF16)),
        compiler_params=_params(48, ("arbitrary",)),
    )(x2, g_norm, wint, token)


def _memkv_fwd(mem2, g_mem, wkv):
    def body(m_ref, g_ref, w_ref, o_ref):
        mv = m_ref[...]
        o_ref[...] = _dot(mv * _rstd(mv) * g_ref[...], w_ref[...])

    return pl.pallas_call(
        body, name="memkv_fwd", out_shape=SDS((B_LOC * N_MEM, 2 * MEM_W), F32), compiler_params=_params(32),
    )(mem2, g_mem, wkv)


N_BIAS = 7


def _fill_bias_tables(sl_ref, tab):
    for cfg, (d, length) in enumerate(DILATIONS):
        nk = min(length, 2 * CHUNK)
        r = lax.broadcasted_iota(jnp.int32, (CHUNK, nk), 0)
        c = lax.broadcasted_iota(jnp.int32, (CHUNK, nk), 1)
        for var in range(3 if length > nk else 1):
            rel = jnp.abs(r - c + var * RADIUS)
            dist = rel.astype(F32) * float(d)
            for h in range(2):
                slope = sl_ref[0, 0:1, h * HEAD:h * HEAD + 1]
                tab[3 * cfg + var, h * CHUNK:(h + 1) * CHUNK, 0:nk] = jnp.where(rel <= RADIUS, -slope * dist, NEG)


def _attn_blocks(visit, unroll):
    def step(t, carry):
        for cfg, (d, length) in enumerate(DILATIONS):
            nblk = length // CHUNK
            if nblk == 1:
                visit(cfg, 0, t, t, length, t)
                continue
            rho, i = (0, t) if d == 1 else (t // nblk, t % nblk)
            ks = jnp.clip(i * CHUNK - RADIUS, 0, length - 2 * CHUNK)
            visit(cfg, (i * CHUNK - ks) // RADIUS, rho + d * (i * CHUNK), rho + d * ks, 2 * CHUNK, t)
        return carry
    lax.fori_loop(0, 16, step, 0, unroll=unroll)


def _stack_heads(v, left):
    return jnp.concatenate([jnp.where(left, v, 0.0), jnp.where(left, 0.0, v)], axis=0)


def _unstack_heads(v, left):
    return jnp.where(left, v[0:CHUNK], v[CHUNK:2 * CHUNK])


def _rows(start, n, d):
    return pl.ds(start, n) if d == 1 else pl.ds(start, n, stride=d)


def _blk16(col0):
    d, length = DILATIONS[2]
    return pl.BlockSpec((1, d, length, 128), lambda b, hp: (b, 0, 0, col0 // 128 + hp))


def _attn_fwd(proj, qkv16, slopes):
    def body(sl_ref, q_ref, k_ref, v_ref, q16_ref, k16_ref, v16_ref, a_ref, lse_ref, *scr):
        o_c, m_c, l_c, tab = scr[0:3], scr[3:6], scr[6:9], scr[9]
        left = _left_lanes(CHUNK)
        _fill_bias_tables(sl_ref, tab)

        def block(cfg, var, q0, k0, nk, t):
            d = DILATIONS[cfg][0]
            rq, rk = _rows(q0, CHUNK, d), _rows(k0, nk, d)
            if cfg == 2:
                qb, kw, vw = q16_ref[0, t].astype(F32), k16_ref[0, t], v16_ref[0, t]
            else:
                qb, kw, vw = q_ref[rq, :], k_ref[rk, :], v_ref[rk, :]
            qs = _stack_heads(qb * SCALE, left)
            s = _dot_nt(qs, kw) + tab[3 * cfg + var, :, 0:nk]
            m = jnp.max(s, axis=-1, keepdims=True)
            p = jnp.exp(s - m)
            o_c[cfg][rq, :] = _unstack_heads(_dot(p, vw), left)
            m_c[cfg][rq, :] = _unstack_heads(m, left)
            l_c[cfg][rq, :] = _unstack_heads(jnp.sum(p, axis=-1, keepdims=True), left)
        _attn_blocks(block, 8)

        def merge(j, carry):
            rows = pl.ds(pl.multiple_of(j * 256, 256), 256)
            ms = [m_c[i][rows, :] for i in range(3)]
            top = jnp.maximum(jnp.maximum(ms[0], ms[1]), ms[2])
            ws = [jnp.exp(m - top) for m in ms]
            den = l_c[0][rows, :] * ws[0] + l_c[1][rows, :] * ws[1] + l_c[2][rows, :] * ws[2]
            num = o_c[0][rows, :] * ws[0] + o_c[1][rows, :] * ws[1] + o_c[2][rows, :] * ws[2]
            a_ref[rows, :] = num / den
            lse_ref[rows, :] = top + jnp.log(den)
            return carry
        lax.fori_loop(0, SEQ // 256, merge, 0)

    blk = lambda col0: pl.BlockSpec((SEQ, 128), lambda b, hp: (b, col0 // 128 + hp))
    out = pl.BlockSpec((SEQ, 128), lambda b, hp: (b, hp))
    return pl.pallas_call(
        body, name="attn_fwd", grid=(B_LOC, 4),
        in_specs=[pl.BlockSpec((1, 8, 128), lambda b, hp: (hp, 0, 0)), blk(C_QA), blk(C_KA), blk(C_VA),
                  _blk16(C_QA), _blk16(C_KA), _blk16(C_VA)],
        out_specs=(out, out),
        out_shape=(SDS((T_LOC, ATTN_W), F32), SDS((T_LOC, ATTN_W), F32)),
        scratch_shapes=[pltpu.VMEM((SEQ, 128), F32)] * 9 + [pltpu.VMEM((N_BIAS, 2 * CHUNK, 2 * CHUNK), F32)],
        compiler_params=_params(VMEM_NO_STAGING_MIB, ("arbitrary", "arbitrary")),
    )(slopes, proj, proj, proj, qkv16, qkv16, qkv16)


def _chunks_side_by_side(v, pr, tm):
    return jnp.concatenate([v[ch * CHUNK:(ch + 1) * CHUNK, pr * 128:(pr + 1) * 128] for ch in range(tm // CHUNK)], axis=1)


def _first_group_lanes(tm):
    return lax.broadcasted_iota(jnp.int32, (CHUNK, tm), 1) % 128 < HEAD


def _store_chunks(dst_ref, pr, val, tm):
    for ch in range(tm // CHUNK):
        dst_ref[ch * CHUNK:(ch + 1) * CHUNK, pr * 128:(pr + 1) * 128] = val[:, ch * CHUNK:(ch + 1) * CHUNK]


def _sgu_mix(vn, ws_ref, dst_ref, tm):
    first = _first_group_lanes(tm)
    for pr in range(2):
        vp = _chunks_side_by_side(vn, pr, tm)
        _store_chunks(dst_ref, pr, jnp.where(first, _dot(ws_ref[2 * pr], vp), _dot(ws_ref[2 * pr + 1], vp)), tm)


def _mem_head_of_lane(rows):
    return lax.broadcasted_iota(jnp.int32, (rows, MEM_W), 1) // HEAD


def _stack_mem_heads(v, rows):
    head = _mem_head_of_lane(rows)
    return jnp.concatenate([jnp.where(head == h, v, 0.0) for h in range(4)], axis=0)


def _unstack_mem_heads(v, rows):
    head = _mem_head_of_lane(rows)
    out = v[0:rows]
    for h in range(1, 4):
        out = jnp.where(head == h, v[h * rows:(h + 1) * rows], out)
    return out


def _mem_attn_probs(q, kmem, rows):
    qs = _stack_mem_heads(q, rows).astype(BF16)
    s = _dot_nt(qs, kmem) * SCALE
    e = jnp.exp(s - jnp.max(s, axis=-1, keepdims=True))
    return e * (1.0 / jnp.sum(e, axis=-1, keepdims=True)), qs


def _branch_blocks(tm):
    col = lambda w, c0: pl.BlockSpec((tm, w), lambda i: (i, c0 // w))
    return [col(512, C_ZA), col(256, C_UB), col(256, C_VB), col(256, C_ZB), col(256, C_QM), col(256, C_ZM)]


def _branch_fwd(proj, a, kv, w_s, b_exp, g_v):
    tm = 512
    per_ex = SEQ // tm

    def body(za_ref, ub_ref, vb_ref, zb_ref, qm_ref, zm_ref, a_ref, kv_ref, ws_ref, be_ref, gv_ref, o_ref, mix):
        o_ref[:, 0:ATTN_W] = (_silu_and_grad(za_ref[...])[0] * a_ref[...]).astype(BF16)
        gu = _gelu_and_grad(ub_ref[...])[0]
        gv = _gelu_and_grad(vb_ref[...])[0]
        vn = gv * _rstd(gv) * gv_ref[...]
        _sgu_mix(vn.astype(BF16), ws_ref, mix, tm)
        sg = gu * (mix[...] + jnp.concatenate([be_ref[...]] * (tm // CHUNK), axis=0))
        o_ref[:, ATTN_W:ATTN_W + SGU_W] = (_silu_and_grad(zb_ref[...])[0] * sg).astype(BF16)
        p = _mem_attn_probs(qm_ref[...], kv_ref[:, 0:MEM_W], tm)[0]
        mo = _unstack_mem_heads(_dot(p, kv_ref[:, MEM_W:2 * MEM_W]), tm)
        o_ref[:, ATTN_W + SGU_W:D_MODEL] = (_silu_and_grad(zm_ref[...])[0] * mo).astype(BF16)

    full = lambda shape: pl.BlockSpec(shape, lambda i: (0,) * len(shape))
    return pl.pallas_call(
        body, name="branch_fwd", grid=(T_LOC // tm,),
        in_specs=_branch_blocks(tm) + [
            pl.BlockSpec((tm, ATTN_W), lambda i: (i, 0)), pl.BlockSpec((N_MEM, 2 * MEM_W), lambda i: (i // per_ex, 0)),
            full((4, CHUNK, CHUNK)), full((CHUNK, SGU_W)), full((1, SGU_W))],
        out_specs=pl.BlockSpec((tm, D_MODEL), lambda i: (i, 0)),
        out_shape=SDS((T_LOC, D_MODEL), BF16),
        scratch_shapes=[pltpu.VMEM((tm, SGU_W), F32)],
        compiler_params=_params(VMEM_NO_STAGING_MIB, ("arbitrary",)),
    )(proj, proj, proj, proj, proj, proj, a, kv, w_s, b_exp, g_v)


def _outproj_loss(gated, wout, x2, tgt2, g_final):
    tm = 512

    def body(g_ref, w_ref, x_ref, t_ref, gf_ref, dh2_ref, loss_ref, dgf_ref, dwo_ref, dwo_acc):
        @pl.when(pl.program_id(0) == 0)
        def _():
            loss_ref[...] = jnp.zeros_like(loss_ref)
            dgf_ref[...] = jnp.zeros_like(dgf_ref)
            dwo_acc[...] = jnp.zeros_like(dwo_acc)
        gated = g_ref[...]
        h2 = x_ref[...] + _dot(gated, w_ref[...])
        r = _rstd(h2)
        gf = gf_ref[...]
        err = h2 * r * gf - t_ref[...]
        loss_ref[...] += 0.5 * jnp.sum(jnp.mean(err * err, axis=-1, keepdims=True))
        dy = err * (1.0 / D_MODEL)
        dh2 = _rms_bwd(h2, r, gf, dy)
        dh2_ref[...] = dh2
        dgf_ref[...] += jnp.sum(dy * (h2 * r), axis=0, keepdims=True)
        dwo_acc[...] += _dot_tn(gated, dh2)

        @pl.when(pl.program_id(0) == pl.num_programs(0) - 1)
        def _():
            _cast_rows(dwo_acc, dwo_ref, D_MODEL)

    row = pl.BlockSpec((tm, D_MODEL), lambda i: (i, 0))
    vec = pl.BlockSpec((1, D_MODEL), lambda i: (0, 0))
    square = pl.BlockSpec((D_MODEL, D_MODEL), lambda i: (0, 0))
    return pl.pallas_call(
        body, name="outproj_loss", grid=(T_LOC // tm,),
        in_specs=[row, square, row, row, vec],
        out_specs=(row, pl.BlockSpec((8, 128), lambda i: (0, 0)), vec, square),
        out_shape=(SDS((T_LOC, D_MODEL), F32), SDS((8, 128), F32), SDS((1, D_MODEL), F32), SDS((D_MODEL, D_MODEL), BF16)),
        scratch_shapes=[pltpu.VMEM((D_MODEL, D_MODEL), F32)],
        compiler_params=_params(VMEM_NO_STAGING_MIB, ("arbitrary",)),
    )(gated, wout, x2, tgt2, g_final)


def _branch_bwd(dh2, wout, proj, a, kv, w_s, b_exp, g_v):
    tm = 512
    per_ex = SEQ // tm

    def body(dh2_ref, w_ref, za_ref, ub_ref, vb_ref, zb_ref, qm_ref, zm_ref, a_ref, kv_ref, ws_ref,
             be_ref, gv_ref, da_ref, dr_ref, dkv_ref, dws_ref, db_ref, dgv_ref, mix, dvn, dmsum):
        i = pl.program_id(0)

        @pl.when(i == 0)
        def _():
            dws_ref[...] = jnp.zeros_like(dws_ref)
            dgv_ref[...] = jnp.zeros_like(dgv_ref)
            dmsum[...] = jnp.zeros_like(dmsum)

        @pl.when(i % per_ex == 0)
        def _():
            dkv_ref[...] = jnp.zeros_like(dkv_ref)

        dg = _dot_nt(dh2_ref[...], w_ref[...])

        sa, dsa = _silu_and_grad(za_ref[...])
        dga = dg[:, 0:ATTN_W]
        da_ref[...] = dga * sa
        dr_ref[:, 0:512] = (dga * a_ref[...] * dsa).astype(BF16)

        ub, vb = ub_ref[...], vb_ref[...]
        gu, dgu = _gelu_and_grad(ub)
        gv, dgv = _gelu_and_grad(vb)
        rv = _rstd(gv)
        gain = gv_ref[...]
        vn = (gv * rv * gain).astype(BF16)
        _sgu_mix(vn, ws_ref, mix, tm)
        mixed = mix[...] + jnp.concatenate([be_ref[...]] * (tm // CHUNK), axis=0)
        sb, dsb = _silu_and_grad(zb_ref[...])
        dgb = dg[:, ATTN_W:ATTN_W + SGU_W]
        dsg = dgb * sb
        dr_ref[:, 512:768] = (dsg * mixed * dgu).astype(BF16)
        dr_ref[:, 1024:1280] = (dgb * (gu * mixed) * dsb).astype(BF16)
        dmix = dsg * gu
        for ch in range(tm // CHUNK):
            dmsum[...] += dmix[ch * CHUNK:(ch + 1) * CHUNK, :]
        first = _first_group_lanes(tm)
        for pr in range(2):
            dmp, vp = _chunks_side_by_side(dmix, pr, tm), _chunks_side_by_side(vn, pr, tm)
            dws_ref[2 * pr] += _dot_nt(jnp.where(first, dmp, 0.0), vp)
            dws_ref[2 * pr + 1] += _dot_nt(jnp.where(first, 0.0, dmp), vp)
            _store_chunks(dvn, pr, jnp.where(first, _dot_tn(ws_ref[2 * pr], dmp), _dot_tn(ws_ref[2 * pr + 1], dmp)), tm)
        dvn_v = dvn[...]
        dgv_ref[...] += jnp.sum(dvn_v * (gv * rv), axis=0, keepdims=True)
        dr_ref[:, 768:1024] = (_rms_bwd(gv, rv, gain, dvn_v) * dgv).astype(BF16)

        szm, dszm = _silu_and_grad(zm_ref[...])
        dgm = dg[:, ATTN_W + SGU_W:D_MODEL]
        kmem, vmem_ = kv_ref[:, 0:MEM_W].astype(BF16), kv_ref[:, MEM_W:2 * MEM_W].astype(BF16)
        p, qs = _mem_attn_probs(qm_ref[...], kmem, tm)
        dmos = _stack_mem_heads(dgm * szm, tm).astype(BF16)
        dr_ref[:, 1536:1792] = (dgm * _unstack_mem_heads(_dot(p, vmem_), tm) * dszm).astype(BF16)
        dp = _dot_nt(dmos, vmem_)
        ds = (p * (dp - jnp.sum(p * dp, axis=-1, keepdims=True)) * SCALE).astype(BF16)
        dr_ref[:, 1280:1536] = _unstack_mem_heads(_dot(ds, kmem), tm).astype(BF16)
        dkv_ref[:, 0:MEM_W] += _dot_tn(ds, qs)
        dkv_ref[:, MEM_W:2 * MEM_W] += _dot_tn(p, dmos)

        @pl.when(i == pl.num_programs(0) - 1)
        def _():
            tot = dmsum[...]
            hi = tot.astype(BF16)
            lo = (tot - hi.astype(F32)).astype(BF16)
            grp = (lax.broadcasted_iota(jnp.int32, (SGU_W, 128), 0) // HEAD
                   == lax.broadcasted_iota(jnp.int32, (SGU_W, 128), 1)).astype(BF16)
            db_ref[...] = (_dot(hi, grp) + _dot(lo, grp)).T[0:4, :]

    full = lambda shape: pl.BlockSpec(shape, lambda i: (0,) * len(shape))
    row = lambda w: pl.BlockSpec((tm, w), lambda i: (i, 0))
    return pl.pallas_call(
        body, name="branch_bwd", grid=(T_LOC // tm,),
        in_specs=[row(D_MODEL), full((D_MODEL, D_MODEL))] + _branch_blocks(tm) + [
            row(ATTN_W), pl.BlockSpec((N_MEM, 2 * MEM_W), lambda i: (i // per_ex, 0)),
            full((4, CHUNK, CHUNK)), full((CHUNK, SGU_W)), full((1, SGU_W))],
        out_specs=(row(ATTN_W), row(REST_W), pl.BlockSpec((N_MEM, 2 * MEM_W), lambda i: (i // per_ex, 0)),
                   full((4, CHUNK, CHUNK)), full((4, CHUNK)), full((1, SGU_W))),
        out_shape=(SDS((T_LOC, ATTN_W), F32), SDS((T_LOC, REST_W), BF16), SDS((B_LOC * N_MEM, 2 * MEM_W), F32),
                   SDS((4, CHUNK, CHUNK), F32), SDS((4, CHUNK), F32), SDS((1, SGU_W), F32)),
        scratch_shapes=[pltpu.VMEM((tm, SGU_W), F32), pltpu.VMEM((tm, SGU_W), F32), pltpu.VMEM((CHUNK, SGU_W), F32)],
        compiler_params=_params(56, ("arbitrary",)),
    )(dh2, wout, proj, proj, proj, proj, proj, proj, a, kv, w_s, b_exp, g_v)


def _attn_bwd(proj, qkv16, slopes, da, a, lse, token):
    def body(sl_ref, q_ref, k_ref, v_ref, q16_ref, k16_ref, v16_ref, da_ref, a_ref, lse_ref, _,
             dq_ref, dk_ref, dv_ref, *scr):
        dq_s, dk_s, dv_s, tab = scr[0:3], scr[3:6], scr[6:9], scr[9]
        lse_h, delta_h = scr[10:12], scr[12:14]
        p_all, ds_all = scr[14], scr[15]
        left = _left_lanes(CHUNK)
        _fill_bias_tables(sl_ref, tab)

        def prep(j, carry):
            rows = pl.ds(pl.multiple_of(j * 256, 256), 256)
            l256 = _left_lanes(256)
            prod = da_ref[rows, :] * a_ref[rows, :]
            delta_h[0][rows, :] = jnp.broadcast_to(jnp.sum(jnp.where(l256, prod, 0.0), axis=-1, keepdims=True), (256, 128))
            delta_h[1][rows, :] = jnp.broadcast_to(jnp.sum(jnp.where(l256, 0.0, prod), axis=-1, keepdims=True), (256, 128))
            pair = lse_ref[rows, :]
            other = pltpu.roll(pair, HEAD, axis=1)
            lse_h[0][rows, :] = jnp.where(l256, pair, other)
            lse_h[1][rows, :] = jnp.where(l256, other, pair)
            zero = jnp.zeros((256, 128), F32)
            for cfg in range(3):
                dk_s[cfg][rows, :] = zero
                dv_s[cfg][rows, :] = zero
            return carry
        lax.fori_loop(0, SEQ // 256, prep, 0)

        def per_row(halves, rq, nk):
            v = jnp.concatenate([halves[0][rq, :], halves[1][rq, :]], axis=0)
            return v if nk == 128 else jnp.concatenate([v, v], axis=1)

        def qkv(cfg, rq, rk, t):
            if cfg == 2:
                return q16_ref[0, t].astype(F32), k16_ref[0, t], v16_ref[0, t]
            return q_ref[rq, :], k_ref[rk, :], v_ref[rk, :]

        def probs(cfg, var, q0, k0, nk, t):
            d = DILATIONS[cfg][0]
            rq, rk = _rows(q0, CHUNK, d), _rows(k0, nk, d)
            qb, kw, vw = qkv(cfg, rq, rk, t)
            qs = _stack_heads(qb * SCALE, left)
            das = _stack_heads(da_ref[rq, :], left)
            s = _dot_nt(qs, kw) + tab[3 * cfg + var, :, 0:nk]
            p = jnp.exp(s - per_row(lse_h, rq, nk))
            p_all[16 * cfg + t, :, 0:nk] = p.astype(BF16)
            ds_all[16 * cfg + t, :, 0:nk] = (p * (_dot_nt(das, vw) - per_row(delta_h, rq, nk))).astype(BF16)
        _attn_blocks(probs, 4)

        def grads(cfg, var, q0, k0, nk, t):
            d = DILATIONS[cfg][0]
            rq, rk = _rows(q0, CHUNK, d), _rows(k0, nk, d)
            qb, kw, _ = qkv(cfg, rq, rk, t)
            qs = _stack_heads(qb * SCALE, left).astype(BF16)
            das = _stack_heads(da_ref[rq, :], left).astype(BF16)
            p, ds = p_all[16 * cfg + t, :, 0:nk], ds_all[16 * cfg + t, :, 0:nk]
            dq_s[cfg][rq, :] = _unstack_heads(_dot(ds, kw), left) * SCALE
            dk_s[cfg][rk, :] += _dot_tn(ds, qs)
            dv_s[cfg][rk, :] += _dot_tn(p, das)
        _attn_blocks(grads, 4)

        def flush(j, carry):
            rows = pl.ds(pl.multiple_of(j * 256, 256), 256)
            for acc, dst in ((dq_s, dq_ref), (dk_s, dk_ref), (dv_s, dv_ref)):
                dst[rows, :] = (acc[0][rows, :] + acc[1][rows, :] + acc[2][rows, :]).astype(BF16)
            return carry
        lax.fori_loop(0, SEQ // 256, flush, 0)

    blk = lambda col0: pl.BlockSpec((SEQ, 128), lambda b, hp: (b, col0 // 128 + hp))
    own = pl.BlockSpec((SEQ, 128), lambda b, hp: (b, hp))
    return pl.pallas_call(
        body, name="attn_bwd", grid=(B_LOC, 4),
        in_specs=[pl.BlockSpec((1, 8, 128), lambda b, hp: (hp, 0, 0)), blk(C_QA), blk(C_KA), blk(C_VA),
                  _blk16(C_QA), _blk16(C_KA), _blk16(C_VA), own, own, own, _TOKEN],
        out_specs=(own, own, own),
        out_shape=(SDS((T_LOC, ATTN_W), BF16),) * 3,
        scratch_shapes=[pltpu.VMEM((SEQ, 128), F32)] * 9 + [pltpu.VMEM((N_BIAS, 2 * CHUNK, 2 * CHUNK), F32)]
        + [pltpu.VMEM((SEQ, 128), F32)] * 4 + [pltpu.VMEM((48, 2 * CHUNK, 2 * CHUNK), BF16)] * 2,
        compiler_params=_params(52, ("arbitrary", "arbitrary")),
    )(slopes, proj, proj, proj, qkv16, qkv16, qkv16, da, a, lse, token)


def _dproj_specs(tm):
    third = pl.BlockSpec((tm, ATTN_W), lambda i: (i, 0))
    return [third, third, third, pl.BlockSpec((tm, REST_W), lambda i: (i, 0))]


def _dx(dq, dk, dv, dr, wint, x2, dh2, g_norm, token):
    tm = 512

    def body(dq_ref, dk_ref, dv_ref, dr_ref, w_ref, x_ref, dh2_ref, g_ref, _, gx_ref, dgn_ref):
        @pl.when(pl.program_id(0) == 0)
        def _():
            dgn_ref[...] = jnp.zeros_like(dgn_ref)
        dh = (_dot(dq_ref[...], w_ref[C_QA:C_KA, :]) + _dot(dk_ref[...], w_ref[C_KA:C_VA, :])
              + _dot(dv_ref[...], w_ref[C_VA:C_ZA, :]) + _dot(dr_ref[...], w_ref[C_ZA:IN_COLS, :]))
        xv = x_ref[...]
        r = _rstd(xv)
        gx_ref[...] = dh2_ref[...] + _rms_bwd(xv, r, g_ref[...], dh)
        dgn_ref[...] += jnp.sum(dh * (xv * r), axis=0, keepdims=True)

    row = pl.BlockSpec((tm, D_MODEL), lambda i: (i, 0))
    vec = pl.BlockSpec((1, D_MODEL), lambda i: (0, 0))
    return pl.pallas_call(
        body, name="dx", grid=(T_LOC // tm,),
        in_specs=_dproj_specs(tm) + [pl.BlockSpec((IN_COLS, D_MODEL), lambda i: (0, 0)), row, row, vec, _TOKEN],
        out_specs=(row, vec),
        out_shape=(SDS((T_LOC, D_MODEL), F32), SDS((1, D_MODEL), F32)),
        compiler_params=_params(48, ("arbitrary",)),
    )(dq, dk, dv, dr, wint, x2, dh2, g_norm, token)


def _dwin(dq, dk, dv, dr, x2, g_norm, half, token):
    tm = 1024
    width = D_MODEL // 2
    cols = slice(half * width, (half + 1) * width)

    def body(dq_ref, dk_ref, dv_ref, dr_ref, x_ref, g_ref, _, o_ref, acc):
        @pl.when(pl.program_id(0) == 0)
        def _():
            acc[...] = jnp.zeros_like(acc)
        xv = x_ref[...]
        h = (xv[:, cols] * _rstd(xv) * g_ref[:, cols]).astype(BF16)
        acc[C_QA:C_KA, :] += _dot_tn(dq_ref[...], h)
        acc[C_KA:C_VA, :] += _dot_tn(dk_ref[...], h)
        acc[C_VA:C_ZA, :] += _dot_tn(dv_ref[...], h)
        acc[C_ZA:IN_COLS, :] += _dot_tn(dr_ref[...], h)

        @pl.when(pl.program_id(0) == pl.num_programs(0) - 1)
        def _():
            _cast_rows(acc, o_ref, IN_COLS)

    return pl.pallas_call(
        body, name="dwin%d" % half, grid=(T_LOC // tm,),
        in_specs=_dproj_specs(tm) + [pl.BlockSpec((tm, D_MODEL), lambda i: (i, 0)),
                                     pl.BlockSpec((1, D_MODEL), lambda i: (0, 0)), _TOKEN],
        out_specs=pl.BlockSpec((IN_COLS, width), lambda i: (0, 0)),
        out_shape=SDS((IN_COLS, width), BF16),
        scratch_shapes=[pltpu.VMEM((IN_COLS, width), F32)],
        compiler_params=_params(48, ("arbitrary",)),
    )(dq, dk, dv, dr, x2, g_norm, token)


def _memkv_bwd(dkv, mem2, g_mem, wkv):
    def body(dkv_ref, m_ref, g_ref, w_ref, dw_ref, dg_ref):
        mv = m_ref[...]
        r = _rstd(mv)
        dkv_v = dkv_ref[...].astype(BF16)
        dw_ref[...] = _dot_tn(mv * r * g_ref[...], dkv_v).astype(BF16)
        dg_ref[...] = jnp.sum(_dot_nt(dkv_v, w_ref[...]) * (mv * r), axis=0, keepdims=True)

    return pl.pallas_call(
        body, name="memkv_bwd", out_shape=(SDS((D_MODEL, 2 * MEM_W), BF16), SDS((1, D_MODEL), F32)),
        compiler_params=_params(32),
    )(dkv, mem2, g_mem, wkv)


def _allreduce_small(parts):
    n = len(parts)

    def body(*refs):
        ins, outs, bufs = refs[0:n], refs[n:2 * n], refs[2 * n:3 * n]
        send_sems, recv_sems = refs[3 * n], refs[3 * n + 1]
        pos = _mesh_pos()
        me = _flat(pos)
        for a in range(n):
            bufs[a][me] = ins[a][...]

        def copy(a, k, slot):
            return pltpu.make_async_remote_copy(
                src_ref=ins[a], dst_ref=bufs[a].at[slot],
                send_sem=send_sems.at[7 * a + k - 1], recv_sem=recv_sems.at[7 * a + k - 1],
                device_id=_peer(pos, k), device_id_type=MESH)

        sent = [copy(a, k, me) for a in range(n) for k in range(1, N_DEV)]
        for cp in sent:
            cp.start()
        for a in range(n):
            for k in range(1, N_DEV):
                copy(a, k, _flat(_peer(pos, k))).wait_recv()
        for cp in sent:
            cp.wait_send()
        for a in range(n):
            acc = bufs[a][0]
            for s in range(1, N_DEV):
                acc = acc + bufs[a][s]
            outs[a][...] = acc

    vmem = pl.BlockSpec(memory_space=pltpu.VMEM)
    return pl.pallas_call(
        body, name="allreduce_small",
        out_shape=tuple(SDS(p.shape, F32) for p in parts),
        in_specs=[vmem] * n, out_specs=(vmem,) * n,
        scratch_shapes=[pltpu.VMEM((N_DEV,) + p.shape, F32) for p in parts]
        + [pltpu.SemaphoreType.DMA((7 * n,)), pltpu.SemaphoreType.DMA((7 * n,))],
        compiler_params=_params(16),
    )(*parts)


_HBM = pl.BlockSpec(memory_space=pltpu.HBM)
_SEM = pl.BlockSpec(memory_space=pltpu.SEMAPHORE)
_SIDE_EFFECT = pltpu.SideEffectType.DATAFLOW_SIDE_EFFECTING


def _exchange_copies(src_refs, land_refs, scatter, send_sems, recv_sems):
    pos = _mesh_pos()
    copies = []
    for a, (src, land) in enumerate(zip(src_refs, land_refs)):
        n = land.shape[1]
        for k in range(1, N_DEV):
            peer = _peer(pos, k)
            piece = src.at[pl.ds(pl.multiple_of(_flat(peer) * n, 16), n), :] if scatter[a] else src
            copies.append(pltpu.make_async_remote_copy(
                src_ref=piece, dst_ref=land.at[_flat(pos)],
                send_sem=send_sems.at[7 * a + k - 1], recv_sem=recv_sems.at[7 * a + k - 1],
                device_id=peer, device_id_type=MESH))
    return copies


def _own_copies(src_refs, land_refs, scatter, send_sems):
    n = len(src_refs)
    me = _flat(_mesh_pos())
    copies = []
    for a, (src, land) in enumerate(zip(src_refs, land_refs)):
        rows = land.shape[1]
        piece = src.at[pl.ds(pl.multiple_of(me * rows, 16), rows), :] if scatter[a] else src
        copies.append(pltpu.make_async_copy(piece, land.at[me], send_sems.at[7 * n + a]))
    return copies


def _exchange_start(name, srcs, scatter):
    n = len(srcs)

    def body(*refs):
        for cp in _exchange_copies(refs[0:n], refs[n:2 * n], scatter, refs[2 * n], refs[2 * n + 1]):
            cp.start()
        for cp in _own_copies(refs[0:n], refs[n:2 * n], scatter, refs[2 * n]):
            cp.start()
        refs[-1][...] = jnp.zeros_like(refs[-1])

    lands = [lax.empty((N_DEV,) + (t.shape[0] // N_DEV if sc else t.shape[0],) + t.shape[1:], t.dtype)
             for t, sc in zip(srcs, scatter)]
    ops = [pltpu.with_memory_space_constraint(t, pltpu.HBM) for t in (*srcs, *lands)]
    out = pl.pallas_call(
        body, name=name,
        out_shape=(pltpu.SemaphoreType.DMA((8 * n,)), pltpu.SemaphoreType.DMA((7 * n,)),
                   *[pltpu.HBM(t.shape, t.dtype) for t in ops], SDS((8, 128), F32)),
        in_specs=[_HBM] * (2 * n),
        out_specs=(_SEM, _SEM, *[_HBM] * (2 * n), pl.BlockSpec(memory_space=pltpu.VMEM)),
        input_output_aliases={i: 2 + i for i in range(2 * n)},
        compiler_params=pltpu.CompilerParams(has_side_effects=_SIDE_EFFECT),
    )(*ops)
    return out[0], out[1], out[2:2 + n], out[2 + n:2 + 2 * n], out[-1]


def _exchange_wait(name, started, scatter, after):
    send_sems, recv_sems, srcs, lands, _ = started
    n = len(srcs)

    def body(*refs):
        for cp in _exchange_copies(refs[0:n], refs[n:2 * n], scatter, refs[2 * n], refs[2 * n + 1]):
            cp.wait_send()
            cp.wait_recv()
        for cp in _own_copies(refs[0:n], refs[n:2 * n], scatter, refs[2 * n]):
            cp.wait()

    out = pl.pallas_call(
        body, name=name,
        out_shape=tuple(pltpu.HBM(t.shape, t.dtype) for t in (*srcs, *lands)),
        in_specs=[_HBM] * (2 * n) + [_SEM, _SEM, pl.BlockSpec(memory_space=pl.ANY)],
        out_specs=(_HBM,) * (2 * n),
        input_output_aliases={i: i for i in range(2 * n)},
        compiler_params=pltpu.CompilerParams(has_side_effects=_SIDE_EFFECT),
    )(*srcs, *lands, send_sems, recv_sems, after)
    return out[n:]


def _adamw(w, g, m, v):
    m = ADAM_B1 * m + (1.0 - ADAM_B1) * g
    v = ADAM_B2 * v + (1.0 - ADAM_B2) * (g * g)
    m_hat = m / (1.0 - ADAM_B1 ** ADAM_STEP)
    v_hat = v / (1.0 - ADAM_B2 ** ADAM_STEP)
    return -ADAM_LR * (m_hat / (jnp.sqrt(v_hat) + ADAM_EPS) + ADAM_WD * w), m, v


def _adam_slots(name, pieces, w, m, v):
    rows, cols = w.shape
    starts = [sum(p.shape[2] for p in pieces[:i]) for i in range(len(pieces) + 1)]
    assert starts[-1] == cols and all(p.shape[1] == rows for p in pieces)

    def body(*refs):
        s_refs, (w_ref, m_ref, v_ref, g_o, d_o, m_o, v_o, acc) = refs[:len(pieces)], refs[len(pieces):]
        s = pl.program_id(0)

        @pl.when(s == 0)
        def _():
            for i, s_ref in enumerate(s_refs):
                acc[:, starts[i]:starts[i + 1]] = s_ref[0].astype(F32)

        @pl.when(s > 0)
        def _():
            for i, s_ref in enumerate(s_refs):
                acc[:, starts[i]:starts[i + 1]] += s_ref[0].astype(F32)

        @pl.when(s == N_DEV - 1)
        def _():
            g = acc[...]
            g_o[...] = g
            d_o[...], m_o[...], v_o[...] = _adamw(w_ref[...], g, m_ref[...], v_ref[...])

    full = pl.BlockSpec((rows, cols), lambda s: (0, 0))
    return pl.pallas_call(
        body, name="adam_" + name, grid=(N_DEV,),
        in_specs=[pl.BlockSpec((1, rows, p.shape[2]), lambda s: (s, 0, 0)) for p in pieces] + [full, full, full],
        out_specs=(full,) * 4, out_shape=(SDS((rows, cols), F32),) * 4,
        scratch_shapes=[pltpu.VMEM((rows, cols), F32)],
        compiler_params=_params(40, ("arbitrary",)),
    )(*pieces, w, m, v)


def _adam_small(ws, gs, ms, vs, loss_slots):
    n = len(ws)

    def total(ref, like):
        if len(ref.shape) == len(like.shape):
            return ref[...]
        acc = ref[0]
        for s in range(1, N_DEV):
            acc = acc + ref[s]
        return acc

    def body(*refs):
        w_r, g_r, m_r, v_r = refs[0:n], refs[n:2 * n], refs[2 * n:3 * n], refs[3 * n:4 * n]
        loss_r, outs = refs[4 * n], refs[4 * n + 1:]
        for a in range(n):
            g = total(g_r[a], w_r[a])
            outs[a][...] = g
            outs[n + 1 + 3 * a][...], outs[n + 2 + 3 * a][...], outs[n + 3 + 3 * a][...] = _adamw(
                w_r[a][...], g, m_r[a][...], v_r[a][...])
        outs[n][...] = total(loss_r, outs[n])

    out = pl.pallas_call(
        body, name="adam_small",
        out_shape=tuple(SDS(w.shape, F32) for w in ws) + (SDS(loss_slots.shape[1:], F32),)
        + tuple(SDS(w.shape, F32) for w in ws for _ in range(3)),
        compiler_params=_params(16),
    )(*ws, *gs, *ms, *vs, loss_slots)
    return out[0:n], out[n], out[n + 1:]


def kernel(x, mem, g_norm, w_in, w_sgu_spatial, b_sgu_spatial, g_sgu_v, g_mem, w_mem_kv, w_out, g_final, loss_target, m_g_norm, m_w_in, m_w_sgu_spatial, m_b_sgu_spatial, m_g_sgu_v, m_g_mem, m_w_mem_kv, m_w_out, m_g_final, v_g_norm, v_w_in, v_w_sgu_spatial, v_b_sgu_spatial, v_g_sgu_v, v_g_mem, v_w_mem_kv, v_w_out, v_g_final):
    x2 = x.reshape(T_LOC, D_MODEL)
    tgt2 = loss_target.reshape(T_LOC, D_MODEL)
    mem2 = mem.reshape(B_LOC * N_MEM, D_MODEL)
    w_s = w_sgu_spatial[0]
    b_exp = jnp.repeat(b_sgu_spatial[0].T, HEAD, axis=1)
    slope = jnp.power(2.0, -8.0 * (jnp.arange(8, dtype=F32) + 1.0) / 8)
    slopes = jnp.broadcast_to(jnp.repeat(slope.reshape(4, 2), HEAD, axis=1)[:, None, :], (4, 8, 128))

    tr = lambda t: jnp.transpose(t[0])

    wint, wkv_own, wout_own = _allgather_weights(tr(w_in), w_mem_kv[0], w_out[0])
    started0 = _exchange_start("exchange0_start", [wkv_own, wout_own], [False, False])
    proj, qkv16 = _proj_fwd(x2, g_norm, wint, started0[4])
    wkv, wout = _exchange_wait("exchange0_wait", started0, [False, False], proj)
    wkv, wout = wkv.reshape(D_MODEL, 2 * MEM_W), wout.reshape(D_MODEL, D_MODEL)
    kv = _memkv_fwd(mem2, g_mem, wkv)
    a, lse = _attn_fwd(proj, qkv16, slopes)
    gated = _branch_fwd(proj, a, kv, w_s, b_exp, g_sgu_v)
    dh2, loss8, dgf, dwout = _outproj_loss(gated, wout, x2, tgt2, g_final.reshape(1, D_MODEL))

    da, dr, dkv, dws, dbs, dgv = _branch_bwd(dh2, wout, proj, a, kv, w_s, b_exp, g_sgu_v)
    dwkv, dgm = _memkv_bwd(dkv, mem2, g_mem, wkv)

    early = [dws.reshape(4 * CHUNK, CHUNK), dbs, dgv, dgm, dgf, loss8]
    scatter1 = [True, True] + [False] * len(early)
    started1 = _exchange_start("exchange1_start", [dwkv, dwout] + early, scatter1)
    dq, dk, dv = _attn_bwd(proj, qkv16, slopes, da, a, lse, started1[4])
    s_wkv, s_wout, s_ws, s_bs, s_gv, s_gm, s_gf, s_loss = _exchange_wait("exchange1_wait", started1, scatter1, dq)

    dwint0 = _dwin(dq, dk, dv, dr, x2, g_norm, 0, started1[4])
    started2 = _exchange_start("exchange2_start", [dwint0], [True])
    dwint1 = _dwin(dq, dk, dv, dr, x2, g_norm, 1, started2[4])
    started3 = _exchange_start("exchange3_start", [dwint1], [True])
    grad_x, dgn = _dx(dq, dk, dv, dr, wint, x2, dh2, g_norm, started3[4])
    s_win0, = _exchange_wait("exchange2_wait", started2, [True], grad_x)
    dgn_sum, = _allreduce_small([dgn])
    s_win1, = _exchange_wait("exchange3_wait", started3, [True], dgn_sum)

    g_win, d_win, m_win, v_win = map(
        jnp.transpose, _adam_slots("w_in", [s_win0, s_win1], tr(w_in), tr(m_w_in), tr(v_w_in)))
    g_wkv, d_wkv, m_wkv, v_wkv = _adam_slots("w_mem_kv", [s_wkv], w_mem_kv[0], m_w_mem_kv[0], v_w_mem_kv[0])
    g_wout, d_wout, m_wout, v_wout = _adam_slots("w_out", [s_wout], w_out[0], m_w_out[0], v_w_out[0])

    small_shapes = [(1, D_MODEL), (4 * CHUNK, CHUNK), (4, CHUNK), (1, SGU_W), (1, D_MODEL), (1, D_MODEL)]
    pack = lambda arrs: [t.reshape(s) for t, s in zip(arrs, small_shapes)]
    g_small, loss_sum, upd = _adam_small(
        pack([g_norm, w_sgu_spatial, b_sgu_spatial, g_sgu_v, g_mem, g_final]),
        [dgn_sum, s_ws, s_bs, s_gv, s_gm, s_gf],
        pack([m_g_norm, m_w_sgu_spatial, m_b_sgu_spatial, m_g_sgu_v, m_g_mem, m_g_final]),
        pack([v_g_norm, v_w_sgu_spatial, v_b_sgu_spatial, v_g_sgu_v, v_g_mem, v_g_final]), s_loss)
    out_shapes = [g_norm.shape, w_sgu_spatial.shape, b_sgu_spatial.shape, g_sgu_v.shape, g_mem.shape, g_final.shape]
    unpack = lambda arrs: [t.reshape(s) for t, s in zip(arrs, out_shapes)]
    gs = unpack(g_small)
    ds, nms, nvs = unpack(upd[0::3]), unpack(upd[1::3]), unpack(upd[2::3])

    loss = loss_sum[0, 0]

    def assemble(small, win, wkv_, wout_):
        return [small[0], win[None], small[1], small[2], small[3], small[4], wkv_[None], wout_[None], small[5]]

    return (loss, grad_x.reshape(x.shape),
            *assemble(gs, g_win, g_wkv, g_wout), *assemble(ds, d_win, d_wkv, d_wout),
            *assemble(nms, m_win, m_wkv, m_wout), *assemble(nvs, v_win, v_wkv, v_wout))
```

```python
import jax
import jax.numpy as jnp
from jax import lax
from jax.experimental import pallas as pl
from jax.experimental.pallas import tpu as pltpu

F32 = jnp.float32
BF16 = jnp.bfloat16
SDS = jax.ShapeDtypeStruct
MESH = pl.DeviceIdType.MESH

N_DEV = 8
D_MODEL = 1024
SEQ = 2048
B_LOC = 2
T_LOC = B_LOC * SEQ
N_MEM = 256
HEAD = 64
ATTN_W = 512
SGU_W = 256
MEM_W = 256
IN_COLS = 3328
W_IN_SHARD = IN_COLS // N_DEV
CHUNK = 128
DILATIONS = ((1, 2048), (4, 512), (16, 128))
RADIUS = 64
EPS = 1e-6
NEG = -1e30
SCALE = HEAD ** -0.5
C_QA, C_KA, C_VA, C_ZA, C_UB, C_VB, C_ZB, C_QM, C_ZM = 0, 512, 1024, 1536, 2048, 2304, 2560, 2816, 3072
QKV_W = 1536
REST_W = IN_COLS - QKV_W

ADAM_LR, ADAM_B1, ADAM_B2, ADAM_EPS, ADAM_WD, ADAM_STEP = 0.001, 0.9, 0.999, 1e-08, 0.01, 10

V7X_VMEM_MIB = 64
VMEM_NO_STAGING_MIB = V7X_VMEM_MIB - 6


def _params(vmem_mib, sem=None):
    assert vmem_mib < V7X_VMEM_MIB
    return pltpu.CompilerParams(vmem_limit_bytes=vmem_mib << 20, dimension_semantics=sem)


_TOKEN = pl.BlockSpec(memory_space=pl.ANY)


def _dot(a, b):
    return jnp.dot(a.astype(BF16), b.astype(BF16), preferred_element_type=F32)


def _dot_nt(a, b):
    return lax.dot_general(a.astype(BF16), b.astype(BF16), (((1,), (1,)), ((), ())), preferred_element_type=F32)


def _dot_tn(a, b):
    return lax.dot_general(a.astype(BF16), b.astype(BF16), (((0,), (0,)), ((), ())), preferred_element_type=F32)


def _rstd(v):
    return lax.rsqrt(jnp.mean(v * v, axis=-1, keepdims=True) + EPS)


def _rms_bwd(v, r, g, dy):
    gdy = g * dy
    return r * gdy - v * (r * r * r * jnp.mean(gdy * v, axis=-1, keepdims=True))


def _sigmoid(z):
    return 1.0 / (1.0 + jnp.exp(-z))


def _silu_and_grad(z):
    s = _sigmoid(z)
    return z * s, s * (1.0 + z * (1.0 - s))


_G_C = 0.7978845608028654
_G_K = 0.044715


def _gelu_and_grad(v):
    t = jnp.tanh(_G_C * (v + _G_K * (v * v * v)))
    cdf = 0.5 * (1.0 + t)
    return v * cdf, cdf + 0.5 * v * (1.0 - t * t) * (_G_C * (1.0 + 3.0 * _G_K * v * v))


def _cast_rows(src_ref, dst_ref, rows, step=256):
    def one(i, carry):
        r = pl.ds(pl.multiple_of(i * step, step), step)
        dst_ref[r, :] = src_ref[r, :].astype(dst_ref.dtype)
        return carry
    lax.fori_loop(0, rows // step, one, 0)


def _left_lanes(rows):
    return lax.broadcasted_iota(jnp.int32, (rows, 128), 1) < HEAD


def _mesh_pos():
    return lax.axis_index("x"), lax.axis_index("y"), lax.axis_index("c")


def _peer(pos, k):
    x, y, c = pos
    return (1 - x if k & 4 else x, 1 - y if k & 2 else y, 1 - c if k & 1 else c)


def _flat(pos):
    return 4 * pos[0] + 2 * pos[1] + pos[2]


def _allgather_weights(w_in_t, w_kv, w_out):
    def body(win_ref, wkv_ref, wout_ref, wint_o, wkv_o, wout_o, send_sems, recv_sems):
        x, y, c = _mesh_pos()
        me, sib = (x, y, c), (x, y, 1 - c)
        chips = [(1 - x, y), (x, 1 - y), (1 - x, 1 - y)]

        def rows(p):
            return wint_o.at[pl.ds(pl.multiple_of(_flat(p) * W_IN_SHARD, 16), W_IN_SHARD), :]

        rows(me)[...] = win_ref[...].astype(BF16)

        def copy(k, block, to):
            return pltpu.make_async_remote_copy(
                src_ref=rows(block), dst_ref=rows(block), send_sem=send_sems.at[k], recv_sem=recv_sems.at[k],
                device_id=to, device_id_type=MESH)

        first = [copy(0, me, sib)] + [copy(1 + j, me, (*chip, c)) for j, chip in enumerate(chips)]
        for cp in first:
            cp.start()
        wkv_o[...] = wkv_ref[...].astype(BF16)
        wout_o[...] = wout_ref[...].astype(BF16)
        passed = []
        for j, chip in enumerate(chips):
            copy(1 + j, (*chip, c), me).wait_recv()
            fwd = copy(4 + j, (*chip, c), sib)
            fwd.start()
            passed.append(fwd)
        copy(0, sib, me).wait_recv()
        for j, chip in enumerate(chips):
            copy(4 + j, (*chip, 1 - c), me).wait_recv()
        for cp in first + passed:
            cp.wait_send()

    vmem = pl.BlockSpec(memory_space=pltpu.VMEM)
    return pl.pallas_call(
        body, name="allgather_weights",
        out_shape=(SDS((IN_COLS, D_MODEL), BF16), SDS(w_kv.shape, BF16), SDS(w_out.shape, BF16)),
        in_specs=[vmem, vmem, vmem], out_specs=(vmem, vmem, vmem),
        scratch_shapes=[pltpu.SemaphoreType.DMA((7,)), pltpu.SemaphoreType.DMA((7,))],
        compiler_params=_params(40),
    )(w_in_t, w_kv, w_out)


def _proj_fwd(x2, g_norm, wint, token):
    tm = 512
    sub = 256
    d = DILATIONS[2][0]
    per_ex = SEQ // tm

    def body(x_ref, g_ref, w_ref, _, o_ref, o16_ref):
        xv = x_ref[...]
        h = xv * _rstd(xv) * g_ref[...]
        res = _dot_nt(h, w_ref[...])
        o_ref[...] = res
        r_out = lax.broadcasted_iota(jnp.int32, (sub, sub), 0)
        r_in = lax.broadcasted_iota(jnp.int32, (sub, sub), 1)
        pick = (r_in == d * (r_out % (sub // d)) + r_out // (sub // d)).astype(BF16)
        for part in range(tm // sub):
            grouped = _dot(pick, res[part * sub:(part + 1) * sub, 0:QKV_W]).astype(BF16)
            for rho in range(d):
                o16_ref[0, rho, part * (sub // d):(part + 1) * (sub // d), :] = (
                    grouped[rho * (sub // d):(rho + 1) * (sub // d), :])

    return pl.pallas_call(
        body, name="proj_fwd", grid=(T_LOC // tm,),
        in_specs=[pl.BlockSpec((tm, D_MODEL), lambda i: (i, 0)), pl.BlockSpec((1, D_MODEL), lambda i: (0, 0)),
                  pl.BlockSpec((IN_COLS, D_MODEL), lambda i: (0, 0)), _TOKEN],
        out_specs=(pl.BlockSpec((tm, IN_COLS), lambda i: (i, 0)),
                   pl.BlockSpec((1, d, tm // d, QKV_W), lambda i: (i // per_ex, 0, i % per_ex, 0))),
        out_shape=(SDS((T_LOC, IN_COLS), F32), SDS((B_LOC, d, SEQ // d, QKV_W), BF16)),
        compiler_params=_params(48, ("arbitrary",)),
    )(x2, g_norm, wint, token)


def _memkv_fwd(mem2, g_mem, wkv):
    def body(m_ref, g_ref, w_ref, o_ref):
        mv = m_ref[...]
        o_ref[...] = _dot(mv * _rstd(mv) * g_ref[...], w_ref[...])

    return pl.pallas_call(
        body, name="memkv_fwd", out_shape=SDS((B_LOC * N_MEM, 2 * MEM_W), F32), compiler_params=_params(32),
    )(mem2, g_mem, wkv)


N_BIAS = 7


def _fill_bias_tables(sl_ref, tab):
    for cfg, (d, length) in enumerate(DILATIONS):
        nk = min(length, 2 * CHUNK)
        r = lax.broadcasted_iota(jnp.int32, (CHUNK, nk), 0)
        c = lax.broadcasted_iota(jnp.int32, (CHUNK, nk), 1)
        for var in range(3 if length > nk else 1):
            rel = jnp.abs(r - c + var * RADIUS)
            dist = rel.astype(F32) * float(d)
            for h in range(2):
                slope = sl_ref[0, 0:1, h * HEAD:h * HEAD + 1]
                tab[3 * cfg + var, h * CHUNK:(h + 1) * CHUNK, 0:nk] = jnp.where(rel <= RADIUS, -slope * dist, NEG)


def _attn_blocks(visit, unroll):
    def step(t, carry):
        for cfg, (d, length) in enumerate(DILATIONS):
            nblk = length // CHUNK
            if nblk == 1:
                visit(cfg, 0, t, t, length, t)
                continue
            rho, i = (0, t) if d == 1 else (t // nblk, t % nblk)
            ks = jnp.clip(i * CHUNK - RADIUS, 0, length - 2 * CHUNK)
            visit(cfg, (i * CHUNK - ks) // RADIUS, rho + d * (i * CHUNK), rho + d * ks, 2 * CHUNK, t)
        return carry
    lax.fori_loop(0, 16, step, 0, unroll=unroll)


def _stack_heads(v, left):
    return jnp.concatenate([jnp.where(left, v, 0.0), jnp.where(left, 0.0, v)], axis=0)


def _unstack_heads(v, left):
    return jnp.where(left, v[0:CHUNK], v[CHUNK:2 * CHUNK])


def _rows(start, n, d):
    return pl.ds(start, n) if d == 1 else pl.ds(start, n, stride=d)


def _blk16(col0):
    d, length = DILATIONS[2]
    return pl.BlockSpec((1, d, length, 128), lambda b, hp: (b, 0, 0, col0 // 128 + hp))


def _attn_fwd(proj, qkv16, slopes):
    def body(sl_ref, q_ref, k_ref, v_ref, q16_ref, k16_ref, v16_ref, a_ref, lse_ref, *scr):
        o_c, m_c, l_c, tab = scr[0:3], scr[3:6], scr[6:9], scr[9]
        left = _left_lanes(CHUNK)
        _fill_bias_tables(sl_ref, tab)

        def block(cfg, var, q0, k0, nk, t):
            d = DILATIONS[cfg][0]
            rq, rk = _rows(q0, CHUNK, d), _rows(k0, nk, d)
            if cfg == 2:
                qb, kw, vw = q16_ref[0, t].astype(F32), k16_ref[0, t], v16_ref[0, t]
            else:
                qb, kw, vw = q_ref[rq, :], k_ref[rk, :], v_ref[rk, :]
            qs = _stack_heads(qb * SCALE, left)
            s = _dot_nt(qs, kw) + tab[3 * cfg + var, :, 0:nk]
            m = jnp.max(s, axis=-1, keepdims=True)
            p = jnp.exp(s - m)
            o_c[cfg][rq, :] = _unstack_heads(_dot(p, vw), left)
            m_c[cfg][rq, :] = _unstack_heads(m, left)
            l_c[cfg][rq, :] = _unstack_heads(jnp.sum(p, axis=-1, keepdims=True), left)
        _attn_blocks(block, 8)

        def merge(j, carry):
            rows = pl.ds(pl.multiple_of(j * 256, 256), 256)
            ms = [m_c[i][rows, :] for i in range(3)]
            top = jnp.maximum(jnp.maximum(ms[0], ms[1]), ms[2])
            ws = [jnp.exp(m - top) for m in ms]
            den = l_c[0][rows, :] * ws[0] + l_c[1][rows, :] * ws[1] + l_c[2][rows, :] * ws[2]
            num = o_c[0][rows, :] * ws[0] + o_c[1][rows, :] * ws[1] + o_c[2][rows, :] * ws[2]
            a_ref[rows, :] = num / den
            lse_ref[rows, :] = top + jnp.log(den)
            return carry
        lax.fori_loop(0, SEQ // 256, merge, 0)

    blk = lambda col0: pl.BlockSpec((SEQ, 128), lambda b, hp: (b, col0 // 128 + hp))
    out = pl.BlockSpec((SEQ, 128), lambda b, hp: (b, hp))
    return pl.pallas_call(
        body, name="attn_fwd", grid=(B_LOC, 4),
        in_specs=[pl.BlockSpec((1, 8, 128), lambda b, hp: (hp, 0, 0)), blk(C_QA), blk(C_KA), blk(C_VA),
                  _blk16(C_QA), _blk16(C_KA), _blk16(C_VA)],
        out_specs=(out, out),
        out_shape=(SDS((T_LOC, ATTN_W), F32), SDS((T_LOC, ATTN_W), F32)),
        scratch_shapes=[pltpu.VMEM((SEQ, 128), F32)] * 9 + [pltpu.VMEM((N_BIAS, 2 * CHUNK, 2 * CHUNK), F32)],
        compiler_params=_params(VMEM_NO_STAGING_MIB, ("arbitrary", "arbitrary")),
    )(slopes, proj, proj, proj, qkv16, qkv16, qkv16)


def _chunks_side_by_side(v, pr, tm):
    return jnp.concatenate([v[ch * CHUNK:(ch + 1) * CHUNK, pr * 128:(pr + 1) * 128] for ch in range(tm // CHUNK)], axis=1)


def _first_group_lanes(tm):
    return lax.broadcasted_iota(jnp.int32, (CHUNK, tm), 1) % 128 < HEAD


def _store_chunks(dst_ref, pr, val, tm):
    for ch in range(tm // CHUNK):
        dst_ref[ch * CHUNK:(ch + 1) * CHUNK, pr * 128:(pr + 1) * 128] = val[:, ch * CHUNK:(ch + 1) * CHUNK]


def _sgu_mix(vn, ws_ref, dst_ref, tm):
    first = _first_group_lanes(tm)
    for pr in range(2):
        vp = _chunks_side_by_side(vn, pr, tm)
        _store_chunks(dst_ref, pr, jnp.where(first, _dot(ws_ref[2 * pr], vp), _dot(ws_ref[2 * pr + 1], vp)), tm)


def _mem_head_of_lane(rows):
    return lax.broadcasted_iota(jnp.int32, (rows, MEM_W), 1) // HEAD


def _stack_mem_heads(v, rows):
    head = _mem_head_of_lane(rows)
    return jnp.concatenate([jnp.where(head == h, v, 0.0) for h in range(4)], axis=0)


def _unstack_mem_heads(v, rows):
    head = _mem_head_of_lane(rows)
    out = v[0:rows]
    for h in range(1, 4):
        out = jnp.where(head == h, v[h * rows:(h + 1) * rows], out)
    return out


def _mem_attn_probs(q, kmem, rows):
    qs = _stack_mem_heads(q, rows).astype(BF16)
    s = _dot_nt(qs, kmem) * SCALE
    e = jnp.exp(s - jnp.max(s, axis=-1, keepdims=True))
    return e * (1.0 / jnp.sum(e, axis=-1, keepdims=True)), qs


def _branch_blocks(tm):
    col = lambda w, c0: pl.BlockSpec((tm, w), lambda i: (i, c0 // w))
    return [col(512, C_ZA), col(256, C_UB), col(256, C_VB), col(256, C_ZB), col(256, C_QM), col(256, C_ZM)]


def _branch_fwd(proj, a, kv, w_s, b_exp, g_v):
    tm = 512
    per_ex = SEQ // tm

    def body(za_ref, ub_ref, vb_ref, zb_ref, qm_ref, zm_ref, a_ref, kv_ref, ws_ref, be_ref, gv_ref, o_ref, mix):
        o_ref[:, 0:ATTN_W] = (_silu_and_grad(za_ref[...])[0] * a_ref[...]).astype(BF16)
        gu = _gelu_and_grad(ub_ref[...])[0]
        gv = _gelu_and_grad(vb_ref[...])[0]
        vn = gv * _rstd(gv) * gv_ref[...]
        _sgu_mix(vn.astype(BF16), ws_ref, mix, tm)
        sg = gu * (mix[...] + jnp.concatenate([be_ref[...]] * (tm // CHUNK), axis=0))
        o_ref[:, ATTN_W:ATTN_W + SGU_W] = (_silu_and_grad(zb_ref[...])[0] * sg).astype(BF16)
        p = _mem_attn_probs(qm_ref[...], kv_ref[:, 0:MEM_W], tm)[0]
        mo = _unstack_mem_heads(_dot(p, kv_ref[:, MEM_W:2 * MEM_W]), tm)
        o_ref[:, ATTN_W + SGU_W:D_MODEL] = (_silu_and_grad(zm_ref[...])[0] * mo).astype(BF16)

    full = lambda shape: pl.BlockSpec(shape, lambda i: (0,) * len(shape))
    return pl.pallas_call(
        body, name="branch_fwd", grid=(T_LOC // tm,),
        in_specs=_branch_blocks(tm) + [
            pl.BlockSpec((tm, ATTN_W), lambda i: (i, 0)), pl.BlockSpec((N_MEM, 2 * MEM_W), lambda i: (i // per_ex, 0)),
            full((4, CHUNK, CHUNK)), full((CHUNK, SGU_W)), full((1, SGU_W))],
        out_specs=pl.BlockSpec((tm, D_MODEL), lambda i: (i, 0)),
        out_shape=SDS((T_LOC, D_MODEL), BF16),
        scratch_shapes=[pltpu.VMEM((tm, SGU_W), F32)],
        compiler_params=_params(VMEM_NO_STAGING_MIB, ("arbitrary",)),
    )(proj, proj, proj, proj, proj, proj, a, kv, w_s, b_exp, g_v)


def _outproj_loss(gated, wout, x2, tgt2, g_final):
    tm = 512

    def body(g_ref, w_ref, x_ref, t_ref, gf_ref, dh2_ref, loss_ref, dgf_ref, dwo_ref, dwo_acc):
        @pl.when(pl.program_id(0) == 0)
        def _():
            loss_ref[...] = jnp.zeros_like(loss_ref)
            dgf_ref[...] = jnp.zeros_like(dgf_ref)
            dwo_acc[...] = jnp.zeros_like(dwo_acc)
        gated = g_ref[...]
        h2 = x_ref[...] + _dot(gated, w_ref[...])
        r = _rstd(h2)
        gf = gf_ref[...]
        err = h2 * r * gf - t_ref[...]
        loss_ref[...] += 0.5 * jnp.sum(jnp.mean(err * err, axis=-1, keepdims=True))
        dy = err * (1.0 / D_MODEL)
        dh2 = _rms_bwd(h2, r, gf, dy)
        dh2_ref[...] = dh2
        dgf_ref[...] += jnp.sum(dy * (h2 * r), axis=0, keepdims=True)
        dwo_acc[...] += _dot_tn(gated, dh2)

        @pl.when(pl.program_id(0) == pl.num_programs(0) - 1)
        def _():
            _cast_rows(dwo_acc, dwo_ref, D_MODEL)

    row = pl.BlockSpec((tm, D_MODEL), lambda i: (i, 0))
    vec = pl.BlockSpec((1, D_MODEL), lambda i: (0, 0))
    square = pl.BlockSpec((D_MODEL, D_MODEL), lambda i: (0, 0))
    return pl.pallas_call(
        body, name="outproj_loss", grid=(T_LOC // tm,),
        in_specs=[row, square, row, row, vec],
        out_specs=(row, pl.BlockSpec((8, 128), lambda i: (0, 0)), vec, square),
        out_shape=(SDS((T_LOC, D_MODEL), F32), SDS((8, 128), F32), SDS((1, D_MODEL), F32), SDS((D_MODEL, D_MODEL), BF16)),
        scratch_shapes=[pltpu.VMEM((D_MODEL, D_MODEL), F32)],
        compiler_params=_params(VMEM_NO_STAGING_MIB, ("arbitrary",)),
    )(gated, wout, x2, tgt2, g_final)


def _branch_bwd(dh2, wout, proj, a, kv, w_s, b_exp, g_v):
    tm = 512
    per_ex = SEQ // tm

    def body(dh2_ref, w_ref, za_ref, ub_ref, vb_ref, zb_ref, qm_ref, zm_ref, a_ref, kv_ref, ws_ref,
             be_ref, gv_ref, da_ref, dr_ref, dkv_ref, dws_ref, db_ref, dgv_ref, mix, dvn, dmsum):
        i = pl.program_id(0)

        @pl.when(i == 0)
        def _():
            dws_ref[...] = jnp.zeros_like(dws_ref)
            dgv_ref[...] = jnp.zeros_like(dgv_ref)
            dmsum[...] = jnp.zeros_like(dmsum)

        @pl.when(i % per_ex == 0)
        def _():
            dkv_ref[...] = jnp.zeros_like(dkv_ref)

        dg = _dot_nt(dh2_ref[...], w_ref[...])

        sa, dsa = _silu_and_grad(za_ref[...])
        dga = dg[:, 0:ATTN_W]
        da_ref[...] = dga * sa
        dr_ref[:, 0:512] = (dga * a_ref[...] * dsa).astype(BF16)

        ub, vb = ub_ref[...], vb_ref[...]
        gu, dgu = _gelu_and_grad(ub)
        gv, dgv = _gelu_and_grad(vb)
        rv = _rstd(gv)
        gain = gv_ref[...]
        vn = (gv * rv * gain).astype(BF16)
        _sgu_mix(vn, ws_ref, mix, tm)
        mixed = mix[...] + jnp.concatenate([be_ref[...]] * (tm // CHUNK), axis=0)
        sb, dsb = _silu_and_grad(zb_ref[...])
        dgb = dg[:, ATTN_W:ATTN_W + SGU_W]
        dsg = dgb * sb
        dr_ref[:, 512:768] = (dsg * mixed * dgu).astype(BF16)
        dr_ref[:, 1024:1280] = (dgb * (gu * mixed) * dsb).astype(BF16)
        dmix = dsg * gu
        for ch in range(tm // CHUNK):
            dmsum[...] += dmix[ch * CHUNK:(ch + 1) * CHUNK, :]
        first = _first_group_lanes(tm)
        for pr in range(2):
            dmp, vp = _chunks_side_by_side(dmix, pr, tm), _chunks_side_by_side(vn, pr, tm)
            dws_ref[2 * pr] += _dot_nt(jnp.where(first, dmp, 0.0), vp)
            dws_ref[2 * pr + 1] += _dot_nt(jnp.where(first, 0.0, dmp), vp)
            _store_chunks(dvn, pr, jnp.where(first, _dot_tn(ws_ref[2 * pr], dmp), _dot_tn(ws_ref[2 * pr + 1], dmp)), tm)
        dvn_v = dvn[...]
        dgv_ref[...] += jnp.sum(dvn_v * (gv * rv), axis=0, keepdims=True)
        dr_ref[:, 768:1024] = (_rms_bwd(gv, rv, gain, dvn_v) * dgv).astype(BF16)

        szm, dszm = _silu_and_grad(zm_ref[...])
        dgm = dg[:, ATTN_W + SGU_W:D_MODEL]
        kmem, vmem_ = kv_ref[:, 0:MEM_W].astype(BF16), kv_ref[:, MEM_W:2 * MEM_W].astype(BF16)
        p, qs = _mem_attn_probs(qm_ref[...], kmem, tm)
        dmos = _stack_mem_heads(dgm * szm, tm).astype(BF16)
        dr_ref[:, 1536:1792] = (dgm * _unstack_mem_heads(_dot(p, vmem_), tm) * dszm).astype(BF16)
        dp = _dot_nt(dmos, vmem_)
        ds = (p * (dp - jnp.sum(p * dp, axis=-1, keepdims=True)) * SCALE).astype(BF16)
        dr_ref[:, 1280:1536] = _unstack_mem_heads(_dot(ds, kmem), tm).astype(BF16)
        dkv_ref[:, 0:MEM_W] += _dot_tn(ds, qs)
        dkv_ref[:, MEM_W:2 * MEM_W] += _dot_tn(p, dmos)

        @pl.when(i == pl.num_programs(0) - 1)
        def _():
            tot = dmsum[...]
            hi = tot.astype(BF16)
            lo = (tot - hi.astype(F32)).astype(BF16)
            grp = (lax.broadcasted_iota(jnp.int32, (SGU_W, 128), 0) // HEAD
                   == lax.broadcasted_iota(jnp.int32, (SGU_W, 128), 1)).astype(BF16)
            db_ref[...] = (_dot(hi, grp) + _dot(lo, grp)).T[0:4, :]

    full = lambda shape: pl.BlockSpec(shape, lambda i: (0,) * len(shape))
    row = lambda w: pl.BlockSpec((tm, w), lambda i: (i, 0))
    return pl.pallas_call(
        body, name="branch_bwd", grid=(T_LOC // tm,),
        in_specs=[row(D_MODEL), full((D_MODEL, D_MODEL))] + _branch_blocks(tm) + [
            row(ATTN_W), pl.BlockSpec((N_MEM, 2 * MEM_W), lambda i: (i // per_ex, 0)),
            full((4, CHUNK, CHUNK)), full((CHUNK, SGU_W)), full((1, SGU_W))],
        out_specs=(row(ATTN_W), row(REST_W), pl.BlockSpec((N_MEM, 2 * MEM_W), lambda i: (i // per_ex, 0)),
                   full((4, CHUNK, CHUNK)), full((4, CHUNK)), full((1, SGU_W))),
        out_shape=(SDS((T_LOC, ATTN_W), F32), SDS((T_LOC, REST_W), BF16), SDS((B_LOC * N_MEM, 2 * MEM_W), F32),
                   SDS((4, CHUNK, CHUNK), F32), SDS((4, CHUNK), F32), SDS((1, SGU_W), F32)),
        scratch_shapes=[pltpu.VMEM((tm, SGU_W), F32), pltpu.VMEM((tm, SGU_W), F32), pltpu.VMEM((CHUNK, SGU_W), F32)],
        compiler_params=_params(56, ("arbitrary",)),
    )(dh2, wout, proj, proj, proj, proj, proj, proj, a, kv, w_s, b_exp, g_v)


def _attn_bwd(proj, qkv16, slopes, da, a, lse, token):
    def body(sl_ref, q_ref, k_ref, v_ref, q16_ref, k16_ref, v16_ref, da_ref, a_ref, lse_ref, _,
             dq_ref, dk_ref, dv_ref, *scr):
        dq_s, dk_s, dv_s, tab = scr[0:3], scr[3:6], scr[6:9], scr[9]
        lse_h, delta_h = scr[10:12], scr[12:14]
        p_all, ds_all = scr[14], scr[15]
        left = _left_lanes(CHUNK)
        _fill_bias_tables(sl_ref, tab)

        def prep(j, carry):
            rows = pl.ds(pl.multiple_of(j * 256, 256), 256)
            l256 = _left_lanes(256)
            prod = da_ref[rows, :] * a_ref[rows, :]
            delta_h[0][rows, :] = jnp.broadcast_to(jnp.sum(jnp.where(l256, prod, 0.0), axis=-1, keepdims=True), (256, 128))
            delta_h[1][rows, :] = jnp.broadcast_to(jnp.sum(jnp.where(l256, 0.0, prod), axis=-1, keepdims=True), (256, 128))
            pair = lse_ref[rows, :]
            other = pltpu.roll(pair, HEAD, axis=1)
            lse_h[0][rows, :] = jnp.where(l256, pair, other)
            lse_h[1][rows, :] = jnp.where(l256, other, pair)
            zero = jnp.zeros((256, 128), F32)
            for cfg in range(3):
                dk_s[cfg][rows, :] = zero
                dv_s[cfg][rows, :] = zero
            return carry
        lax.fori_loop(0, SEQ // 256, prep, 0)

        def per_row(halves, rq, nk):
            v = jnp.concatenate([halves[0][rq, :], halves[1][rq, :]], axis=0)
            return v if nk == 128 else jnp.concatenate([v, v], axis=1)

        def qkv(cfg, rq, rk, t):
            if cfg == 2:
                return q16_ref[0, t].astype(F32), k16_ref[0, t], v16_ref[0, t]
            return q_ref[rq, :], k_ref[rk, :], v_ref[rk, :]

        def probs(cfg, var, q0, k0, nk, t):
            d = DILATIONS[cfg][0]
            rq, rk = _rows(q0, CHUNK, d), _rows(k0, nk, d)
            qb, kw, vw = qkv(cfg, rq, rk, t)
            qs = _stack_heads(qb * SCALE, left)
            das = _stack_heads(da_ref[rq, :], left)
            s = _dot_nt(qs, kw) + tab[3 * cfg + var, :, 0:nk]
            p = jnp.exp(s - per_row(lse_h, rq, nk))
            p_all[16 * cfg + t, :, 0:nk] = p.astype(BF16)
            ds_all[16 * cfg + t, :, 0:nk] = (p * (_dot_nt(das, vw) - per_row(delta_h, rq, nk))).astype(BF16)
        _attn_blocks(probs, 4)

        def grads(cfg, var, q0, k0, nk, t):
            d = DILATIONS[cfg][0]
            rq, rk = _rows(q0, CHUNK, d), _rows(k0, nk, d)
            qb, kw, _ = qkv(cfg, rq, rk, t)
            qs = _stack_heads(qb * SCALE, left).astype(BF16)
            das = _stack_heads(da_ref[rq, :], left).astype(BF16)
            p, ds = p_all[16 * cfg + t, :, 0:nk], ds_all[16 * cfg + t, :, 0:nk]
            dq_s[cfg][rq, :] = _unstack_heads(_dot(ds, kw), left) * SCALE
            dk_s[cfg][rk, :] += _dot_tn(ds, qs)
            dv_s[cfg][rk, :] += _dot_tn(p, das)
        _attn_blocks(grads, 4)

        def flush(j, carry):
            rows = pl.ds(pl.multiple_of(j * 256, 256), 256)
            for acc, dst in ((dq_s, dq_ref), (dk_s, dk_ref), (dv_s, dv_ref)):
                dst[rows, :] = (acc[0][rows, :] + acc[1][rows, :] + acc[2][rows, :]).astype(BF16)
            return carry
        lax.fori_loop(0, SEQ // 256, flush, 0)

    blk = lambda col0: pl.BlockSpec((SEQ, 128), lambda b, hp: (b, col0 // 128 + hp))
    own = pl.BlockSpec((SEQ, 128), lambda b, hp: (b, hp))
    return pl.pallas_call(
        body, name="attn_bwd", grid=(B_LOC, 4),
        in_specs=[pl.BlockSpec((1, 8, 128), lambda b, hp: (hp, 0, 0)), blk(C_QA), blk(C_KA), blk(C_VA),
                  _blk16(C_QA), _blk16(C_KA), _blk16(C_VA), own, own, own, _TOKEN],
        out_specs=(own, own, own),
        out_shape=(SDS((T_LOC, ATTN_W), BF16),) * 3,
        scratch_shapes=[pltpu.VMEM((SEQ, 128), F32)] * 9 + [pltpu.VMEM((N_BIAS, 2 * CHUNK, 2 * CHUNK), F32)]
        + [pltpu.VMEM((SEQ, 128), F32)] * 4 + [pltpu.VMEM((48, 2 * CHUNK, 2 * CHUNK), BF16)] * 2,
        compiler_params=_params(52, ("arbitrary", "arbitrary")),
    )(slopes, proj, proj, proj, qkv16, qkv16, qkv16, da, a, lse, token)


def _dproj_specs(tm):
    third = pl.BlockSpec((tm, ATTN_W), lambda i: (i, 0))
    return [third, third, third, pl.BlockSpec((tm, REST_W), lambda i: (i, 0))]


def _dx(dq, dk, dv, dr, wint, x2, dh2, g_norm, token):
    tm = 512

    def body(dq_ref, dk_ref, dv_ref, dr_ref, w_ref, x_ref, dh2_ref, g_ref, _, gx_ref, dgn_ref):
        @pl.when(pl.program_id(0) == 0)
        def _():
            dgn_ref[...] = jnp.zeros_like(dgn_ref)
        dh = (_dot(dq_ref[...], w_ref[C_QA:C_KA, :]) + _dot(dk_ref[...], w_ref[C_KA:C_VA, :])
              + _dot(dv_ref[...], w_ref[C_VA:C_ZA, :]) + _dot(dr_ref[...], w_ref[C_ZA:IN_COLS, :]))
        xv = x_ref[...]
        r = _rstd(xv)
        gx_ref[...] = dh2_ref[...] + _rms_bwd(xv, r, g_ref[...], dh)
        dgn_ref[...] += jnp.sum(dh * (xv * r), axis=0, keepdims=True)

    row = pl.BlockSpec((tm, D_MODEL), lambda i: (i, 0))
    vec = pl.BlockSpec((1, D_MODEL), lambda i: (0, 0))
    return pl.pallas_call(
        body, name="dx", grid=(T_LOC // tm,),
        in_specs=_dproj_specs(tm) + [pl.BlockSpec((IN_COLS, D_MODEL), lambda i: (0, 0)), row, row, vec, _TOKEN],
        out_specs=(row, vec),
        out_shape=(SDS((T_LOC, D_MODEL), F32), SDS((1, D_MODEL), F32)),
        compiler_params=_params(48, ("arbitrary",)),
    )(dq, dk, dv, dr, wint, x2, dh2, g_norm, token)


def _dwin(dq, dk, dv, dr, x2, g_norm, half, token):
    tm = 1024
    width = D_MODEL // 2
    cols = slice(half * width, (half + 1) * width)

    def body(dq_ref, dk_ref, dv_ref, dr_ref, x_ref, g_ref, _, o_ref, acc):
        @pl.when(pl.program_id(0) == 0)
        def _():
            acc[...] = jnp.zeros_like(acc)
        xv = x_ref[...]
        h = (xv[:, cols] * _rstd(xv) * g_ref[:, cols]).astype(BF16)
        acc[C_QA:C_KA, :] += _dot_tn(dq_ref[...], h)
        acc[C_KA:C_VA, :] += _dot_tn(dk_ref[...], h)
        acc[C_VA:C_ZA, :] += _dot_tn(dv_ref[...], h)
        acc[C_ZA:IN_COLS, :] += _dot_tn(dr_ref[...], h)

        @pl.when(pl.program_id(0) == pl.num_programs(0) - 1)
        def _():
            _cast_rows(acc, o_ref, IN_COLS)

    return pl.pallas_call(
        body, name="dwin%d" % half, grid=(T_LOC // tm,),
        in_specs=_dproj_specs(tm) + [pl.BlockSpec((tm, D_MODEL), lambda i: (i, 0)),
                                     pl.BlockSpec((1, D_MODEL), lambda i: (0, 0)), _TOKEN],
        out_specs=pl.BlockSpec((IN_COLS, width), lambda i: (0, 0)),
        out_shape=SDS((IN_COLS, width), BF16),
        scratch_shapes=[pltpu.VMEM((IN_COLS, width), F32)],
        compiler_params=_params(48, ("arbitrary",)),
    )(dq, dk, dv, dr, x2, g_norm, token)


def _memkv_bwd(dkv, mem2, g_mem, wkv):
    def body(dkv_ref, m_ref, g_ref, w_ref, dw_ref, dg_ref):
        mv = m_ref[...]
        r = _rstd(mv)
        dkv_v = dkv_ref[...].astype(BF16)
        dw_ref[...] = _dot_tn(mv * r * g_ref[...], dkv_v).astype(BF16)
        dg_ref[...] = jnp.sum(_dot_nt(dkv_v, w_ref[...]) * (mv * r), axis=0, keepdims=True)

    return pl.pallas_call(
        body, name="memkv_bwd", out_shape=(SDS((D_MODEL, 2 * MEM_W), BF16), SDS((1, D_MODEL), F32)),
        compiler_params=_params(32),
    )(dkv, mem2, g_mem, wkv)


_HBM = pl.BlockSpec(memory_space=pltpu.HBM)
_SEM = pl.BlockSpec(memory_space=pltpu.SEMAPHORE)
_SIDE_EFFECT = pltpu.SideEffectType.DATAFLOW_SIDE_EFFECTING


def _exchange_copies(src_refs, land_refs, scatter, send_sems, recv_sems):
    pos = _mesh_pos()
    copies = []
    for a, (src, land) in enumerate(zip(src_refs, land_refs)):
        n = land.shape[1]
        for k in range(1, N_DEV):
            peer = _peer(pos, k)
            piece = src.at[pl.ds(pl.multiple_of(_flat(peer) * n, 16), n), :] if scatter[a] else src
            copies.append(pltpu.make_async_remote_copy(
                src_ref=piece, dst_ref=land.at[_flat(pos)],
                send_sem=send_sems.at[7 * a + k - 1], recv_sem=recv_sems.at[7 * a + k - 1],
                device_id=peer, device_id_type=MESH))
    return copies


def _own_copies(src_refs, land_refs, scatter, send_sems):
    n = len(src_refs)
    me = _flat(_mesh_pos())
    copies = []
    for a, (src, land) in enumerate(zip(src_refs, land_refs)):
        rows = land.shape[1]
        piece = src.at[pl.ds(pl.multiple_of(me * rows, 16), rows), :] if scatter[a] else src
        copies.append(pltpu.make_async_copy(piece, land.at[me], send_sems.at[7 * n + a]))
    return copies


def _exchange_start(name, srcs, scatter):
    n = len(srcs)

    def body(*refs):
        for cp in _exchange_copies(refs[0:n], refs[n:2 * n], scatter, refs[2 * n], refs[2 * n + 1]):
            cp.start()
        for cp in _own_copies(refs[0:n], refs[n:2 * n], scatter, refs[2 * n]):
            cp.start()
        refs[-1][...] = jnp.zeros_like(refs[-1])

    lands = [lax.empty((N_DEV,) + (t.shape[0] // N_DEV if sc else t.shape[0],) + t.shape[1:], t.dtype)
             for t, sc in zip(srcs, scatter)]
    ops = [pltpu.with_memory_space_constraint(t, pltpu.HBM) for t in (*srcs, *lands)]
    out = pl.pallas_call(
        body, name=name,
        out_shape=(pltpu.SemaphoreType.DMA((8 * n,)), pltpu.SemaphoreType.DMA((7 * n,)),
                   *[pltpu.HBM(t.shape, t.dtype) for t in ops], SDS((8, 128), F32)),
        in_specs=[_HBM] * (2 * n),
        out_specs=(_SEM, _SEM, *[_HBM] * (2 * n), pl.BlockSpec(memory_space=pltpu.VMEM)),
        input_output_aliases={i: 2 + i for i in range(2 * n)},
        compiler_params=pltpu.CompilerParams(has_side_effects=_SIDE_EFFECT),
    )(*ops)
    return out[0], out[1], out[2:2 + n], out[2 + n:2 + 2 * n], out[-1]


def _exchange_wait(name, started, scatter, after):
    send_sems, recv_sems, srcs, lands, _ = started
    n = len(srcs)

    def body(*refs):
        for cp in _exchange_copies(refs[0:n], refs[n:2 * n], scatter, refs[2 * n], refs[2 * n + 1]):
            cp.wait_send()
            cp.wait_recv()
        for cp in _own_copies(refs[0:n], refs[n:2 * n], scatter, refs[2 * n]):
            cp.wait()

    out = pl.pallas_call(
        body, name=name,
        out_shape=tuple(pltpu.HBM(t.shape, t.dtype) for t in (*srcs, *lands)),
        in_specs=[_HBM] * (2 * n) + [_SEM, _SEM, pl.BlockSpec(memory_space=pl.ANY)],
        out_specs=(_HBM,) * (2 * n),
        input_output_aliases={i: i for i in range(2 * n)},
        compiler_params=pltpu.CompilerParams(has_side_effects=_SIDE_EFFECT),
    )(*srcs, *lands, send_sems, recv_sems, after)
    return out[n:]


def _adamw(w, g, m, v):
    m = ADAM_B1 * m + (1.0 - ADAM_B1) * g
    v = ADAM_B2 * v + (1.0 - ADAM_B2) * (g * g)
    m_hat = m / (1.0 - ADAM_B1 ** ADAM_STEP)
    v_hat = v / (1.0 - ADAM_B2 ** ADAM_STEP)
    return -ADAM_LR * (m_hat / (jnp.sqrt(v_hat) + ADAM_EPS) + ADAM_WD * w), m, v


def _adam_slots(name, pieces, w, m, v):
    rows, cols = w.shape
    starts = [sum(p.shape[2] for p in pieces[:i]) for i in range(len(pieces) + 1)]
    assert starts[-1] == cols and all(p.shape[1] == rows for p in pieces)

    def body(*refs):
        s_refs, (w_ref, m_ref, v_ref, g_o, d_o, m_o, v_o, acc) = refs[:len(pieces)], refs[len(pieces):]
        s = pl.program_id(0)

        @pl.when(s == 0)
        def _():
            for i, s_ref in enumerate(s_refs):
                acc[:, starts[i]:starts[i + 1]] = s_ref[0].astype(F32)

        @pl.when(s > 0)
        def _():
            for i, s_ref in enumerate(s_refs):
                acc[:, starts[i]:starts[i + 1]] += s_ref[0].astype(F32)

        @pl.when(s == N_DEV - 1)
        def _():
            g = acc[...]
            g_o[...] = g
            d_o[...], m_o[...], v_o[...] = _adamw(w_ref[...], g, m_ref[...], v_ref[...])

    full = pl.BlockSpec((rows, cols), lambda s: (0, 0))
    return pl.pallas_call(
        body, name="adam_" + name, grid=(N_DEV,),
        in_specs=[pl.BlockSpec((1, rows, p.shape[2]), lambda s: (s, 0, 0)) for p in pieces] + [full, full, full],
        out_specs=(full,) * 4, out_shape=(SDS((rows, cols), F32),) * 4,
        scratch_shapes=[pltpu.VMEM((rows, cols), F32)],
        compiler_params=_params(40, ("arbitrary",)),
    )(*pieces, w, m, v)


def _adam_small(ws, gs, ms, vs, loss_slots):
    n = len(ws)

    def total(ref, like):
        if len(ref.shape) == len(like.shape):
            return ref[...]
        acc = ref[0]
        for s in range(1, N_DEV):
            acc = acc + ref[s]
        return acc

    def body(*refs):
        w_r, g_r, m_r, v_r = refs[0:n], refs[n:2 * n], refs[2 * n:3 * n], refs[3 * n:4 * n]
        loss_r, outs = refs[4 * n], refs[4 * n + 1:]
        for a in range(n):
            g = total(g_r[a], w_r[a])
            outs[a][...] = g
            outs[n + 1 + 3 * a][...], outs[n + 2 + 3 * a][...], outs[n + 3 + 3 * a][...] = _adamw(
                w_r[a][...], g, m_r[a][...], v_r[a][...])
        outs[n][...] = total(loss_r, outs[n])

    out = pl.pallas_call(
        body, name="adam_small",
        out_shape=tuple(SDS(w.shape, F32) for w in ws) + (SDS(loss_slots.shape[1:], F32),)
        + tuple(SDS(w.shape, F32) for w in ws for _ in range(3)),
        compiler_params=_params(16),
    )(*ws, *gs, *ms, *vs, loss_slots)
    return out[0:n], out[n], out[n + 1:]


def kernel(x, mem, g_norm, w_in, w_sgu_spatial, b_sgu_spatial, g_sgu_v, g_mem, w_mem_kv, w_out, g_final, loss_target, m_g_norm, m_w_in, m_w_sgu_spatial, m_b_sgu_spatial, m_g_sgu_v, m_g_mem, m_w_mem_kv, m_w_out, m_g_final, v_g_norm, v_w_in, v_w_sgu_spatial, v_b_sgu_spatial, v_g_sgu_v, v_g_mem, v_w_mem_kv, v_w_out, v_g_final):
    x2 = x.reshape(T_LOC, D_MODEL)
    tgt2 = loss_target.reshape(T_LOC, D_MODEL)
    mem2 = mem.reshape(B_LOC * N_MEM, D_MODEL)
    w_s = w_sgu_spatial[0]
    b_exp = jnp.repeat(b_sgu_spatial[0].T, HEAD, axis=1)
    slope = jnp.power(2.0, -8.0 * (jnp.arange(8, dtype=F32) + 1.0) / 8)
    slopes = jnp.broadcast_to(jnp.repeat(slope.reshape(4, 2), HEAD, axis=1)[:, None, :], (4, 8, 128))

    tr = lambda t: jnp.transpose(t[0])

    wint, wkv_own, wout_own = _allgather_weights(tr(w_in), w_mem_kv[0], w_out[0])
    started0 = _exchange_start("exchange0_start", [wkv_own, wout_own], [False, False])
    proj, qkv16 = _proj_fwd(x2, g_norm, wint, started0[4])
    wkv, wout = _exchange_wait("exchange0_wait", started0, [False, False], proj)
    wkv, wout = wkv.reshape(D_MODEL, 2 * MEM_W), wout.reshape(D_MODEL, D_MODEL)
    kv = _memkv_fwd(mem2, g_mem, wkv)
    a, lse = _attn_fwd(proj, qkv16, slopes)
    gated = _branch_fwd(proj, a, kv, w_s, b_exp, g_sgu_v)
    dh2, loss8, dgf, dwout = _outproj_loss(gated, wout, x2, tgt2, g_final.reshape(1, D_MODEL))

    da, dr, dkv, dws, dbs, dgv = _branch_bwd(dh2, wout, proj, a, kv, w_s, b_exp, g_sgu_v)
    dwkv, dgm = _memkv_bwd(dkv, mem2, g_mem, wkv)

    early = [dws.reshape(4 * CHUNK, CHUNK), dbs, dgv, dgm, dgf, loss8]
    scatter1 = [True, True] + [False] * len(early)
    started1 = _exchange_start("exchange1_start", [dwkv, dwout] + early, scatter1)
    dq, dk, dv = _attn_bwd(proj, qkv16, slopes, da, a, lse, started1[4])
    s_wkv, s_wout, s_ws, s_bs, s_gv, s_gm, s_gf, s_loss = _exchange_wait("exchange1_wait", started1, scatter1, dq)

    dwint0 = _dwin(dq, dk, dv, dr, x2, g_norm, 0, started1[4])
    started2 = _exchange_start("exchange2_start", [dwint0], [True])
    dwint1 = _dwin(dq, dk, dv, dr, x2, g_norm, 1, started2[4])
    started3 = _exchange_start("exchange3_start", [dwint1], [True])
    grad_x, dgn = _dx(dq, dk, dv, dr, wint, x2, dh2, g_norm, started3[4])
    started4 = _exchange_start("exchange4_start", [dgn], [False])
    s_win0, = _exchange_wait("exchange2_wait", started2, [True], started4[4])
    s_win1, = _exchange_wait("exchange3_wait", started3, [True], started4[4])

    g_win, d_win, m_win, v_win = _adam_slots("w_in", [s_win0, s_win1], tr(w_in), tr(m_w_in), tr(v_w_in))
    s_gn, = _exchange_wait("exchange4_wait", started4, [False], g_win)
    g_win, d_win, m_win, v_win = map(jnp.transpose, (g_win, d_win, m_win, v_win))
    g_wkv, d_wkv, m_wkv, v_wkv = _adam_slots("w_mem_kv", [s_wkv], w_mem_kv[0], m_w_mem_kv[0], v_w_mem_kv[0])
    g_wout, d_wout, m_wout, v_wout = _adam_slots("w_out", [s_wout], w_out[0], m_w_out[0], v_w_out[0])

    small_shapes = [(1, D_MODEL), (4 * CHUNK, CHUNK), (4, CHUNK), (1, SGU_W), (1, D_MODEL), (1, D_MODEL)]
    pack = lambda arrs: [t.reshape(s) for t, s in zip(arrs, small_shapes)]
    g_small, loss_sum, upd = _adam_small(
        pack([g_norm, w_sgu_spatial, b_sgu_spatial, g_sgu_v, g_mem, g_final]),
        [s_gn, s_ws, s_bs, s_gv, s_gm, s_gf],
        pack([m_g_norm, m_w_sgu_spatial, m_b_sgu_spatial, m_g_sgu_v, m_g_mem, m_g_final]),
        pack([v_g_norm, v_w_sgu_spatial, v_b_sgu_spatial, v_g_sgu_v, v_g_mem, v_g_final]), s_loss)
    out_shapes = [g_norm.shape, w_sgu_spatial.shape, b_sgu_spatial.shape, g_sgu_v.shape, g_mem.shape, g_final.shape]
    unpack = lambda arrs: [t.reshape(s) for t, s in zip(arrs, out_shapes)]
    gs = unpack(g_small)
    ds, nms, nvs = unpack(upd[0::3]), unpack(upd[1::3]), unpack(upd[2::3])

    loss = loss_sum[0, 0]

    def assemble(small, win, wkv_, wout_):
        return [small[0], win[None], small[1], small[2], small[3], small[4], wkv_[None], wout_[None], small[5]]

    return (loss, grad_x.reshape(x.shape),
            *assemble(gs, g_win, g_wkv, g_wout), *assemble(ds, d_win, d_wkv, d_wout),
            *assemble(nms, m_win, m_wkv, m_wout), *assemble(nvs, v_win, v_wkv, v_wout))
```

```python
import jax
import jax.numpy as jnp
from jax import lax
from jax.experimental import pallas as pl
from jax.experimental.pallas import tpu as pltpu

F32 = jnp.float32
BF16 = jnp.bfloat16
SDS = jax.ShapeDtypeStruct
MESH = pl.DeviceIdType.MESH

N_DEV = 8
D_MODEL = 1024
SEQ = 2048
B_LOC = 2
T_LOC = B_LOC * SEQ
N_MEM = 256
HEAD = 64
ATTN_W = 512
SGU_W = 256
MEM_W = 256
IN_COLS = 3328
W_IN_SHARD = IN_COLS // N_DEV
CHUNK = 128
DILATIONS = ((1, 2048), (4, 512), (16, 128))
RADIUS = 64
EPS = 1e-6
NEG = -1e30
SCALE = HEAD ** -0.5
C_QA, C_KA, C_VA, C_ZA, C_UB, C_VB, C_ZB, C_QM, C_ZM = 0, 512, 1024, 1536, 2048, 2304, 2560, 2816, 3072
QKV_W = 1536
REST_W = IN_COLS - QKV_W

ADAM_LR, ADAM_B1, ADAM_B2, ADAM_EPS, ADAM_WD, ADAM_STEP = 0.001, 0.9, 0.999, 1e-08, 0.01, 10

V7X_VMEM_MIB = 64
VMEM_NO_STAGING_MIB = V7X_VMEM_MIB - 6


def _params(vmem_mib, sem=None):
    assert vmem_mib < V7X_VMEM_MIB
    return pltpu.CompilerParams(vmem_limit_bytes=vmem_mib << 20, dimension_semantics=sem)


_TOKEN = pl.BlockSpec(memory_space=pl.ANY)


def _dot(a, b):
    return jnp.dot(a.astype(BF16), b.astype(BF16), preferred_element_type=F32)


def _dot_nt(a, b):
    return lax.dot_general(a.astype(BF16), b.astype(BF16), (((1,), (1,)), ((), ())), preferred_element_type=F32)


def _dot_tn(a, b):
    return lax.dot_general(a.astype(BF16), b.astype(BF16), (((0,), (0,)), ((), ())), preferred_element_type=F32)


def _rstd(v):
    return lax.rsqrt(jnp.mean(v * v, axis=-1, keepdims=True) + EPS)


def _rms_bwd(v, r, g, dy):
    gdy = g * dy
    return r * gdy - v * (r * r * r * jnp.mean(gdy * v, axis=-1, keepdims=True))


def _sigmoid(z):
    return 1.0 / (1.0 + jnp.exp(-z))


def _silu_and_grad(z):
    s = _sigmoid(z)
    return z * s, s * (1.0 + z * (1.0 - s))


_G_C = 0.7978845608028654
_G_K = 0.044715


def _gelu_and_grad(v):
    t = jnp.tanh(_G_C * (v + _G_K * (v * v * v)))
    cdf = 0.5 * (1.0 + t)
    return v * cdf, cdf + 0.5 * v * (1.0 - t * t) * (_G_C * (1.0 + 3.0 * _G_K * v * v))


def _cast_rows(src_ref, dst_ref, rows, step=256):
    def one(i, carry):
        r = pl.ds(pl.multiple_of(i * step, step), step)
        dst_ref[r, :] = src_ref[r, :].astype(dst_ref.dtype)
        return carry
    lax.fori_loop(0, rows // step, one, 0)


def _left_lanes(rows):
    return lax.broadcasted_iota(jnp.int32, (rows, 128), 1) < HEAD


def _mesh_pos():
    return lax.axis_index("x"), lax.axis_index("y"), lax.axis_index("c")


def _peer(pos, k):
    x, y, c = pos
    return (1 - x if k & 4 else x, 1 - y if k & 2 else y, 1 - c if k & 1 else c)


def _flat(pos):
    return 4 * pos[0] + 2 * pos[1] + pos[2]


def _allgather_weights(w_in_t, w_kv, w_out):
    def body(win_ref, wkv_ref, wout_ref, wint_o, wkv_o, wout_o, send_sems, recv_sems):
        x, y, c = _mesh_pos()
        me, sib = (x, y, c), (x, y, 1 - c)
        chips = [(1 - x, y), (x, 1 - y), (1 - x, 1 - y)]

        def rows(p):
            return wint_o.at[pl.ds(pl.multiple_of(_flat(p) * W_IN_SHARD, 16), W_IN_SHARD), :]

        rows(me)[...] = win_ref[...].astype(BF16)

        def copy(k, block, to):
            return pltpu.make_async_remote_copy(
                src_ref=rows(block), dst_ref=rows(block), send_sem=send_sems.at[k], recv_sem=recv_sems.at[k],
                device_id=to, device_id_type=MESH)

        first = [copy(0, me, sib)] + [copy(1 + j, me, (*chip, c)) for j, chip in enumerate(chips)]
        for cp in first:
            cp.start()
        wkv_o[...] = wkv_ref[...].astype(BF16)
        wout_o[...] = wout_ref[...].astype(BF16)
        passed = []
        for j, chip in enumerate(chips):
            copy(1 + j, (*chip, c), me).wait_recv()
            fwd = copy(4 + j, (*chip, c), sib)
            fwd.start()
            passed.append(fwd)
        copy(0, sib, me).wait_recv()
        for j, chip in enumerate(chips):
            copy(4 + j, (*chip, 1 - c), me).wait_recv()
        for cp in first + passed:
            cp.wait_send()

    vmem = pl.BlockSpec(memory_space=pltpu.VMEM)
    return pl.pallas_call(
        body, name="allgather_weights",
        out_shape=(SDS((IN_COLS, D_MODEL), BF16), SDS(w_kv.shape, BF16), SDS(w_out.shape, BF16)),
        in_specs=[vmem, vmem, vmem], out_specs=(vmem, vmem, vmem),
        scratch_shapes=[pltpu.SemaphoreType.DMA((7,)), pltpu.SemaphoreType.DMA((7,))],
        compiler_params=_params(40),
    )(w_in_t, w_kv, w_out)


def _proj_fwd(x2, g_norm, wint, token):
    tm = 512
    sub = 256
    d = DILATIONS[2][0]
    per_ex = SEQ // tm

    def body(x_ref, g_ref, w_ref, _, o_ref, o16_ref):
        xv = x_ref[...]
        h = xv * _rstd(xv) * g_ref[...]
        res = _dot_nt(h, w_ref[...])
        o_ref[...] = res
        r_out = lax.broadcasted_iota(jnp.int32, (sub, sub), 0)
        r_in = lax.broadcasted_iota(jnp.int32, (sub, sub), 1)
        pick = (r_in == d * (r_out % (sub // d)) + r_out // (sub // d)).astype(BF16)
        for part in range(tm // sub):
            grouped = _dot(pick, res[part * sub:(part + 1) * sub, 0:QKV_W]).astype(BF16)
            for rho in range(d):
                o16_ref[0, rho, part * (sub // d):(part + 1) * (sub // d), :] = (
                    grouped[rho * (sub // d):(rho + 1) * (sub // d), :])

    return pl.pallas_call(
        body, name="proj_fwd", grid=(T_LOC // tm,),
        in_specs=[pl.BlockSpec((tm, D_MODEL), lambda i: (i, 0)), pl.BlockSpec((1, D_MODEL), lambda i: (0, 0)),
                  pl.BlockSpec((IN_COLS, D_MODEL), lambda i: (0, 0)), _TOKEN],
        out_specs=(pl.BlockSpec((tm, IN_COLS), lambda i: (i, 0)),
                   pl.BlockSpec((1, d, tm // d, QKV_W), lambda i: (i // per_ex, 0, i % per_ex, 0))),
        out_shape=(SDS((T_LOC, IN_COLS), F32), SDS((B_LOC, d, SEQ // d, QKV_W), BF16)),
        compiler_params=_params(48, ("arbitrary",)),
    )(x2, g_norm, wint, token)


def _memkv_fwd(mem2, g_mem, wkv):
    def body(m_ref, g_ref, w_ref, o_ref):
        mv = m_ref[...]
        o_ref[...] = _dot(mv * _rstd(mv) * g_ref[...], w_ref[...])

    return pl.pallas_call(
        body, name="memkv_fwd", out_shape=SDS((B_LOC * N_MEM, 2 * MEM_W), F32), compiler_params=_params(32),
    )(mem2, g_mem, wkv)


N_BIAS = 7


def _fill_bias_tables(sl_ref, tab):
    for cfg, (d, length) in enumerate(DILATIONS):
        nk = min(length, 2 * CHUNK)
        r = lax.broadcasted_iota(jnp.int32, (CHUNK, nk), 0)
        c = lax.broadcasted_iota(jnp.int32, (CHUNK, nk), 1)
        for var in range(3 if length > nk else 1):
            rel = jnp.abs(r - c + var * RADIUS)
            dist = rel.astype(F32) * float(d)
            for h in range(2):
                slope = sl_ref[0, 0:1, h * HEAD:h * HEAD + 1]
                tab[3 * cfg + var, h * CHUNK:(h + 1) * CHUNK, 0:nk] = jnp.where(rel <= RADIUS, -slope * dist, NEG)


def _attn_blocks(visit, unroll):
    def step(t, carry):
        for cfg, (d, length) in enumerate(DILATIONS):
            nblk = length // CHUNK
            if nblk == 1:
                visit(cfg, 0, t, t, length, t)
                continue
            rho, i = (0, t) if d == 1 else (t // nblk, t % nblk)
            ks = jnp.clip(i * CHUNK - RADIUS, 0, length - 2 * CHUNK)
            visit(cfg, (i * CHUNK - ks) // RADIUS, rho + d * (i * CHUNK), rho + d * ks, 2 * CHUNK, t)
        return carry
    lax.fori_loop(0, 16, step, 0, unroll=unroll)


def _stack_heads(v, left):
    return jnp.concatenate([jnp.where(left, v, 0.0), jnp.where(left, 0.0, v)], axis=0)


def _unstack_heads(v, left):
    return jnp.where(left, v[0:CHUNK], v[CHUNK:2 * CHUNK])


def _rows(start, n, d):
    return pl.ds(start, n) if d == 1 else pl.ds(start, n, stride=d)


def _blk16(col0):
    d, length = DILATIONS[2]
    return pl.BlockSpec((1, d, length, 128), lambda b, hp: (b, 0, 0, col0 // 128 + hp))


def _attn_fwd(proj, qkv16, slopes):
    def body(sl_ref, q_ref, k_ref, v_ref, q16_ref, k16_ref, v16_ref, a_ref, lse_ref, *scr):
        o_c, m_c, l_c, tab = scr[0:3], scr[3:6], scr[6:9], scr[9]
        left = _left_lanes(CHUNK)
        _fill_bias_tables(sl_ref, tab)

        def block(cfg, var, q0, k0, nk, t):
            d = DILATIONS[cfg][0]
            rq, rk = _rows(q0, CHUNK, d), _rows(k0, nk, d)
            if cfg == 2:
                qb, kw, vw = q16_ref[0, t].astype(F32), k16_ref[0, t], v16_ref[0, t]
            else:
                qb, kw, vw = q_ref[rq, :], k_ref[rk, :], v_ref[rk, :]
            qs = _stack_heads(qb * SCALE, left)
            s = _dot_nt(qs, kw) + tab[3 * cfg + var, :, 0:nk]
            m = jnp.max(s, axis=-1, keepdims=True)
            p = jnp.exp(s - m)
            o_c[cfg][rq, :] = _unstack_heads(_dot(p, vw), left)
            m_c[cfg][rq, :] = _unstack_heads(m, left)
            l_c[cfg][rq, :] = _unstack_heads(jnp.sum(p, axis=-1, keepdims=True), left)
        _attn_blocks(block, 16)

        def merge(j, carry):
            rows = pl.ds(pl.multiple_of(j * 256, 256), 256)
            ms = [m_c[i][rows, :] for i in range(3)]
            top = jnp.maximum(jnp.maximum(ms[0], ms[1]), ms[2])
            ws = [jnp.exp(m - top) for m in ms]
            den = l_c[0][rows, :] * ws[0] + l_c[1][rows, :] * ws[1] + l_c[2][rows, :] * ws[2]
            num = o_c[0][rows, :] * ws[0] + o_c[1][rows, :] * ws[1] + o_c[2][rows, :] * ws[2]
            a_ref[rows, :] = num / den
            lse_ref[rows, :] = top + jnp.log(den)
            return carry
        lax.fori_loop(0, SEQ // 256, merge, 0)

    blk = lambda col0: pl.BlockSpec((SEQ, 128), lambda b, hp: (b, col0 // 128 + hp))
    out = pl.BlockSpec((SEQ, 128), lambda b, hp: (b, hp))
    return pl.pallas_call(
        body, name="attn_fwd", grid=(B_LOC, 4),
        in_specs=[pl.BlockSpec((1, 8, 128), lambda b, hp: (hp, 0, 0)), blk(C_QA), blk(C_KA), blk(C_VA),
                  _blk16(C_QA), _blk16(C_KA), _blk16(C_VA)],
        out_specs=(out, out),
        out_shape=(SDS((T_LOC, ATTN_W), F32), SDS((T_LOC, ATTN_W), F32)),
        scratch_shapes=[pltpu.VMEM((SEQ, 128), F32)] * 9 + [pltpu.VMEM((N_BIAS, 2 * CHUNK, 2 * CHUNK), F32)],
        compiler_params=_params(VMEM_NO_STAGING_MIB, ("arbitrary", "arbitrary")),
    )(slopes, proj, proj, proj, qkv16, qkv16, qkv16)


def _chunks_side_by_side(v, pr, tm):
    return jnp.concatenate([v[ch * CHUNK:(ch + 1) * CHUNK, pr * 128:(pr + 1) * 128] for ch in range(tm // CHUNK)], axis=1)


def _first_group_lanes(tm):
    return lax.broadcasted_iota(jnp.int32, (CHUNK, tm), 1) % 128 < HEAD


def _store_chunks(dst_ref, pr, val, tm):
    for ch in range(tm // CHUNK):
        dst_ref[ch * CHUNK:(ch + 1) * CHUNK, pr * 128:(pr + 1) * 128] = val[:, ch * CHUNK:(ch + 1) * CHUNK]


def _sgu_mix(vn, ws_ref, dst_ref, tm):
    first = _first_group_lanes(tm)
    for pr in range(2):
        vp = _chunks_side_by_side(vn, pr, tm)
        _store_chunks(dst_ref, pr, jnp.where(first, _dot(ws_ref[2 * pr], vp), _dot(ws_ref[2 * pr + 1], vp)), tm)


def _mem_head_of_lane(rows):
    return lax.broadcasted_iota(jnp.int32, (rows, MEM_W), 1) // HEAD


def _stack_mem_heads(v, rows):
    head = _mem_head_of_lane(rows)
    return jnp.concatenate([jnp.where(head == h, v, 0.0) for h in range(4)], axis=0)


def _unstack_mem_heads(v, rows):
    head = _mem_head_of_lane(rows)
    out = v[0:rows]
    for h in range(1, 4):
        out = jnp.where(head == h, v[h * rows:(h + 1) * rows], out)
    return out


def _mem_attn_probs(q, kmem, rows):
    qs = _stack_mem_heads(q, rows).astype(BF16)
    s = _dot_nt(qs, kmem) * SCALE
    e = jnp.exp(s - jnp.max(s, axis=-1, keepdims=True))
    return e * (1.0 / jnp.sum(e, axis=-1, keepdims=True)), qs


def _branch_blocks(tm):
    col = lambda w, c0: pl.BlockSpec((tm, w), lambda i: (i, c0 // w))
    return [col(512, C_ZA), col(256, C_UB), col(256, C_VB), col(256, C_ZB), col(256, C_QM), col(256, C_ZM)]


def _branch_fwd(proj, a, kv, w_s, b_exp, g_v):
    tm = 512
    per_ex = SEQ // tm

    def body(za_ref, ub_ref, vb_ref, zb_ref, qm_ref, zm_ref, a_ref, kv_ref, ws_ref, be_ref, gv_ref, o_ref, mix):
        o_ref[:, 0:ATTN_W] = (_silu_and_grad(za_ref[...])[0] * a_ref[...]).astype(BF16)
        gu = _gelu_and_grad(ub_ref[...])[0]
        gv = _gelu_and_grad(vb_ref[...])[0]
        vn = gv * _rstd(gv) * gv_ref[...]
        _sgu_mix(vn.astype(BF16), ws_ref, mix, tm)
        sg = gu * (mix[...] + jnp.concatenate([be_ref[...]] * (tm // CHUNK), axis=0))
        o_ref[:, ATTN_W:ATTN_W + SGU_W] = (_silu_and_grad(zb_ref[...])[0] * sg).astype(BF16)
        p = _mem_attn_probs(qm_ref[...], kv_ref[:, 0:MEM_W], tm)[0]
        mo = _unstack_mem_heads(_dot(p, kv_ref[:, MEM_W:2 * MEM_W]), tm)
        o_ref[:, ATTN_W + SGU_W:D_MODEL] = (_silu_and_grad(zm_ref[...])[0] * mo).astype(BF16)

    full = lambda shape: pl.BlockSpec(shape, lambda i: (0,) * len(shape))
    return pl.pallas_call(
        body, name="branch_fwd", grid=(T_LOC // tm,),
        in_specs=_branch_blocks(tm) + [
            pl.BlockSpec((tm, ATTN_W), lambda i: (i, 0)), pl.BlockSpec((N_MEM, 2 * MEM_W), lambda i: (i // per_ex, 0)),
            full((4, CHUNK, CHUNK)), full((CHUNK, SGU_W)), full((1, SGU_W))],
        out_specs=pl.BlockSpec((tm, D_MODEL), lambda i: (i, 0)),
        out_shape=SDS((T_LOC, D_MODEL), BF16),
        scratch_shapes=[pltpu.VMEM((tm, SGU_W), F32)],
        compiler_params=_params(VMEM_NO_STAGING_MIB, ("arbitrary",)),
    )(proj, proj, proj, proj, proj, proj, a, kv, w_s, b_exp, g_v)


def _outproj_loss(gated, wout, x2, tgt2, g_final):
    tm = 512

    def body(g_ref, w_ref, x_ref, t_ref, gf_ref, dh2_ref, loss_ref, dgf_ref, dwo_ref, dwo_acc):
        @pl.when(pl.program_id(0) == 0)
        def _():
            loss_ref[...] = jnp.zeros_like(loss_ref)
            dgf_ref[...] = jnp.zeros_like(dgf_ref)
            dwo_acc[...] = jnp.zeros_like(dwo_acc)
        gated = g_ref[...]
        h2 = x_ref[...] + _dot(gated, w_ref[...])
        r = _rstd(h2)
        gf = gf_ref[...]
        err = h2 * r * gf - t_ref[...]
        loss_ref[...] += 0.5 * jnp.sum(jnp.mean(err * err, axis=-1, keepdims=True))
        dy = err * (1.0 / D_MODEL)
        dh2 = _rms_bwd(h2, r, gf, dy)
        dh2_ref[...] = dh2
        dgf_ref[...] += jnp.sum(dy * (h2 * r), axis=0, keepdims=True)
        dwo_acc[...] += _dot_tn(gated, dh2)

        @pl.when(pl.program_id(0) == pl.num_programs(0) - 1)
        def _():
            _cast_rows(dwo_acc, dwo_ref, D_MODEL)

    row = pl.BlockSpec((tm, D_MODEL), lambda i: (i, 0))
    vec = pl.BlockSpec((1, D_MODEL), lambda i: (0, 0))
    square = pl.BlockSpec((D_MODEL, D_MODEL), lambda i: (0, 0))
    return pl.pallas_call(
        body, name="outproj_loss", grid=(T_LOC // tm,),
        in_specs=[row, square, row, row, vec],
        out_specs=(row, pl.BlockSpec((8, 128), lambda i: (0, 0)), vec, square),
        out_shape=(SDS((T_LOC, D_MODEL), F32), SDS((8, 128), F32), SDS((1, D_MODEL), F32), SDS((D_MODEL, D_MODEL), BF16)),
        scratch_shapes=[pltpu.VMEM((D_MODEL, D_MODEL), F32)],
        compiler_params=_params(VMEM_NO_STAGING_MIB, ("arbitrary",)),
    )(gated, wout, x2, tgt2, g_final)


def _branch_bwd(dh2, wout, proj, a, kv, w_s, b_exp, g_v):
    tm = 512
    per_ex = SEQ // tm

    def body(dh2_ref, w_ref, za_ref, ub_ref, vb_ref, zb_ref, qm_ref, zm_ref, a_ref, kv_ref, ws_ref,
             be_ref, gv_ref, da_ref, dr_ref, dkv_ref, dws_ref, db_ref, dgv_ref, mix, dvn, dmsum):
        i = pl.program_id(0)

        @pl.when(i == 0)
        def _():
            dws_ref[...] = jnp.zeros_like(dws_ref)
            dgv_ref[...] = jnp.zeros_like(dgv_ref)
            dmsum[...] = jnp.zeros_like(dmsum)

        @pl.when(i % per_ex == 0)
        def _():
            dkv_ref[...] = jnp.zeros_like(dkv_ref)

        dg = _dot_nt(dh2_ref[...], w_ref[...])

        sa, dsa = _silu_and_grad(za_ref[...])
        dga = dg[:, 0:ATTN_W]
        da_ref[...] = dga * sa
        dr_ref[:, 0:512] = (dga * a_ref[...] * dsa).astype(BF16)

        ub, vb = ub_ref[...], vb_ref[...]
        gu, dgu = _gelu_and_grad(ub)
        gv, dgv = _gelu_and_grad(vb)
        rv = _rstd(gv)
        gain = gv_ref[...]
        vn = (gv * rv * gain).astype(BF16)
        _sgu_mix(vn, ws_ref, mix, tm)
        mixed = mix[...] + jnp.concatenate([be_ref[...]] * (tm // CHUNK), axis=0)
        sb, dsb = _silu_and_grad(zb_ref[...])
        dgb = dg[:, ATTN_W:ATTN_W + SGU_W]
        dsg = dgb * sb
        dr_ref[:, 512:768] = (dsg * mixed * dgu).astype(BF16)
        dr_ref[:, 1024:1280] = (dgb * (gu * mixed) * dsb).astype(BF16)
        dmix = dsg * gu
        for ch in range(tm // CHUNK):
            dmsum[...] += dmix[ch * CHUNK:(ch + 1) * CHUNK, :]
        first = _first_group_lanes(tm)
        for pr in range(2):
            dmp, vp = _chunks_side_by_side(dmix, pr, tm), _chunks_side_by_side(vn, pr, tm)
            dws_ref[2 * pr] += _dot_nt(jnp.where(first, dmp, 0.0), vp)
            dws_ref[2 * pr + 1] += _dot_nt(jnp.where(first, 0.0, dmp), vp)
            _store_chunks(dvn, pr, jnp.where(first, _dot_tn(ws_ref[2 * pr], dmp), _dot_tn(ws_ref[2 * pr + 1], dmp)), tm)
        dvn_v = dvn[...]
        dgv_ref[...] += jnp.sum(dvn_v * (gv * rv), axis=0, keepdims=True)
        dr_ref[:, 768:1024] = (_rms_bwd(gv, rv, gain, dvn_v) * dgv).astype(BF16)

        szm, dszm = _silu_and_grad(zm_ref[...])
        dgm = dg[:, ATTN_W + SGU_W:D_MODEL]
        kmem, vmem_ = kv_ref[:, 0:MEM_W].astype(BF16), kv_ref[:, MEM_W:2 * MEM_W].astype(BF16)
        p, qs = _mem_attn_probs(qm_ref[...], kmem, tm)
        dmos = _stack_mem_heads(dgm * szm, tm).astype(BF16)
        dr_ref[:, 1536:1792] = (dgm * _unstack_mem_heads(_dot(p, vmem_), tm) * dszm).astype(BF16)
        dp = _dot_nt(dmos, vmem_)
        ds = (p * (dp - jnp.sum(p * dp, axis=-1, keepdims=True)) * SCALE).astype(BF16)
        dr_ref[:, 1280:1536] = _unstack_mem_heads(_dot(ds, kmem), tm).astype(BF16)
        dkv_ref[:, 0:MEM_W] += _dot_tn(ds, qs)
        dkv_ref[:, MEM_W:2 * MEM_W] += _dot_tn(p, dmos)

        @pl.when(i == pl.num_programs(0) - 1)
        def _():
            tot = dmsum[...]
            hi = tot.astype(BF16)
            lo = (tot - hi.astype(F32)).astype(BF16)
            grp = (lax.broadcasted_iota(jnp.int32, (SGU_W, 128), 0) // HEAD
                   == lax.broadcasted_iota(jnp.int32, (SGU_W, 128), 1)).astype(BF16)
            db_ref[...] = (_dot(hi, grp) + _dot(lo, grp)).T[0:4, :]

    full = lambda shape: pl.BlockSpec(shape, lambda i: (0,) * len(shape))
    row = lambda w: pl.BlockSpec((tm, w), lambda i: (i, 0))
    return pl.pallas_call(
        body, name="branch_bwd", grid=(T_LOC // tm,),
        in_specs=[row(D_MODEL), full((D_MODEL, D_MODEL))] + _branch_blocks(tm) + [
            row(ATTN_W), pl.BlockSpec((N_MEM, 2 * MEM_W), lambda i: (i // per_ex, 0)),
            full((4, CHUNK, CHUNK)), full((CHUNK, SGU_W)), full((1, SGU_W))],
        out_specs=(row(ATTN_W), row(REST_W), pl.BlockSpec((N_MEM, 2 * MEM_W), lambda i: (i // per_ex, 0)),
                   full((4, CHUNK, CHUNK)), full((4, CHUNK)), full((1, SGU_W))),
        out_shape=(SDS((T_LOC, ATTN_W), F32), SDS((T_LOC, REST_W), BF16), SDS((B_LOC * N_MEM, 2 * MEM_W), F32),
                   SDS((4, CHUNK, CHUNK), F32), SDS((4, CHUNK), F32), SDS((1, SGU_W), F32)),
        scratch_shapes=[pltpu.VMEM((tm, SGU_W), F32), pltpu.VMEM((tm, SGU_W), F32), pltpu.VMEM((CHUNK, SGU_W), F32)],
        compiler_params=_params(56, ("arbitrary",)),
    )(dh2, wout, proj, proj, proj, proj, proj, proj, a, kv, w_s, b_exp, g_v)


def _attn_bwd(proj, qkv16, slopes, da, a, lse, token):
    def body(sl_ref, q_ref, k_ref, v_ref, q16_ref, k16_ref, v16_ref, da_ref, a_ref, lse_ref, _,
             dq_ref, dk_ref, dv_ref, *scr):
        dq_s, dk_s, dv_s, tab = scr[0:3], scr[3:6], scr[6:9], scr[9]
        lse_h, delta_h = scr[10:12], scr[12:14]
        p_all, ds_all = scr[14], scr[15]
        left = _left_lanes(CHUNK)
        _fill_bias_tables(sl_ref, tab)

        def prep(j, carry):
            rows = pl.ds(pl.multiple_of(j * 256, 256), 256)
            l256 = _left_lanes(256)
            prod = da_ref[rows, :] * a_ref[rows, :]
            delta_h[0][rows, :] = jnp.broadcast_to(jnp.sum(jnp.where(l256, prod, 0.0), axis=-1, keepdims=True), (256, 128))
            delta_h[1][rows, :] = jnp.broadcast_to(jnp.sum(jnp.where(l256, 0.0, prod), axis=-1, keepdims=True), (256, 128))
            pair = lse_ref[rows, :]
            other = pltpu.roll(pair, HEAD, axis=1)
            lse_h[0][rows, :] = jnp.where(l256, pair, other)
            lse_h[1][rows, :] = jnp.where(l256, other, pair)
            zero = jnp.zeros((256, 128), F32)
            for cfg in range(3):
                dk_s[cfg][rows, :] = zero
                dv_s[cfg][rows, :] = zero
            return carry
        lax.fori_loop(0, SEQ // 256, prep, 0)

        def per_row(halves, rq, nk):
            v = jnp.concatenate([halves[0][rq, :], halves[1][rq, :]], axis=0)
            return v if nk == 128 else jnp.concatenate([v, v], axis=1)

        def qkv(cfg, rq, rk, t):
            if cfg == 2:
                return q16_ref[0, t].astype(F32), k16_ref[0, t], v16_ref[0, t]
            return q_ref[rq, :], k_ref[rk, :], v_ref[rk, :]

        def probs(cfg, var, q0, k0, nk, t):
            d = DILATIONS[cfg][0]
            rq, rk = _rows(q0, CHUNK, d), _rows(k0, nk, d)
            qb, kw, vw = qkv(cfg, rq, rk, t)
            qs = _stack_heads(qb * SCALE, left)
            das = _stack_heads(da_ref[rq, :], left)
            s = _dot_nt(qs, kw) + tab[3 * cfg + var, :, 0:nk]
            p = jnp.exp(s - per_row(lse_h, rq, nk))
            p_all[16 * cfg + t, :, 0:nk] = p.astype(BF16)
            ds_all[16 * cfg + t, :, 0:nk] = (p * (_dot_nt(das, vw) - per_row(delta_h, rq, nk))).astype(BF16)
        _attn_blocks(probs, 4)

        def grads(cfg, var, q0, k0, nk, t):
            d = DILATIONS[cfg][0]
            rq, rk = _rows(q0, CHUNK, d), _rows(k0, nk, d)
            qb, kw, _ = qkv(cfg, rq, rk, t)
            qs = _stack_heads(qb * SCALE, left).astype(BF16)
            das = _stack_heads(da_ref[rq, :], left).astype(BF16)
            p, ds = p_all[16 * cfg + t, :, 0:nk], ds_all[16 * cfg + t, :, 0:nk]
            dq_s[cfg][rq, :] = _unstack_heads(_dot(ds, kw), left) * SCALE
            dk_s[cfg][rk, :] += _dot_tn(ds, qs)
            dv_s[cfg][rk, :] += _dot_tn(p, das)
        _attn_blocks(grads, 4)

        def flush(j, carry):
            rows = pl.ds(pl.multiple_of(j * 256, 256), 256)
            for acc, dst in ((dq_s, dq_ref), (dk_s, dk_ref), (dv_s, dv_ref)):
                dst[rows, :] = (acc[0][rows, :] + acc[1][rows, :] + acc[2][rows, :]).astype(BF16)
            return carry
        lax.fori_loop(0, SEQ // 256, flush, 0)

    blk = lambda col0: pl.BlockSpec((SEQ, 128), lambda b, hp: (b, col0 // 128 + hp))
    own = pl.BlockSpec((SEQ, 128), lambda b, hp: (b, hp))
    return pl.pallas_call(
        body, name="attn_bwd", grid=(B_LOC, 4),
        in_specs=[pl.BlockSpec((1, 8, 128), lambda b, hp: (hp, 0, 0)), blk(C_QA), blk(C_KA), blk(C_VA),
                  _blk16(C_QA), _blk16(C_KA), _blk16(C_VA), own, own, own, _TOKEN],
        out_specs=(own, own, own),
        out_shape=(SDS((T_LOC, ATTN_W), BF16),) * 3,
        scratch_shapes=[pltpu.VMEM((SEQ, 128), F32)] * 9 + [pltpu.VMEM((N_BIAS, 2 * CHUNK, 2 * CHUNK), F32)]
        + [pltpu.VMEM((SEQ, 128), F32)] * 4 + [pltpu.VMEM((48, 2 * CHUNK, 2 * CHUNK), BF16)] * 2,
        compiler_params=_params(52, ("arbitrary", "arbitrary")),
    )(slopes, proj, proj, proj, qkv16, qkv16, qkv16, da, a, lse, token)


def _dproj_specs(tm):
    third = pl.BlockSpec((tm, ATTN_W), lambda i: (i, 0))
    return [third, third, third, pl.BlockSpec((tm, REST_W), lambda i: (i, 0))]


def _dx(dq, dk, dv, dr, wint, x2, dh2, g_norm, token):
    tm = 512

    def body(dq_ref, dk_ref, dv_ref, dr_ref, w_ref, x_ref, dh2_ref, g_ref, _, gx_ref, dgn_ref):
        @pl.when(pl.program_id(0) == 0)
        def _():
            dgn_ref[...] = jnp.zeros_like(dgn_ref)
        dh = (_dot(dq_ref[...], w_ref[C_QA:C_KA, :]) + _dot(dk_ref[...], w_ref[C_KA:C_VA, :])
              + _dot(dv_ref[...], w_ref[C_VA:C_ZA, :]) + _dot(dr_ref[...], w_ref[C_ZA:IN_COLS, :]))
        xv = x_ref[...]
        r = _rstd(xv)
        gx_ref[...] = dh2_ref[...] + _rms_bwd(xv, r, g_ref[...], dh)
        dgn_ref[...] += jnp.sum(dh * (xv * r), axis=0, keepdims=True)

    row = pl.BlockSpec((tm, D_MODEL), lambda i: (i, 0))
    vec = pl.BlockSpec((1, D_MODEL), lambda i: (0, 0))
    return pl.pallas_call(
        body, name="dx", grid=(T_LOC // tm,),
        in_specs=_dproj_specs(tm) + [pl.BlockSpec((IN_COLS, D_MODEL), lambda i: (0, 0)), row, row, vec, _TOKEN],
        out_specs=(row, vec),
        out_shape=(SDS((T_LOC, D_MODEL), F32), SDS((1, D_MODEL), F32)),
        compiler_params=_params(48, ("arbitrary",)),
    )(dq, dk, dv, dr, wint, x2, dh2, g_norm, token)


def _dwin(dq, dk, dv, dr, x2, g_norm, half, token):
    tm = 1024
    width = D_MODEL // 2
    cols = slice(half * width, (half + 1) * width)

    def body(dq_ref, dk_ref, dv_ref, dr_ref, x_ref, g_ref, _, o_ref, acc):
        @pl.when(pl.program_id(0) == 0)
        def _():
            acc[...] = jnp.zeros_like(acc)
        xv = x_ref[...]
        h = (xv[:, cols] * _rstd(xv) * g_ref[:, cols]).astype(BF16)
        acc[C_QA:C_KA, :] += _dot_tn(dq_ref[...], h)
        acc[C_KA:C_VA, :] += _dot_tn(dk_ref[...], h)
        acc[C_VA:C_ZA, :] += _dot_tn(dv_ref[...], h)
        acc[C_ZA:IN_COLS, :] += _dot_tn(dr_ref[...], h)

        @pl.when(pl.program_id(0) == pl.num_programs(0) - 1)
        def _():
            _cast_rows(acc, o_ref, IN_COLS)

    return pl.pallas_call(
        body, name="dwin%d" % half, grid=(T_LOC // tm,),
        in_specs=_dproj_specs(tm) + [pl.BlockSpec((tm, D_MODEL), lambda i: (i, 0)),
                                     pl.BlockSpec((1, D_MODEL), lambda i: (0, 0)), _TOKEN],
        out_specs=pl.BlockSpec((IN_COLS, width), lambda i: (0, 0)),
        out_shape=SDS((IN_COLS, width), BF16),
        scratch_shapes=[pltpu.VMEM((IN_COLS, width), F32)],
        compiler_params=_params(48, ("arbitrary",)),
    )(dq, dk, dv, dr, x2, g_norm, token)


def _memkv_bwd(dkv, mem2, g_mem, wkv):
    def body(dkv_ref, m_ref, g_ref, w_ref, dw_ref, dg_ref):
        mv = m_ref[...]
        r = _rstd(mv)
        dkv_v = dkv_ref[...].astype(BF16)
        dw_ref[...] = _dot_tn(mv * r * g_ref[...], dkv_v).astype(BF16)
        dg_ref[...] = jnp.sum(_dot_nt(dkv_v, w_ref[...]) * (mv * r), axis=0, keepdims=True)

    return pl.pallas_call(
        body, name="memkv_bwd", out_shape=(SDS((D_MODEL, 2 * MEM_W), BF16), SDS((1, D_MODEL), F32)),
        compiler_params=_params(32),
    )(dkv, mem2, g_mem, wkv)


def _allreduce_small(parts):
    n = len(parts)

    def body(*refs):
        ins, outs, bufs = refs[0:n], refs[n:2 * n], refs[2 * n:3 * n]
        send_sems, recv_sems = refs[3 * n], refs[3 * n + 1]
        pos = _mesh_pos()
        me = _flat(pos)
        for a in range(n):
            bufs[a][me] = ins[a][...]

        def copy(a, k, slot):
            return pltpu.make_async_remote_copy(
                src_ref=ins[a], dst_ref=bufs[a].at[slot],
                send_sem=send_sems.at[7 * a + k - 1], recv_sem=recv_sems.at[7 * a + k - 1],
                device_id=_peer(pos, k), device_id_type=MESH)

        sent = [copy(a, k, me) for a in range(n) for k in range(1, N_DEV)]
        for cp in sent:
            cp.start()
        for a in range(n):
            for k in range(1, N_DEV):
                copy(a, k, _flat(_peer(pos, k))).wait_recv()
        for cp in sent:
            cp.wait_send()
        for a in range(n):
            acc = bufs[a][0]
            for s in range(1, N_DEV):
                acc = acc + bufs[a][s]
            outs[a][...] = acc

    vmem = pl.BlockSpec(memory_space=pltpu.VMEM)
    return pl.pallas_call(
        body, name="allreduce_small",
        out_shape=tuple(SDS(p.shape, F32) for p in parts),
        in_specs=[vmem] * n, out_specs=(vmem,) * n,
        scratch_shapes=[pltpu.VMEM((N_DEV,) + p.shape, F32) for p in parts]
        + [pltpu.SemaphoreType.DMA((7 * n,)), pltpu.SemaphoreType.DMA((7 * n,))],
        compiler_params=_params(16),
    )(*parts)


_HBM = pl.BlockSpec(memory_space=pltpu.HBM)
_SEM = pl.BlockSpec(memory_space=pltpu.SEMAPHORE)
_SIDE_EFFECT = pltpu.SideEffectType.DATAFLOW_SIDE_EFFECTING


def _exchange_copies(src_refs, land_refs, scatter, send_sems, recv_sems):
    pos = _mesh_pos()
    copies = []
    for a, (src, land) in enumerate(zip(src_refs, land_refs)):
        n = land.shape[1]
        for k in range(1, N_DEV):
            peer = _peer(pos, k)
            piece = src.at[pl.ds(pl.multiple_of(_flat(peer) * n, 16), n), :] if scatter[a] else src
            copies.append(pltpu.make_async_remote_copy(
                src_ref=piece, dst_ref=land.at[_flat(pos)],
                send_sem=send_sems.at[7 * a + k - 1], recv_sem=recv_sems.at[7 * a + k - 1],
                device_id=peer, device_id_type=MESH))
    return copies


def _own_copies(src_refs, land_refs, scatter, send_sems):
    n = len(src_refs)
    me = _flat(_mesh_pos())
    copies = []
    for a, (src, land) in enumerate(zip(src_refs, land_refs)):
        rows = land.shape[1]
        piece = src.at[pl.ds(pl.multiple_of(me * rows, 16), rows), :] if scatter[a] else src
        copies.append(pltpu.make_async_copy(piece, land.at[me], send_sems.at[7 * n + a]))
    return copies


def _exchange_start(name, srcs, scatter):
    n = len(srcs)

    def body(*refs):
        for cp in _exchange_copies(refs[0:n], refs[n:2 * n], scatter, refs[2 * n], refs[2 * n + 1]):
            cp.start()
        for cp in _own_copies(refs[0:n], refs[n:2 * n], scatter, refs[2 * n]):
            cp.start()
        refs[-1][...] = jnp.zeros_like(refs[-1])

    lands = [lax.empty((N_DEV,) + (t.shape[0] // N_DEV if sc else t.shape[0],) + t.shape[1:], t.dtype)
             for t, sc in zip(srcs, scatter)]
    ops = [pltpu.with_memory_space_constraint(t, pltpu.HBM) for t in (*srcs, *lands)]
    out = pl.pallas_call(
        body, name=name,
        out_shape=(pltpu.SemaphoreType.DMA((8 * n,)), pltpu.SemaphoreType.DMA((7 * n,)),
                   *[pltpu.HBM(t.shape, t.dtype) for t in ops], SDS((8, 128), F32)),
        in_specs=[_HBM] * (2 * n),
        out_specs=(_SEM, _SEM, *[_HBM] * (2 * n), pl.BlockSpec(memory_space=pltpu.VMEM)),
        input_output_aliases={i: 2 + i for i in range(2 * n)},
        compiler_params=pltpu.CompilerParams(has_side_effects=_SIDE_EFFECT),
    )(*ops)
    return out[0], out[1], out[2:2 + n], out[2 + n:2 + 2 * n], out[-1]


def _exchange_wait(name, started, scatter, after):
    send_sems, recv_sems, srcs, lands, _ = started
    n = len(srcs)

    def body(*refs):
        for cp in _exchange_copies(refs[0:n], refs[n:2 * n], scatter, refs[2 * n], refs[2 * n + 1]):
            cp.wait_send()
            cp.wait_recv()
        for cp in _own_copies(refs[0:n], refs[n:2 * n], scatter, refs[2 * n]):
            cp.wait()

    out = pl.pallas_call(
        body, name=name,
        out_shape=tuple(pltpu.HBM(t.shape, t.dtype) for t in (*srcs, *lands)),
        in_specs=[_HBM] * (2 * n) + [_SEM, _SEM, pl.BlockSpec(memory_space=pl.ANY)],
        out_specs=(_HBM,) * (2 * n),
        input_output_aliases={i: i for i in range(2 * n)},
        compiler_params=pltpu.CompilerParams(has_side_effects=_SIDE_EFFECT),
    )(*srcs, *lands, send_sems, recv_sems, after)
    return out[n:]


def _adamw(w, g, m, v):
    m = ADAM_B1 * m + (1.0 - ADAM_B1) * g
    v = ADAM_B2 * v + (1.0 - ADAM_B2) * (g * g)
    m_hat = m / (1.0 - ADAM_B1 ** ADAM_STEP)
    v_hat = v / (1.0 - ADAM_B2 ** ADAM_STEP)
    return -ADAM_LR * (m_hat / (jnp.sqrt(v_hat) + ADAM_EPS) + ADAM_WD * w), m, v


def _adam_slots(name, pieces, w, m, v):
    rows, cols = w.shape
    starts = [sum(p.shape[2] for p in pieces[:i]) for i in range(len(pieces) + 1)]
    assert starts[-1] == cols and all(p.shape[1] == rows for p in pieces)

    def body(*refs):
        s_refs, (w_ref, m_ref, v_ref, g_o, d_o, m_o, v_o, acc) = refs[:len(pieces)], refs[len(pieces):]
        s = pl.program_id(0)

        @pl.when(s == 0)
        def _():
            for i, s_ref in enumerate(s_refs):
                acc[:, starts[i]:starts[i + 1]] = s_ref[0].astype(F32)

        @pl.when(s > 0)
        def _():
            for i, s_ref in enumerate(s_refs):
                acc[:, starts[i]:starts[i + 1]] += s_ref[0].astype(F32)

        @pl.when(s == N_DEV - 1)
        def _():
            g = acc[...]
            g_o[...] = g
            d_o[...], m_o[...], v_o[...] = _adamw(w_ref[...], g, m_ref[...], v_ref[...])

    full = pl.BlockSpec((rows, cols), lambda s: (0, 0))
    return pl.pallas_call(
        body, name="adam_" + name, grid=(N_DEV,),
        in_specs=[pl.BlockSpec((1, rows, p.shape[2]), lambda s: (s, 0, 0)) for p in pieces] + [full, full, full],
        out_specs=(full,) * 4, out_shape=(SDS((rows, cols), F32),) * 4,
        scratch_shapes=[pltpu.VMEM((rows, cols), F32)],
        compiler_params=_params(40, ("arbitrary",)),
    )(*pieces, w, m, v)


def _adam_small(ws, gs, ms, vs, loss_slots):
    n = len(ws)

    def total(ref, like):
        if len(ref.shape) == len(like.shape):
            return ref[...]
        acc = ref[0]
        for s in range(1, N_DEV):
            acc = acc + ref[s]
        return acc

    def body(*refs):
        w_r, g_r, m_r, v_r = refs[0:n], refs[n:2 * n], refs[2 * n:3 * n], refs[3 * n:4 * n]
        loss_r, outs = refs[4 * n], refs[4 * n + 1:]
        for a in range(n):
            g = total(g_r[a], w_r[a])
            outs[a][...] = g
            outs[n + 1 + 3 * a][...], outs[n + 2 + 3 * a][...], outs[n + 3 + 3 * a][...] = _adamw(
                w_r[a][...], g, m_r[a][...], v_r[a][...])
        outs[n][...] = total(loss_r, outs[n])

    out = pl.pallas_call(
        body, name="adam_small",
        out_shape=tuple(SDS(w.shape, F32) for w in ws) + (SDS(loss_slots.shape[1:], F32),)
        + tuple(SDS(w.shape, F32) for w in ws for _ in range(3)),
        compiler_params=_params(16),
    )(*ws, *gs, *ms, *vs, loss_slots)
    return out[0:n], out[n], out[n + 1:]


def kernel(x, mem, g_norm, w_in, w_sgu_spatial, b_sgu_spatial, g_sgu_v, g_mem, w_mem_kv, w_out, g_final, loss_target, m_g_norm, m_w_in, m_w_sgu_spatial, m_b_sgu_spatial, m_g_sgu_v, m_g_mem, m_w_mem_kv, m_w_out, m_g_final, v_g_norm, v_w_in, v_w_sgu_spatial, v_b_sgu_spatial, v_g_sgu_v, v_g_mem, v_w_mem_kv, v_w_out, v_g_final):
    x2 = x.reshape(T_LOC, D_MODEL)
    tgt2 = loss_target.reshape(T_LOC, D_MODEL)
    mem2 = mem.reshape(B_LOC * N_MEM, D_MODEL)
    w_s = w_sgu_spatial[0]
    b_exp = jnp.repeat(b_sgu_spatial[0].T, HEAD, axis=1)
    slope = jnp.power(2.0, -8.0 * (jnp.arange(8, dtype=F32) + 1.0) / 8)
    slopes = jnp.broadcast_to(jnp.repeat(slope.reshape(4, 2), HEAD, axis=1)[:, None, :], (4, 8, 128))

    tr = lambda t: jnp.transpose(t[0])

    wint, wkv_own, wout_own = _allgather_weights(tr(w_in), w_mem_kv[0], w_out[0])
    started0 = _exchange_start("exchange0_start", [wkv_own, wout_own], [False, False])
    proj, qkv16 = _proj_fwd(x2, g_norm, wint, started0[4])
    wkv, wout = _exchange_wait("exchange0_wait", started0, [False, False], proj)
    wkv, wout = wkv.reshape(D_MODEL, 2 * MEM_W), wout.reshape(D_MODEL, D_MODEL)
    kv = _memkv_fwd(mem2, g_mem, wkv)
    a, lse = _attn_fwd(proj, qkv16, slopes)
    gated = _branch_fwd(proj, a, kv, w_s, b_exp, g_sgu_v)
    dh2, loss8, dgf, dwout = _outproj_loss(gated, wout, x2, tgt2, g_final.reshape(1, D_MODEL))

    da, dr, dkv, dws, dbs, dgv = _branch_bwd(dh2, wout, proj, a, kv, w_s, b_exp, g_sgu_v)
    dwkv, dgm = _memkv_bwd(dkv, mem2, g_mem, wkv)

    early = [dws.reshape(4 * CHUNK, CHUNK), dbs, dgv, dgm, dgf, loss8]
    scatter1 = [True, True] + [False] * len(early)
    started1 = _exchange_start("exchange1_start", [dwkv, dwout] + early, scatter1)
    dq, dk, dv = _attn_bwd(proj, qkv16, slopes, da, a, lse, started1[4])
    s_wkv, s_wout, s_ws, s_bs, s_gv, s_gm, s_gf, s_loss = _exchange_wait("exchange1_wait", started1, scatter1, dq)

    dwint0 = _dwin(dq, dk, dv, dr, x2, g_norm, 0, started1[4])
    started2 = _exchange_start("exchange2_start", [dwint0], [True])
    dwint1 = _dwin(dq, dk, dv, dr, x2, g_norm, 1, started2[4])
    started3 = _exchange_start("exchange3_start", [dwint1], [True])
    grad_x, dgn = _dx(dq, dk, dv, dr, wint, x2, dh2, g_norm, started3[4])
    s_win0, = _exchange_wait("exchange2_wait", started2, [True], grad_x)
    dgn_sum, = _allreduce_small([dgn])
    s_win1, = _exchange_wait("exchange3_wait", started3, [True], dgn_sum)

    g_win, d_win, m_win, v_win = map(
        jnp.transpose, _adam_slots("w_in", [s_win0, s_win1], tr(w_in), tr(m_w_in), tr(v_w_in)))
    g_wkv, d_wkv, m_wkv, v_wkv = _adam_slots("w_mem_kv", [s_wkv], w_mem_kv[0], m_w_mem_kv[0], v_w_mem_kv[0])
    g_wout, d_wout, m_wout, v_wout = _adam_slots("w_out", [s_wout], w_out[0], m_w_out[0], v_w_out[0])

    small_shapes = [(1, D_MODEL), (4 * CHUNK, CHUNK), (4, CHUNK), (1, SGU_W), (1, D_MODEL), (1, D_MODEL)]
    pack = lambda arrs: [t.reshape(s) for t, s in zip(arrs, small_shapes)]
    g_small, loss_sum, upd = _adam_small(
        pack([g_norm, w_sgu_spatial, b_sgu_spatial, g_sgu_v, g_mem, g_final]),
        [dgn_sum, s_ws, s_bs, s_gv, s_gm, s_gf],
        pack([m_g_norm, m_w_sgu_spatial, m_b_sgu_spatial, m_g_sgu_v, m_g_mem, m_g_final]),
        pack([v_g_norm, v_w_sgu_spatial, v_b_sgu_spatial, v_g_sgu_v, v_g_mem, v_g_final]), s_loss)
    out_shapes = [g_norm.shape, w_sgu_spatial.shape, b_sgu_spatial.shape, g_sgu_v.shape, g_mem.shape, g_final.shape]
    unpack = lambda arrs: [t.reshape(s) for t, s in zip(arrs, out_shapes)]
    gs = unpack(g_small)
    ds, nms, nvs = unpack(upd[0::3]), unpack(upd[1::3]), unpack(upd[2::3])

    loss = loss_sum[0, 0]

    def assemble(small, win, wkv_, wout_):
        return [small[0], win[None], small[1], small[2], small[3], small[4], wkv_[None], wout_[None], small[5]]

    return (loss, grad_x.reshape(x.shape),
            *assemble(gs, g_win, g_wkv, g_wout), *assemble(ds, d_win, d_wkv, d_wout),
            *assemble(nms, m_win, m_wkv, m_wout), *assemble(nvs, v_win, v_wkv, v_wout))
```

```python
import jax
import jax.numpy as jnp
from jax import lax
from jax.experimental import pallas as pl
from jax.experimental.pallas import tpu as pltpu

F32 = jnp.float32
BF16 = jnp.bfloat16
SDS = jax.ShapeDtypeStruct
MESH = pl.DeviceIdType.MESH

N_DEV = 8
D_MODEL = 1024
SEQ = 2048
B_LOC = 2
T_LOC = B_LOC * SEQ
N_MEM = 256
HEAD = 64
ATTN_W = 512
SGU_W = 256
MEM_W = 256
IN_COLS = 3328
W_IN_SHARD = IN_COLS // N_DEV
CHUNK = 128
DILATIONS = ((1, 2048), (4, 512), (16, 128))
RADIUS = 64
EPS = 1e-6
NEG = -1e30
SCALE = HEAD ** -0.5
C_QA, C_KA, C_VA, C_ZA, C_UB, C_VB, C_ZB, C_QM, C_ZM = 0, 512, 1024, 1536, 2048, 2304, 2560, 2816, 3072
QKV_W = 1536
REST_W = IN_COLS - QKV_W

ADAM_LR, ADAM_B1, ADAM_B2, ADAM_EPS, ADAM_WD, ADAM_STEP = 0.001, 0.9, 0.999, 1e-08, 0.01, 10

V7X_VMEM_MIB = 64
VMEM_NO_STAGING_MIB = V7X_VMEM_MIB - 6


def _params(vmem_mib, sem=None):
    assert vmem_mib < V7X_VMEM_MIB
    return pltpu.CompilerParams(vmem_limit_bytes=vmem_mib << 20, dimension_semantics=sem)


_TOKEN = pl.BlockSpec(memory_space=pl.ANY)


def _dot(a, b):
    return jnp.dot(a.astype(BF16), b.astype(BF16), preferred_element_type=F32)


def _dot_nt(a, b):
    return lax.dot_general(a.astype(BF16), b.astype(BF16), (((1,), (1,)), ((), ())), preferred_element_type=F32)


def _dot_tn(a, b):
    return lax.dot_general(a.astype(BF16), b.astype(BF16), (((0,), (0,)), ((), ())), preferred_element_type=F32)


def _rstd(v):
    return lax.rsqrt(jnp.mean(v * v, axis=-1, keepdims=True) + EPS)


def _rms_bwd(v, r, g, dy):
    gdy = g * dy
    return r * gdy - v * (r * r * r * jnp.mean(gdy * v, axis=-1, keepdims=True))


def _sigmoid(z):
    return 1.0 / (1.0 + jnp.exp(-z))


def _silu_and_grad(z):
    s = _sigmoid(z)
    return z * s, s * (1.0 + z * (1.0 - s))


_G_C = 0.7978845608028654
_G_K = 0.044715


def _gelu_and_grad(v):
    t = jnp.tanh(_G_C * (v + _G_K * (v * v * v)))
    cdf = 0.5 * (1.0 + t)
    return v * cdf, cdf + 0.5 * v * (1.0 - t * t) * (_G_C * (1.0 + 3.0 * _G_K * v * v))


def _cast_rows(src_ref, dst_ref, rows, step=256):
    def one(i, carry):
        r = pl.ds(pl.multiple_of(i * step, step), step)
        dst_ref[r, :] = src_ref[r, :].astype(dst_ref.dtype)
        return carry
    lax.fori_loop(0, rows // step, one, 0)


def _left_lanes(rows):
    return lax.broadcasted_iota(jnp.int32, (rows, 128), 1) < HEAD


def _mesh_pos():
    return lax.axis_index("x"), lax.axis_index("y"), lax.axis_index("c")


def _peer(pos, k):
    x, y, c = pos
    return (1 - x if k & 4 else x, 1 - y if k & 2 else y, 1 - c if k & 1 else c)


def _flat(pos):
    return 4 * pos[0] + 2 * pos[1] + pos[2]


def _allgather_weights(w_in_t, w_kv, w_out):
    def body(win_ref, wkv_ref, wout_ref, wint_o, wkv_o, wout_o, send_sems, recv_sems):
        x, y, c = _mesh_pos()
        me, sib = (x, y, c), (x, y, 1 - c)
        chips = [(1 - x, y), (x, 1 - y), (1 - x, 1 - y)]

        def rows(p):
            return wint_o.at[pl.ds(pl.multiple_of(_flat(p) * W_IN_SHARD, 16), W_IN_SHARD), :]

        rows(me)[...] = win_ref[...].astype(BF16)

        def copy(k, block, to):
            return pltpu.make_async_remote_copy(
                src_ref=rows(block), dst_ref=rows(block), send_sem=send_sems.at[k], recv_sem=recv_sems.at[k],
                device_id=to, device_id_type=MESH)

        first = [copy(0, me, sib)] + [copy(1 + j, me, (*chip, c)) for j, chip in enumerate(chips)]
        for cp in first:
            cp.start()
        wkv_o[...] = wkv_ref[...].astype(BF16)
        wout_o[...] = wout_ref[...].astype(BF16)
        passed = []
        for j, chip in enumerate(chips):
            copy(1 + j, (*chip, c), me).wait_recv()
            fwd = copy(4 + j, (*chip, c), sib)
            fwd.start()
            passed.append(fwd)
        copy(0, sib, me).wait_recv()
        for j, chip in enumerate(chips):
            copy(4 + j, (*chip, 1 - c), me).wait_recv()
        for cp in first + passed:
            cp.wait_send()

    vmem = pl.BlockSpec(memory_space=pltpu.VMEM)
    return pl.pallas_call(
        body, name="allgather_weights",
        out_shape=(SDS((IN_COLS, D_MODEL), BF16), SDS(w_kv.shape, BF16), SDS(w_out.shape, BF16)),
        in_specs=[vmem, vmem, vmem], out_specs=(vmem, vmem, vmem),
        scratch_shapes=[pltpu.SemaphoreType.DMA((7,)), pltpu.SemaphoreType.DMA((7,))],
        compiler_params=_params(40),
    )(w_in_t, w_kv, w_out)


def _proj_fwd(x2, g_norm, wint, token):
    tm = 512
    sub = 256
    d = DILATIONS[2][0]
    per_ex = SEQ // tm

    def body(x_ref, g_ref, w_ref, _, o_ref, o16_ref):
        xv = x_ref[...]
        h = xv * _rstd(xv) * g_ref[...]
        res = _dot_nt(h, w_ref[...])
        o_ref[...] = res
        r_out = lax.broadcasted_iota(jnp.int32, (sub, sub), 0)
        r_in = lax.broadcasted_iota(jnp.int32, (sub, sub), 1)
        pick = (r_in == d * (r_out % (sub // d)) + r_out // (sub // d)).astype(BF16)
        for part in range(tm // sub):
            grouped = _dot(pick, res[part * sub:(part + 1) * sub, 0:QKV_W]).astype(BF16)
            for rho in range(d):
                o16_ref[0, rho, part * (sub // d):(part + 1) * (sub // d), :] = (
                    grouped[rho * (sub // d):(rho + 1) * (sub // d), :])

    return pl.pallas_call(
        body, name="proj_fwd", grid=(T_LOC // tm,),
        in_specs=[pl.BlockSpec((tm, D_MODEL), lambda i: (i, 0)), pl.BlockSpec((1, D_MODEL), lambda i: (0, 0)),
                  pl.BlockSpec((IN_COLS, D_MODEL), lambda i: (0, 0)), _TOKEN],
        out_specs=(pl.BlockSpec((tm, IN_COLS), lambda i: (i, 0)),
                   pl.BlockSpec((1, d, tm // d, QKV_W), lambda i: (i // per_ex, 0, i % per_ex, 0))),
        out_shape=(SDS((T_LOC, IN_COLS), F32), SDS((B_LOC, d, SEQ // d, QKV_W), BF16)),
        compiler_params=_params(48, ("arbitrary",)),
    )(x2, g_norm, wint, token)


def _memkv_fwd(mem2, g_mem, wkv):
    def body(m_ref, g_ref, w_ref, o_ref):
        mv = m_ref[...]
        o_ref[...] = _dot(mv * _rstd(mv) * g_ref[...], w_ref[...])

    return pl.pallas_call(
        body, name="memkv_fwd", out_shape=SDS((B_LOC * N_MEM, 2 * MEM_W), F32), compiler_params=_params(32),
    )(mem2, g_mem, wkv)


N_BIAS = 7


def _fill_bias_tables(sl_ref, tab):
    for cfg, (d, length) in enumerate(DILATIONS):
        nk = min(length, 2 * CHUNK)
        r = lax.broadcasted_iota(jnp.int32, (CHUNK, nk), 0)
        c = lax.broadcasted_iota(jnp.int32, (CHUNK, nk), 1)
        for var in range(3 if length > nk else 1):
            rel = jnp.abs(r - c + var * RADIUS)
            dist = rel.astype(F32) * float(d)
            for h in range(2):
                slope = sl_ref[0, 0:1, h * HEAD:h * HEAD + 1]
                tab[3 * cfg + var, h * CHUNK:(h + 1) * CHUNK, 0:nk] = jnp.where(rel <= RADIUS, -slope * dist, NEG)


def _attn_blocks(visit, unroll):
    def step(t, carry):
        for cfg, (d, length) in enumerate(DILATIONS):
            nblk = length // CHUNK
            if nblk == 1:
                visit(cfg, 0, t, t, length, t)
                continue
            rho, i = (0, t) if d == 1 else (t // nblk, t % nblk)
            ks = jnp.clip(i * CHUNK - RADIUS, 0, length - 2 * CHUNK)
            visit(cfg, (i * CHUNK - ks) // RADIUS, rho + d * (i * CHUNK), rho + d * ks, 2 * CHUNK, t)
        return carry
    lax.fori_loop(0, 16, step, 0, unroll=unroll)


def _stack_heads(v, left):
    return jnp.concatenate([jnp.where(left, v, 0.0), jnp.where(left, 0.0, v)], axis=0)


def _unstack_heads(v, left):
    return jnp.where(left, v[0:CHUNK], v[CHUNK:2 * CHUNK])


def _rows(start, n, d):
    return pl.ds(start, n) if d == 1 else pl.ds(start, n, stride=d)


def _blk16(col0):
    d, length = DILATIONS[2]
    return pl.BlockSpec((1, d, length, 128), lambda b, hp: (b, 0, 0, col0 // 128 + hp))


def _attn_fwd(proj, qkv16, slopes):
    def body(sl_ref, q_ref, k_ref, v_ref, q16_ref, k16_ref, v16_ref, a_ref, lse_ref, *scr):
        o_c, m_c, l_c, tab = scr[0:3], scr[3:6], scr[6:9], scr[9]
        left = _left_lanes(CHUNK)
        _fill_bias_tables(sl_ref, tab)

        def block(cfg, var, q0, k0, nk, t):
            d = DILATIONS[cfg][0]
            rq, rk = _rows(q0, CHUNK, d), _rows(k0, nk, d)
            if cfg == 2:
                qb, kw, vw = q16_ref[0, t].astype(F32), k16_ref[0, t], v16_ref[0, t]
            else:
                qb, kw, vw = q_ref[rq, :], k_ref[rk, :], v_ref[rk, :]
            qs = _stack_heads(qb * SCALE, left)
            s = _dot_nt(qs, kw) + tab[3 * cfg + var, :, 0:nk]
            m = jnp.max(s, axis=-1, keepdims=True)
            p = jnp.exp(s - m)
            o_c[cfg][rq, :] = _unstack_heads(_dot(p, vw), left)
            m_c[cfg][rq, :] = _unstack_heads(m, left)
            l_c[cfg][rq, :] = _unstack_heads(jnp.sum(p, axis=-1, keepdims=True), left)
        _attn_blocks(block, 16)

        def merge(j, carry):
            rows = pl.ds(pl.multiple_of(j * 256, 256), 256)
            ms = [m_c[i][rows, :] for i in range(3)]
            top = jnp.maximum(jnp.maximum(ms[0], ms[1]), ms[2])
            ws = [jnp.exp(m - top) for m in ms]
            den = l_c[0][rows, :] * ws[0] + l_c[1][rows, :] * ws[1] + l_c[2][rows, :] * ws[2]
            num = o_c[0][rows, :] * ws[0] + o_c[1][rows, :] * ws[1] + o_c[2][rows, :] * ws[2]
            a_ref[rows, :] = num / den
            lse_ref[rows, :] = top + jnp.log(den)
            return carry
        lax.fori_loop(0, SEQ // 256, merge, 0)

    blk = lambda col0: pl.BlockSpec((SEQ, 128), lambda b, hp: (b, col0 // 128 + hp))
    out = pl.BlockSpec((SEQ, 128), lambda b, hp: (b, hp))
    return pl.pallas_call(
        body, name="attn_fwd", grid=(B_LOC, 4),
        in_specs=[pl.BlockSpec((1, 8, 128), lambda b, hp: (hp, 0, 0)), blk(C_QA), blk(C_KA), blk(C_VA),
                  _blk16(C_QA), _blk16(C_KA), _blk16(C_VA)],
        out_specs=(out, out),
        out_shape=(SDS((T_LOC, ATTN_W), F32), SDS((T_LOC, ATTN_W), F32)),
        scratch_shapes=[pltpu.VMEM((SEQ, 128), F32)] * 9 + [pltpu.VMEM((N_BIAS, 2 * CHUNK, 2 * CHUNK), F32)],
        compiler_params=_params(VMEM_NO_STAGING_MIB, ("arbitrary", "arbitrary")),
    )(slopes, proj, proj, proj, qkv16, qkv16, qkv16)


def _chunks_side_by_side(v, pr, tm):
    return jnp.concatenate([v[ch * CHUNK:(ch + 1) * CHUNK, pr * 128:(pr + 1) * 128] for ch in range(tm // CHUNK)], axis=1)


def _first_group_lanes(tm):
    return lax.broadcasted_iota(jnp.int32, (CHUNK, tm), 1) % 128 < HEAD


def _store_chunks(dst_ref, pr, val, tm):
    for ch in range(tm // CHUNK):
        dst_ref[ch * CHUNK:(ch + 1) * CHUNK, pr * 128:(pr + 1) * 128] = val[:, ch * CHUNK:(ch + 1) * CHUNK]


def _sgu_mix(vn, ws_ref, dst_ref, tm):
    first = _first_group_lanes(tm)
    for pr in range(2):
        vp = _chunks_side_by_side(vn, pr, tm)
        _store_chunks(dst_ref, pr, jnp.where(first, _dot(ws_ref[2 * pr], vp), _dot(ws_ref[2 * pr + 1], vp)), tm)


def _mem_head_of_lane(rows):
    return lax.broadcasted_iota(jnp.int32, (rows, MEM_W), 1) // HEAD


def _stack_mem_heads(v, rows):
    head = _mem_head_of_lane(rows)
    return jnp.concatenate([jnp.where(head == h, v, 0.0) for h in range(4)], axis=0)


def _unstack_mem_heads(v, rows):
    head = _mem_head_of_lane(rows)
    out = v[0:rows]
    for h in range(1, 4):
        out = jnp.where(head == h, v[h * rows:(h + 1) * rows], out)
    return out


def _mem_attn_probs(q, kmem, rows):
    qs = _stack_mem_heads(q, rows).astype(BF16)
    s = _dot_nt(qs, kmem) * SCALE
    e = jnp.exp(s - jnp.max(s, axis=-1, keepdims=True))
    return e * (1.0 / jnp.sum(e, axis=-1, keepdims=True)), qs


def _branch_blocks(tm):
    col = lambda w, c0: pl.BlockSpec((tm, w), lambda i: (i, c0 // w))
    return [col(512, C_ZA), col(256, C_UB), col(256, C_VB), col(256, C_ZB), col(256, C_QM), col(256, C_ZM)]


def _branch_fwd(proj, a, kv, w_s, b_exp, g_v):
    tm = 512
    per_ex = SEQ // tm

    def body(za_ref, ub_ref, vb_ref, zb_ref, qm_ref, zm_ref, a_ref, kv_ref, ws_ref, be_ref, gv_ref, o_ref, mix):
        o_ref[:, 0:ATTN_W] = (_silu_and_grad(za_ref[...])[0] * a_ref[...]).astype(BF16)
        gu = _gelu_and_grad(ub_ref[...])[0]
        gv = _gelu_and_grad(vb_ref[...])[0]
        vn = gv * _rstd(gv) * gv_ref[...]
        _sgu_mix(vn.astype(BF16), ws_ref, mix, tm)
        sg = gu * (mix[...] + jnp.concatenate([be_ref[...]] * (tm // CHUNK), axis=0))
        o_ref[:, ATTN_W:ATTN_W + SGU_W] = (_silu_and_grad(zb_ref[...])[0] * sg).astype(BF16)
        p = _mem_attn_probs(qm_ref[...], kv_ref[:, 0:MEM_W], tm)[0]
        mo = _unstack_mem_heads(_dot(p, kv_ref[:, MEM_W:2 * MEM_W]), tm)
        o_ref[:, ATTN_W + SGU_W:D_MODEL] = (_silu_and_grad(zm_ref[...])[0] * mo).astype(BF16)

    full = lambda shape: pl.BlockSpec(shape, lambda i: (0,) * len(shape))
    return pl.pallas_call(
        body, name="branch_fwd", grid=(T_LOC // tm,),
        in_specs=_branch_blocks(tm) + [
            pl.BlockSpec((tm, ATTN_W), lambda i: (i, 0)), pl.BlockSpec((N_MEM, 2 * MEM_W), lambda i: (i // per_ex, 0)),
            full((4, CHUNK, CHUNK)), full((CHUNK, SGU_W)), full((1, SGU_W))],
        out_specs=pl.BlockSpec((tm, D_MODEL), lambda i: (i, 0)),
        out_shape=SDS((T_LOC, D_MODEL), BF16),
        scratch_shapes=[pltpu.VMEM((tm, SGU_W), F32)],
        compiler_params=_params(VMEM_NO_STAGING_MIB, ("arbitrary",)),
    )(proj, proj, proj, proj, proj, proj, a, kv, w_s, b_exp, g_v)


def _outproj_loss(gated, wout, x2, tgt2, g_final):
    tm = 512

    def body(g_ref, w_ref, x_ref, t_ref, gf_ref, dh2_ref, loss_ref, dgf_ref, dwo_ref, dwo_acc):
        @pl.when(pl.program_id(0) == 0)
        def _():
            loss_ref[...] = jnp.zeros_like(loss_ref)
            dgf_ref[...] = jnp.zeros_like(dgf_ref)
            dwo_acc[...] = jnp.zeros_like(dwo_acc)
        gated = g_ref[...]
        h2 = x_ref[...] + _dot(gated, w_ref[...])
        r = _rstd(h2)
        gf = gf_ref[...]
        err = h2 * r * gf - t_ref[...]
        loss_ref[...] += 0.5 * jnp.sum(jnp.mean(err * err, axis=-1, keepdims=True))
        dy = err * (1.0 / D_MODEL)
        dh2 = _rms_bwd(h2, r, gf, dy)
        dh2_ref[...] = dh2
        dgf_ref[...] += jnp.sum(dy * (h2 * r), axis=0, keepdims=True)
        dwo_acc[...] += _dot_tn(gated, dh2)

        @pl.when(pl.program_id(0) == pl.num_programs(0) - 1)
        def _():
            _cast_rows(dwo_acc, dwo_ref, D_MODEL)

    row = pl.BlockSpec((tm, D_MODEL), lambda i: (i, 0))
    vec = pl.BlockSpec((1, D_MODEL), lambda i: (0, 0))
    square = pl.BlockSpec((D_MODEL, D_MODEL), lambda i: (0, 0))
    return pl.pallas_call(
        body, name="outproj_loss", grid=(T_LOC // tm,),
        in_specs=[row, square, row, row, vec],
        out_specs=(row, pl.BlockSpec((8, 128), lambda i: (0, 0)), vec, square),
        out_shape=(SDS((T_LOC, D_MODEL), F32), SDS((8, 128), F32), SDS((1, D_MODEL), F32), SDS((D_MODEL, D_MODEL), BF16)),
        scratch_shapes=[pltpu.VMEM((D_MODEL, D_MODEL), F32)],
        compiler_params=_params(VMEM_NO_STAGING_MIB, ("arbitrary",)),
    )(gated, wout, x2, tgt2, g_final)


def _branch_bwd(dh2, wout, proj, a, kv, w_s, b_exp, g_v):
    tm = 512
    per_ex = SEQ // tm

    def body(dh2_ref, w_ref, za_ref, ub_ref, vb_ref, zb_ref, qm_ref, zm_ref, a_ref, kv_ref, ws_ref,
             be_ref, gv_ref, da_ref, dr_ref, dkv_ref, dws_ref, db_ref, dgv_ref, mix, dvn, dmsum):
        i = pl.program_id(0)

        @pl.when(i == 0)
        def _():
            dws_ref[...] = jnp.zeros_like(dws_ref)
            dgv_ref[...] = jnp.zeros_like(dgv_ref)
            dmsum[...] = jnp.zeros_like(dmsum)

        @pl.when(i % per_ex == 0)
        def _():
            dkv_ref[...] = jnp.zeros_like(dkv_ref)

        dg = _dot_nt(dh2_ref[...], w_ref[...])

        sa, dsa = _silu_and_grad(za_ref[...])
        dga = dg[:, 0:ATTN_W]
        da_ref[...] = dga * sa
        dr_ref[:, 0:512] = (dga * a_ref[...] * dsa).astype(BF16)

        ub, vb = ub_ref[...], vb_ref[...]
        gu, dgu = _gelu_and_grad(ub)
        gv, dgv = _gelu_and_grad(vb)
        rv = _rstd(gv)
        gain = gv_ref[...]
        vn = (gv * rv * gain).astype(BF16)
        _sgu_mix(vn, ws_ref, mix, tm)
        mixed = mix[...] + jnp.concatenate([be_ref[...]] * (tm // CHUNK), axis=0)
        sb, dsb = _silu_and_grad(zb_ref[...])
        dgb = dg[:, ATTN_W:ATTN_W + SGU_W]
        dsg = dgb * sb
        dr_ref[:, 512:768] = (dsg * mixed * dgu).astype(BF16)
        dr_ref[:, 1024:1280] = (dgb * (gu * mixed) * dsb).astype(BF16)
        dmix = dsg * gu
        for ch in range(tm // CHUNK):
            dmsum[...] += dmix[ch * CHUNK:(ch + 1) * CHUNK, :]
        first = _first_group_lanes(tm)
        for pr in range(2):
            dmp, vp = _chunks_side_by_side(dmix, pr, tm), _chunks_side_by_side(vn, pr, tm)
            dws_ref[2 * pr] += _dot_nt(jnp.where(first, dmp, 0.0), vp)
            dws_ref[2 * pr + 1] += _dot_nt(jnp.where(first, 0.0, dmp), vp)
            _store_chunks(dvn, pr, jnp.where(first, _dot_tn(ws_ref[2 * pr], dmp), _dot_tn(ws_ref[2 * pr + 1], dmp)), tm)
        dvn_v = dvn[...]
        dgv_ref[...] += jnp.sum(dvn_v * (gv * rv), axis=0, keepdims=True)
        dr_ref[:, 768:1024] = (_rms_bwd(gv, rv, gain, dvn_v) * dgv).astype(BF16)

        szm, dszm = _silu_and_grad(zm_ref[...])
        dgm = dg[:, ATTN_W + SGU_W:D_MODEL]
        kmem, vmem_ = kv_ref[:, 0:MEM_W].astype(BF16), kv_ref[:, MEM_W:2 * MEM_W].astype(BF16)
        p, qs = _mem_attn_probs(qm_ref[...], kmem, tm)
        dmos = _stack_mem_heads(dgm * szm, tm).astype(BF16)
        dr_ref[:, 1536:1792] = (dgm * _unstack_mem_heads(_dot(p, vmem_), tm) * dszm).astype(BF16)
        dp = _dot_nt(dmos, vmem_)
        ds = (p * (dp - jnp.sum(p * dp, axis=-1, keepdims=True)) * SCALE).astype(BF16)
        dr_ref[:, 1280:1536] = _unstack_mem_heads(_dot(ds, kmem), tm).astype(BF16)
        dkv_ref[:, 0:MEM_W] += _dot_tn(ds, qs)
        dkv_ref[:, MEM_W:2 * MEM_W] += _dot_tn(p, dmos)

        @pl.when(i == pl.num_programs(0) - 1)
        def _():
            tot = dmsum[...]
            hi = tot.astype(BF16)
            lo = (tot - hi.astype(F32)).astype(BF16)
            grp = (lax.broadcasted_iota(jnp.int32, (SGU_W, 128), 0) // HEAD
                   == lax.broadcasted_iota(jnp.int32, (SGU_W, 128), 1)).astype(BF16)
            db_ref[...] = (_dot(hi, grp) + _dot(lo, grp)).T[0:4, :]

    full = lambda shape: pl.BlockSpec(shape, lambda i: (0,) * len(shape))
    row = lambda w: pl.BlockSpec((tm, w), lambda i: (i, 0))
    return pl.pallas_call(
        body, name="branch_bwd", grid=(T_LOC // tm,),
        in_specs=[row(D_MODEL), full((D_MODEL, D_MODEL))] + _branch_blocks(tm) + [
            row(ATTN_W), pl.BlockSpec((N_MEM, 2 * MEM_W), lambda i: (i // per_ex, 0)),
            full((4, CHUNK, CHUNK)), full((CHUNK, SGU_W)), full((1, SGU_W))],
        out_specs=(row(ATTN_W), row(REST_W), pl.BlockSpec((N_MEM, 2 * MEM_W), lambda i: (i // per_ex, 0)),
                   full((4, CHUNK, CHUNK)), full((4, CHUNK)), full((1, SGU_W))),
        out_shape=(SDS((T_LOC, ATTN_W), F32), SDS((T_LOC, REST_W), BF16), SDS((B_LOC * N_MEM, 2 * MEM_W), F32),
                   SDS((4, CHUNK, CHUNK), F32), SDS((4, CHUNK), F32), SDS((1, SGU_W), F32)),
        scratch_shapes=[pltpu.VMEM((tm, SGU_W), F32), pltpu.VMEM((tm, SGU_W), F32), pltpu.VMEM((CHUNK, SGU_W), F32)],
        compiler_params=_params(56, ("arbitrary",)),
    )(dh2, wout, proj, proj, proj, proj, proj, proj, a, kv, w_s, b_exp, g_v)


def _attn_bwd(proj, qkv16, slopes, da, a, lse, token):
    def body(sl_ref, q_ref, k_ref, v_ref, q16_ref, k16_ref, v16_ref, da_ref, a_ref, lse_ref, _,
             dq_ref, dk_ref, dv_ref, *scr):
        dq_s, dk_s, dv_s, tab = scr[0:3], scr[3:6], scr[6:9], scr[9]
        lse_h, delta_h = scr[10:12], scr[12:14]
        p_all, ds_all = scr[14], scr[15]
        left = _left_lanes(CHUNK)
        _fill_bias_tables(sl_ref, tab)

        def prep(j, carry):
            rows = pl.ds(pl.multiple_of(j * 256, 256), 256)
            l256 = _left_lanes(256)
            prod = da_ref[rows, :] * a_ref[rows, :]
            delta_h[0][rows, :] = jnp.broadcast_to(jnp.sum(jnp.where(l256, prod, 0.0), axis=-1, keepdims=True), (256, 128))
            delta_h[1][rows, :] = jnp.broadcast_to(jnp.sum(jnp.where(l256, 0.0, prod), axis=-1, keepdims=True), (256, 128))
            pair = lse_ref[rows, :]
            other = pltpu.roll(pair, HEAD, axis=1)
            lse_h[0][rows, :] = jnp.where(l256, pair, other)
            lse_h[1][rows, :] = jnp.where(l256, other, pair)
            zero = jnp.zeros((256, 128), F32)
            for cfg in range(3):
                dk_s[cfg][rows, :] = zero
                dv_s[cfg][rows, :] = zero
            return carry
        lax.fori_loop(0, SEQ // 256, prep, 0)

        def per_row(halves, rq, nk):
            v = jnp.concatenate([halves[0][rq, :], halves[1][rq, :]], axis=0)
            return v if nk == 128 else jnp.concatenate([v, v], axis=1)

        def qkv(cfg, rq, rk, t):
            if cfg == 2:
                return q16_ref[0, t].astype(F32), k16_ref[0, t], v16_ref[0, t]
            return q_ref[rq, :], k_ref[rk, :], v_ref[rk, :]

        def probs(cfg, var, q0, k0, nk, t):
            d = DILATIONS[cfg][0]
            rq, rk = _rows(q0, CHUNK, d), _rows(k0, nk, d)
            qb, kw, vw = qkv(cfg, rq, rk, t)
            qs = _stack_heads(qb * SCALE, left)
            das = _stack_heads(da_ref[rq, :], left)
            s = _dot_nt(qs, kw) + tab[3 * cfg + var, :, 0:nk]
            p = jnp.exp(s - per_row(lse_h, rq, nk))
            p_all[16 * cfg + t, :, 0:nk] = p.astype(BF16)
            ds_all[16 * cfg + t, :, 0:nk] = (p * (_dot_nt(das, vw) - per_row(delta_h, rq, nk))).astype(BF16)
        _attn_blocks(probs, 16)

        def grads(cfg, var, q0, k0, nk, t):
            d = DILATIONS[cfg][0]
            rq, rk = _rows(q0, CHUNK, d), _rows(k0, nk, d)
            qb, kw, _ = qkv(cfg, rq, rk, t)
            qs = _stack_heads(qb * SCALE, left).astype(BF16)
            das = _stack_heads(da_ref[rq, :], left).astype(BF16)
            p, ds = p_all[16 * cfg + t, :, 0:nk], ds_all[16 * cfg + t, :, 0:nk]
            dq_s[cfg][rq, :] = _unstack_heads(_dot(ds, kw), left) * SCALE
            dk_s[cfg][rk, :] += _dot_tn(ds, qs)
            dv_s[cfg][rk, :] += _dot_tn(p, das)
        _attn_blocks(grads, 8)

        def flush(j, carry):
            rows = pl.ds(pl.multiple_of(j * 256, 256), 256)
            for acc, dst in ((dq_s, dq_ref), (dk_s, dk_ref), (dv_s, dv_ref)):
                dst[rows, :] = (acc[0][rows, :] + acc[1][rows, :] + acc[2][rows, :]).astype(BF16)
            return carry
        lax.fori_loop(0, SEQ // 256, flush, 0)

    blk = lambda col0: pl.BlockSpec((SEQ, 128), lambda b, hp: (b, col0 // 128 + hp))
    own = pl.BlockSpec((SEQ, 128), lambda b, hp: (b, hp))
    return pl.pallas_call(
        body, name="attn_bwd", grid=(B_LOC, 4),
        in_specs=[pl.BlockSpec((1, 8, 128), lambda b, hp: (hp, 0, 0)), blk(C_QA), blk(C_KA), blk(C_VA),
                  _blk16(C_QA), _blk16(C_KA), _blk16(C_VA), own, own, own, _TOKEN],
        out_specs=(own, own, own),
        out_shape=(SDS((T_LOC, ATTN_W), BF16),) * 3,
        scratch_shapes=[pltpu.VMEM((SEQ, 128), F32)] * 9 + [pltpu.VMEM((N_BIAS, 2 * CHUNK, 2 * CHUNK), F32)]
        + [pltpu.VMEM((SEQ, 128), F32)] * 4 + [pltpu.VMEM((48, 2 * CHUNK, 2 * CHUNK), BF16)] * 2,
        compiler_params=_params(52, ("arbitrary", "arbitrary")),
    )(slopes, proj, proj, proj, qkv16, qkv16, qkv16, da, a, lse, token)


def _dproj_specs(tm):
    third = pl.BlockSpec((tm, ATTN_W), lambda i: (i, 0))
    return [third, third, third, pl.BlockSpec((tm, REST_W), lambda i: (i, 0))]


def _dx(dq, dk, dv, dr, wint, x2, dh2, g_norm, token):
    tm = 512

    def body(dq_ref, dk_ref, dv_ref, dr_ref, w_ref, x_ref, dh2_ref, g_ref, _, gx_ref, dgn_ref):
        @pl.when(pl.program_id(0) == 0)
        def _():
            dgn_ref[...] = jnp.zeros_like(dgn_ref)
        dh = (_dot(dq_ref[...], w_ref[C_QA:C_KA, :]) + _dot(dk_ref[...], w_ref[C_KA:C_VA, :])
              + _dot(dv_ref[...], w_ref[C_VA:C_ZA, :]) + _dot(dr_ref[...], w_ref[C_ZA:IN_COLS, :]))
        xv = x_ref[...]
        r = _rstd(xv)
        gx_ref[...] = dh2_ref[...] + _rms_bwd(xv, r, g_ref[...], dh)
        dgn_ref[...] += jnp.sum(dh * (xv * r), axis=0, keepdims=True)

    row = pl.BlockSpec((tm, D_MODEL), lambda i: (i, 0))
    vec = pl.BlockSpec((1, D_MODEL), lambda i: (0, 0))
    return pl.pallas_call(
        body, name="dx", grid=(T_LOC // tm,),
        in_specs=_dproj_specs(tm) + [pl.BlockSpec((IN_COLS, D_MODEL), lambda i: (0, 0)), row, row, vec, _TOKEN],
        out_specs=(row, vec),
        out_shape=(SDS((T_LOC, D_MODEL), F32), SDS((1, D_MODEL), F32)),
        compiler_params=_params(48, ("arbitrary",)),
    )(dq, dk, dv, dr, wint, x2, dh2, g_norm, token)


def _dwin(dq, dk, dv, dr, x2, g_norm, half, token):
    tm = 1024
    width = D_MODEL // 2
    cols = slice(half * width, (half + 1) * width)

    def body(dq_ref, dk_ref, dv_ref, dr_ref, x_ref, g_ref, _, o_ref, acc):
        @pl.when(pl.program_id(0) == 0)
        def _():
            acc[...] = jnp.zeros_like(acc)
        xv = x_ref[...]
        h = (xv[:, cols] * _rstd(xv) * g_ref[:, cols]).astype(BF16)
        acc[C_QA:C_KA, :] += _dot_tn(dq_ref[...], h)
        acc[C_KA:C_VA, :] += _dot_tn(dk_ref[...], h)
        acc[C_VA:C_ZA, :] += _dot_tn(dv_ref[...], h)
        acc[C_ZA:IN_COLS, :] += _dot_tn(dr_ref[...], h)

        @pl.when(pl.program_id(0) == pl.num_programs(0) - 1)
        def _():
            _cast_rows(acc, o_ref, IN_COLS)

    return pl.pallas_call(
        body, name="dwin%d" % half, grid=(T_LOC // tm,),
        in_specs=_dproj_specs(tm) + [pl.BlockSpec((tm, D_MODEL), lambda i: (i, 0)),
                                     pl.BlockSpec((1, D_MODEL), lambda i: (0, 0)), _TOKEN],
        out_specs=pl.BlockSpec((IN_COLS, width), lambda i: (0, 0)),
        out_shape=SDS((IN_COLS, width), BF16),
        scratch_shapes=[pltpu.VMEM((IN_COLS, width), F32)],
        compiler_params=_params(48, ("arbitrary",)),
    )(dq, dk, dv, dr, x2, g_norm, token)


def _memkv_bwd(dkv, mem2, g_mem, wkv):
    def body(dkv_ref, m_ref, g_ref, w_ref, dw_ref, dg_ref):
        mv = m_ref[...]
        r = _rstd(mv)
        dkv_v = dkv_ref[...].astype(BF16)
        dw_ref[...] = _dot_tn(mv * r * g_ref[...], dkv_v).astype(BF16)
        dg_ref[...] = jnp.sum(_dot_nt(dkv_v, w_ref[...]) * (mv * r), axis=0, keepdims=True)

    return pl.pallas_call(
        body, name="memkv_bwd", out_shape=(SDS((D_MODEL, 2 * MEM_W), BF16), SDS((1, D_MODEL), F32)),
        compiler_params=_params(32),
    )(dkv, mem2, g_mem, wkv)


def _allreduce_small(parts):
    n = len(parts)

    def body(*refs):
        ins, outs, bufs = refs[0:n], refs[n:2 * n], refs[2 * n:3 * n]
        send_sems, recv_sems = refs[3 * n], refs[3 * n + 1]
        pos = _mesh_pos()
        me = _flat(pos)
        for a in range(n):
            bufs[a][me] = ins[a][...]

        def copy(a, k, slot):
            return pltpu.make_async_remote_copy(
                src_ref=ins[a], dst_ref=bufs[a].at[slot],
                send_sem=send_sems.at[7 * a + k - 1], recv_sem=recv_sems.at[7 * a + k - 1],
                device_id=_peer(pos, k), device_id_type=MESH)

        sent = [copy(a, k, me) for a in range(n) for k in range(1, N_DEV)]
        for cp in sent:
            cp.start()
        for a in range(n):
            for k in range(1, N_DEV):
                copy(a, k, _flat(_peer(pos, k))).wait_recv()
        for cp in sent:
            cp.wait_send()
        for a in range(n):
            acc = bufs[a][0]
            for s in range(1, N_DEV):
                acc = acc + bufs[a][s]
            outs[a][...] = acc

    vmem = pl.BlockSpec(memory_space=pltpu.VMEM)
    return pl.pallas_call(
        body, name="allreduce_small",
        out_shape=tuple(SDS(p.shape, F32) for p in parts),
        in_specs=[vmem] * n, out_specs=(vmem,) * n,
        scratch_shapes=[pltpu.VMEM((N_DEV,) + p.shape, F32) for p in parts]
        + [pltpu.SemaphoreType.DMA((7 * n,)), pltpu.SemaphoreType.DMA((7 * n,))],
        compiler_params=_params(16),
    )(*parts)


_HBM = pl.BlockSpec(memory_space=pltpu.HBM)
_SEM = pl.BlockSpec(memory_space=pltpu.SEMAPHORE)
_SIDE_EFFECT = pltpu.SideEffectType.DATAFLOW_SIDE_EFFECTING


def _exchange_copies(src_refs, land_refs, scatter, send_sems, recv_sems):
    pos = _mesh_pos()
    copies = []
    for a, (src, land) in enumerate(zip(src_refs, land_refs)):
        n = land.shape[1]
        for k in range(1, N_DEV):
            peer = _peer(pos, k)
            piece = src.at[pl.ds(pl.multiple_of(_flat(peer) * n, 16), n), :] if scatter[a] else src
            copies.append(pltpu.make_async_remote_copy(
                src_ref=piece, dst_ref=land.at[_flat(pos)],
                send_sem=send_sems.at[7 * a + k - 1], recv_sem=recv_sems.at[7 * a + k - 1],
                device_id=peer, device_id_type=MESH))
    return copies


def _own_copies(src_refs, land_refs, scatter, send_sems):
    n = len(src_refs)
    me = _flat(_mesh_pos())
    copies = []
    for a, (src, land) in enumerate(zip(src_refs, land_refs)):
        rows = land.shape[1]
        piece = src.at[pl.ds(pl.multiple_of(me * rows, 16), rows), :] if scatter[a] else src
        copies.append(pltpu.make_async_copy(piece, land.at[me], send_sems.at[7 * n + a]))
    return copies


def _exchange_start(name, srcs, scatter):
    n = len(srcs)

    def body(*refs):
        for cp in _exchange_copies(refs[0:n], refs[n:2 * n], scatter, refs[2 * n], refs[2 * n + 1]):
            cp.start()
        for cp in _own_copies(refs[0:n], refs[n:2 * n], scatter, refs[2 * n]):
            cp.start()
        refs[-1][...] = jnp.zeros_like(refs[-1])

    lands = [lax.empty((N_DEV,) + (t.shape[0] // N_DEV if sc else t.shape[0],) + t.shape[1:], t.dtype)
             for t, sc in zip(srcs, scatter)]
    ops = [pltpu.with_memory_space_constraint(t, pltpu.HBM) for t in (*srcs, *lands)]
    out = pl.pallas_call(
        body, name=name,
        out_shape=(pltpu.SemaphoreType.DMA((8 * n,)), pltpu.SemaphoreType.DMA((7 * n,)),
                   *[pltpu.HBM(t.shape, t.dtype) for t in ops], SDS((8, 128), F32)),
        in_specs=[_HBM] * (2 * n),
        out_specs=(_SEM, _SEM, *[_HBM] * (2 * n), pl.BlockSpec(memory_space=pltpu.VMEM)),
        input_output_aliases={i: 2 + i for i in range(2 * n)},
        compiler_params=pltpu.CompilerParams(has_side_effects=_SIDE_EFFECT),
    )(*ops)
    return out[0], out[1], out[2:2 + n], out[2 + n:2 + 2 * n], out[-1]


def _exchange_wait(name, started, scatter, after):
    send_sems, recv_sems, srcs, lands, _ = started
    n = len(srcs)

    def body(*refs):
        for cp in _exchange_copies(refs[0:n], refs[n:2 * n], scatter, refs[2 * n], refs[2 * n + 1]):
            cp.wait_send()
            cp.wait_recv()
        for cp in _own_copies(refs[0:n], refs[n:2 * n], scatter, refs[2 * n]):
            cp.wait()

    out = pl.pallas_call(
        body, name=name,
        out_shape=tuple(pltpu.HBM(t.shape, t.dtype) for t in (*srcs, *lands)),
        in_specs=[_HBM] * (2 * n) + [_SEM, _SEM, pl.BlockSpec(memory_space=pl.ANY)],
        out_specs=(_HBM,) * (2 * n),
        input_output_aliases={i: i for i in range(2 * n)},
        compiler_params=pltpu.CompilerParams(has_side_effects=_SIDE_EFFECT),
    )(*srcs, *lands, send_sems, recv_sems, after)
    return out[n:]


def _adamw(w, g, m, v):
    m = ADAM_B1 * m + (1.0 - ADAM_B1) * g
    v = ADAM_B2 * v + (1.0 - ADAM_B2) * (g * g)
    m_hat = m / (1.0 - ADAM_B1 ** ADAM_STEP)
    v_hat = v / (1.0 - ADAM_B2 ** ADAM_STEP)
    return -ADAM_LR * (m_hat / (jnp.sqrt(v_hat) + ADAM_EPS) + ADAM_WD * w), m, v


def _adam_slots(name, pieces, w, m, v):
    rows, cols = w.shape
    starts = [sum(p.shape[2] for p in pieces[:i]) for i in range(len(pieces) + 1)]
    assert starts[-1] == cols and all(p.shape[1] == rows for p in pieces)

    def body(*refs):
        s_refs, (w_ref, m_ref, v_ref, g_o, d_o, m_o, v_o, acc) = refs[:len(pieces)], refs[len(pieces):]
        s = pl.program_id(0)

        @pl.when(s == 0)
        def _():
            for i, s_ref in enumerate(s_refs):
                acc[:, starts[i]:starts[i + 1]] = s_ref[0].astype(F32)

        @pl.when(s > 0)
        def _():
            for i, s_ref in enumerate(s_refs):
                acc[:, starts[i]:starts[i + 1]] += s_ref[0].astype(F32)

        @pl.when(s == N_DEV - 1)
        def _():
            g = acc[...]
            g_o[...] = g
            d_o[...], m_o[...], v_o[...] = _adamw(w_ref[...], g, m_ref[...], v_ref[...])

    full = pl.BlockSpec((rows, cols), lambda s: (0, 0))
    return pl.pallas_call(
        body, name="adam_" + name, grid=(N_DEV,),
        in_specs=[pl.BlockSpec((1, rows, p.shape[2]), lambda s: (s, 0, 0)) for p in pieces] + [full, full, full],
        out_specs=(full,) * 4, out_shape=(SDS((rows, cols), F32),) * 4,
        scratch_shapes=[pltpu.VMEM((rows, cols), F32)],
        compiler_params=_params(40, ("arbitrary",)),
    )(*pieces, w, m, v)


def _adam_small(ws, gs, ms, vs, loss_slots):
    n = len(ws)

    def total(ref, like):
        if len(ref.shape) == len(like.shape):
            return ref[...]
        acc = ref[0]
        for s in range(1, N_DEV):
            acc = acc + ref[s]
        return acc

    def body(*refs):
        w_r, g_r, m_r, v_r = refs[0:n], refs[n:2 * n], refs[2 * n:3 * n], refs[3 * n:4 * n]
        loss_r, outs = refs[4 * n], refs[4 * n + 1:]
        for a in range(n):
            g = total(g_r[a], w_r[a])
            outs[a][...] = g
            outs[n + 1 + 3 * a][...], outs[n + 2 + 3 * a][...], outs[n + 3 + 3 * a][...] = _adamw(
                w_r[a][...], g, m_r[a][...], v_r[a][...])
        outs[n][...] = total(loss_r, outs[n])

    out = pl.pallas_call(
        body, name="adam_small",
        out_shape=tuple(SDS(w.shape, F32) for w in ws) + (SDS(loss_slots.shape[1:], F32),)
        + tuple(SDS(w.shape, F32) for w in ws for _ in range(3)),
        compiler_params=_params(16),
    )(*ws, *gs, *ms, *vs, loss_slots)
    return out[0:n], out[n], out[n + 1:]


def kernel(x, mem, g_norm, w_in, w_sgu_spatial, b_sgu_spatial, g_sgu_v, g_mem, w_mem_kv, w_out, g_final, loss_target, m_g_norm, m_w_in, m_w_sgu_spatial, m_b_sgu_spatial, m_g_sgu_v, m_g_mem, m_w_mem_kv, m_w_out, m_g_final, v_g_norm, v_w_in, v_w_sgu_spatial, v_b_sgu_spatial, v_g_sgu_v, v_g_mem, v_w_mem_kv, v_w_out, v_g_final):
    x2 = x.reshape(T_LOC, D_MODEL)
    tgt2 = loss_target.reshape(T_LOC, D_MODEL)
    mem2 = mem.reshape(B_LOC * N_MEM, D_MODEL)
    w_s = w_sgu_spatial[0]
    b_exp = jnp.repeat(b_sgu_spatial[0].T, HEAD, axis=1)
    slope = jnp.power(2.0, -8.0 * (jnp.arange(8, dtype=F32) + 1.0) / 8)
    slopes = jnp.broadcast_to(jnp.repeat(slope.reshape(4, 2), HEAD, axis=1)[:, None, :], (4, 8, 128))

    tr = lambda t: jnp.transpose(t[0])

    wint, wkv_own, wout_own = _allgather_weights(tr(w_in), w_mem_kv[0], w_out[0])
    started0 = _exchange_start("exchange0_start", [wkv_own, wout_own], [False, False])
    proj, qkv16 = _proj_fwd(x2, g_norm, wint, started0[4])
    wkv, wout = _exchange_wait("exchange0_wait", started0, [False, False], proj)
    wkv, wout = wkv.reshape(D_MODEL, 2 * MEM_W), wout.reshape(D_MODEL, D_MODEL)
    kv = _memkv_fwd(mem2, g_mem, wkv)
    a, lse = _attn_fwd(proj, qkv16, slopes)
    gated = _branch_fwd(proj, a, kv, w_s, b_exp, g_sgu_v)
    dh2, loss8, dgf, dwout = _outproj_loss(gated, wout, x2, tgt2, g_final.reshape(1, D_MODEL))

    da, dr, dkv, dws, dbs, dgv = _branch_bwd(dh2, wout, proj, a, kv, w_s, b_exp, g_sgu_v)
    dwkv, dgm = _memkv_bwd(dkv, mem2, g_mem, wkv)

    early = [dws.reshape(4 * CHUNK, CHUNK), dbs, dgv, dgm, dgf, loss8]
    scatter1 = [True, True] + [False] * len(early)
    started1 = _exchange_start("exchange1_start", [dwkv, dwout] + early, scatter1)
    dq, dk, dv = _attn_bwd(proj, qkv16, slopes, da, a, lse, started1[4])
    s_wkv, s_wout, s_ws, s_bs, s_gv, s_gm, s_gf, s_loss = _exchange_wait("exchange1_wait", started1, scatter1, dq)

    dwint0 = _dwin(dq, dk, dv, dr, x2, g_norm, 0, started1[4])
    started2 = _exchange_start("exchange2_start", [dwint0], [True])
    dwint1 = _dwin(dq, dk, dv, dr, x2, g_norm, 1, started2[4])
    started3 = _exchange_start("exchange3_start", [dwint1], [True])
    grad_x, dgn = _dx(dq, dk, dv, dr, wint, x2, dh2, g_norm, started3[4])
    s_win0, = _exchange_wait("exchange2_wait", started2, [True], grad_x)
    dgn_sum, = _allreduce_small([dgn])
    s_win1, = _exchange_wait("exchange3_wait", started3, [True], dgn_sum)

    g_win, d_win, m_win, v_win = map(
        jnp.transpose, _adam_slots("w_in", [s_win0, s_win1], tr(w_in), tr(m_w_in), tr(v_w_in)))
    g_wkv, d_wkv, m_wkv, v_wkv = _adam_slots("w_mem_kv", [s_wkv], w_mem_kv[0], m_w_mem_kv[0], v_w_mem_kv[0])
    g_wout, d_wout, m_wout, v_wout = _adam_slots("w_out", [s_wout], w_out[0], m_w_out[0], v_w_out[0])

    small_shapes = [(1, D_MODEL), (4 * CHUNK, CHUNK), (4, CHUNK), (1, SGU_W), (1, D_MODEL), (1, D_MODEL)]
    pack = lambda arrs: [t.reshape(s) for t, s in zip(arrs, small_shapes)]
    g_small, loss_sum, upd = _adam_small(
        pack([g_norm, w_sgu_spatial, b_sgu_spatial, g_sgu_v, g_mem, g_final]),
        [dgn_sum, s_ws, s_bs, s_gv, s_gm, s_gf],
        pack([m_g_norm, m_w_sgu_spatial, m_b_sgu_spatial, m_g_sgu_v, m_g_mem, m_g_final]),
        pack([v_g_norm, v_w_sgu_spatial, v_b_sgu_spatial, v_g_sgu_v, v_g_mem, v_g_final]), s_loss)
    out_shapes = [g_norm.shape, w_sgu_spatial.shape, b_sgu_spatial.shape, g_sgu_v.shape, g_mem.shape, g_final.shape]
    unpack = lambda arrs: [t.reshape(s) for t, s in zip(arrs, out_shapes)]
    gs = unpack(g_small)
    ds, nms, nvs = unpack(upd[0::3]), unpack(upd[1::3]), unpack(upd[2::3])

    loss = loss_sum[0, 0]

    def assemble(small, win, wkv_, wout_):
        return [small[0], win[None], small[1], small[2], small[3], small[4], wkv_[None], wout_[None], small[5]]

    return (loss, grad_x.reshape(x.shape),
            *assemble(gs, g_win, g_wkv, g_wout), *assemble(ds, d_win, d_wkv, d_wout),
            *assemble(nms, m_win, m_wkv, m_wout), *assemble(nvs, v_win, v_wkv, v_wout))
```

```python
import jax
import jax.numpy as jnp
from jax import lax
from jax.experimental import pallas as pl
from jax.experimental.pallas import tpu as pltpu

F32 = jnp.float32
BF16 = jnp.bfloat16
SDS = jax.ShapeDtypeStruct
MESH = pl.DeviceIdType.MESH

N_DEV = 8
D_MODEL = 1024
SEQ = 2048
B_LOC = 2
T_LOC = B_LOC * SEQ
N_MEM = 256
HEAD = 64
ATTN_W = 512
SGU_W = 256
MEM_W = 256
IN_COLS = 3328
W_IN_SHARD = IN_COLS // N_DEV
CHUNK = 128
DILATIONS = ((1, 2048), (4, 512), (16, 128))
RADIUS = 64
EPS = 1e-6
NEG = -1e30
SCALE = HEAD ** -0.5
C_QA, C_KA, C_VA, C_ZA, C_UB, C_VB, C_ZB, C_QM, C_ZM = 0, 512, 1024, 1536, 2048, 2304, 2560, 2816, 3072
QKV_W = 1536
REST_W = IN_COLS - QKV_W

ADAM_LR, ADAM_B1, ADAM_B2, ADAM_EPS, ADAM_WD, ADAM_STEP = 0.001, 0.9, 0.999, 1e-08, 0.01, 10

V7X_VMEM_MIB = 64
VMEM_NO_STAGING_MIB = V7X_VMEM_MIB - 6


def _params(vmem_mib, sem=None):
    assert vmem_mib < V7X_VMEM_MIB
    return pltpu.CompilerParams(vmem_limit_bytes=vmem_mib << 20, dimension_semantics=sem)


_TOKEN = pl.BlockSpec(memory_space=pl.ANY)


def _dot(a, b):
    return jnp.dot(a.astype(BF16), b.astype(BF16), preferred_element_type=F32)


def _dot_nt(a, b):
    return lax.dot_general(a.astype(BF16), b.astype(BF16), (((1,), (1,)), ((), ())), preferred_element_type=F32)


def _dot_tn(a, b):
    return lax.dot_general(a.astype(BF16), b.astype(BF16), (((0,), (0,)), ((), ())), preferred_element_type=F32)


def _rstd(v):
    return lax.rsqrt(jnp.mean(v * v, axis=-1, keepdims=True) + EPS)


def _rms_bwd(v, r, g, dy):
    gdy = g * dy
    return r * gdy - v * (r * r * r * jnp.mean(gdy * v, axis=-1, keepdims=True))


def _sigmoid(z):
    return 1.0 / (1.0 + jnp.exp(-z))


def _silu_and_grad(z):
    s = _sigmoid(z)
    return z * s, s * (1.0 + z * (1.0 - s))


_G_C = 0.7978845608028654
_G_K = 0.044715


def _gelu_and_grad(v):
    t = jnp.tanh(_G_C * (v + _G_K * (v * v * v)))
    cdf = 0.5 * (1.0 + t)
    return v * cdf, cdf + 0.5 * v * (1.0 - t * t) * (_G_C * (1.0 + 3.0 * _G_K * v * v))


def _cast_rows(src_ref, dst_ref, rows, step=256):
    def one(i, carry):
        r = pl.ds(pl.multiple_of(i * step, step), step)
        dst_ref[r, :] = src_ref[r, :].astype(dst_ref.dtype)
        return carry
    lax.fori_loop(0, rows // step, one, 0)


def _left_lanes(rows):
    return lax.broadcasted_iota(jnp.int32, (rows, 128), 1) < HEAD


def _mesh_pos():
    return lax.axis_index("x"), lax.axis_index("y"), lax.axis_index("c")


def _peer(pos, k):
    x, y, c = pos
    return (1 - x if k & 4 else x, 1 - y if k & 2 else y, 1 - c if k & 1 else c)


def _flat(pos):
    return 4 * pos[0] + 2 * pos[1] + pos[2]


def _allgather_weights(w_in_t, w_kv, w_out, x2, g_norm):
    rows_h = 512
    n_h = T_LOC // rows_h

    def body(win_ref, wkv_ref, wout_ref, x_hbm, g_ref, wint_o, wkv_o, wout_o, h_hbm,
             send_sems, recv_sems, xbuf, hbuf, in_sems, out_sems):
        x, y, c = _mesh_pos()
        me, sib = (x, y, c), (x, y, 1 - c)
        chips = [(1 - x, y), (x, 1 - y), (1 - x, 1 - y)]

        def rows(p):
            return wint_o.at[pl.ds(pl.multiple_of(_flat(p) * W_IN_SHARD, 16), W_IN_SHARD), :]

        rows(me)[...] = win_ref[...].astype(BF16)

        def copy(k, block, to):
            return pltpu.make_async_remote_copy(
                src_ref=rows(block), dst_ref=rows(block), send_sem=send_sems.at[k], recv_sem=recv_sems.at[k],
                device_id=to, device_id_type=MESH)

        first = [copy(0, me, sib)] + [copy(1 + j, me, (*chip, c)) for j, chip in enumerate(chips)]
        for cp in first:
            cp.start()
        wkv_o[...] = wkv_ref[...].astype(BF16)
        wout_o[...] = wout_ref[...].astype(BF16)

        def fetch(i):
            return pltpu.make_async_copy(x_hbm.at[pl.ds(i * rows_h, rows_h), :], xbuf.at[i % 2], in_sems.at[i % 2])

        def put(i):
            return pltpu.make_async_copy(hbuf.at[i % 2], h_hbm.at[pl.ds(i * rows_h, rows_h), :], out_sems.at[i % 2])

        fetch(0).start()
        for i in range(n_h):
            if i + 1 < n_h:
                fetch(i + 1).start()
            fetch(i).wait()
            if i >= 2:
                put(i - 2).wait()
            xv = xbuf[i % 2]
            hbuf[i % 2] = (xv * _rstd(xv) * g_ref[...]).astype(BF16)
            put(i).start()
        put(n_h - 2).wait()
        put(n_h - 1).wait()

        passed = []
        for j, chip in enumerate(chips):
            copy(1 + j, (*chip, c), me).wait_recv()
            fwd = copy(4 + j, (*chip, c), sib)
            fwd.start()
            passed.append(fwd)
        copy(0, sib, me).wait_recv()
        for j, chip in enumerate(chips):
            copy(4 + j, (*chip, 1 - c), me).wait_recv()
        for cp in first + passed:
            cp.wait_send()

    vmem = pl.BlockSpec(memory_space=pltpu.VMEM)
    hbm = pl.BlockSpec(memory_space=pl.ANY)
    return pl.pallas_call(
        body, name="allgather_weights",
        out_shape=(SDS((IN_COLS, D_MODEL), BF16), SDS(w_kv.shape, BF16), SDS(w_out.shape, BF16),
                   SDS((T_LOC, D_MODEL), BF16)),
        in_specs=[vmem, vmem, vmem, hbm, vmem], out_specs=(vmem, vmem, vmem, hbm),
        scratch_shapes=[pltpu.SemaphoreType.DMA((7,)), pltpu.SemaphoreType.DMA((7,)),
                        pltpu.VMEM((2, rows_h, D_MODEL), F32), pltpu.VMEM((2, rows_h, D_MODEL), BF16),
                        pltpu.SemaphoreType.DMA((2,)), pltpu.SemaphoreType.DMA((2,))],
        compiler_params=_params(40),
    )(w_in_t, w_kv, w_out, x2, g_norm)


def _proj_fwd(h16, wint, token):
    tm = 512
    sub = 256
    d = DILATIONS[2][0]
    per_ex = SEQ // tm

    def body(h_ref, w_ref, _, o_ref, o16_ref):
        res = _dot_nt(h_ref[...], w_ref[...])
        o_ref[...] = res
        r_out = lax.broadcasted_iota(jnp.int32, (sub, sub), 0)
        r_in = lax.broadcasted_iota(jnp.int32, (sub, sub), 1)
        pick = (r_in == d * (r_out % (sub // d)) + r_out // (sub // d)).astype(BF16)
        for part in range(tm // sub):
            grouped = _dot(pick, res[part * sub:(part + 1) * sub, 0:QKV_W]).astype(BF16)
            for rho in range(d):
                o16_ref[0, rho, part * (sub // d):(part + 1) * (sub // d), :] = (
                    grouped[rho * (sub // d):(rho + 1) * (sub // d), :])

    return pl.pallas_call(
        body, name="proj_fwd", grid=(T_LOC // tm,),
        in_specs=[pl.BlockSpec((tm, D_MODEL), lambda i: (i, 0)),
                  pl.BlockSpec((IN_COLS, D_MODEL), lambda i: (0, 0)), _TOKEN],
        out_specs=(pl.BlockSpec((tm, IN_COLS), lambda i: (i, 0)),
                   pl.BlockSpec((1, d, tm // d, QKV_W), lambda i: (i // per_ex, 0, i % per_ex, 0))),
        out_shape=(SDS((T_LOC, IN_COLS), F32), SDS((B_LOC, d, SEQ // d, QKV_W), BF16)),
        compiler_params=_params(48, ("arbitrary",)),
    )(h16, wint, token)


def _memkv_fwd(mem2, g_mem, wkv):
    def body(m_ref, g_ref, w_ref, o_ref):
        mv = m_ref[...]
        o_ref[...] = _dot(mv * _rstd(mv) * g_ref[...], w_ref[...])

    return pl.pallas_call(
        body, name="memkv_fwd", out_shape=SDS((B_LOC * N_MEM, 2 * MEM_W), F32), compiler_params=_params(32),
    )(mem2, g_mem, wkv)


N_BIAS = 7


def _fill_bias_tables(sl_ref, tab):
    for cfg, (d, length) in enumerate(DILATIONS):
        nk = min(length, 2 * CHUNK)
        r = lax.broadcasted_iota(jnp.int32, (CHUNK, nk), 0)
        c = lax.broadcasted_iota(jnp.int32, (CHUNK, nk), 1)
        for var in range(3 if length > nk else 1):
            rel = jnp.abs(r - c + var * RADIUS)
            dist = rel.astype(F32) * float(d)
            for h in range(2):
                slope = sl_ref[0, 0:1, h * HEAD:h * HEAD + 1]
                tab[3 * cfg + var, h * CHUNK:(h + 1) * CHUNK, 0:nk] = jnp.where(rel <= RADIUS, -slope * dist, NEG)


def _attn_blocks(visit, unroll):
    def step(t, carry):
        for cfg, (d, length) in enumerate(DILATIONS):
            nblk = length // CHUNK
            if nblk == 1:
                visit(cfg, 0, t, t, length, t)
                continue
            rho, i = (0, t) if d == 1 else (t // nblk, t % nblk)
            ks = jnp.clip(i * CHUNK - RADIUS, 0, length - 2 * CHUNK)
            visit(cfg, (i * CHUNK - ks) // RADIUS, rho + d * (i * CHUNK), rho + d * ks, 2 * CHUNK, t)
        return carry
    lax.fori_loop(0, 16, step, 0, unroll=unroll)


def _stack_heads(v, left):
    return jnp.concatenate([jnp.where(left, v, 0.0), jnp.where(left, 0.0, v)], axis=0)


def _unstack_heads(v, left):
    return jnp.where(left, v[0:CHUNK], v[CHUNK:2 * CHUNK])


def _rows(start, n, d):
    return pl.ds(start, n) if d == 1 else pl.ds(start, n, stride=d)


def _blk16(col0):
    d, length = DILATIONS[2]
    return pl.BlockSpec((1, d, length, 128), lambda b, hp: (b, 0, 0, col0 // 128 + hp))


def _attn_fwd(proj, qkv16, slopes):
    def body(sl_ref, q_ref, k_ref, v_ref, q16_ref, k16_ref, v16_ref, a_ref, lse_ref, *scr):
        o_c, m_c, l_c, tab = scr[0:3], scr[3:6], scr[6:9], scr[9]
        left = _left_lanes(CHUNK)
        _fill_bias_tables(sl_ref, tab)

        def block(cfg, var, q0, k0, nk, t):
            d = DILATIONS[cfg][0]
            rq, rk = _rows(q0, CHUNK, d), _rows(k0, nk, d)
            if cfg == 2:
                qb, kw, vw = q16_ref[0, t].astype(F32), k16_ref[0, t], v16_ref[0, t]
            else:
                qb, kw, vw = q_ref[rq, :], k_ref[rk, :], v_ref[rk, :]
            qs = _stack_heads(qb * SCALE, left)
            s = _dot_nt(qs, kw) + tab[3 * cfg + var, :, 0:nk]
            m = jnp.max(s, axis=-1, keepdims=True)
            p = jnp.exp(s - m)
            o_c[cfg][rq, :] = _unstack_heads(_dot(p, vw), left)
            m_c[cfg][rq, :] = _unstack_heads(m, left)
            l_c[cfg][rq, :] = _unstack_heads(jnp.sum(p, axis=-1, keepdims=True), left)
        _attn_blocks(block, 16)

        def merge(j, carry):
            rows = pl.ds(pl.multiple_of(j * 256, 256), 256)
            ms = [m_c[i][rows, :] for i in range(3)]
            top = jnp.maximum(jnp.maximum(ms[0], ms[1]), ms[2])
            ws = [jnp.exp(m - top) for m in ms]
            den = l_c[0][rows, :] * ws[0] + l_c[1][rows, :] * ws[1] + l_c[2][rows, :] * ws[2]
            num = o_c[0][rows, :] * ws[0] + o_c[1][rows, :] * ws[1] + o_c[2][rows, :] * ws[2]
            a_ref[rows, :] = num / den
            lse_ref[rows, :] = top + jnp.log(den)
            return carry
        lax.fori_loop(0, SEQ // 256, merge, 0)

    blk = lambda col0: pl.BlockSpec((SEQ, 128), lambda b, hp: (b, col0 // 128 + hp))
    out = pl.BlockSpec((SEQ, 128), lambda b, hp: (b, hp))
    return pl.pallas_call(
        body, name="attn_fwd", grid=(B_LOC, 4),
        in_specs=[pl.BlockSpec((1, 8, 128), lambda b, hp: (hp, 0, 0)), blk(C_QA), blk(C_KA), blk(C_VA),
                  _blk16(C_QA), _blk16(C_KA), _blk16(C_VA)],
        out_specs=(out, out),
        out_shape=(SDS((T_LOC, ATTN_W), F32), SDS((T_LOC, ATTN_W), F32)),
        scratch_shapes=[pltpu.VMEM((SEQ, 128), F32)] * 9 + [pltpu.VMEM((N_BIAS, 2 * CHUNK, 2 * CHUNK), F32)],
        compiler_params=_params(VMEM_NO_STAGING_MIB, ("arbitrary", "arbitrary")),
    )(slopes, proj, proj, proj, qkv16, qkv16, qkv16)


def _chunks_side_by_side(v, pr, tm):
    return jnp.concatenate([v[ch * CHUNK:(ch + 1) * CHUNK, pr * 128:(pr + 1) * 128] for ch in range(tm // CHUNK)], axis=1)


def _first_group_lanes(tm):
    return lax.broadcasted_iota(jnp.int32, (CHUNK, tm), 1) % 128 < HEAD


def _store_chunks(dst_ref, pr, val, tm):
    for ch in range(tm // CHUNK):
        dst_ref[ch * CHUNK:(ch + 1) * CHUNK, pr * 128:(pr + 1) * 128] = val[:, ch * CHUNK:(ch + 1) * CHUNK]


def _sgu_mix(vn, ws_ref, dst_ref, tm):
    first = _first_group_lanes(tm)
    for pr in range(2):
        vp = _chunks_side_by_side(vn, pr, tm)
        _store_chunks(dst_ref, pr, jnp.where(first, _dot(ws_ref[2 * pr], vp), _dot(ws_ref[2 * pr + 1], vp)), tm)


def _mem_head_of_lane(rows):
    return lax.broadcasted_iota(jnp.int32, (rows, MEM_W), 1) // HEAD


def _stack_mem_heads(v, rows):
    head = _mem_head_of_lane(rows)
    return jnp.concatenate([jnp.where(head == h, v, 0.0) for h in range(4)], axis=0)


def _unstack_mem_heads(v, rows):
    head = _mem_head_of_lane(rows)
    out = v[0:rows]
    for h in range(1, 4):
        out = jnp.where(head == h, v[h * rows:(h + 1) * rows], out)
    return out


def _mem_attn_probs(q, kmem, rows):
    qs = _stack_mem_heads(q, rows).astype(BF16)
    s = _dot_nt(qs, kmem) * SCALE
    e = jnp.exp(s - jnp.max(s, axis=-1, keepdims=True))
    return e * (1.0 / jnp.sum(e, axis=-1, keepdims=True)), qs


def _branch_blocks(tm):
    col = lambda w, c0: pl.BlockSpec((tm, w), lambda i: (i, c0 // w))
    return [col(512, C_ZA), col(256, C_UB), col(256, C_VB), col(256, C_ZB), col(256, C_QM), col(256, C_ZM)]


def _branch_fwd(proj, a, kv, w_s, b_exp, g_v):
    tm = 512
    per_ex = SEQ // tm

    def body(za_ref, ub_ref, vb_ref, zb_ref, qm_ref, zm_ref, a_ref, kv_ref, ws_ref, be_ref, gv_ref, o_ref, mix):
        o_ref[:, 0:ATTN_W] = (_silu_and_grad(za_ref[...])[0] * a_ref[...]).astype(BF16)
        gu = _gelu_and_grad(ub_ref[...])[0]
        gv = _gelu_and_grad(vb_ref[...])[0]
        vn = gv * _rstd(gv) * gv_ref[...]
        _sgu_mix(vn.astype(BF16), ws_ref, mix, tm)
        sg = gu * (mix[...] + jnp.concatenate([be_ref[...]] * (tm // CHUNK), axis=0))
        o_ref[:, ATTN_W:ATTN_W + SGU_W] = (_silu_and_grad(zb_ref[...])[0] * sg).astype(BF16)
        p = _mem_attn_probs(qm_ref[...], kv_ref[:, 0:MEM_W], tm)[0]
        mo = _unstack_mem_heads(_dot(p, kv_ref[:, MEM_W:2 * MEM_W]), tm)
        o_ref[:, ATTN_W + SGU_W:D_MODEL] = (_silu_and_grad(zm_ref[...])[0] * mo).astype(BF16)

    full = lambda shape: pl.BlockSpec(shape, lambda i: (0,) * len(shape))
    return pl.pallas_call(
        body, name="branch_fwd", grid=(T_LOC // tm,),
        in_specs=_branch_blocks(tm) + [
            pl.BlockSpec((tm, ATTN_W), lambda i: (i, 0)), pl.BlockSpec((N_MEM, 2 * MEM_W), lambda i: (i // per_ex, 0)),
            full((4, CHUNK, CHUNK)), full((CHUNK, SGU_W)), full((1, SGU_W))],
        out_specs=pl.BlockSpec((tm, D_MODEL), lambda i: (i, 0)),
        out_shape=SDS((T_LOC, D_MODEL), BF16),
        scratch_shapes=[pltpu.VMEM((tm, SGU_W), F32)],
        compiler_params=_params(VMEM_NO_STAGING_MIB, ("arbitrary",)),
    )(proj, proj, proj, proj, proj, proj, a, kv, w_s, b_exp, g_v)


def _outproj_loss(gated, wout, x2, tgt2, g_final):
    tm = 512

    def body(g_ref, w_ref, x_ref, t_ref, gf_ref, dh2_ref, loss_ref, dgf_ref, dwo_ref, dwo_acc):
        @pl.when(pl.program_id(0) == 0)
        def _():
            loss_ref[...] = jnp.zeros_like(loss_ref)
            dgf_ref[...] = jnp.zeros_like(dgf_ref)
            dwo_acc[...] = jnp.zeros_like(dwo_acc)
        gated = g_ref[...]
        h2 = x_ref[...] + _dot(gated, w_ref[...])
        r = _rstd(h2)
        gf = gf_ref[...]
        err = h2 * r * gf - t_ref[...]
        loss_ref[...] += 0.5 * jnp.sum(jnp.mean(err * err, axis=-1, keepdims=True))
        dy = err * (1.0 / D_MODEL)
        dh2 = _rms_bwd(h2, r, gf, dy)
        dh2_ref[...] = dh2
        dgf_ref[...] += jnp.sum(dy * (h2 * r), axis=0, keepdims=True)
        dwo_acc[...] += _dot_tn(gated, dh2)

        @pl.when(pl.program_id(0) == pl.num_programs(0) - 1)
        def _():
            _cast_rows(dwo_acc, dwo_ref, D_MODEL)

    row = pl.BlockSpec((tm, D_MODEL), lambda i: (i, 0))
    vec = pl.BlockSpec((1, D_MODEL), lambda i: (0, 0))
    square = pl.BlockSpec((D_MODEL, D_MODEL), lambda i: (0, 0))
    return pl.pallas_call(
        body, name="outproj_loss", grid=(T_LOC // tm,),
        in_specs=[row, square, row, row, vec],
        out_specs=(row, pl.BlockSpec((8, 128), lambda i: (0, 0)), vec, square),
        out_shape=(SDS((T_LOC, D_MODEL), F32), SDS((8, 128), F32), SDS((1, D_MODEL), F32), SDS((D_MODEL, D_MODEL), BF16)),
        scratch_shapes=[pltpu.VMEM((D_MODEL, D_MODEL), F32)],
        compiler_params=_params(VMEM_NO_STAGING_MIB, ("arbitrary",)),
    )(gated, wout, x2, tgt2, g_final)


def _branch_bwd(dh2, wout, proj, a, kv, w_s, b_exp, g_v):
    tm = 512
    per_ex = SEQ // tm

    def body(dh2_ref, w_ref, za_ref, ub_ref, vb_ref, zb_ref, qm_ref, zm_ref, a_ref, kv_ref, ws_ref,
             be_ref, gv_ref, da_ref, dr_ref, dkv_ref, dws_ref, db_ref, dgv_ref, mix, dvn, dmsum):
        i = pl.program_id(0)

        @pl.when(i == 0)
        def _():
            dws_ref[...] = jnp.zeros_like(dws_ref)
            dgv_ref[...] = jnp.zeros_like(dgv_ref)
            dmsum[...] = jnp.zeros_like(dmsum)

        @pl.when(i % per_ex == 0)
        def _():
            dkv_ref[...] = jnp.zeros_like(dkv_ref)

        dg = _dot_nt(dh2_ref[...], w_ref[...])

        sa, dsa = _silu_and_grad(za_ref[...])
        dga = dg[:, 0:ATTN_W]
        da_ref[...] = dga * sa
        dr_ref[:, 0:512] = (dga * a_ref[...] * dsa).astype(BF16)

        ub, vb = ub_ref[...], vb_ref[...]
        gu, dgu = _gelu_and_grad(ub)
        gv, dgv = _gelu_and_grad(vb)
        rv = _rstd(gv)
        gain = gv_ref[...]
        vn = (gv * rv * gain).astype(BF16)
        _sgu_mix(vn, ws_ref, mix, tm)
        mixed = mix[...] + jnp.concatenate([be_ref[...]] * (tm // CHUNK), axis=0)
        sb, dsb = _silu_and_grad(zb_ref[...])
        dgb = dg[:, ATTN_W:ATTN_W + SGU_W]
        dsg = dgb * sb
        dr_ref[:, 512:768] = (dsg * mixed * dgu).astype(BF16)
        dr_ref[:, 1024:1280] = (dgb * (gu * mixed) * dsb).astype(BF16)
        dmix = dsg * gu
        for ch in range(tm // CHUNK):
            dmsum[...] += dmix[ch * CHUNK:(ch + 1) * CHUNK, :]
        first = _first_group_lanes(tm)
        for pr in range(2):
            dmp, vp = _chunks_side_by_side(dmix, pr, tm), _chunks_side_by_side(vn, pr, tm)
            dws_ref[2 * pr] += _dot_nt(jnp.where(first, dmp, 0.0), vp)
            dws_ref[2 * pr + 1] += _dot_nt(jnp.where(first, 0.0, dmp), vp)
            _store_chunks(dvn, pr, jnp.where(first, _dot_tn(ws_ref[2 * pr], dmp), _dot_tn(ws_ref[2 * pr + 1], dmp)), tm)
        dvn_v = dvn[...]
        dgv_ref[...] += jnp.sum(dvn_v * (gv * rv), axis=0, keepdims=True)
        dr_ref[:, 768:1024] = (_rms_bwd(gv, rv, gain, dvn_v) * dgv).astype(BF16)

        szm, dszm = _silu_and_grad(zm_ref[...])
        dgm = dg[:, ATTN_W + SGU_W:D_MODEL]
        kmem, vmem_ = kv_ref[:, 0:MEM_W].astype(BF16), kv_ref[:, MEM_W:2 * MEM_W].astype(BF16)
        p, qs = _mem_attn_probs(qm_ref[...], kmem, tm)
        dmos = _stack_mem_heads(dgm * szm, tm).astype(BF16)
        dr_ref[:, 1536:1792] = (dgm * _unstack_mem_heads(_dot(p, vmem_), tm) * dszm).astype(BF16)
        dp = _dot_nt(dmos, vmem_)
        ds = (p * (dp - jnp.sum(p * dp, axis=-1, keepdims=True)) * SCALE).astype(BF16)
        dr_ref[:, 1280:1536] = _unstack_mem_heads(_dot(ds, kmem), tm).astype(BF16)
        dkv_ref[:, 0:MEM_W] += _dot_tn(ds, qs)
        dkv_ref[:, MEM_W:2 * MEM_W] += _dot_tn(p, dmos)

        @pl.when(i == pl.num_programs(0) - 1)
        def _():
            tot = dmsum[...]
            hi = tot.astype(BF16)
            lo = (tot - hi.astype(F32)).astype(BF16)
            grp = (lax.broadcasted_iota(jnp.int32, (SGU_W, 128), 0) // HEAD
                   == lax.broadcasted_iota(jnp.int32, (SGU_W, 128), 1)).astype(BF16)
            db_ref[...] = (_dot(hi, grp) + _dot(lo, grp)).T[0:4, :]

    full = lambda shape: pl.BlockSpec(shape, lambda i: (0,) * len(shape))
    row = lambda w: pl.BlockSpec((tm, w), lambda i: (i, 0))
    return pl.pallas_call(
        body, name="branch_bwd", grid=(T_LOC // tm,),
        in_specs=[row(D_MODEL), full((D_MODEL, D_MODEL))] + _branch_blocks(tm) + [
            row(ATTN_W), pl.BlockSpec((N_MEM, 2 * MEM_W), lambda i: (i // per_ex, 0)),
            full((4, CHUNK, CHUNK)), full((CHUNK, SGU_W)), full((1, SGU_W))],
        out_specs=(row(ATTN_W), row(REST_W), pl.BlockSpec((N_MEM, 2 * MEM_W), lambda i: (i // per_ex, 0)),
                   full((4, CHUNK, CHUNK)), full((4, CHUNK)), full((1, SGU_W))),
        out_shape=(SDS((T_LOC, ATTN_W), F32), SDS((T_LOC, REST_W), BF16), SDS((B_LOC * N_MEM, 2 * MEM_W), F32),
                   SDS((4, CHUNK, CHUNK), F32), SDS((4, CHUNK), F32), SDS((1, SGU_W), F32)),
        scratch_shapes=[pltpu.VMEM((tm, SGU_W), F32), pltpu.VMEM((tm, SGU_W), F32), pltpu.VMEM((CHUNK, SGU_W), F32)],
        compiler_params=_params(56, ("arbitrary",)),
    )(dh2, wout, proj, proj, proj, proj, proj, proj, a, kv, w_s, b_exp, g_v)


def _attn_bwd(proj, qkv16, slopes, da, a, lse, token):
    def body(sl_ref, q_ref, k_ref, v_ref, q16_ref, k16_ref, v16_ref, da_ref, a_ref, lse_ref, _,
             dq_ref, dk_ref, dv_ref, *scr):
        dq_s, dk_s, dv_s, tab = scr[0:3], scr[3:6], scr[6:9], scr[9]
        lse_h, delta_h = scr[10:12], scr[12:14]
        p_all, ds_all = scr[14], scr[15]
        left = _left_lanes(CHUNK)
        _fill_bias_tables(sl_ref, tab)

        def prep(j, carry):
            rows = pl.ds(pl.multiple_of(j * 256, 256), 256)
            l256 = _left_lanes(256)
            prod = da_ref[rows, :] * a_ref[rows, :]
            delta_h[0][rows, :] = jnp.broadcast_to(jnp.sum(jnp.where(l256, prod, 0.0), axis=-1, keepdims=True), (256, 128))
            delta_h[1][rows, :] = jnp.broadcast_to(jnp.sum(jnp.where(l256, 0.0, prod), axis=-1, keepdims=True), (256, 128))
            pair = lse_ref[rows, :]
            other = pltpu.roll(pair, HEAD, axis=1)
            lse_h[0][rows, :] = jnp.where(l256, pair, other)
            lse_h[1][rows, :] = jnp.where(l256, other, pair)
            zero = jnp.zeros((256, 128), F32)
            for cfg in range(3):
                dk_s[cfg][rows, :] = zero
                dv_s[cfg][rows, :] = zero
            return carry
        lax.fori_loop(0, SEQ // 256, prep, 0)

        def per_row(halves, rq, nk):
            v = jnp.concatenate([halves[0][rq, :], halves[1][rq, :]], axis=0)
            return v if nk == 128 else jnp.concatenate([v, v], axis=1)

        def qkv(cfg, rq, rk, t):
            if cfg == 2:
                return q16_ref[0, t].astype(F32), k16_ref[0, t], v16_ref[0, t]
            return q_ref[rq, :], k_ref[rk, :], v_ref[rk, :]

        def probs(cfg, var, q0, k0, nk, t):
            d = DILATIONS[cfg][0]
            rq, rk = _rows(q0, CHUNK, d), _rows(k0, nk, d)
            qb, kw, vw = qkv(cfg, rq, rk, t)
            qs = _stack_heads(qb * SCALE, left)
            das = _stack_heads(da_ref[rq, :], left)
            s = _dot_nt(qs, kw) + tab[3 * cfg + var, :, 0:nk]
            p = jnp.exp(s - per_row(lse_h, rq, nk))
            p_all[16 * cfg + t, :, 0:nk] = p.astype(BF16)
            ds_all[16 * cfg + t, :, 0:nk] = (p * (_dot_nt(das, vw) - per_row(delta_h, rq, nk))).astype(BF16)
        _attn_blocks(probs, 16)

        def grads(cfg, var, q0, k0, nk, t):
            d = DILATIONS[cfg][0]
            rq, rk = _rows(q0, CHUNK, d), _rows(k0, nk, d)
            qb, kw, _ = qkv(cfg, rq, rk, t)
            qs = _stack_heads(qb * SCALE, left).astype(BF16)
            das = _stack_heads(da_ref[rq, :], left).astype(BF16)
            p, ds = p_all[16 * cfg + t, :, 0:nk], ds_all[16 * cfg + t, :, 0:nk]
            dq_s[cfg][rq, :] = _unstack_heads(_dot(ds, kw), left) * SCALE
            dk_s[cfg][rk, :] += _dot_tn(ds, qs)
            dv_s[cfg][rk, :] += _dot_tn(p, das)
        _attn_blocks(grads, 8)

        def flush(j, carry):
            rows = pl.ds(pl.multiple_of(j * 256, 256), 256)
            for acc, dst in ((dq_s, dq_ref), (dk_s, dk_ref), (dv_s, dv_ref)):
                dst[rows, :] = (acc[0][rows, :] + acc[1][rows, :] + acc[2][rows, :]).astype(BF16)
            return carry
        lax.fori_loop(0, SEQ // 256, flush, 0)

    blk = lambda col0: pl.BlockSpec((SEQ, 128), lambda b, hp: (b, col0 // 128 + hp))
    own = pl.BlockSpec((SEQ, 128), lambda b, hp: (b, hp))
    return pl.pallas_call(
        body, name="attn_bwd", grid=(B_LOC, 4),
        in_specs=[pl.BlockSpec((1, 8, 128), lambda b, hp: (hp, 0, 0)), blk(C_QA), blk(C_KA), blk(C_VA),
                  _blk16(C_QA), _blk16(C_KA), _blk16(C_VA), own, own, own, _TOKEN],
        out_specs=(own, own, own),
        out_shape=(SDS((T_LOC, ATTN_W), BF16),) * 3,
        scratch_shapes=[pltpu.VMEM((SEQ, 128), F32)] * 9 + [pltpu.VMEM((N_BIAS, 2 * CHUNK, 2 * CHUNK), F32)]
        + [pltpu.VMEM((SEQ, 128), F32)] * 4 + [pltpu.VMEM((48, 2 * CHUNK, 2 * CHUNK), BF16)] * 2,
        compiler_params=_params(52, ("arbitrary", "arbitrary")),
    )(slopes, proj, proj, proj, qkv16, qkv16, qkv16, da, a, lse, token)


def _dproj_specs(tm):
    third = pl.BlockSpec((tm, ATTN_W), lambda i: (i, 0))
    return [third, third, third, pl.BlockSpec((tm, REST_W), lambda i: (i, 0))]


def _dx(dq, dk, dv, dr, wint, x2, dh2, g_norm, token):
    tm = 512

    def body(dq_ref, dk_ref, dv_ref, dr_ref, w_ref, x_ref, dh2_ref, g_ref, _, gx_ref, dgn_ref):
        @pl.when(pl.program_id(0) == 0)
        def _():
            dgn_ref[...] = jnp.zeros_like(dgn_ref)
        dh = (_dot(dq_ref[...], w_ref[C_QA:C_KA, :]) + _dot(dk_ref[...], w_ref[C_KA:C_VA, :])
              + _dot(dv_ref[...], w_ref[C_VA:C_ZA, :]) + _dot(dr_ref[...], w_ref[C_ZA:IN_COLS, :]))
        xv = x_ref[...]
        r = _rstd(xv)
        gx_ref[...] = dh2_ref[...] + _rms_bwd(xv, r, g_ref[...], dh)
        dgn_ref[...] += jnp.sum(dh * (xv * r), axis=0, keepdims=True)

    row = pl.BlockSpec((tm, D_MODEL), lambda i: (i, 0))
    vec = pl.BlockSpec((1, D_MODEL), lambda i: (0, 0))
    return pl.pallas_call(
        body, name="dx", grid=(T_LOC // tm,),
        in_specs=_dproj_specs(tm) + [pl.BlockSpec((IN_COLS, D_MODEL), lambda i: (0, 0)), row, row, vec, _TOKEN],
        out_specs=(row, vec),
        out_shape=(SDS((T_LOC, D_MODEL), F32), SDS((1, D_MODEL), F32)),
        compiler_params=_params(48, ("arbitrary",)),
    )(dq, dk, dv, dr, wint, x2, dh2, g_norm, token)


def _dwin(dq, dk, dv, dr, h16, half, token):
    tm = 1024
    width = D_MODEL // 2

    def body(dq_ref, dk_ref, dv_ref, dr_ref, h_ref, _, o_ref, acc):
        @pl.when(pl.program_id(0) == 0)
        def _():
            acc[...] = jnp.zeros_like(acc)
        h = h_ref[...]
        acc[C_QA:C_KA, :] += _dot_tn(dq_ref[...], h)
        acc[C_KA:C_VA, :] += _dot_tn(dk_ref[...], h)
        acc[C_VA:C_ZA, :] += _dot_tn(dv_ref[...], h)
        acc[C_ZA:IN_COLS, :] += _dot_tn(dr_ref[...], h)

        @pl.when(pl.program_id(0) == pl.num_programs(0) - 1)
        def _():
            _cast_rows(acc, o_ref, IN_COLS)

    return pl.pallas_call(
        body, name="dwin%d" % half, grid=(T_LOC // tm,),
        in_specs=_dproj_specs(tm) + [pl.BlockSpec((tm, width), lambda i: (i, half)), _TOKEN],
        out_specs=pl.BlockSpec((IN_COLS, width), lambda i: (0, 0)),
        out_shape=SDS((IN_COLS, width), BF16),
        scratch_shapes=[pltpu.VMEM((IN_COLS, width), F32)],
        compiler_params=_params(48, ("arbitrary",)),
    )(dq, dk, dv, dr, h16, token)


def _memkv_bwd(dkv, mem2, g_mem, wkv):
    def body(dkv_ref, m_ref, g_ref, w_ref, dw_ref, dg_ref):
        mv = m_ref[...]
        r = _rstd(mv)
        dkv_v = dkv_ref[...].astype(BF16)
        dw_ref[...] = _dot_tn(mv * r * g_ref[...], dkv_v).astype(BF16)
        dg_ref[...] = jnp.sum(_dot_nt(dkv_v, w_ref[...]) * (mv * r), axis=0, keepdims=True)

    return pl.pallas_call(
        body, name="memkv_bwd", out_shape=(SDS((D_MODEL, 2 * MEM_W), BF16), SDS((1, D_MODEL), F32)),
        compiler_params=_params(32),
    )(dkv, mem2, g_mem, wkv)


def _allreduce_small(parts):
    n = len(parts)

    def body(*refs):
        ins, outs, bufs = refs[0:n], refs[n:2 * n], refs[2 * n:3 * n]
        send_sems, recv_sems = refs[3 * n], refs[3 * n + 1]
        pos = _mesh_pos()
        me = _flat(pos)
        for a in range(n):
            bufs[a][me] = ins[a][...]

        def copy(a, k, slot):
            return pltpu.make_async_remote_copy(
                src_ref=ins[a], dst_ref=bufs[a].at[slot],
                send_sem=send_sems.at[7 * a + k - 1], recv_sem=recv_sems.at[7 * a + k - 1],
                device_id=_peer(pos, k), device_id_type=MESH)

        sent = [copy(a, k, me) for a in range(n) for k in range(1, N_DEV)]
        for cp in sent:
            cp.start()
        for a in range(n):
            for k in range(1, N_DEV):
                copy(a, k, _flat(_peer(pos, k))).wait_recv()
        for cp in sent:
            cp.wait_send()
        for a in range(n):
            acc = bufs[a][0]
            for s in range(1, N_DEV):
                acc = acc + bufs[a][s]
            outs[a][...] = acc

    vmem = pl.BlockSpec(memory_space=pltpu.VMEM)
    return pl.pallas_call(
        body, name="allreduce_small",
        out_shape=tuple(SDS(p.shape, F32) for p in parts),
        in_specs=[vmem] * n, out_specs=(vmem,) * n,
        scratch_shapes=[pltpu.VMEM((N_DEV,) + p.shape, F32) for p in parts]
        + [pltpu.SemaphoreType.DMA((7 * n,)), pltpu.SemaphoreType.DMA((7 * n,))],
        compiler_params=_params(16),
    )(*parts)


_HBM = pl.BlockSpec(memory_space=pltpu.HBM)
_SEM = pl.BlockSpec(memory_space=pltpu.SEMAPHORE)
_SIDE_EFFECT = pltpu.SideEffectType.DATAFLOW_SIDE_EFFECTING


def _exchange_copies(src_refs, land_refs, scatter, send_sems, recv_sems):
    pos = _mesh_pos()
    copies = []
    for a, (src, land) in enumerate(zip(src_refs, land_refs)):
        n = land.shape[1]
        for k in range(1, N_DEV):
            peer = _peer(pos, k)
            piece = src.at[pl.ds(pl.multiple_of(_flat(peer) * n, 16), n), :] if scatter[a] else src
            copies.append(pltpu.make_async_remote_copy(
                src_ref=piece, dst_ref=land.at[_flat(pos)],
                send_sem=send_sems.at[7 * a + k - 1], recv_sem=recv_sems.at[7 * a + k - 1],
                device_id=peer, device_id_type=MESH))
    return copies


def _own_copies(src_refs, land_refs, scatter, send_sems):
    n = len(src_refs)
    me = _flat(_mesh_pos())
    copies = []
    for a, (src, land) in enumerate(zip(src_refs, land_refs)):
        rows = land.shape[1]
        piece = src.at[pl.ds(pl.multiple_of(me * rows, 16), rows), :] if scatter[a] else src
        copies.append(pltpu.make_async_copy(piece, land.at[me], send_sems.at[7 * n + a]))
    return copies


def _exchange_start(name, srcs, scatter):
    n = len(srcs)

    def body(*refs):
        for cp in _exchange_copies(refs[0:n], refs[n:2 * n], scatter, refs[2 * n], refs[2 * n + 1]):
            cp.start()
        for cp in _own_copies(refs[0:n], refs[n:2 * n], scatter, refs[2 * n]):
            cp.start()
        refs[-1][...] = jnp.zeros_like(refs[-1])

    lands = [lax.empty((N_DEV,) + (t.shape[0] // N_DEV if sc else t.shape[0],) + t.shape[1:], t.dtype)
             for t, sc in zip(srcs, scatter)]
    ops = [pltpu.with_memory_space_constraint(t, pltpu.HBM) for t in (*srcs, *lands)]
    out = pl.pallas_call(
        body, name=name,
        out_shape=(pltpu.SemaphoreType.DMA((8 * n,)), pltpu.SemaphoreType.DMA((7 * n,)),
                   *[pltpu.HBM(t.shape, t.dtype) for t in ops], SDS((8, 128), F32)),
        in_specs=[_HBM] * (2 * n),
        out_specs=(_SEM, _SEM, *[_HBM] * (2 * n), pl.BlockSpec(memory_space=pltpu.VMEM)),
        input_output_aliases={i: 2 + i for i in range(2 * n)},
        compiler_params=pltpu.CompilerParams(has_side_effects=_SIDE_EFFECT),
    )(*ops)
    return out[0], out[1], out[2:2 + n], out[2 + n:2 + 2 * n], out[-1]


def _exchange_wait(name, started, scatter, after):
    send_sems, recv_sems, srcs, lands, _ = started
    n = len(srcs)

    def body(*refs):
        for cp in _exchange_copies(refs[0:n], refs[n:2 * n], scatter, refs[2 * n], refs[2 * n + 1]):
            cp.wait_send()
            cp.wait_recv()
        for cp in _own_copies(refs[0:n], refs[n:2 * n], scatter, refs[2 * n]):
            cp.wait()

    out = pl.pallas_call(
        body, name=name,
        out_shape=tuple(pltpu.HBM(t.shape, t.dtype) for t in (*srcs, *lands)),
        in_specs=[_HBM] * (2 * n) + [_SEM, _SEM, pl.BlockSpec(memory_space=pl.ANY)],
        out_specs=(_HBM,) * (2 * n),
        input_output_aliases={i: i for i in range(2 * n)},
        compiler_params=pltpu.CompilerParams(has_side_effects=_SIDE_EFFECT),
    )(*srcs, *lands, send_sems, recv_sems, after)
    return out[n:]


def _adamw(w, g, m, v):
    m = ADAM_B1 * m + (1.0 - ADAM_B1) * g
    v = ADAM_B2 * v + (1.0 - ADAM_B2) * (g * g)
    m_hat = m / (1.0 - ADAM_B1 ** ADAM_STEP)
    v_hat = v / (1.0 - ADAM_B2 ** ADAM_STEP)
    return -ADAM_LR * (m_hat / (jnp.sqrt(v_hat) + ADAM_EPS) + ADAM_WD * w), m, v


def _adam_slots(name, pieces, w, m, v):
    rows, cols = w.shape
    starts = [sum(p.shape[2] for p in pieces[:i]) for i in range(len(pieces) + 1)]
    assert starts[-1] == cols and all(p.shape[1] == rows for p in pieces)

    def body(*refs):
        s_refs, (w_ref, m_ref, v_ref, g_o, d_o, m_o, v_o, acc) = refs[:len(pieces)], refs[len(pieces):]
        s = pl.program_id(0)

        @pl.when(s == 0)
        def _():
            for i, s_ref in enumerate(s_refs):
                acc[:, starts[i]:starts[i + 1]] = s_ref[0].astype(F32)

        @pl.when(s > 0)
        def _():
            for i, s_ref in enumerate(s_refs):
                acc[:, starts[i]:starts[i + 1]] += s_ref[0].astype(F32)

        @pl.when(s == N_DEV - 1)
        def _():
            g = acc[...]
            g_o[...] = g
            d_o[...], m_o[...], v_o[...] = _adamw(w_ref[...], g, m_ref[...], v_ref[...])

    full = pl.BlockSpec((rows, cols), lambda s: (0, 0))
    return pl.pallas_call(
        body, name="adam_" + name, grid=(N_DEV,),
        in_specs=[pl.BlockSpec((1, rows, p.shape[2]), lambda s: (s, 0, 0)) for p in pieces] + [full, full, full],
        out_specs=(full,) * 4, out_shape=(SDS((rows, cols), F32),) * 4,
        scratch_shapes=[pltpu.VMEM((rows, cols), F32)],
        compiler_params=_params(40, ("arbitrary",)),
    )(*pieces, w, m, v)


def _adam_small(ws, gs, ms, vs, loss_slots):
    n = len(ws)

    def total(ref, like):
        if len(ref.shape) == len(like.shape):
            return ref[...]
        acc = ref[0]
        for s in range(1, N_DEV):
            acc = acc + ref[s]
        return acc

    def body(*refs):
        w_r, g_r, m_r, v_r = refs[0:n], refs[n:2 * n], refs[2 * n:3 * n], refs[3 * n:4 * n]
        loss_r, outs = refs[4 * n], refs[4 * n + 1:]
        for a in range(n):
            g = total(g_r[a], w_r[a])
            outs[a][...] = g
            outs[n + 1 + 3 * a][...], outs[n + 2 + 3 * a][...], outs[n + 3 + 3 * a][...] = _adamw(
                w_r[a][...], g, m_r[a][...], v_r[a][...])
        outs[n][...] = total(loss_r, outs[n])

    out = pl.pallas_call(
        body, name="adam_small",
        out_shape=tuple(SDS(w.shape, F32) for w in ws) + (SDS(loss_slots.shape[1:], F32),)
        + tuple(SDS(w.shape, F32) for w in ws for _ in range(3)),
        compiler_params=_params(16),
    )(*ws, *gs, *ms, *vs, loss_slots)
    return out[0:n], out[n], out[n + 1:]


def kernel(x, mem, g_norm, w_in, w_sgu_spatial, b_sgu_spatial, g_sgu_v, g_mem, w_mem_kv, w_out, g_final, loss_target, m_g_norm, m_w_in, m_w_sgu_spatial, m_b_sgu_spatial, m_g_sgu_v, m_g_mem, m_w_mem_kv, m_w_out, m_g_final, v_g_norm, v_w_in, v_w_sgu_spatial, v_b_sgu_spatial, v_g_sgu_v, v_g_mem, v_w_mem_kv, v_w_out, v_g_final):
    x2 = x.reshape(T_LOC, D_MODEL)
    tgt2 = loss_target.reshape(T_LOC, D_MODEL)
    mem2 = mem.reshape(B_LOC * N_MEM, D_MODEL)
    w_s = w_sgu_spatial[0]
    b_exp = jnp.repeat(b_sgu_spatial[0].T, HEAD, axis=1)
    slope = jnp.power(2.0, -8.0 * (jnp.arange(8, dtype=F32) + 1.0) / 8)
    slopes = jnp.broadcast_to(jnp.repeat(slope.reshape(4, 2), HEAD, axis=1)[:, None, :], (4, 8, 128))

    tr = lambda t: jnp.transpose(t[0])

    wint, wkv_own, wout_own, h16 = _allgather_weights(tr(w_in), w_mem_kv[0], w_out[0], x2, g_norm)
    started0 = _exchange_start("exchange0_start", [wkv_own, wout_own], [False, False])
    proj, qkv16 = _proj_fwd(h16, wint, started0[4])
    wkv, wout = _exchange_wait("exchange0_wait", started0, [False, False], proj)
    wkv, wout = wkv.reshape(D_MODEL, 2 * MEM_W), wout.reshape(D_MODEL, D_MODEL)
    kv = _memkv_fwd(mem2, g_mem, wkv)
    a, lse = _attn_fwd(proj, qkv16, slopes)
    gated = _branch_fwd(proj, a, kv, w_s, b_exp, g_sgu_v)
    dh2, loss8, dgf, dwout = _outproj_loss(gated, wout, x2, tgt2, g_final.reshape(1, D_MODEL))

    da, dr, dkv, dws, dbs, dgv = _branch_bwd(dh2, wout, proj, a, kv, w_s, b_exp, g_sgu_v)
    dwkv, dgm = _memkv_bwd(dkv, mem2, g_mem, wkv)

    early = [dws.reshape(4 * CHUNK, CHUNK), dbs, dgv, dgm, dgf, loss8]
    scatter1 = [True, True] + [False] * len(early)
    started1 = _exchange_start("exchange1_start", [dwkv, dwout] + early, scatter1)
    dq, dk, dv = _attn_bwd(proj, qkv16, slopes, da, a, lse, started1[4])
    s_wkv, s_wout, s_ws, s_bs, s_gv, s_gm, s_gf, s_loss = _exchange_wait("exchange1_wait", started1, scatter1, dq)

    dwint0 = _dwin(dq, dk, dv, dr, h16, 0, started1[4])
    started2 = _exchange_start("exchange2_start", [dwint0], [True])
    dwint1 = _dwin(dq, dk, dv, dr, h16, 1, started2[4])
    started3 = _exchange_start("exchange3_start", [dwint1], [True])
    grad_x, dgn = _dx(dq, dk, dv, dr, wint, x2, dh2, g_norm, started3[4])
    s_win0, = _exchange_wait("exchange2_wait", started2, [True], grad_x)
    dgn_sum, = _allreduce_small([dgn])
    s_win1, = _exchange_wait("exchange3_wait", started3, [True], dgn_sum)

    g_win, d_win, m_win, v_win = map(
        jnp.transpose, _adam_slots("w_in", [s_win0, s_win1], tr(w_in), tr(m_w_in), tr(v_w_in)))
    g_wkv, d_wkv, m_wkv, v_wkv = _adam_slots("w_mem_kv", [s_wkv], w_mem_kv[0], m_w_mem_kv[0], v_w_mem_kv[0])
    g_wout, d_wout, m_wout, v_wout = _adam_slots("w_out", [s_wout], w_out[0], m_w_out[0], v_w_out[0])

    small_shapes = [(1, D_MODEL), (4 * CHUNK, CHUNK), (4, CHUNK), (1, SGU_W), (1, D_MODEL), (1, D_MODEL)]
    pack = lambda arrs: [t.reshape(s) for t, s in zip(arrs, small_shapes)]
    g_small, loss_sum, upd = _adam_small(
        pack([g_norm, w_sgu_spatial, b_sgu_spatial, g_sgu_v, g_mem, g_final]),
        [dgn_sum, s_ws, s_bs, s_gv, s_gm, s_gf],
        pack([m_g_norm, m_w_sgu_spatial, m_b_sgu_spatial, m_g_sgu_v, m_g_mem, m_g_final]),
        pack([v_g_norm, v_w_sgu_spatial, v_b_sgu_spatial, v_g_sgu_v, v_g_mem, v_g_final]), s_loss)
    out_shapes = [g_norm.shape, w_sgu_spatial.shape, b_sgu_spatial.shape, g_sgu_v.shape, g_mem.shape, g_final.shape]
    unpack = lambda arrs: [t.reshape(s) for t, s in zip(arrs, out_shapes)]
    gs = unpack(g_small)
    ds, nms, nvs = unpack(upd[0::3]), unpack(upd[1::3]), unpack(upd[2::3])

    loss = loss_sum[0, 0]

    def assemble(small, win, wkv_, wout_):
        return [small[0], win[None], small[1], small[2], small[3], small[4], wkv_[None], wout_[None], small[5]]

    return (loss, grad_x.reshape(x.shape),
            *assemble(gs, g_win, g_wkv, g_wout), *assemble(ds, d_win, d_wkv, d_wout),
            *assemble(nms, m_win, m_wkv, m_wout), *assemble(nvs, v_win, v_wkv, v_wout))
```

```python
import jax
import jax.numpy as jnp
from jax import lax
from jax.experimental import pallas as pl
from jax.experimental.pallas import tpu as pltpu

F32 = jnp.float32
BF16 = jnp.bfloat16
SDS = jax.ShapeDtypeStruct
MESH = pl.DeviceIdType.MESH

N_DEV = 8
D_MODEL = 1024
SEQ = 2048
B_LOC = 2
T_LOC = B_LOC * SEQ
N_MEM = 256
HEAD = 64
ATTN_W = 512
SGU_W = 256
MEM_W = 256
IN_COLS = 3328
W_IN_SHARD = IN_COLS // N_DEV
CHUNK = 128
DILATIONS = ((1, 2048), (4, 512), (16, 128))
RADIUS = 64
EPS = 1e-6
NEG = -1e30
SCALE = HEAD ** -0.5
C_QA, C_KA, C_VA, C_ZA, C_UB, C_VB, C_ZB, C_QM, C_ZM = 0, 512, 1024, 1536, 2048, 2304, 2560, 2816, 3072
QKV_W = 1536
REST_W = IN_COLS - QKV_W

ADAM_LR, ADAM_B1, ADAM_B2, ADAM_EPS, ADAM_WD, ADAM_STEP = 0.001, 0.9, 0.999, 1e-08, 0.01, 10

V7X_VMEM_MIB = 64
VMEM_NO_STAGING_MIB = V7X_VMEM_MIB - 6


def _params(vmem_mib, sem=None):
    assert vmem_mib < V7X_VMEM_MIB
    return pltpu.CompilerParams(vmem_limit_bytes=vmem_mib << 20, dimension_semantics=sem)


_TOKEN = pl.BlockSpec(memory_space=pl.ANY)


def _dot(a, b):
    return jnp.dot(a.astype(BF16), b.astype(BF16), preferred_element_type=F32)


def _dot_nt(a, b):
    return lax.dot_general(a.astype(BF16), b.astype(BF16), (((1,), (1,)), ((), ())), preferred_element_type=F32)


def _dot_tn(a, b):
    return lax.dot_general(a.astype(BF16), b.astype(BF16), (((0,), (0,)), ((), ())), preferred_element_type=F32)


def _rstd(v):
    return lax.rsqrt(jnp.mean(v * v, axis=-1, keepdims=True) + EPS)


def _rms_bwd(v, r, g, dy):
    gdy = g * dy
    return r * gdy - v * (r * r * r * jnp.mean(gdy * v, axis=-1, keepdims=True))


def _sigmoid(z):
    return 1.0 / (1.0 + jnp.exp(-z))


def _silu_and_grad(z):
    s = _sigmoid(z)
    return z * s, s * (1.0 + z * (1.0 - s))


_G_C = 0.7978845608028654
_G_K = 0.044715


def _gelu_and_grad(v):
    t = jnp.tanh(_G_C * (v + _G_K * (v * v * v)))
    cdf = 0.5 * (1.0 + t)
    return v * cdf, cdf + 0.5 * v * (1.0 - t * t) * (_G_C * (1.0 + 3.0 * _G_K * v * v))


def _cast_rows(src_ref, dst_ref, rows, step=256):
    def one(i, carry):
        r = pl.ds(pl.multiple_of(i * step, step), step)
        dst_ref[r, :] = src_ref[r, :].astype(dst_ref.dtype)
        return carry
    lax.fori_loop(0, rows // step, one, 0)


def _left_lanes(rows):
    return lax.broadcasted_iota(jnp.int32, (rows, 128), 1) < HEAD


def _mesh_pos():
    return lax.axis_index("x"), lax.axis_index("y"), lax.axis_index("c")


def _peer(pos, k):
    x, y, c = pos
    return (1 - x if k & 4 else x, 1 - y if k & 2 else y, 1 - c if k & 1 else c)


def _flat(pos):
    return 4 * pos[0] + 2 * pos[1] + pos[2]


def _allgather_weights(w_in_t, w_kv, w_out, x2, g_norm):
    rows_h = 512
    n_h = T_LOC // rows_h

    def body(win_ref, wkv_ref, wout_ref, x_hbm, g_ref, wint_o, wkv_o, wout_o, h_hbm,
             send_sems, recv_sems, xbuf, hbuf, in_sems, out_sems, own_buf, own_sem):
        x, y, c = _mesh_pos()
        me, sib = (x, y, c), (x, y, 1 - c)
        chips = [(1 - x, y), (x, 1 - y), (1 - x, 1 - y)]

        def rows(p):
            return wint_o.at[pl.ds(pl.multiple_of(_flat(p) * W_IN_SHARD, 16), W_IN_SHARD), :]

        own_buf[...] = win_ref[...].astype(BF16)
        own_local = pltpu.make_async_copy(own_buf, rows(me), own_sem.at[0])
        own_local.start()

        def copy(k, block, to):
            return pltpu.make_async_remote_copy(
                src_ref=own_buf if block is me else rows(block), dst_ref=rows(block),
                send_sem=send_sems.at[k], recv_sem=recv_sems.at[k], device_id=to, device_id_type=MESH)

        first = [copy(0, me, sib)] + [copy(1 + j, me, (*chip, c)) for j, chip in enumerate(chips)]
        for cp in first:
            cp.start()
        wkv_o[...] = wkv_ref[...].astype(BF16)
        wout_o[...] = wout_ref[...].astype(BF16)

        def fetch(i):
            return pltpu.make_async_copy(x_hbm.at[pl.ds(i * rows_h, rows_h), :], xbuf.at[i % 2], in_sems.at[i % 2])

        def put(i):
            return pltpu.make_async_copy(hbuf.at[i % 2], h_hbm.at[pl.ds(i * rows_h, rows_h), :], out_sems.at[i % 2])

        fetch(0).start()
        for i in range(n_h):
            if i + 1 < n_h:
                fetch(i + 1).start()
            fetch(i).wait()
            if i >= 2:
                put(i - 2).wait()
            xv = xbuf[i % 2]
            hbuf[i % 2] = (xv * _rstd(xv) * g_ref[...]).astype(BF16)
            put(i).start()
        put(n_h - 2).wait()
        put(n_h - 1).wait()

        passed = []
        for j, chip in enumerate(chips):
            copy(1 + j, (*chip, c), me).wait_recv()
            fwd = copy(4 + j, (*chip, c), sib)
            fwd.start()
            passed.append(fwd)
        copy(0, sib, me).wait_recv()
        for j, chip in enumerate(chips):
            copy(4 + j, (*chip, 1 - c), me).wait_recv()
        for cp in first + passed:
            cp.wait_send()
        own_local.wait()

    vmem = pl.BlockSpec(memory_space=pltpu.VMEM)
    hbm = pl.BlockSpec(memory_space=pl.ANY)
    return pl.pallas_call(
        body, name="allgather_weights",
        out_shape=(SDS((IN_COLS, D_MODEL), BF16), SDS(w_kv.shape, BF16), SDS(w_out.shape, BF16),
                   SDS((T_LOC, D_MODEL), BF16)),
        in_specs=[vmem, vmem, vmem, hbm, vmem], out_specs=(hbm, vmem, vmem, hbm),
        scratch_shapes=[pltpu.SemaphoreType.DMA((7,)), pltpu.SemaphoreType.DMA((7,)),
                        pltpu.VMEM((2, rows_h, D_MODEL), F32), pltpu.VMEM((2, rows_h, D_MODEL), BF16),
                        pltpu.SemaphoreType.DMA((2,)), pltpu.SemaphoreType.DMA((2,)),
                        pltpu.VMEM((W_IN_SHARD, D_MODEL), BF16), pltpu.SemaphoreType.DMA((1,))],
        compiler_params=_params(40),
    )(w_in_t, w_kv, w_out, x2, g_norm)


def _proj_fwd(h16, wint, token):
    tm = 512
    sub = 256
    d = DILATIONS[2][0]
    per_ex = SEQ // tm

    def body(h_ref, w_ref, _, o_ref, o16_ref):
        res = _dot_nt(h_ref[...], w_ref[...])
        o_ref[...] = res
        r_out = lax.broadcasted_iota(jnp.int32, (sub, sub), 0)
        r_in = lax.broadcasted_iota(jnp.int32, (sub, sub), 1)
        pick = (r_in == d * (r_out % (sub // d)) + r_out // (sub // d)).astype(BF16)
        for part in range(tm // sub):
            grouped = _dot(pick, res[part * sub:(part + 1) * sub, 0:QKV_W]).astype(BF16)
            for rho in range(d):
                o16_ref[0, rho, part * (sub // d):(part + 1) * (sub // d), :] = (
                    grouped[rho * (sub // d):(rho + 1) * (sub // d), :])

    return pl.pallas_call(
        body, name="proj_fwd", grid=(T_LOC // tm,),
        in_specs=[pl.BlockSpec((tm, D_MODEL), lambda i: (i, 0)),
                  pl.BlockSpec((IN_COLS, D_MODEL), lambda i: (0, 0)), _TOKEN],
        out_specs=(pl.BlockSpec((tm, IN_COLS), lambda i: (i, 0)),
                   pl.BlockSpec((1, d, tm // d, QKV_W), lambda i: (i // per_ex, 0, i % per_ex, 0))),
        out_shape=(SDS((T_LOC, IN_COLS), F32), SDS((B_LOC, d, SEQ // d, QKV_W), BF16)),
        compiler_params=_params(48, ("arbitrary",)),
    )(h16, wint, token)


def _memkv_fwd(mem2, g_mem, wkv):
    def body(m_ref, g_ref, w_ref, o_ref):
        mv = m_ref[...]
        o_ref[...] = _dot(mv * _rstd(mv) * g_ref[...], w_ref[...])

    return pl.pallas_call(
        body, name="memkv_fwd", out_shape=SDS((B_LOC * N_MEM, 2 * MEM_W), F32), compiler_params=_params(32),
    )(mem2, g_mem, wkv)


N_BIAS = 7


def _fill_bias_tables(sl_ref, tab):
    for cfg, (d, length) in enumerate(DILATIONS):
        nk = min(length, 2 * CHUNK)
        r = lax.broadcasted_iota(jnp.int32, (CHUNK, nk), 0)
        c = lax.broadcasted_iota(jnp.int32, (CHUNK, nk), 1)
        for var in range(3 if length > nk else 1):
            rel = jnp.abs(r - c + var * RADIUS)
            dist = rel.astype(F32) * float(d)
            for h in range(2):
                slope = sl_ref[0, 0:1, h * HEAD:h * HEAD + 1]
                tab[3 * cfg + var, h * CHUNK:(h + 1) * CHUNK, 0:nk] = jnp.where(rel <= RADIUS, -slope * dist, NEG)


def _attn_blocks(visit, unroll):
    def step(t, carry):
        for cfg, (d, length) in enumerate(DILATIONS):
            nblk = length // CHUNK
            if nblk == 1:
                visit(cfg, 0, t, t, length, t)
                continue
            rho, i = (0, t) if d == 1 else (t // nblk, t % nblk)
            ks = jnp.clip(i * CHUNK - RADIUS, 0, length - 2 * CHUNK)
            visit(cfg, (i * CHUNK - ks) // RADIUS, rho + d * (i * CHUNK), rho + d * ks, 2 * CHUNK, t)
        return carry
    lax.fori_loop(0, 16, step, 0, unroll=unroll)


def _stack_heads(v, left):
    return jnp.concatenate([jnp.where(left, v, 0.0), jnp.where(left, 0.0, v)], axis=0)


def _unstack_heads(v, left):
    return jnp.where(left, v[0:CHUNK], v[CHUNK:2 * CHUNK])


def _rows(start, n, d):
    return pl.ds(start, n) if d == 1 else pl.ds(start, n, stride=d)


def _blk16(col0):
    d, length = DILATIONS[2]
    return pl.BlockSpec((1, d, length, 128), lambda b, hp: (b, 0, 0, col0 // 128 + hp))


def _attn_fwd(proj, qkv16, slopes):
    def body(sl_ref, q_ref, k_ref, v_ref, q16_ref, k16_ref, v16_ref, a_ref, lse_ref, *scr):
        o_c, m_c, l_c, tab = scr[0:3], scr[3:6], scr[6:9], scr[9]
        left = _left_lanes(CHUNK)
        _fill_bias_tables(sl_ref, tab)

        def block(cfg, var, q0, k0, nk, t):
            d = DILATIONS[cfg][0]
            rq, rk = _rows(q0, CHUNK, d), _rows(k0, nk, d)
            if cfg == 2:
                qb, kw, vw = q16_ref[0, t].astype(F32), k16_ref[0, t], v16_ref[0, t]
            else:
                qb, kw, vw = q_ref[rq, :], k_ref[rk, :], v_ref[rk, :]
            qs = _stack_heads(qb * SCALE, left)
            s = _dot_nt(qs, kw) + tab[3 * cfg + var, :, 0:nk]
            m = jnp.max(s, axis=-1, keepdims=True)
            p = jnp.exp(s - m)
            o_c[cfg][rq, :] = _unstack_heads(_dot(p, vw), left)
            m_c[cfg][rq, :] = _unstack_heads(m, left)
            l_c[cfg][rq, :] = _unstack_heads(jnp.sum(p, axis=-1, keepdims=True), left)
        _attn_blocks(block, 16)

        def merge(j, carry):
            rows = pl.ds(pl.multiple_of(j * 256, 256), 256)
            ms = [m_c[i][rows, :] for i in range(3)]
            top = jnp.maximum(jnp.maximum(ms[0], ms[1]), ms[2])
            ws = [jnp.exp(m - top) for m in ms]
            den = l_c[0][rows, :] * ws[0] + l_c[1][rows, :] * ws[1] + l_c[2][rows, :] * ws[2]
            num = o_c[0][rows, :] * ws[0] + o_c[1][rows, :] * ws[1] + o_c[2][rows, :] * ws[2]
            a_ref[rows, :] = num / den
            lse_ref[rows, :] = top + jnp.log(den)
            return carry
        lax.fori_loop(0, SEQ // 256, merge, 0)

    blk = lambda col0: pl.BlockSpec((SEQ, 128), lambda b, hp: (b, col0 // 128 + hp))
    out = pl.BlockSpec((SEQ, 128), lambda b, hp: (b, hp))
    return pl.pallas_call(
        body, name="attn_fwd", grid=(B_LOC, 4),
        in_specs=[pl.BlockSpec((1, 8, 128), lambda b, hp: (hp, 0, 0)), blk(C_QA), blk(C_KA), blk(C_VA),
                  _blk16(C_QA), _blk16(C_KA), _blk16(C_VA)],
        out_specs=(out, out),
        out_shape=(SDS((T_LOC, ATTN_W), F32), SDS((T_LOC, ATTN_W), F32)),
        scratch_shapes=[pltpu.VMEM((SEQ, 128), F32)] * 9 + [pltpu.VMEM((N_BIAS, 2 * CHUNK, 2 * CHUNK), F32)],
        compiler_params=_params(VMEM_NO_STAGING_MIB, ("arbitrary", "arbitrary")),
    )(slopes, proj, proj, proj, qkv16, qkv16, qkv16)


def _chunks_side_by_side(v, pr, tm):
    return jnp.concatenate([v[ch * CHUNK:(ch + 1) * CHUNK, pr * 128:(pr + 1) * 128] for ch in range(tm // CHUNK)], axis=1)


def _first_group_lanes(tm):
    return lax.broadcasted_iota(jnp.int32, (CHUNK, tm), 1) % 128 < HEAD


def _store_chunks(dst_ref, pr, val, tm):
    for ch in range(tm // CHUNK):
        dst_ref[ch * CHUNK:(ch + 1) * CHUNK, pr * 128:(pr + 1) * 128] = val[:, ch * CHUNK:(ch + 1) * CHUNK]


def _sgu_mix(vn, ws_ref, dst_ref, tm):
    first = _first_group_lanes(tm)
    for pr in range(2):
        vp = _chunks_side_by_side(vn, pr, tm)
        _store_chunks(dst_ref, pr, jnp.where(first, _dot(ws_ref[2 * pr], vp), _dot(ws_ref[2 * pr + 1], vp)), tm)


def _mem_head_of_lane(rows):
    return lax.broadcasted_iota(jnp.int32, (rows, MEM_W), 1) // HEAD


def _stack_mem_heads(v, rows):
    head = _mem_head_of_lane(rows)
    return jnp.concatenate([jnp.where(head == h, v, 0.0) for h in range(4)], axis=0)


def _unstack_mem_heads(v, rows):
    head = _mem_head_of_lane(rows)
    out = v[0:rows]
    for h in range(1, 4):
        out = jnp.where(head == h, v[h * rows:(h + 1) * rows], out)
    return out


def _mem_attn_probs(q, kmem, rows):
    qs = _stack_mem_heads(q, rows).astype(BF16)
    s = _dot_nt(qs, kmem) * SCALE
    e = jnp.exp(s - jnp.max(s, axis=-1, keepdims=True))
    return e * (1.0 / jnp.sum(e, axis=-1, keepdims=True)), qs


def _branch_blocks(tm):
    col = lambda w, c0: pl.BlockSpec((tm, w), lambda i: (i, c0 // w))
    return [col(512, C_ZA), col(256, C_UB), col(256, C_VB), col(256, C_ZB), col(256, C_QM), col(256, C_ZM)]


def _branch_fwd(proj, a, kv, w_s, b_exp, g_v):
    tm = 512
    per_ex = SEQ // tm

    def body(za_ref, ub_ref, vb_ref, zb_ref, qm_ref, zm_ref, a_ref, kv_ref, ws_ref, be_ref, gv_ref, o_ref, mix):
        o_ref[:, 0:ATTN_W] = (_silu_and_grad(za_ref[...])[0] * a_ref[...]).astype(BF16)
        gu = _gelu_and_grad(ub_ref[...])[0]
        gv = _gelu_and_grad(vb_ref[...])[0]
        vn = gv * _rstd(gv) * gv_ref[...]
        _sgu_mix(vn.astype(BF16), ws_ref, mix, tm)
        sg = gu * (mix[...] + jnp.concatenate([be_ref[...]] * (tm // CHUNK), axis=0))
        o_ref[:, ATTN_W:ATTN_W + SGU_W] = (_silu_and_grad(zb_ref[...])[0] * sg).astype(BF16)
        p = _mem_attn_probs(qm_ref[...], kv_ref[:, 0:MEM_W], tm)[0]
        mo = _unstack_mem_heads(_dot(p, kv_ref[:, MEM_W:2 * MEM_W]), tm)
        o_ref[:, ATTN_W + SGU_W:D_MODEL] = (_silu_and_grad(zm_ref[...])[0] * mo).astype(BF16)

    full = lambda shape: pl.BlockSpec(shape, lambda i: (0,) * len(shape))
    return pl.pallas_call(
        body, name="branch_fwd", grid=(T_LOC // tm,),
        in_specs=_branch_blocks(tm) + [
            pl.BlockSpec((tm, ATTN_W), lambda i: (i, 0)), pl.BlockSpec((N_MEM, 2 * MEM_W), lambda i: (i // per_ex, 0)),
            full((4, CHUNK, CHUNK)), full((CHUNK, SGU_W)), full((1, SGU_W))],
        out_specs=pl.BlockSpec((tm, D_MODEL), lambda i: (i, 0)),
        out_shape=SDS((T_LOC, D_MODEL), BF16),
        scratch_shapes=[pltpu.VMEM((tm, SGU_W), F32)],
        compiler_params=_params(VMEM_NO_STAGING_MIB, ("arbitrary",)),
    )(proj, proj, proj, proj, proj, proj, a, kv, w_s, b_exp, g_v)


def _outproj_loss(gated, wout, x2, tgt2, g_final):
    tm = 512

    def body(g_ref, w_ref, x_ref, t_ref, gf_ref, dh2_ref, loss_ref, dgf_ref, dwo_ref, dwo_acc):
        @pl.when(pl.program_id(0) == 0)
        def _():
            loss_ref[...] = jnp.zeros_like(loss_ref)
            dgf_ref[...] = jnp.zeros_like(dgf_ref)
            dwo_acc[...] = jnp.zeros_like(dwo_acc)
        gated = g_ref[...]
        h2 = x_ref[...] + _dot(gated, w_ref[...])
        r = _rstd(h2)
        gf = gf_ref[...]
        err = h2 * r * gf - t_ref[...]
        loss_ref[...] += 0.5 * jnp.sum(jnp.mean(err * err, axis=-1, keepdims=True))
        dy = err * (1.0 / D_MODEL)
        dh2 = _rms_bwd(h2, r, gf, dy)
        dh2_ref[...] = dh2
        dgf_ref[...] += jnp.sum(dy * (h2 * r), axis=0, keepdims=True)
        dwo_acc[...] += _dot_tn(gated, dh2)

        @pl.when(pl.program_id(0) == pl.num_programs(0) - 1)
        def _():
            _cast_rows(dwo_acc, dwo_ref, D_MODEL)

    row = pl.BlockSpec((tm, D_MODEL), lambda i: (i, 0))
    vec = pl.BlockSpec((1, D_MODEL), lambda i: (0, 0))
    square = pl.BlockSpec((D_MODEL, D_MODEL), lambda i: (0, 0))
    return pl.pallas_call(
        body, name="outproj_loss", grid=(T_LOC // tm,),
        in_specs=[row, square, row, row, vec],
        out_specs=(row, pl.BlockSpec((8, 128), lambda i: (0, 0)), vec, square),
        out_shape=(SDS((T_LOC, D_MODEL), F32), SDS((8, 128), F32), SDS((1, D_MODEL), F32), SDS((D_MODEL, D_MODEL), BF16)),
        scratch_shapes=[pltpu.VMEM((D_MODEL, D_MODEL), F32)],
        compiler_params=_params(VMEM_NO_STAGING_MIB, ("arbitrary",)),
    )(gated, wout, x2, tgt2, g_final)


def _branch_bwd(dh2, wout, proj, a, kv, w_s, b_exp, g_v):
    tm = 512
    per_ex = SEQ // tm

    def body(dh2_ref, w_ref, za_ref, ub_ref, vb_ref, zb_ref, qm_ref, zm_ref, a_ref, kv_ref, ws_ref,
             be_ref, gv_ref, da_ref, dr_ref, dkv_ref, dws_ref, db_ref, dgv_ref, mix, dvn, dmsum):
        i = pl.program_id(0)

        @pl.when(i == 0)
        def _():
            dws_ref[...] = jnp.zeros_like(dws_ref)
            dgv_ref[...] = jnp.zeros_like(dgv_ref)
            dmsum[...] = jnp.zeros_like(dmsum)

        @pl.when(i % per_ex == 0)
        def _():
            dkv_ref[...] = jnp.zeros_like(dkv_ref)

        dg = _dot_nt(dh2_ref[...], w_ref[...])

        sa, dsa = _silu_and_grad(za_ref[...])
        dga = dg[:, 0:ATTN_W]
        da_ref[...] = dga * sa
        dr_ref[:, 0:512] = (dga * a_ref[...] * dsa).astype(BF16)

        ub, vb = ub_ref[...], vb_ref[...]
        gu, dgu = _gelu_and_grad(ub)
        gv, dgv = _gelu_and_grad(vb)
        rv = _rstd(gv)
        gain = gv_ref[...]
        vn = (gv * rv * gain).astype(BF16)
        _sgu_mix(vn, ws_ref, mix, tm)
        mixed = mix[...] + jnp.concatenate([be_ref[...]] * (tm // CHUNK), axis=0)
        sb, dsb = _silu_and_grad(zb_ref[...])
        dgb = dg[:, ATTN_W:ATTN_W + SGU_W]
        dsg = dgb * sb
        dr_ref[:, 512:768] = (dsg * mixed * dgu).astype(BF16)
        dr_ref[:, 1024:1280] = (dgb * (gu * mixed) * dsb).astype(BF16)
        dmix = dsg * gu
        for ch in range(tm // CHUNK):
            dmsum[...] += dmix[ch * CHUNK:(ch + 1) * CHUNK, :]
        first = _first_group_lanes(tm)
        for pr in range(2):
            dmp, vp = _chunks_side_by_side(dmix, pr, tm), _chunks_side_by_side(vn, pr, tm)
            dws_ref[2 * pr] += _dot_nt(jnp.where(first, dmp, 0.0), vp)
            dws_ref[2 * pr + 1] += _dot_nt(jnp.where(first, 0.0, dmp), vp)
            _store_chunks(dvn, pr, jnp.where(first, _dot_tn(ws_ref[2 * pr], dmp), _dot_tn(ws_ref[2 * pr + 1], dmp)), tm)
        dvn_v = dvn[...]
        dgv_ref[...] += jnp.sum(dvn_v * (gv * rv), axis=0, keepdims=True)
        dr_ref[:, 768:1024] = (_rms_bwd(gv, rv, gain, dvn_v) * dgv).astype(BF16)

        szm, dszm = _silu_and_grad(zm_ref[...])
        dgm = dg[:, ATTN_W + SGU_W:D_MODEL]
        kmem, vmem_ = kv_ref[:, 0:MEM_W].astype(BF16), kv_ref[:, MEM_W:2 * MEM_W].astype(BF16)
        p, qs = _mem_attn_probs(qm_ref[...], kmem, tm)
        dmos = _stack_mem_heads(dgm * szm, tm).astype(BF16)
        dr_ref[:, 1536:1792] = (dgm * _unstack_mem_heads(_dot(p, vmem_), tm) * dszm).astype(BF16)
        dp = _dot_nt(dmos, vmem_)
        ds = (p * (dp - jnp.sum(p * dp, axis=-1, keepdims=True)) * SCALE).astype(BF16)
        dr_ref[:, 1280:1536] = _unstack_mem_heads(_dot(ds, kmem), tm).astype(BF16)
        dkv_ref[:, 0:MEM_W] += _dot_tn(ds, qs)
        dkv_ref[:, MEM_W:2 * MEM_W] += _dot_tn(p, dmos)

        @pl.when(i == pl.num_programs(0) - 1)
        def _():
            tot = dmsum[...]
            hi = tot.astype(BF16)
            lo = (tot - hi.astype(F32)).astype(BF16)
            grp = (lax.broadcasted_iota(jnp.int32, (SGU_W, 128), 0) // HEAD
                   == lax.broadcasted_iota(jnp.int32, (SGU_W, 128), 1)).astype(BF16)
            db_ref[...] = (_dot(hi, grp) + _dot(lo, grp)).T[0:4, :]

    full = lambda shape: pl.BlockSpec(shape, lambda i: (0,) * len(shape))
    row = lambda w: pl.BlockSpec((tm, w), lambda i: (i, 0))
    return pl.pallas_call(
        body, name="branch_bwd", grid=(T_LOC // tm,),
        in_specs=[row(D_MODEL), full((D_MODEL, D_MODEL))] + _branch_blocks(tm) + [
            row(ATTN_W), pl.BlockSpec((N_MEM, 2 * MEM_W), lambda i: (i // per_ex, 0)),
            full((4, CHUNK, CHUNK)), full((CHUNK, SGU_W)), full((1, SGU_W))],
        out_specs=(row(ATTN_W), row(REST_W), pl.BlockSpec((N_MEM, 2 * MEM_W), lambda i: (i // per_ex, 0)),
                   full((4, CHUNK, CHUNK)), full((4, CHUNK)), full((1, SGU_W))),
        out_shape=(SDS((T_LOC, ATTN_W), F32), SDS((T_LOC, REST_W), BF16), SDS((B_LOC * N_MEM, 2 * MEM_W), F32),
                   SDS((4, CHUNK, CHUNK), F32), SDS((4, CHUNK), F32), SDS((1, SGU_W), F32)),
        scratch_shapes=[pltpu.VMEM((tm, SGU_W), F32), pltpu.VMEM((tm, SGU_W), F32), pltpu.VMEM((CHUNK, SGU_W), F32)],
        compiler_params=_params(56, ("arbitrary",)),
    )(dh2, wout, proj, proj, proj, proj, proj, proj, a, kv, w_s, b_exp, g_v)


def _attn_bwd(proj, qkv16, slopes, da, a, lse, token):
    def body(sl_ref, q_ref, k_ref, v_ref, q16_ref, k16_ref, v16_ref, da_ref, a_ref, lse_ref, _,
             dq_ref, dk_ref, dv_ref, *scr):
        dq_s, dk_s, dv_s, tab = scr[0:3], scr[3:6], scr[6:9], scr[9]
        lse_h, delta_h = scr[10:12], scr[12:14]
        p_all, ds_all = scr[14], scr[15]
        left = _left_lanes(CHUNK)
        _fill_bias_tables(sl_ref, tab)

        def prep(j, carry):
            rows = pl.ds(pl.multiple_of(j * 256, 256), 256)
            l256 = _left_lanes(256)
            prod = da_ref[rows, :] * a_ref[rows, :]
            delta_h[0][rows, :] = jnp.broadcast_to(jnp.sum(jnp.where(l256, prod, 0.0), axis=-1, keepdims=True), (256, 128))
            delta_h[1][rows, :] = jnp.broadcast_to(jnp.sum(jnp.where(l256, 0.0, prod), axis=-1, keepdims=True), (256, 128))
            pair = lse_ref[rows, :]
            other = pltpu.roll(pair, HEAD, axis=1)
            lse_h[0][rows, :] = jnp.where(l256, pair, other)
            lse_h[1][rows, :] = jnp.where(l256, other, pair)
            zero = jnp.zeros((256, 128), F32)
            for cfg in range(3):
                dk_s[cfg][rows, :] = zero
                dv_s[cfg][rows, :] = zero
            return carry
        lax.fori_loop(0, SEQ // 256, prep, 0)

        def per_row(halves, rq, nk):
            v = jnp.concatenate([halves[0][rq, :], halves[1][rq, :]], axis=0)
            return v if nk == 128 else jnp.concatenate([v, v], axis=1)

        def qkv(cfg, rq, rk, t):
            if cfg == 2:
                return q16_ref[0, t].astype(F32), k16_ref[0, t], v16_ref[0, t]
            return q_ref[rq, :], k_ref[rk, :], v_ref[rk, :]

        def probs(cfg, var, q0, k0, nk, t):
            d = DILATIONS[cfg][0]
            rq, rk = _rows(q0, CHUNK, d), _rows(k0, nk, d)
            qb, kw, vw = qkv(cfg, rq, rk, t)
            qs = _stack_heads(qb * SCALE, left)
            das = _stack_heads(da_ref[rq, :], left)
            s = _dot_nt(qs, kw) + tab[3 * cfg + var, :, 0:nk]
            p = jnp.exp(s - per_row(lse_h, rq, nk))
            p_all[16 * cfg + t, :, 0:nk] = p.astype(BF16)
            ds_all[16 * cfg + t, :, 0:nk] = (p * (_dot_nt(das, vw) - per_row(delta_h, rq, nk))).astype(BF16)
        _attn_blocks(probs, 16)

        def grads(cfg, var, q0, k0, nk, t):
            d = DILATIONS[cfg][0]
            rq, rk = _rows(q0, CHUNK, d), _rows(k0, nk, d)
            qb, kw, _ = qkv(cfg, rq, rk, t)
            qs = _stack_heads(qb * SCALE, left).astype(BF16)
            das = _stack_heads(da_ref[rq, :], left).astype(BF16)
            p, ds = p_all[16 * cfg + t, :, 0:nk], ds_all[16 * cfg + t, :, 0:nk]
            dq_s[cfg][rq, :] = _unstack_heads(_dot(ds, kw), left) * SCALE
            dk_s[cfg][rk, :] += _dot_tn(ds, qs)
            dv_s[cfg][rk, :] += _dot_tn(p, das)
        _attn_blocks(grads, 8)

        def flush(j, carry):
            rows = pl.ds(pl.multiple_of(j * 256, 256), 256)
            for acc, dst in ((dq_s, dq_ref), (dk_s, dk_ref), (dv_s, dv_ref)):
                dst[rows, :] = (acc[0][rows, :] + acc[1][rows, :] + acc[2][rows, :]).astype(BF16)
            return carry
        lax.fori_loop(0, SEQ // 256, flush, 0)

    blk = lambda col0: pl.BlockSpec((SEQ, 128), lambda b, hp: (b, col0 // 128 + hp))
    own = pl.BlockSpec((SEQ, 128), lambda b, hp: (b, hp))
    return pl.pallas_call(
        body, name="attn_bwd", grid=(B_LOC, 4),
        in_specs=[pl.BlockSpec((1, 8, 128), lambda b, hp: (hp, 0, 0)), blk(C_QA), blk(C_KA), blk(C_VA),
                  _blk16(C_QA), _blk16(C_KA), _blk16(C_VA), own, own, own, _TOKEN],
        out_specs=(own, own, own),
        out_shape=(SDS((T_LOC, ATTN_W), BF16),) * 3,
        scratch_shapes=[pltpu.VMEM((SEQ, 128), F32)] * 9 + [pltpu.VMEM((N_BIAS, 2 * CHUNK, 2 * CHUNK), F32)]
        + [pltpu.VMEM((SEQ, 128), F32)] * 4 + [pltpu.VMEM((48, 2 * CHUNK, 2 * CHUNK), BF16)] * 2,
        compiler_params=_params(52, ("arbitrary", "arbitrary")),
    )(slopes, proj, proj, proj, qkv16, qkv16, qkv16, da, a, lse, token)


def _dproj_specs(tm):
    third = pl.BlockSpec((tm, ATTN_W), lambda i: (i, 0))
    return [third, third, third, pl.BlockSpec((tm, REST_W), lambda i: (i, 0))]


def _dx(dq, dk, dv, dr, wint, x2, dh2, g_norm, token):
    tm = 512

    def body(dq_ref, dk_ref, dv_ref, dr_ref, w_ref, x_ref, dh2_ref, g_ref, _, gx_ref, dgn_ref):
        @pl.when(pl.program_id(0) == 0)
        def _():
            dgn_ref[...] = jnp.zeros_like(dgn_ref)
        dh = (_dot(dq_ref[...], w_ref[C_QA:C_KA, :]) + _dot(dk_ref[...], w_ref[C_KA:C_VA, :])
              + _dot(dv_ref[...], w_ref[C_VA:C_ZA, :]) + _dot(dr_ref[...], w_ref[C_ZA:IN_COLS, :]))
        xv = x_ref[...]
        r = _rstd(xv)
        gx_ref[...] = dh2_ref[...] + _rms_bwd(xv, r, g_ref[...], dh)
        dgn_ref[...] += jnp.sum(dh * (xv * r), axis=0, keepdims=True)

    row = pl.BlockSpec((tm, D_MODEL), lambda i: (i, 0))
    vec = pl.BlockSpec((1, D_MODEL), lambda i: (0, 0))
    return pl.pallas_call(
        body, name="dx", grid=(T_LOC // tm,),
        in_specs=_dproj_specs(tm) + [pl.BlockSpec((IN_COLS, D_MODEL), lambda i: (0, 0)), row, row, vec, _TOKEN],
        out_specs=(row, vec),
        out_shape=(SDS((T_LOC, D_MODEL), F32), SDS((1, D_MODEL), F32)),
        compiler_params=_params(48, ("arbitrary",)),
    )(dq, dk, dv, dr, wint, x2, dh2, g_norm, token)


def _dwin(dq, dk, dv, dr, h16, half, token):
    tm = 1024
    width = D_MODEL // 2

    def body(dq_ref, dk_ref, dv_ref, dr_ref, h_ref, _, o_ref, acc):
        @pl.when(pl.program_id(0) == 0)
        def _():
            acc[...] = jnp.zeros_like(acc)
        h = h_ref[...]
        acc[C_QA:C_KA, :] += _dot_tn(dq_ref[...], h)
        acc[C_KA:C_VA, :] += _dot_tn(dk_ref[...], h)
        acc[C_VA:C_ZA, :] += _dot_tn(dv_ref[...], h)
        acc[C_ZA:IN_COLS, :] += _dot_tn(dr_ref[...], h)

        @pl.when(pl.program_id(0) == pl.num_programs(0) - 1)
        def _():
            _cast_rows(acc, o_ref, IN_COLS)

    return pl.pallas_call(
        body, name="dwin%d" % half, grid=(T_LOC // tm,),
        in_specs=_dproj_specs(tm) + [pl.BlockSpec((tm, width), lambda i: (i, half)), _TOKEN],
        out_specs=pl.BlockSpec((IN_COLS, width), lambda i: (0, 0)),
        out_shape=SDS((IN_COLS, width), BF16),
        scratch_shapes=[pltpu.VMEM((IN_COLS, width), F32)],
        compiler_params=_params(48, ("arbitrary",)),
    )(dq, dk, dv, dr, h16, token)


def _memkv_bwd(dkv, mem2, g_mem, wkv):
    def body(dkv_ref, m_ref, g_ref, w_ref, dw_ref, dg_ref):
        mv = m_ref[...]
        r = _rstd(mv)
        dkv_v = dkv_ref[...].astype(BF16)
        dw_ref[...] = _dot_tn(mv * r * g_ref[...], dkv_v).astype(BF16)
        dg_ref[...] = jnp.sum(_dot_nt(dkv_v, w_ref[...]) * (mv * r), axis=0, keepdims=True)

    return pl.pallas_call(
        body, name="memkv_bwd", out_shape=(SDS((D_MODEL, 2 * MEM_W), BF16), SDS((1, D_MODEL), F32)),
        compiler_params=_params(32),
    )(dkv, mem2, g_mem, wkv)


def _allreduce_small(parts):
    n = len(parts)

    def body(*refs):
        ins, outs, bufs = refs[0:n], refs[n:2 * n], refs[2 * n:3 * n]
        send_sems, recv_sems = refs[3 * n], refs[3 * n + 1]
        pos = _mesh_pos()
        me = _flat(pos)
        for a in range(n):
            bufs[a][me] = ins[a][...]

        def copy(a, k, slot):
            return pltpu.make_async_remote_copy(
                src_ref=ins[a], dst_ref=bufs[a].at[slot],
                send_sem=send_sems.at[7 * a + k - 1], recv_sem=recv_sems.at[7 * a + k - 1],
                device_id=_peer(pos, k), device_id_type=MESH)

        sent = [copy(a, k, me) for a in range(n) for k in range(1, N_DEV)]
        for cp in sent:
            cp.start()
        for a in range(n):
            for k in range(1, N_DEV):
                copy(a, k, _flat(_peer(pos, k))).wait_recv()
        for cp in sent:
            cp.wait_send()
        for a in range(n):
            acc = bufs[a][0]
            for s in range(1, N_DEV):
                acc = acc + bufs[a][s]
            outs[a][...] = acc

    vmem = pl.BlockSpec(memory_space=pltpu.VMEM)
    return pl.pallas_call(
        body, name="allreduce_small",
        out_shape=tuple(SDS(p.shape, F32) for p in parts),
        in_specs=[vmem] * n, out_specs=(vmem,) * n,
        scratch_shapes=[pltpu.VMEM((N_DEV,) + p.shape, F32) for p in parts]
        + [pltpu.SemaphoreType.DMA((7 * n,)), pltpu.SemaphoreType.DMA((7 * n,))],
        compiler_params=_params(16),
    )(*parts)


_HBM = pl.BlockSpec(memory_space=pltpu.HBM)
_SEM = pl.BlockSpec(memory_space=pltpu.SEMAPHORE)
_SIDE_EFFECT = pltpu.SideEffectType.DATAFLOW_SIDE_EFFECTING


def _exchange_copies(src_refs, land_refs, scatter, send_sems, recv_sems):
    pos = _mesh_pos()
    copies = []
    for a, (src, land) in enumerate(zip(src_refs, land_refs)):
        n = land.shape[1]
        for k in range(1, N_DEV):
            peer = _peer(pos, k)
            piece = src.at[pl.ds(pl.multiple_of(_flat(peer) * n, 16), n), :] if scatter[a] else src
            copies.append(pltpu.make_async_remote_copy(
                src_ref=piece, dst_ref=land.at[_flat(pos)],
                send_sem=send_sems.at[7 * a + k - 1], recv_sem=recv_sems.at[7 * a + k - 1],
                device_id=peer, device_id_type=MESH))
    return copies


def _own_copies(src_refs, land_refs, scatter, send_sems):
    n = len(src_refs)
    me = _flat(_mesh_pos())
    copies = []
    for a, (src, land) in enumerate(zip(src_refs, land_refs)):
        rows = land.shape[1]
        piece = src.at[pl.ds(pl.multiple_of(me * rows, 16), rows), :] if scatter[a] else src
        copies.append(pltpu.make_async_copy(piece, land.at[me], send_sems.at[7 * n + a]))
    return copies


def _exchange_start(name, srcs, scatter):
    n = len(srcs)

    def body(*refs):
        for cp in _exchange_copies(refs[0:n], refs[n:2 * n], scatter, refs[2 * n], refs[2 * n + 1]):
            cp.start()
        for cp in _own_copies(refs[0:n], refs[n:2 * n], scatter, refs[2 * n]):
            cp.start()
        refs[-1][...] = jnp.zeros_like(refs[-1])

    lands = [lax.empty((N_DEV,) + (t.shape[0] // N_DEV if sc else t.shape[0],) + t.shape[1:], t.dtype)
             for t, sc in zip(srcs, scatter)]
    ops = [pltpu.with_memory_space_constraint(t, pltpu.HBM) for t in (*srcs, *lands)]
    out = pl.pallas_call(
        body, name=name,
        out_shape=(pltpu.SemaphoreType.DMA((8 * n,)), pltpu.SemaphoreType.DMA((7 * n,)),
                   *[pltpu.HBM(t.shape, t.dtype) for t in ops], SDS((8, 128), F32)),
        in_specs=[_HBM] * (2 * n),
        out_specs=(_SEM, _SEM, *[_HBM] * (2 * n), pl.BlockSpec(memory_space=pltpu.VMEM)),
        input_output_aliases={i: 2 + i for i in range(2 * n)},
        compiler_params=pltpu.CompilerParams(has_side_effects=_SIDE_EFFECT),
    )(*ops)
    return out[0], out[1], out[2:2 + n], out[2 + n:2 + 2 * n], out[-1]


def _exchange_wait(name, started, scatter, after):
    send_sems, recv_sems, srcs, lands, _ = started
    n = len(srcs)

    def body(*refs):
        for cp in _exchange_copies(refs[0:n], refs[n:2 * n], scatter, refs[2 * n], refs[2 * n + 1]):
            cp.wait_send()
            cp.wait_recv()
        for cp in _own_copies(refs[0:n], refs[n:2 * n], scatter, refs[2 * n]):
            cp.wait()

    out = pl.pallas_call(
        body, name=name,
        out_shape=tuple(pltpu.HBM(t.shape, t.dtype) for t in (*srcs, *lands)),
        in_specs=[_HBM] * (2 * n) + [_SEM, _SEM, pl.BlockSpec(memory_space=pl.ANY)],
        out_specs=(_HBM,) * (2 * n),
        input_output_aliases={i: i for i in range(2 * n)},
        compiler_params=pltpu.CompilerParams(has_side_effects=_SIDE_EFFECT),
    )(*srcs, *lands, send_sems, recv_sems, after)
    return out[n:]


def _adamw(w, g, m, v):
    m = ADAM_B1 * m + (1.0 - ADAM_B1) * g
    v = ADAM_B2 * v + (1.0 - ADAM_B2) * (g * g)
    m_hat = m / (1.0 - ADAM_B1 ** ADAM_STEP)
    v_hat = v / (1.0 - ADAM_B2 ** ADAM_STEP)
    return -ADAM_LR * (m_hat / (jnp.sqrt(v_hat) + ADAM_EPS) + ADAM_WD * w), m, v


def _adam_slots(name, pieces, w, m, v):
    rows, cols = w.shape
    starts = [sum(p.shape[2] for p in pieces[:i]) for i in range(len(pieces) + 1)]
    assert starts[-1] == cols and all(p.shape[1] == rows for p in pieces)

    def body(*refs):
        s_refs, (w_ref, m_ref, v_ref, g_o, d_o, m_o, v_o, acc) = refs[:len(pieces)], refs[len(pieces):]
        s = pl.program_id(0)

        @pl.when(s == 0)
        def _():
            for i, s_ref in enumerate(s_refs):
                acc[:, starts[i]:starts[i + 1]] = s_ref[0].astype(F32)

        @pl.when(s > 0)
        def _():
            for i, s_ref in enumerate(s_refs):
                acc[:, starts[i]:starts[i + 1]] += s_ref[0].astype(F32)

        @pl.when(s == N_DEV - 1)
        def _():
            g = acc[...]
            g_o[...] = g
            d_o[...], m_o[...], v_o[...] = _adamw(w_ref[...], g, m_ref[...], v_ref[...])

    full = pl.BlockSpec((rows, cols), lambda s: (0, 0))
    return pl.pallas_call(
        body, name="adam_" + name, grid=(N_DEV,),
        in_specs=[pl.BlockSpec((1, rows, p.shape[2]), lambda s: (s, 0, 0)) for p in pieces] + [full, full, full],
        out_specs=(full,) * 4, out_shape=(SDS((rows, cols), F32),) * 4,
        scratch_shapes=[pltpu.VMEM((rows, cols), F32)],
        compiler_params=_params(40, ("arbitrary",)),
    )(*pieces, w, m, v)


def _adam_small(ws, gs, ms, vs, loss_slots):
    n = len(ws)

    def total(ref, like):
        if len(ref.shape) == len(like.shape):
            return ref[...]
        acc = ref[0]
        for s in range(1, N_DEV):
            acc = acc + ref[s]
        return acc

    def body(*refs):
        w_r, g_r, m_r, v_r = refs[0:n], refs[n:2 * n], refs[2 * n:3 * n], refs[3 * n:4 * n]
        loss_r, outs = refs[4 * n], refs[4 * n + 1:]
        for a in range(n):
            g = total(g_r[a], w_r[a])
            outs[a][...] = g
            outs[n + 1 + 3 * a][...], outs[n + 2 + 3 * a][...], outs[n + 3 + 3 * a][...] = _adamw(
                w_r[a][...], g, m_r[a][...], v_r[a][...])
        outs[n][...] = total(loss_r, outs[n])

    out = pl.pallas_call(
        body, name="adam_small",
        out_shape=tuple(SDS(w.shape, F32) for w in ws) + (SDS(loss_slots.shape[1:], F32),)
        + tuple(SDS(w.shape, F32) for w in ws for _ in range(3)),
        compiler_params=_params(16),
    )(*ws, *gs, *ms, *vs, loss_slots)
    return out[0:n], out[n], out[n + 1:]


def kernel(x, mem, g_norm, w_in, w_sgu_spatial, b_sgu_spatial, g_sgu_v, g_mem, w_mem_kv, w_out, g_final, loss_target, m_g_norm, m_w_in, m_w_sgu_spatial, m_b_sgu_spatial, m_g_sgu_v, m_g_mem, m_w_mem_kv, m_w_out, m_g_final, v_g_norm, v_w_in, v_w_sgu_spatial, v_b_sgu_spatial, v_g_sgu_v, v_g_mem, v_w_mem_kv, v_w_out, v_g_final):
    x2 = x.reshape(T_LOC, D_MODEL)
    tgt2 = loss_target.reshape(T_LOC, D_MODEL)
    mem2 = mem.reshape(B_LOC * N_MEM, D_MODEL)
    w_s = w_sgu_spatial[0]
    b_exp = jnp.repeat(b_sgu_spatial[0].T, HEAD, axis=1)
    slope = jnp.power(2.0, -8.0 * (jnp.arange(8, dtype=F32) + 1.0) / 8)
    slopes = jnp.broadcast_to(jnp.repeat(slope.reshape(4, 2), HEAD, axis=1)[:, None, :], (4, 8, 128))

    tr = lambda t: jnp.transpose(t[0])

    wint, wkv_own, wout_own, h16 = _allgather_weights(tr(w_in), w_mem_kv[0], w_out[0], x2, g_norm)
    started0 = _exchange_start("exchange0_start", [wkv_own, wout_own], [False, False])
    proj, qkv16 = _proj_fwd(h16, wint, started0[4])
    wkv, wout = _exchange_wait("exchange0_wait", started0, [False, False], proj)
    wkv, wout = wkv.reshape(D_MODEL, 2 * MEM_W), wout.reshape(D_MODEL, D_MODEL)
    kv = _memkv_fwd(mem2, g_mem, wkv)
    a, lse = _attn_fwd(proj, qkv16, slopes)
    gated = _branch_fwd(proj, a, kv, w_s, b_exp, g_sgu_v)
    dh2, loss8, dgf, dwout = _outproj_loss(gated, wout, x2, tgt2, g_final.reshape(1, D_MODEL))

    da, dr, dkv, dws, dbs, dgv = _branch_bwd(dh2, wout, proj, a, kv, w_s, b_exp, g_sgu_v)
    dwkv, dgm = _memkv_bwd(dkv, mem2, g_mem, wkv)

    early = [dws.reshape(4 * CHUNK, CHUNK), dbs, dgv, dgm, dgf, loss8]
    scatter1 = [True, True] + [False] * len(early)
    started1 = _exchange_start("exchange1_start", [dwkv, dwout] + early, scatter1)
    dq, dk, dv = _attn_bwd(proj, qkv16, slopes, da, a, lse, started1[4])
    s_wkv, s_wout, s_ws, s_bs, s_gv, s_gm, s_gf, s_loss = _exchange_wait("exchange1_wait", started1, scatter1, dq)

    dwint0 = _dwin(dq, dk, dv, dr, h16, 0, started1[4])
    started2 = _exchange_start("exchange2_start", [dwint0], [True])
    dwint1 = _dwin(dq, dk, dv, dr, h16, 1, started2[4])
    started3 = _exchange_start("exchange3_start", [dwint1], [True])
    grad_x, dgn = _dx(dq, dk, dv, dr, wint, x2, dh2, g_norm, started3[4])
    s_win0, = _exchange_wait("exchange2_wait", started2, [True], grad_x)
    dgn_sum, = _allreduce_small([dgn])
    s_win1, = _exchange_wait("exchange3_wait", started3, [True], dgn_sum)

    g_win, d_win, m_win, v_win = map(
        jnp.transpose, _adam_slots("w_in", [s_win0, s_win1], tr(w_in), tr(m_w_in), tr(v_w_in)))
    g_wkv, d_wkv, m_wkv, v_wkv = _adam_slots("w_mem_kv", [s_wkv], w_mem_kv[0], m_w_mem_kv[0], v_w_mem_kv[0])
    g_wout, d_wout, m_wout, v_wout = _adam_slots("w_out", [s_wout], w_out[0], m_w_out[0], v_w_out[0])

    small_shapes = [(1, D_MODEL), (4 * CHUNK, CHUNK), (4, CHUNK), (1, SGU_W), (1, D_MODEL), (1, D_MODEL)]
    pack = lambda arrs: [t.reshape(s) for t, s in zip(arrs, small_shapes)]
    g_small, loss_sum, upd = _adam_small(
        pack([g_norm, w_sgu_spatial, b_sgu_spatial, g_sgu_v, g_mem, g_final]),
        [dgn_sum, s_ws, s_bs, s_gv, s_gm, s_gf],
        pack([m_g_norm, m_w_sgu_spatial, m_b_sgu_spatial, m_g_sgu_v, m_g_mem, m_g_final]),
        pack([v_g_norm, v_w_sgu_spatial, v_b_sgu_spatial, v_g_sgu_v, v_g_mem, v_g_final]), s_loss)
    out_shapes = [g_norm.shape, w_sgu_spatial.shape, b_sgu_spatial.shape, g_sgu_v.shape, g_mem.shape, g_final.shape]
    unpack = lambda arrs: [t.reshape(s) for t, s in zip(arrs, out_shapes)]
    gs = unpack(g_small)
    ds, nms, nvs = unpack(upd[0::3]), unpack(upd[1::3]), unpack(upd[2::3])

    loss = loss_sum[0, 0]

    def assemble(small, win, wkv_, wout_):
        return [small[0], win[None], small[1], small[2], small[3], small[4], wkv_[None], wout_[None], small[5]]

    return (loss, grad_x.reshape(x.shape),
            *assemble(gs, g_win, g_wkv, g_wout), *assemble(ds, d_win, d_wkv, d_wout),
            *assemble(nms, m_win, m_wkv, m_wout), *assemble(nvs, v_win, v_wkv, v_wout))
```
